```python
import math
import jax, jax.numpy as jnp
from jax import lax
import numpy as np

D_MODEL = 1024
BATCH = 8
SEQ = 8192
DEPTH = 4

HEAD_DIM = 64
A_HEADS = 4
A_CONFIGS = ((128, 1), (512, 4), (2048, 16))
A_ROPE_DIMS = HEAD_DIM // 4
ROPE_THETA = 500000.0
B_Q_HEADS = 8
B_KV_HEADS = 2
B_AXIAL_THETA = 10000.0
C_HEADS = 4
C_ROWS_MAX = 8
C_COLS = 16
GRID_W = 64
Q_BLOCK = 128
D_FF = 4 * D_MODEL
A_WIDTH = A_HEADS * HEAD_DIM
B_Q_WIDTH = B_Q_HEADS * HEAD_DIM
B_KV_WIDTH = B_KV_HEADS * HEAD_DIM
C_WIDTH = C_HEADS * HEAD_DIM
N_BRANCHES = 3
QKV_COLS = 3 * A_WIDTH + B_Q_WIDTH + 2 * B_KV_WIDTH + 3 * C_WIDTH
IN_COLS = QKV_COLS + N_BRANCHES * D_MODEL
DEEPNORM_ALPHA = (2 * DEPTH) ** 0.25
DEEPNORM_BETA = (8 * DEPTH) ** -0.25
LN_EPS = 1e-5
RMS_EPS = 1e-6
NEG_INF = -1e30

kernel_name = "hybrid_dilated_axial_neighbourhood_encoder"


def _split_points():
    sizes = [A_WIDTH, A_WIDTH, A_WIDTH, B_Q_WIDTH, B_KV_WIDTH, B_KV_WIDTH,
             C_WIDTH, C_WIDTH, C_WIDTH]
    return tuple(int(v) for v in np.cumsum(sizes))


def layer_norm(x, g, b):
    xf = x.astype(jnp.float32)
    mu = jnp.mean(xf, -1, keepdims=True)
    var = jnp.mean(jnp.square(xf - mu), -1, keepdims=True)
    y = (xf - mu) * lax.rsqrt(var + LN_EPS)
    return (y * g.astype(jnp.float32) + b.astype(jnp.float32)).astype(x.dtype)


def rms_norm(x, g):
    xf = x.astype(jnp.float32)
    y = xf * lax.rsqrt(jnp.mean(jnp.square(xf), -1, keepdims=True) + RMS_EPS)
    return (y * g.astype(jnp.float32)).astype(x.dtype)


def rotary(x, pos, theta):
    half = x.shape[-1] // 2
    inv = theta ** (-jnp.arange(half, dtype=jnp.float32) / half)
    ang = pos.astype(jnp.float32)[:, None] * inv[None, :]
    cos = jnp.cos(ang)[None, :, None, :]
    sin = jnp.sin(ang)[None, :, None, :]
    xf = x.astype(jnp.float32)
    x1, x2 = xf[..., :half], xf[..., half:]
    return jnp.concatenate([x1 * cos - x2 * sin, x2 * cos + x1 * sin], -1).astype(x.dtype)


def partial_rotary(x, pos):
    return jnp.concatenate([rotary(x[..., :A_ROPE_DIMS], pos, ROPE_THETA),
                            x[..., A_ROPE_DIMS:]], -1)


def axial_rotary(x, row, col):
    half = x.shape[-1] // 2
    return jnp.concatenate([rotary(x[..., :half], row, B_AXIAL_THETA),
                            rotary(x[..., half:], col, B_AXIAL_THETA)], -1)


def banded_window_stats(q, k, v, radius):
    L, hd = q.shape[-2], q.shape[-1]
    lead = q.shape[:-2]
    nb = -(-L // Q_BLOCK)
    lp = nb * Q_BLOCK
    pad_q = [(0, 0)] * len(lead) + [(0, lp - L), (0, 0)]
    pad_kv = [(0, 0)] * len(lead) + [(radius, lp - L + radius), (0, 0)]
    qb = jnp.pad(q, pad_q).reshape(lead + (nb, Q_BLOCK, hd))
    kp = jnp.pad(k, pad_kv)
    vp = jnp.pad(v, pad_kv)
    span = Q_BLOCK + 2 * radius
    idx = jnp.arange(nb)[:, None] * Q_BLOCK + jnp.arange(span)[None, :]
    kb = kp[..., idx, :]
    vb = vp[..., idx, :].astype(jnp.float32)
    qi = jnp.arange(lp).reshape(nb, Q_BLOCK)[:, :, None]
    kj = (idx - radius)[:, None, :]
    mask = (jnp.abs(qi - kj) <= radius) & (kj >= 0) & (kj < L)
    s = jnp.einsum('...nqd,...nkd->...nqk', qb, kb, preferred_element_type=jnp.float32)
    s = jnp.where(mask, s, NEG_INF)
    m = jnp.max(s, -1)
    p = jnp.exp(s - m[..., None])
    l = jnp.sum(p, -1)
    o = jnp.einsum('...nqk,...nkd->...nqd', p, vb)
    m = m.reshape(lead + (lp,))[..., :L]
    l = l.reshape(lead + (lp,))[..., :L]
    o = o.reshape(lead + (lp, hd))[..., :L, :]
    return m, l, o


def dilated_attention(q, k, v):
    b, s, h, hd = q.shape
    ms, ls, outs = [], [], []
    for window, dil in A_CONFIGS:
        radius = window // (2 * dil)
        L = s // dil

        def to_sub(t):
            return t.reshape(b, L, dil, h, hd).transpose(0, 2, 3, 1, 4)

        m, l, o = banded_window_stats(to_sub(q), to_sub(k), to_sub(v), radius)
        ms.append(m.transpose(0, 3, 1, 2).reshape(b, s, h))
        ls.append(l.transpose(0, 3, 1, 2).reshape(b, s, h))
        outs.append(o.transpose(0, 3, 1, 2, 4).reshape(b, s, h, hd))
    m_all = jnp.stack(ms)
    l_all = jnp.stack(ls)
    o_all = jnp.stack(outs)
    m_max = jnp.max(m_all, 0)
    w = jnp.exp(m_all - m_max)
    out = jnp.sum(w[..., None] * o_all, 0) / jnp.sum(w * l_all, 0)[..., None]
    return out.astype(q.dtype)


def axial_gqa(q, k, v):
    b, s, hq, hd = q.shape
    hkv = k.shape[2]
    g = hq // hkv
    nq = s // Q_BLOCK
    qb = q.reshape(b, nq, Q_BLOCK, hkv, g, hd).transpose(1, 0, 2, 3, 4, 5)

    def block(qblk):
        sc = jnp.einsum('bqhgd,bkhd->bhgqk', qblk, k, preferred_element_type=jnp.float32)
        p = jax.nn.softmax(sc, -1)
        return jnp.einsum('bhgqk,bkhd->bqhgd', p.astype(v.dtype), v)

    o = lax.map(block, qb)
    return o.transpose(1, 0, 2, 3, 4, 5).reshape(b, s, hq * hd)


def neighbourhood_attention(q, k, v, rpb):
    b, s, h, hd = q.shape
    rows = s // GRID_W
    kr = min(C_ROWS_MAX, rows)

    def grid(t):
        return t.reshape(b, rows, GRID_W, h, hd).transpose(0, 3, 1, 2, 4)

    qg, kg, vg = grid(q), grid(k), grid(v)
    r = jnp.arange(rows)
    r0 = jnp.clip(r - kr // 2, 0, rows - kr)
    row_idx = r0[:, None] + jnp.arange(kr)[None, :]
    kn = kg[:, :, row_idx]
    vn = vg[:, :, row_idx]
    c = jnp.arange(GRID_W)
    c0 = jnp.clip(c - C_COLS // 2, 0, GRID_W - C_COLS)
    col_mask = (c[None, :] >= c0[:, None]) & (c[None, :] < c0[:, None] + C_COLS)
    dr = row_idx - r[:, None] + (C_ROWS_MAX - 1)
    dc = jnp.clip(c[None, :] - c[:, None] + (C_COLS - 1), 0, 2 * C_COLS - 2)
    bias = rpb[:, dr[:, None, :, None], dc[None, :, None, :]]
    sc = jnp.einsum('bhrqd,bhrikd->bhrqik', qg, kn, preferred_element_type=jnp.float32)
    sc = sc + bias.astype(jnp.float32)
    sc = jnp.where(col_mask[:, None, :], sc, NEG_INF)
    p = jax.nn.softmax(sc.reshape(b, h, rows, GRID_W, kr * GRID_W), -1)
    p = p.reshape(b, h, rows, GRID_W, kr, GRID_W)
    o = jnp.einsum('bhrqik,bhrikd->bhrqd', p.astype(v.dtype), vn)
    return o.transpose(0, 2, 3, 1, 4).reshape(b, s, h * hd)


def _fwd_setup_inputs(seed: int = 0) -> dict:
    key = jax.random.key(seed)
    ks = jax.random.split(key, 16)
    f32 = jnp.float32

    def normal(k, shape, scale):
        return jax.random.normal(k, shape, f32) * scale

    x = normal(ks[0], (BATCH, SEQ, D_MODEL), 1.0)
    w_in = normal(ks[1], (DEPTH, D_MODEL, IN_COLS), D_MODEL ** -0.5)
    b_gate = normal(ks[2], (DEPTH, N_BRANCHES * D_MODEL), 0.02)
    q_norm_b = 1.0 + normal(ks[3], (DEPTH, HEAD_DIM), 0.02)
    k_norm_b = 1.0 + normal(ks[4], (DEPTH, HEAD_DIM), 0.02)
    rpb_c = normal(ks[5], (DEPTH, C_HEADS, 2 * C_ROWS_MAX - 1, 2 * C_COLS - 1), 0.1)
    w_branch_a = normal(ks[6], (DEPTH, A_WIDTH, D_MODEL), A_WIDTH ** -0.5 * DEEPNORM_BETA)
    w_branch_b = normal(ks[7], (DEPTH, B_Q_WIDTH, D_MODEL), B_Q_WIDTH ** -0.5 * DEEPNORM_BETA)
    w_branch_c = normal(ks[8], (DEPTH, C_WIDTH, D_MODEL), C_WIDTH ** -0.5 * DEEPNORM_BETA)
    w_out = normal(ks[9], (DEPTH, D_MODEL, D_MODEL), D_MODEL ** -0.5 * DEEPNORM_BETA)
    ln1_g = 1.0 + normal(ks[10], (DEPTH, D_MODEL), 0.02)
    ln1_b = normal(ks[11], (DEPTH, D_MODEL), 0.02)
    w_up = normal(ks[12], (DEPTH, D_MODEL, D_FF), D_MODEL ** -0.5)
    w_down = normal(ks[13], (DEPTH, D_FF, D_MODEL), D_FF ** -0.5 * DEEPNORM_BETA)
    ln2_g = 1.0 + normal(ks[14], (DEPTH, D_MODEL), 0.02)
    ln2_b = normal(ks[15], (DEPTH, D_MODEL), 0.02)
    return {"x": x, "w_in": w_in, "b_gate": b_gate, "q_norm_b": q_norm_b,
            "k_norm_b": k_norm_b, "rpb_c": rpb_c, "w_branch_a": w_branch_a,
            "w_branch_b": w_branch_b, "w_branch_c": w_branch_c, "w_out": w_out,
            "ln1_g": ln1_g, "ln1_b": ln1_b, "w_up": w_up, "w_down": w_down,
            "ln2_g": ln2_g, "ln2_b": ln2_b}


def _fwd_reference(x, w_in, b_gate, q_norm_b, k_norm_b, rpb_c, w_branch_a, w_branch_b,
              w_branch_c, w_out, ln1_g, ln1_b, w_up, w_down, ln2_g, ln2_b):
    b, s, _ = x.shape
    pos = jnp.arange(s)
    row = pos // GRID_W
    col = pos % GRID_W
    scale = HEAD_DIM ** -0.5
    splits = _split_points()

    def heads(t, n):
        return t.reshape(b, s, n, HEAD_DIM)

    for layer in range(DEPTH):
        h = x @ w_in[layer]
        qa, ka, va, qb, kb, vb, qc, kc, vc, gate_logits = jnp.split(h, splits, axis=-1)

        qa = partial_rotary(heads(qa, A_HEADS), pos) * scale
        ka = partial_rotary(heads(ka, A_HEADS), pos)
        oa = dilated_attention(qa, ka, heads(va, A_HEADS)).reshape(b, s, A_WIDTH)

        qb = axial_rotary(rms_norm(heads(qb, B_Q_HEADS), q_norm_b[layer]), row, col) * scale
        kb = axial_rotary(rms_norm(heads(kb, B_KV_HEADS), k_norm_b[layer]), row, col)
        ob = axial_gqa(qb, kb, heads(vb, B_KV_HEADS))

        oc = neighbourhood_attention(heads(qc, C_HEADS) * scale, heads(kc, C_HEADS),
                                     heads(vc, C_HEADS), rpb_c[layer])

        g = jax.nn.sigmoid((gate_logits + b_gate[layer]).astype(jnp.float32)).astype(x.dtype)
        g = g.reshape(b, s, N_BRANCHES, D_MODEL)
        merged = (g[:, :, 0] * (oa @ w_branch_a[layer])
                  + g[:, :, 1] * (ob @ w_branch_b[layer])
                  + g[:, :, 2] * (oc @ w_branch_c[layer]))
        mix = merged @ w_out[layer]
        x = layer_norm(DEEPNORM_ALPHA * x + mix, ln1_g[layer], ln1_b[layer])

        ff = jnp.square(jax.nn.relu(x @ w_up[layer])) @ w_down[layer]
        x = layer_norm(DEEPNORM_ALPHA * x + ff, ln2_g[layer], ln2_b[layer])
    return x


import jax as _jax
import jax.numpy as _jnp

TWIN_FORMAT = 'train_step'
FWD_PARAMS = ['x', 'w_in', 'b_gate', 'q_norm_b', 'k_norm_b', 'rpb_c', 'w_branch_a', 'w_branch_b', 'w_branch_c', 'w_out', 'ln1_g', 'ln1_b', 'w_up', 'w_down', 'ln2_g', 'ln2_b']
TWIN_WEIGHTS = ['w_in', 'b_gate', 'q_norm_b', 'k_norm_b', 'rpb_c', 'w_branch_a', 'w_branch_b', 'w_branch_c', 'w_out', 'ln1_g', 'ln1_b', 'w_up', 'w_down', 'ln2_g', 'ln2_b']
TWIN_DIFF_INPUT = 'x'
TWIN_INPUTS = ['x', 'w_in', 'b_gate', 'q_norm_b', 'k_norm_b', 'rpb_c', 'w_branch_a', 'w_branch_b', 'w_branch_c', 'w_out', 'ln1_g', 'ln1_b', 'w_up', 'w_down', 'ln2_g', 'ln2_b', 'loss_target', 'm_w_in', 'm_b_gate', 'm_q_norm_b', 'm_k_norm_b', 'm_rpb_c', 'm_w_branch_a', 'm_w_branch_b', 'm_w_branch_c', 'm_w_out', 'm_ln1_g', 'm_ln1_b', 'm_w_up', 'm_w_down', 'm_ln2_g', 'm_ln2_b', 'v_w_in', 'v_b_gate', 'v_q_norm_b', 'v_k_norm_b', 'v_rpb_c', 'v_w_branch_a', 'v_w_branch_b', 'v_w_branch_c', 'v_w_out', 'v_ln1_g', 'v_ln1_b', 'v_w_up', 'v_w_down', 'v_ln2_g', 'v_ln2_b']
TWIN_OUTPUTS = ['loss', 'grad_x', 'grad_w_in', 'grad_b_gate', 'grad_q_norm_b', 'grad_k_norm_b', 'grad_rpb_c', 'grad_w_branch_a', 'grad_w_branch_b', 'grad_w_branch_c', 'grad_w_out', 'grad_ln1_g', 'grad_ln1_b', 'grad_w_up', 'grad_w_down', 'grad_ln2_g', 'grad_ln2_b', 'delta_w_in', 'delta_b_gate', 'delta_q_norm_b', 'delta_k_norm_b', 'delta_rpb_c', 'delta_w_branch_a', 'delta_w_branch_b', 'delta_w_branch_c', 'delta_w_out', 'delta_ln1_g', 'delta_ln1_b', 'delta_w_up', 'delta_w_down', 'delta_ln2_g', 'delta_ln2_b', 'new_m_w_in', 'new_m_b_gate', 'new_m_q_norm_b', 'new_m_k_norm_b', 'new_m_rpb_c', 'new_m_w_branch_a', 'new_m_w_branch_b', 'new_m_w_branch_c', 'new_m_w_out', 'new_m_ln1_g', 'new_m_ln1_b', 'new_m_w_up', 'new_m_w_down', 'new_m_ln2_g', 'new_m_ln2_b', 'new_v_w_in', 'new_v_b_gate', 'new_v_q_norm_b', 'new_v_k_norm_b', 'new_v_rpb_c', 'new_v_w_branch_a', 'new_v_w_branch_b', 'new_v_w_branch_c', 'new_v_w_out', 'new_v_ln1_g', 'new_v_ln1_b', 'new_v_w_up', 'new_v_w_down', 'new_v_ln2_g', 'new_v_ln2_b']
TWIN_LEAF_KINDS = {'loss': 'loss', 'grad_x': 'grad_x', 'grad_w_in': 'grad_w', 'grad_b_gate': 'grad_w', 'grad_q_norm_b': 'grad_w', 'grad_k_norm_b': 'grad_w', 'grad_rpb_c': 'grad_w', 'grad_w_branch_a': 'grad_w', 'grad_w_branch_b': 'grad_w', 'grad_w_branch_c': 'grad_w', 'grad_w_out': 'grad_w', 'grad_ln1_g': 'grad_w', 'grad_ln1_b': 'grad_w', 'grad_w_up': 'grad_w', 'grad_w_down': 'grad_w', 'grad_ln2_g': 'grad_w', 'grad_ln2_b': 'grad_w', 'delta_w_in': 'delta_w', 'delta_b_gate': 'delta_w', 'delta_q_norm_b': 'delta_w', 'delta_k_norm_b': 'delta_w', 'delta_rpb_c': 'delta_w', 'delta_w_branch_a': 'delta_w', 'delta_w_branch_b': 'delta_w', 'delta_w_branch_c': 'delta_w', 'delta_w_out': 'delta_w', 'delta_ln1_g': 'delta_w', 'delta_ln1_b': 'delta_w', 'delta_w_up': 'delta_w', 'delta_w_down': 'delta_w', 'delta_ln2_g': 'delta_w', 'delta_ln2_b': 'delta_w', 'new_m_w_in': 'new_m', 'new_m_b_gate': 'new_m', 'new_m_q_norm_b': 'new_m', 'new_m_k_norm_b': 'new_m', 'new_m_rpb_c': 'new_m', 'new_m_w_branch_a': 'new_m', 'new_m_w_branch_b': 'new_m', 'new_m_w_branch_c': 'new_m', 'new_m_w_out': 'new_m', 'new_m_ln1_g': 'new_m', 'new_m_ln1_b': 'new_m', 'new_m_w_up': 'new_m', 'new_m_w_down': 'new_m', 'new_m_ln2_g': 'new_m', 'new_m_ln2_b': 'new_m', 'new_v_w_in': 'new_v', 'new_v_b_gate': 'new_v', 'new_v_q_norm_b': 'new_v', 'new_v_k_norm_b': 'new_v', 'new_v_rpb_c': 'new_v', 'new_v_w_branch_a': 'new_v', 'new_v_w_branch_b': 'new_v', 'new_v_w_branch_c': 'new_v', 'new_v_w_out': 'new_v', 'new_v_ln1_g': 'new_v', 'new_v_ln1_b': 'new_v', 'new_v_w_up': 'new_v', 'new_v_w_down': 'new_v', 'new_v_ln2_g': 'new_v', 'new_v_ln2_b': 'new_v'}


def _forward(args):
    return _fwd_reference(*[args[k] for k in FWD_PARAMS])


def _output_shape():
    def fwd():
        inp = _fwd_setup_inputs(0)
        return _fwd_reference(*[inp[k] for k in FWD_PARAMS])
    out = _jax.eval_shape(fwd)
    return out.shape, out.dtype

N_MICROBATCH = 1
ADAM_LR = 0.001
ADAM_B1 = 0.9
ADAM_B2 = 0.999
ADAM_EPS = 1e-08
ADAM_WD = 0.01
ADAM_STEP = 10
PER_EXAMPLE_BATCH_AXIS = {'x': 0, 'loss_target': 0}
SHARED_INPUTS = []
_WEIGHT_DTYPES = {'w_in': _jnp.float32, 'b_gate': _jnp.float32, 'q_norm_b': _jnp.float32, 'k_norm_b': _jnp.float32, 'rpb_c': _jnp.float32, 'w_branch_a': _jnp.float32, 'w_branch_b': _jnp.float32, 'w_branch_c': _jnp.float32, 'w_out': _jnp.float32, 'ln1_g': _jnp.float32, 'ln1_b': _jnp.float32, 'w_up': _jnp.float32, 'w_down': _jnp.float32, 'ln2_g': _jnp.float32, 'ln2_b': _jnp.float32}
MOMENT_SCALE = {'w_in': 4.313195e-03, 'b_gate': 2.841246e-03, 'q_norm_b': 5.692599e-03, 'k_norm_b': 5.911681e-03, 'rpb_c': 1.951414e-03, 'w_branch_a': 1.634098e-02, 'w_branch_b': 1.849025e-02, 'w_branch_c': 1.469239e-02, 'w_out': 2.826823e-02, 'ln1_g': 1.863386e+00, 'ln1_b': 1.089619e+00, 'w_up': 4.473789e-02, 'w_down': 3.008574e-01, 'ln2_g': 3.225605e+01, 'ln2_b': 7.576857e+00}


def _to_microbatches(a, axis):
    t = _jnp.moveaxis(a, axis, 0)
    t = t.reshape((N_MICROBATCH, t.shape[0] // N_MICROBATCH) + t.shape[1:])
    return _jnp.moveaxis(t, 1, axis + 1)


def setup_inputs(seed: int = 0) -> dict:
    inp = _fwd_setup_inputs(seed)
    key = _jax.random.fold_in(_jax.random.key(seed), 7919)
    shape, _ = _output_shape()
    out = dict(inp)
    out["loss_target"] = _jax.random.normal(_jax.random.fold_in(key, 0), shape, _jnp.float32)
    for i, name in enumerate(TWIN_WEIGHTS):
        w = inp[name].astype(_jnp.float32)
        if MOMENT_SCALE is None:
            s = _jnp.sqrt(_jnp.mean(_jnp.square(w)) + 1e-30)
        else:
            s = MOMENT_SCALE[name]
        km, kv = _jax.random.split(_jax.random.fold_in(key, i + 1))
        out[name] = w
        out["m_" + name] = s * _jax.random.normal(km, w.shape, _jnp.float32)
        out["v_" + name] = (s * s) * _jax.random.uniform(kv, w.shape, _jnp.float32, 0.5, 1.5)
    if N_MICROBATCH > 1:
        for name, axis in PER_EXAMPLE_BATCH_AXIS.items():
            out[name] = _to_microbatches(out[name], axis)
    return {'x': out['x'], 'w_in': out['w_in'], 'b_gate': out['b_gate'], 'q_norm_b': out['q_norm_b'], 'k_norm_b': out['k_norm_b'], 'rpb_c': out['rpb_c'], 'w_branch_a': out['w_branch_a'], 'w_branch_b': out['w_branch_b'], 'w_branch_c': out['w_branch_c'], 'w_out': out['w_out'], 'ln1_g': out['ln1_g'], 'ln1_b': out['ln1_b'], 'w_up': out['w_up'], 'w_down': out['w_down'], 'ln2_g': out['ln2_g'], 'ln2_b': out['ln2_b'], 'loss_target': out['loss_target'], 'm_w_in': out['m_w_in'], 'm_b_gate': out['m_b_gate'], 'm_q_norm_b': out['m_q_norm_b'], 'm_k_norm_b': out['m_k_norm_b'], 'm_rpb_c': out['m_rpb_c'], 'm_w_branch_a': out['m_w_branch_a'], 'm_w_branch_b': out['m_w_branch_b'], 'm_w_branch_c': out['m_w_branch_c'], 'm_w_out': out['m_w_out'], 'm_ln1_g': out['m_ln1_g'], 'm_ln1_b': out['m_ln1_b'], 'm_w_up': out['m_w_up'], 'm_w_down': out['m_w_down'], 'm_ln2_g': out['m_ln2_g'], 'm_ln2_b': out['m_ln2_b'], 'v_w_in': out['v_w_in'], 'v_b_gate': out['v_b_gate'], 'v_q_norm_b': out['v_q_norm_b'], 'v_k_norm_b': out['v_k_norm_b'], 'v_rpb_c': out['v_rpb_c'], 'v_w_branch_a': out['v_w_branch_a'], 'v_w_branch_b': out['v_w_branch_b'], 'v_w_branch_c': out['v_w_branch_c'], 'v_w_out': out['v_w_out'], 'v_ln1_g': out['v_ln1_g'], 'v_ln1_b': out['v_ln1_b'], 'v_w_up': out['v_w_up'], 'v_w_down': out['v_w_down'], 'v_ln2_g': out['v_ln2_g'], 'v_ln2_b': out['v_ln2_b']}


def _loss(weights, diff, rest, loss_target):
    with _jax.named_scope("forward"):
        args = {**rest, TWIN_DIFF_INPUT: diff, **{k: w.astype(_WEIGHT_DTYPES[k]) for k, w in weights.items()}}
        y = _forward(args)
    with _jax.named_scope("loss_head"):
        err = _jnp.square(y.astype(_jnp.float32) - loss_target)
        return 0.5 * _jnp.sum(_jnp.mean(err, axis=-1)) if err.ndim else 0.5 * err


def _adamw(w, g, m, v):
    m = ADAM_B1 * m + (1.0 - ADAM_B1) * g
    v = ADAM_B2 * v + (1.0 - ADAM_B2) * _jnp.square(g)
    m_hat = m / (1.0 - ADAM_B1 ** ADAM_STEP)
    v_hat = v / (1.0 - ADAM_B2 ** ADAM_STEP)
    delta = -ADAM_LR * (m_hat / (_jnp.sqrt(v_hat) + ADAM_EPS) + ADAM_WD * w)
    return delta, m, v


def reference(x, w_in, b_gate, q_norm_b, k_norm_b, rpb_c, w_branch_a, w_branch_b, w_branch_c, w_out, ln1_g, ln1_b, w_up, w_down, ln2_g, ln2_b, loss_target, m_w_in, m_b_gate, m_q_norm_b, m_k_norm_b, m_rpb_c, m_w_branch_a, m_w_branch_b, m_w_branch_c, m_w_out, m_ln1_g, m_ln1_b, m_w_up, m_w_down, m_ln2_g, m_ln2_b, v_w_in, v_b_gate, v_q_norm_b, v_k_norm_b, v_rpb_c, v_w_branch_a, v_w_branch_b, v_w_branch_c, v_w_out, v_ln1_g, v_ln1_b, v_w_up, v_w_down, v_ln2_g, v_ln2_b):
    given = dict(x=x, w_in=w_in, b_gate=b_gate, q_norm_b=q_norm_b, k_norm_b=k_norm_b, rpb_c=rpb_c, w_branch_a=w_branch_a, w_branch_b=w_branch_b, w_branch_c=w_branch_c, w_out=w_out, ln1_g=ln1_g, ln1_b=ln1_b, w_up=w_up, w_down=w_down, ln2_g=ln2_g, ln2_b=ln2_b, loss_target=loss_target, m_w_in=m_w_in, m_b_gate=m_b_gate, m_q_norm_b=m_q_norm_b, m_k_norm_b=m_k_norm_b, m_rpb_c=m_rpb_c, m_w_branch_a=m_w_branch_a, m_w_branch_b=m_w_branch_b, m_w_branch_c=m_w_branch_c, m_w_out=m_w_out, m_ln1_g=m_ln1_g, m_ln1_b=m_ln1_b, m_w_up=m_w_up, m_w_down=m_w_down, m_ln2_g=m_ln2_g, m_ln2_b=m_ln2_b, v_w_in=v_w_in, v_b_gate=v_b_gate, v_q_norm_b=v_q_norm_b, v_k_norm_b=v_k_norm_b, v_rpb_c=v_rpb_c, v_w_branch_a=v_w_branch_a, v_w_branch_b=v_w_branch_b, v_w_branch_c=v_w_branch_c, v_w_out=v_w_out, v_ln1_g=v_ln1_g, v_ln1_b=v_ln1_b, v_w_up=v_w_up, v_w_down=v_w_down, v_ln2_g=v_ln2_g, v_ln2_b=v_ln2_b)
    weights = {n: given[n] for n in TWIN_WEIGHTS}
    shared = {n: given[n] for n in SHARED_INPUTS}
    per_example = {n: given[n] for n in ['x']}
    grad_fn = _jax.value_and_grad(_loss, argnums=(0, 1))

    def one_microbatch(ex, loss_target):
        ex = dict(ex)
        diff = ex.pop(TWIN_DIFF_INPUT)
        return grad_fn(weights, diff, {**shared, **ex}, loss_target)

    if N_MICROBATCH == 1:
        loss, (grad_w, grad_x) = one_microbatch(per_example, given["loss_target"])
    else:
        def body(carry, xs):
            loss_sum, grad_sum = carry
            l_k, (gw_k, gx_k) = one_microbatch(xs[0], xs[1])
            with _jax.named_scope("update"):
                return (loss_sum + l_k, _jax.tree.map(_jnp.add, grad_sum, gw_k)), gx_k

        init = (_jnp.zeros((), _jnp.float32), _jax.tree.map(_jnp.zeros_like, weights))
        (loss, grad_w), grad_x = _jax.lax.scan(body, init, (per_example, given["loss_target"]))
    with _jax.named_scope("update"):
        delta_w, new_m, new_v = {}, {}, {}
        for n in TWIN_WEIGHTS:
            delta_w[n], new_m[n], new_v[n] = _adamw(weights[n], grad_w[n], given["m_" + n], given["v_" + n])
    return (loss, grad_x, *[grad_w[n] for n in TWIN_WEIGHTS], *[delta_w[n] for n in TWIN_WEIGHTS],
            *[new_m[n] for n in TWIN_WEIGHTS], *[new_v[n] for n in TWIN_WEIGHTS])
```

```python
import functools

import numpy as np
import jax
import jax.numpy as jnp
from jax import lax
from jax.experimental import pallas as pl
from jax.experimental.pallas import tpu as pltpu

F32 = jnp.float32
_MXU = jnp.bfloat16

HEAD = 64
A_W, BQ_W, BKV_W, C_W = 256, 512, 128, 256
QKV_W = 768
A_DILATIONS = (1, 4, 16)
A_RADIUS = 64
A_ROPE_HALF = 8
AX_ROPE_HALF = 16
ROPE_THETA = 500000.0
AX_THETA = 10000.0
GRID_W = 64
C_ROWS = 8
C_COLS = 16
BAND = 128
BT_TILES = 18
LN_EPS = 1e-5
RMS_EPS = 1e-6
NEG = -1e30
SCALE = HEAD ** -0.5
ADAM_LR, ADAM_B1, ADAM_B2, ADAM_EPS, ADAM_WD, ADAM_STEP = 0.001, 0.9, 0.999, 1e-08, 0.01, 10
V7X_VMEM_LIMIT = 48 * 1024 * 1024
MESH = pl.DeviceIdType.MESH
ANY = pl.BlockSpec(memory_space=pl.ANY)


def _params(*sem):
    return pltpu.CompilerParams(dimension_semantics=sem or None, vmem_limit_bytes=V7X_VMEM_LIMIT)


def _tile(n, pref, align=128):
    if n <= pref:
        return n
    t = (pref // align) * align
    while t >= align:
        if n % t == 0:
            return t
        t -= align
    return n


def _mm(a, b, *, name, mode="nn", outs=((F32),), epilogue=None, extras=(), tm=512, tn=1024, tk=1024, exact=False):
    if mode == "nn":
        m, k = a.shape
    else:
        k, m = a.shape
    k2, n = b.shape
    assert k == k2, (a.shape, b.shape, mode)
    tm, tn, tk = _tile(m, tm), _tile(n, tn), _tile(k, tk)
    nk = k // tk
    n_ex, n_out = len(extras), len(outs)
    mx = F32 if exact else _MXU
    prec = lax.Precision.HIGHEST if exact else None

    def body(*refs):
        a_ref, b_ref = refs[0], refs[1]
        ex = refs[2:2 + n_ex]
        out_refs = refs[2 + n_ex:2 + n_ex + n_out]
        acc = refs[-1]
        kk = pl.program_id(2)

        @pl.when(kk == 0)
        def _():
            acc[...] = jnp.zeros_like(acc)

        av, bv = a_ref[...].astype(mx), b_ref[...].astype(mx)
        dims = (((1,), (0,)), ((), ())) if mode == "nn" else (((0,), (0,)), ((), ()))
        acc[...] += lax.dot_general(av, bv, dims, preferred_element_type=F32, precision=prec)

        @pl.when(kk == nk - 1)
        def _():
            res = acc[...]
            vals = epilogue(res, *[e[...] for e in ex]) if epilogue is not None else (res,)
            for o, v in zip(out_refs, vals):
                o[...] = v.astype(o.dtype)

    a_spec = pl.BlockSpec((tm, tk), lambda i, j, kk: (i, kk)) if mode == "nn" else pl.BlockSpec((tk, tm), lambda i, j, kk: (kk, i))
    o_spec = pl.BlockSpec((tm, tn), lambda i, j, kk: (i, j))
    res = pl.pallas_call(
        body, name=name, grid=(m // tm, n // tn, nk),
        in_specs=[a_spec, pl.BlockSpec((tk, tn), lambda i, j, kk: (kk, j))] + [o_spec] * n_ex,
        out_specs=[o_spec] * n_out,
        out_shape=[jax.ShapeDtypeStruct((m, n), d) for d in outs],
        scratch_shapes=[pltpu.VMEM((tm, tn), F32)],
        compiler_params=_params("parallel", "parallel", "arbitrary"),
    )(a, b, *extras)
    return res[0] if n_out == 1 else res


def _rows(tm, width, cb=0):
    return pl.BlockSpec((tm, width), lambda t: (t, cb))


def _whole(arr):
    nd = arr.ndim
    return pl.BlockSpec(arr.shape, lambda t: (0,) * nd)


def _rowwise(fn, name, rows, tm, ins, outs):
    n_in = len(ins)

    def body(*refs):
        vals = fn(*[r[...] for r in refs[:n_in]])
        first = pl.program_id(0) == 0
        for (ncols, _, kind), o, v in zip(outs, refs[n_in:], vals):
            if kind == "row":
                o[...] = v.astype(o.dtype)
            else:
                part = v.reshape(tm // 8, 8, ncols).sum(0)

                @pl.when(first)
                def _(o=o, part=part):
                    o[...] = part

                @pl.when(jnp.logical_not(first))
                def _(o=o, part=part):
                    o[...] += part

    out_specs = [_rows(tm, n) if kind == "row" else pl.BlockSpec((8, n), lambda t: (0, 0)) for n, _, kind in outs]
    out_shape = [jax.ShapeDtypeStruct((rows if kind == "row" else 8, n), d) for n, d, kind in outs]
    res = pl.pallas_call(
        body, name=name, grid=(rows // tm,),
        in_specs=[s for _, s in ins], out_specs=out_specs, out_shape=out_shape,
        compiler_params=_params("arbitrary"),
    )(*[a for a, _ in ins])
    return res


def _lane_lo(width=128):
    return (lax.broadcasted_iota(jnp.int32, (1, width), 1) & (HEAD * 2 - 1)) < HEAD


def _group_sum(x):
    w = x.shape[-1]
    lane = lax.broadcasted_iota(jnp.int32, (1, w), 1)
    k = HEAD // 2
    while k >= 1:
        x = x + jnp.where((lane & k) != 0, pltpu.roll(x, k, 1), pltpu.roll(x, w - k, 1))
        k //= 2
    return x


def _rot(x, c, sm, sp, shift):
    w = x.shape[-1]
    return x * c + pltpu.roll(x, w - shift, 1) * sm + pltpu.roll(x, shift, 1) * sp


def _rot_t(dy, c, sm, sp, shift):
    w = dy.shape[-1]
    return dy * c + pltpu.roll(dy * sm, shift, 1) + pltpu.roll(dy * sp, w - shift, 1)


def _rope_tables(pos_parts, half, thetas):
    cs, sms, sps = [], [], []
    for pos, theta in zip(pos_parts, thetas):
        inv = theta ** (-jnp.arange(half, dtype=F32) / half)
        ang = pos.astype(F32)[:, None] * inv[None, :]
        co, si, ze = jnp.cos(ang), jnp.sin(ang), jnp.zeros_like(ang)
        cs += [co, co]
        sms += [-si, ze]
        sps += [ze, si]
    return [jnp.concatenate(t, axis=1) for t in (cs, sms, sps)]


def _tables(s):
    pos = jnp.arange(s)
    ca, sma, spa = _rope_tables([pos], A_ROPE_HALF, [ROPE_THETA])
    pad = HEAD - 2 * A_ROPE_HALF
    ca = jnp.concatenate([ca, jnp.ones((s, pad), F32)], 1)
    sma, spa = [jnp.concatenate([t, jnp.zeros((s, pad), F32)], 1) for t in (sma, spa)]
    tab_a = [jnp.tile(t, (1, A_W // HEAD)) for t in (ca, sma, spa)]
    ax = _rope_tables([pos // GRID_W, pos % GRID_W], AX_ROPE_HALF, [AX_THETA, AX_THETA])
    tab_q = [jnp.tile(t, (1, BQ_W // HEAD)) for t in ax]
    tab_k = [jnp.tile(t, (1, BKV_W // HEAD)) for t in ax]
    return tab_a, tab_q, tab_k


def _prep_a(ha, tab, tm):
    s = ha.shape[0]

    def fn(h, c, sm, sp):
        q, k, v = h[:, :A_W], h[:, A_W:2 * A_W], h[:, 2 * A_W:]
        return _rot(q, c, sm, sp, A_ROPE_HALF) * SCALE, _rot(k, c, sm, sp, A_ROPE_HALF), v

    return _rowwise(fn, "prep_a", s, tm, [(ha, _rows(tm, QKV_W))] + [(t, _rows(tm, A_W)) for t in tab],
                    [(A_W, _MXU, "row")] * 3)


def _rms(x, g):
    ms = _group_sum(x * x) * (1.0 / HEAD)
    return x * lax.rsqrt(ms + RMS_EPS) * g


def _prep_b(hb, gq, gk, tab_q, tab_k, tm):
    s = hb.shape[0]

    def fn(h, gq, gk, cq, smq, spq, ck, smk, spk):
        xq, xk, v = h[:, :BQ_W], h[:, BQ_W:BQ_W + BKV_W], h[:, BQ_W + BKV_W:]
        q = _rot(_rms(xq, gq), cq, smq, spq, AX_ROPE_HALF) * SCALE
        k = _rot(_rms(xk, gk), ck, smk, spk, AX_ROPE_HALF)
        lo = _lane_lo()
        kr, vr = pltpu.roll(k, HEAD, 1), pltpu.roll(v, HEAD, 1)
        kd = jnp.concatenate([jnp.where(lo, k, kr), jnp.where(lo, kr, k)], 1)
        vd = jnp.concatenate([jnp.where(lo, v, vr), jnp.where(lo, vr, v)], 1)
        return q, kd, vd

    ins = [(hb, _rows(tm, QKV_W)), (gq, _whole(gq)), (gk, _whole(gk))]
    ins += [(t, _rows(tm, BQ_W)) for t in tab_q] + [(t, _rows(tm, BKV_W)) for t in tab_k]
    return _rowwise(fn, "prep_b", s, tm, ins, [(BQ_W, _MXU, "row"), (2 * BKV_W, _MXU, "row"), (2 * BKV_W, _MXU, "row")])


def _prep_c(hc, tm):
    def fn(h):
        return h[:, :C_W] * SCALE, h[:, C_W:2 * C_W], h[:, 2 * C_W:]

    return _rowwise(fn, "prep_c", hc.shape[0], tm, [(hc, _rows(tm, QKV_W))], [(C_W, _MXU, "row")] * 3)


def _combine_a(os_, ms, ls, tm):
    s = os_[0].shape[0]

    def fn(o1, o2, o3, m1, m2, m3, l1, l2, l3):
        lo = _lane_lo()
        outs, lses = [], []
        for p in range(A_W // 128):
            st = slice(p * 256, (p + 1) * 256)
            mm = [m[:, st] for m in (m1, m2, m3)]
            ll = [l[:, st] for l in (l1, l2, l3)]
            mmax = jnp.maximum(jnp.maximum(mm[0], mm[1]), mm[2])
            ws = [jnp.exp(m - mmax) for m in mm]
            den = ws[0] * ll[0] + ws[1] * ll[1] + ws[2] * ll[2]
            lses.append(mmax + jnp.log(den))
            num = sum(jnp.where(lo, w[:, :128], w[:, 128:]) * o[:, p * 128:(p + 1) * 128] for w, o in zip(ws, (o1, o2, o3)))
            outs.append(num / jnp.where(lo, den[:, :128], den[:, 128:]))
        return jnp.concatenate(outs, 1), jnp.concatenate(lses, 1)

    ins = [(o, _rows(tm, A_W)) for o in os_] + [(m, _rows(tm, 2 * A_W)) for m in ms] + [(l, _rows(tm, 2 * A_W)) for l in ls]
    return _rowwise(fn, "combine_a", s, tm, ins, [(A_W, F32, "row"), (2 * A_W, F32, "row")])


def _gates(hg, bg, d):
    return [jax.nn.sigmoid(hg[:, i * d:(i + 1) * d] + bg[:, i * d:(i + 1) * d]) for i in range(3)]


def _gate_merge(hg, bg, pa, pb, pc, tm):
    s, d = pa.shape

    def fn(hg, bg, pa, pb, pc):
        g = _gates(hg, bg, d)
        return (g[0] * pa + g[1] * pb + g[2] * pc,)

    ins = [(hg, _rows(tm, 3 * d)), (bg, _whole(bg))] + [(p, _rows(tm, d)) for p in (pa, pb, pc)]
    return _rowwise(fn, "gate_merge", s, tm, ins, [(d, _MXU, "row")])[0]


def _gate_bwd(dm, hg, bg, pa, pb, pc, tm):
    s, d = pa.shape

    def fn(dm, hg, bg, pa, pb, pc):
        g = _gates(hg, bg, d)
        dlog = jnp.concatenate([dm * p * gi * (1.0 - gi) for p, gi in zip((pa, pb, pc), g)], 1)
        return dm * g[0], dm * g[1], dm * g[2], dlog, dlog

    ins = [(dm, _rows(tm, d)), (hg, _rows(tm, 3 * d)), (bg, _whole(bg))] + [(p, _rows(tm, d)) for p in (pa, pb, pc)]
    return _rowwise(fn, "gate_bwd", s, tm, ins, [(d, _MXU, "row")] * 3 + [(3 * d, _MXU, "row"), (3 * d, F32, "acc")])


def _ln_stats(r):
    mu = jnp.mean(r, -1, keepdims=True)
    xc = r - mu
    var = jnp.mean(xc * xc, -1, keepdims=True)
    rstd = lax.rsqrt(var + LN_EPS)
    return xc * rstd, rstd


def _ln_fwd(x, br, g, b, alpha, name, tm):
    s, d = x.shape

    def fn(x, br, g, b):
        r = alpha * x + br
        xhat, _ = _ln_stats(r)
        y = xhat * g + b
        return r, y, y

    ins = [(x, _rows(tm, d)), (br, _rows(tm, d)), (g, _whole(g)), (b, _whole(b))]
    return _rowwise(fn, name, s, tm, ins, [(d, F32, "row"), (d, F32, "row"), (d, _MXU, "row")])


def _ln_bwd(dy, r, g, name, tm):
    s, d = r.shape

    def fn(dy, r, g):
        xhat, rstd = _ln_stats(r)
        dxh = dy * g
        dr = rstd * (dxh - jnp.mean(dxh, -1, keepdims=True) - xhat * jnp.mean(dxh * xhat, -1, keepdims=True))
        return dr, dy * xhat, dy

    ins = [(dy, _rows(tm, d)), (r, _rows(tm, d)), (g, _whole(g))]
    return _rowwise(fn, name, s, tm, ins, [(d, F32, "row"), (d, F32, "acc"), (d, F32, "acc")])


def _loss_head(y, target, tm):
    s, d = y.shape

    def fn(y, t):
        diff = y - t
        return diff * diff, diff * (1.0 / d)

    sq, dy = _rowwise(fn, "loss_head", s, tm, [(y, _rows(tm, d)), (target, _rows(tm, d))], [(d, F32, "acc"), (d, F32, "row")])
    return sq, dy


def _post_a(dqs, dks, dvs, tab, tm):
    s = dqs[0].shape[0]

    def fn(q1, q2, q3, k1, k2, k3, v1, v2, v3, c, sm, sp):
        dq = _rot_t((q1 + q2 + q3) * SCALE, c, sm, sp, A_ROPE_HALF)
        dk = _rot_t(k1 + k2 + k3, c, sm, sp, A_ROPE_HALF)
        return (jnp.concatenate([dq, dk, v1 + v2 + v3], 1),)

    ins = [(t, _rows(tm, A_W)) for t in (*dqs, *dks, *dvs, *tab)]
    return _rowwise(fn, "post_a", s, tm, ins, [(QKV_W, _MXU, "row")])[0]


def _post_b(dq, dkd, dvd, hb, gq, gk, tab_q, tab_k, tm):
    s = dq.shape[0]

    def back(dz, x, g, c, sm, sp):
        dy = _rot_t(dz, c, sm, sp, AX_ROPE_HALF)
        rstd = lax.rsqrt(_group_sum(x * x) * (1.0 / HEAD) + RMS_EPS)
        xh = x * rstd
        dxh = dy * g
        return rstd * (dxh - xh * (_group_sum(dxh * xh) * (1.0 / HEAD))), dy * xh

    def fn(dq, dkd, dvd, h, gq, gk, cq, smq, spq, ck, smk, spk):
        lo = _lane_lo()
        dk = jnp.where(lo, dkd[:, :128], dkd[:, 128:])
        dv = jnp.where(lo, dvd[:, :128], dvd[:, 128:])
        dxq, dgq = back(dq * SCALE, h[:, :BQ_W], gq, cq, smq, spq)
        dxk, dgk = back(dk, h[:, BQ_W:BQ_W + BKV_W], gk, ck, smk, spk)
        return jnp.concatenate([dxq, dxk, dv], 1), dgq, dgk

    ins = [(dq, _rows(tm, BQ_W)), (dkd, _rows(tm, 2 * BKV_W)), (dvd, _rows(tm, 2 * BKV_W)), (hb, _rows(tm, QKV_W)),
           (gq, _whole(gq)), (gk, _whole(gk))]
    ins += [(t, _rows(tm, BQ_W)) for t in tab_q] + [(t, _rows(tm, BKV_W)) for t in tab_k]
    return _rowwise(fn, "post_b", s, tm, ins, [(QKV_W, _MXU, "row"), (BQ_W, F32, "acc"), (BKV_W, F32, "acc")])


def _post_c(dq, dk, dv, tm):
    def fn(dq, dk, dv):
        return (jnp.concatenate([dq * SCALE, dk, dv], 1),)

    return _rowwise(fn, "post_c", dq.shape[0], tm, [(t, _rows(tm, C_W)) for t in (dq, dk, dv)], [(QKV_W, _MXU, "row")])[0]


def _adamw(w, g, m, v, name):
    rows, cols = w.shape
    tm = _tile(rows, 256, 8)

    def fn(w, g, m, v):
        m = ADAM_B1 * m + (1.0 - ADAM_B1) * g
        v = ADAM_B2 * v + (1.0 - ADAM_B2) * (g * g)
        m_hat = m / (1.0 - ADAM_B1 ** ADAM_STEP)
        v_hat = v / (1.0 - ADAM_B2 ** ADAM_STEP)
        delta = -ADAM_LR * (m_hat / (jnp.sqrt(v_hat) + ADAM_EPS) + ADAM_WD * w)
        return delta, m, v

    return _rowwise(fn, name, rows, tm, [(t, _rows(tm, cols)) for t in (w, g, m, v)], [(cols, F32, "row")] * 3)


def _dot_t(a, b):
    return lax.dot_general(a, b, (((1,), (1,)), ((), ())), preferred_element_type=F32)


def _tdot(a, b):
    return lax.dot_general(a, b, (((0,), (0,)), ((), ())), preferred_element_type=F32)


def _head_masks():
    lo = _lane_lo()
    return lo, (lo, jnp.logical_not(lo))


def _rep(x, rows):
    return jnp.broadcast_to(x, (rows, 128))


def _flash_fwd(q, kd, vd, tq, tk):
    s = q.shape[0]
    tq, tk = _tile(s, tq), _tile(s, tk)
    nk = s // tk
    mx = _MXU

    def body(q_ref, k_ref, v_ref, o_ref, lse_ref, m_ref, l_ref, acc_ref):
        kk = pl.program_id(2)

        @pl.when(kk == 0)
        def _():
            m_ref[...] = jnp.full_like(m_ref, NEG)
            l_ref[...] = jnp.zeros_like(l_ref)
            acc_ref[...] = jnp.zeros_like(acc_ref)

        q2, k2, v2 = q_ref[...], k_ref[...], v_ref[...]
        lo, masks = _head_masks()
        alphas, pvs = [], []
        for h in range(2):
            sc = _dot_t(jnp.where(masks[h], q2, jnp.zeros_like(q2)), k2)
            m_prev = m_ref[h]
            m_new = jnp.maximum(m_prev, jnp.max(sc, -1, keepdims=True))
            alpha = jnp.exp(m_prev - m_new)
            p = jnp.exp(sc - m_new)
            l_ref[h] = alpha * l_ref[h] + jnp.sum(p, -1, keepdims=True)
            m_ref[h] = m_new
            alphas.append(alpha)
            pvs.append(jnp.dot(p.astype(mx), v2, preferred_element_type=F32))
        acc_ref[...] = acc_ref[...] * jnp.where(lo, alphas[0], alphas[1]) + jnp.where(lo, pvs[0], pvs[1])

        @pl.when(kk == nk - 1)
        def _():
            l0, l1 = l_ref[0], l_ref[1]
            o_ref[...] = acc_ref[...] / jnp.where(lo, l0, l1)
            lse_ref[:, :128] = _rep(m_ref[0] + jnp.log(l0), tq)
            lse_ref[:, 128:] = _rep(m_ref[1] + jnp.log(l1), tq)

    return pl.pallas_call(
        body, name="attn_b_fwd", grid=(BQ_W // 128, s // tq, nk),
        in_specs=[pl.BlockSpec((tq, 128), lambda j, i, kk: (i, j)),
                  pl.BlockSpec((tk, 128), lambda j, i, kk: (kk, j // 2)),
                  pl.BlockSpec((tk, 128), lambda j, i, kk: (kk, j // 2))],
        out_specs=[pl.BlockSpec((tq, 128), lambda j, i, kk: (i, j)), pl.BlockSpec((tq, 256), lambda j, i, kk: (i, j))],
        out_shape=[jax.ShapeDtypeStruct((s, BQ_W), F32), jax.ShapeDtypeStruct((s, 2 * BQ_W), F32)],
        scratch_shapes=[pltpu.VMEM((2, tq, 1), F32), pltpu.VMEM((2, tq, 1), F32), pltpu.VMEM((tq, 128), F32)],
        compiler_params=_params("parallel", "parallel", "arbitrary"),
    )(q, kd, vd)


def _p_and_ds(q2, k2, v2, do2, o2, lse2, masks, mask=None, bias=None):
    mx = _MXU
    out = []
    for h in range(2):
        qh = jnp.where(masks[h], q2, jnp.zeros_like(q2))
        sc = _dot_t(qh, k2)
        if bias is not None:
            sc = sc + bias[h]
        if mask is not None:
            sc = jnp.where(mask, sc, NEG)
        lse = jnp.max(lse2[:, h * 128:(h + 1) * 128], -1, keepdims=True)
        p = jnp.exp(sc - lse)
        doh = jnp.where(masks[h], do2, jnp.zeros_like(do2))
        delta = jnp.sum(doh * o2, -1, keepdims=True)
        dp = _dot_t(doh.astype(mx), v2)
        ds = p * (dp - delta)
        out.append((qh, p, ds, doh))
    return out


def _flash_dq(q, kd, vd, do, o, lse, tq, tk):
    s = q.shape[0]
    tq, tk = _tile(s, tq), _tile(s, tk)
    nk = s // tk
    mx = _MXU

    def body(q_ref, k_ref, v_ref, do_ref, o_ref, lse_ref, dq_ref, acc_ref):
        kk = pl.program_id(2)

        @pl.when(kk == 0)
        def _():
            acc_ref[...] = jnp.zeros_like(acc_ref)

        lo, masks = _head_masks()
        k2 = k_ref[...]
        hs = _p_and_ds(q_ref[...], k2, v_ref[...], do_ref[...], o_ref[...], lse_ref[...], masks)
        dqs = [jnp.dot(ds.astype(mx), k2, preferred_element_type=F32) for _, _, ds, _ in hs]
        acc_ref[...] += jnp.where(lo, dqs[0], dqs[1])

        @pl.when(kk == nk - 1)
        def _():
            dq_ref[...] = acc_ref[...]

    qs = pl.BlockSpec((tq, 128), lambda j, i, kk: (i, j))
    ks = pl.BlockSpec((tk, 128), lambda j, i, kk: (kk, j // 2))
    return pl.pallas_call(
        body, name="attn_b_dq", grid=(BQ_W // 128, s // tq, nk),
        in_specs=[qs, ks, ks, qs, qs, pl.BlockSpec((tq, 256), lambda j, i, kk: (i, j))],
        out_specs=qs, out_shape=jax.ShapeDtypeStruct((s, BQ_W), F32),
        scratch_shapes=[pltpu.VMEM((tq, 128), F32)],
        compiler_params=_params("parallel", "parallel", "arbitrary"),
    )(q, kd, vd, do, o, lse)


def _flash_dkv(q, kd, vd, do, o, lse, tq, tk):
    s = q.shape[0]
    tq, tk = _tile(s, tq), _tile(s, tk)
    nq = s // tq
    group = BQ_W // 128 // 2
    mx = _MXU

    def body(k_ref, v_ref, q_ref, do_ref, o_ref, lse_ref, dk_ref, dv_ref, dk_acc, dv_acc):
        jj, i = pl.program_id(2), pl.program_id(3)

        @pl.when((jj == 0) & (i == 0))
        def _():
            dk_acc[...] = jnp.zeros_like(dk_acc)
            dv_acc[...] = jnp.zeros_like(dv_acc)

        _, masks = _head_masks()
        hs = _p_and_ds(q_ref[...], k_ref[...], v_ref[...], do_ref[...], o_ref[...], lse_ref[...], masks)
        dk_acc[...] += sum(_tdot(ds.astype(mx), qh) for qh, _, ds, _ in hs)
        dv_acc[...] += sum(_tdot(p.astype(mx), doh.astype(mx)) for _, p, _, doh in hs)

        @pl.when((jj == group - 1) & (i == nq - 1))
        def _():
            dk_ref[...] = dk_acc[...] + pltpu.roll(dk_acc[...], HEAD, 1)
            dv_ref[...] = dv_acc[...] + pltpu.roll(dv_acc[...], HEAD, 1)

    ks = pl.BlockSpec((tk, 128), lambda e, kk, jj, i: (kk, e))
    qs = pl.BlockSpec((tq, 128), lambda e, kk, jj, i: (i, group * e + jj))
    return pl.pallas_call(
        body, name="attn_b_dkv", grid=(BKV_W // HEAD, s // tk, group, nq),
        in_specs=[ks, ks, qs, qs, qs, pl.BlockSpec((tq, 256), lambda e, kk, jj, i: (i, group * e + jj))],
        out_specs=[ks, ks], out_shape=[jax.ShapeDtypeStruct((s, 2 * BKV_W), F32)] * 2,
        scratch_shapes=[pltpu.VMEM((tk, 128), F32)] * 2,
        compiler_params=_params("parallel", "parallel", "arbitrary", "arbitrary"),
    )(kd, vd, q, do, o, lse)


class _BandA:
    nk, hq, has_bias, name = 3, 1, False, "a"

    def __init__(self, nb):
        self.nb = nb

    def kstart(self, i):
        return i - 1

    def kblock(self, i, jj):
        return jnp.clip(i - 1 + jj, 0, self.nb - 1)

    def _mask(self, qpos, kpos):
        n = self.nb * BAND
        return (jnp.abs(qpos - kpos) <= A_RADIUS) & (kpos >= 0) & (kpos < n) & (qpos >= 0) & (qpos < n)

    def kmask(self, i):
        qpos = i * BAND + lax.broadcasted_iota(jnp.int32, (BAND, 1), 0)
        kpos = (i - 1) * BAND + lax.broadcasted_iota(jnp.int32, (1, self.nk * BAND), 1)
        return self._mask(qpos, kpos)

    def qmask(self, kb):
        nq = 2 * self.hq + 1
        qpos = (kb - self.hq) * BAND + lax.broadcasted_iota(jnp.int32, (nq * BAND, 1), 0)
        kpos = kb * BAND + lax.broadcasted_iota(jnp.int32, (1, BAND), 1)
        return self._mask(qpos, kpos)


class _BandC:
    nk, hq, has_bias, name = 5, 3, True, "c"

    def __init__(self, nb):
        assert nb >= self.nk
        self.nb = nb
        self.rows = nb * BAND // GRID_W

    def kstart(self, i):
        return jnp.clip(i - 2, 0, self.nb - self.nk)

    def kblock(self, i, jj):
        return self.kstart(i) + jj

    def _mask(self, qidx, kidx):
        sh = GRID_W.bit_length() - 1
        qrow, cq = qidx >> sh, qidx & (GRID_W - 1)
        krow, ck = kidx >> sh, kidx & (GRID_W - 1)
        r0 = jnp.clip(qrow - C_ROWS // 2, 0, self.rows - C_ROWS)
        c0 = jnp.clip(cq - C_COLS // 2, 0, GRID_W - C_COLS)
        ok = (qidx >= 0) & (qidx < self.nb * BAND)
        return ok & (krow >= r0) & (krow < r0 + C_ROWS) & (ck >= c0) & (ck < c0 + C_COLS)

    def kmask(self, i):
        qidx = i * BAND + lax.broadcasted_iota(jnp.int32, (BAND, 1), 0)
        kidx = self.kstart(i) * BAND + lax.broadcasted_iota(jnp.int32, (1, self.nk * BAND), 1)
        return self._mask(qidx, kidx)

    def qmask(self, kb):
        nq = 2 * self.hq + 1
        qidx = (kb - self.hq) * BAND + lax.broadcasted_iota(jnp.int32, (nq * BAND, 1), 0)
        kidx = kb * BAND + lax.broadcasted_iota(jnp.int32, (1, BAND), 1)
        return self._mask(qidx, kidx)

    def tile_index(self, i, kblk, a):
        per = BAND // GRID_W
        return per * (kblk - i) - a + (C_ROWS - 1) + 2


def _band_bias_k(band, bt_ref, h, i):
    per = BAND // GRID_W
    rows = []
    for a in range(per):
        rows.append(jnp.concatenate([bt_ref[h, band.tile_index(i, band.kstart(i) + jj, a)] for jj in range(band.nk)], 1))
    return jnp.concatenate(rows, 0)


def _band_bias_q(band, bt_ref, h):
    per = BAND // GRID_W
    return jnp.concatenate([bt_ref[h, band.tile_index(off, 0, a)] for off in range(-band.hq, band.hq + 1) for a in range(per)], 0)


def _band_fwd(band, q, k, v, bt=None):
    n, w = q.shape
    nb, ncb, nk = n // BAND, w // 128, band.nk
    mx = _MXU
    raw = not band.has_bias

    def body(*refs):
        q_ref, k_refs, v_refs = refs[0], refs[1:1 + nk], refs[1 + nk:1 + 2 * nk]
        rest = refs[1 + 2 * nk:]
        bt_ref = rest[0] if band.has_bias else None
        outs = rest[1:] if band.has_bias else rest
        i = pl.program_id(1)
        q2 = q_ref[...]
        kcat = jnp.concatenate([r[...] for r in k_refs], 0)
        vcat = jnp.concatenate([r[...] for r in v_refs], 0)
        lo, masks = _head_masks()
        mask = band.kmask(i)
        os_, ms, ls = [], [], []
        for h in range(2):
            sc = _dot_t(jnp.where(masks[h], q2, jnp.zeros_like(q2)), kcat)
            if band.has_bias:
                sc = sc + _band_bias_k(band, bt_ref, h, i)
            sc = jnp.where(mask, sc, NEG)
            m = jnp.max(sc, -1, keepdims=True)
            p = jnp.exp(sc - m)
            ms.append(m)
            ls.append(jnp.sum(p, -1, keepdims=True))
            os_.append(jnp.dot(p.astype(mx), vcat, preferred_element_type=F32))
        if raw:
            o_ref, m_ref, l_ref = outs
            o_ref[...] = jnp.where(lo, os_[0], os_[1])
            for h in range(2):
                m_ref[:, h * 128:(h + 1) * 128] = _rep(ms[h], BAND)
                l_ref[:, h * 128:(h + 1) * 128] = _rep(ls[h], BAND)
        else:
            o_ref, lse_ref = outs
            o_ref[...] = jnp.where(lo, os_[0] / ls[0], os_[1] / ls[1])
            for h in range(2):
                lse_ref[:, h * 128:(h + 1) * 128] = _rep(ms[h] + jnp.log(ls[h]), BAND)

    qs = pl.BlockSpec((BAND, 128), lambda cb, i: (i, cb))
    ks = [pl.BlockSpec((BAND, 128), functools.partial(lambda cb, i, jj: (band.kblock(i, jj), cb), jj=jj)) for jj in range(nk)]
    st = pl.BlockSpec((BAND, 256), lambda cb, i: (i, cb))
    in_specs, args = [qs] + ks + ks, [q] + [k] * nk + [v] * nk
    if band.has_bias:
        in_specs.append(pl.BlockSpec((2, BT_TILES, GRID_W, 128), lambda cb, i: (cb, 0, 0, 0)))
        args.append(bt)
    n_stats = 2 if raw else 1
    return pl.pallas_call(
        body, name="attn_%s_fwd" % band.name, grid=(ncb, nb), in_specs=in_specs,
        out_specs=[qs] + [st] * n_stats,
        out_shape=[jax.ShapeDtypeStruct((n, w), F32)] + [jax.ShapeDtypeStruct((n, 2 * w), F32)] * n_stats,
        compiler_params=_params("parallel", "arbitrary"),
    )(*args)


def _band_dq(band, q, k, v, do, o, lse, bt=None):
    n, w = q.shape
    nb, ncb, nk = n // BAND, w // 128, band.nk
    mx = _MXU
    per = BAND // GRID_W

    def body(*refs):
        q_ref, k_refs, v_refs = refs[0], refs[1:1 + nk], refs[1 + nk:1 + 2 * nk]
        do_ref, o_ref, lse_ref = refs[1 + 2 * nk:4 + 2 * nk]
        rest = refs[4 + 2 * nk:]
        i = pl.program_id(1)
        kcat = jnp.concatenate([r[...] for r in k_refs], 0)
        vcat = jnp.concatenate([r[...] for r in v_refs], 0)
        lo, masks = _head_masks()
        bias = [_band_bias_k(band, rest[0], h, i) for h in range(2)] if band.has_bias else None
        hs = _p_and_ds(q_ref[...], kcat, vcat, do_ref[...], o_ref[...], lse_ref[...], masks, band.kmask(i), bias)
        dqs = [jnp.dot(ds.astype(mx), kcat, preferred_element_type=F32) for _, _, ds, _ in hs]
        dq_ref = rest[1] if band.has_bias else rest[0]
        dq_ref[...] = jnp.where(lo, dqs[0], dqs[1])
        if band.has_bias:
            dbt_ref = rest[2]

            @pl.when(i == 0)
            def _():
                dbt_ref[...] = jnp.zeros_like(dbt_ref)

            for h in range(2):
                ds = hs[h][2]
                for a in range(per):
                    for jj in range(nk):
                        idx = band.tile_index(i, band.kstart(i) + jj, a)
                        dbt_ref[h, idx] += ds[a * GRID_W:(a + 1) * GRID_W, jj * 128:(jj + 1) * 128]

    qs = pl.BlockSpec((BAND, 128), lambda cb, i: (i, cb))
    ks = [pl.BlockSpec((BAND, 128), functools.partial(lambda cb, i, jj: (band.kblock(i, jj), cb), jj=jj)) for jj in range(nk)]
    st = pl.BlockSpec((BAND, 256), lambda cb, i: (i, cb))
    in_specs, args = [qs] + ks + ks + [qs, qs, st], [q] + [k] * nk + [v] * nk + [do, o, lse]
    out_specs, out_shape = [qs], [jax.ShapeDtypeStruct((n, w), F32)]
    if band.has_bias:
        bts = pl.BlockSpec((2, BT_TILES, GRID_W, 128), lambda cb, i: (cb, 0, 0, 0))
        in_specs.append(bts)
        args.append(bt)
        out_specs.append(bts)
        out_shape.append(jax.ShapeDtypeStruct(bt.shape, F32))
    return pl.pallas_call(
        body, name="attn_%s_dq" % band.name, grid=(ncb, nb), in_specs=in_specs, out_specs=out_specs, out_shape=out_shape,
        compiler_params=_params("parallel", "arbitrary"),
    )(*args)


def _band_dkv(band, q, k, v, do, o, lse, bt=None):
    n, w = q.shape
    nb, ncb = n // BAND, w // 128
    nq = 2 * band.hq + 1
    mx = _MXU

    def body(*refs):
        k_ref, v_ref = refs[0], refs[1]
        groups = [refs[2 + g * nq:2 + (g + 1) * nq] for g in range(4)]
        rest = refs[2 + 4 * nq:]
        kb = pl.program_id(1)
        qcat, docat, ocat, lsecat = [jnp.concatenate([r[...] for r in g], 0) for g in groups]
        _, masks = _head_masks()
        bias = [_band_bias_q(band, rest[0], h) for h in range(2)] if band.has_bias else None
        hs = _p_and_ds(qcat, k_ref[...], v_ref[...], docat, ocat, lsecat, masks, band.qmask(kb), bias)
        dk_ref, dv_ref = rest[-2], rest[-1]
        dk_ref[...] = sum(_tdot(ds.astype(mx), qh) for qh, _, ds, _ in hs)
        dv_ref[...] = sum(_tdot(p.astype(mx), doh.astype(mx)) for _, p, _, doh in hs)

    ks = pl.BlockSpec((BAND, 128), lambda cb, kb: (kb, cb))

    def qspec(width, off):
        return pl.BlockSpec((BAND, width), lambda cb, kb: (jnp.clip(kb + off, 0, nb - 1), cb))

    offs = range(-band.hq, band.hq + 1)
    in_specs = [ks, ks] + [qspec(128, off) for _ in range(3) for off in offs] + [qspec(256, off) for off in offs]
    args = [k, v] + [q] * nq + [do] * nq + [o] * nq + [lse] * nq
    if band.has_bias:
        in_specs.append(pl.BlockSpec((2, BT_TILES, GRID_W, 128), lambda cb, kb: (cb, 0, 0, 0)))
        args.append(bt)
    return pl.pallas_call(
        body, name="attn_%s_dkv" % band.name, grid=(ncb, nb), in_specs=in_specs, out_specs=[ks, ks],
        out_shape=[jax.ShapeDtypeStruct((n, w), F32)] * 2,
        compiler_params=_params("parallel", "arbitrary"),
    )(*args)


def _dc_onehot():
    c = np.arange(GRID_W)
    dc = np.clip(c[None, :] - c[:, None] + (C_COLS - 1), 0, 2 * C_COLS - 2).reshape(-1)
    m = np.zeros((GRID_W * GRID_W, 128), np.float32)
    m[np.arange(dc.size), dc] = 1.0
    return m


def _bias_tiles(rpb):
    h, nr, ncol = rpb.shape
    flat = jnp.pad(rpb.reshape(h * nr, ncol), ((0, (-h * nr) % 8), (0, 128 - ncol)))
    tiles = _mm(flat, jnp.asarray(_dc_onehot().T), name="rpb_tiles", exact=True, tn=GRID_W * GRID_W)
    tiles = tiles[:h * nr].reshape(h, nr, GRID_W, GRID_W)
    tiles = jnp.pad(tiles, ((0, 0), (2, BT_TILES + 1 - nr - 2), (0, 0), (0, 0)))
    return jnp.concatenate([tiles[:, :BT_TILES], tiles[:, 1:BT_TILES + 1]], -1)


def _bias_tiles_grad(dbt, nr, ncol):
    h = dbt.shape[0]
    d = dbt[:, 2:2 + nr, :, :GRID_W] + dbt[:, 1:1 + nr, :, GRID_W:]
    flat = jnp.pad(d.reshape(h * nr, GRID_W * GRID_W), ((0, (-h * nr) % 8), (0, 0)))
    g = _mm(flat, jnp.asarray(_dc_onehot()), name="rpb_grad", exact=True, tk=GRID_W * GRID_W)
    return g[:h * nr, :ncol].reshape(h, nr, ncol)


TM = 256
TQ_B, TK_B = 512, 512


def _relu2(acc):
    r = jnp.maximum(acc, 0.0)
    return acc, r * r


def _layer_fwd(x, xb, w, sm, tabs, alpha):
    tab_a, tab_q, tab_k = tabs
    s, d = x.shape
    ha = _mm(xb, w["in_a"], name="in_a", tn=QKV_W)
    hb = _mm(xb, w["in_b"], name="in_b", tn=QKV_W)
    hc = _mm(xb, w["in_c"], name="in_c", tn=QKV_W)
    hg = _mm(xb, w["in_g"], name="in_g")

    qa, ka, va = _prep_a(ha, tab_a, TM)
    os_, ms, ls = [], [], []
    for dil in A_DILATIONS:
        shp = (s // dil, dil * A_W)
        o_c, m_c, l_c = _band_fwd(_BandA(s // dil // BAND), qa.reshape(shp), ka.reshape(shp), va.reshape(shp))
        os_.append(o_c.reshape(s, A_W))
        ms.append(m_c.reshape(s, 2 * A_W))
        ls.append(l_c.reshape(s, 2 * A_W))
    oa, lse_a = _combine_a(os_, ms, ls, TM)

    qb, kd, vd = _prep_b(hb, sm["q_norm"], sm["k_norm"], tab_q, tab_k, TM)
    ob, lse_b = _flash_fwd(qb, kd, vd, TQ_B, TK_B)

    qc, kc, vc = _prep_c(hc, TM)
    bt = _bias_tiles(sm["rpb"])
    oc, lse_c = _band_fwd(_BandC(s // BAND), qc, kc, vc, bt)

    pa = _mm(oa, w["br_a"], name="br_a")
    pb = _mm(ob, w["br_b"], name="br_b")
    pc = _mm(oc, w["br_c"], name="br_c")
    merged = _gate_merge(hg, sm["b_gate"], pa, pb, pc, TM)
    mix = _mm(merged, w["out"], name="w_out")
    r1, x1, x1b = _ln_fwd(x, mix, sm["ln1_g"], sm["ln1_b"], alpha, "ln1_fwd", TM)
    u, act = _mm(x1b, w["up"], name="w_up", outs=(F32, _MXU), epilogue=_relu2)
    ff = _mm(act, w["down"], name="w_down")
    r2, x2, x2b = _ln_fwd(x1, ff, sm["ln2_g"], sm["ln2_b"], alpha, "ln2_fwd", TM)
    saved = dict(xb=xb, hb=hb, hg=hg, qa=qa, ka=ka, va=va, oa=oa, lse_a=lse_a, qb=qb, kd=kd, vd=vd, ob=ob, lse_b=lse_b,
                 qc=qc, kc=kc, vc=vc, oc=oc, lse_c=lse_c, bt=bt, pa=pa, pb=pb, pc=pc, merged=merged, r1=r1, x1b=x1b,
                 u=u, act=act, r2=r2)
    return x2, x2b, saved


def _layer_bwd(dx2, w, wt, sm, sv, tabs, alpha):
    tab_a, tab_q, tab_k = tabs
    s, d = dx2.shape
    g = {}
    dr2, dg2, db2 = _ln_bwd(dx2, sv["r2"], sm["ln2_g"], "ln2_bwd", TM)
    g["ln2_g"], g["ln2_b"] = dg2.sum(0), db2.sum(0)
    du = _mm(dr2, wt["down"], name="d_act", outs=(_MXU,), extras=(sv["u"],),
             epilogue=lambda acc, u: (acc * (2.0 * jnp.maximum(u, 0.0)),))
    g["w_down"] = _mm(sv["act"], dr2, mode="tn", name="g_w_down")
    g["w_up"] = _mm(sv["x1b"], du, mode="tn", name="g_w_up")
    dx1 = _mm(du, wt["up"], name="d_x1", extras=(dr2,), epilogue=lambda acc, e: (acc + alpha * e,))
    dr1, dg1, db1 = _ln_bwd(dx1, sv["r1"], sm["ln1_g"], "ln1_bwd", TM)
    g["ln1_g"], g["ln1_b"] = dg1.sum(0), db1.sum(0)
    g["w_out"] = _mm(sv["merged"], dr1, mode="tn", name="g_w_out")
    dmerged = _mm(dr1, wt["out"], name="d_merged")
    dpa, dpb, dpc, dlog, gb = _gate_bwd(dmerged, sv["hg"], sm["b_gate"], sv["pa"], sv["pb"], sv["pc"], TM)
    g["b_gate"] = gb.sum(0)
    g["w_branch_a"] = _mm(sv["oa"], dpa, mode="tn", name="g_br_a")
    g["w_branch_b"] = _mm(sv["ob"], dpb, mode="tn", name="g_br_b")
    g["w_branch_c"] = _mm(sv["oc"], dpc, mode="tn", name="g_br_c")
    doa = _mm(dpa, wt["br_a"], name="d_oa")
    dob = _mm(dpb, wt["br_b"], name="d_ob")
    doc = _mm(dpc, wt["br_c"], name="d_oc")

    dqs, dks, dvs = [], [], []
    for dil in A_DILATIONS:
        band = _BandA(s // dil // BAND)
        args = [t.reshape(s // dil, dil * t.shape[1]) for t in (sv["qa"], sv["ka"], sv["va"], doa, sv["oa"], sv["lse_a"])]
        dqs.append(_band_dq(band, *args)[0].reshape(s, A_W))
        dk_c, dv_c = _band_dkv(band, *args)
        dks.append(dk_c.reshape(s, A_W))
        dvs.append(dv_c.reshape(s, A_W))
    dha = _post_a(dqs, dks, dvs, tab_a, TM)

    bargs = (sv["qb"], sv["kd"], sv["vd"], dob, sv["ob"], sv["lse_b"])
    dqb = _flash_dq(*bargs, TQ_B, TK_B)
    dkd, dvd = _flash_dkv(*bargs, TQ_B, TK_B)
    dhb, gq, gk = _post_b(dqb, dkd, dvd, sv["hb"], sm["q_norm"], sm["k_norm"], tab_q, tab_k, TM)
    g["q_norm_b"] = gq.sum(0).reshape(-1, HEAD).sum(0)
    g["k_norm_b"] = gk.sum(0).reshape(-1, HEAD).sum(0)

    band_c = _BandC(s // BAND)
    cargs = (sv["qc"], sv["kc"], sv["vc"], doc, sv["oc"], sv["lse_c"], sv["bt"])
    dqc, dbt = _band_dq(band_c, *cargs)
    dkc, dvc = _band_dkv(band_c, *cargs)
    dhc = _post_c(dqc, dkc, dvc, TM)
    g["rpb_c"] = _bias_tiles_grad(dbt, 2 * C_ROWS - 1, 2 * C_COLS - 1)

    xb = sv["xb"]
    g["w_in"] = jnp.concatenate([_mm(xb, dh, mode="tn", name="g_in_" + nm)
                                 for nm, dh in (("a", dha), ("b", dhb), ("c", dhc), ("g", dlog))], 1)
    dx = _mm(dha, wt["in_a"], name="d_x_a", extras=(dr1,), epilogue=lambda acc, e: (acc + alpha * e,))
    for nm, dh in (("b", dhb), ("c", dhc), ("g", dlog)):
        dx = _mm(dh, wt["in_" + nm], name="d_x_" + nm, extras=(dx,), epilogue=lambda acc, e: (acc + e,))
    return dx, g


BIG = ("w_in", "w_branch_a", "w_branch_b", "w_branch_c", "w_out", "w_up", "w_down")
ROW_SHARDED = ("w_out", "w_down")
SMALL = ("b_gate", "q_norm_b", "k_norm_b", "rpb_c", "ln1_g", "ln1_b", "ln2_g", "ln2_b")


def _local_step(x, target, wfull, small):
    s, d = x.shape
    depth = wfull["w_in"].shape[0]
    alpha = (2 * depth) ** 0.25
    tabs = _tables(s)
    ws, wts, sms = [], [], []
    for l in range(depth):
        wi = wfull["w_in"][l]
        w = dict(in_a=wi[:, :QKV_W], in_b=wi[:, QKV_W:2 * QKV_W], in_c=wi[:, 2 * QKV_W:3 * QKV_W], in_g=wi[:, 3 * QKV_W:],
                 br_a=wfull["w_branch_a"][l], br_b=wfull["w_branch_b"][l], br_c=wfull["w_branch_c"][l],
                 out=wfull["w_out"][l], up=wfull["w_up"][l], down=wfull["w_down"][l])
        ws.append(w)
        wts.append({k: v.T for k, v in w.items()})
        sms.append(dict(b_gate=small["b_gate"][l][None], q_norm=jnp.tile(small["q_norm_b"][l], BQ_W // HEAD)[None],
                        k_norm=jnp.tile(small["k_norm_b"][l], BKV_W // HEAD)[None], rpb=small["rpb_c"][l],
                        ln1_g=small["ln1_g"][l][None], ln1_b=small["ln1_b"][l][None],
                        ln2_g=small["ln2_g"][l][None], ln2_b=small["ln2_b"][l][None]))
    saved = []
    h, hb = x, x.astype(_MXU)
    for l in range(depth):
        h, hb, sv = _layer_fwd(h, hb, ws[l], sms[l], tabs, alpha)
        saved.append(sv)
    sq, dy = _loss_head(h, target, TM)
    grads = [None] * depth
    for l in reversed(range(depth)):
        dy, grads[l] = _layer_bwd(dy, ws[l], wts[l], sms[l], saved[l], tabs, alpha)
    stacked = {k: jnp.stack([gl[k] for gl in grads]) for k in grads[0]}
    return sq, dy, stacked


def _place():
    return lax.axis_index("x"), lax.axis_index("y"), lax.axis_index("c")


def _other_chips(x, y):
    return [(1 - x, y), (x, 1 - y), (1 - x, 1 - y)]


def _gather_shards(flat):
    r = flat.shape[0]
    rh = r // 2

    def body(src, out, send_sems, recv_sems, local_sem):
        x, y, c = _place()
        sibling = (x, y, 1 - c)
        chips = _other_chips(x, y)

        def half(chip, hc):
            return out.at[2 * chip[0] + chip[1], pl.ds(hc * rh, rh), :]

        def copy(k, src_ref, dst_ref, to):
            return pltpu.make_async_remote_copy(src_ref=src_ref, dst_ref=dst_ref, send_sem=send_sems.at[k],
                                                recv_sem=recv_sems.at[k], device_id=to, device_id_type=MESH)

        mine = pltpu.make_async_copy(src, out.at[2 * x + y], local_sem)
        mine.start()
        first = [copy(k, src.at[pl.ds(c * rh, rh), :], half((x, y), c), (*chip, c)) for k, chip in enumerate(chips)]
        for cp in first:
            cp.start()
        passed = [copy(3 + k, half(chip, c), half(chip, c), sibling) for k, chip in enumerate(chips)]
        for k, chip in enumerate(chips):
            copy(k, half(chip, c), half(chip, c), sibling).wait_recv()
            passed[k].start()
        for k, chip in enumerate(chips):
            copy(3 + k, half(chip, 1 - c), half(chip, 1 - c), sibling).wait_recv()
        for cp in first + passed:
            cp.wait_send()
        mine.wait()

    return pl.pallas_call(
        body, name="gather_weights", in_specs=[ANY], out_specs=ANY,
        out_shape=jax.ShapeDtypeStruct((4, r, 128), flat.dtype),
        scratch_shapes=[pltpu.SemaphoreType.DMA((6,)), pltpu.SemaphoreType.DMA((6,)), pltpu.SemaphoreType.DMA],
    )(flat)


def _pair_exchange(part):
    _, r, _ = part.shape
    rh = r // 2

    def body(src, out, send_sem, recv_sem):
        x, y, c = _place()
        cp = pltpu.make_async_remote_copy(src_ref=src.at[:, pl.ds((1 - c) * rh, rh), :], dst_ref=out, send_sem=send_sem,
                                          recv_sem=recv_sem, device_id=(x, y, 1 - c), device_id_type=MESH)
        cp.start()
        cp.wait()

    return pl.pallas_call(
        body, name="grad_pair_exchange", in_specs=[ANY], out_specs=ANY,
        out_shape=jax.ShapeDtypeStruct((4, rh, 128), part.dtype),
        scratch_shapes=[pltpu.SemaphoreType.DMA, pltpu.SemaphoreType.DMA],
    )(part)


def _chip_exchange(t):
    _, rh, _ = t.shape

    def body(src, out, send_sems, recv_sems):
        x, y, c = _place()
        cps = [pltpu.make_async_remote_copy(src_ref=src.at[2 * chip[0] + chip[1]], dst_ref=out.at[k], send_sem=send_sems.at[k],
                                            recv_sem=recv_sems.at[k], device_id=(*chip, c), device_id_type=MESH)
               for k, chip in enumerate(_other_chips(x, y))]
        for cp in cps:
            cp.start()
        for cp in cps:
            cp.wait()

    return pl.pallas_call(
        body, name="grad_chip_exchange", in_specs=[ANY], out_specs=ANY,
        out_shape=jax.ShapeDtypeStruct((3, rh, 128), t.dtype),
        scratch_shapes=[pltpu.SemaphoreType.DMA((3,)), pltpu.SemaphoreType.DMA((3,))],
    )(t)


def _pair_share(half):
    rh = half.shape[0]

    def body(src, out, send_sem, recv_sem, local_sem):
        x, y, c = _place()
        rows = out.at[pl.ds(c * rh, rh), :]
        mine = pltpu.make_async_copy(src, rows, local_sem)
        mine.start()
        cp = pltpu.make_async_remote_copy(src_ref=src, dst_ref=rows, send_sem=send_sem, recv_sem=recv_sem,
                                          device_id=(x, y, 1 - c), device_id_type=MESH)
        cp.start()
        cp.wait_send()
        theirs = out.at[pl.ds((1 - c) * rh, rh), :]
        pltpu.make_async_remote_copy(src_ref=src, dst_ref=theirs, send_sem=send_sem, recv_sem=recv_sem,
                                     device_id=(x, y, 1 - c), device_id_type=MESH).wait_recv()
        mine.wait()

    return pl.pallas_call(
        body, name="grad_pair_share", in_specs=[ANY], out_specs=ANY,
        out_shape=jax.ShapeDtypeStruct((2 * rh, 128), half.dtype),
        scratch_shapes=[pltpu.SemaphoreType.DMA, pltpu.SemaphoreType.DMA, pltpu.SemaphoreType.DMA],
    )(half)


def _gather_all(v):
    r = v.shape[0]

    def body(src, out, send_sems, recv_sems, local_sem):
        x, y, c = _place()
        me = 4 * x + 2 * y + c
        mine = pltpu.make_async_copy(src, out.at[me], local_sem)
        mine.start()
        cps = []
        for k in range(1, 8):
            fx, fy, fc = (k >> 2) & 1, (k >> 1) & 1, k & 1
            peer = (x + fx - 2 * x * fx, y + fy - 2 * y * fy, c + fc - 2 * c * fc)
            cps.append(pltpu.make_async_remote_copy(src_ref=src, dst_ref=out.at[me], send_sem=send_sems.at[k - 1],
                                                    recv_sem=recv_sems.at[k - 1], device_id=peer, device_id_type=MESH))
        for cp in cps:
            cp.start()
        for k in range(1, 8):
            fx, fy, fc = (k >> 2) & 1, (k >> 1) & 1, k & 1
            frm = 4 * (x + fx - 2 * x * fx) + 2 * (y + fy - 2 * y * fy) + (c + fc - 2 * c * fc)
            pltpu.make_async_remote_copy(src_ref=src, dst_ref=out.at[frm], send_sem=send_sems.at[k - 1],
                                         recv_sem=recv_sems.at[k - 1], device_id=(x, y, c), device_id_type=MESH).wait_recv()
        for cp in cps:
            cp.wait_send()
        mine.wait()

    return pl.pallas_call(
        body, name="gather_small_grads", in_specs=[ANY], out_specs=ANY,
        out_shape=jax.ShapeDtypeStruct((8, r, 128), v.dtype),
        scratch_shapes=[pltpu.SemaphoreType.DMA((7,)), pltpu.SemaphoreType.DMA((7,)), pltpu.SemaphoreType.DMA],
    )(v)


def _sum_slots(parts, name):
    n, r, _ = parts.shape
    tr = _tile(r, 1024, 8)

    def body(p_ref, o_ref):
        acc = p_ref[0]
        for j in range(1, n):
            acc = acc + p_ref[j]
        o_ref[...] = acc

    return pl.pallas_call(
        body, name=name, grid=(r // tr,), in_specs=[pl.BlockSpec((n, tr, 128), lambda i: (0, i, 0))],
        out_specs=pl.BlockSpec((tr, 128), lambda i: (i, 0)), out_shape=jax.ShapeDtypeStruct((r, 128), parts.dtype),
        compiler_params=_params("parallel"),
    )(parts)


def _add_sibling_half(part, recv, c):
    _, rh, _ = recv.shape
    tr = _tile(rh, 1024, 8)
    nblk = rh // tr

    def body(c_ref, p_ref, r_ref, o_ref):
        o_ref[...] = p_ref[...] + r_ref[...]

    return pl.pallas_call(
        body, name="grad_pair_sum",
        grid_spec=pltpu.PrefetchScalarGridSpec(
            num_scalar_prefetch=1, grid=(4, nblk),
            in_specs=[pl.BlockSpec((None, tr, 128), lambda j, i, c_ref: (j, c_ref[0] * nblk + i, 0)),
                      pl.BlockSpec((None, tr, 128), lambda j, i, c_ref: (j, i, 0))],
            out_specs=pl.BlockSpec((None, tr, 128), lambda j, i, c_ref: (j, i, 0))),
        out_shape=jax.ShapeDtypeStruct(recv.shape, recv.dtype),
        compiler_params=_params("parallel", "parallel"),
    )(c, part, recv)


def _add_chips(t, recv, me):
    _, rh, _ = t.shape
    tr = _tile(rh, 1024, 8)

    def body(me_ref, t_ref, r_ref, o_ref):
        o_ref[...] = ((t_ref[...] + r_ref[0]) + r_ref[1]) + r_ref[2]

    return pl.pallas_call(
        body, name="grad_chip_sum",
        grid_spec=pltpu.PrefetchScalarGridSpec(
            num_scalar_prefetch=1, grid=(rh // tr,),
            in_specs=[pl.BlockSpec((None, tr, 128), lambda i, me_ref: (me_ref[0], i, 0)),
                      pl.BlockSpec((3, tr, 128), lambda i, me_ref: (0, i, 0))],
            out_specs=pl.BlockSpec((tr, 128), lambda i, me_ref: (i, 0))),
        out_shape=jax.ShapeDtypeStruct((rh, 128), t.dtype),
        compiler_params=_params("parallel"),
    )(me, t, recv)


def _reduce_scatter(part):
    x, y, c = _place()
    t = _add_sibling_half(part, _pair_exchange(part), jnp.reshape(c, (1,)).astype(jnp.int32))
    half = _add_chips(t, _chip_exchange(t), jnp.reshape(2 * x + y, (1,)).astype(jnp.int32))
    return _pair_share(half)


def _to_rows(parts, mult):
    flat = jnp.concatenate([p.reshape(-1) for p in parts])
    flat = jnp.pad(flat, (0, (-flat.size) % (128 * mult)))
    return flat.reshape(-1, 128)


def _from_rows(rows, shapes):
    flat, out, at = rows.reshape(-1), [], 0
    for shp in shapes:
        n = int(np.prod(shp))
        out.append(flat[at:at + n].reshape(shp))
        at += n
    return out


def _full_from_shards(g, name, shard_shape):
    depth = shard_shape[0]
    if name in ROW_SHARDED:
        return jnp.moveaxis(g, 0, 1).reshape(depth, 4 * shard_shape[1], shard_shape[2])
    return jnp.moveaxis(g, 0, 2).reshape(depth, shard_shape[1], 4 * shard_shape[2])


def _shards_from_full(full, name):
    depth, rows, cols = full.shape
    if name in ROW_SHARDED:
        return jnp.moveaxis(full.reshape(depth, 4, rows // 4, cols), 1, 0)
    return jnp.moveaxis(full.reshape(depth, rows, 4, cols // 4), 2, 0)


def kernel(x, w_in, b_gate, q_norm_b, k_norm_b, rpb_c, w_branch_a, w_branch_b, w_branch_c, w_out, ln1_g, ln1_b, w_up, w_down, ln2_g, ln2_b, loss_target, m_w_in, m_b_gate, m_q_norm_b, m_k_norm_b, m_rpb_c, m_w_branch_a, m_w_branch_b, m_w_branch_c, m_w_out, m_ln1_g, m_ln1_b, m_w_up, m_w_down, m_ln2_g, m_ln2_b, v_w_in, v_b_gate, v_q_norm_b, v_k_norm_b, v_rpb_c, v_w_branch_a, v_w_branch_b, v_w_branch_c, v_w_out, v_ln1_g, v_ln1_b, v_w_up, v_w_down, v_ln2_g, v_ln2_b):
    args = dict(locals())
    big_shard = {n: args[n] for n in BIG}
    small = {n: args[n] for n in SMALL}
    shapes = [big_shard[n].shape for n in BIG]

    flat = _to_rows([big_shard[n].astype(_MXU) for n in BIG], 32)
    gathered = _gather_shards(flat)
    per_chip = [_from_rows(gathered[j], shapes) for j in range(4)]
    wfull = {n: _full_from_shards(jnp.stack([per_chip[j][i] for j in range(4)]), n, shapes[i]) for i, n in enumerate(BIG)}

    sq, grad_x, grads = _local_step(x[0], loss_target[0], wfull, small)
    loss = lax.psum(0.5 * jnp.sum(sq) / x.shape[-1], ("x", "y", "c"))

    part = jnp.stack([_to_rows([_shards_from_full(grads[n], n)[j] for n in BIG], 16) for j in range(4)])
    g_big = _from_rows(_reduce_scatter(part), shapes)
    small_shapes = [small[n].shape for n in SMALL]
    g_small = _from_rows(_sum_slots(_gather_all(_to_rows([grads[n] for n in SMALL], 8)), "small_grad_sum"), small_shapes)
    grad = dict(zip(BIG, g_big))
    grad.update(zip(SMALL, g_small))

    delta, new_m, new_v = {}, {}, {}
    for n in BIG:
        shp = big_shard[n].shape
        two_d = lambda t: t.reshape(-1, shp[-1])
        res = _adamw(two_d(big_shard[n]), two_d(grad[n]), two_d(args["m_" + n]), two_d(args["v_" + n]), "adamw_" + n)
        delta[n], new_m[n], new_v[n] = [t.reshape(shp) for t in res]
    packed = [_to_rows([args[pre + n] for n in SMALL], 8) for pre in ("", "m_", "v_")]
    res = _adamw(packed[0], _to_rows([grad[n] for n in SMALL], 8), packed[1], packed[2], "adamw_small")
    for dst, rows in zip((delta, new_m, new_v), res):
        dst.update(zip(SMALL, _from_rows(rows, small_shapes)))

    order = ("w_in", "b_gate", "q_norm_b", "k_norm_b", "rpb_c", "w_branch_a", "w_branch_b", "w_branch_c", "w_out",
             "ln1_g", "ln1_b", "w_up", "w_down", "ln2_g", "ln2_b")
    return (loss, grad_x[None], *[grad[n] for n in order], *[delta[n] for n in order],
            *[new_m[n] for n in order], *[new_v[n] for n in order])
```

```python
import functools

import numpy as np
import jax
import jax.numpy as jnp
from jax import lax
from jax.experimental import pallas as pl
from jax.experimental.pallas import tpu as pltpu

F32 = jnp.float32
_MXU = jnp.bfloat16

HEAD = 64
A_W, BQ_W, BKV_W, C_W = 256, 512, 128, 256
QKV_W = 768
A_DILATIONS = (1, 4, 16)
A_RADIUS = 64
A_ROPE_HALF = 8
AX_ROPE_HALF = 16
ROPE_THETA = 500000.0
AX_THETA = 10000.0
GRID_W = 64
C_ROWS = 8
C_COLS = 16
BAND = 128
BT_TILES = 18
LN_EPS = 1e-5
RMS_EPS = 1e-6
NEG = -1e30
SCALE = HEAD ** -0.5
LOG2E = 1.4426950408889634
LN2 = 0.6931471805599453
ADAM_LR, ADAM_B1, ADAM_B2, ADAM_EPS, ADAM_WD, ADAM_STEP = 0.001, 0.9, 0.999, 1e-08, 0.01, 10
V7X_VMEM_LIMIT = 48 * 1024 * 1024
MESH = pl.DeviceIdType.MESH
ANY = pl.BlockSpec(memory_space=pl.ANY)


def _params(*sem):
    return pltpu.CompilerParams(dimension_semantics=sem or None, vmem_limit_bytes=V7X_VMEM_LIMIT)


def _tile(n, pref, align=128):
    if n <= pref:
        return n
    t = (pref // align) * align
    while t >= align:
        if n % t == 0:
            return t
        t -= align
    return n


def _mm(a, b, *, name, mode="nn", outs=((F32),), epilogue=None, extras=(), tm=512, tn=1024, tk=1024, exact=False):
    if mode == "nn":
        m, k = a.shape
    else:
        k, m = a.shape
    k2, n = b.shape
    assert k == k2, (a.shape, b.shape, mode)
    tm, tn, tk = _tile(m, tm), _tile(n, tn), _tile(k, tk)
    nk = k // tk
    n_ex, n_out = len(extras), len(outs)
    mx = F32 if exact else _MXU
    prec = lax.Precision.HIGHEST if exact else None

    def body(*refs):
        a_ref, b_ref = refs[0], refs[1]
        ex = refs[2:2 + n_ex]
        out_refs = refs[2 + n_ex:2 + n_ex + n_out]
        acc = refs[-1]
        kk = pl.program_id(2)

        @pl.when(kk == 0)
        def _():
            acc[...] = jnp.zeros_like(acc)

        av, bv = a_ref[...].astype(mx), b_ref[...].astype(mx)
        dims = (((1,), (0,)), ((), ())) if mode == "nn" else (((0,), (0,)), ((), ()))
        acc[...] += lax.dot_general(av, bv, dims, preferred_element_type=F32, precision=prec)

        @pl.when(kk == nk - 1)
        def _():
            res = acc[...]
            vals = epilogue(res, *[e[...] for e in ex]) if epilogue is not None else (res,)
            for o, v in zip(out_refs, vals):
                o[...] = v.astype(o.dtype)

    a_spec = pl.BlockSpec((tm, tk), lambda i, j, kk: (i, kk)) if mode == "nn" else pl.BlockSpec((tk, tm), lambda i, j, kk: (kk, i))
    o_spec = pl.BlockSpec((tm, tn), lambda i, j, kk: (i, j))
    res = pl.pallas_call(
        body, name=name, grid=(m // tm, n // tn, nk),
        in_specs=[a_spec, pl.BlockSpec((tk, tn), lambda i, j, kk: (kk, j))] + [o_spec] * n_ex,
        out_specs=[o_spec] * n_out,
        out_shape=[jax.ShapeDtypeStruct((m, n), d) for d in outs],
        scratch_shapes=[pltpu.VMEM((tm, tn), F32)],
        compiler_params=_params("parallel", "parallel", "arbitrary"),
    )(a, b, *extras)
    return res[0] if n_out == 1 else res


def _rows(tm, width, cb=0):
    return pl.BlockSpec((tm, width), lambda t: (t, cb))


def _whole(arr):
    nd = arr.ndim
    return pl.BlockSpec(arr.shape, lambda t: (0,) * nd)


def _rowwise(fn, name, rows, tm, ins, outs):
    n_in = len(ins)

    def body(*refs):
        vals = fn(*[r[...] for r in refs[:n_in]])
        first = pl.program_id(0) == 0
        for (ncols, _, kind), o, v in zip(outs, refs[n_in:], vals):
            if kind == "row":
                o[...] = v.astype(o.dtype)
            else:
                part = v.reshape(tm // 8, 8, ncols).sum(0)

                @pl.when(first)
                def _(o=o, part=part):
                    o[...] = part

                @pl.when(jnp.logical_not(first))
                def _(o=o, part=part):
                    o[...] += part

    out_specs = [_rows(tm, n) if kind == "row" else pl.BlockSpec((8, n), lambda t: (0, 0)) for n, _, kind in outs]
    out_shape = [jax.ShapeDtypeStruct((rows if kind == "row" else 8, n), d) for n, d, kind in outs]
    res = pl.pallas_call(
        body, name=name, grid=(rows // tm,),
        in_specs=[s for _, s in ins], out_specs=out_specs, out_shape=out_shape,
        compiler_params=_params("arbitrary"),
    )(*[a for a, _ in ins])
    return res


def _lane_lo(width=128):
    return (lax.broadcasted_iota(jnp.int32, (1, width), 1) & (HEAD * 2 - 1)) < HEAD


def _group_sum(x):
    w = x.shape[-1]
    lane = lax.broadcasted_iota(jnp.int32, (1, w), 1)
    k = HEAD // 2
    while k >= 1:
        x = x + jnp.where((lane & k) != 0, pltpu.roll(x, k, 1), pltpu.roll(x, w - k, 1))
        k //= 2
    return x


def _rot(x, c, sm, sp, shift):
    w = x.shape[-1]
    return x * c + pltpu.roll(x, w - shift, 1) * sm + pltpu.roll(x, shift, 1) * sp


def _rot_t(dy, c, sm, sp, shift):
    w = dy.shape[-1]
    return dy * c + pltpu.roll(dy * sm, shift, 1) + pltpu.roll(dy * sp, w - shift, 1)


def _rope_tables(pos_parts, half, thetas):
    cs, sms, sps = [], [], []
    for pos, theta in zip(pos_parts, thetas):
        inv = theta ** (-jnp.arange(half, dtype=F32) / half)
        ang = pos.astype(F32)[:, None] * inv[None, :]
        co, si, ze = jnp.cos(ang), jnp.sin(ang), jnp.zeros_like(ang)
        cs += [co, co]
        sms += [-si, ze]
        sps += [ze, si]
    return [jnp.concatenate(t, axis=1) for t in (cs, sms, sps)]


def _tables(s):
    pos = jnp.arange(s)
    ca, sma, spa = _rope_tables([pos], A_ROPE_HALF, [ROPE_THETA])
    pad = HEAD - 2 * A_ROPE_HALF
    ca = jnp.concatenate([ca, jnp.ones((s, pad), F32)], 1)
    sma, spa = [jnp.concatenate([t, jnp.zeros((s, pad), F32)], 1) for t in (sma, spa)]
    tab_a = [jnp.tile(t, (1, A_W // HEAD)) for t in (ca, sma, spa)]
    ax = _rope_tables([pos // GRID_W, pos % GRID_W], AX_ROPE_HALF, [AX_THETA, AX_THETA])
    tab_q = [jnp.tile(t, (1, BQ_W // HEAD)) for t in ax]
    tab_k = [jnp.tile(t, (1, BKV_W // HEAD)) for t in ax]
    return tab_a, tab_q, tab_k


def _prep_a(ha, tab, tm):
    s = ha.shape[0]

    def fn(h, c, sm, sp):
        q, k, v = h[:, :A_W], h[:, A_W:2 * A_W], h[:, 2 * A_W:]
        return _rot(q, c, sm, sp, A_ROPE_HALF) * SCALE, _rot(k, c, sm, sp, A_ROPE_HALF), v

    return _rowwise(fn, "prep_a", s, tm, [(ha, _rows(tm, QKV_W))] + [(t, _rows(tm, A_W)) for t in tab],
                    [(A_W, _MXU, "row")] * 3)


def _rms(x, g):
    ms = _group_sum(x * x) * (1.0 / HEAD)
    return x * lax.rsqrt(ms + RMS_EPS) * g


def _prep_b(hb, gq, gk, tab_q, tab_k, tm):
    s = hb.shape[0]

    def fn(h, gq, gk, cq, smq, spq, ck, smk, spk):
        xq, xk, v = h[:, :BQ_W], h[:, BQ_W:BQ_W + BKV_W], h[:, BQ_W + BKV_W:]
        q = _rot(_rms(xq, gq), cq, smq, spq, AX_ROPE_HALF) * (SCALE * LOG2E)
        k = _rot(_rms(xk, gk), ck, smk, spk, AX_ROPE_HALF)
        lo = _lane_lo()
        kr, vr = pltpu.roll(k, HEAD, 1), pltpu.roll(v, HEAD, 1)
        kd = jnp.concatenate([jnp.where(lo, k, kr), jnp.where(lo, kr, k)], 1)
        vd = jnp.concatenate([jnp.where(lo, v, vr), jnp.where(lo, vr, v)], 1)
        v1 = jnp.concatenate([jnp.where(lo, v, 1.0), jnp.where(lo, vr, 1.0)], 1)
        return q, kd, vd, v1

    ins = [(hb, _rows(tm, QKV_W)), (gq, _whole(gq)), (gk, _whole(gk))]
    ins += [(t, _rows(tm, BQ_W)) for t in tab_q] + [(t, _rows(tm, BKV_W)) for t in tab_k]
    return _rowwise(fn, "prep_b", s, tm, ins, [(BQ_W, _MXU, "row")] + [(2 * BKV_W, _MXU, "row")] * 3)


def _prep_c(hc, tm):
    def fn(h):
        return h[:, :C_W] * SCALE, h[:, C_W:2 * C_W], h[:, 2 * C_W:]

    return _rowwise(fn, "prep_c", hc.shape[0], tm, [(hc, _rows(tm, QKV_W))], [(C_W, _MXU, "row")] * 3)


def _combine_a(os_, ms, ls, tm):
    s = os_[0].shape[0]

    def fn(o1, o2, o3, m1, m2, m3, l1, l2, l3):
        lo = _lane_lo()
        outs, lses = [], []
        for p in range(A_W // 128):
            st = slice(p * 256, (p + 1) * 256)
            mm = [m[:, st] for m in (m1, m2, m3)]
            ll = [l[:, st] for l in (l1, l2, l3)]
            mmax = jnp.maximum(jnp.maximum(mm[0], mm[1]), mm[2])
            ws = [jnp.exp(m - mmax) for m in mm]
            den = ws[0] * ll[0] + ws[1] * ll[1] + ws[2] * ll[2]
            lses.append(mmax + jnp.log(den))
            num = sum(jnp.where(lo, w[:, :128], w[:, 128:]) * o[:, p * 128:(p + 1) * 128] for w, o in zip(ws, (o1, o2, o3)))
            outs.append(num / jnp.where(lo, den[:, :128], den[:, 128:]))
        return jnp.concatenate(outs, 1), jnp.concatenate(lses, 1)

    ins = [(o, _rows(tm, A_W)) for o in os_] + [(m, _rows(tm, 2 * A_W)) for m in ms] + [(l, _rows(tm, 2 * A_W)) for l in ls]
    return _rowwise(fn, "combine_a", s, tm, ins, [(A_W, F32, "row"), (2 * A_W, F32, "row")])


def _gates(hg, bg, d):
    return [jax.nn.sigmoid(hg[:, i * d:(i + 1) * d] + bg[:, i * d:(i + 1) * d]) for i in range(3)]


def _gate_merge(hg, bg, pa, pb, pc, tm):
    s, d = pa.shape

    def fn(hg, bg, pa, pb, pc):
        g = _gates(hg, bg, d)
        return (g[0] * pa + g[1] * pb + g[2] * pc,)

    ins = [(hg, _rows(tm, 3 * d)), (bg, _whole(bg))] + [(p, _rows(tm, d)) for p in (pa, pb, pc)]
    return _rowwise(fn, "gate_merge", s, tm, ins, [(d, _MXU, "row")])[0]


def _gate_bwd(dm, hg, bg, pa, pb, pc, tm):
    s, d = pa.shape

    def fn(dm, hg, bg, pa, pb, pc):
        g = _gates(hg, bg, d)
        dlog = jnp.concatenate([dm * p * gi * (1.0 - gi) for p, gi in zip((pa, pb, pc), g)], 1)
        return dm * g[0], dm * g[1], dm * g[2], dlog, dlog

    ins = [(dm, _rows(tm, d)), (hg, _rows(tm, 3 * d)), (bg, _whole(bg))] + [(p, _rows(tm, d)) for p in (pa, pb, pc)]
    return _rowwise(fn, "gate_bwd", s, tm, ins, [(d, _MXU, "row")] * 3 + [(3 * d, _MXU, "row"), (3 * d, F32, "acc")])


def _ln_stats(r):
    mu = jnp.mean(r, -1, keepdims=True)
    xc = r - mu
    var = jnp.mean(xc * xc, -1, keepdims=True)
    rstd = lax.rsqrt(var + LN_EPS)
    return xc * rstd, rstd


def _ln_fwd(x, br, g, b, alpha, name, tm):
    s, d = x.shape

    def fn(x, br, g, b):
        r = alpha * x + br
        xhat, _ = _ln_stats(r)
        y = xhat * g + b
        return r, y, y

    ins = [(x, _rows(tm, d)), (br, _rows(tm, d)), (g, _whole(g)), (b, _whole(b))]
    return _rowwise(fn, name, s, tm, ins, [(d, F32, "row"), (d, F32, "row"), (d, _MXU, "row")])


def _ln_bwd(dy, r, g, name, tm):
    s, d = r.shape

    def fn(dy, r, g):
        xhat, rstd = _ln_stats(r)
        dxh = dy * g
        dr = rstd * (dxh - jnp.mean(dxh, -1, keepdims=True) - xhat * jnp.mean(dxh * xhat, -1, keepdims=True))
        return dr, dy * xhat, dy

    ins = [(dy, _rows(tm, d)), (r, _rows(tm, d)), (g, _whole(g))]
    return _rowwise(fn, name, s, tm, ins, [(d, F32, "row"), (d, F32, "acc"), (d, F32, "acc")])


def _loss_head(y, target, tm):
    s, d = y.shape

    def fn(y, t):
        diff = y - t
        return diff * diff, diff * (1.0 / d)

    sq, dy = _rowwise(fn, "loss_head", s, tm, [(y, _rows(tm, d)), (target, _rows(tm, d))], [(d, F32, "acc"), (d, F32, "row")])
    return sq, dy


def _post_a(dqs, dks, dvs, tab, tm):
    s = dqs[0].shape[0]

    def fn(q1, q2, q3, k1, k2, k3, v1, v2, v3, c, sm, sp):
        dq = _rot_t((q1 + q2 + q3) * SCALE, c, sm, sp, A_ROPE_HALF)
        dk = _rot_t(k1 + k2 + k3, c, sm, sp, A_ROPE_HALF)
        return (jnp.concatenate([dq, dk, v1 + v2 + v3], 1),)

    ins = [(t, _rows(tm, A_W)) for t in (*dqs, *dks, *dvs, *tab)]
    return _rowwise(fn, "post_a", s, tm, ins, [(QKV_W, _MXU, "row")])[0]


def _post_b(dq, dkd, dvd, hb, gq, gk, tab_q, tab_k, tm):
    s = dq.shape[0]

    def back(dz, x, g, c, sm, sp):
        dy = _rot_t(dz, c, sm, sp, AX_ROPE_HALF)
        rstd = lax.rsqrt(_group_sum(x * x) * (1.0 / HEAD) + RMS_EPS)
        xh = x * rstd
        dxh = dy * g
        return rstd * (dxh - xh * (_group_sum(dxh * xh) * (1.0 / HEAD))), dy * xh

    def fn(dq, dkd, dvd, h, gq, gk, cq, smq, spq, ck, smk, spk):
        lo = _lane_lo()
        dk = jnp.where(lo, dkd[:, :128], dkd[:, 128:])
        dv = jnp.where(lo, dvd[:, :128], dvd[:, 128:])
        dxq, dgq = back(dq * SCALE, h[:, :BQ_W], gq, cq, smq, spq)
        dxk, dgk = back(dk, h[:, BQ_W:BQ_W + BKV_W], gk, ck, smk, spk)
        return jnp.concatenate([dxq, dxk, dv], 1), dgq, dgk

    ins = [(dq, _rows(tm, BQ_W)), (dkd, _rows(tm, 2 * BKV_W)), (dvd, _rows(tm, 2 * BKV_W)), (hb, _rows(tm, QKV_W)),
           (gq, _whole(gq)), (gk, _whole(gk))]
    ins += [(t, _rows(tm, BQ_W)) for t in tab_q] + [(t, _rows(tm, BKV_W)) for t in tab_k]
    return _rowwise(fn, "post_b", s, tm, ins, [(QKV_W, _MXU, "row"), (BQ_W, F32, "acc"), (BKV_W, F32, "acc")])


def _post_c(dq, dk, dv, tm):
    def fn(dq, dk, dv):
        return (jnp.concatenate([dq * SCALE, dk, dv], 1),)

    return _rowwise(fn, "post_c", dq.shape[0], tm, [(t, _rows(tm, C_W)) for t in (dq, dk, dv)], [(QKV_W, _MXU, "row")])[0]


def _adamw(w, g, m, v, name):
    rows, cols = w.shape
    tm = _tile(rows, 256, 8)

    def fn(w, g, m, v):
        m = ADAM_B1 * m + (1.0 - ADAM_B1) * g
        v = ADAM_B2 * v + (1.0 - ADAM_B2) * (g * g)
        m_hat = m / (1.0 - ADAM_B1 ** ADAM_STEP)
        v_hat = v / (1.0 - ADAM_B2 ** ADAM_STEP)
        delta = -ADAM_LR * (m_hat / (jnp.sqrt(v_hat) + ADAM_EPS) + ADAM_WD * w)
        return delta, m, v

    return _rowwise(fn, name, rows, tm, [(t, _rows(tm, cols)) for t in (w, g, m, v)], [(cols, F32, "row")] * 3)


def _dot_t(a, b):
    return lax.dot_general(a, b, (((1,), (1,)), ((), ())), preferred_element_type=F32)


def _tdot(a, b):
    return lax.dot_general(a, b, (((0,), (0,)), ((), ())), preferred_element_type=F32)


def _head_masks():
    lo = _lane_lo()
    return lo, (lo, jnp.logical_not(lo))


def _rep(x, rows):
    return jnp.broadcast_to(x, (rows, 128))


def _row_lo():
    return lax.broadcasted_iota(jnp.int32, (128, 1), 0) < HEAD


def _flash_fwd(q, kd, v1, tq, tk):
    s = q.shape[0]
    tq, tk = _tile(s, tq), _tile(s, tk)
    nk = s // tk
    mx = _MXU

    def body(q_ref, k_ref, v_ref, o_ref, lse_ref, m_ref, acc_ref):
        kk = pl.program_id(2)

        @pl.when(kk == 0)
        def _():
            m_ref[...] = jnp.full_like(m_ref, NEG)
            acc_ref[...] = jnp.zeros_like(acc_ref)

        q2, k2, v2 = q_ref[...], k_ref[...], v_ref[...]
        _, masks = _head_masks()
        for h in range(2):
            st = _dot_t(k2, jnp.where(masks[h], q2, jnp.zeros_like(q2)))
            m_prev = m_ref[h]
            m_new = jnp.maximum(m_prev, jnp.max(st, 0, keepdims=True))
            p = jnp.exp2(st - m_new)
            m_ref[h] = m_new
            acc_ref[h] = acc_ref[h] * jnp.exp2(m_prev - m_new) + _tdot(v2, p.astype(mx))

        @pl.when(kk == nk - 1)
        def _():
            a0, a1 = acc_ref[0], acc_ref[1]
            l0, l1 = a0[HEAD:HEAD + 1], a1[HEAD:HEAD + 1]
            o_ref[...] = jnp.concatenate([a0[:HEAD] / l0, a1[:HEAD] / l1], 0).T
            lse_ref[...] = jnp.concatenate([m_ref[0] + jnp.log2(l0), m_ref[1] + jnp.log2(l1), jnp.zeros((6, tq), F32)], 0)

    return pl.pallas_call(
        body, name="attn_b_fwd", grid=(BQ_W // 128, s // tq, nk),
        in_specs=[pl.BlockSpec((tq, 128), lambda j, i, kk: (i, j)),
                  pl.BlockSpec((tk, 128), lambda j, i, kk: (kk, j // 2)),
                  pl.BlockSpec((tk, 128), lambda j, i, kk: (kk, j // 2))],
        out_specs=[pl.BlockSpec((tq, 128), lambda j, i, kk: (i, j)), pl.BlockSpec((None, 8, tq), lambda j, i, kk: (j, 0, i))],
        out_shape=[jax.ShapeDtypeStruct((s, BQ_W), F32), jax.ShapeDtypeStruct((BQ_W // 128, 8, s), F32)],
        scratch_shapes=[pltpu.VMEM((2, 1, tq), F32), pltpu.VMEM((2, 128, tq), F32)],
        compiler_params=_params("parallel", "parallel", "arbitrary"),
    )(q, kd, v1)


def _delta_b(do, o, tq):
    s = do.shape[0]
    tq = _tile(s, tq)

    def body(do_ref, o_ref, d_ref):
        prod = do_ref[...] * o_ref[...]
        row = lax.broadcasted_iota(jnp.int32, (8, 128), 0)
        lane = lax.broadcasted_iota(jnp.int32, (8, 128), 1)
        sel = jnp.where(((row == 0) & (lane < HEAD)) | ((row == 1) & (lane >= HEAD)), 1.0, 0.0).astype(F32)
        d_ref[...] = lax.dot_general(sel, prod, (((1,), (1,)), ((), ())), preferred_element_type=F32,
                                     precision=lax.Precision.HIGHEST)

    qs = pl.BlockSpec((tq, 128), lambda j, i: (i, j))
    return pl.pallas_call(
        body, name="attn_b_delta", grid=(BQ_W // 128, s // tq), in_specs=[qs, qs],
        out_specs=pl.BlockSpec((None, 8, tq), lambda j, i: (j, 0, i)),
        out_shape=jax.ShapeDtypeStruct((BQ_W // 128, 8, s), F32),
        compiler_params=_params("parallel", "parallel"),
    )(do, o)


def _flash_bwd(q, kd, vd, do, lse, delta, tq, tk):
    s = q.shape[0]
    tq, tk = _tile(s, tq), _tile(s, tk)
    nq, nk = s // tq, s // tk
    group = BQ_W // 128 // 2
    mx = _MXU

    def body(k_ref, v_ref, q_ref, do_ref, lse_ref, dl_ref, dq_hbm, dk_ref, dv_ref, dk_acc, dv_acc, dqt, stage, sem):
        e, kk, jj, i = pl.program_id(0), pl.program_id(1), pl.program_id(2), pl.program_id(3)

        @pl.when((jj == 0) & (i == 0))
        def _():
            dk_acc[...] = jnp.zeros_like(dk_acc)
            dv_acc[...] = jnp.zeros_like(dv_acc)

        @pl.when(kk == 0)
        def _():
            dqt[jj, i] = jnp.zeros((128, tq), F32)

        q2, k2, v2, do2 = q_ref[...], k_ref[...], v_ref[...], do_ref[...].astype(mx)
        lse8, dl8 = lse_ref[...], dl_ref[...]
        _, masks = _head_masks()
        dqs = []
        for h in range(2):
            qh = jnp.where(masks[h], q2, jnp.zeros_like(q2))
            doh = jnp.where(masks[h], do2, jnp.zeros_like(do2))
            p = jnp.exp2(_dot_t(k2, qh) - lse8[h:h + 1])
            ds = p * (_dot_t(v2, doh) - dl8[h:h + 1])
            p, ds = p.astype(mx), ds.astype(mx)
            dv_acc[...] += jnp.dot(p, doh, preferred_element_type=F32)
            dk_acc[...] += jnp.dot(ds, qh, preferred_element_type=F32)
            dqs.append(_tdot(k2, ds))
        dqt[jj, i] += jnp.where(_row_lo(), dqs[0], dqs[1])

        @pl.when(kk == nk - 1)
        def _():
            stage[...] = dqt[jj, i].T
            lane0 = pl.multiple_of((group * e + jj) * 128, 128)
            cp = pltpu.make_async_copy(stage, dq_hbm.at[pl.ds(pl.multiple_of(i * tq, tq), tq), pl.ds(lane0, 128)], sem)
            cp.start()
            cp.wait()

        @pl.when((jj == group - 1) & (i == nq - 1))
        def _():
            dk_ref[...] = (dk_acc[...] + pltpu.roll(dk_acc[...], HEAD, 1)) * LN2
            dv_ref[...] = dv_acc[...] + pltpu.roll(dv_acc[...], HEAD, 1)

    ks = pl.BlockSpec((tk, 128), lambda e, kk, jj, i: (kk, e))
    qs = pl.BlockSpec((tq, 128), lambda e, kk, jj, i: (i, group * e + jj))
    st = pl.BlockSpec((None, 8, tq), lambda e, kk, jj, i: (group * e + jj, 0, i))
    return pl.pallas_call(
        body, name="attn_b_bwd", grid=(BKV_W // HEAD, nk, group, nq),
        in_specs=[ks, ks, qs, qs, st, st], out_specs=[ANY, ks, ks],
        out_shape=[jax.ShapeDtypeStruct((s, BQ_W), F32)] + [jax.ShapeDtypeStruct((s, 2 * BKV_W), F32)] * 2,
        scratch_shapes=[pltpu.VMEM((tk, 128), F32)] * 2 + [pltpu.VMEM((group, nq, 128, tq), F32), pltpu.VMEM((tq, 128), F32),
                                                          pltpu.SemaphoreType.DMA],
        compiler_params=_params("parallel", "arbitrary", "arbitrary", "arbitrary"),
    )(kd, vd, q, do, lse, delta)


def _p_and_ds(q2, k2, v2, do2, o2, lse2, masks, mask=None, bias=None):
    mx = _MXU
    out = []
    for h in range(2):
        qh = jnp.where(masks[h], q2, jnp.zeros_like(q2))
        sc = _dot_t(qh, k2)
        if bias is not None:
            sc = sc + bias[h]
        if mask is not None:
            sc = jnp.where(mask, sc, NEG)
        lse = jnp.max(lse2[:, h * 128:(h + 1) * 128], -1, keepdims=True)
        p = jnp.exp(sc - lse)
        doh = jnp.where(masks[h], do2, jnp.zeros_like(do2))
        delta = jnp.sum(doh * o2, -1, keepdims=True)
        dp = _dot_t(doh.astype(mx), v2)
        ds = p * (dp - delta)
        out.append((qh, p, ds, doh))
    return out


class _BandA:
    hb, has_bias, name = 1, False, "a"

    def __init__(self, nb):
        self.nb = nb

    def mask(self, qidx, kidx):
        n = self.nb * BAND
        return (jnp.abs(qidx - kidx) <= A_RADIUS) & (kidx >= 0) & (kidx < n) & (qidx >= 0) & (qidx < n)


class _BandC:
    hb, has_bias, name = 3, True, "c"

    def __init__(self, nb):
        self.nb = nb
        self.rows = nb * BAND // GRID_W
        per = BAND // GRID_W
        assert self.rows >= C_ROWS and (C_ROWS - 1) // per <= self.hb
        assert (self.rows - 1) // per - (self.rows - C_ROWS) // per <= self.hb

    def mask(self, qidx, kidx):
        n = self.nb * BAND
        sh = GRID_W.bit_length() - 1
        qrow, cq = qidx >> sh, qidx & (GRID_W - 1)
        krow, ck = kidx >> sh, kidx & (GRID_W - 1)
        r0 = jnp.clip(qrow - C_ROWS // 2, 0, self.rows - C_ROWS)
        c0 = jnp.clip(cq - C_COLS // 2, 0, GRID_W - C_COLS)
        ok = (qidx >= 0) & (qidx < n) & (kidx >= 0) & (kidx < n)
        return ok & (krow >= r0) & (krow < r0 + C_ROWS) & (ck >= c0) & (ck < c0 + C_COLS)


def _bias_tile(off, a):
    return (BAND // GRID_W) * off - a + (C_ROWS - 1) + 2


def _band_bias_k(band, bt_ref, h):
    per = BAND // GRID_W
    return jnp.concatenate([jnp.concatenate([bt_ref[h, _bias_tile(off, a)] for off in range(-band.hb, band.hb + 1)], 1)
                            for a in range(per)], 0)


def _band_bias_q(band, bt_ref, h):
    per = BAND // GRID_W
    return jnp.concatenate([bt_ref[h, _bias_tile(-off, a)] for off in range(-band.hb, band.hb + 1) for a in range(per)], 0)


def _band_split(nb, ncb):
    cb = max(c for c in (4, 2, 1) if ncb % c == 0)
    rb = max(r for r in (4, 2, 1) if nb % r == 0 and r * cb <= 8)
    return rb, cb


def _band_specs(band, rb, cb, nb, width):
    def edge(first):
        return pl.BlockSpec((BAND, cb * width), lambda c, i: (jnp.clip(i * rb + first, 0, nb - 1), c))

    main = pl.BlockSpec((rb * BAND, cb * width), lambda c, i: (i, c))
    return [edge(t - band.hb) for t in range(band.hb)] + [main] + [edge(rb + t) for t in range(band.hb)]


def _band_rows(band, refs, rb, r, lanes):
    hb = band.hb
    parts = []
    for b in range(r, r + 2 * hb + 1):
        if b < hb:
            parts.append(refs[b][:, lanes])
        elif b < hb + rb:
            parts.append(refs[hb][(b - hb) * BAND:(b - hb + 1) * BAND, lanes])
        else:
            parts.append(refs[b - rb + 1][:, lanes])
    return jnp.concatenate(parts, 0)


def _band_idx(band, blk, rows_of_blocks, axis):
    shape = (rows_of_blocks * BAND, 1) if axis == 0 else (1, rows_of_blocks * BAND)
    return blk * BAND + lax.broadcasted_iota(jnp.int32, shape, axis)


def _band_fwd(band, q, k, v, bt=None):
    n, w = q.shape
    nb, ncb, nband = n // BAND, w // 128, 2 * band.hb + 1
    rb, cb = _band_split(nb, ncb)
    mx = _MXU
    raw = not band.has_bias

    def body(*refs):
        q_ref, k_refs, v_refs = refs[0], refs[1:1 + nband], refs[1 + nband:1 + 2 * nband]
        rest = refs[1 + 2 * nband:]
        bt_ref = rest[0] if band.has_bias else None
        outs = rest[1:] if band.has_bias else rest
        i = pl.program_id(1)
        lo, masks = _head_masks()
        for r in range(rb):
            blk = i * rb + r
            mask = band.mask(_band_idx(band, blk, 1, 0), _band_idx(band, blk - band.hb, nband, 1))
            for c in range(cb):
                lanes, rows = slice(c * 128, (c + 1) * 128), slice(r * BAND, (r + 1) * BAND)
                q2 = q_ref[rows, lanes]
                kcat, vcat = _band_rows(band, k_refs, rb, r, lanes), _band_rows(band, v_refs, rb, r, lanes)
                os_, ms, ls = [], [], []
                for h in range(2):
                    sc = _dot_t(jnp.where(masks[h], q2, jnp.zeros_like(q2)), kcat)
                    if band.has_bias:
                        sc = sc + _band_bias_k(band, bt_ref, 2 * c + h)
                    sc = jnp.where(mask, sc, NEG)
                    m = jnp.max(sc, -1, keepdims=True)
                    p = jnp.exp(sc - m)
                    ms.append(m)
                    ls.append(jnp.sum(p, -1, keepdims=True))
                    os_.append(jnp.dot(p.astype(mx), vcat, preferred_element_type=F32))
                if raw:
                    o_ref, m_ref, l_ref = outs
                    o_ref[rows, lanes] = jnp.where(lo, os_[0], os_[1])
                    for h in range(2):
                        st_lanes = slice(c * 256 + h * 128, c * 256 + (h + 1) * 128)
                        m_ref[rows, st_lanes] = _rep(ms[h], BAND)
                        l_ref[rows, st_lanes] = _rep(ls[h], BAND)
                else:
                    o_ref, lse_ref = outs
                    o_ref[rows, lanes] = jnp.where(lo, os_[0] / ls[0], os_[1] / ls[1])
                    for h in range(2):
                        lse_ref[rows, c * 256 + h * 128:c * 256 + (h + 1) * 128] = _rep(ms[h] + jnp.log(ls[h]), BAND)

    qs = pl.BlockSpec((rb * BAND, cb * 128), lambda c, i: (i, c))
    ks = _band_specs(band, rb, cb, nb, 128)
    st = pl.BlockSpec((rb * BAND, cb * 256), lambda c, i: (i, c))
    in_specs, args = [qs] + ks + ks, [q] + [k] * nband + [v] * nband
    if band.has_bias:
        in_specs.append(pl.BlockSpec((2 * cb, BT_TILES, GRID_W, 128), lambda c, i: (c, 0, 0, 0)))
        args.append(bt)
    n_stats = 2 if raw else 1
    return pl.pallas_call(
        body, name="attn_%s_fwd" % band.name, grid=(ncb // cb, nb // rb), in_specs=in_specs,
        out_specs=[qs] + [st] * n_stats,
        out_shape=[jax.ShapeDtypeStruct((n, w), F32)] + [jax.ShapeDtypeStruct((n, 2 * w), F32)] * n_stats,
        compiler_params=_params("parallel", "arbitrary"),
    )(*args)


def _band_dq(band, q, k, v, do, o, lse, bt=None):
    n, w = q.shape
    nb, ncb, nband = n // BAND, w // 128, 2 * band.hb + 1
    rb, cb = _band_split(nb, ncb)
    mx = _MXU
    per = BAND // GRID_W

    def body(*refs):
        q_ref, k_refs, v_refs = refs[0], refs[1:1 + nband], refs[1 + nband:1 + 2 * nband]
        do_ref, o_ref, lse_ref = refs[1 + 2 * nband:4 + 2 * nband]
        rest = refs[4 + 2 * nband:]
        dq_ref = rest[1] if band.has_bias else rest[0]
        i = pl.program_id(1)
        lo, masks = _head_masks()
        if band.has_bias:
            dbt_ref = rest[2]

            @pl.when(i == 0)
            def _():
                dbt_ref[...] = jnp.zeros_like(dbt_ref)

        for r in range(rb):
            blk = i * rb + r
            mask = band.mask(_band_idx(band, blk, 1, 0), _band_idx(band, blk - band.hb, nband, 1))
            for c in range(cb):
                lanes, rows = slice(c * 128, (c + 1) * 128), slice(r * BAND, (r + 1) * BAND)
                kcat, vcat = _band_rows(band, k_refs, rb, r, lanes), _band_rows(band, v_refs, rb, r, lanes)
                bias = [_band_bias_k(band, rest[0], 2 * c + h) for h in range(2)] if band.has_bias else None
                hs = _p_and_ds(q_ref[rows, lanes], kcat, vcat, do_ref[rows, lanes], o_ref[rows, lanes],
                               lse_ref[rows, c * 256:(c + 1) * 256], masks, mask, bias)
                dqs = [jnp.dot(ds.astype(mx), kcat, preferred_element_type=F32) for _, _, ds, _ in hs]
                dq_ref[rows, lanes] = jnp.where(lo, dqs[0], dqs[1])
                if band.has_bias:
                    for h in range(2):
                        ds = hs[h][2]
                        for a in range(per):
                            for t in range(nband):
                                tile = ds[a * GRID_W:(a + 1) * GRID_W, t * 128:(t + 1) * 128]
                                dbt_ref[2 * c + h, _bias_tile(t - band.hb, a)] += tile

    qs = pl.BlockSpec((rb * BAND, cb * 128), lambda c, i: (i, c))
    ks = _band_specs(band, rb, cb, nb, 128)
    st = pl.BlockSpec((rb * BAND, cb * 256), lambda c, i: (i, c))
    in_specs, args = [qs] + ks + ks + [qs, qs, st], [q] + [k] * nband + [v] * nband + [do, o, lse]
    out_specs, out_shape = [qs], [jax.ShapeDtypeStruct((n, w), F32)]
    if band.has_bias:
        bts = pl.BlockSpec((2 * cb, BT_TILES, GRID_W, 128), lambda c, i: (c, 0, 0, 0))
        in_specs.append(bts)
        args.append(bt)
        out_specs.append(bts)
        out_shape.append(jax.ShapeDtypeStruct(bt.shape, F32))
    return pl.pallas_call(
        body, name="attn_%s_dq" % band.name, grid=(ncb // cb, nb // rb), in_specs=in_specs, out_specs=out_specs,
        out_shape=out_shape, compiler_params=_params("parallel", "arbitrary"),
    )(*args)


def _band_dkv(band, q, k, v, do, o, lse, bt=None):
    n, w = q.shape
    nb, ncb, nband = n // BAND, w // 128, 2 * band.hb + 1
    rb, cb = _band_split(nb, ncb)
    mx = _MXU

    def body(*refs):
        k_ref, v_ref = refs[0], refs[1]
        q_refs, do_refs, o_refs, lse_refs = [refs[2 + g * nband:2 + (g + 1) * nband] for g in range(4)]
        rest = refs[2 + 4 * nband:]
        dk_ref, dv_ref = rest[-2], rest[-1]
        i = pl.program_id(1)
        _, masks = _head_masks()
        for r in range(rb):
            blk = i * rb + r
            mask = band.mask(_band_idx(band, blk - band.hb, nband, 0), _band_idx(band, blk, 1, 1))
            for c in range(cb):
                lanes, rows = slice(c * 128, (c + 1) * 128), slice(r * BAND, (r + 1) * BAND)
                qcat, docat, ocat = [_band_rows(band, g, rb, r, lanes) for g in (q_refs, do_refs, o_refs)]
                lsecat = _band_rows(band, lse_refs, rb, r, slice(c * 256, (c + 1) * 256))
                bias = [_band_bias_q(band, rest[0], 2 * c + h) for h in range(2)] if band.has_bias else None
                hs = _p_and_ds(qcat, k_ref[rows, lanes], v_ref[rows, lanes], docat, ocat, lsecat, masks, mask, bias)
                dk_ref[rows, lanes] = sum(_tdot(ds.astype(mx), qh) for qh, _, ds, _ in hs)
                dv_ref[rows, lanes] = sum(_tdot(p.astype(mx), doh.astype(mx)) for _, p, _, doh in hs)

    ks = pl.BlockSpec((rb * BAND, cb * 128), lambda c, i: (i, c))
    in_specs = [ks, ks] + _band_specs(band, rb, cb, nb, 128) * 3 + _band_specs(band, rb, cb, nb, 256)
    args = [k, v] + [q] * nband + [do] * nband + [o] * nband + [lse] * nband
    if band.has_bias:
        in_specs.append(pl.BlockSpec((2 * cb, BT_TILES, GRID_W, 128), lambda c, i: (c, 0, 0, 0)))
        args.append(bt)
    return pl.pallas_call(
        body, name="attn_%s_dkv" % band.name, grid=(ncb // cb, nb // rb), in_specs=in_specs, out_specs=[ks, ks],
        out_shape=[jax.ShapeDtypeStruct((n, w), F32)] * 2,
        compiler_params=_params("parallel", "arbitrary"),
    )(*args)


def _dc_onehot():
    c = np.arange(GRID_W)
    dc = np.clip(c[None, :] - c[:, None] + (C_COLS - 1), 0, 2 * C_COLS - 2).reshape(-1)
    m = np.zeros((GRID_W * GRID_W, 128), np.float32)
    m[np.arange(dc.size), dc] = 1.0
    return m


def _bias_tiles(rpb):
    h, nr, ncol = rpb.shape
    flat = jnp.pad(rpb.reshape(h * nr, ncol), ((0, (-h * nr) % 8), (0, 128 - ncol)))
    tiles = _mm(flat, jnp.asarray(_dc_onehot().T), name="rpb_tiles", exact=True, tn=GRID_W * GRID_W)
    tiles = tiles[:h * nr].reshape(h, nr, GRID_W, GRID_W)
    tiles = jnp.pad(tiles, ((0, 0), (2, BT_TILES + 1 - nr - 2), (0, 0), (0, 0)))
    return jnp.concatenate([tiles[:, :BT_TILES], tiles[:, 1:BT_TILES + 1]], -1)


def _bias_tiles_grad(dbt, nr, ncol):
    h = dbt.shape[0]
    d = dbt[:, 2:2 + nr, :, :GRID_W] + dbt[:, 1:1 + nr, :, GRID_W:]
    flat = jnp.pad(d.reshape(h * nr, GRID_W * GRID_W), ((0, (-h * nr) % 8), (0, 0)))
    g = _mm(flat, jnp.asarray(_dc_onehot()), name="rpb_grad", exact=True, tk=GRID_W * GRID_W)
    return g[:h * nr, :ncol].reshape(h, nr, ncol)


TM = 256
TQ_B, TK_B = 1024, 1024


def _relu2(acc):
    r = jnp.maximum(acc, 0.0)
    return acc, r * r


def _layer_fwd(x, xb, w, sm, tabs, alpha):
    tab_a, tab_q, tab_k = tabs
    s, d = x.shape
    ha = _mm(xb, w["in_a"], name="in_a", tn=QKV_W)
    hb = _mm(xb, w["in_b"], name="in_b", tn=QKV_W)
    hc = _mm(xb, w["in_c"], name="in_c", tn=QKV_W)
    hg = _mm(xb, w["in_g"], name="in_g")

    qa, ka, va = _prep_a(ha, tab_a, TM)
    os_, ms, ls = [], [], []
    for dil in A_DILATIONS:
        shp = (s // dil, dil * A_W)
        o_c, m_c, l_c = _band_fwd(_BandA(s // dil // BAND), qa.reshape(shp), ka.reshape(shp), va.reshape(shp))
        os_.append(o_c.reshape(s, A_W))
        ms.append(m_c.reshape(s, 2 * A_W))
        ls.append(l_c.reshape(s, 2 * A_W))
    oa, lse_a = _combine_a(os_, ms, ls, TM)

    qb, kd, vd, v1 = _prep_b(hb, sm["q_norm"], sm["k_norm"], tab_q, tab_k, TM)
    ob, lse_b = _flash_fwd(qb, kd, v1, TQ_B, TK_B)

    qc, kc, vc = _prep_c(hc, TM)
    bt = _bias_tiles(sm["rpb"])
    oc, lse_c = _band_fwd(_BandC(s // BAND), qc, kc, vc, bt)

    pa = _mm(oa, w["br_a"], name="br_a")
    pb = _mm(ob, w["br_b"], name="br_b")
    pc = _mm(oc, w["br_c"], name="br_c")
    merged = _gate_merge(hg, sm["b_gate"], pa, pb, pc, TM)
    mix = _mm(merged, w["out"], name="w_out")
    r1, x1, x1b = _ln_fwd(x, mix, sm["ln1_g"], sm["ln1_b"], alpha, "ln1_fwd", TM)
    u, act = _mm(x1b, w["up"], name="w_up", outs=(F32, _MXU), epilogue=_relu2)
    ff = _mm(act, w["down"], name="w_down")
    r2, x2, x2b = _ln_fwd(x1, ff, sm["ln2_g"], sm["ln2_b"], alpha, "ln2_fwd", TM)
    saved = dict(xb=xb, hb=hb, hg=hg, qa=qa, ka=ka, va=va, oa=oa, lse_a=lse_a, qb=qb, kd=kd, vd=vd, ob=ob, lse_b=lse_b,
                 qc=qc, kc=kc, vc=vc, oc=oc, lse_c=lse_c, bt=bt, pa=pa, pb=pb, pc=pc, merged=merged, r1=r1, x1b=x1b,
                 u=u, act=act, r2=r2)
    return x2, x2b, saved


def _layer_bwd(dx2, w, wt, sm, sv, tabs, alpha):
    tab_a, tab_q, tab_k = tabs
    s, d = dx2.shape
    g = {}
    dr2, dg2, db2 = _ln_bwd(dx2, sv["r2"], sm["ln2_g"], "ln2_bwd", TM)
    g["ln2_g"], g["ln2_b"] = dg2.sum(0), db2.sum(0)
    du = _mm(dr2, wt["down"], name="d_act", outs=(_MXU,), extras=(sv["u"],),
             epilogue=lambda acc, u: (acc * (2.0 * jnp.maximum(u, 0.0)),))
    g["w_down"] = _mm(sv["act"], dr2, mode="tn", name="g_w_down")
    g["w_up"] = _mm(sv["x1b"], du, mode="tn", name="g_w_up")
    dx1 = _mm(du, wt["up"], name="d_x1", extras=(dr2,), epilogue=lambda acc, e: (acc + alpha * e,))
    dr1, dg1, db1 = _ln_bwd(dx1, sv["r1"], sm["ln1_g"], "ln1_bwd", TM)
    g["ln1_g"], g["ln1_b"] = dg1.sum(0), db1.sum(0)
    g["w_out"] = _mm(sv["merged"], dr1, mode="tn", name="g_w_out")
    dmerged = _mm(dr1, wt["out"], name="d_merged")
    dpa, dpb, dpc, dlog, gb = _gate_bwd(dmerged, sv["hg"], sm["b_gate"], sv["pa"], sv["pb"], sv["pc"], TM)
    g["b_gate"] = gb.sum(0)
    g["w_branch_a"] = _mm(sv["oa"], dpa, mode="tn", name="g_br_a")
    g["w_branch_b"] = _mm(sv["ob"], dpb, mode="tn", name="g_br_b")
    g["w_branch_c"] = _mm(sv["oc"], dpc, mode="tn", name="g_br_c")
    doa = _mm(dpa, wt["br_a"], name="d_oa")
    dob = _mm(dpb, wt["br_b"], name="d_ob")
    doc = _mm(dpc, wt["br_c"], name="d_oc")

    dqs, dks, dvs = [], [], []
    for dil in A_DILATIONS:
        band = _BandA(s // dil // BAND)
        args = [t.reshape(s // dil, dil * t.shape[1]) for t in (sv["qa"], sv["ka"], sv["va"], doa, sv["oa"], sv["lse_a"])]
        dqs.append(_band_dq(band, *args)[0].reshape(s, A_W))
        dk_c, dv_c = _band_dkv(band, *args)
        dks.append(dk_c.reshape(s, A_W))
        dvs.append(dv_c.reshape(s, A_W))
    dha = _post_a(dqs, dks, dvs, tab_a, TM)

    dqb, dkd, dvd = _flash_bwd(sv["qb"], sv["kd"], sv["vd"], dob, sv["lse_b"], _delta_b(dob, sv["ob"], TQ_B), TQ_B, TK_B)
    dhb, gq, gk = _post_b(dqb, dkd, dvd, sv["hb"], sm["q_norm"], sm["k_norm"], tab_q, tab_k, TM)
    g["q_norm_b"] = gq.sum(0).reshape(-1, HEAD).sum(0)
    g["k_norm_b"] = gk.sum(0).reshape(-1, HEAD).sum(0)

    band_c = _BandC(s // BAND)
    cargs = (sv["qc"], sv["kc"], sv["vc"], doc, sv["oc"], sv["lse_c"], sv["bt"])
    dqc, dbt = _band_dq(band_c, *cargs)
    dkc, dvc = _band_dkv(band_c, *cargs)
    dhc = _post_c(dqc, dkc, dvc, TM)
    g["rpb_c"] = _bias_tiles_grad(dbt, 2 * C_ROWS - 1, 2 * C_COLS - 1)

    xb = sv["xb"]
    g["w_in"] = jnp.concatenate([_mm(xb, dh, mode="tn", name="g_in_" + nm)
                                 for nm, dh in (("a", dha), ("b", dhb), ("c", dhc), ("g", dlog))], 1)
    dx = _mm(dha, wt["in_a"], name="d_x_a", extras=(dr1,), epilogue=lambda acc, e: (acc + alpha * e,))
    for nm, dh in (("b", dhb), ("c", dhc), ("g", dlog)):
        dx = _mm(dh, wt["in_" + nm], name="d_x_" + nm, extras=(dx,), epilogue=lambda acc, e: (acc + e,))
    return dx, g


BIG = ("w_in", "w_branch_a", "w_branch_b", "w_branch_c", "w_out", "w_up", "w_down")
ROW_SHARDED = ("w_out", "w_down")
SMALL = ("b_gate", "q_norm_b", "k_norm_b", "rpb_c", "ln1_g", "ln1_b", "ln2_g", "ln2_b")


def _local_step(x, target, wfull, small):
    s, d = x.shape
    depth = wfull["w_in"].shape[0]
    alpha = (2 * depth) ** 0.25
    tabs = _tables(s)
    ws, wts, sms = [], [], []
    for l in range(depth):
        wi = wfull["w_in"][l]
        w = dict(in_a=wi[:, :QKV_W], in_b=wi[:, QKV_W:2 * QKV_W], in_c=wi[:, 2 * QKV_W:3 * QKV_W], in_g=wi[:, 3 * QKV_W:],
                 br_a=wfull["w_branch_a"][l], br_b=wfull["w_branch_b"][l], br_c=wfull["w_branch_c"][l],
                 out=wfull["w_out"][l], up=wfull["w_up"][l], down=wfull["w_down"][l])
        ws.append(w)
        wts.append({k: v.T for k, v in w.items()})
        sms.append(dict(b_gate=small["b_gate"][l][None], q_norm=jnp.tile(small["q_norm_b"][l], BQ_W // HEAD)[None],
                        k_norm=jnp.tile(small["k_norm_b"][l], BKV_W // HEAD)[None], rpb=small["rpb_c"][l],
                        ln1_g=small["ln1_g"][l][None], ln1_b=small["ln1_b"][l][None],
                        ln2_g=small["ln2_g"][l][None], ln2_b=small["ln2_b"][l][None]))
    saved = []
    h, hb = x, x.astype(_MXU)
    for l in range(depth):
        h, hb, sv = _layer_fwd(h, hb, ws[l], sms[l], tabs, alpha)
        saved.append(sv)
    sq, dy = _loss_head(h, target, TM)
    grads = [None] * depth
    for l in reversed(range(depth)):
        dy, grads[l] = _layer_bwd(dy, ws[l], wts[l], sms[l], saved[l], tabs, alpha)
    stacked = {k: jnp.stack([gl[k] for gl in grads]) for k in grads[0]}
    return sq, dy, stacked


def _place():
    return lax.axis_index("x"), lax.axis_index("y"), lax.axis_index("c")


def _other_chips(x, y):
    return [(1 - x, y), (x, 1 - y), (1 - x, 1 - y)]


def _gather_shards(flat):
    r = flat.shape[0]
    rh = r // 2

    def body(src, out, send_sems, recv_sems, local_sem):
        x, y, c = _place()
        sibling = (x, y, 1 - c)
        chips = _other_chips(x, y)

        def half(chip, hc):
            return out.at[2 * chip[0] + chip[1], pl.ds(hc * rh, rh), :]

        def copy(k, src_ref, dst_ref, to):
            return pltpu.make_async_remote_copy(src_ref=src_ref, dst_ref=dst_ref, send_sem=send_sems.at[k],
                                                recv_sem=recv_sems.at[k], device_id=to, device_id_type=MESH)

        mine = pltpu.make_async_copy(src, out.at[2 * x + y], local_sem)
        mine.start()
        first = [copy(k, src.at[pl.ds(c * rh, rh), :], half((x, y), c), (*chip, c)) for k, chip in enumerate(chips)]
        for cp in first:
            cp.start()
        passed = [copy(3 + k, half(chip, c), half(chip, c), sibling) for k, chip in enumerate(chips)]
        for k, chip in enumerate(chips):
            copy(k, half(chip, c), half(chip, c), sibling).wait_recv()
            passed[k].start()
        for k, chip in enumerate(chips):
            copy(3 + k, half(chip, 1 - c), half(chip, 1 - c), sibling).wait_recv()
        for cp in first + passed:
            cp.wait_send()
        mine.wait()

    return pl.pallas_call(
        body, name="gather_weights", in_specs=[ANY], out_specs=ANY,
        out_shape=jax.ShapeDtypeStruct((4, r, 128), flat.dtype),
        scratch_shapes=[pltpu.SemaphoreType.DMA((6,)), pltpu.SemaphoreType.DMA((6,)), pltpu.SemaphoreType.DMA],
    )(flat)


def _pair_exchange(part):
    _, r, _ = part.shape
    rh = r // 2

    def body(src, out, send_sem, recv_sem):
        x, y, c = _place()
        cp = pltpu.make_async_remote_copy(src_ref=src.at[:, pl.ds((1 - c) * rh, rh), :], dst_ref=out, send_sem=send_sem,
                                          recv_sem=recv_sem, device_id=(x, y, 1 - c), device_id_type=MESH)
        cp.start()
        cp.wait()

    return pl.pallas_call(
        body, name="grad_pair_exchange", in_specs=[ANY], out_specs=ANY,
        out_shape=jax.ShapeDtypeStruct((4, rh, 128), part.dtype),
        scratch_shapes=[pltpu.SemaphoreType.DMA, pltpu.SemaphoreType.DMA],
    )(part)


def _chip_exchange(t):
    _, rh, _ = t.shape

    def body(src, out, send_sems, recv_sems):
        x, y, c = _place()
        cps = [pltpu.make_async_remote_copy(src_ref=src.at[2 * chip[0] + chip[1]], dst_ref=out.at[k], send_sem=send_sems.at[k],
                                            recv_sem=recv_sems.at[k], device_id=(*chip, c), device_id_type=MESH)
               for k, chip in enumerate(_other_chips(x, y))]
        for cp in cps:
            cp.start()
        for cp in cps:
            cp.wait()

    return pl.pallas_call(
        body, name="grad_chip_exchange", in_specs=[ANY], out_specs=ANY,
        out_shape=jax.ShapeDtypeStruct((3, rh, 128), t.dtype),
        scratch_shapes=[pltpu.SemaphoreType.DMA((3,)), pltpu.SemaphoreType.DMA((3,))],
    )(t)


def _pair_share(half):
    rh = half.shape[0]

    def body(src, out, send_sem, recv_sem, local_sem):
        x, y, c = _place()
        rows = out.at[pl.ds(c * rh, rh), :]
        mine = pltpu.make_async_copy(src, rows, local_sem)
        mine.start()
        cp = pltpu.make_async_remote_copy(src_ref=src, dst_ref=rows, send_sem=send_sem, recv_sem=recv_sem,
                                          device_id=(x, y, 1 - c), device_id_type=MESH)
        cp.start()
        cp.wait_send()
        theirs = out.at[pl.ds((1 - c) * rh, rh), :]
        pltpu.make_async_remote_copy(src_ref=src, dst_ref=theirs, send_sem=send_sem, recv_sem=recv_sem,
                                     device_id=(x, y, 1 - c), device_id_type=MESH).wait_recv()
        mine.wait()

    return pl.pallas_call(
        body, name="grad_pair_share", in_specs=[ANY], out_specs=ANY,
        out_shape=jax.ShapeDtypeStruct((2 * rh, 128), half.dtype),
        scratch_shapes=[pltpu.SemaphoreType.DMA, pltpu.SemaphoreType.DMA, pltpu.SemaphoreType.DMA],
    )(half)


def _gather_all(v):
    r = v.shape[0]

    def body(src, out, send_sems, recv_sems, local_sem):
        x, y, c = _place()
        me = 4 * x + 2 * y + c
        mine = pltpu.make_async_copy(src, out.at[me], local_sem)
        mine.start()
        cps = []
        for k in range(1, 8):
            fx, fy, fc = (k >> 2) & 1, (k >> 1) & 1, k & 1
            peer = (x + fx - 2 * x * fx, y + fy - 2 * y * fy, c + fc - 2 * c * fc)
            cps.append(pltpu.make_async_remote_copy(src_ref=src, dst_ref=out.at[me], send_sem=send_sems.at[k - 1],
                                                    recv_sem=recv_sems.at[k - 1], device_id=peer, device_id_type=MESH))
        for cp in cps:
            cp.start()
        for k in range(1, 8):
            fx, fy, fc = (k >> 2) & 1, (k >> 1) & 1, k & 1
            frm = 4 * (x + fx - 2 * x * fx) + 2 * (y + fy - 2 * y * fy) + (c + fc - 2 * c * fc)
            pltpu.make_async_remote_copy(src_ref=src, dst_ref=out.at[frm], send_sem=send_sems.at[k - 1],
                                         recv_sem=recv_sems.at[k - 1], device_id=(x, y, c), device_id_type=MESH).wait_recv()
        for cp in cps:
            cp.wait_send()
        mine.wait()

    return pl.pallas_call(
        body, name="gather_small_grads", in_specs=[ANY], out_specs=ANY,
        out_shape=jax.ShapeDtypeStruct((8, r, 128), v.dtype),
        scratch_shapes=[pltpu.SemaphoreType.DMA((7,)), pltpu.SemaphoreType.DMA((7,)), pltpu.SemaphoreType.DMA],
    )(v)


def _sum_slots(parts, name):
    n, r, _ = parts.shape
    tr = _tile(r, 1024, 8)

    def body(p_ref, o_ref):
        acc = p_ref[0]
        for j in range(1, n):
            acc = acc + p_ref[j]
        o_ref[...] = acc

    return pl.pallas_call(
        body, name=name, grid=(r // tr,), in_specs=[pl.BlockSpec((n, tr, 128), lambda i: (0, i, 0))],
        out_specs=pl.BlockSpec((tr, 128), lambda i: (i, 0)), out_shape=jax.ShapeDtypeStruct((r, 128), parts.dtype),
        compiler_params=_params("parallel"),
    )(parts)


def _add_sibling_half(part, recv, c):
    _, rh, _ = recv.shape
    tr = _tile(rh, 1024, 8)
    nblk = rh // tr

    def body(c_ref, p_ref, r_ref, o_ref):
        o_ref[...] = p_ref[...] + r_ref[...]

    return pl.pallas_call(
        body, name="grad_pair_sum",
        grid_spec=pltpu.PrefetchScalarGridSpec(
            num_scalar_prefetch=1, grid=(4, nblk),
            in_specs=[pl.BlockSpec((None, tr, 128), lambda j, i, c_ref: (j, c_ref[0] * nblk + i, 0)),
                      pl.BlockSpec((None, tr, 128), lambda j, i, c_ref: (j, i, 0))],
            out_specs=pl.BlockSpec((None, tr, 128), lambda j, i, c_ref: (j, i, 0))),
        out_shape=jax.ShapeDtypeStruct(recv.shape, recv.dtype),
        compiler_params=_params("parallel", "parallel"),
    )(c, part, recv)


def _add_chips(t, recv, me):
    _, rh, _ = t.shape
    tr = _tile(rh, 1024, 8)

    def body(me_ref, t_ref, r_ref, o_ref):
        o_ref[...] = ((t_ref[...] + r_ref[0]) + r_ref[1]) + r_ref[2]

    return pl.pallas_call(
        body, name="grad_chip_sum",
        grid_spec=pltpu.PrefetchScalarGridSpec(
            num_scalar_prefetch=1, grid=(rh // tr,),
            in_specs=[pl.BlockSpec((None, tr, 128), lambda i, me_ref: (me_ref[0], i, 0)),
                      pl.BlockSpec((3, tr, 128), lambda i, me_ref: (0, i, 0))],
            out_specs=pl.BlockSpec((tr, 128), lambda i, me_ref: (i, 0))),
        out_shape=jax.ShapeDtypeStruct((rh, 128), t.dtype),
        compiler_params=_params("parallel"),
    )(me, t, recv)


def _reduce_scatter(part):
    x, y, c = _place()
    t = _add_sibling_half(part, _pair_exchange(part), jnp.reshape(c, (1,)).astype(jnp.int32))
    half = _add_chips(t, _chip_exchange(t), jnp.reshape(2 * x + y, (1,)).astype(jnp.int32))
    return _pair_share(half)


def _to_rows(parts, mult):
    flat = jnp.concatenate([p.reshape(-1) for p in parts])
    flat = jnp.pad(flat, (0, (-flat.size) % (128 * mult)))
    return flat.reshape(-1, 128)


def _from_rows(rows, shapes):
    flat, out, at = rows.reshape(-1), [], 0
    for shp in shapes:
        n = int(np.prod(shp))
        out.append(flat[at:at + n].reshape(shp))
        at += n
    return out


def _full_from_shards(g, name, shard_shape):
    depth = shard_shape[0]
    if name in ROW_SHARDED:
        return jnp.moveaxis(g, 0, 1).reshape(depth, 4 * shard_shape[1], shard_shape[2])
    return jnp.moveaxis(g, 0, 2).reshape(depth, shard_shape[1], 4 * shard_shape[2])


def _shards_from_full(full, name):
    depth, rows, cols = full.shape
    if name in ROW_SHARDED:
        return jnp.moveaxis(full.reshape(depth, 4, rows // 4, cols), 1, 0)
    return jnp.moveaxis(full.reshape(depth, rows, 4, cols // 4), 2, 0)


def kernel(x, w_in, b_gate, q_norm_b, k_norm_b, rpb_c, w_branch_a, w_branch_b, w_branch_c, w_out, ln1_g, ln1_b, w_up, w_down, ln2_g, ln2_b, loss_target, m_w_in, m_b_gate, m_q_norm_b, m_k_norm_b, m_rpb_c, m_w_branch_a, m_w_branch_b, m_w_branch_c, m_w_out, m_ln1_g, m_ln1_b, m_w_up, m_w_down, m_ln2_g, m_ln2_b, v_w_in, v_b_gate, v_q_norm_b, v_k_norm_b, v_rpb_c, v_w_branch_a, v_w_branch_b, v_w_branch_c, v_w_out, v_ln1_g, v_ln1_b, v_w_up, v_w_down, v_ln2_g, v_ln2_b):
    args = dict(locals())
    big_shard = {n: args[n] for n in BIG}
    small = {n: args[n] for n in SMALL}
    shapes = [big_shard[n].shape for n in BIG]

    flat = _to_rows([big_shard[n].astype(_MXU) for n in BIG], 32)
    gathered = _gather_shards(flat)
    per_chip = [_from_rows(gathered[j], shapes) for j in range(4)]
    wfull = {n: _full_from_shards(jnp.stack([per_chip[j][i] for j in range(4)]), n, shapes[i]) for i, n in enumerate(BIG)}

    sq, grad_x, grads = _local_step(x[0], loss_target[0], wfull, small)
    loss = lax.psum(0.5 * jnp.sum(sq) / x.shape[-1], ("x", "y", "c"))

    part = jnp.stack([_to_rows([_shards_from_full(grads[n], n)[j] for n in BIG], 16) for j in range(4)])
    g_big = _from_rows(_reduce_scatter(part), shapes)
    small_shapes = [small[n].shape for n in SMALL]
    g_small = _from_rows(_sum_slots(_gather_all(_to_rows([grads[n] for n in SMALL], 8)), "small_grad_sum"), small_shapes)
    grad = dict(zip(BIG, g_big))
    grad.update(zip(SMALL, g_small))

    delta, new_m, new_v = {}, {}, {}
    for n in BIG:
        shp = big_shard[n].shape
        two_d = lambda t: t.reshape(-1, shp[-1])
        res = _adamw(two_d(big_shard[n]), two_d(grad[n]), two_d(args["m_" + n]), two_d(args["v_" + n]), "adamw_" + n)
        delta[n], new_m[n], new_v[n] = [t.reshape(shp) for t in res]
    packed = [_to_rows([args[pre + n] for n in SMALL], 8) for pre in ("", "m_", "v_")]
    res = _adamw(packed[0], _to_rows([grad[n] for n in SMALL], 8), packed[1], packed[2], "adamw_small")
    for dst, rows in zip((delta, new_m, new_v), res):
        dst.update(zip(SMALL, _from_rows(rows, small_shapes)))

    order = ("w_in", "b_gate", "q_norm_b", "k_norm_b", "rpb_c", "w_branch_a", "w_branch_b", "w_branch_c", "w_out",
             "ln1_g", "ln1_b", "w_up", "w_down", "ln2_g", "ln2_b")
    return (loss, grad_x[None], *[grad[n] for n in order], *[delta[n] for n in order],
            *[new_m[n] for n in order], *[new_v[n] for n in order])
```

```python
import functools

import numpy as np
import jax
import jax.numpy as jnp
from jax import lax
from jax.experimental import pallas as pl
from jax.experimental.pallas import tpu as pltpu

F32 = jnp.float32
_MXU = jnp.bfloat16

HEAD = 64
A_W, BQ_W, BKV_W, C_W = 256, 512, 128, 256
QKV_W = 768
A_DILATIONS = (1, 4, 16)
A_RADIUS = 64
A_ROPE_HALF = 8
AX_ROPE_HALF = 16
ROPE_THETA = 500000.0
AX_THETA = 10000.0
GRID_W = 64
C_ROWS = 8
C_COLS = 16
BAND = 128
BT_TILES = 18
LN_EPS = 1e-5
RMS_EPS = 1e-6
NEG = -1e30
SCALE = HEAD ** -0.5
LOG2E = 1.4426950408889634
LN2 = 0.6931471805599453
ADAM_LR, ADAM_B1, ADAM_B2, ADAM_EPS, ADAM_WD, ADAM_STEP = 0.001, 0.9, 0.999, 1e-08, 0.01, 10
V7X_VMEM_LIMIT = 48 * 1024 * 1024
MESH = pl.DeviceIdType.MESH
ANY = pl.BlockSpec(memory_space=pl.ANY)


def _params(*sem):
    return pltpu.CompilerParams(dimension_semantics=sem or None, vmem_limit_bytes=V7X_VMEM_LIMIT)


def _tile(n, pref, align=128):
    if n <= pref:
        return n
    t = (pref // align) * align
    while t >= align:
        if n % t == 0:
            return t
        t -= align
    return n


def _mm(a, b, *, name, mode="nn", outs=((F32),), epilogue=None, extras=(), tm=1024, tn=1024, tk=2048, exact=False):
    if mode == "nn":
        m, k = a.shape
    else:
        k, m = a.shape
    k2, n = b.shape
    assert k == k2, (a.shape, b.shape, mode)
    tm, tn, tk = _tile(m, tm), _tile(n, tn), _tile(k, tk)
    nk = k // tk
    n_ex, n_out = len(extras), len(outs)
    mx = F32 if exact else _MXU
    prec = lax.Precision.HIGHEST if exact else None

    def body(*refs):
        a_ref, b_ref = refs[0], refs[1]
        ex = refs[2:2 + n_ex]
        out_refs = refs[2 + n_ex:2 + n_ex + n_out]
        kk = pl.program_id(2)
        av, bv = a_ref[...].astype(mx), b_ref[...].astype(mx)
        dims = (((1,), (0,)), ((), ())) if mode == "nn" else (((0,), (0,)), ((), ()))
        part = lax.dot_general(av, bv, dims, preferred_element_type=F32, precision=prec)

        def finish(res):
            vals = epilogue(res, *[e[...] for e in ex]) if epilogue is not None else (res,)
            for o, v in zip(out_refs, vals):
                o[...] = v.astype(o.dtype)

        if nk == 1:
            finish(part)
        else:
            acc = refs[-1]

            @pl.when(kk == 0)
            def _():
                acc[...] = part

            @pl.when((kk > 0) & (kk < nk - 1))
            def _():
                acc[...] += part

            @pl.when(kk == nk - 1)
            def _():
                finish(acc[...] + part)

    a_spec = pl.BlockSpec((tm, tk), lambda i, j, kk: (i, kk)) if mode == "nn" else pl.BlockSpec((tk, tm), lambda i, j, kk: (kk, i))
    o_spec = pl.BlockSpec((tm, tn), lambda i, j, kk: (i, j))
    res = pl.pallas_call(
        body, name=name, grid=(m // tm, n // tn, nk),
        in_specs=[a_spec, pl.BlockSpec((tk, tn), lambda i, j, kk: (kk, j))] + [o_spec] * n_ex,
        out_specs=[o_spec] * n_out,
        out_shape=[jax.ShapeDtypeStruct((m, n), d) for d in outs],
        scratch_shapes=[pltpu.VMEM((tm, tn), F32)] if nk > 1 else [],
        compiler_params=_params("parallel", "parallel", "arbitrary"),
    )(a, b, *extras)
    return res[0] if n_out == 1 else res


def _rows(tm, width, cb=0):
    return pl.BlockSpec((tm, width), lambda t: (t, cb))


def _whole(arr):
    nd = arr.ndim
    return pl.BlockSpec(arr.shape, lambda t: (0,) * nd)


def _rowwise(fn, name, rows, tm, ins, outs):
    n_in = len(ins)

    def body(*refs):
        vals = fn(*[r[...] for r in refs[:n_in]])
        first = pl.program_id(0) == 0
        for (ncols, _, kind), o, v in zip(outs, refs[n_in:], vals):
            if kind == "row":
                o[...] = v.astype(o.dtype)
            else:
                part = v.reshape(tm // 8, 8, ncols).sum(0)

                @pl.when(first)
                def _(o=o, part=part):
                    o[...] = part

                @pl.when(jnp.logical_not(first))
                def _(o=o, part=part):
                    o[...] += part

    out_specs = [_rows(tm, n) if kind == "row" else pl.BlockSpec((8, n), lambda t: (0, 0)) for n, _, kind in outs]
    out_shape = [jax.ShapeDtypeStruct((rows if kind == "row" else 8, n), d) for n, d, kind in outs]
    res = pl.pallas_call(
        body, name=name, grid=(rows // tm,),
        in_specs=[s for _, s in ins], out_specs=out_specs, out_shape=out_shape,
        compiler_params=_params("arbitrary"),
    )(*[a for a, _ in ins])
    return res


def _lane_lo(width=128):
    return (lax.broadcasted_iota(jnp.int32, (1, width), 1) & (HEAD * 2 - 1)) < HEAD


def _group_sum(x):
    w = x.shape[-1]
    lane = lax.broadcasted_iota(jnp.int32, (1, w), 1)
    k = HEAD // 2
    while k >= 1:
        x = x + jnp.where((lane & k) != 0, pltpu.roll(x, k, 1), pltpu.roll(x, w - k, 1))
        k //= 2
    return x


def _rot(x, c, sm, sp, shift):
    w = x.shape[-1]
    return x * c + pltpu.roll(x, w - shift, 1) * sm + pltpu.roll(x, shift, 1) * sp


def _rot_t(dy, c, sm, sp, shift):
    w = dy.shape[-1]
    return dy * c + pltpu.roll(dy * sm, shift, 1) + pltpu.roll(dy * sp, w - shift, 1)


def _rope_tables(pos_parts, half, thetas):
    cs, sms, sps = [], [], []
    for pos, theta in zip(pos_parts, thetas):
        inv = theta ** (-jnp.arange(half, dtype=F32) / half)
        ang = pos.astype(F32)[:, None] * inv[None, :]
        co, si, ze = jnp.cos(ang), jnp.sin(ang), jnp.zeros_like(ang)
        cs += [co, co]
        sms += [-si, ze]
        sps += [ze, si]
    return [jnp.concatenate(t, axis=1) for t in (cs, sms, sps)]


def _tables(s):
    pos = jnp.arange(s)
    ca, sma, spa = _rope_tables([pos], A_ROPE_HALF, [ROPE_THETA])
    pad = HEAD - 2 * A_ROPE_HALF
    ca = jnp.concatenate([ca, jnp.ones((s, pad), F32)], 1)
    sma, spa = [jnp.concatenate([t, jnp.zeros((s, pad), F32)], 1) for t in (sma, spa)]
    tab_a = [jnp.tile(t, (1, A_W // HEAD)) for t in (ca, sma, spa)]
    ax = _rope_tables([pos // GRID_W, pos % GRID_W], AX_ROPE_HALF, [AX_THETA, AX_THETA])
    tab_q = [jnp.tile(t, (1, BQ_W // HEAD)) for t in ax]
    tab_k = [jnp.tile(t, (1, BKV_W // HEAD)) for t in ax]
    return tab_a, tab_q, tab_k


def _prep_a(ha, tab, tm):
    s = ha.shape[0]

    def fn(h, c, sm, sp):
        q, k, v = h[:, :A_W], h[:, A_W:2 * A_W], h[:, 2 * A_W:]
        return _rot(q, c, sm, sp, A_ROPE_HALF) * SCALE, _rot(k, c, sm, sp, A_ROPE_HALF), v

    return _rowwise(fn, "prep_a", s, tm, [(ha, _rows(tm, QKV_W))] + [(t, _rows(tm, A_W)) for t in tab],
                    [(A_W, _MXU, "row")] * 3)


def _rms(x, g):
    ms = _group_sum(x * x) * (1.0 / HEAD)
    return x * lax.rsqrt(ms + RMS_EPS) * g


def _prep_b(hb, gq, gk, tab_q, tab_k, tm):
    s = hb.shape[0]

    def fn(h, gq, gk, cq, smq, spq, ck, smk, spk):
        xq, xk, v = h[:, :BQ_W], h[:, BQ_W:BQ_W + BKV_W], h[:, BQ_W + BKV_W:]
        q = _rot(_rms(xq, gq), cq, smq, spq, AX_ROPE_HALF) * (SCALE * LOG2E)
        k = _rot(_rms(xk, gk), ck, smk, spk, AX_ROPE_HALF)
        lo = _lane_lo()
        kr, vr = pltpu.roll(k, HEAD, 1), pltpu.roll(v, HEAD, 1)
        kd = jnp.concatenate([jnp.where(lo, k, kr), jnp.where(lo, kr, k)], 1)
        vd = jnp.concatenate([jnp.where(lo, v, vr), jnp.where(lo, vr, v)], 1)
        v1 = jnp.concatenate([jnp.where(lo, v, 1.0), jnp.where(lo, vr, 1.0)], 1)
        return q, kd, vd, v1

    ins = [(hb, _rows(tm, QKV_W)), (gq, _whole(gq)), (gk, _whole(gk))]
    ins += [(t, _rows(tm, BQ_W)) for t in tab_q] + [(t, _rows(tm, BKV_W)) for t in tab_k]
    return _rowwise(fn, "prep_b", s, tm, ins, [(BQ_W, _MXU, "row")] + [(2 * BKV_W, _MXU, "row")] * 3)


def _prep_c(hc, tm):
    def fn(h):
        return h[:, :C_W] * SCALE, h[:, C_W:2 * C_W], h[:, 2 * C_W:]

    return _rowwise(fn, "prep_c", hc.shape[0], tm, [(hc, _rows(tm, QKV_W))], [(C_W, _MXU, "row")] * 3)


def _combine_a(os_, ms, ls, tm):
    s = os_[0].shape[0]

    def fn(o1, o2, o3, m1, m2, m3, l1, l2, l3):
        lo = _lane_lo()
        outs, lses = [], []
        for p in range(A_W // 128):
            st = slice(p * 256, (p + 1) * 256)
            mm = [m[:, st] for m in (m1, m2, m3)]
            ll = [l[:, st] for l in (l1, l2, l3)]
            mmax = jnp.maximum(jnp.maximum(mm[0], mm[1]), mm[2])
            ws = [jnp.exp(m - mmax) for m in mm]
            den = ws[0] * ll[0] + ws[1] * ll[1] + ws[2] * ll[2]
            lses.append(mmax + jnp.log(den))
            num = sum(jnp.where(lo, w[:, :128], w[:, 128:]) * o[:, p * 128:(p + 1) * 128] for w, o in zip(ws, (o1, o2, o3)))
            outs.append(num / jnp.where(lo, den[:, :128], den[:, 128:]))
        return jnp.concatenate(outs, 1), jnp.concatenate(lses, 1)

    ins = [(o, _rows(tm, A_W)) for o in os_] + [(m, _rows(tm, 2 * A_W)) for m in ms] + [(l, _rows(tm, 2 * A_W)) for l in ls]
    return _rowwise(fn, "combine_a", s, tm, ins, [(A_W, F32, "row"), (2 * A_W, F32, "row")])


def _gates(hg, bg, d):
    return [jax.nn.sigmoid(hg[:, i * d:(i + 1) * d] + bg[:, i * d:(i + 1) * d]) for i in range(3)]


def _gate_merge(hg, bg, pa, pb, pc, tm):
    s, d = pa.shape

    def fn(hg, bg, pa, pb, pc):
        g = _gates(hg, bg, d)
        return (g[0] * pa + g[1] * pb + g[2] * pc,)

    ins = [(hg, _rows(tm, 3 * d)), (bg, _whole(bg))] + [(p, _rows(tm, d)) for p in (pa, pb, pc)]
    return _rowwise(fn, "gate_merge", s, tm, ins, [(d, _MXU, "row")])[0]


def _gate_bwd(dm, hg, bg, pa, pb, pc, tm):
    s, d = pa.shape

    def fn(dm, hg, bg, pa, pb, pc):
        g = _gates(hg, bg, d)
        dlog = jnp.concatenate([dm * p * gi * (1.0 - gi) for p, gi in zip((pa, pb, pc), g)], 1)
        return dm * g[0], dm * g[1], dm * g[2], dlog, dlog

    ins = [(dm, _rows(tm, d)), (hg, _rows(tm, 3 * d)), (bg, _whole(bg))] + [(p, _rows(tm, d)) for p in (pa, pb, pc)]
    return _rowwise(fn, "gate_bwd", s, tm, ins, [(d, _MXU, "row")] * 3 + [(3 * d, _MXU, "row"), (3 * d, F32, "acc")])


def _ln_stats(r):
    mu = jnp.mean(r, -1, keepdims=True)
    xc = r - mu
    var = jnp.mean(xc * xc, -1, keepdims=True)
    rstd = lax.rsqrt(var + LN_EPS)
    return xc * rstd, rstd


def _ln_fwd(x, br, g, b, alpha, name, tm):
    s, d = x.shape

    def fn(x, br, g, b):
        r = alpha * x + br
        xhat, _ = _ln_stats(r)
        y = xhat * g + b
        return r, y, y

    ins = [(x, _rows(tm, d)), (br, _rows(tm, d)), (g, _whole(g)), (b, _whole(b))]
    return _rowwise(fn, name, s, tm, ins, [(d, F32, "row"), (d, F32, "row"), (d, _MXU, "row")])


def _ln_bwd(dy, r, g, name, tm):
    s, d = r.shape

    def fn(dy, r, g):
        xhat, rstd = _ln_stats(r)
        dxh = dy * g
        dr = rstd * (dxh - jnp.mean(dxh, -1, keepdims=True) - xhat * jnp.mean(dxh * xhat, -1, keepdims=True))
        return dr, dr, dy * xhat, dy

    ins = [(dy, _rows(tm, d)), (r, _rows(tm, d)), (g, _whole(g))]
    return _rowwise(fn, name, s, tm, ins, [(d, F32, "row"), (d, _MXU, "row"), (d, F32, "acc"), (d, F32, "acc")])


def _loss_head(y, target, tm):
    s, d = y.shape

    def fn(y, t):
        diff = y - t
        return diff * diff, diff * (1.0 / d)

    sq, dy = _rowwise(fn, "loss_head", s, tm, [(y, _rows(tm, d)), (target, _rows(tm, d))], [(d, F32, "acc"), (d, F32, "row")])
    return sq, dy


def _post_a(dqs, dks, dvs, tab, tm):
    s = dqs[0].shape[0]

    def fn(q1, q2, q3, k1, k2, k3, v1, v2, v3, c, sm, sp):
        dq = _rot_t((q1 + q2 + q3) * SCALE, c, sm, sp, A_ROPE_HALF)
        dk = _rot_t(k1 + k2 + k3, c, sm, sp, A_ROPE_HALF)
        return (jnp.concatenate([dq, dk, v1 + v2 + v3], 1),)

    ins = [(t, _rows(tm, A_W)) for t in (*dqs, *dks, *dvs, *tab)]
    return _rowwise(fn, "post_a", s, tm, ins, [(QKV_W, _MXU, "row")])[0]


def _post_b(dq, dkd, dvd, hb, gq, gk, tab_q, tab_k, tm):
    s = dq.shape[0]

    def back(dz, x, g, c, sm, sp):
        dy = _rot_t(dz, c, sm, sp, AX_ROPE_HALF)
        rstd = lax.rsqrt(_group_sum(x * x) * (1.0 / HEAD) + RMS_EPS)
        xh = x * rstd
        dxh = dy * g
        return rstd * (dxh - xh * (_group_sum(dxh * xh) * (1.0 / HEAD))), dy * xh

    def fn(dq, dkd, dvd, h, gq, gk, cq, smq, spq, ck, smk, spk):
        lo = _lane_lo()
        dk = jnp.where(lo, dkd[:, :128], dkd[:, 128:])
        dv = jnp.where(lo, dvd[:, :128], dvd[:, 128:])
        dxq, dgq = back(dq * SCALE, h[:, :BQ_W], gq, cq, smq, spq)
        dxk, dgk = back(dk, h[:, BQ_W:BQ_W + BKV_W], gk, ck, smk, spk)
        return jnp.concatenate([dxq, dxk, dv], 1), dgq, dgk

    ins = [(dq, _rows(tm, BQ_W)), (dkd, _rows(tm, 2 * BKV_W)), (dvd, _rows(tm, 2 * BKV_W)), (hb, _rows(tm, QKV_W)),
           (gq, _whole(gq)), (gk, _whole(gk))]
    ins += [(t, _rows(tm, BQ_W)) for t in tab_q] + [(t, _rows(tm, BKV_W)) for t in tab_k]
    return _rowwise(fn, "post_b", s, tm, ins, [(QKV_W, _MXU, "row"), (BQ_W, F32, "acc"), (BKV_W, F32, "acc")])


def _post_c(dq, dk, dv, tm):
    def fn(dq, dk, dv):
        return (jnp.concatenate([dq * SCALE, dk, dv], 1),)

    return _rowwise(fn, "post_c", dq.shape[0], tm, [(t, _rows(tm, C_W)) for t in (dq, dk, dv)], [(QKV_W, _MXU, "row")])[0]


def _adamw(w, g, m, v, name):
    rows, cols = w.shape
    tm = _tile(rows, 256, 8)

    def fn(w, g, m, v):
        m = ADAM_B1 * m + (1.0 - ADAM_B1) * g
        v = ADAM_B2 * v + (1.0 - ADAM_B2) * (g * g)
        m_hat = m / (1.0 - ADAM_B1 ** ADAM_STEP)
        v_hat = v / (1.0 - ADAM_B2 ** ADAM_STEP)
        delta = -ADAM_LR * (m_hat / (jnp.sqrt(v_hat) + ADAM_EPS) + ADAM_WD * w)
        return delta, m, v

    return _rowwise(fn, name, rows, tm, [(t, _rows(tm, cols)) for t in (w, g, m, v)], [(cols, F32, "row")] * 3)


def _dot_t(a, b):
    return lax.dot_general(a, b, (((1,), (1,)), ((), ())), preferred_element_type=F32)


def _tdot(a, b):
    return lax.dot_general(a, b, (((0,), (0,)), ((), ())), preferred_element_type=F32)


def _head_masks():
    lo = _lane_lo()
    return lo, (lo, jnp.logical_not(lo))


def _rep(x, rows):
    return jnp.broadcast_to(x, (rows, 128))


def _row_lo():
    return lax.broadcasted_iota(jnp.int32, (128, 1), 0) < HEAD


def _flash_fwd(q, kd, v1, tq, tk):
    s = q.shape[0]
    tq, tk = _tile(s, tq), _tile(s, tk)
    nk = s // tk
    mx = _MXU

    def body(q_ref, k_ref, v_ref, o_ref, lse_ref, m_ref, acc_ref):
        kk = pl.program_id(2)

        @pl.when(kk == 0)
        def _():
            m_ref[...] = jnp.full_like(m_ref, NEG)
            acc_ref[...] = jnp.zeros_like(acc_ref)

        q2, k2, v2 = q_ref[...], k_ref[...], v_ref[...]
        _, masks = _head_masks()
        for h in range(2):
            st = _dot_t(k2, jnp.where(masks[h], q2, jnp.zeros_like(q2)))
            m_prev = m_ref[h]
            m_new = jnp.maximum(m_prev, jnp.max(st, 0, keepdims=True))
            p = jnp.exp2(st - m_new)
            m_ref[h] = m_new
            acc_ref[h] = acc_ref[h] * jnp.exp2(m_prev - m_new) + _tdot(v2, p.astype(mx))

        @pl.when(kk == nk - 1)
        def _():
            a0, a1 = acc_ref[0], acc_ref[1]
            l0, l1 = a0[HEAD:HEAD + 1], a1[HEAD:HEAD + 1]
            o_ref[...] = jnp.concatenate([a0[:HEAD] / l0, a1[:HEAD] / l1], 0).T
            lse_ref[...] = jnp.concatenate([m_ref[0] + jnp.log2(l0), m_ref[1] + jnp.log2(l1), jnp.zeros((6, tq), F32)], 0)

    return pl.pallas_call(
        body, name="attn_b_fwd", grid=(BQ_W // 128, s // tq, nk),
        in_specs=[pl.BlockSpec((tq, 128), lambda j, i, kk: (i, j)),
                  pl.BlockSpec((tk, 128), lambda j, i, kk: (kk, j // 2)),
                  pl.BlockSpec((tk, 128), lambda j, i, kk: (kk, j // 2))],
        out_specs=[pl.BlockSpec((tq, 128), lambda j, i, kk: (i, j)), pl.BlockSpec((None, 8, tq), lambda j, i, kk: (j, 0, i))],
        out_shape=[jax.ShapeDtypeStruct((s, BQ_W), F32), jax.ShapeDtypeStruct((BQ_W // 128, 8, s), F32)],
        scratch_shapes=[pltpu.VMEM((2, 1, tq), F32), pltpu.VMEM((2, 128, tq), F32)],
        compiler_params=_params("parallel", "parallel", "arbitrary"),
    )(q, kd, v1)


def _delta_b(do, o, tq):
    s = do.shape[0]
    tq = _tile(s, tq)

    def body(do_ref, o_ref, d_ref):
        prod = do_ref[...] * o_ref[...]
        row = lax.broadcasted_iota(jnp.int32, (8, 128), 0)
        lane = lax.broadcasted_iota(jnp.int32, (8, 128), 1)
        sel = jnp.where(((row == 0) & (lane < HEAD)) | ((row == 1) & (lane >= HEAD)), 1.0, 0.0).astype(F32)
        d_ref[...] = lax.dot_general(sel, prod, (((1,), (1,)), ((), ())), preferred_element_type=F32,
                                     precision=lax.Precision.HIGHEST)

    qs = pl.BlockSpec((tq, 128), lambda j, i: (i, j))
    return pl.pallas_call(
        body, name="attn_b_delta", grid=(BQ_W // 128, s // tq), in_specs=[qs, qs],
        out_specs=pl.BlockSpec((None, 8, tq), lambda j, i: (j, 0, i)),
        out_shape=jax.ShapeDtypeStruct((BQ_W // 128, 8, s), F32),
        compiler_params=_params("parallel", "parallel"),
    )(do, o)


def _flash_bwd(q, kd, vd, do, lse, delta, tq, tk):
    s = q.shape[0]
    tq, tk = _tile(s, tq), _tile(s, tk)
    nq, nk = s // tq, s // tk
    group = BQ_W // 128 // 2
    mx = _MXU

    def body(k_ref, v_ref, q_ref, do_ref, lse_ref, dl_ref, dq_hbm, dk_ref, dv_ref, dk_acc, dv_acc, dqt, stage, sem):
        e, kk, jj, i = pl.program_id(0), pl.program_id(1), pl.program_id(2), pl.program_id(3)

        @pl.when((jj == 0) & (i == 0))
        def _():
            dk_acc[...] = jnp.zeros_like(dk_acc)
            dv_acc[...] = jnp.zeros_like(dv_acc)

        @pl.when(kk == 0)
        def _():
            dqt[jj, i] = jnp.zeros((128, tq), F32)

        q2, k2, v2, do2 = q_ref[...], k_ref[...], v_ref[...], do_ref[...].astype(mx)
        lse8, dl8 = lse_ref[...], dl_ref[...]
        _, masks = _head_masks()
        dqs = []
        for h in range(2):
            qh = jnp.where(masks[h], q2, jnp.zeros_like(q2))
            doh = jnp.where(masks[h], do2, jnp.zeros_like(do2))
            p = jnp.exp2(_dot_t(k2, qh) - lse8[h:h + 1])
            ds = p * (_dot_t(v2, doh) - dl8[h:h + 1])
            p, ds = p.astype(mx), ds.astype(mx)
            dv_acc[...] += jnp.dot(p, doh, preferred_element_type=F32)
            dk_acc[...] += jnp.dot(ds, qh, preferred_element_type=F32)
            dqs.append(_tdot(k2, ds))
        dqt[jj, i] += jnp.where(_row_lo(), dqs[0], dqs[1])

        @pl.when(kk == nk - 1)
        def _():
            stage[...] = dqt[jj, i].T
            lane0 = pl.multiple_of((group * e + jj) * 128, 128)
            cp = pltpu.make_async_copy(stage, dq_hbm.at[pl.ds(pl.multiple_of(i * tq, tq), tq), pl.ds(lane0, 128)], sem)
            cp.start()
            cp.wait()

        @pl.when((jj == group - 1) & (i == nq - 1))
        def _():
            dk_ref[...] = (dk_acc[...] + pltpu.roll(dk_acc[...], HEAD, 1)) * LN2
            dv_ref[...] = dv_acc[...] + pltpu.roll(dv_acc[...], HEAD, 1)

    ks = pl.BlockSpec((tk, 128), lambda e, kk, jj, i: (kk, e))
    qs = pl.BlockSpec((tq, 128), lambda e, kk, jj, i: (i, group * e + jj))
    st = pl.BlockSpec((None, 8, tq), lambda e, kk, jj, i: (group * e + jj, 0, i))
    return pl.pallas_call(
        body, name="attn_b_bwd", grid=(BKV_W // HEAD, nk, group, nq),
        in_specs=[ks, ks, qs, qs, st, st], out_specs=[ANY, ks, ks],
        out_shape=[jax.ShapeDtypeStruct((s, BQ_W), F32)] + [jax.ShapeDtypeStruct((s, 2 * BKV_W), F32)] * 2,
        scratch_shapes=[pltpu.VMEM((tk, 128), F32)] * 2 + [pltpu.VMEM((group, nq, 128, tq), F32), pltpu.VMEM((tq, 128), F32),
                                                          pltpu.SemaphoreType.DMA],
        compiler_params=_params("parallel", "arbitrary", "arbitrary", "arbitrary"),
    )(kd, vd, q, do, lse, delta)


def _p_and_ds(q2, k2, v2, do2, o2, lse2, masks, mask=None, bias=None):
    mx = _MXU
    out = []
    for h in range(2):
        qh = jnp.where(masks[h], q2, jnp.zeros_like(q2))
        sc = _dot_t(qh, k2)
        if bias is not None:
            sc = sc + bias[h]
        if mask is not None:
            sc = jnp.where(mask, sc, NEG)
        lse = jnp.max(lse2[:, h * 128:(h + 1) * 128], -1, keepdims=True)
        p = jnp.exp(sc - lse)
        doh = jnp.where(masks[h], do2, jnp.zeros_like(do2))
        delta = jnp.sum(doh * o2, -1, keepdims=True)
        dp = _dot_t(doh.astype(mx), v2)
        ds = p * (dp - delta)
        out.append((qh, p, ds, doh))
    return out


class _BandA:
    hb, has_bias, name = 1, False, "a"

    def __init__(self, nb):
        self.nb = nb

    def mask(self, qidx, kidx):
        n = self.nb * BAND
        return (jnp.abs(qidx - kidx) <= A_RADIUS) & (kidx >= 0) & (kidx < n) & (qidx >= 0) & (qidx < n)


class _BandC:
    hb, has_bias, name = 3, True, "c"

    def __init__(self, nb):
        self.nb = nb
        self.rows = nb * BAND // GRID_W
        per = BAND // GRID_W
        assert self.rows >= C_ROWS and (C_ROWS - 1) // per <= self.hb
        assert (self.rows - 1) // per - (self.rows - C_ROWS) // per <= self.hb

    def mask(self, qidx, kidx):
        n = self.nb * BAND
        sh = GRID_W.bit_length() - 1
        qrow, cq = qidx >> sh, qidx & (GRID_W - 1)
        krow, ck = kidx >> sh, kidx & (GRID_W - 1)
        r0 = jnp.clip(qrow - C_ROWS // 2, 0, self.rows - C_ROWS)
        c0 = jnp.clip(cq - C_COLS // 2, 0, GRID_W - C_COLS)
        ok = (qidx >= 0) & (qidx < n) & (kidx >= 0) & (kidx < n)
        return ok & (krow >= r0) & (krow < r0 + C_ROWS) & (ck >= c0) & (ck < c0 + C_COLS)


def _bias_tile(off, a):
    return (BAND // GRID_W) * off - a + (C_ROWS - 1) + 2


def _band_bias_k(band, bt_ref, h):
    per = BAND // GRID_W
    return jnp.concatenate([jnp.concatenate([bt_ref[h, _bias_tile(off, a)] for off in range(-band.hb, band.hb + 1)], 1)
                            for a in range(per)], 0)


def _band_bias_q(band, bt_ref, h):
    per = BAND // GRID_W
    return jnp.concatenate([bt_ref[h, _bias_tile(-off, a)] for off in range(-band.hb, band.hb + 1) for a in range(per)], 0)


def _band_split(nb, ncb):
    cb = max(c for c in (4, 2, 1) if ncb % c == 0)
    rb = max(r for r in (4, 2, 1) if nb % r == 0 and r * cb <= 8)
    return rb, cb


def _band_specs(band, rb, cb, nb, width):
    def edge(first):
        return pl.BlockSpec((BAND, cb * width), lambda c, i: (jnp.clip(i * rb + first, 0, nb - 1), c))

    main = pl.BlockSpec((rb * BAND, cb * width), lambda c, i: (i, c))
    return [edge(t - band.hb) for t in range(band.hb)] + [main] + [edge(rb + t) for t in range(band.hb)]


def _band_rows(band, refs, rb, r, lanes):
    hb = band.hb
    parts = []
    for b in range(r, r + 2 * hb + 1):
        if b < hb:
            parts.append(refs[b][:, lanes])
        elif b < hb + rb:
            parts.append(refs[hb][(b - hb) * BAND:(b - hb + 1) * BAND, lanes])
        else:
            parts.append(refs[b - rb + 1][:, lanes])
    return jnp.concatenate(parts, 0)


def _band_idx(band, blk, rows_of_blocks, axis):
    shape = (rows_of_blocks * BAND, 1) if axis == 0 else (1, rows_of_blocks * BAND)
    return blk * BAND + lax.broadcasted_iota(jnp.int32, shape, axis)


def _band_fwd(band, q, k, v, bt=None):
    n, w = q.shape
    nb, ncb, nband = n // BAND, w // 128, 2 * band.hb + 1
    rb, cb = _band_split(nb, ncb)
    mx = _MXU
    raw = not band.has_bias

    def body(*refs):
        q_ref, k_refs, v_refs = refs[0], refs[1:1 + nband], refs[1 + nband:1 + 2 * nband]
        rest = refs[1 + 2 * nband:]
        bt_ref = rest[0] if band.has_bias else None
        outs = rest[1:] if band.has_bias else rest
        i = pl.program_id(1)
        lo, masks = _head_masks()
        for r in range(rb):
            blk = i * rb + r
            mask = band.mask(_band_idx(band, blk, 1, 0), _band_idx(band, blk - band.hb, nband, 1))
            for c in range(cb):
                lanes, rows = slice(c * 128, (c + 1) * 128), slice(r * BAND, (r + 1) * BAND)
                q2 = q_ref[rows, lanes]
                kcat, vcat = _band_rows(band, k_refs, rb, r, lanes), _band_rows(band, v_refs, rb, r, lanes)
                os_, ms, ls = [], [], []
                for h in range(2):
                    sc = _dot_t(jnp.where(masks[h], q2, jnp.zeros_like(q2)), kcat)
                    if band.has_bias:
                        sc = sc + _band_bias_k(band, bt_ref, 2 * c + h)
                    sc = jnp.where(mask, sc, NEG)
                    m = jnp.max(sc, -1, keepdims=True)
                    p = jnp.exp(sc - m)
                    ms.append(m)
                    ls.append(jnp.sum(p, -1, keepdims=True))
                    os_.append(jnp.dot(p.astype(mx), vcat, preferred_element_type=F32))
                if raw:
                    o_ref, m_ref, l_ref = outs
                    o_ref[rows, lanes] = jnp.where(lo, os_[0], os_[1])
                    for h in range(2):
                        st_lanes = slice(c * 256 + h * 128, c * 256 + (h + 1) * 128)
                        m_ref[rows, st_lanes] = _rep(ms[h], BAND)
                        l_ref[rows, st_lanes] = _rep(ls[h], BAND)
                else:
                    o_ref, lse_ref = outs
                    o_ref[rows, lanes] = jnp.where(lo, os_[0] / ls[0], os_[1] / ls[1])
                    for h in range(2):
                        lse_ref[rows, c * 256 + h * 128:c * 256 + (h + 1) * 128] = _rep(ms[h] + jnp.log(ls[h]), BAND)

    qs = pl.BlockSpec((rb * BAND, cb * 128), lambda c, i: (i, c))
    ks = _band_specs(band, rb, cb, nb, 128)
    st = pl.BlockSpec((rb * BAND, cb * 256), lambda c, i: (i, c))
    in_specs, args = [qs] + ks + ks, [q] + [k] * nband + [v] * nband
    if band.has_bias:
        in_specs.append(pl.BlockSpec((2 * cb, BT_TILES, GRID_W, 128), lambda c, i: (c, 0, 0, 0)))
        args.append(bt)
    n_stats = 2 if raw else 1
    return pl.pallas_call(
        body, name="attn_%s_fwd" % band.name, grid=(ncb // cb, nb // rb), in_specs=in_specs,
        out_specs=[qs] + [st] * n_stats,
        out_shape=[jax.ShapeDtypeStruct((n, w), F32)] + [jax.ShapeDtypeStruct((n, 2 * w), F32)] * n_stats,
        compiler_params=_params("parallel", "arbitrary"),
    )(*args)


def _band_dq(band, q, k, v, do, o, lse, bt=None):
    n, w = q.shape
    nb, ncb, nband = n // BAND, w // 128, 2 * band.hb + 1
    rb, cb = _band_split(nb, ncb)
    mx = _MXU
    per = BAND // GRID_W

    def body(*refs):
        q_ref, k_refs, v_refs = refs[0], refs[1:1 + nband], refs[1 + nband:1 + 2 * nband]
        do_ref, o_ref, lse_ref = refs[1 + 2 * nband:4 + 2 * nband]
        rest = refs[4 + 2 * nband:]
        dq_ref = rest[1] if band.has_bias else rest[0]
        i = pl.program_id(1)
        lo, masks = _head_masks()
        if band.has_bias:
            dbt_ref = rest[2]

            @pl.when(i == 0)
            def _():
                dbt_ref[...] = jnp.zeros_like(dbt_ref)

        for r in range(rb):
            blk = i * rb + r
            mask = band.mask(_band_idx(band, blk, 1, 0), _band_idx(band, blk - band.hb, nband, 1))
            for c in range(cb):
                lanes, rows = slice(c * 128, (c + 1) * 128), slice(r * BAND, (r + 1) * BAND)
                kcat, vcat = _band_rows(band, k_refs, rb, r, lanes), _band_rows(band, v_refs, rb, r, lanes)
                bias = [_band_bias_k(band, rest[0], 2 * c + h) for h in range(2)] if band.has_bias else None
                hs = _p_and_ds(q_ref[rows, lanes], kcat, vcat, do_ref[rows, lanes], o_ref[rows, lanes],
                               lse_ref[rows, c * 256:(c + 1) * 256], masks, mask, bias)
                dqs = [jnp.dot(ds.astype(mx), kcat, preferred_element_type=F32) for _, _, ds, _ in hs]
                dq_ref[rows, lanes] = jnp.where(lo, dqs[0], dqs[1])
                if band.has_bias:
                    for h in range(2):
                        ds = hs[h][2]
                        for a in range(per):
                            for t in range(nband):
                                tile = ds[a * GRID_W:(a + 1) * GRID_W, t * 128:(t + 1) * 128]
                                dbt_ref[2 * c + h, _bias_tile(t - band.hb, a)] += tile

    qs = pl.BlockSpec((rb * BAND, cb * 128), lambda c, i: (i, c))
    ks = _band_specs(band, rb, cb, nb, 128)
    st = pl.BlockSpec((rb * BAND, cb * 256), lambda c, i: (i, c))
    in_specs, args = [qs] + ks + ks + [qs, qs, st], [q] + [k] * nband + [v] * nband + [do, o, lse]
    out_specs, out_shape = [qs], [jax.ShapeDtypeStruct((n, w), F32)]
    if band.has_bias:
        bts = pl.BlockSpec((2 * cb, BT_TILES, GRID_W, 128), lambda c, i: (c, 0, 0, 0))
        in_specs.append(bts)
        args.append(bt)
        out_specs.append(bts)
        out_shape.append(jax.ShapeDtypeStruct(bt.shape, F32))
    return pl.pallas_call(
        body, name="attn_%s_dq" % band.name, grid=(ncb // cb, nb // rb), in_specs=in_specs, out_specs=out_specs,
        out_shape=out_shape, compiler_params=_params("parallel", "arbitrary"),
    )(*args)


def _band_dkv(band, q, k, v, do, o, lse, bt=None):
    n, w = q.shape
    nb, ncb, nband = n // BAND, w // 128, 2 * band.hb + 1
    rb, cb = _band_split(nb, ncb)
    mx = _MXU

    def body(*refs):
        k_ref, v_ref = refs[0], refs[1]
        q_refs, do_refs, o_refs, lse_refs = [refs[2 + g * nband:2 + (g + 1) * nband] for g in range(4)]
        rest = refs[2 + 4 * nband:]
        dk_ref, dv_ref = rest[-2], rest[-1]
        i = pl.program_id(1)
        _, masks = _head_masks()
        for r in range(rb):
            blk = i * rb + r
            mask = band.mask(_band_idx(band, blk - band.hb, nband, 0), _band_idx(band, blk, 1, 1))
            for c in range(cb):
                lanes, rows = slice(c * 128, (c + 1) * 128), slice(r * BAND, (r + 1) * BAND)
                qcat, docat, ocat = [_band_rows(band, g, rb, r, lanes) for g in (q_refs, do_refs, o_refs)]
                lsecat = _band_rows(band, lse_refs, rb, r, slice(c * 256, (c + 1) * 256))
                bias = [_band_bias_q(band, rest[0], 2 * c + h) for h in range(2)] if band.has_bias else None
                hs = _p_and_ds(qcat, k_ref[rows, lanes], v_ref[rows, lanes], docat, ocat, lsecat, masks, mask, bias)
                dk_ref[rows, lanes] = sum(_tdot(ds.astype(mx), qh) for qh, _, ds, _ in hs)
                dv_ref[rows, lanes] = sum(_tdot(p.astype(mx), doh.astype(mx)) for _, p, _, doh in hs)

    ks = pl.BlockSpec((rb * BAND, cb * 128), lambda c, i: (i, c))
    in_specs = [ks, ks] + _band_specs(band, rb, cb, nb, 128) * 3 + _band_specs(band, rb, cb, nb, 256)
    args = [k, v] + [q] * nband + [do] * nband + [o] * nband + [lse] * nband
    if band.has_bias:
        in_specs.append(pl.BlockSpec((2 * cb, BT_TILES, GRID_W, 128), lambda c, i: (c, 0, 0, 0)))
        args.append(bt)
    return pl.pallas_call(
        body, name="attn_%s_dkv" % band.name, grid=(ncb // cb, nb // rb), in_specs=in_specs, out_specs=[ks, ks],
        out_shape=[jax.ShapeDtypeStruct((n, w), F32)] * 2,
        compiler_params=_params("parallel", "arbitrary"),
    )(*args)


def _dc_onehot():
    c = np.arange(GRID_W)
    dc = np.clip(c[None, :] - c[:, None] + (C_COLS - 1), 0, 2 * C_COLS - 2).reshape(-1)
    m = np.zeros((GRID_W * GRID_W, 128), np.float32)
    m[np.arange(dc.size), dc] = 1.0
    return m


def _bias_tiles(rpb):
    h, nr, ncol = rpb.shape
    flat = jnp.pad(rpb.reshape(h * nr, ncol), ((0, (-h * nr) % 8), (0, 128 - ncol)))
    tiles = _mm(flat, jnp.asarray(_dc_onehot().T), name="rpb_tiles", exact=True, tn=GRID_W * GRID_W)
    tiles = tiles[:h * nr].reshape(h, nr, GRID_W, GRID_W)
    tiles = jnp.pad(tiles, ((0, 0), (2, BT_TILES + 1 - nr - 2), (0, 0), (0, 0)))
    return jnp.concatenate([tiles[:, :BT_TILES], tiles[:, 1:BT_TILES + 1]], -1)


def _bias_tiles_grad(dbt, nr, ncol):
    h = dbt.shape[0]
    d = dbt[:, 2:2 + nr, :, :GRID_W] + dbt[:, 1:1 + nr, :, GRID_W:]
    flat = jnp.pad(d.reshape(h * nr, GRID_W * GRID_W), ((0, (-h * nr) % 8), (0, 0)))
    g = _mm(flat, jnp.asarray(_dc_onehot()), name="rpb_grad", exact=True, tk=GRID_W * GRID_W)
    return g[:h * nr, :ncol].reshape(h, nr, ncol)


TM = 256
TQ_B, TK_B = 1024, 1024


def _relu2(acc):
    r = jnp.maximum(acc, 0.0)
    return (r * r,)


def _layer_fwd(x, xb, w, sm, tabs, alpha):
    tab_a, tab_q, tab_k = tabs
    s, d = x.shape
    ha = _mm(xb, w["in_a"], name="in_a", tn=QKV_W)
    hb = _mm(xb, w["in_b"], name="in_b", tn=QKV_W)
    hc = _mm(xb, w["in_c"], name="in_c", tn=QKV_W)
    hg = _mm(xb, w["in_g"], name="in_g", outs=(_MXU,))

    qa, ka, va = _prep_a(ha, tab_a, TM)
    os_, ms, ls = [], [], []
    for dil in A_DILATIONS:
        shp = (s // dil, dil * A_W)
        o_c, m_c, l_c = _band_fwd(_BandA(s // dil // BAND), qa.reshape(shp), ka.reshape(shp), va.reshape(shp))
        os_.append(o_c.reshape(s, A_W))
        ms.append(m_c.reshape(s, 2 * A_W))
        ls.append(l_c.reshape(s, 2 * A_W))
    oa, lse_a = _combine_a(os_, ms, ls, TM)

    qb, kd, vd, v1 = _prep_b(hb, sm["q_norm"], sm["k_norm"], tab_q, tab_k, TM)
    ob, lse_b = _flash_fwd(qb, kd, v1, TQ_B, TK_B)

    qc, kc, vc = _prep_c(hc, TM)
    bt = _bias_tiles(sm["rpb"])
    oc, lse_c = _band_fwd(_BandC(s // BAND), qc, kc, vc, bt)

    pa = _mm(oa, w["br_a"], name="br_a", outs=(_MXU,))
    pb = _mm(ob, w["br_b"], name="br_b", outs=(_MXU,))
    pc = _mm(oc, w["br_c"], name="br_c", outs=(_MXU,))
    merged = _gate_merge(hg, sm["b_gate"], pa, pb, pc, TM)
    mix = _mm(merged, w["out"], name="w_out")
    r1, x1, x1b = _ln_fwd(x, mix, sm["ln1_g"], sm["ln1_b"], alpha, "ln1_fwd", TM)
    act = _mm(x1b, w["up"], name="w_up", outs=(_MXU,), epilogue=_relu2)
    ff = _mm(act, w["down"], name="w_down")
    r2, x2, x2b = _ln_fwd(x1, ff, sm["ln2_g"], sm["ln2_b"], alpha, "ln2_fwd", TM)
    saved = dict(xb=xb, hb=hb, hg=hg, qa=qa, ka=ka, va=va, oa=oa, lse_a=lse_a, qb=qb, kd=kd, vd=vd, ob=ob, lse_b=lse_b,
                 qc=qc, kc=kc, vc=vc, oc=oc, lse_c=lse_c, bt=bt, pa=pa, pb=pb, pc=pc, merged=merged, r1=r1, x1b=x1b,
                 act=act, r2=r2)
    return x2, x2b, saved


def _layer_bwd(dx2, w, wt, sm, sv, tabs, alpha):
    tab_a, tab_q, tab_k = tabs
    s, d = dx2.shape
    g = {}
    dr2, dr2b, dg2, db2 = _ln_bwd(dx2, sv["r2"], sm["ln2_g"], "ln2_bwd", TM)
    g["ln2_g"], g["ln2_b"] = dg2.sum(0), db2.sum(0)
    du = _mm(dr2b, wt["down"], name="d_act", outs=(_MXU,), extras=(sv["act"],),
             epilogue=lambda acc, act: (acc * (2.0 * jnp.sqrt(act.astype(F32))),))
    g["w_down"] = _mm(sv["act"], dr2b, mode="tn", name="g_w_down")
    g["w_up"] = _mm(sv["x1b"], du, mode="tn", name="g_w_up")
    dx1 = _mm(du, wt["up"], name="d_x1", extras=(dr2,), epilogue=lambda acc, e: (acc + alpha * e,))
    dr1, dr1b, dg1, db1 = _ln_bwd(dx1, sv["r1"], sm["ln1_g"], "ln1_bwd", TM)
    g["ln1_g"], g["ln1_b"] = dg1.sum(0), db1.sum(0)
    g["w_out"] = _mm(sv["merged"], dr1b, mode="tn", name="g_w_out")
    dmerged = _mm(dr1b, wt["out"], name="d_merged")
    dpa, dpb, dpc, dlog, gb = _gate_bwd(dmerged, sv["hg"], sm["b_gate"], sv["pa"], sv["pb"], sv["pc"], TM)
    g["b_gate"] = gb.sum(0)
    g["w_branch_a"] = _mm(sv["oa"], dpa, mode="tn", name="g_br_a")
    g["w_branch_b"] = _mm(sv["ob"], dpb, mode="tn", name="g_br_b")
    g["w_branch_c"] = _mm(sv["oc"], dpc, mode="tn", name="g_br_c")
    doa = _mm(dpa, wt["br_a"], name="d_oa")
    dob = _mm(dpb, wt["br_b"], name="d_ob")
    doc = _mm(dpc, wt["br_c"], name="d_oc")

    dqs, dks, dvs = [], [], []
    for dil in A_DILATIONS:
        band = _BandA(s // dil // BAND)
        args = [t.reshape(s // dil, dil * t.shape[1]) for t in (sv["qa"], sv["ka"], sv["va"], doa, sv["oa"], sv["lse_a"])]
        dqs.append(_band_dq(band, *args)[0].reshape(s, A_W))
        dk_c, dv_c = _band_dkv(band, *args)
        dks.append(dk_c.reshape(s, A_W))
        dvs.append(dv_c.reshape(s, A_W))
    dha = _post_a(dqs, dks, dvs, tab_a, TM)

    dqb, dkd, dvd = _flash_bwd(sv["qb"], sv["kd"], sv["vd"], dob, sv["lse_b"], _delta_b(dob, sv["ob"], TQ_B), TQ_B, TK_B)
    dhb, gq, gk = _post_b(dqb, dkd, dvd, sv["hb"], sm["q_norm"], sm["k_norm"], tab_q, tab_k, TM)
    g["q_norm_b"] = gq.sum(0).reshape(-1, HEAD).sum(0)
    g["k_norm_b"] = gk.sum(0).reshape(-1, HEAD).sum(0)

    band_c = _BandC(s // BAND)
    cargs = (sv["qc"], sv["kc"], sv["vc"], doc, sv["oc"], sv["lse_c"], sv["bt"])
    dqc, dbt = _band_dq(band_c, *cargs)
    dkc, dvc = _band_dkv(band_c, *cargs)
    dhc = _post_c(dqc, dkc, dvc, TM)
    g["rpb_c"] = _bias_tiles_grad(dbt, 2 * C_ROWS - 1, 2 * C_COLS - 1)

    xb = sv["xb"]
    g["w_in"] = jnp.concatenate([_mm(xb, dh, mode="tn", name="g_in_" + nm)
                                 for nm, dh in (("a", dha), ("b", dhb), ("c", dhc), ("g", dlog))], 1)
    dx = _mm(dha, wt["in_a"], name="d_x_a", extras=(dr1,), epilogue=lambda acc, e: (acc + alpha * e,))
    for nm, dh in (("b", dhb), ("c", dhc), ("g", dlog)):
        dx = _mm(dh, wt["in_" + nm], name="d_x_" + nm, extras=(dx,), epilogue=lambda acc, e: (acc + e,))
    return dx, g


BIG = ("w_in", "w_branch_a", "w_branch_b", "w_branch_c", "w_out", "w_up", "w_down")
ROW_SHARDED = ("w_out", "w_down")
SMALL = ("b_gate", "q_norm_b", "k_norm_b", "rpb_c", "ln1_g", "ln1_b", "ln2_g", "ln2_b")


def _local_step(x, target, wfull, small):
    s, d = x.shape
    depth = wfull["w_in"].shape[0]
    alpha = (2 * depth) ** 0.25
    tabs = _tables(s)
    ws, wts, sms = [], [], []
    for l in range(depth):
        wi = wfull["w_in"][l]
        w = dict(in_a=wi[:, :QKV_W], in_b=wi[:, QKV_W:2 * QKV_W], in_c=wi[:, 2 * QKV_W:3 * QKV_W], in_g=wi[:, 3 * QKV_W:],
                 br_a=wfull["w_branch_a"][l], br_b=wfull["w_branch_b"][l], br_c=wfull["w_branch_c"][l],
                 out=wfull["w_out"][l], up=wfull["w_up"][l], down=wfull["w_down"][l])
        ws.append(w)
        wts.append({k: v.T for k, v in w.items()})
        sms.append(dict(b_gate=small["b_gate"][l][None], q_norm=jnp.tile(small["q_norm_b"][l], BQ_W // HEAD)[None],
                        k_norm=jnp.tile(small["k_norm_b"][l], BKV_W // HEAD)[None], rpb=small["rpb_c"][l],
                        ln1_g=small["ln1_g"][l][None], ln1_b=small["ln1_b"][l][None],
                        ln2_g=small["ln2_g"][l][None], ln2_b=small["ln2_b"][l][None]))
    saved = []
    h, hb = x, x.astype(_MXU)
    for l in range(depth):
        h, hb, sv = _layer_fwd(h, hb, ws[l], sms[l], tabs, alpha)
        saved.append(sv)
    sq, dy = _loss_head(h, target, TM)
    grads = [None] * depth
    for l in reversed(range(depth)):
        dy, grads[l] = _layer_bwd(dy, ws[l], wts[l], sms[l], saved[l], tabs, alpha)
    stacked = {k: jnp.stack([gl[k] for gl in grads]) for k in grads[0]}
    return sq, dy, stacked


def _place():
    return lax.axis_index("x"), lax.axis_index("y"), lax.axis_index("c")


def _other_chips(x, y):
    return [(1 - x, y), (x, 1 - y), (1 - x, 1 - y)]


def _gather_shards(flat):
    r = flat.shape[0]
    rh = r // 2

    def body(src, out, send_sems, recv_sems, local_sem):
        x, y, c = _place()
        sibling = (x, y, 1 - c)
        chips = _other_chips(x, y)

        def half(chip, hc):
            return out.at[2 * chip[0] + chip[1], pl.ds(hc * rh, rh), :]

        def copy(k, src_ref, dst_ref, to):
            return pltpu.make_async_remote_copy(src_ref=src_ref, dst_ref=dst_ref, send_sem=send_sems.at[k],
                                                recv_sem=recv_sems.at[k], device_id=to, device_id_type=MESH)

        mine = pltpu.make_async_copy(src, out.at[2 * x + y], local_sem)
        mine.start()
        first = [copy(k, src.at[pl.ds(c * rh, rh), :], half((x, y), c), (*chip, c)) for k, chip in enumerate(chips)]
        for cp in first:
            cp.start()
        passed = [copy(3 + k, half(chip, c), half(chip, c), sibling) for k, chip in enumerate(chips)]
        for k, chip in enumerate(chips):
            copy(k, half(chip, c), half(chip, c), sibling).wait_recv()
            passed[k].start()
        for k, chip in enumerate(chips):
            copy(3 + k, half(chip, 1 - c), half(chip, 1 - c), sibling).wait_recv()
        for cp in first + passed:
            cp.wait_send()
        mine.wait()

    return pl.pallas_call(
        body, name="gather_weights", in_specs=[ANY], out_specs=ANY,
        out_shape=jax.ShapeDtypeStruct((4, r, 128), flat.dtype),
        scratch_shapes=[pltpu.SemaphoreType.DMA((6,)), pltpu.SemaphoreType.DMA((6,)), pltpu.SemaphoreType.DMA],
    )(flat)


def _pair_exchange(part):
    _, r, _ = part.shape
    rh = r // 2

    def body(src, out, send_sem, recv_sem):
        x, y, c = _place()
        cp = pltpu.make_async_remote_copy(src_ref=src.at[:, pl.ds((1 - c) * rh, rh), :], dst_ref=out, send_sem=send_sem,
                                          recv_sem=recv_sem, device_id=(x, y, 1 - c), device_id_type=MESH)
        cp.start()
        cp.wait()

    return pl.pallas_call(
        body, name="grad_pair_exchange", in_specs=[ANY], out_specs=ANY,
        out_shape=jax.ShapeDtypeStruct((4, rh, 128), part.dtype),
        scratch_shapes=[pltpu.SemaphoreType.DMA, pltpu.SemaphoreType.DMA],
    )(part)


def _chip_exchange(t):
    _, rh, _ = t.shape

    def body(src, out, send_sems, recv_sems):
        x, y, c = _place()
        cps = [pltpu.make_async_remote_copy(src_ref=src.at[2 * chip[0] + chip[1]], dst_ref=out.at[k], send_sem=send_sems.at[k],
                                            recv_sem=recv_sems.at[k], device_id=(*chip, c), device_id_type=MESH)
               for k, chip in enumerate(_other_chips(x, y))]
        for cp in cps:
            cp.start()
        for cp in cps:
            cp.wait()

    return pl.pallas_call(
        body, name="grad_chip_exchange", in_specs=[ANY], out_specs=ANY,
        out_shape=jax.ShapeDtypeStruct((3, rh, 128), t.dtype),
        scratch_shapes=[pltpu.SemaphoreType.DMA((3,)), pltpu.SemaphoreType.DMA((3,))],
    )(t)


def _pair_share(half):
    rh = half.shape[0]

    def body(src, out, send_sem, recv_sem, local_sem):
        x, y, c = _place()
        rows = out.at[pl.ds(c * rh, rh), :]
        mine = pltpu.make_async_copy(src, rows, local_sem)
        mine.start()
        cp = pltpu.make_async_remote_copy(src_ref=src, dst_ref=rows, send_sem=send_sem, recv_sem=recv_sem,
                                          device_id=(x, y, 1 - c), device_id_type=MESH)
        cp.start()
        cp.wait_send()
        theirs = out.at[pl.ds((1 - c) * rh, rh), :]
        pltpu.make_async_remote_copy(src_ref=src, dst_ref=theirs, send_sem=send_sem, recv_sem=recv_sem,
                                     device_id=(x, y, 1 - c), device_id_type=MESH).wait_recv()
        mine.wait()

    return pl.pallas_call(
        body, name="grad_pair_share", in_specs=[ANY], out_specs=ANY,
        out_shape=jax.ShapeDtypeStruct((2 * rh, 128), half.dtype),
        scratch_shapes=[pltpu.SemaphoreType.DMA, pltpu.SemaphoreType.DMA, pltpu.SemaphoreType.DMA],
    )(half)


def _gather_all(v):
    r = v.shape[0]

    def body(src, out, send_sems, recv_sems, local_sem):
        x, y, c = _place()
        me = 4 * x + 2 * y + c
        mine = pltpu.make_async_copy(src, out.at[me], local_sem)
        mine.start()
        cps = []
        for k in range(1, 8):
            fx, fy, fc = (k >> 2) & 1, (k >> 1) & 1, k & 1
            peer = (x + fx - 2 * x * fx, y + fy - 2 * y * fy, c + fc - 2 * c * fc)
            cps.append(pltpu.make_async_remote_copy(src_ref=src, dst_ref=out.at[me], send_sem=send_sems.at[k - 1],
                                                    recv_sem=recv_sems.at[k - 1], device_id=peer, device_id_type=MESH))
        for cp in cps:
            cp.start()
        for k in range(1, 8):
            fx, fy, fc = (k >> 2) & 1, (k >> 1) & 1, k & 1
            frm = 4 * (x + fx - 2 * x * fx) + 2 * (y + fy - 2 * y * fy) + (c + fc - 2 * c * fc)
            pltpu.make_async_remote_copy(src_ref=src, dst_ref=out.at[frm], send_sem=send_sems.at[k - 1],
                                         recv_sem=recv_sems.at[k - 1], device_id=(x, y, c), device_id_type=MESH).wait_recv()
        for cp in cps:
            cp.wait_send()
        mine.wait()

    return pl.pallas_call(
        body, name="gather_small_grads", in_specs=[ANY], out_specs=ANY,
        out_shape=jax.ShapeDtypeStruct((8, r, 128), v.dtype),
        scratch_shapes=[pltpu.SemaphoreType.DMA((7,)), pltpu.SemaphoreType.DMA((7,)), pltpu.SemaphoreType.DMA],
    )(v)


def _sum_slots(parts, name):
    n, r, _ = parts.shape
    tr = _tile(r, 1024, 8)

    def body(p_ref, o_ref):
        acc = p_ref[0]
        for j in range(1, n):
            acc = acc + p_ref[j]
        o_ref[...] = acc

    return pl.pallas_call(
        body, name=name, grid=(r // tr,), in_specs=[pl.BlockSpec((n, tr, 128), lambda i: (0, i, 0))],
        out_specs=pl.BlockSpec((tr, 128), lambda i: (i, 0)), out_shape=jax.ShapeDtypeStruct((r, 128), parts.dtype),
        compiler_params=_params("parallel"),
    )(parts)


def _add_sibling_half(part, recv, c):
    _, rh, _ = recv.shape
    tr = _tile(rh, 1024, 8)
    nblk = rh // tr

    def body(c_ref, p_ref, r_ref, o_ref):
        o_ref[...] = (p_ref[...].astype(F32) + r_ref[...].astype(F32)).astype(o_ref.dtype)

    return pl.pallas_call(
        body, name="grad_pair_sum",
        grid_spec=pltpu.PrefetchScalarGridSpec(
            num_scalar_prefetch=1, grid=(4, nblk),
            in_specs=[pl.BlockSpec((None, tr, 128), lambda j, i, c_ref: (j, c_ref[0] * nblk + i, 0)),
                      pl.BlockSpec((None, tr, 128), lambda j, i, c_ref: (j, i, 0))],
            out_specs=pl.BlockSpec((None, tr, 128), lambda j, i, c_ref: (j, i, 0))),
        out_shape=jax.ShapeDtypeStruct(recv.shape, recv.dtype),
        compiler_params=_params("parallel", "parallel"),
    )(c, part, recv)


def _add_chips(t, recv, me):
    _, rh, _ = t.shape
    tr = _tile(rh, 1024, 8)

    def body(me_ref, t_ref, r_ref, o_ref):
        f = lambda v: v.astype(F32)
        o_ref[...] = ((f(t_ref[...]) + f(r_ref[0])) + f(r_ref[1])) + f(r_ref[2])

    return pl.pallas_call(
        body, name="grad_chip_sum",
        grid_spec=pltpu.PrefetchScalarGridSpec(
            num_scalar_prefetch=1, grid=(rh // tr,),
            in_specs=[pl.BlockSpec((None, tr, 128), lambda i, me_ref: (me_ref[0], i, 0)),
                      pl.BlockSpec((3, tr, 128), lambda i, me_ref: (0, i, 0))],
            out_specs=pl.BlockSpec((tr, 128), lambda i, me_ref: (i, 0))),
        out_shape=jax.ShapeDtypeStruct((rh, 128), F32),
        compiler_params=_params("parallel"),
    )(me, t, recv)


def _reduce_scatter(part):
    x, y, c = _place()
    t = _add_sibling_half(part, _pair_exchange(part), jnp.reshape(c, (1,)).astype(jnp.int32))
    half = _add_chips(t, _chip_exchange(t), jnp.reshape(2 * x + y, (1,)).astype(jnp.int32))
    return _pair_share(half)


def _to_rows(parts, mult):
    flat = jnp.concatenate([p.reshape(-1) for p in parts])
    flat = jnp.pad(flat, (0, (-flat.size) % (128 * mult)))
    return flat.reshape(-1, 128)


def _from_rows(rows, shapes):
    flat, out, at = rows.reshape(-1), [], 0
    for shp in shapes:
        n = int(np.prod(shp))
        out.append(flat[at:at + n].reshape(shp))
        at += n
    return out


def _full_from_shards(g, name, shard_shape):
    depth = shard_shape[0]
    if name in ROW_SHARDED:
        return jnp.moveaxis(g, 0, 1).reshape(depth, 4 * shard_shape[1], shard_shape[2])
    return jnp.moveaxis(g, 0, 2).reshape(depth, shard_shape[1], 4 * shard_shape[2])


def _shards_from_full(full, name):
    depth, rows, cols = full.shape
    if name in ROW_SHARDED:
        return jnp.moveaxis(full.reshape(depth, 4, rows // 4, cols), 1, 0)
    return jnp.moveaxis(full.reshape(depth, rows, 4, cols // 4), 2, 0)


def kernel(x, w_in, b_gate, q_norm_b, k_norm_b, rpb_c, w_branch_a, w_branch_b, w_branch_c, w_out, ln1_g, ln1_b, w_up, w_down, ln2_g, ln2_b, loss_target, m_w_in, m_b_gate, m_q_norm_b, m_k_norm_b, m_rpb_c, m_w_branch_a, m_w_branch_b, m_w_branch_c, m_w_out, m_ln1_g, m_ln1_b, m_w_up, m_w_down, m_ln2_g, m_ln2_b, v_w_in, v_b_gate, v_q_norm_b, v_k_norm_b, v_rpb_c, v_w_branch_a, v_w_branch_b, v_w_branch_c, v_w_out, v_ln1_g, v_ln1_b, v_w_up, v_w_down, v_ln2_g, v_ln2_b):
    args = dict(locals())
    big_shard = {n: args[n] for n in BIG}
    small = {n: args[n] for n in SMALL}
    shapes = [big_shard[n].shape for n in BIG]

    flat = _to_rows([big_shard[n].astype(_MXU) for n in BIG], 32)
    gathered = _gather_shards(flat)
    per_chip = [_from_rows(gathered[j], shapes) for j in range(4)]
    wfull = {n: _full_from_shards(jnp.stack([per_chip[j][i] for j in range(4)]), n, shapes[i]) for i, n in enumerate(BIG)}

    sq, grad_x, grads = _local_step(x[0], loss_target[0], wfull, small)
    loss = lax.psum(0.5 * jnp.sum(sq) / x.shape[-1], ("x", "y", "c"))

    part = jnp.stack([_to_rows([_shards_from_full(grads[n], n)[j].astype(_MXU) for n in BIG], 32) for j in range(4)])
    g_big = _from_rows(_reduce_scatter(part), shapes)
    small_shapes = [small[n].shape for n in SMALL]
    g_small = _from_rows(_sum_slots(_gather_all(_to_rows([grads[n] for n in SMALL], 8)), "small_grad_sum"), small_shapes)
    grad = dict(zip(BIG, g_big))
    grad.update(zip(SMALL, g_small))

    delta, new_m, new_v = {}, {}, {}
    for n in BIG:
        shp = big_shard[n].shape
        two_d = lambda t: t.reshape(-1, shp[-1])
        res = _adamw(two_d(big_shard[n]), two_d(grad[n]), two_d(args["m_" + n]), two_d(args["v_" + n]), "adamw_" + n)
        delta[n], new_m[n], new_v[n] = [t.reshape(shp) for t in res]
    packed = [_to_rows([args[pre + n] for n in SMALL], 8) for pre in ("", "m_", "v_")]
    res = _adamw(packed[0], _to_rows([grad[n] for n in SMALL], 8), packed[1], packed[2], "adamw_small")
    for dst, rows in zip((delta, new_m, new_v), res):
        dst.update(zip(SMALL, _from_rows(rows, small_shapes)))

    order = ("w_in", "b_gate", "q_norm_b", "k_norm_b", "rpb_c", "w_branch_a", "w_branch_b", "w_branch_c", "w_out",
             "ln1_g", "ln1_b", "w_up", "w_down", "ln2_g", "ln2_b")
    return (loss, grad_x[None], *[grad[n] for n in order], *[delta[n] for n in order],
            *[new_m[n] for n in order], *[new_v[n] for n in order])
```

```python
import functools

import numpy as np
import jax
import jax.numpy as jnp
from jax import lax
from jax.experimental import pallas as pl
from jax.experimental.pallas import tpu as pltpu

F32 = jnp.float32
_MXU = jnp.bfloat16

HEAD = 64
A_W, BQ_W, BKV_W, C_W = 256, 512, 128, 256
QKV_W = 768
A_DILATIONS = (1, 4, 16)
A_RADIUS = 64
A_ROPE_HALF = 8
AX_ROPE_HALF = 16
ROPE_THETA = 500000.0
AX_THETA = 10000.0
GRID_W = 64
C_ROWS = 8
C_COLS = 16
BAND = 128
BT_TILES = 18
LN_EPS = 1e-5
RMS_EPS = 1e-6
NEG = -1e30
SCALE = HEAD ** -0.5
LOG2E = 1.4426950408889634
LN2 = 0.6931471805599453
ADAM_LR, ADAM_B1, ADAM_B2, ADAM_EPS, ADAM_WD, ADAM_STEP = 0.001, 0.9, 0.999, 1e-08, 0.01, 10
V7X_VMEM_LIMIT = 48 * 1024 * 1024
MESH = pl.DeviceIdType.MESH
ANY = pl.BlockSpec(memory_space=pl.ANY)


def _params(*sem):
    return pltpu.CompilerParams(dimension_semantics=sem or None, vmem_limit_bytes=V7X_VMEM_LIMIT)


def _tile(n, pref, align=128):
    if n <= pref:
        return n
    t = (pref // align) * align
    while t >= align:
        if n % t == 0:
            return t
        t -= align
    return n


def _mm(a, b, *, name, mode="nn", outs=((F32),), epilogue=None, extras=(), tm=1024, tn=1024, tk=2048, exact=False,
        b_cols=None, b_off=0):
    b, b_lead = b if isinstance(b, tuple) else (b, None)
    m, k = a.shape if mode != "tn" else a.shape[::-1]
    b_rows, b_last = b.shape[-2], (b_cols or b.shape[-1])
    k2, n = (b_rows, b_last) if mode != "nt" else (b_last, b_rows)
    assert k == k2, (a.shape, b.shape, mode)
    tm, tn, tk = _tile(m, tm), _tile(n, tn), _tile(k, tk)
    nk = k // tk
    n_ex, n_out = len(extras), len(outs)
    mx = F32 if exact else _MXU
    prec = lax.Precision.HIGHEST if exact else None
    dims = {"nn": (((1,), (0,)), ((), ())), "nt": (((1,), (1,)), ((), ())), "tn": (((0,), (0,)), ((), ()))}[mode]

    def body(*refs):
        a_ref, b_ref = refs[0], refs[1]
        ex = refs[2:2 + n_ex]
        out_refs = refs[2 + n_ex:2 + n_ex + n_out]
        kk = pl.program_id(2)
        av, bv = a_ref[...].astype(mx), b_ref[...].astype(mx)
        part = lax.dot_general(av, bv, dims, preferred_element_type=F32, precision=prec)

        def finish(res):
            vals = epilogue(res, *[e[...] for e in ex]) if epilogue is not None else (res,)
            for o, v in zip(out_refs, vals):
                o[...] = v.astype(o.dtype)

        if nk == 1:
            finish(part)
        else:
            acc = refs[-1]

            @pl.when(kk == 0)
            def _():
                acc[...] = part

            @pl.when((kk > 0) & (kk < nk - 1))
            def _():
                acc[...] += part

            @pl.when(kk == nk - 1)
            def _():
                finish(acc[...] + part)

    a_spec = pl.BlockSpec((tm, tk), lambda i, j, kk: (i, kk)) if mode != "tn" else pl.BlockSpec((tk, tm), lambda i, j, kk: (kk, i))
    lead = () if b_lead is None else (None,)
    at = () if b_lead is None else (b_lead,)
    if mode == "nt":
        b_spec = pl.BlockSpec(lead + (tn, tk), lambda i, j, kk: at + (j, kk + b_off))
    else:
        b_spec = pl.BlockSpec(lead + (tk, tn), lambda i, j, kk: at + (kk, j + b_off))
    o_spec = pl.BlockSpec((tm, tn), lambda i, j, kk: (i, j))
    res = pl.pallas_call(
        body, name=name, grid=(m // tm, n // tn, nk),
        in_specs=[a_spec, b_spec] + [o_spec] * n_ex,
        out_specs=[o_spec] * n_out,
        out_shape=[jax.ShapeDtypeStruct((m, n), d) for d in outs],
        scratch_shapes=[pltpu.VMEM((tm, tn), F32)] if nk > 1 else [],
        compiler_params=_params("parallel", "parallel", "arbitrary"),
    )(a, b, *extras)
    return res[0] if n_out == 1 else res


def _rows(tm, width, cb=0):
    return pl.BlockSpec((tm, width), lambda t: (t, cb))


def _whole(arr):
    nd = arr.ndim
    return pl.BlockSpec(arr.shape, lambda t: (0,) * nd)


def _rowwise(fn, name, rows, tm, ins, outs):
    n_in, n_out = len(ins), len(outs)
    dil_in = [spec[1:] if isinstance(spec, tuple) else None for _, spec in ins]
    in_specs = [_rows(tm // spec[1], spec[1] * spec[2]) if isinstance(spec, tuple) else spec for _, spec in ins]
    scratch = [pltpu.VMEM((di[1] // 128, tm, 128), F32) for di in dil_in if di] + \
              [pltpu.VMEM((n // 128, tm, 128), F32) for n, _, kind in outs if isinstance(kind, int)]

    def body(*refs):
        scr = list(refs[n_in + n_out:])
        blocks = []
        for r, di in zip(refs[:n_in], dil_in):
            if di is None:
                blocks.append(r[...])
            else:
                d, n = di
                s_ref = scr.pop(0)
                for j in range(d):
                    for b in range(n // 128):
                        lanes = slice(j * n + b * 128, j * n + (b + 1) * 128)
                        s_ref.at[b][pl.ds(j, tm // d, stride=d), :] = r[:, lanes].astype(F32)
                blocks.append(jnp.concatenate([s_ref[b] for b in range(n // 128)], 1))
        vals = fn(*blocks)
        first = pl.program_id(0) == 0
        for (ncols, _, kind), o, v in zip(outs, refs[n_in:n_in + n_out], vals):
            if kind == "row":
                o[...] = v.astype(o.dtype)
            elif isinstance(kind, int):
                s_ref = scr.pop(0)
                for b in range(ncols // 128):
                    s_ref[b] = v[:, b * 128:(b + 1) * 128].astype(F32)
                for j in range(kind):
                    for b in range(ncols // 128):
                        lanes = slice(j * ncols + b * 128, j * ncols + (b + 1) * 128)
                        o[:, lanes] = s_ref.at[b][pl.ds(j, tm // kind, stride=kind), :].astype(o.dtype)
            else:
                part = v.reshape(tm // 8, 8, ncols).sum(0)

                @pl.when(first)
                def _(o=o, part=part):
                    o[...] = part

                @pl.when(jnp.logical_not(first))
                def _(o=o, part=part):
                    o[...] += part

    def out_spec(n, kind):
        if kind == "row":
            return _rows(tm, n), (rows, n)
        if isinstance(kind, int):
            return _rows(tm // kind, kind * n), (rows // kind, kind * n)
        return pl.BlockSpec((8, n), lambda t: (0, 0)), (8, n)

    specs = [out_spec(n, kind) for n, _, kind in outs]
    res = pl.pallas_call(
        body, name=name, grid=(rows // tm,),
        in_specs=in_specs, out_specs=[s for s, _ in specs],
        out_shape=[jax.ShapeDtypeStruct(shp, d) for (_, shp), (_, d, _) in zip(specs, outs)],
        scratch_shapes=scratch, compiler_params=_params("arbitrary"),
    )(*[a for a, _ in ins])
    return res


def _lane_lo(width=128):
    return (lax.broadcasted_iota(jnp.int32, (1, width), 1) & (HEAD * 2 - 1)) < HEAD


def _group_sum(x):
    w = x.shape[-1]
    lane = lax.broadcasted_iota(jnp.int32, (1, w), 1)
    k = HEAD // 2
    while k >= 1:
        x = x + jnp.where((lane & k) != 0, pltpu.roll(x, k, 1), pltpu.roll(x, w - k, 1))
        k //= 2
    return x


def _rot(x, c, sm, sp, shift):
    w = x.shape[-1]
    return x * c + pltpu.roll(x, w - shift, 1) * sm + pltpu.roll(x, shift, 1) * sp


def _rot_t(dy, c, sm, sp, shift):
    w = dy.shape[-1]
    return dy * c + pltpu.roll(dy * sm, shift, 1) + pltpu.roll(dy * sp, w - shift, 1)


def _rope_tables(pos_parts, half, thetas):
    cs, sms, sps = [], [], []
    for pos, theta in zip(pos_parts, thetas):
        inv = theta ** (-jnp.arange(half, dtype=F32) / half)
        ang = pos.astype(F32)[:, None] * inv[None, :]
        co, si, ze = jnp.cos(ang), jnp.sin(ang), jnp.zeros_like(ang)
        cs += [co, co]
        sms += [-si, ze]
        sps += [ze, si]
    return [jnp.concatenate(t, axis=1) for t in (cs, sms, sps)]


def _tables(s):
    pos = jnp.arange(s)
    ca, sma, spa = _rope_tables([pos], A_ROPE_HALF, [ROPE_THETA])
    pad = HEAD - 2 * A_ROPE_HALF
    ca = jnp.concatenate([ca, jnp.ones((s, pad), F32)], 1)
    sma, spa = [jnp.concatenate([t, jnp.zeros((s, pad), F32)], 1) for t in (sma, spa)]
    tab_a = [jnp.tile(t, (1, A_W // HEAD)) for t in (ca, sma, spa)]
    ax = _rope_tables([pos // GRID_W, pos % GRID_W], AX_ROPE_HALF, [AX_THETA, AX_THETA])
    tab_q = [jnp.tile(t, (1, BQ_W // HEAD)) for t in ax]
    tab_k = [jnp.tile(t, (1, BKV_W // HEAD)) for t in ax]
    return tab_a, tab_q, tab_k


def _prep_a(ha, tab, tm):
    s = ha.shape[0]

    def fn(h, c, sm, sp):
        q, k, v = h[:, :A_W], h[:, A_W:2 * A_W], h[:, 2 * A_W:]
        q, k = _rot(q, c, sm, sp, A_ROPE_HALF) * SCALE, _rot(k, c, sm, sp, A_ROPE_HALF)
        return [t for t in (q, k, v) for _ in A_DILATIONS]

    res = _rowwise(fn, "prep_a", s, tm, [(ha, _rows(tm, QKV_W))] + [(t, _rows(tm, A_W)) for t in tab],
                   [(A_W, _MXU, _dil_kind(d)) for _ in range(3) for d in A_DILATIONS])
    n = len(A_DILATIONS)
    return [dict(zip(A_DILATIONS, res[i * n:(i + 1) * n])) for i in range(3)]


def _dil_kind(d):
    return "row" if d == 1 else d


def _dil_spec(d, tm, ncols):
    return _rows(tm, ncols) if d == 1 else ("dil", d, ncols)


def _to_dilations(x, name, tm):
    s, n = x.shape
    res = _rowwise(lambda v: [v for d in A_DILATIONS if d > 1], name, s, tm, [(x, _rows(tm, n))],
                   [(n, x.dtype, d) for d in A_DILATIONS if d > 1])
    return {1: x, **dict(zip([d for d in A_DILATIONS if d > 1], res))}


def _rms(x, g):
    ms = _group_sum(x * x) * (1.0 / HEAD)
    return x * lax.rsqrt(ms + RMS_EPS) * g


def _prep_b(hb, gq, gk, tab_q, tab_k, tm):
    s = hb.shape[0]

    def fn(h, gq, gk, cq, smq, spq, ck, smk, spk):
        xq, xk, v = h[:, :BQ_W], h[:, BQ_W:BQ_W + BKV_W], h[:, BQ_W + BKV_W:]
        q = _rot(_rms(xq, gq), cq, smq, spq, AX_ROPE_HALF) * (SCALE * LOG2E)
        k = _rot(_rms(xk, gk), ck, smk, spk, AX_ROPE_HALF)
        lo = _lane_lo()
        kr, vr = pltpu.roll(k, HEAD, 1), pltpu.roll(v, HEAD, 1)
        kd = jnp.concatenate([jnp.where(lo, k, kr), jnp.where(lo, kr, k)], 1)
        vd = jnp.concatenate([jnp.where(lo, v, vr), jnp.where(lo, vr, v)], 1)
        v1 = jnp.concatenate([jnp.where(lo, v, 1.0), jnp.where(lo, vr, 1.0)], 1)
        return q, kd, vd, v1

    ins = [(hb, _rows(tm, QKV_W)), (gq, _whole(gq)), (gk, _whole(gk))]
    ins += [(t, _rows(tm, BQ_W)) for t in tab_q] + [(t, _rows(tm, BKV_W)) for t in tab_k]
    return _rowwise(fn, "prep_b", s, tm, ins, [(BQ_W, _MXU, "row")] + [(2 * BKV_W, _MXU, "row")] * 3)


def _prep_c(hc, tm):
    def fn(h):
        return h[:, :C_W] * SCALE, h[:, C_W:2 * C_W], h[:, 2 * C_W:]

    return _rowwise(fn, "prep_c", hc.shape[0], tm, [(hc, _rows(tm, QKV_W))], [(C_W, _MXU, "row")] * 3)


def _combine_a(os_, ms, ls, s, tm):
    def fn(o1, o2, o3, m1, m2, m3, l1, l2, l3):
        lo = _lane_lo()
        outs, lses = [], []
        for p in range(A_W // 128):
            st = slice(p * 256, (p + 1) * 256)
            mm = [m[:, st] for m in (m1, m2, m3)]
            ll = [l[:, st] for l in (l1, l2, l3)]
            mmax = jnp.maximum(jnp.maximum(mm[0], mm[1]), mm[2])
            ws = [jnp.exp(m - mmax) for m in mm]
            den = ws[0] * ll[0] + ws[1] * ll[1] + ws[2] * ll[2]
            lses.append(mmax + jnp.log(den))
            num = sum(jnp.where(lo, w[:, :128], w[:, 128:]) * o[:, p * 128:(p + 1) * 128] for w, o in zip(ws, (o1, o2, o3)))
            outs.append(num / jnp.where(lo, den[:, :128], den[:, 128:]))
        o, lse = jnp.concatenate(outs, 1), jnp.concatenate(lses, 1)
        return [o] * len(A_DILATIONS) + [lse] * len(A_DILATIONS)

    ins = [(t, _dil_spec(d, tm, w)) for ts, w in ((os_, A_W), (ms, 2 * A_W), (ls, 2 * A_W)) for t, d in zip(ts, A_DILATIONS)]
    res = _rowwise(fn, "combine_a", s, tm, ins,
                   [(w, F32, _dil_kind(d)) for w in (A_W, 2 * A_W) for d in A_DILATIONS])
    n = len(A_DILATIONS)
    return dict(zip(A_DILATIONS, res[:n])), dict(zip(A_DILATIONS, res[n:]))


def _gates(hg, bg, d):
    return [jax.nn.sigmoid(hg[:, i * d:(i + 1) * d] + bg[:, i * d:(i + 1) * d]) for i in range(3)]


def _gate_merge(hg, bg, pa, pb, pc, tm):
    s, d = pa.shape

    def fn(hg, bg, pa, pb, pc):
        g = _gates(hg, bg, d)
        return (g[0] * pa + g[1] * pb + g[2] * pc,)

    ins = [(hg, _rows(tm, 3 * d)), (bg, _whole(bg))] + [(p, _rows(tm, d)) for p in (pa, pb, pc)]
    return _rowwise(fn, "gate_merge", s, tm, ins, [(d, _MXU, "row")])[0]


def _gate_bwd(dm, hg, bg, pa, pb, pc, tm):
    s, d = pa.shape

    def fn(dm, hg, bg, pa, pb, pc):
        g = _gates(hg, bg, d)
        dlog = jnp.concatenate([dm * p * gi * (1.0 - gi) for p, gi in zip((pa, pb, pc), g)], 1)
        return dm * g[0], dm * g[1], dm * g[2], dlog, dlog

    ins = [(dm, _rows(tm, d)), (hg, _rows(tm, 3 * d)), (bg, _whole(bg))] + [(p, _rows(tm, d)) for p in (pa, pb, pc)]
    return _rowwise(fn, "gate_bwd", s, tm, ins, [(d, _MXU, "row")] * 3 + [(3 * d, _MXU, "row"), (3 * d, F32, "acc")])


def _ln_stats(r):
    mu = jnp.mean(r, -1, keepdims=True)
    xc = r - mu
    var = jnp.mean(xc * xc, -1, keepdims=True)
    rstd = lax.rsqrt(var + LN_EPS)
    return xc * rstd, rstd


def _ln_fwd(x, br, g, b, alpha, name, tm):
    s, d = x.shape

    def fn(x, br, g, b):
        r = alpha * x + br
        xhat, _ = _ln_stats(r)
        y = xhat * g + b
        return r, y, y

    ins = [(x, _rows(tm, d)), (br, _rows(tm, d)), (g, _whole(g)), (b, _whole(b))]
    return _rowwise(fn, name, s, tm, ins, [(d, F32, "row"), (d, F32, "row"), (d, _MXU, "row")])


def _ln_bwd(dy, r, g, name, tm):
    s, d = r.shape

    def fn(dy, r, g):
        xhat, rstd = _ln_stats(r)
        dxh = dy * g
        dr = rstd * (dxh - jnp.mean(dxh, -1, keepdims=True) - xhat * jnp.mean(dxh * xhat, -1, keepdims=True))
        return dr, dr, dy * xhat, dy

    ins = [(dy, _rows(tm, d)), (r, _rows(tm, d)), (g, _whole(g))]
    return _rowwise(fn, name, s, tm, ins, [(d, F32, "row"), (d, _MXU, "row"), (d, F32, "acc"), (d, F32, "acc")])


def _loss_head(y, target, tm):
    s, d = y.shape

    def fn(y, t):
        diff = y - t
        return diff * diff, diff * (1.0 / d)

    sq, dy = _rowwise(fn, "loss_head", s, tm, [(y, _rows(tm, d)), (target, _rows(tm, d))], [(d, F32, "acc"), (d, F32, "row")])
    return sq, dy


def _post_a(dqs, dks, dvs, tab, s, tm):
    def fn(q1, q2, q3, k1, k2, k3, v1, v2, v3, c, sm, sp):
        dq = _rot_t((q1 + q2 + q3) * SCALE, c, sm, sp, A_ROPE_HALF)
        dk = _rot_t(k1 + k2 + k3, c, sm, sp, A_ROPE_HALF)
        return (jnp.concatenate([dq, dk, v1 + v2 + v3], 1),)

    ins = [(t, _dil_spec(d, tm, A_W)) for ts in (dqs, dks, dvs) for t, d in zip(ts, A_DILATIONS)]
    ins += [(t, _rows(tm, A_W)) for t in tab]
    return _rowwise(fn, "post_a", s, tm, ins, [(QKV_W, _MXU, "row")])[0]


def _post_b(dq, dkd, dvd, hb, gq, gk, tab_q, tab_k, tm):
    s = dq.shape[0]

    def back(dz, x, g, c, sm, sp):
        dy = _rot_t(dz, c, sm, sp, AX_ROPE_HALF)
        rstd = lax.rsqrt(_group_sum(x * x) * (1.0 / HEAD) + RMS_EPS)
        xh = x * rstd
        dxh = dy * g
        return rstd * (dxh - xh * (_group_sum(dxh * xh) * (1.0 / HEAD))), dy * xh

    def fn(dq, dkd, dvd, h, gq, gk, cq, smq, spq, ck, smk, spk):
        lo = _lane_lo()
        dk = jnp.where(lo, dkd[:, :128], dkd[:, 128:])
        dv = jnp.where(lo, dvd[:, :128], dvd[:, 128:])
        dxq, dgq = back(dq * SCALE, h[:, :BQ_W], gq, cq, smq, spq)
        dxk, dgk = back(dk, h[:, BQ_W:BQ_W + BKV_W], gk, ck, smk, spk)
        return jnp.concatenate([dxq, dxk, dv], 1), dgq, dgk

    ins = [(dq, _rows(tm, BQ_W)), (dkd, _rows(tm, 2 * BKV_W)), (dvd, _rows(tm, 2 * BKV_W)), (hb, _rows(tm, QKV_W)),
           (gq, _whole(gq)), (gk, _whole(gk))]
    ins += [(t, _rows(tm, BQ_W)) for t in tab_q] + [(t, _rows(tm, BKV_W)) for t in tab_k]
    return _rowwise(fn, "post_b", s, tm, ins, [(QKV_W, _MXU, "row"), (BQ_W, F32, "acc"), (BKV_W, F32, "acc")])


def _post_c(dq, dk, dv, tm):
    def fn(dq, dk, dv):
        return (jnp.concatenate([dq * SCALE, dk, dv], 1),)

    return _rowwise(fn, "post_c", dq.shape[0], tm, [(t, _rows(tm, C_W)) for t in (dq, dk, dv)], [(QKV_W, _MXU, "row")])[0]


def _adamw(w, g, m, v, name):
    rows, cols = w.shape
    tm = _tile(rows, 256, 8)

    def fn(w, g, m, v):
        m = ADAM_B1 * m + (1.0 - ADAM_B1) * g
        v = ADAM_B2 * v + (1.0 - ADAM_B2) * (g * g)
        m_hat = m / (1.0 - ADAM_B1 ** ADAM_STEP)
        v_hat = v / (1.0 - ADAM_B2 ** ADAM_STEP)
        delta = -ADAM_LR * (m_hat / (jnp.sqrt(v_hat) + ADAM_EPS) + ADAM_WD * w)
        return delta, m, v

    return _rowwise(fn, name, rows, tm, [(t, _rows(tm, cols)) for t in (w, g, m, v)], [(cols, F32, "row")] * 3)


def _dot_t(a, b):
    return lax.dot_general(a, b, (((1,), (1,)), ((), ())), preferred_element_type=F32)


def _tdot(a, b):
    return lax.dot_general(a, b, (((0,), (0,)), ((), ())), preferred_element_type=F32)


def _head_masks():
    lo = _lane_lo()
    return lo, (lo, jnp.logical_not(lo))


def _rep(x, rows):
    return jnp.broadcast_to(x, (rows, 128))


def _row_lo():
    return lax.broadcasted_iota(jnp.int32, (128, 1), 0) < HEAD


def _flash_fwd(q, kd, v1, tq, tk):
    s = q.shape[0]
    tq, tk = _tile(s, tq), _tile(s, tk)
    nk = s // tk
    mx = _MXU

    def body(q_ref, k_ref, v_ref, o_ref, lse_ref, m_ref, acc_ref):
        kk = pl.program_id(2)

        @pl.when(kk == 0)
        def _():
            m_ref[...] = jnp.full_like(m_ref, NEG)
            acc_ref[...] = jnp.zeros_like(acc_ref)

        q2, k2, v2 = q_ref[...], k_ref[...], v_ref[...]
        _, masks = _head_masks()
        for h in range(2):
            st = _dot_t(k2, jnp.where(masks[h], q2, jnp.zeros_like(q2)))
            m_prev = m_ref[h]
            m_new = jnp.maximum(m_prev, jnp.max(st, 0, keepdims=True))
            p = jnp.exp2(st - m_new)
            m_ref[h] = m_new
            acc_ref[h] = acc_ref[h] * jnp.exp2(m_prev - m_new) + _tdot(v2, p.astype(mx))

        @pl.when(kk == nk - 1)
        def _():
            a0, a1 = acc_ref[0], acc_ref[1]
            l0, l1 = a0[HEAD:HEAD + 1], a1[HEAD:HEAD + 1]
            o_ref[...] = jnp.concatenate([a0[:HEAD] / l0, a1[:HEAD] / l1], 0).T
            lse_ref[...] = jnp.concatenate([m_ref[0] + jnp.log2(l0), m_ref[1] + jnp.log2(l1), jnp.zeros((6, tq), F32)], 0)

    return pl.pallas_call(
        body, name="attn_b_fwd", grid=(BQ_W // 128, s // tq, nk),
        in_specs=[pl.BlockSpec((tq, 128), lambda j, i, kk: (i, j)),
                  pl.BlockSpec((tk, 128), lambda j, i, kk: (kk, j // 2)),
                  pl.BlockSpec((tk, 128), lambda j, i, kk: (kk, j // 2))],
        out_specs=[pl.BlockSpec((tq, 128), lambda j, i, kk: (i, j)), pl.BlockSpec((None, 8, tq), lambda j, i, kk: (j, 0, i))],
        out_shape=[jax.ShapeDtypeStruct((s, BQ_W), F32), jax.ShapeDtypeStruct((BQ_W // 128, 8, s), F32)],
        scratch_shapes=[pltpu.VMEM((2, 1, tq), F32), pltpu.VMEM((2, 128, tq), F32)],
        compiler_params=_params("parallel", "parallel", "arbitrary"),
    )(q, kd, v1)


def _delta_b(do, o, tq):
    s = do.shape[0]
    tq = _tile(s, tq)

    def body(do_ref, o_ref, d_ref):
        prod = do_ref[...] * o_ref[...]
        row = lax.broadcasted_iota(jnp.int32, (8, 128), 0)
        lane = lax.broadcasted_iota(jnp.int32, (8, 128), 1)
        sel = jnp.where(((row == 0) & (lane < HEAD)) | ((row == 1) & (lane >= HEAD)), 1.0, 0.0).astype(F32)
        d_ref[...] = lax.dot_general(sel, prod, (((1,), (1,)), ((), ())), preferred_element_type=F32,
                                     precision=lax.Precision.HIGHEST)

    qs = pl.BlockSpec((tq, 128), lambda j, i: (i, j))
    return pl.pallas_call(
        body, name="attn_b_delta", grid=(BQ_W // 128, s // tq), in_specs=[qs, qs],
        out_specs=pl.BlockSpec((None, 8, tq), lambda j, i: (j, 0, i)),
        out_shape=jax.ShapeDtypeStruct((BQ_W // 128, 8, s), F32),
        compiler_params=_params("parallel", "parallel"),
    )(do, o)


def _flash_bwd(q, kd, vd, do, lse, delta, tq, tk):
    s = q.shape[0]
    tq, tk = _tile(s, tq), _tile(s, tk)
    nq, nk = s // tq, s // tk
    group = BQ_W // 128 // 2
    mx = _MXU

    def body(k_ref, v_ref, q_ref, do_ref, lse_ref, dl_ref, dq_hbm, dk_ref, dv_ref, dk_acc, dv_acc, dqt, stage, sem):
        e, kk, jj, i = pl.program_id(0), pl.program_id(1), pl.program_id(2), pl.program_id(3)

        @pl.when((jj == 0) & (i == 0))
        def _():
            dk_acc[...] = jnp.zeros_like(dk_acc)
            dv_acc[...] = jnp.zeros_like(dv_acc)

        @pl.when(kk == 0)
        def _():
            dqt[jj, i] = jnp.zeros((128, tq), F32)

        q2, k2, v2, do2 = q_ref[...], k_ref[...], v_ref[...], do_ref[...].astype(mx)
        lse8, dl8 = lse_ref[...], dl_ref[...]
        _, masks = _head_masks()
        dqs = []
        for h in range(2):
            qh = jnp.where(masks[h], q2, jnp.zeros_like(q2))
            doh = jnp.where(masks[h], do2, jnp.zeros_like(do2))
            p = jnp.exp2(_dot_t(k2, qh) - lse8[h:h + 1])
            ds = p * (_dot_t(v2, doh) - dl8[h:h + 1])
            p, ds = p.astype(mx), ds.astype(mx)
            dv_acc[...] += jnp.dot(p, doh, preferred_element_type=F32)
            dk_acc[...] += jnp.dot(ds, qh, preferred_element_type=F32)
            dqs.append(_tdot(k2, ds))
        dqt[jj, i] += jnp.where(_row_lo(), dqs[0], dqs[1])

        @pl.when(kk == nk - 1)
        def _():
            stage[...] = dqt[jj, i].T
            lane0 = pl.multiple_of((group * e + jj) * 128, 128)
            cp = pltpu.make_async_copy(stage, dq_hbm.at[pl.ds(pl.multiple_of(i * tq, tq), tq), pl.ds(lane0, 128)], sem)
            cp.start()
            cp.wait()

        @pl.when((jj == group - 1) & (i == nq - 1))
        def _():
            dk_ref[...] = (dk_acc[...] + pltpu.roll(dk_acc[...], HEAD, 1)) * LN2
            dv_ref[...] = dv_acc[...] + pltpu.roll(dv_acc[...], HEAD, 1)

    ks = pl.BlockSpec((tk, 128), lambda e, kk, jj, i: (kk, e))
    qs = pl.BlockSpec((tq, 128), lambda e, kk, jj, i: (i, group * e + jj))
    st = pl.BlockSpec((None, 8, tq), lambda e, kk, jj, i: (group * e + jj, 0, i))
    return pl.pallas_call(
        body, name="attn_b_bwd", grid=(BKV_W // HEAD, nk, group, nq),
        in_specs=[ks, ks, qs, qs, st, st], out_specs=[ANY, ks, ks],
        out_shape=[jax.ShapeDtypeStruct((s, BQ_W), F32)] + [jax.ShapeDtypeStruct((s, 2 * BKV_W), F32)] * 2,
        scratch_shapes=[pltpu.VMEM((tk, 128), F32)] * 2 + [pltpu.VMEM((group, nq, 128, tq), F32), pltpu.VMEM((tq, 128), F32),
                                                          pltpu.SemaphoreType.DMA],
        compiler_params=_params("parallel", "arbitrary", "arbitrary", "arbitrary"),
    )(kd, vd, q, do, lse, delta)


def _p_and_ds(q2, k2, v2, do2, o2, lse2, masks, mask=None, bias=None):
    mx = _MXU
    out = []
    for h in range(2):
        qh = jnp.where(masks[h], q2, jnp.zeros_like(q2))
        sc = _dot_t(qh, k2)
        if bias is not None:
            sc = sc + bias[h]
        if mask is not None:
            sc = jnp.where(mask, sc, NEG)
        lse = jnp.max(lse2[:, h * 128:(h + 1) * 128], -1, keepdims=True)
        p = jnp.exp(sc - lse)
        doh = jnp.where(masks[h], do2, jnp.zeros_like(do2))
        delta = jnp.sum(doh * o2, -1, keepdims=True)
        dp = _dot_t(doh.astype(mx), v2)
        ds = p * (dp - delta)
        out.append((qh, p, ds, doh))
    return out


class _BandA:
    hb, has_bias, name = 1, False, "a"

    def __init__(self, nb):
        self.nb = nb

    def mask(self, qidx, kidx):
        n = self.nb * BAND
        return (jnp.abs(qidx - kidx) <= A_RADIUS) & (kidx >= 0) & (kidx < n) & (qidx >= 0) & (qidx < n)


class _BandC:
    hb, has_bias, name = 3, True, "c"

    def __init__(self, nb):
        self.nb = nb
        self.rows = nb * BAND // GRID_W
        per = BAND // GRID_W
        assert self.rows >= C_ROWS and (C_ROWS - 1) // per <= self.hb
        assert (self.rows - 1) // per - (self.rows - C_ROWS) // per <= self.hb

    def mask(self, qidx, kidx):
        n = self.nb * BAND
        sh = GRID_W.bit_length() - 1
        qrow, cq = qidx >> sh, qidx & (GRID_W - 1)
        krow, ck = kidx >> sh, kidx & (GRID_W - 1)
        r0 = jnp.clip(qrow - C_ROWS // 2, 0, self.rows - C_ROWS)
        c0 = jnp.clip(cq - C_COLS // 2, 0, GRID_W - C_COLS)
        ok = (qidx >= 0) & (qidx < n) & (kidx >= 0) & (kidx < n)
        return ok & (krow >= r0) & (krow < r0 + C_ROWS) & (ck >= c0) & (ck < c0 + C_COLS)


def _bias_tile(off, a):
    return (BAND // GRID_W) * off - a + (C_ROWS - 1) + 2


def _band_bias_k(band, bt_ref, h):
    per = BAND // GRID_W
    return jnp.concatenate([jnp.concatenate([bt_ref[h, _bias_tile(off, a)] for off in range(-band.hb, band.hb + 1)], 1)
                            for a in range(per)], 0)


def _band_bias_q(band, bt_ref, h):
    per = BAND // GRID_W
    return jnp.concatenate([bt_ref[h, _bias_tile(-off, a)] for off in range(-band.hb, band.hb + 1) for a in range(per)], 0)


def _band_split(nb, ncb):
    cb = max(c for c in (4, 2, 1) if ncb % c == 0)
    rb = max(r for r in (4, 2, 1) if nb % r == 0 and r * cb <= 8)
    return rb, cb


def _band_specs(band, rb, cb, nb, width):
    def edge(first):
        return pl.BlockSpec((BAND, cb * width), lambda c, i: (jnp.clip(i * rb + first, 0, nb - 1), c))

    main = pl.BlockSpec((rb * BAND, cb * width), lambda c, i: (i, c))
    return [edge(t - band.hb) for t in range(band.hb)] + [main] + [edge(rb + t) for t in range(band.hb)]


def _band_rows(band, refs, rb, r, lanes):
    hb = band.hb
    parts = []
    for b in range(r, r + 2 * hb + 1):
        if b < hb:
            parts.append(refs[b][:, lanes])
        elif b < hb + rb:
            parts.append(refs[hb][(b - hb) * BAND:(b - hb + 1) * BAND, lanes])
        else:
            parts.append(refs[b - rb + 1][:, lanes])
    return jnp.concatenate(parts, 0)


def _band_idx(band, blk, rows_of_blocks, axis):
    shape = (rows_of_blocks * BAND, 1) if axis == 0 else (1, rows_of_blocks * BAND)
    return blk * BAND + lax.broadcasted_iota(jnp.int32, shape, axis)


def _band_fwd(band, q, k, v, bt=None):
    n, w = q.shape
    nb, ncb, nband = n // BAND, w // 128, 2 * band.hb + 1
    rb, cb = _band_split(nb, ncb)
    mx = _MXU
    raw = not band.has_bias

    def body(*refs):
        q_ref, k_refs, v_refs = refs[0], refs[1:1 + nband], refs[1 + nband:1 + 2 * nband]
        rest = refs[1 + 2 * nband:]
        bt_ref = rest[0] if band.has_bias else None
        outs = rest[1:] if band.has_bias else rest
        i = pl.program_id(1)
        lo, masks = _head_masks()
        for r in range(rb):
            blk = i * rb + r
            mask = band.mask(_band_idx(band, blk, 1, 0), _band_idx(band, blk - band.hb, nband, 1))
            for c in range(cb):
                lanes, rows = slice(c * 128, (c + 1) * 128), slice(r * BAND, (r + 1) * BAND)
                q2 = q_ref[rows, lanes]
                kcat, vcat = _band_rows(band, k_refs, rb, r, lanes), _band_rows(band, v_refs, rb, r, lanes)
                os_, ms, ls = [], [], []
                for h in range(2):
                    sc = _dot_t(jnp.where(masks[h], q2, jnp.zeros_like(q2)), kcat)
                    if band.has_bias:
                        sc = sc + _band_bias_k(band, bt_ref, 2 * c + h)
                    sc = jnp.where(mask, sc, NEG)
                    m = jnp.max(sc, -1, keepdims=True)
                    p = jnp.exp(sc - m)
                    ms.append(m)
                    ls.append(jnp.sum(p, -1, keepdims=True))
                    os_.append(jnp.dot(p.astype(mx), vcat, preferred_element_type=F32))
                if raw:
                    o_ref, m_ref, l_ref = outs
                    o_ref[rows, lanes] = jnp.where(lo, os_[0], os_[1])
                    for h in range(2):
                        st_lanes = slice(c * 256 + h * 128, c * 256 + (h + 1) * 128)
                        m_ref[rows, st_lanes] = _rep(ms[h], BAND)
                        l_ref[rows, st_lanes] = _rep(ls[h], BAND)
                else:
                    o_ref, lse_ref = outs
                    o_ref[rows, lanes] = jnp.where(lo, os_[0] / ls[0], os_[1] / ls[1])
                    for h in range(2):
                        lse_ref[rows, c * 256 + h * 128:c * 256 + (h + 1) * 128] = _rep(ms[h] + jnp.log(ls[h]), BAND)

    qs = pl.BlockSpec((rb * BAND, cb * 128), lambda c, i: (i, c))
    ks = _band_specs(band, rb, cb, nb, 128)
    st = pl.BlockSpec((rb * BAND, cb * 256), lambda c, i: (i, c))
    in_specs, args = [qs] + ks + ks, [q] + [k] * nband + [v] * nband
    if band.has_bias:
        in_specs.append(pl.BlockSpec((2 * cb, BT_TILES, GRID_W, 128), lambda c, i: (c, 0, 0, 0)))
        args.append(bt)
    n_stats = 2 if raw else 1
    return pl.pallas_call(
        body, name="attn_%s_fwd" % band.name, grid=(ncb // cb, nb // rb), in_specs=in_specs,
        out_specs=[qs] + [st] * n_stats,
        out_shape=[jax.ShapeDtypeStruct((n, w), F32)] + [jax.ShapeDtypeStruct((n, 2 * w), F32)] * n_stats,
        compiler_params=_params("parallel", "arbitrary"),
    )(*args)


def _band_dq(band, q, k, v, do, o, lse, bt=None):
    n, w = q.shape
    nb, ncb, nband = n // BAND, w // 128, 2 * band.hb + 1
    rb, cb = _band_split(nb, ncb)
    mx = _MXU
    per = BAND // GRID_W

    def body(*refs):
        q_ref, k_refs, v_refs = refs[0], refs[1:1 + nband], refs[1 + nband:1 + 2 * nband]
        do_ref, o_ref, lse_ref = refs[1 + 2 * nband:4 + 2 * nband]
        rest = refs[4 + 2 * nband:]
        dq_ref = rest[1] if band.has_bias else rest[0]
        i = pl.program_id(1)
        lo, masks = _head_masks()
        if band.has_bias:
            dbt_ref = rest[2]

            @pl.when(i == 0)
            def _():
                dbt_ref[...] = jnp.zeros_like(dbt_ref)

        for r in range(rb):
            blk = i * rb + r
            mask = band.mask(_band_idx(band, blk, 1, 0), _band_idx(band, blk - band.hb, nband, 1))
            for c in range(cb):
                lanes, rows = slice(c * 128, (c + 1) * 128), slice(r * BAND, (r + 1) * BAND)
                kcat, vcat = _band_rows(band, k_refs, rb, r, lanes), _band_rows(band, v_refs, rb, r, lanes)
                bias = [_band_bias_k(band, rest[0], 2 * c + h) for h in range(2)] if band.has_bias else None
                hs = _p_and_ds(q_ref[rows, lanes], kcat, vcat, do_ref[rows, lanes], o_ref[rows, lanes],
                               lse_ref[rows, c * 256:(c + 1) * 256], masks, mask, bias)
                dqs = [jnp.dot(ds.astype(mx), kcat, preferred_element_type=F32) for _, _, ds, _ in hs]
                dq_ref[rows, lanes] = jnp.where(lo, dqs[0], dqs[1])
                if band.has_bias:
                    for h in range(2):
                        ds = hs[h][2]
                        for a in range(per):
                            for t in range(nband):
                                tile = ds[a * GRID_W:(a + 1) * GRID_W, t * 128:(t + 1) * 128]
                                dbt_ref[2 * c + h, _bias_tile(t - band.hb, a)] += tile

    qs = pl.BlockSpec((rb * BAND, cb * 128), lambda c, i: (i, c))
    ks = _band_specs(band, rb, cb, nb, 128)
    st = pl.BlockSpec((rb * BAND, cb * 256), lambda c, i: (i, c))
    in_specs, args = [qs] + ks + ks + [qs, qs, st], [q] + [k] * nband + [v] * nband + [do, o, lse]
    out_specs, out_shape = [qs], [jax.ShapeDtypeStruct((n, w), F32)]
    if band.has_bias:
        bts = pl.BlockSpec((2 * cb, BT_TILES, GRID_W, 128), lambda c, i: (c, 0, 0, 0))
        in_specs.append(bts)
        args.append(bt)
        out_specs.append(bts)
        out_shape.append(jax.ShapeDtypeStruct(bt.shape, F32))
    return pl.pallas_call(
        body, name="attn_%s_dq" % band.name, grid=(ncb // cb, nb // rb), in_specs=in_specs, out_specs=out_specs,
        out_shape=out_shape, compiler_params=_params("parallel", "arbitrary"),
    )(*args)


def _band_dkv(band, q, k, v, do, o, lse, bt=None):
    n, w = q.shape
    nb, ncb, nband = n // BAND, w // 128, 2 * band.hb + 1
    rb, cb = _band_split(nb, ncb)
    mx = _MXU

    def body(*refs):
        k_ref, v_ref = refs[0], refs[1]
        q_refs, do_refs, o_refs, lse_refs = [refs[2 + g * nband:2 + (g + 1) * nband] for g in range(4)]
        rest = refs[2 + 4 * nband:]
        dk_ref, dv_ref = rest[-2], rest[-1]
        i = pl.program_id(1)
        _, masks = _head_masks()
        for r in range(rb):
            blk = i * rb + r
            mask = band.mask(_band_idx(band, blk - band.hb, nband, 0), _band_idx(band, blk, 1, 1))
            for c in range(cb):
                lanes, rows = slice(c * 128, (c + 1) * 128), slice(r * BAND, (r + 1) * BAND)
                qcat, docat, ocat = [_band_rows(band, g, rb, r, lanes) for g in (q_refs, do_refs, o_refs)]
                lsecat = _band_rows(band, lse_refs, rb, r, slice(c * 256, (c + 1) * 256))
                bias = [_band_bias_q(band, rest[0], 2 * c + h) for h in range(2)] if band.has_bias else None
                hs = _p_and_ds(qcat, k_ref[rows, lanes], v_ref[rows, lanes], docat, ocat, lsecat, masks, mask, bias)
                dk_ref[rows, lanes] = sum(_tdot(ds.astype(mx), qh) for qh, _, ds, _ in hs)
                dv_ref[rows, lanes] = sum(_tdot(p.astype(mx), doh.astype(mx)) for _, p, _, doh in hs)

    ks = pl.BlockSpec((rb * BAND, cb * 128), lambda c, i: (i, c))
    in_specs = [ks, ks] + _band_specs(band, rb, cb, nb, 128) * 3 + _band_specs(band, rb, cb, nb, 256)
    args = [k, v] + [q] * nband + [do] * nband + [o] * nband + [lse] * nband
    if band.has_bias:
        in_specs.append(pl.BlockSpec((2 * cb, BT_TILES, GRID_W, 128), lambda c, i: (c, 0, 0, 0)))
        args.append(bt)
    return pl.pallas_call(
        body, name="attn_%s_dkv" % band.name, grid=(ncb // cb, nb // rb), in_specs=in_specs, out_specs=[ks, ks],
        out_shape=[jax.ShapeDtypeStruct((n, w), F32)] * 2,
        compiler_params=_params("parallel", "arbitrary"),
    )(*args)


def _dc_onehot():
    c = np.arange(GRID_W)
    dc = np.clip(c[None, :] - c[:, None] + (C_COLS - 1), 0, 2 * C_COLS - 2).reshape(-1)
    m = np.zeros((GRID_W * GRID_W, 128), np.float32)
    m[np.arange(dc.size), dc] = 1.0
    return m


def _bias_tiles(rpb):
    h, nr, ncol = rpb.shape
    flat = jnp.pad(rpb.reshape(h * nr, ncol), ((0, (-h * nr) % 8), (0, 128 - ncol)))
    tiles = _mm(flat, jnp.asarray(_dc_onehot().T), name="rpb_tiles", exact=True, tn=GRID_W * GRID_W)
    tiles = tiles[:h * nr].reshape(h, nr, GRID_W, GRID_W)
    tiles = jnp.pad(tiles, ((0, 0), (2, BT_TILES + 1 - nr - 2), (0, 0), (0, 0)))
    return jnp.concatenate([tiles[:, :BT_TILES], tiles[:, 1:BT_TILES + 1]], -1)


def _bias_tiles_grad(dbt, nr, ncol):
    h = dbt.shape[0]
    d = dbt[:, 2:2 + nr, :, :GRID_W] + dbt[:, 1:1 + nr, :, GRID_W:]
    flat = jnp.pad(d.reshape(h * nr, GRID_W * GRID_W), ((0, (-h * nr) % 8), (0, 0)))
    g = _mm(flat, jnp.asarray(_dc_onehot()), name="rpb_grad", exact=True, tk=GRID_W * GRID_W)
    return g[:h * nr, :ncol].reshape(h, nr, ncol)


TM = 256
TQ_B, TK_B = 1024, 1024


def _relu2(acc):
    r = jnp.maximum(acc, 0.0)
    return (r * r,)


def _layer_fwd(x, xb, w, sm, tabs, alpha):
    tab_a, tab_q, tab_k = tabs
    s, d = x.shape
    ha = _mm(xb, w["in"], name="in_a", tn=QKV_W, b_cols=QKV_W, b_off=0)
    hb = _mm(xb, w["in"], name="in_b", tn=QKV_W, b_cols=QKV_W, b_off=1)
    hc = _mm(xb, w["in"], name="in_c", tn=QKV_W, b_cols=QKV_W, b_off=2)
    hg = _mm(xb, w["in"], name="in_g", outs=(_MXU,), tn=QKV_W, b_cols=3 * d, b_off=3)

    qa, ka, va = _prep_a(ha, tab_a, TM)
    stats = [_band_fwd(_BandA(s // dil // BAND), qa[dil], ka[dil], va[dil]) for dil in A_DILATIONS]
    oas, lse_a = _combine_a(*zip(*stats), s, TM)
    oa = oas[1]

    qb, kd, vd, v1 = _prep_b(hb, sm["q_norm"], sm["k_norm"], tab_q, tab_k, TM)
    ob, lse_b = _flash_fwd(qb, kd, v1, TQ_B, TK_B)

    qc, kc, vc = _prep_c(hc, TM)
    bt = _bias_tiles(sm["rpb"])
    oc, lse_c = _band_fwd(_BandC(s // BAND), qc, kc, vc, bt)

    pa = _mm(oa, w["br_a"], name="br_a", outs=(_MXU,))
    pb = _mm(ob, w["br_b"], name="br_b", outs=(_MXU,))
    pc = _mm(oc, w["br_c"], name="br_c", outs=(_MXU,))
    merged = _gate_merge(hg, sm["b_gate"], pa, pb, pc, TM)
    mix = _mm(merged, w["out"], name="w_out")
    r1, x1, x1b = _ln_fwd(x, mix, sm["ln1_g"], sm["ln1_b"], alpha, "ln1_fwd", TM)
    act = _mm(x1b, w["up"], name="w_up", outs=(_MXU,), epilogue=_relu2)
    ff = _mm(act, w["down"], name="w_down")
    r2, x2, x2b = _ln_fwd(x1, ff, sm["ln2_g"], sm["ln2_b"], alpha, "ln2_fwd", TM)
    saved = dict(xb=xb, hb=hb, hg=hg, qa=qa, ka=ka, va=va, oa=oa, oas=oas, lse_a=lse_a, qb=qb, kd=kd, vd=vd, ob=ob, lse_b=lse_b,
                 qc=qc, kc=kc, vc=vc, oc=oc, lse_c=lse_c, bt=bt, pa=pa, pb=pb, pc=pc, merged=merged, r1=r1, x1b=x1b,
                 act=act, r2=r2)
    return x2, x2b, saved


def _layer_bwd(dx2, w, sm, sv, tabs, alpha):
    tab_a, tab_q, tab_k = tabs
    s, d = dx2.shape
    g = {}
    dr2, dr2b, dg2, db2 = _ln_bwd(dx2, sv["r2"], sm["ln2_g"], "ln2_bwd", TM)
    g["ln2_g"], g["ln2_b"] = dg2.sum(0), db2.sum(0)
    du = _mm(dr2b, w["down"], mode="nt", name="d_act", outs=(_MXU,), extras=(sv["act"],),
             epilogue=lambda acc, act: (acc * (2.0 * jnp.sqrt(act.astype(F32))),))
    g["w_down"] = _mm(sv["act"], dr2b, mode="tn", name="g_w_down")
    g["w_up"] = _mm(sv["x1b"], du, mode="tn", name="g_w_up")
    dx1 = _mm(du, w["up"], mode="nt", name="d_x1", extras=(dr2,), epilogue=lambda acc, e: (acc + alpha * e,))
    dr1, dr1b, dg1, db1 = _ln_bwd(dx1, sv["r1"], sm["ln1_g"], "ln1_bwd", TM)
    g["ln1_g"], g["ln1_b"] = dg1.sum(0), db1.sum(0)
    g["w_out"] = _mm(sv["merged"], dr1b, mode="tn", name="g_w_out")
    dmerged = _mm(dr1b, w["out"], mode="nt", name="d_merged")
    dpa, dpb, dpc, dlog, gb = _gate_bwd(dmerged, sv["hg"], sm["b_gate"], sv["pa"], sv["pb"], sv["pc"], TM)
    g["b_gate"] = gb.sum(0)
    g["w_branch_a"] = _mm(sv["oa"], dpa, mode="tn", name="g_br_a")
    g["w_branch_b"] = _mm(sv["ob"], dpb, mode="tn", name="g_br_b")
    g["w_branch_c"] = _mm(sv["oc"], dpc, mode="tn", name="g_br_c")
    doa = _mm(dpa, w["br_a"], mode="nt", name="d_oa")
    dob = _mm(dpb, w["br_b"], mode="nt", name="d_ob")
    doc = _mm(dpc, w["br_c"], mode="nt", name="d_oc")

    dqs, dks, dvs = [], [], []
    doas = _to_dilations(doa, "d_oa_layouts", TM)
    for dil in A_DILATIONS:
        band = _BandA(s // dil // BAND)
        args = [t[dil] for t in (sv["qa"], sv["ka"], sv["va"], doas, sv["oas"], sv["lse_a"])]
        dqs.append(_band_dq(band, *args)[0])
        dk_c, dv_c = _band_dkv(band, *args)
        dks.append(dk_c)
        dvs.append(dv_c)
    dha = _post_a(dqs, dks, dvs, tab_a, s, TM)

    dqb, dkd, dvd = _flash_bwd(sv["qb"], sv["kd"], sv["vd"], dob, sv["lse_b"], _delta_b(dob, sv["ob"], TQ_B), TQ_B, TK_B)
    dhb, gq, gk = _post_b(dqb, dkd, dvd, sv["hb"], sm["q_norm"], sm["k_norm"], tab_q, tab_k, TM)
    g["q_norm_b"] = gq.sum(0).reshape(-1, HEAD).sum(0)
    g["k_norm_b"] = gk.sum(0).reshape(-1, HEAD).sum(0)

    band_c = _BandC(s // BAND)
    cargs = (sv["qc"], sv["kc"], sv["vc"], doc, sv["oc"], sv["lse_c"], sv["bt"])
    dqc, dbt = _band_dq(band_c, *cargs)
    dkc, dvc = _band_dkv(band_c, *cargs)
    dhc = _post_c(dqc, dkc, dvc, TM)
    g["rpb_c"] = _bias_tiles_grad(dbt, 2 * C_ROWS - 1, 2 * C_COLS - 1)

    xb = sv["xb"]
    g["w_in"] = jnp.concatenate([_mm(xb, dh, mode="tn", name="g_in_" + nm)
                                 for nm, dh in (("a", dha), ("b", dhb), ("c", dhc), ("g", dlog))], 1)
    dx = _mm(dha, w["in"], mode="nt", name="d_x_a", tk=QKV_W, b_cols=QKV_W, b_off=0, extras=(dr1,),
             epilogue=lambda acc, e: (acc + alpha * e,))
    for nm, dh, off in (("b", dhb, 1), ("c", dhc, 2), ("g", dlog, 3)):
        dx = _mm(dh, w["in"], mode="nt", name="d_x_" + nm, tk=QKV_W, b_cols=dh.shape[1], b_off=off, extras=(dx,),
                 epilogue=lambda acc, e: (acc + e,))
    return dx, g


BIG = ("w_in", "w_branch_a", "w_branch_b", "w_branch_c", "w_out", "w_up", "w_down")
ROW_SHARDED = ("w_out", "w_down")
SMALL = ("b_gate", "q_norm_b", "k_norm_b", "rpb_c", "ln1_g", "ln1_b", "ln2_g", "ln2_b")


def _local_step(x, target, wfull, small):
    s, d = x.shape
    depth = wfull["w_in"].shape[0]
    alpha = (2 * depth) ** 0.25
    tabs = _tables(s)
    ws, sms = [], []
    names = dict(w_in="in", w_branch_a="br_a", w_branch_b="br_b", w_branch_c="br_c", w_out="out", w_up="up", w_down="down")
    for l in range(depth):
        ws.append({short: (wfull[n], l) for n, short in names.items()})
        sms.append(dict(b_gate=small["b_gate"][l][None], q_norm=jnp.tile(small["q_norm_b"][l], BQ_W // HEAD)[None],
                        k_norm=jnp.tile(small["k_norm_b"][l], BKV_W // HEAD)[None], rpb=small["rpb_c"][l],
                        ln1_g=small["ln1_g"][l][None], ln1_b=small["ln1_b"][l][None],
                        ln2_g=small["ln2_g"][l][None], ln2_b=small["ln2_b"][l][None]))
    saved = []
    h, hb = x, x.astype(_MXU)
    for l in range(depth):
        h, hb, sv = _layer_fwd(h, hb, ws[l], sms[l], tabs, alpha)
        saved.append(sv)
    sq, dy = _loss_head(h, target, TM)
    grads = [None] * depth
    for l in reversed(range(depth)):
        dy, grads[l] = _layer_bwd(dy, ws[l], sms[l], saved[l], tabs, alpha)
    stacked = {k: jnp.stack([gl[k] for gl in grads]) for k in grads[0]}
    return sq, dy, stacked


def _place():
    return lax.axis_index("x"), lax.axis_index("y"), lax.axis_index("c")


def _flip(a, b):
    return a + b - 2 * a * b


def _other_chips(x, y):
    return [(1 - x, y), (x, 1 - y), (1 - x, 1 - y)]


def _gather_shards(flat):
    r = flat.shape[0]
    rh = r // 2

    def body(src, out, send_sems, recv_sems):
        x, y, c = _place()
        me, sibling = (x, y), (x, y, 1 - c)
        n1, n2, dg = (_flip(x, 1 - c), _flip(y, c)), (_flip(x, c), _flip(y, 1 - c)), (1 - x, 1 - y)

        def half(chip, hc):
            return out.at[2 * chip[0] + chip[1], pl.ds(hc * rh, rh), :]

        def copy(k, src_ref, dst_ref, to):
            return pltpu.make_async_remote_copy(src_ref=src_ref, dst_ref=dst_ref, send_sem=send_sems.at[k],
                                                recv_sem=recv_sems.at[k], device_id=to, device_id_type=MESH)

        own = src.at[pl.ds(c * rh, rh), :]
        sends = [copy(0, own, half(me, c), (*n1, c)), copy(1, own, half(me, c), (*n2, c)), copy(6, src, out.at[2 * x + y], sibling)]
        for cp in sends:
            cp.start()
        for k, chip, j in ((0, n1, c), (1, n2, 1 - c), (2, dg, 2)):
            copy(k, half(chip, c), half(chip, c), sibling).wait_recv()
            if k == 0:
                sends.append(copy(2, half(n1, c), half(n1, c), (*n2, c)))
                sends[-1].start()
            sends.append(copy(3 + j, half(chip, c), half(chip, c), sibling))
            sends[-1].start()
        for j, chip in enumerate(_other_chips(x, y)):
            copy(3 + j, half(chip, 1 - c), half(chip, 1 - c), sibling).wait_recv()
        copy(6, src, out.at[2 * x + y], sibling).wait_recv()
        for cp in sends:
            cp.wait_send()

    return pl.pallas_call(
        body, name="gather_weights", in_specs=[ANY], out_specs=ANY,
        out_shape=jax.ShapeDtypeStruct((4, r, 128), flat.dtype),
        scratch_shapes=[pltpu.SemaphoreType.DMA((7,)), pltpu.SemaphoreType.DMA((7,))],
    )(flat)


def _pair_exchange(part):
    _, r, _ = part.shape
    rh = r // 2

    def body(src, out, send_sem, recv_sem):
        x, y, c = _place()
        cp = pltpu.make_async_remote_copy(src_ref=src.at[:, pl.ds((1 - c) * rh, rh), :], dst_ref=out, send_sem=send_sem,
                                          recv_sem=recv_sem, device_id=(x, y, 1 - c), device_id_type=MESH)
        cp.start()
        cp.wait()

    return pl.pallas_call(
        body, name="grad_pair_exchange", in_specs=[ANY], out_specs=ANY,
        out_shape=jax.ShapeDtypeStruct((4, rh, 128), part.dtype),
        scratch_shapes=[pltpu.SemaphoreType.DMA, pltpu.SemaphoreType.DMA],
    )(part)


def _chip_exchange(t):
    _, rh, _ = t.shape

    def body(src, out, send_sems, recv_sems):
        x, y, c = _place()
        cps = [pltpu.make_async_remote_copy(src_ref=src.at[2 * chip[0] + chip[1]], dst_ref=out.at[k], send_sem=send_sems.at[k],
                                            recv_sem=recv_sems.at[k], device_id=(*chip, c), device_id_type=MESH)
               for k, chip in enumerate(_other_chips(x, y))]
        for cp in cps:
            cp.start()
        for cp in cps:
            cp.wait()

    return pl.pallas_call(
        body, name="grad_chip_exchange", in_specs=[ANY], out_specs=ANY,
        out_shape=jax.ShapeDtypeStruct((3, rh, 128), t.dtype),
        scratch_shapes=[pltpu.SemaphoreType.DMA((3,)), pltpu.SemaphoreType.DMA((3,))],
    )(t)


def _pair_share(half):
    rh = half.shape[0]

    def body(src, out, send_sem, recv_sem):
        x, y, c = _place()
        cp = pltpu.make_async_remote_copy(src_ref=src, dst_ref=out, send_sem=send_sem, recv_sem=recv_sem,
                                          device_id=(x, y, 1 - c), device_id_type=MESH)
        cp.start()
        cp.wait()

    theirs = pl.pallas_call(
        body, name="grad_pair_share", in_specs=[ANY], out_specs=ANY,
        out_shape=jax.ShapeDtypeStruct((rh, 128), half.dtype),
        scratch_shapes=[pltpu.SemaphoreType.DMA, pltpu.SemaphoreType.DMA],
    )(half)

    tr = _tile(rh, 2048, 8)

    def join(c_ref, mine_ref, theirs_ref, o_ref):
        o_ref[...] = jnp.where(pl.program_id(0) == c_ref[0], mine_ref[...], theirs_ref[...])

    spec = pl.BlockSpec((tr, 128), lambda h, i, c_ref: (i, 0))
    c = jnp.reshape(lax.axis_index("c"), (1,)).astype(jnp.int32)
    return pl.pallas_call(
        join, name="grad_pair_join",
        grid_spec=pltpu.PrefetchScalarGridSpec(
            num_scalar_prefetch=1, grid=(2, rh // tr), in_specs=[spec, spec],
            out_specs=pl.BlockSpec((tr, 128), lambda h, i, c_ref: (h * (rh // tr) + i, 0))),
        out_shape=jax.ShapeDtypeStruct((2 * rh, 128), half.dtype),
        compiler_params=_params("parallel", "parallel"),
    )(c, half, theirs)


def _gather_all(v):
    r = v.shape[0]

    def body(src, out, send_sems, recv_sems, local_sem):
        x, y, c = _place()
        me = 4 * x + 2 * y + c
        mine = pltpu.make_async_copy(src, out.at[me], local_sem)
        mine.start()
        cps = []
        for k in range(1, 8):
            fx, fy, fc = (k >> 2) & 1, (k >> 1) & 1, k & 1
            peer = (x + fx - 2 * x * fx, y + fy - 2 * y * fy, c + fc - 2 * c * fc)
            cps.append(pltpu.make_async_remote_copy(src_ref=src, dst_ref=out.at[me], send_sem=send_sems.at[k - 1],
                                                    recv_sem=recv_sems.at[k - 1], device_id=peer, device_id_type=MESH))
        for cp in cps:
            cp.start()
        for k in range(1, 8):
            fx, fy, fc = (k >> 2) & 1, (k >> 1) & 1, k & 1
            frm = 4 * (x + fx - 2 * x * fx) + 2 * (y + fy - 2 * y * fy) + (c + fc - 2 * c * fc)
            pltpu.make_async_remote_copy(src_ref=src, dst_ref=out.at[frm], send_sem=send_sems.at[k - 1],
                                         recv_sem=recv_sems.at[k - 1], device_id=(x, y, c), device_id_type=MESH).wait_recv()
        for cp in cps:
            cp.wait_send()
        mine.wait()

    return pl.pallas_call(
        body, name="gather_small_grads", in_specs=[ANY], out_specs=ANY,
        out_shape=jax.ShapeDtypeStruct((8, r, 128), v.dtype),
        scratch_shapes=[pltpu.SemaphoreType.DMA((7,)), pltpu.SemaphoreType.DMA((7,)), pltpu.SemaphoreType.DMA],
    )(v)


def _sum_slots(parts, name):
    n, r, _ = parts.shape
    tr = _tile(r, 1024, 8)

    def body(p_ref, o_ref):
        acc = p_ref[0]
        for j in range(1, n):
            acc = acc + p_ref[j]
        o_ref[...] = acc

    return pl.pallas_call(
        body, name=name, grid=(r // tr,), in_specs=[pl.BlockSpec((n, tr, 128), lambda i: (0, i, 0))],
        out_specs=pl.BlockSpec((tr, 128), lambda i: (i, 0)), out_shape=jax.ShapeDtypeStruct((r, 128), parts.dtype),
        compiler_params=_params("parallel"),
    )(parts)


def _add_sibling_half(part, recv, c):
    _, rh, _ = recv.shape
    tr = _tile(rh, 1024, 8)
    nblk = rh // tr

    def body(c_ref, p_ref, r_ref, o_ref):
        o_ref[...] = (p_ref[...].astype(F32) + r_ref[...].astype(F32)).astype(o_ref.dtype)

    return pl.pallas_call(
        body, name="grad_pair_sum",
        grid_spec=pltpu.PrefetchScalarGridSpec(
            num_scalar_prefetch=1, grid=(4, nblk),
            in_specs=[pl.BlockSpec((None, tr, 128), lambda j, i, c_ref: (j, c_ref[0] * nblk + i, 0)),
                      pl.BlockSpec((None, tr, 128), lambda j, i, c_ref: (j, i, 0))],
            out_specs=pl.BlockSpec((None, tr, 128), lambda j, i, c_ref: (j, i, 0))),
        out_shape=jax.ShapeDtypeStruct(recv.shape, recv.dtype),
        compiler_params=_params("parallel", "parallel"),
    )(c, part, recv)


def _add_chips(t, recv, me):
    _, rh, _ = t.shape
    tr = _tile(rh, 1024, 8)

    def body(me_ref, t_ref, r_ref, o_ref):
        f = lambda v: v.astype(F32)
        o_ref[...] = ((f(t_ref[...]) + f(r_ref[0])) + f(r_ref[1])) + f(r_ref[2])

    return pl.pallas_call(
        body, name="grad_chip_sum",
        grid_spec=pltpu.PrefetchScalarGridSpec(
            num_scalar_prefetch=1, grid=(rh // tr,),
            in_specs=[pl.BlockSpec((None, tr, 128), lambda i, me_ref: (me_ref[0], i, 0)),
                      pl.BlockSpec((3, tr, 128), lambda i, me_ref: (0, i, 0))],
            out_specs=pl.BlockSpec((tr, 128), lambda i, me_ref: (i, 0))),
        out_shape=jax.ShapeDtypeStruct((rh, 128), F32),
        compiler_params=_params("parallel"),
    )(me, t, recv)


def _reduce_scatter(part):
    x, y, c = _place()
    t = _add_sibling_half(part, _pair_exchange(part), jnp.reshape(c, (1,)).astype(jnp.int32))
    half = _add_chips(t, _chip_exchange(t), jnp.reshape(2 * x + y, (1,)).astype(jnp.int32))
    return _pair_share(half)


def _to_rows(parts, mult):
    flat = jnp.concatenate([p.reshape(-1) for p in parts])
    flat = jnp.pad(flat, (0, (-flat.size) % (128 * mult)))
    return flat.reshape(-1, 128)


def _from_rows(rows, shapes):
    flat, out, at = rows.reshape(-1), [], 0
    for shp in shapes:
        n = int(np.prod(shp))
        out.append(flat[at:at + n].reshape(shp))
        at += n
    return out


def _full_from_shards(g, name, shard_shape):
    depth = shard_shape[0]
    if name in ROW_SHARDED:
        return jnp.moveaxis(g, 0, 1).reshape(depth, 4 * shard_shape[1], shard_shape[2])
    return jnp.moveaxis(g, 0, 2).reshape(depth, shard_shape[1], 4 * shard_shape[2])


def _shards_from_full(full, name):
    depth, rows, cols = full.shape
    if name in ROW_SHARDED:
        return jnp.moveaxis(full.reshape(depth, 4, rows // 4, cols), 1, 0)
    return jnp.moveaxis(full.reshape(depth, rows, 4, cols // 4), 2, 0)


def kernel(x, w_in, b_gate, q_norm_b, k_norm_b, rpb_c, w_branch_a, w_branch_b, w_branch_c, w_out, ln1_g, ln1_b, w_up, w_down, ln2_g, ln2_b, loss_target, m_w_in, m_b_gate, m_q_norm_b, m_k_norm_b, m_rpb_c, m_w_branch_a, m_w_branch_b, m_w_branch_c, m_w_out, m_ln1_g, m_ln1_b, m_w_up, m_w_down, m_ln2_g, m_ln2_b, v_w_in, v_b_gate, v_q_norm_b, v_k_norm_b, v_rpb_c, v_w_branch_a, v_w_branch_b, v_w_branch_c, v_w_out, v_ln1_g, v_ln1_b, v_w_up, v_w_down, v_ln2_g, v_ln2_b):
    args = dict(locals())
    big_shard = {n: args[n] for n in BIG}
    small = {n: args[n] for n in SMALL}
    shapes = [big_shard[n].shape for n in BIG]

    flat = _to_rows([big_shard[n].astype(_MXU) for n in BIG], 32)
    gathered = _gather_shards(flat)
    per_chip = [_from_rows(gathered[j], shapes) for j in range(4)]
    wfull = {n: _full_from_shards(jnp.stack([per_chip[j][i] for j in range(4)]), n, shapes[i]) for i, n in enumerate(BIG)}

    sq, grad_x, grads = _local_step(x[0], loss_target[0], wfull, small)
    loss = lax.psum(0.5 * jnp.sum(sq) / x.shape[-1], ("x", "y", "c"))

    part = jnp.stack([_to_rows([_shards_from_full(grads[n], n)[j].astype(_MXU) for n in BIG], 32) for j in range(4)])
    g_big = _from_rows(_reduce_scatter(part), shapes)
    small_shapes = [small[n].shape for n in SMALL]
    g_small = _from_rows(_sum_slots(_gather_all(_to_rows([grads[n] for n in SMALL], 8)), "small_grad_sum"), small_shapes)
    grad = dict(zip(BIG, g_big))
    grad.update(zip(SMALL, g_small))

    delta, new_m, new_v = {}, {}, {}
    for n in BIG:
        shp = big_shard[n].shape
        two_d = lambda t: t.reshape(-1, shp[-1])
        res = _adamw(two_d(big_shard[n]), two_d(grad[n]), two_d(args["m_" + n]), two_d(args["v_" + n]), "adamw_" + n)
        delta[n], new_m[n], new_v[n] = [t.reshape(shp) for t in res]
    packed = [_to_rows([args[pre + n] for n in SMALL], 8) for pre in ("", "m_", "v_")]
    res = _adamw(packed[0], _to_rows([grad[n] for n in SMALL], 8), packed[1], packed[2], "adamw_small")
    for dst, rows in zip((delta, new_m, new_v), res):
        dst.update(zip(SMALL, _from_rows(rows, small_shapes)))

    order = ("w_in", "b_gate", "q_norm_b", "k_norm_b", "rpb_c", "w_branch_a", "w_branch_b", "w_branch_c", "w_out",
             "ln1_g", "ln1_b", "w_up", "w_down", "ln2_g", "ln2_b")
    return (loss, grad_x[None], *[grad[n] for n in order], *[delta[n] for n in order],
            *[new_m[n] for n in order], *[new_v[n] for n in order])
```

```python
import functools

import numpy as np
import jax
import jax.numpy as jnp
from jax import lax
from jax.experimental import pallas as pl
from jax.experimental.pallas import tpu as pltpu

F32 = jnp.float32
_MXU = jnp.bfloat16

HEAD = 64
A_W, BQ_W, BKV_W, C_W = 256, 512, 128, 256
QKV_W = 768
A_DILATIONS = (1, 4, 16)
A_RADIUS = 64
A_ROPE_HALF = 8
AX_ROPE_HALF = 16
ROPE_THETA = 500000.0
AX_THETA = 10000.0
GRID_W = 64
C_ROWS = 8
C_COLS = 16
BAND = 128
BT_TILES = 18
LN_EPS = 1e-5
RMS_EPS = 1e-6
NEG = -1e30
SCALE = HEAD ** -0.5
LOG2E = 1.4426950408889634
LN2 = 0.6931471805599453
ADAM_LR, ADAM_B1, ADAM_B2, ADAM_EPS, ADAM_WD, ADAM_STEP = 0.001, 0.9, 0.999, 1e-08, 0.01, 10
V7X_VMEM_LIMIT = 48 * 1024 * 1024
MESH = pl.DeviceIdType.MESH
ANY = pl.BlockSpec(memory_space=pl.ANY)


def _params(*sem):
    return pltpu.CompilerParams(dimension_semantics=sem or None, vmem_limit_bytes=V7X_VMEM_LIMIT)


def _tile(n, pref, align=128):
    if n <= pref:
        return n
    t = (pref // align) * align
    while t >= align:
        if n % t == 0:
            return t
        t -= align
    return n


def _mm(a, b, *, name, mode="nn", outs=((F32),), epilogue=None, extras=(), tm=1024, tn=1024, tk=2048, exact=False,
        b_cols=None, b_off=0):
    b, b_lead = b if isinstance(b, tuple) else (b, None)
    m, k = a.shape if mode != "tn" else a.shape[::-1]
    b_rows, b_last = b.shape[-2], (b_cols or b.shape[-1])
    k2, n = (b_rows, b_last) if mode != "nt" else (b_last, b_rows)
    assert k == k2, (a.shape, b.shape, mode)
    tm, tn, tk = _tile(m, tm), _tile(n, tn), _tile(k, tk)
    nk = k // tk
    n_ex, n_out = len(extras), len(outs)
    mx = F32 if exact else _MXU
    prec = lax.Precision.HIGHEST if exact else None
    dims = {"nn": (((1,), (0,)), ((), ())), "nt": (((1,), (1,)), ((), ())), "tn": (((0,), (0,)), ((), ()))}[mode]

    def body(*refs):
        a_ref, b_ref = refs[0], refs[1]
        ex = refs[2:2 + n_ex]
        out_refs = refs[2 + n_ex:2 + n_ex + n_out]
        kk = pl.program_id(2)
        av, bv = a_ref[...].astype(mx), b_ref[...].astype(mx)
        part = lax.dot_general(av, bv, dims, preferred_element_type=F32, precision=prec)

        def finish(res):
            vals = epilogue(res, *[e[...] for e in ex]) if epilogue is not None else (res,)
            for o, v in zip(out_refs, vals):
                o[...] = v.astype(o.dtype)

        if nk == 1:
            finish(part)
        else:
            acc = refs[-1]

            @pl.when(kk == 0)
            def _():
                acc[...] = part

            @pl.when((kk > 0) & (kk < nk - 1))
            def _():
                acc[...] += part

            @pl.when(kk == nk - 1)
            def _():
                finish(acc[...] + part)

    a_spec = pl.BlockSpec((tm, tk), lambda i, j, kk: (i, kk)) if mode != "tn" else pl.BlockSpec((tk, tm), lambda i, j, kk: (kk, i))
    lead = () if b_lead is None else (None,)
    at = () if b_lead is None else (b_lead,)
    if mode == "nt":
        b_spec = pl.BlockSpec(lead + (tn, tk), lambda i, j, kk: at + (j, kk + b_off))
    else:
        b_spec = pl.BlockSpec(lead + (tk, tn), lambda i, j, kk: at + (kk, j + b_off))
    o_spec = pl.BlockSpec((tm, tn), lambda i, j, kk: (i, j))
    res = pl.pallas_call(
        body, name=name, grid=(m // tm, n // tn, nk),
        in_specs=[a_spec, b_spec] + [o_spec if e.shape[0] > 1 else pl.BlockSpec((1, tn), lambda i, j, kk: (0, j)) for e in extras],
        out_specs=[o_spec] * n_out,
        out_shape=[jax.ShapeDtypeStruct((m, n), d) for d in outs],
        scratch_shapes=[pltpu.VMEM((tm, tn), F32)] if nk > 1 else [],
        compiler_params=_params("parallel", "parallel", "arbitrary"),
    )(a, b, *extras)
    return res[0] if n_out == 1 else res


def _rows(tm, width, cb=0):
    return pl.BlockSpec((tm, width), lambda t: (t, cb))


def _whole(arr):
    nd = arr.ndim
    return pl.BlockSpec(arr.shape, lambda t: (0,) * nd)


def _rowwise(fn, name, rows, tm, ins, outs):
    n_in, n_out = len(ins), len(outs)
    dil_in = [spec[1:] if isinstance(spec, tuple) else None for _, spec in ins]
    in_specs = [_rows(tm // spec[1], spec[1] * spec[2]) if isinstance(spec, tuple) else spec for _, spec in ins]
    scratch = [pltpu.VMEM((di[1] // 128, tm, 128), F32) for di in dil_in if di] + \
              [pltpu.VMEM((n // 128, tm, 128), F32) for n, _, kind in outs if isinstance(kind, int)]

    def body(*refs):
        scr = list(refs[n_in + n_out:])
        blocks = []
        for r, di in zip(refs[:n_in], dil_in):
            if di is None:
                blocks.append(r[...])
            else:
                d, n = di
                s_ref = scr.pop(0)
                for j in range(d):
                    for b in range(n // 128):
                        lanes = slice(j * n + b * 128, j * n + (b + 1) * 128)
                        s_ref.at[b][pl.ds(j, tm // d, stride=d), :] = r[:, lanes].astype(F32)
                blocks.append(jnp.concatenate([s_ref[b] for b in range(n // 128)], 1))
        vals = fn(*blocks)
        first = pl.program_id(0) == 0
        for (ncols, _, kind), o, v in zip(outs, refs[n_in:n_in + n_out], vals):
            if kind == "row":
                o[...] = v.astype(o.dtype)
            elif isinstance(kind, int):
                s_ref = scr.pop(0)
                for b in range(ncols // 128):
                    s_ref[b] = v[:, b * 128:(b + 1) * 128].astype(F32)
                for j in range(kind):
                    for b in range(ncols // 128):
                        lanes = slice(j * ncols + b * 128, j * ncols + (b + 1) * 128)
                        o[:, lanes] = s_ref.at[b][pl.ds(j, tm // kind, stride=kind), :].astype(o.dtype)
            else:
                part = v.reshape(tm // 8, 8, ncols).sum(0)

                @pl.when(first)
                def _(o=o, part=part):
                    o[...] = part

                @pl.when(jnp.logical_not(first))
                def _(o=o, part=part):
                    o[...] += part

    def out_spec(n, kind):
        if kind == "row":
            return _rows(tm, n), (rows, n)
        if isinstance(kind, int):
            return _rows(tm // kind, kind * n), (rows // kind, kind * n)
        return pl.BlockSpec((8, n), lambda t: (0, 0)), (8, n)

    specs = [out_spec(n, kind) for n, _, kind in outs]
    res = pl.pallas_call(
        body, name=name, grid=(rows // tm,),
        in_specs=in_specs, out_specs=[s for s, _ in specs],
        out_shape=[jax.ShapeDtypeStruct(shp, d) for (_, shp), (_, d, _) in zip(specs, outs)],
        scratch_shapes=scratch, compiler_params=_params("arbitrary"),
    )(*[a for a, _ in ins])
    return res


def _lane_lo(width=128):
    return (lax.broadcasted_iota(jnp.int32, (1, width), 1) & (HEAD * 2 - 1)) < HEAD


def _group_sum(x):
    w = x.shape[-1]
    sh = HEAD.bit_length() - 1
    same = (lax.broadcasted_iota(jnp.int32, (w, w), 0) >> sh) == (lax.broadcasted_iota(jnp.int32, (w, w), 1) >> sh)
    ones = jnp.where(same, 1.0, 0.0).astype(jnp.bfloat16)
    hi = x.astype(jnp.bfloat16)
    lo = (x - hi.astype(F32)).astype(jnp.bfloat16)
    return jnp.dot(hi, ones, preferred_element_type=F32) + jnp.dot(lo, ones, preferred_element_type=F32)


def _rot(x, c, sm, sp, shift):
    w = x.shape[-1]
    return x * c + pltpu.roll(x, w - shift, 1) * sm + pltpu.roll(x, shift, 1) * sp


def _rot_t(dy, c, sm, sp, shift):
    w = dy.shape[-1]
    return dy * c + pltpu.roll(dy * sm, shift, 1) + pltpu.roll(dy * sp, w - shift, 1)


def _rope_tables(pos_parts, half, thetas):
    cs, sms, sps = [], [], []
    for pos, theta in zip(pos_parts, thetas):
        inv = theta ** (-jnp.arange(half, dtype=F32) / half)
        ang = pos.astype(F32)[:, None] * inv[None, :]
        co, si, ze = jnp.cos(ang), jnp.sin(ang), jnp.zeros_like(ang)
        cs += [co, co]
        sms += [-si, ze]
        sps += [ze, si]
    return [jnp.concatenate(t, axis=1) for t in (cs, sms, sps)]


def _tables(s):
    pos = jnp.arange(s)
    ca, sma, spa = _rope_tables([pos], A_ROPE_HALF, [ROPE_THETA])
    pad = HEAD - 2 * A_ROPE_HALF
    ca = jnp.concatenate([ca, jnp.ones((s, pad), F32)], 1)
    sma, spa = [jnp.concatenate([t, jnp.zeros((s, pad), F32)], 1) for t in (sma, spa)]
    tab_a = [jnp.tile(t, (1, A_W // HEAD)) for t in (ca, sma, spa)]
    ax = _rope_tables([pos // GRID_W, pos % GRID_W], AX_ROPE_HALF, [AX_THETA, AX_THETA])
    tab_q = [jnp.tile(t, (1, BQ_W // HEAD)) for t in ax]
    tab_k = [jnp.tile(t, (1, BKV_W // HEAD)) for t in ax]
    return tab_a, tab_q, tab_k


def _prep_a(ha, tab, tm):
    s = ha.shape[0]

    def fn(h, c, sm, sp):
        q, k, v = h[:, :A_W], h[:, A_W:2 * A_W], h[:, 2 * A_W:]
        q, k = _rot(q, c, sm, sp, A_ROPE_HALF) * SCALE, _rot(k, c, sm, sp, A_ROPE_HALF)
        return [t for t in (q, k, v) for _ in A_DILATIONS]

    res = _rowwise(fn, "prep_a", s, tm, [(ha, _rows(tm, QKV_W))] + [(t, _rows(tm, A_W)) for t in tab],
                   [(A_W, _MXU, _dil_kind(d)) for _ in range(3) for d in A_DILATIONS])
    n = len(A_DILATIONS)
    return [dict(zip(A_DILATIONS, res[i * n:(i + 1) * n])) for i in range(3)]


def _dil_kind(d):
    return "row" if d == 1 else d


def _dil_spec(d, tm, ncols):
    return _rows(tm, ncols) if d == 1 else ("dil", d, ncols)


def _to_dilations(x, name, tm):
    s, n = x.shape
    res = _rowwise(lambda v: [v for d in A_DILATIONS if d > 1], name, s, tm, [(x, _rows(tm, n))],
                   [(n, x.dtype, d) for d in A_DILATIONS if d > 1])
    return {1: x, **dict(zip([d for d in A_DILATIONS if d > 1], res))}


def _rms(x, g):
    ms = _group_sum(x * x) * (1.0 / HEAD)
    return x * lax.rsqrt(ms + RMS_EPS) * g


def _prep_b(hb, gq, gk, tab_q, tab_k, tm):
    s = hb.shape[0]

    def fn(h, gq, gk, cq, smq, spq, ck, smk, spk):
        xq, xk, v = h[:, :BQ_W], h[:, BQ_W:BQ_W + BKV_W], h[:, BQ_W + BKV_W:]
        q = _rot(_rms(xq, gq), cq, smq, spq, AX_ROPE_HALF) * (SCALE * LOG2E)
        k = _rot(_rms(xk, gk), ck, smk, spk, AX_ROPE_HALF)
        lo = _lane_lo()
        kr, vr = pltpu.roll(k, HEAD, 1), pltpu.roll(v, HEAD, 1)
        kd = jnp.concatenate([jnp.where(lo, k, kr), jnp.where(lo, kr, k)], 1)
        vd = jnp.concatenate([jnp.where(lo, v, vr), jnp.where(lo, vr, v)], 1)
        v1 = jnp.concatenate([jnp.where(lo, v, 1.0), jnp.where(lo, vr, 1.0)], 1)
        return q, kd, vd, v1

    ins = [(hb, _rows(tm, QKV_W)), (gq, _whole(gq)), (gk, _whole(gk))]
    ins += [(t, _rows(tm, BQ_W)) for t in tab_q] + [(t, _rows(tm, BKV_W)) for t in tab_k]
    return _rowwise(fn, "prep_b", s, tm, ins, [(BQ_W, _MXU, "row")] + [(2 * BKV_W, _MXU, "row")] * 3)


def _prep_c(hc, tm):
    def fn(h):
        return h[:, :C_W] * SCALE, h[:, C_W:2 * C_W], h[:, 2 * C_W:]

    return _rowwise(fn, "prep_c", hc.shape[0], tm, [(hc, _rows(tm, QKV_W))], [(C_W, _MXU, "row")] * 3)


def _combine_a(os_, ms, ls, s, tm):
    def fn(o1, o2, o3, m1, m2, m3, l1, l2, l3):
        lo = _lane_lo()
        outs, lses = [], []
        for p in range(A_W // 128):
            st = slice(p * 256, (p + 1) * 256)
            mm = [m[:, st] for m in (m1, m2, m3)]
            ll = [l[:, st] for l in (l1, l2, l3)]
            mmax = jnp.maximum(jnp.maximum(mm[0], mm[1]), mm[2])
            ws = [jnp.exp(m - mmax) for m in mm]
            den = ws[0] * ll[0] + ws[1] * ll[1] + ws[2] * ll[2]
            lses.append(mmax + jnp.log(den))
            num = sum(jnp.where(lo, w[:, :128], w[:, 128:]) * o[:, p * 128:(p + 1) * 128] for w, o in zip(ws, (o1, o2, o3)))
            outs.append(num / jnp.where(lo, den[:, :128], den[:, 128:]))
        o, lse = jnp.concatenate(outs, 1), jnp.concatenate(lses, 1)
        return [o] * len(A_DILATIONS) + [lse] * len(A_DILATIONS)

    ins = [(t, _dil_spec(d, tm, w)) for ts, w in ((os_, A_W), (ms, 2 * A_W), (ls, 2 * A_W)) for t, d in zip(ts, A_DILATIONS)]
    res = _rowwise(fn, "combine_a", s, tm, ins,
                   [(w, F32, _dil_kind(d)) for w in (A_W, 2 * A_W) for d in A_DILATIONS])
    n = len(A_DILATIONS)
    return dict(zip(A_DILATIONS, res[:n])), dict(zip(A_DILATIONS, res[n:]))


def _gates(hg, bg, d):
    return [jax.nn.sigmoid(hg[:, i * d:(i + 1) * d] + bg[:, i * d:(i + 1) * d]) for i in range(3)]


def _gate_merge(hg, bg, pa, pb, pc, tm):
    s, d = pa.shape

    def fn(hg, bg, pa, pb, pc):
        g = _gates(hg, bg, d)
        return (g[0] * pa + g[1] * pb + g[2] * pc,)

    ins = [(hg, _rows(tm, 3 * d)), (bg, _whole(bg))] + [(p, _rows(tm, d)) for p in (pa, pb, pc)]
    return _rowwise(fn, "gate_merge", s, tm, ins, [(d, _MXU, "row")])[0]


def _gate_bwd(dm, hg, bg, pa, pb, pc, tm):
    s, d = pa.shape

    def fn(dm, hg, bg, pa, pb, pc):
        g = _gates(hg, bg, d)
        dlog = jnp.concatenate([dm * p * gi * (1.0 - gi) for p, gi in zip((pa, pb, pc), g)], 1)
        return dm * g[0], dm * g[1], dm * g[2], dlog, dlog

    ins = [(dm, _rows(tm, d)), (hg, _rows(tm, 3 * d)), (bg, _whole(bg))] + [(p, _rows(tm, d)) for p in (pa, pb, pc)]
    return _rowwise(fn, "gate_bwd", s, tm, ins, [(d, _MXU, "row")] * 3 + [(3 * d, _MXU, "row"), (3 * d, F32, "acc")])


def _ln_stats(r):
    mu = jnp.mean(r, -1, keepdims=True)
    xc = r - mu
    var = jnp.mean(xc * xc, -1, keepdims=True)
    rstd = lax.rsqrt(var + LN_EPS)
    return xc * rstd, rstd


def _ln_epilogue(alpha):
    def fn(br, x, g, b):
        r = alpha * x + br
        xhat, _ = _ln_stats(r)
        y = xhat * g + b
        return r, y, y

    return fn


def _ln_bwd(dy, r, g, name, tm):
    s, d = r.shape

    def fn(dy, r, g):
        xhat, rstd = _ln_stats(r)
        dxh = dy * g
        dr = rstd * (dxh - jnp.mean(dxh, -1, keepdims=True) - xhat * jnp.mean(dxh * xhat, -1, keepdims=True))
        return dr, dr, dy * xhat, dy

    ins = [(dy, _rows(tm, d)), (r, _rows(tm, d)), (g, _whole(g))]
    return _rowwise(fn, name, s, tm, ins, [(d, F32, "row"), (d, _MXU, "row"), (d, F32, "acc"), (d, F32, "acc")])


def _loss_head(y, target, tm):
    s, d = y.shape

    def fn(y, t):
        diff = y - t
        return diff * diff, diff * (1.0 / d)

    sq, dy = _rowwise(fn, "loss_head", s, tm, [(y, _rows(tm, d)), (target, _rows(tm, d))], [(d, F32, "acc"), (d, F32, "row")])
    return sq, dy


def _post_a(dqs, dks, dvs, tab, s, tm):
    def fn(q1, q2, q3, k1, k2, k3, v1, v2, v3, c, sm, sp):
        dq = _rot_t((q1 + q2 + q3) * SCALE, c, sm, sp, A_ROPE_HALF)
        dk = _rot_t(k1 + k2 + k3, c, sm, sp, A_ROPE_HALF)
        return (jnp.concatenate([dq, dk, v1 + v2 + v3], 1),)

    ins = [(t, _dil_spec(d, tm, A_W)) for ts in (dqs, dks, dvs) for t, d in zip(ts, A_DILATIONS)]
    ins += [(t, _rows(tm, A_W)) for t in tab]
    return _rowwise(fn, "post_a", s, tm, ins, [(QKV_W, _MXU, "row")])[0]


def _post_b(dq, dkd, dvd, hb, gq, gk, tab_q, tab_k, tm):
    s = dq.shape[0]

    def back(dz, x, g, c, sm, sp):
        dy = _rot_t(dz, c, sm, sp, AX_ROPE_HALF)
        rstd = lax.rsqrt(_group_sum(x * x) * (1.0 / HEAD) + RMS_EPS)
        xh = x * rstd
        dxh = dy * g
        return rstd * (dxh - xh * (_group_sum(dxh * xh) * (1.0 / HEAD))), dy * xh

    def fn(dq, dkd, dvd, h, gq, gk, cq, smq, spq, ck, smk, spk):
        lo = _lane_lo()
        dk = jnp.where(lo, dkd[:, :128], dkd[:, 128:])
        dv = jnp.where(lo, dvd[:, :128], dvd[:, 128:])
        dxq, dgq = back(dq * SCALE, h[:, :BQ_W], gq, cq, smq, spq)
        dxk, dgk = back(dk, h[:, BQ_W:BQ_W + BKV_W], gk, ck, smk, spk)
        return jnp.concatenate([dxq, dxk, dv], 1), dgq, dgk

    ins = [(dq, _rows(tm, BQ_W)), (dkd, _rows(tm, 2 * BKV_W)), (dvd, _rows(tm, 2 * BKV_W)), (hb, _rows(tm, QKV_W)),
           (gq, _whole(gq)), (gk, _whole(gk))]
    ins += [(t, _rows(tm, BQ_W)) for t in tab_q] + [(t, _rows(tm, BKV_W)) for t in tab_k]
    return _rowwise(fn, "post_b", s, tm, ins, [(QKV_W, _MXU, "row"), (BQ_W, F32, "acc"), (BKV_W, F32, "acc")])


def _post_c(dq, dk, dv, tm):
    def fn(dq, dk, dv):
        return (jnp.concatenate([dq * SCALE, dk, dv], 1),)

    return _rowwise(fn, "post_c", dq.shape[0], tm, [(t, _rows(tm, C_W)) for t in (dq, dk, dv)], [(QKV_W, _MXU, "row")])[0]


def _adamw(w, g, m, v, name):
    rows, cols = w.shape
    tm = _tile(rows, 256, 8)

    def fn(w, g, m, v):
        m = ADAM_B1 * m + (1.0 - ADAM_B1) * g
        v = ADAM_B2 * v + (1.0 - ADAM_B2) * (g * g)
        m_hat = m / (1.0 - ADAM_B1 ** ADAM_STEP)
        v_hat = v / (1.0 - ADAM_B2 ** ADAM_STEP)
        delta = -ADAM_LR * (m_hat / (jnp.sqrt(v_hat) + ADAM_EPS) + ADAM_WD * w)
        return delta, m, v

    return _rowwise(fn, name, rows, tm, [(t, _rows(tm, cols)) for t in (w, g, m, v)], [(cols, F32, "row")] * 3)


def _dot_t(a, b):
    return lax.dot_general(a, b, (((1,), (1,)), ((), ())), preferred_element_type=F32)


def _tdot(a, b):
    return lax.dot_general(a, b, (((0,), (0,)), ((), ())), preferred_element_type=F32)


def _head_masks():
    lo = _lane_lo()
    return lo, (lo, jnp.logical_not(lo))


def _rep(x, rows):
    return jnp.broadcast_to(x, (rows, 128))


def _row_lo():
    return lax.broadcasted_iota(jnp.int32, (128, 1), 0) < HEAD


def _flash_fwd(q, kd, v1, tq, tk):
    s = q.shape[0]
    tq, tk = _tile(s, tq), _tile(s, tk)
    nk = s // tk
    mx = _MXU

    def body(q_ref, k_ref, v_ref, o_ref, lse_ref, m_ref, acc_ref):
        kk = pl.program_id(2)

        @pl.when(kk == 0)
        def _():
            m_ref[...] = jnp.full_like(m_ref, NEG)
            acc_ref[...] = jnp.zeros_like(acc_ref)

        q2, k2, v2 = q_ref[...], k_ref[...], v_ref[...]
        _, masks = _head_masks()
        for h in range(2):
            st = _dot_t(k2, jnp.where(masks[h], q2, jnp.zeros_like(q2)))
            m_prev = m_ref[h]
            m_new = jnp.maximum(m_prev, jnp.max(st, 0, keepdims=True))
            p = jnp.exp2(st - m_new)
            m_ref[h] = m_new
            acc_ref[h] = acc_ref[h] * jnp.exp2(m_prev - m_new) + _tdot(v2, p.astype(mx))

        @pl.when(kk == nk - 1)
        def _():
            a0, a1 = acc_ref[0], acc_ref[1]
            l0, l1 = a0[HEAD:HEAD + 1], a1[HEAD:HEAD + 1]
            o_ref[...] = jnp.concatenate([a0[:HEAD] / l0, a1[:HEAD] / l1], 0).T
            lse_ref[...] = jnp.concatenate([m_ref[0] + jnp.log2(l0), m_ref[1] + jnp.log2(l1), jnp.zeros((6, tq), F32)], 0)

    return pl.pallas_call(
        body, name="attn_b_fwd", grid=(BQ_W // 128, s // tq, nk),
        in_specs=[pl.BlockSpec((tq, 128), lambda j, i, kk: (i, j)),
                  pl.BlockSpec((tk, 128), lambda j, i, kk: (kk, j // 2)),
                  pl.BlockSpec((tk, 128), lambda j, i, kk: (kk, j // 2))],
        out_specs=[pl.BlockSpec((tq, 128), lambda j, i, kk: (i, j)), pl.BlockSpec((None, 8, tq), lambda j, i, kk: (j, 0, i))],
        out_shape=[jax.ShapeDtypeStruct((s, BQ_W), F32), jax.ShapeDtypeStruct((BQ_W // 128, 8, s), F32)],
        scratch_shapes=[pltpu.VMEM((2, 1, tq), F32), pltpu.VMEM((2, 128, tq), F32)],
        compiler_params=_params("parallel", "parallel", "arbitrary"),
    )(q, kd, v1)


def _delta_b(do, o, tq):
    s = do.shape[0]
    tq = _tile(s, tq)

    def body(do_ref, o_ref, d_ref):
        prod = do_ref[...] * o_ref[...]
        row = lax.broadcasted_iota(jnp.int32, (8, 128), 0)
        lane = lax.broadcasted_iota(jnp.int32, (8, 128), 1)
        sel = jnp.where(((row == 0) & (lane < HEAD)) | ((row == 1) & (lane >= HEAD)), 1.0, 0.0).astype(F32)
        d_ref[...] = lax.dot_general(sel, prod, (((1,), (1,)), ((), ())), preferred_element_type=F32,
                                     precision=lax.Precision.HIGHEST)

    qs = pl.BlockSpec((tq, 128), lambda j, i: (i, j))
    return pl.pallas_call(
        body, name="attn_b_delta", grid=(BQ_W // 128, s // tq), in_specs=[qs, qs],
        out_specs=pl.BlockSpec((None, 8, tq), lambda j, i: (j, 0, i)),
        out_shape=jax.ShapeDtypeStruct((BQ_W // 128, 8, s), F32),
        compiler_params=_params("parallel", "parallel"),
    )(do, o)


def _flash_bwd(q, kd, vd, do, lse, delta, tq, tk):
    s = q.shape[0]
    tq, tk = _tile(s, tq), _tile(s, tk)
    nq, nk = s // tq, s // tk
    group = BQ_W // 128 // 2
    mx = _MXU

    def body(k_ref, v_ref, q_ref, do_ref, lse_ref, dl_ref, dq_hbm, dk_ref, dv_ref, dk_acc, dv_acc, dqt, stage, sem):
        e, kk, jj, i = pl.program_id(0), pl.program_id(1), pl.program_id(2), pl.program_id(3)

        @pl.when((jj == 0) & (i == 0))
        def _():
            dk_acc[...] = jnp.zeros_like(dk_acc)
            dv_acc[...] = jnp.zeros_like(dv_acc)

        @pl.when(kk == 0)
        def _():
            dqt[jj, i] = jnp.zeros((128, tq), F32)

        q2, k2, v2, do2 = q_ref[...], k_ref[...], v_ref[...], do_ref[...].astype(mx)
        lse8, dl8 = lse_ref[...], dl_ref[...]
        _, masks = _head_masks()
        dqs = []
        for h in range(2):
            qh = jnp.where(masks[h], q2, jnp.zeros_like(q2))
            doh = jnp.where(masks[h], do2, jnp.zeros_like(do2))
            p = jnp.exp2(_dot_t(k2, qh) - lse8[h:h + 1])
            ds = p * (_dot_t(v2, doh) - dl8[h:h + 1])
            p, ds = p.astype(mx), ds.astype(mx)
            dv_acc[...] += jnp.dot(p, doh, preferred_element_type=F32)
            dk_acc[...] += jnp.dot(ds, qh, preferred_element_type=F32)
            dqs.append(_tdot(k2, ds))
        dqt[jj, i] += jnp.where(_row_lo(), dqs[0], dqs[1])

        @pl.when(kk == nk - 1)
        def _():
            stage[...] = dqt[jj, i].T
            lane0 = pl.multiple_of((group * e + jj) * 128, 128)
            cp = pltpu.make_async_copy(stage, dq_hbm.at[pl.ds(pl.multiple_of(i * tq, tq), tq), pl.ds(lane0, 128)], sem)
            cp.start()
            cp.wait()

        @pl.when((jj == group - 1) & (i == nq - 1))
        def _():
            dk_ref[...] = (dk_acc[...] + pltpu.roll(dk_acc[...], HEAD, 1)) * LN2
            dv_ref[...] = dv_acc[...] + pltpu.roll(dv_acc[...], HEAD, 1)

    ks = pl.BlockSpec((tk, 128), lambda e, kk, jj, i: (kk, e))
    qs = pl.BlockSpec((tq, 128), lambda e, kk, jj, i: (i, group * e + jj))
    st = pl.BlockSpec((None, 8, tq), lambda e, kk, jj, i: (group * e + jj, 0, i))
    return pl.pallas_call(
        body, name="attn_b_bwd", grid=(BKV_W // HEAD, nk, group, nq),
        in_specs=[ks, ks, qs, qs, st, st], out_specs=[ANY, ks, ks],
        out_shape=[jax.ShapeDtypeStruct((s, BQ_W), F32)] + [jax.ShapeDtypeStruct((s, 2 * BKV_W), F32)] * 2,
        scratch_shapes=[pltpu.VMEM((tk, 128), F32)] * 2 + [pltpu.VMEM((group, nq, 128, tq), F32), pltpu.VMEM((tq, 128), F32),
                                                          pltpu.SemaphoreType.DMA],
        compiler_params=_params("parallel", "arbitrary", "arbitrary", "arbitrary"),
    )(kd, vd, q, do, lse, delta)


def _p_and_ds(q2, k2, v2, do2, o2, lse2, masks, mask=None, bias=None):
    mx = _MXU
    out = []
    for h in range(2):
        qh = jnp.where(masks[h], q2, jnp.zeros_like(q2))
        sc = _dot_t(qh, k2)
        if bias is not None:
            sc = sc + bias[h]
        if mask is not None:
            sc = jnp.where(mask, sc, NEG)
        lse = jnp.max(lse2[:, h * 128:(h + 1) * 128], -1, keepdims=True)
        p = jnp.exp(sc - lse)
        doh = jnp.where(masks[h], do2, jnp.zeros_like(do2))
        delta = jnp.sum(doh * o2, -1, keepdims=True)
        dp = _dot_t(doh.astype(mx), v2)
        ds = p * (dp - delta)
        out.append((qh, p, ds, doh))
    return out


class _BandA:
    hb, has_bias, name = 1, False, "a"

    def __init__(self, nb):
        self.nb = nb

    def mask(self, qidx, kidx):
        n = self.nb * BAND
        return (jnp.abs(qidx - kidx) <= A_RADIUS) & (kidx >= 0) & (kidx < n) & (qidx >= 0) & (qidx < n)


class _BandC:
    hb, has_bias, name = 3, True, "c"

    def __init__(self, nb):
        self.nb = nb
        self.rows = nb * BAND // GRID_W
        per = BAND // GRID_W
        assert self.rows >= C_ROWS and (C_ROWS - 1) // per <= self.hb
        assert (self.rows - 1) // per - (self.rows - C_ROWS) // per <= self.hb

    def mask(self, qidx, kidx):
        n = self.nb * BAND
        sh = GRID_W.bit_length() - 1
        qrow, cq = qidx >> sh, qidx & (GRID_W - 1)
        krow, ck = kidx >> sh, kidx & (GRID_W - 1)
        r0 = jnp.clip(qrow - C_ROWS // 2, 0, self.rows - C_ROWS)
        c0 = jnp.clip(cq - C_COLS // 2, 0, GRID_W - C_COLS)
        ok = (qidx >= 0) & (qidx < n) & (kidx >= 0) & (kidx < n)
        return ok & (krow >= r0) & (krow < r0 + C_ROWS) & (ck >= c0) & (ck < c0 + C_COLS)


def _bias_tile(off, a):
    return (BAND // GRID_W) * off - a + (C_ROWS - 1) + 2


def _band_bias_k(band, bt_ref, h):
    per = BAND // GRID_W
    return jnp.concatenate([jnp.concatenate([bt_ref[h, _bias_tile(off, a)] for off in range(-band.hb, band.hb + 1)], 1)
                            for a in range(per)], 0)


def _band_bias_q(band, bt_ref, h):
    per = BAND // GRID_W
    return jnp.concatenate([bt_ref[h, _bias_tile(-off, a)] for off in range(-band.hb, band.hb + 1) for a in range(per)], 0)


def _band_split(nb, ncb):
    cb = max(c for c in (4, 2, 1) if ncb % c == 0)
    rb = max(r for r in (4, 2, 1) if nb % r == 0 and r * cb <= 8)
    return rb, cb


def _band_specs(band, rb, cb, nb, width):
    def edge(first):
        return pl.BlockSpec((BAND, cb * width), lambda c, i: (jnp.clip(i * rb + first, 0, nb - 1), c))

    main = pl.BlockSpec((rb * BAND, cb * width), lambda c, i: (i, c))
    return [edge(t - band.hb) for t in range(band.hb)] + [main] + [edge(rb + t) for t in range(band.hb)]


def _band_rows(band, refs, rb, r, lanes):
    hb = band.hb
    parts = []
    for b in range(r, r + 2 * hb + 1):
        if b < hb:
            parts.append(refs[b][:, lanes])
        elif b < hb + rb:
            parts.append(refs[hb][(b - hb) * BAND:(b - hb + 1) * BAND, lanes])
        else:
            parts.append(refs[b - rb + 1][:, lanes])
    return jnp.concatenate(parts, 0)


def _band_idx(band, blk, rows_of_blocks, axis):
    shape = (rows_of_blocks * BAND, 1) if axis == 0 else (1, rows_of_blocks * BAND)
    return blk * BAND + lax.broadcasted_iota(jnp.int32, shape, axis)


def _band_fwd(band, q, k, v, bt=None):
    n, w = q.shape
    nb, ncb, nband = n // BAND, w // 128, 2 * band.hb + 1
    rb, cb = _band_split(nb, ncb)
    mx = _MXU
    raw = not band.has_bias

    def body(*refs):
        q_ref, k_refs, v_refs = refs[0], refs[1:1 + nband], refs[1 + nband:1 + 2 * nband]
        rest = refs[1 + 2 * nband:]
        bt_ref = rest[0] if band.has_bias else None
        outs = rest[1:] if band.has_bias else rest
        i = pl.program_id(1)
        lo, masks = _head_masks()
        for r in range(rb):
            blk = i * rb + r
            mask = band.mask(_band_idx(band, blk, 1, 0), _band_idx(band, blk - band.hb, nband, 1))
            for c in range(cb):
                lanes, rows = slice(c * 128, (c + 1) * 128), slice(r * BAND, (r + 1) * BAND)
                q2 = q_ref[rows, lanes]
                kcat, vcat = _band_rows(band, k_refs, rb, r, lanes), _band_rows(band, v_refs, rb, r, lanes)
                os_, ms, ls = [], [], []
                for h in range(2):
                    sc = _dot_t(jnp.where(masks[h], q2, jnp.zeros_like(q2)), kcat)
                    if band.has_bias:
                        sc = sc + _band_bias_k(band, bt_ref, 2 * c + h)
                    sc = jnp.where(mask, sc, NEG)
                    m = jnp.max(sc, -1, keepdims=True)
                    p = jnp.exp(sc - m)
                    ms.append(m)
                    ls.append(jnp.sum(p, -1, keepdims=True))
                    os_.append(jnp.dot(p.astype(mx), vcat, preferred_element_type=F32))
                if raw:
                    o_ref, m_ref, l_ref = outs
                    o_ref[rows, lanes] = jnp.where(lo, os_[0], os_[1])
                    for h in range(2):
                        st_lanes = slice(c * 256 + h * 128, c * 256 + (h + 1) * 128)
                        m_ref[rows, st_lanes] = _rep(ms[h], BAND)
                        l_ref[rows, st_lanes] = _rep(ls[h], BAND)
                else:
                    o_ref, lse_ref = outs
                    o_ref[rows, lanes] = jnp.where(lo, os_[0] / ls[0], os_[1] / ls[1])
                    for h in range(2):
                        lse_ref[rows, c * 256 + h * 128:c * 256 + (h + 1) * 128] = _rep(ms[h] + jnp.log(ls[h]), BAND)

    qs = pl.BlockSpec((rb * BAND, cb * 128), lambda c, i: (i, c))
    ks = _band_specs(band, rb, cb, nb, 128)
    st = pl.BlockSpec((rb * BAND, cb * 256), lambda c, i: (i, c))
    in_specs, args = [qs] + ks + ks, [q] + [k] * nband + [v] * nband
    if band.has_bias:
        in_specs.append(pl.BlockSpec((2 * cb, BT_TILES, GRID_W, 128), lambda c, i: (c, 0, 0, 0)))
        args.append(bt)
    n_stats = 2 if raw else 1
    return pl.pallas_call(
        body, name="attn_%s_fwd" % band.name, grid=(ncb // cb, nb // rb), in_specs=in_specs,
        out_specs=[qs] + [st] * n_stats,
        out_shape=[jax.ShapeDtypeStruct((n, w), F32)] + [jax.ShapeDtypeStruct((n, 2 * w), F32)] * n_stats,
        compiler_params=_params("parallel", "arbitrary"),
    )(*args)


def _band_dq(band, q, k, v, do, o, lse, bt=None):
    n, w = q.shape
    nb, ncb, nband = n // BAND, w // 128, 2 * band.hb + 1
    rb, cb = _band_split(nb, ncb)
    mx = _MXU
    per = BAND // GRID_W

    def body(*refs):
        q_ref, k_refs, v_refs = refs[0], refs[1:1 + nband], refs[1 + nband:1 + 2 * nband]
        do_ref, o_ref, lse_ref = refs[1 + 2 * nband:4 + 2 * nband]
        rest = refs[4 + 2 * nband:]
        dq_ref = rest[1] if band.has_bias else rest[0]
        i = pl.program_id(1)
        lo, masks = _head_masks()
        if band.has_bias:
            dbt_ref = rest[2]

            @pl.when(i == 0)
            def _():
                dbt_ref[...] = jnp.zeros_like(dbt_ref)

        for r in range(rb):
            blk = i * rb + r
            mask = band.mask(_band_idx(band, blk, 1, 0), _band_idx(band, blk - band.hb, nband, 1))
            for c in range(cb):
                lanes, rows = slice(c * 128, (c + 1) * 128), slice(r * BAND, (r + 1) * BAND)
                kcat, vcat = _band_rows(band, k_refs, rb, r, lanes), _band_rows(band, v_refs, rb, r, lanes)
                bias = [_band_bias_k(band, rest[0], 2 * c + h) for h in range(2)] if band.has_bias else None
                hs = _p_and_ds(q_ref[rows, lanes], kcat, vcat, do_ref[rows, lanes], o_ref[rows, lanes],
                               lse_ref[rows, c * 256:(c + 1) * 256], masks, mask, bias)
                dqs = [jnp.dot(ds.astype(mx), kcat, preferred_element_type=F32) for _, _, ds, _ in hs]
                dq_ref[rows, lanes] = jnp.where(lo, dqs[0], dqs[1])
                if band.has_bias:
                    for h in range(2):
                        ds = hs[h][2]
                        for a in range(per):
                            for t in range(nband):
                                tile = ds[a * GRID_W:(a + 1) * GRID_W, t * 128:(t + 1) * 128]
                                dbt_ref[2 * c + h, _bias_tile(t - band.hb, a)] += tile

    qs = pl.BlockSpec((rb * BAND, cb * 128), lambda c, i: (i, c))
    ks = _band_specs(band, rb, cb, nb, 128)
    st = pl.BlockSpec((rb * BAND, cb * 256), lambda c, i: (i, c))
    in_specs, args = [qs] + ks + ks + [qs, qs, st], [q] + [k] * nband + [v] * nband + [do, o, lse]
    out_specs, out_shape = [qs], [jax.ShapeDtypeStruct((n, w), F32)]
    if band.has_bias:
        bts = pl.BlockSpec((2 * cb, BT_TILES, GRID_W, 128), lambda c, i: (c, 0, 0, 0))
        in_specs.append(bts)
        args.append(bt)
        out_specs.append(bts)
        out_shape.append(jax.ShapeDtypeStruct(bt.shape, F32))
    return pl.pallas_call(
        body, name="attn_%s_dq" % band.name, grid=(ncb // cb, nb // rb), in_specs=in_specs, out_specs=out_specs,
        out_shape=out_shape, compiler_params=_params("parallel", "arbitrary"),
    )(*args)


def _band_dkv(band, q, k, v, do, o, lse, bt=None):
    n, w = q.shape
    nb, ncb, nband = n // BAND, w // 128, 2 * band.hb + 1
    rb, cb = _band_split(nb, ncb)
    mx = _MXU

    def body(*refs):
        k_ref, v_ref = refs[0], refs[1]
        q_refs, do_refs, o_refs, lse_refs = [refs[2 + g * nband:2 + (g + 1) * nband] for g in range(4)]
        rest = refs[2 + 4 * nband:]
        dk_ref, dv_ref = rest[-2], rest[-1]
        i = pl.program_id(1)
        _, masks = _head_masks()
        for r in range(rb):
            blk = i * rb + r
            mask = band.mask(_band_idx(band, blk - band.hb, nband, 0), _band_idx(band, blk, 1, 1))
            for c in range(cb):
                lanes, rows = slice(c * 128, (c + 1) * 128), slice(r * BAND, (r + 1) * BAND)
                qcat, docat, ocat = [_band_rows(band, g, rb, r, lanes) for g in (q_refs, do_refs, o_refs)]
                lsecat = _band_rows(band, lse_refs, rb, r, slice(c * 256, (c + 1) * 256))
                bias = [_band_bias_q(band, rest[0], 2 * c + h) for h in range(2)] if band.has_bias else None
                hs = _p_and_ds(qcat, k_ref[rows, lanes], v_ref[rows, lanes], docat, ocat, lsecat, masks, mask, bias)
                dk_ref[rows, lanes] = sum(_tdot(ds.astype(mx), qh) for qh, _, ds, _ in hs)
                dv_ref[rows, lanes] = sum(_tdot(p.astype(mx), doh.astype(mx)) for _, p, _, doh in hs)

    ks = pl.BlockSpec((rb * BAND, cb * 128), lambda c, i: (i, c))
    in_specs = [ks, ks] + _band_specs(band, rb, cb, nb, 128) * 3 + _band_specs(band, rb, cb, nb, 256)
    args = [k, v] + [q] * nband + [do] * nband + [o] * nband + [lse] * nband
    if band.has_bias:
        in_specs.append(pl.BlockSpec((2 * cb, BT_TILES, GRID_W, 128), lambda c, i: (c, 0, 0, 0)))
        args.append(bt)
    return pl.pallas_call(
        body, name="attn_%s_dkv" % band.name, grid=(ncb // cb, nb // rb), in_specs=in_specs, out_specs=[ks, ks],
        out_shape=[jax.ShapeDtypeStruct((n, w), F32)] * 2,
        compiler_params=_params("parallel", "arbitrary"),
    )(*args)


def _dc_onehot():
    c = np.arange(GRID_W)
    dc = np.clip(c[None, :] - c[:, None] + (C_COLS - 1), 0, 2 * C_COLS - 2).reshape(-1)
    m = np.zeros((GRID_W * GRID_W, 128), np.float32)
    m[np.arange(dc.size), dc] = 1.0
    return m


def _bias_tiles(rpb):
    h, nr, ncol = rpb.shape
    flat = jnp.pad(rpb.reshape(h * nr, ncol), ((0, (-h * nr) % 8), (0, 128 - ncol)))
    tiles = _mm(flat, jnp.asarray(_dc_onehot().T), name="rpb_tiles", exact=True, tn=GRID_W * GRID_W)
    tiles = tiles[:h * nr].reshape(h, nr, GRID_W, GRID_W)
    tiles = jnp.pad(tiles, ((0, 0), (2, BT_TILES + 1 - nr - 2), (0, 0), (0, 0)))
    return jnp.concatenate([tiles[:, :BT_TILES], tiles[:, 1:BT_TILES + 1]], -1)


def _bias_tiles_grad(dbt, nr, ncol):
    h = dbt.shape[0]
    d = dbt[:, 2:2 + nr, :, :GRID_W] + dbt[:, 1:1 + nr, :, GRID_W:]
    flat = jnp.pad(d.reshape(h * nr, GRID_W * GRID_W), ((0, (-h * nr) % 8), (0, 0)))
    g = _mm(flat, jnp.asarray(_dc_onehot()), name="rpb_grad", exact=True, tk=GRID_W * GRID_W)
    return g[:h * nr, :ncol].reshape(h, nr, ncol)


TM = 256
TQ_B, TK_B = 1024, 2048


def _relu2(acc):
    r = jnp.maximum(acc, 0.0)
    return (r * r,)


def _layer_fwd(x, xb, w, sm, tabs, alpha):
    tab_a, tab_q, tab_k = tabs
    s, d = x.shape
    ha = _mm(xb, w["in"], name="in_a", tn=QKV_W, b_cols=QKV_W, b_off=0)
    hb = _mm(xb, w["in"], name="in_b", tn=QKV_W, b_cols=QKV_W, b_off=1)
    hc = _mm(xb, w["in"], name="in_c", tn=QKV_W, b_cols=QKV_W, b_off=2)
    hg = _mm(xb, w["in"], name="in_g", outs=(_MXU,), tn=QKV_W, b_cols=3 * d, b_off=3)

    qa, ka, va = _prep_a(ha, tab_a, TM)
    stats = [_band_fwd(_BandA(s // dil // BAND), qa[dil], ka[dil], va[dil]) for dil in A_DILATIONS]
    oas, lse_a = _combine_a(*zip(*stats), s, TM)
    oa = oas[1]

    qb, kd, vd, v1 = _prep_b(hb, sm["q_norm"], sm["k_norm"], tab_q, tab_k, TM)
    ob, lse_b = _flash_fwd(qb, kd, v1, TQ_B, TK_B)

    qc, kc, vc = _prep_c(hc, TM)
    bt = _bias_tiles(sm["rpb"])
    oc, lse_c = _band_fwd(_BandC(s // BAND), qc, kc, vc, bt)

    pa = _mm(oa, w["br_a"], name="br_a", outs=(_MXU,))
    pb = _mm(ob, w["br_b"], name="br_b", outs=(_MXU,))
    pc = _mm(oc, w["br_c"], name="br_c", outs=(_MXU,))
    merged = _gate_merge(hg, sm["b_gate"], pa, pb, pc, TM)
    ln = dict(outs=(F32, F32, _MXU), epilogue=_ln_epilogue(alpha), tm=512, tn=d)
    r1, x1, x1b = _mm(merged, w["out"], name="w_out_ln1", extras=(x, sm["ln1_g"], sm["ln1_b"]), **ln)
    act = _mm(x1b, w["up"], name="w_up", outs=(_MXU,), epilogue=_relu2)
    r2, x2, x2b = _mm(act, w["down"], name="w_down_ln2", extras=(x1, sm["ln2_g"], sm["ln2_b"]), **ln)
    saved = dict(xb=xb, hb=hb, hg=hg, qa=qa, ka=ka, va=va, oa=oa, oas=oas, lse_a=lse_a, qb=qb, kd=kd, vd=vd, ob=ob, lse_b=lse_b,
                 qc=qc, kc=kc, vc=vc, oc=oc, lse_c=lse_c, bt=bt, pa=pa, pb=pb, pc=pc, merged=merged, r1=r1, x1b=x1b,
                 act=act, r2=r2)
    return x2, x2b, saved


def _layer_bwd(dx2, w, sm, sv, tabs, alpha):
    tab_a, tab_q, tab_k = tabs
    s, d = dx2.shape
    g = {}
    dr2, dr2b, dg2, db2 = _ln_bwd(dx2, sv["r2"], sm["ln2_g"], "ln2_bwd", TM)
    g["ln2_g"], g["ln2_b"] = dg2.sum(0), db2.sum(0)
    du = _mm(dr2b, w["down"], mode="nt", name="d_act", outs=(_MXU,), extras=(sv["act"],),
             epilogue=lambda acc, act: (acc * (2.0 * jnp.sqrt(act.astype(F32))),))
    g["w_down"] = _mm(sv["act"], dr2b, mode="tn", name="g_w_down")
    g["w_up"] = _mm(sv["x1b"], du, mode="tn", name="g_w_up")
    dx1 = _mm(du, w["up"], mode="nt", name="d_x1", extras=(dr2,), epilogue=lambda acc, e: (acc + alpha * e,))
    dr1, dr1b, dg1, db1 = _ln_bwd(dx1, sv["r1"], sm["ln1_g"], "ln1_bwd", TM)
    g["ln1_g"], g["ln1_b"] = dg1.sum(0), db1.sum(0)
    g["w_out"] = _mm(sv["merged"], dr1b, mode="tn", name="g_w_out")
    dmerged = _mm(dr1b, w["out"], mode="nt", name="d_merged")
    dpa, dpb, dpc, dlog, gb = _gate_bwd(dmerged, sv["hg"], sm["b_gate"], sv["pa"], sv["pb"], sv["pc"], TM)
    g["b_gate"] = gb.sum(0)
    g["w_branch_a"] = _mm(sv["oa"], dpa, mode="tn", name="g_br_a")
    g["w_branch_b"] = _mm(sv["ob"], dpb, mode="tn", name="g_br_b")
    g["w_branch_c"] = _mm(sv["oc"], dpc, mode="tn", name="g_br_c")
    doa = _mm(dpa, w["br_a"], mode="nt", name="d_oa")
    dob = _mm(dpb, w["br_b"], mode="nt", name="d_ob")
    doc = _mm(dpc, w["br_c"], mode="nt", name="d_oc")

    dqs, dks, dvs = [], [], []
    doas = _to_dilations(doa, "d_oa_layouts", TM)
    for dil in A_DILATIONS:
        band = _BandA(s // dil // BAND)
        args = [t[dil] for t in (sv["qa"], sv["ka"], sv["va"], doas, sv["oas"], sv["lse_a"])]
        dqs.append(_band_dq(band, *args)[0])
        dk_c, dv_c = _band_dkv(band, *args)
        dks.append(dk_c)
        dvs.append(dv_c)
    dha = _post_a(dqs, dks, dvs, tab_a, s, TM)

    dqb, dkd, dvd = _flash_bwd(sv["qb"], sv["kd"], sv["vd"], dob, sv["lse_b"], _delta_b(dob, sv["ob"], TQ_B), TQ_B, TK_B)
    dhb, gq, gk = _post_b(dqb, dkd, dvd, sv["hb"], sm["q_norm"], sm["k_norm"], tab_q, tab_k, TM)
    g["q_norm_b"] = gq.sum(0).reshape(-1, HEAD).sum(0)
    g["k_norm_b"] = gk.sum(0).reshape(-1, HEAD).sum(0)

    band_c = _BandC(s // BAND)
    cargs = (sv["qc"], sv["kc"], sv["vc"], doc, sv["oc"], sv["lse_c"], sv["bt"])
    dqc, dbt = _band_dq(band_c, *cargs)
    dkc, dvc = _band_dkv(band_c, *cargs)
    dhc = _post_c(dqc, dkc, dvc, TM)
    g["rpb_c"] = _bias_tiles_grad(dbt, 2 * C_ROWS - 1, 2 * C_COLS - 1)

    xb = sv["xb"]
    g["w_in"] = jnp.concatenate([_mm(xb, dh, mode="tn", name="g_in_" + nm)
                                 for nm, dh in (("a", dha), ("b", dhb), ("c", dhc), ("g", dlog))], 1)
    dx = _mm(dha, w["in"], mode="nt", name="d_x_a", tk=QKV_W, b_cols=QKV_W, b_off=0, extras=(dr1,),
             epilogue=lambda acc, e: (acc + alpha * e,))
    for nm, dh, off in (("b", dhb, 1), ("c", dhc, 2), ("g", dlog, 3)):
        dx = _mm(dh, w["in"], mode="nt", name="d_x_" + nm, tk=QKV_W, b_cols=dh.shape[1], b_off=off, extras=(dx,),
                 epilogue=lambda acc, e: (acc + e,))
    return dx, g


BIG = ("w_in", "w_branch_a", "w_branch_b", "w_branch_c", "w_out", "w_up", "w_down")
ROW_SHARDED = ("w_out", "w_down")
SMALL = ("b_gate", "q_norm_b", "k_norm_b", "rpb_c", "ln1_g", "ln1_b", "ln2_g", "ln2_b")


def _local_step(x, target, wfull, small):
    s, d = x.shape
    depth = wfull["w_in"].shape[0]
    alpha = (2 * depth) ** 0.25
    tabs = _tables(s)
    ws, sms = [], []
    names = dict(w_in="in", w_branch_a="br_a", w_branch_b="br_b", w_branch_c="br_c", w_out="out", w_up="up", w_down="down")
    for l in range(depth):
        ws.append({short: (wfull[n], l) for n, short in names.items()})
        sms.append(dict(b_gate=small["b_gate"][l][None], q_norm=jnp.tile(small["q_norm_b"][l], BQ_W // HEAD)[None],
                        k_norm=jnp.tile(small["k_norm_b"][l], BKV_W // HEAD)[None], rpb=small["rpb_c"][l],
                        ln1_g=small["ln1_g"][l][None], ln1_b=small["ln1_b"][l][None],
                        ln2_g=small["ln2_g"][l][None], ln2_b=small["ln2_b"][l][None]))
    saved = []
    h, hb = x, x.astype(_MXU)
    for l in range(depth):
        h, hb, sv = _layer_fwd(h, hb, ws[l], sms[l], tabs, alpha)
        saved.append(sv)
    sq, dy = _loss_head(h, target, TM)
    grads = [None] * depth
    for l in reversed(range(depth)):
        dy, grads[l] = _layer_bwd(dy, ws[l], sms[l], saved[l], tabs, alpha)
    stacked = {k: jnp.stack([gl[k] for gl in grads]) for k in grads[0]}
    return sq, dy, stacked


def _place():
    return lax.axis_index("x"), lax.axis_index("y"), lax.axis_index("c")


def _flip(a, b):
    return a + b - 2 * a * b


def _other_chips(x, y):
    return [(1 - x, y), (x, 1 - y), (1 - x, 1 - y)]


def _gather_shards(flat):
    r = flat.shape[0]
    rh = r // 2

    def body(src, out, send_sems, recv_sems):
        x, y, c = _place()
        me, sibling = (x, y), (x, y, 1 - c)
        n1, n2, dg = (_flip(x, 1 - c), _flip(y, c)), (_flip(x, c), _flip(y, 1 - c)), (1 - x, 1 - y)

        def half(chip, hc):
            return out.at[2 * chip[0] + chip[1], pl.ds(hc * rh, rh), :]

        def copy(k, src_ref, dst_ref, to):
            return pltpu.make_async_remote_copy(src_ref=src_ref, dst_ref=dst_ref, send_sem=send_sems.at[k],
                                                recv_sem=recv_sems.at[k], device_id=to, device_id_type=MESH)

        own = src.at[pl.ds(c * rh, rh), :]
        sends = [copy(0, own, half(me, c), (*n1, c)), copy(1, own, half(me, c), (*n2, c)), copy(6, src, out.at[2 * x + y], sibling)]
        for cp in sends:
            cp.start()
        for k, chip, j in ((0, n1, c), (1, n2, 1 - c), (2, dg, 2)):
            copy(k, half(chip, c), half(chip, c), sibling).wait_recv()
            if k == 0:
                sends.append(copy(2, half(n1, c), half(n1, c), (*n2, c)))
                sends[-1].start()
            sends.append(copy(3 + j, half(chip, c), half(chip, c), sibling))
            sends[-1].start()
        for j, chip in enumerate(_other_chips(x, y)):
            copy(3 + j, half(chip, 1 - c), half(chip, 1 - c), sibling).wait_recv()
        copy(6, src, out.at[2 * x + y], sibling).wait_recv()
        for cp in sends:
            cp.wait_send()

    return pl.pallas_call(
        body, name="gather_weights", in_specs=[ANY], out_specs=ANY,
        out_shape=jax.ShapeDtypeStruct((4, r, 128), flat.dtype),
        scratch_shapes=[pltpu.SemaphoreType.DMA((7,)), pltpu.SemaphoreType.DMA((7,))],
    )(flat)


def _pair_exchange(part):
    _, r, _ = part.shape
    rh = r // 2

    def body(src, out, send_sem, recv_sem):
        x, y, c = _place()
        cp = pltpu.make_async_remote_copy(src_ref=src.at[:, pl.ds((1 - c) * rh, rh), :], dst_ref=out, send_sem=send_sem,
                                          recv_sem=recv_sem, device_id=(x, y, 1 - c), device_id_type=MESH)
        cp.start()
        cp.wait()

    return pl.pallas_call(
        body, name="grad_pair_exchange", in_specs=[ANY], out_specs=ANY,
        out_shape=jax.ShapeDtypeStruct((4, rh, 128), part.dtype),
        scratch_shapes=[pltpu.SemaphoreType.DMA, pltpu.SemaphoreType.DMA],
    )(part)


def _chip_exchange(t):
    _, rh, _ = t.shape

    def body(src, out, send_sems, recv_sems):
        x, y, c = _place()
        cps = [pltpu.make_async_remote_copy(src_ref=src.at[2 * chip[0] + chip[1]], dst_ref=out.at[k], send_sem=send_sems.at[k],
                                            recv_sem=recv_sems.at[k], device_id=(*chip, c), device_id_type=MESH)
               for k, chip in enumerate(_other_chips(x, y))]
        for cp in cps:
            cp.start()
        for cp in cps:
            cp.wait()

    return pl.pallas_call(
        body, name="grad_chip_exchange", in_specs=[ANY], out_specs=ANY,
        out_shape=jax.ShapeDtypeStruct((3, rh, 128), t.dtype),
        scratch_shapes=[pltpu.SemaphoreType.DMA((3,)), pltpu.SemaphoreType.DMA((3,))],
    )(t)


def _pair_share(half):
    rh = half.shape[0]

    def body(src, out, send_sem, recv_sem):
        x, y, c = _place()
        cp = pltpu.make_async_remote_copy(src_ref=src, dst_ref=out, send_sem=send_sem, recv_sem=recv_sem,
                                          device_id=(x, y, 1 - c), device_id_type=MESH)
        cp.start()
        cp.wait()

    theirs = pl.pallas_call(
        body, name="grad_pair_share", in_specs=[ANY], out_specs=ANY,
        out_shape=jax.ShapeDtypeStruct((rh, 128), half.dtype),
        scratch_shapes=[pltpu.SemaphoreType.DMA, pltpu.SemaphoreType.DMA],
    )(half)

    tr = _tile(rh, 2048, 8)

    def join(c_ref, mine_ref, theirs_ref, o_ref):
        o_ref[...] = jnp.where(pl.program_id(0) == c_ref[0], mine_ref[...], theirs_ref[...])

    spec = pl.BlockSpec((tr, 128), lambda h, i, c_ref: (i, 0))
    c = jnp.reshape(lax.axis_index("c"), (1,)).astype(jnp.int32)
    return pl.pallas_call(
        join, name="grad_pair_join",
        grid_spec=pltpu.PrefetchScalarGridSpec(
            num_scalar_prefetch=1, grid=(2, rh // tr), in_specs=[spec, spec],
            out_specs=pl.BlockSpec((tr, 128), lambda h, i, c_ref: (h * (rh // tr) + i, 0))),
        out_shape=jax.ShapeDtypeStruct((2 * rh, 128), half.dtype),
        compiler_params=_params("parallel", "parallel"),
    )(c, half, theirs)


def _gather_all(v):
    r = v.shape[0]

    def body(src, out, send_sems, recv_sems, local_sem):
        x, y, c = _place()
        me = 4 * x + 2 * y + c
        mine = pltpu.make_async_copy(src, out.at[me], local_sem)
        mine.start()
        cps = []
        for k in range(1, 8):
            fx, fy, fc = (k >> 2) & 1, (k >> 1) & 1, k & 1
            peer = (x + fx - 2 * x * fx, y + fy - 2 * y * fy, c + fc - 2 * c * fc)
            cps.append(pltpu.make_async_remote_copy(src_ref=src, dst_ref=out.at[me], send_sem=send_sems.at[k - 1],
                                                    recv_sem=recv_sems.at[k - 1], device_id=peer, device_id_type=MESH))
        for cp in cps:
            cp.start()
        for k in range(1, 8):
            fx, fy, fc = (k >> 2) & 1, (k >> 1) & 1, k & 1
            frm = 4 * (x + fx - 2 * x * fx) + 2 * (y + fy - 2 * y * fy) + (c + fc - 2 * c * fc)
            pltpu.make_async_remote_copy(src_ref=src, dst_ref=out.at[frm], send_sem=send_sems.at[k - 1],
                                         recv_sem=recv_sems.at[k - 1], device_id=(x, y, c), device_id_type=MESH).wait_recv()
        for cp in cps:
            cp.wait_send()
        mine.wait()

    return pl.pallas_call(
        body, name="gather_small_grads", in_specs=[ANY], out_specs=ANY,
        out_shape=jax.ShapeDtypeStruct((8, r, 128), v.dtype),
        scratch_shapes=[pltpu.SemaphoreType.DMA((7,)), pltpu.SemaphoreType.DMA((7,)), pltpu.SemaphoreType.DMA],
    )(v)


def _sum_slots(parts, name):
    n, r, _ = parts.shape
    tr = _tile(r, 1024, 8)

    def body(p_ref, o_ref):
        acc = p_ref[0]
        for j in range(1, n):
            acc = acc + p_ref[j]
        o_ref[...] = acc

    return pl.pallas_call(
        body, name=name, grid=(r // tr,), in_specs=[pl.BlockSpec((n, tr, 128), lambda i: (0, i, 0))],
        out_specs=pl.BlockSpec((tr, 128), lambda i: (i, 0)), out_shape=jax.ShapeDtypeStruct((r, 128), parts.dtype),
        compiler_params=_params("parallel"),
    )(parts)


def _add_sibling_half(part, recv, c):
    _, rh, _ = recv.shape
    tr = _tile(rh, 1024, 8)
    nblk = rh // tr

    def body(c_ref, p_ref, r_ref, o_ref):
        o_ref[...] = (p_ref[...].astype(F32) + r_ref[...].astype(F32)).astype(o_ref.dtype)

    return pl.pallas_call(
        body, name="grad_pair_sum",
        grid_spec=pltpu.PrefetchScalarGridSpec(
            num_scalar_prefetch=1, grid=(4, nblk),
            in_specs=[pl.BlockSpec((None, tr, 128), lambda j, i, c_ref: (j, c_ref[0] * nblk + i, 0)),
                      pl.BlockSpec((None, tr, 128), lambda j, i, c_ref: (j, i, 0))],
            out_specs=pl.BlockSpec((None, tr, 128), lambda j, i, c_ref: (j, i, 0))),
        out_shape=jax.ShapeDtypeStruct(recv.shape, recv.dtype),
        compiler_params=_params("parallel", "parallel"),
    )(c, part, recv)


def _add_chips(t, recv, me):
    _, rh, _ = t.shape
    tr = _tile(rh, 1024, 8)

    def body(me_ref, t_ref, r_ref, o_ref):
        f = lambda v: v.astype(F32)
        o_ref[...] = ((f(t_ref[...]) + f(r_ref[0])) + f(r_ref[1])) + f(r_ref[2])

    return pl.pallas_call(
        body, name="grad_chip_sum",
        grid_spec=pltpu.PrefetchScalarGridSpec(
            num_scalar_prefetch=1, grid=(rh // tr,),
            in_specs=[pl.BlockSpec((None, tr, 128), lambda i, me_ref: (me_ref[0], i, 0)),
                      pl.BlockSpec((3, tr, 128), lambda i, me_ref: (0, i, 0))],
            out_specs=pl.BlockSpec((tr, 128), lambda i, me_ref: (i, 0))),
        out_shape=jax.ShapeDtypeStruct((rh, 128), F32),
        compiler_params=_params("parallel"),
    )(me, t, recv)


def _reduce_scatter(part):
    x, y, c = _place()
    t = _add_sibling_half(part, _pair_exchange(part), jnp.reshape(c, (1,)).astype(jnp.int32))
    half = _add_chips(t, _chip_exchange(t), jnp.reshape(2 * x + y, (1,)).astype(jnp.int32))
    return _pair_share(half)


def _to_rows(parts, mult):
    flat = jnp.concatenate([p.reshape(-1) for p in parts])
    flat = jnp.pad(flat, (0, (-flat.size) % (128 * mult)))
    return flat.reshape(-1, 128)


def _from_rows(rows, shapes):
    flat, out, at = rows.reshape(-1), [], 0
    for shp in shapes:
        n = int(np.prod(shp))
        out.append(flat[at:at + n].reshape(shp))
        at += n
    return out


def _full_from_shards(g, name, shard_shape):
    depth = shard_shape[0]
    if name in ROW_SHARDED:
        return jnp.moveaxis(g, 0, 1).reshape(depth, 4 * shard_shape[1], shard_shape[2])
    return jnp.moveaxis(g, 0, 2).reshape(depth, shard_shape[1], 4 * shard_shape[2])


def _shards_from_full(full, name):
    depth, rows, cols = full.shape
    if name in ROW_SHARDED:
        return jnp.moveaxis(full.reshape(depth, 4, rows // 4, cols), 1, 0)
    return jnp.moveaxis(full.reshape(depth, rows, 4, cols // 4), 2, 0)


def kernel(x, w_in, b_gate, q_norm_b, k_norm_b, rpb_c, w_branch_a, w_branch_b, w_branch_c, w_out, ln1_g, ln1_b, w_up, w_down, ln2_g, ln2_b, loss_target, m_w_in, m_b_gate, m_q_norm_b, m_k_norm_b, m_rpb_c, m_w_branch_a, m_w_branch_b, m_w_branch_c, m_w_out, m_ln1_g, m_ln1_b, m_w_up, m_w_down, m_ln2_g, m_ln2_b, v_w_in, v_b_gate, v_q_norm_b, v_k_norm_b, v_rpb_c, v_w_branch_a, v_w_branch_b, v_w_branch_c, v_w_out, v_ln1_g, v_ln1_b, v_w_up, v_w_down, v_ln2_g, v_ln2_b):
    args = dict(locals())
    big_shard = {n: args[n] for n in BIG}
    small = {n: args[n] for n in SMALL}
    shapes = [big_shard[n].shape for n in BIG]

    flat = _to_rows([big_shard[n].astype(_MXU) for n in BIG], 32)
    gathered = _gather_shards(flat)
    per_chip = [_from_rows(gathered[j], shapes) for j in range(4)]
    wfull = {n: _full_from_shards(jnp.stack([per_chip[j][i] for j in range(4)]), n, shapes[i]) for i, n in enumerate(BIG)}

    sq, grad_x, grads = _local_step(x[0], loss_target[0], wfull, small)
    loss = lax.psum(0.5 * jnp.sum(sq) / x.shape[-1], ("x", "y", "c"))

    part = jnp.stack([_to_rows([_shards_from_full(grads[n], n)[j].astype(_MXU) for n in BIG], 32) for j in range(4)])
    g_big = _from_rows(_reduce_scatter(part), shapes)
    small_shapes = [small[n].shape for n in SMALL]
    g_small = _from_rows(_sum_slots(_gather_all(_to_rows([grads[n] for n in SMALL], 8)), "small_grad_sum"), small_shapes)
    grad = dict(zip(BIG, g_big))
    grad.update(zip(SMALL, g_small))

    delta, new_m, new_v = {}, {}, {}
    for n in BIG:
        shp = big_shard[n].shape
        two_d = lambda t: t.reshape(-1, shp[-1])
        res = _adamw(two_d(big_shard[n]), two_d(grad[n]), two_d(args["m_" + n]), two_d(args["v_" + n]), "adamw_" + n)
        delta[n], new_m[n], new_v[n] = [t.reshape(shp) for t in res]
    packed = [_to_rows([args[pre + n] for n in SMALL], 8) for pre in ("", "m_", "v_")]
    res = _adamw(packed[0], _to_rows([grad[n] for n in SMALL], 8), packed[1], packed[2], "adamw_small")
    for dst, rows in zip((delta, new_m, new_v), res):
        dst.update(zip(SMALL, _from_rows(rows, small_shapes)))

    order = ("w_in", "b_gate", "q_norm_b", "k_norm_b", "rpb_c", "w_branch_a", "w_branch_b", "w_branch_c", "w_out",
             "ln1_g", "ln1_b", "w_up", "w_down", "ln2_g", "ln2_b")
    return (loss, grad_x[None], *[grad[n] for n in order], *[delta[n] for n in order],
            *[new_m[n] for n in order], *[new_v[n] for n in order])
```

```python
import functools

import numpy as np
import jax
import jax.numpy as jnp
from jax import lax
from jax.experimental import pallas as pl
from jax.experimental.pallas import tpu as pltpu

F32 = jnp.float32
_MXU = jnp.bfloat16

HEAD = 64
A_W, BQ_W, BKV_W, C_W = 256, 512, 128, 256
QKV_W = 768
A_DILATIONS = (1, 4, 16)
A_RADIUS = 64
A_ROPE_HALF = 8
AX_ROPE_HALF = 16
ROPE_THETA = 500000.0
AX_THETA = 10000.0
GRID_W = 64
C_ROWS = 8
C_COLS = 16
BAND = 128
BT_TILES = 18
LN_EPS = 1e-5
RMS_EPS = 1e-6
NEG = -1e30
SCALE = HEAD ** -0.5
LOG2E = 1.4426950408889634
LN2 = 0.6931471805599453
ADAM_LR, ADAM_B1, ADAM_B2, ADAM_EPS, ADAM_WD, ADAM_STEP = 0.001, 0.9, 0.999, 1e-08, 0.01, 10
V7X_VMEM_LIMIT = 48 * 1024 * 1024
MESH = pl.DeviceIdType.MESH
ANY = pl.BlockSpec(memory_space=pl.ANY)


def _params(*sem):
    return pltpu.CompilerParams(dimension_semantics=sem or None, vmem_limit_bytes=V7X_VMEM_LIMIT)


def _tile(n, pref, align=128):
    if n <= pref:
        return n
    t = (pref // align) * align
    while t >= align:
        if n % t == 0:
            return t
        t -= align
    return n


def _mm(a, b, *, name, mode="nn", outs=((F32),), epilogue=None, extras=(), tm=1024, tn=1024, tk=2048, exact=False,
        b_cols=None, b_off=0):
    b, b_lead = b if isinstance(b, tuple) else (b, None)
    m, k = a.shape if mode != "tn" else a.shape[::-1]
    b_rows, b_last = b.shape[-2], (b_cols or b.shape[-1])
    k2, n = (b_rows, b_last) if mode != "nt" else (b_last, b_rows)
    assert k == k2, (a.shape, b.shape, mode)
    tm, tn, tk = _tile(m, tm), _tile(n, tn), _tile(k, tk)
    nk = k // tk
    n_ex, n_out = len(extras), len(outs)
    mx = F32 if exact else _MXU
    prec = lax.Precision.HIGHEST if exact else None
    dims = {"nn": (((1,), (0,)), ((), ())), "nt": (((1,), (1,)), ((), ())), "tn": (((0,), (0,)), ((), ()))}[mode]

    def body(*refs):
        a_ref, b_ref = refs[0], refs[1]
        ex = refs[2:2 + n_ex]
        out_refs = refs[2 + n_ex:2 + n_ex + n_out]
        kk = pl.program_id(2)
        av, bv = a_ref[...].astype(mx), b_ref[...].astype(mx)
        part = lax.dot_general(av, bv, dims, preferred_element_type=F32, precision=prec)

        def finish(res):
            vals = epilogue(res, *[e[...] for e in ex]) if epilogue is not None else (res,)
            for o, v in zip(out_refs, vals):
                o[...] = v.astype(o.dtype)

        if nk == 1:
            finish(part)
        else:
            acc = refs[-1]

            @pl.when(kk == 0)
            def _():
                acc[...] = part

            @pl.when((kk > 0) & (kk < nk - 1))
            def _():
                acc[...] += part

            @pl.when(kk == nk - 1)
            def _():
                finish(acc[...] + part)

    a_spec = pl.BlockSpec((tm, tk), lambda i, j, kk: (i, kk)) if mode != "tn" else pl.BlockSpec((tk, tm), lambda i, j, kk: (kk, i))
    lead = () if b_lead is None else (None,)
    at = () if b_lead is None else (b_lead,)
    if mode == "nt":
        b_spec = pl.BlockSpec(lead + (tn, tk), lambda i, j, kk: at + (j, kk + b_off))
    else:
        b_spec = pl.BlockSpec(lead + (tk, tn), lambda i, j, kk: at + (kk, j + b_off))
    o_spec = pl.BlockSpec((tm, tn), lambda i, j, kk: (i, j))
    res = pl.pallas_call(
        body, name=name, grid=(m // tm, n // tn, nk),
        in_specs=[a_spec, b_spec] + [o_spec if e.shape[0] > 1 else pl.BlockSpec((1, tn), lambda i, j, kk: (0, j)) for e in extras],
        out_specs=[o_spec] * n_out,
        out_shape=[jax.ShapeDtypeStruct((m, n), d) for d in outs],
        scratch_shapes=[pltpu.VMEM((tm, tn), F32)] if nk > 1 else [],
        compiler_params=_params("parallel", "parallel", "arbitrary"),
    )(a, b, *extras)
    return res[0] if n_out == 1 else res


def _rows(tm, width, cb=0):
    return pl.BlockSpec((tm, width), lambda t: (t, cb))


def _whole(arr):
    nd = arr.ndim
    return pl.BlockSpec(arr.shape, lambda t: (0,) * nd)


def _rowwise(fn, name, rows, tm, ins, outs):
    n_in, n_out = len(ins), len(outs)
    dil_in = [spec[1:] if isinstance(spec, tuple) else None for _, spec in ins]
    in_specs = [_rows(tm // spec[1], spec[1] * spec[2]) if isinstance(spec, tuple) else spec for _, spec in ins]
    scratch = [pltpu.VMEM((di[1] // 128, tm, 128), F32) for di in dil_in if di] + \
              [pltpu.VMEM((n // 128, tm, 128), F32) for n, _, kind in outs if isinstance(kind, int)]

    def body(*refs):
        scr = list(refs[n_in + n_out:])
        blocks = []
        for r, di in zip(refs[:n_in], dil_in):
            if di is None:
                blocks.append(r[...])
            else:
                d, n = di
                s_ref = scr.pop(0)
                for j in range(d):
                    for b in range(n // 128):
                        lanes = slice(j * n + b * 128, j * n + (b + 1) * 128)
                        s_ref.at[b][pl.ds(j, tm // d, stride=d), :] = r[:, lanes].astype(F32)
                blocks.append(jnp.concatenate([s_ref[b] for b in range(n // 128)], 1))
        vals = fn(*blocks)
        first = pl.program_id(0) == 0
        for (ncols, _, kind), o, v in zip(outs, refs[n_in:n_in + n_out], vals):
            if kind == "row":
                o[...] = v.astype(o.dtype)
            elif isinstance(kind, int):
                s_ref = scr.pop(0)
                for b in range(ncols // 128):
                    s_ref[b] = v[:, b * 128:(b + 1) * 128].astype(F32)
                for j in range(kind):
                    for b in range(ncols // 128):
                        lanes = slice(j * ncols + b * 128, j * ncols + (b + 1) * 128)
                        o[:, lanes] = s_ref.at[b][pl.ds(j, tm // kind, stride=kind), :].astype(o.dtype)
            else:
                part = v.reshape(tm // 8, 8, ncols).sum(0)

                @pl.when(first)
                def _(o=o, part=part):
                    o[...] = part

                @pl.when(jnp.logical_not(first))
                def _(o=o, part=part):
                    o[...] += part

    def out_spec(n, kind):
        if kind == "row":
            return _rows(tm, n), (rows, n)
        if isinstance(kind, int):
            return _rows(tm // kind, kind * n), (rows // kind, kind * n)
        return pl.BlockSpec((8, n), lambda t: (0, 0)), (8, n)

    specs = [out_spec(n, kind) for n, _, kind in outs]
    res = pl.pallas_call(
        body, name=name, grid=(rows // tm,),
        in_specs=in_specs, out_specs=[s for s, _ in specs],
        out_shape=[jax.ShapeDtypeStruct(shp, d) for (_, shp), (_, d, _) in zip(specs, outs)],
        scratch_shapes=scratch, compiler_params=_params("arbitrary"),
    )(*[a for a, _ in ins])
    return res


def _lane_lo(width=128):
    return (lax.broadcasted_iota(jnp.int32, (1, width), 1) & (HEAD * 2 - 1)) < HEAD


def _group_sum(x):
    w = x.shape[-1]
    sh = HEAD.bit_length() - 1
    same = (lax.broadcasted_iota(jnp.int32, (w, w), 0) >> sh) == (lax.broadcasted_iota(jnp.int32, (w, w), 1) >> sh)
    ones = jnp.where(same, 1.0, 0.0).astype(jnp.bfloat16)
    hi = x.astype(jnp.bfloat16)
    lo = (x - hi.astype(F32)).astype(jnp.bfloat16)
    return jnp.dot(hi, ones, preferred_element_type=F32) + jnp.dot(lo, ones, preferred_element_type=F32)


def _rot(x, c, sm, sp, shift):
    w = x.shape[-1]
    return x * c + pltpu.roll(x, w - shift, 1) * sm + pltpu.roll(x, shift, 1) * sp


def _rot_t(dy, c, sm, sp, shift):
    w = dy.shape[-1]
    return dy * c + pltpu.roll(dy * sm, shift, 1) + pltpu.roll(dy * sp, w - shift, 1)


def _rope_tables(pos_parts, half, thetas):
    cs, sms, sps = [], [], []
    for pos, theta in zip(pos_parts, thetas):
        inv = theta ** (-jnp.arange(half, dtype=F32) / half)
        ang = pos.astype(F32)[:, None] * inv[None, :]
        co, si, ze = jnp.cos(ang), jnp.sin(ang), jnp.zeros_like(ang)
        cs += [co, co]
        sms += [-si, ze]
        sps += [ze, si]
    return [jnp.concatenate(t, axis=1) for t in (cs, sms, sps)]


def _tables(s):
    pos = jnp.arange(s)
    ca, sma, spa = _rope_tables([pos], A_ROPE_HALF, [ROPE_THETA])
    pad = HEAD - 2 * A_ROPE_HALF
    ca = jnp.concatenate([ca, jnp.ones((s, pad), F32)], 1)
    sma, spa = [jnp.concatenate([t, jnp.zeros((s, pad), F32)], 1) for t in (sma, spa)]
    tab_a = [jnp.tile(t, (1, A_W // HEAD)) for t in (ca, sma, spa)]
    ax = _rope_tables([pos // GRID_W, pos % GRID_W], AX_ROPE_HALF, [AX_THETA, AX_THETA])
    tab_q = [jnp.tile(t, (1, BQ_W // HEAD)) for t in ax]
    tab_k = [jnp.tile(t, (1, BKV_W // HEAD)) for t in ax]
    return tab_a, tab_q, tab_k


def _prep_a(ha, tab, tm):
    s = ha.shape[0]

    def fn(h, c, sm, sp):
        q, k, v = h[:, :A_W], h[:, A_W:2 * A_W], h[:, 2 * A_W:]
        q, k = _rot(q, c, sm, sp, A_ROPE_HALF) * SCALE, _rot(k, c, sm, sp, A_ROPE_HALF)
        return [t for t in (q, k, v) for _ in A_DILATIONS]

    res = _rowwise(fn, "prep_a", s, tm, [(ha, _rows(tm, QKV_W))] + [(t, _rows(tm, A_W)) for t in tab],
                   [(A_W, _MXU, _dil_kind(d)) for _ in range(3) for d in A_DILATIONS])
    n = len(A_DILATIONS)
    return [dict(zip(A_DILATIONS, res[i * n:(i + 1) * n])) for i in range(3)]


def _dil_kind(d):
    return "row" if d == 1 else d


def _dil_spec(d, tm, ncols):
    return _rows(tm, ncols) if d == 1 else ("dil", d, ncols)


def _to_dilations(x, name, tm):
    s, n = x.shape
    res = _rowwise(lambda v: [v for d in A_DILATIONS if d > 1], name, s, tm, [(x, _rows(tm, n))],
                   [(n, x.dtype, d) for d in A_DILATIONS if d > 1])
    return {1: x, **dict(zip([d for d in A_DILATIONS if d > 1], res))}


def _rms(x, g):
    ms = _group_sum(x * x) * (1.0 / HEAD)
    return x * lax.rsqrt(ms + RMS_EPS) * g


def _prep_b(hb, gq, gk, tab_q, tab_k, tm):
    s = hb.shape[0]

    def fn(h, gq, gk, cq, smq, spq, ck, smk, spk):
        xq, xk, v = h[:, :BQ_W], h[:, BQ_W:BQ_W + BKV_W], h[:, BQ_W + BKV_W:]
        q = _rot(_rms(xq, gq), cq, smq, spq, AX_ROPE_HALF) * (SCALE * LOG2E)
        k = _rot(_rms(xk, gk), ck, smk, spk, AX_ROPE_HALF)
        lo = _lane_lo()
        kr, vr = pltpu.roll(k, HEAD, 1), pltpu.roll(v, HEAD, 1)
        kd = jnp.concatenate([jnp.where(lo, k, kr), jnp.where(lo, kr, k)], 1)
        vd = jnp.concatenate([jnp.where(lo, v, vr), jnp.where(lo, vr, v)], 1)
        v1 = jnp.concatenate([jnp.where(lo, v, 1.0), jnp.where(lo, vr, 1.0)], 1)
        return q, kd, vd, v1

    ins = [(hb, _rows(tm, QKV_W)), (gq, _whole(gq)), (gk, _whole(gk))]
    ins += [(t, _rows(tm, BQ_W)) for t in tab_q] + [(t, _rows(tm, BKV_W)) for t in tab_k]
    return _rowwise(fn, "prep_b", s, tm, ins, [(BQ_W, _MXU, "row")] + [(2 * BKV_W, _MXU, "row")] * 3)


def _prep_c(hc, tm):
    def fn(h):
        return h[:, :C_W] * SCALE, h[:, C_W:2 * C_W], h[:, 2 * C_W:]

    return _rowwise(fn, "prep_c", hc.shape[0], tm, [(hc, _rows(tm, QKV_W))], [(C_W, _MXU, "row")] * 3)


def _combine_a(os_, ms, ls, s, tm):
    def fn(o1, o2, o3, m1, m2, m3, l1, l2, l3):
        lo = _lane_lo()
        outs, lses = [], []
        for p in range(A_W // 128):
            st = slice(p * 256, (p + 1) * 256)
            mm = [m[:, st] for m in (m1, m2, m3)]
            ll = [l[:, st] for l in (l1, l2, l3)]
            mmax = jnp.maximum(jnp.maximum(mm[0], mm[1]), mm[2])
            ws = [jnp.exp(m - mmax) for m in mm]
            den = ws[0] * ll[0] + ws[1] * ll[1] + ws[2] * ll[2]
            lses.append(mmax + jnp.log(den))
            num = sum(jnp.where(lo, w[:, :128], w[:, 128:]) * o[:, p * 128:(p + 1) * 128] for w, o in zip(ws, (o1, o2, o3)))
            outs.append(num / jnp.where(lo, den[:, :128], den[:, 128:]))
        o, lse = jnp.concatenate(outs, 1), jnp.concatenate(lses, 1)
        return [o] * len(A_DILATIONS) + [lse] * len(A_DILATIONS)

    ins = [(t, _dil_spec(d, tm, w)) for ts, w in ((os_, A_W), (ms, 2 * A_W), (ls, 2 * A_W)) for t, d in zip(ts, A_DILATIONS)]
    res = _rowwise(fn, "combine_a", s, tm, ins,
                   [(w, F32, _dil_kind(d)) for w in (A_W, 2 * A_W) for d in A_DILATIONS])
    n = len(A_DILATIONS)
    return dict(zip(A_DILATIONS, res[:n])), dict(zip(A_DILATIONS, res[n:]))


def _gates(hg, bg, d):
    return [jax.nn.sigmoid(hg[:, i * d:(i + 1) * d] + bg[:, i * d:(i + 1) * d]) for i in range(3)]


def _gate_merge(hg, bg, pa, pb, pc, tm):
    s, d = pa.shape

    def fn(hg, bg, pa, pb, pc):
        g = _gates(hg, bg, d)
        return (g[0] * pa + g[1] * pb + g[2] * pc,)

    ins = [(hg, _rows(tm, 3 * d)), (bg, _whole(bg))] + [(p, _rows(tm, d)) for p in (pa, pb, pc)]
    return _rowwise(fn, "gate_merge", s, tm, ins, [(d, _MXU, "row")])[0]


def _gate_bwd(dm, hg, bg, pa, pb, pc, tm):
    s, d = pa.shape

    def fn(dm, hg, bg, pa, pb, pc):
        g = _gates(hg, bg, d)
        dlog = jnp.concatenate([dm * p * gi * (1.0 - gi) for p, gi in zip((pa, pb, pc), g)], 1)
        return dm * g[0], dm * g[1], dm * g[2], dlog, dlog

    ins = [(dm, _rows(tm, d)), (hg, _rows(tm, 3 * d)), (bg, _whole(bg))] + [(p, _rows(tm, d)) for p in (pa, pb, pc)]
    return _rowwise(fn, "gate_bwd", s, tm, ins, [(d, _MXU, "row")] * 3 + [(3 * d, _MXU, "row"), (3 * d, F32, "acc")])


def _ln_stats(r):
    mu = jnp.mean(r, -1, keepdims=True)
    xc = r - mu
    var = jnp.mean(xc * xc, -1, keepdims=True)
    rstd = lax.rsqrt(var + LN_EPS)
    return xc * rstd, rstd


def _ln_epilogue(alpha):
    def fn(br, x, g, b):
        r = alpha * x + br
        xhat, _ = _ln_stats(r)
        y = xhat * g + b
        return r, y, y

    return fn


def _ln_bwd(dy, r, g, name, tm):
    s, d = r.shape

    def fn(dy, r, g):
        xhat, rstd = _ln_stats(r)
        dxh = dy * g
        dr = rstd * (dxh - jnp.mean(dxh, -1, keepdims=True) - xhat * jnp.mean(dxh * xhat, -1, keepdims=True))
        return dr, dr, dy * xhat, dy

    ins = [(dy, _rows(tm, d)), (r, _rows(tm, d)), (g, _whole(g))]
    return _rowwise(fn, name, s, tm, ins, [(d, F32, "row"), (d, _MXU, "row"), (d, F32, "acc"), (d, F32, "acc")])


def _loss_head(y, target, tm):
    s, d = y.shape

    def fn(y, t):
        diff = y - t
        return diff * diff, diff * (1.0 / d)

    sq, dy = _rowwise(fn, "loss_head", s, tm, [(y, _rows(tm, d)), (target, _rows(tm, d))], [(d, F32, "acc"), (d, F32, "row")])
    return sq, dy


def _post_a(dqs, dks, dvs, tab, s, tm):
    def fn(q1, q2, q3, k1, k2, k3, v1, v2, v3, c, sm, sp):
        dq = _rot_t((q1 + q2 + q3) * SCALE, c, sm, sp, A_ROPE_HALF)
        dk = _rot_t(k1 + k2 + k3, c, sm, sp, A_ROPE_HALF)
        return (jnp.concatenate([dq, dk, v1 + v2 + v3], 1),)

    ins = [(t, _dil_spec(d, tm, A_W)) for ts in (dqs, dks, dvs) for t, d in zip(ts, A_DILATIONS)]
    ins += [(t, _rows(tm, A_W)) for t in tab]
    return _rowwise(fn, "post_a", s, tm, ins, [(QKV_W, _MXU, "row")])[0]


def _post_b(dq, dkd, dvd, hb, gq, gk, tab_q, tab_k, tm):
    s = dq.shape[0]

    def back(dz, x, g, c, sm, sp):
        dy = _rot_t(dz, c, sm, sp, AX_ROPE_HALF)
        rstd = lax.rsqrt(_group_sum(x * x) * (1.0 / HEAD) + RMS_EPS)
        xh = x * rstd
        dxh = dy * g
        return rstd * (dxh - xh * (_group_sum(dxh * xh) * (1.0 / HEAD))), dy * xh

    def fn(dq, dkd, dvd, h, gq, gk, cq, smq, spq, ck, smk, spk):
        lo = _lane_lo()
        dk = jnp.where(lo, dkd[:, :128], dkd[:, 128:])
        dv = jnp.where(lo, dvd[:, :128], dvd[:, 128:])
        dxq, dgq = back(dq * SCALE, h[:, :BQ_W], gq, cq, smq, spq)
        dxk, dgk = back(dk, h[:, BQ_W:BQ_W + BKV_W], gk, ck, smk, spk)
        return jnp.concatenate([dxq, dxk, dv], 1), dgq, dgk

    ins = [(dq, _rows(tm, BQ_W)), (dkd, _rows(tm, 2 * BKV_W)), (dvd, _rows(tm, 2 * BKV_W)), (hb, _rows(tm, QKV_W)),
           (gq, _whole(gq)), (gk, _whole(gk))]
    ins += [(t, _rows(tm, BQ_W)) for t in tab_q] + [(t, _rows(tm, BKV_W)) for t in tab_k]
    return _rowwise(fn, "post_b", s, tm, ins, [(QKV_W, _MXU, "row"), (BQ_W, F32, "acc"), (BKV_W, F32, "acc")])


def _post_c(dq, dk, dv, tm):
    def fn(dq, dk, dv):
        return (jnp.concatenate([dq * SCALE, dk, dv], 1),)

    return _rowwise(fn, "post_c", dq.shape[0], tm, [(t, _rows(tm, C_W)) for t in (dq, dk, dv)], [(QKV_W, _MXU, "row")])[0]


def _adamw(w, g, m, v, name):
    rows, cols = w.shape
    tm = _tile(rows, 256, 8)

    def fn(w, g, m, v):
        m = ADAM_B1 * m + (1.0 - ADAM_B1) * g
        v = ADAM_B2 * v + (1.0 - ADAM_B2) * (g * g)
        m_hat = m / (1.0 - ADAM_B1 ** ADAM_STEP)
        v_hat = v / (1.0 - ADAM_B2 ** ADAM_STEP)
        delta = -ADAM_LR * (m_hat / (jnp.sqrt(v_hat) + ADAM_EPS) + ADAM_WD * w)
        return delta, m, v

    return _rowwise(fn, name, rows, tm, [(t, _rows(tm, cols)) for t in (w, g, m, v)], [(cols, F32, "row")] * 3)


def _dot_t(a, b):
    return lax.dot_general(a, b, (((1,), (1,)), ((), ())), preferred_element_type=F32)


def _tdot(a, b):
    return lax.dot_general(a, b, (((0,), (0,)), ((), ())), preferred_element_type=F32)


def _head_masks():
    lo = _lane_lo()
    return lo, (lo, jnp.logical_not(lo))


def _rep(x, rows):
    return jnp.broadcast_to(x, (rows, 128))


def _row_lo():
    return lax.broadcasted_iota(jnp.int32, (128, 1), 0) < HEAD


def _flash_fwd(q, kd, v1, tq, tk):
    s = q.shape[0]
    tq, tk = _tile(s, tq), _tile(s, tk)
    nk = s // tk
    mx = _MXU

    def body(q_ref, k_ref, v_ref, o_ref, lse_ref, m_ref, acc_ref):
        kk = pl.program_id(2)

        @pl.when(kk == 0)
        def _():
            m_ref[...] = jnp.full_like(m_ref, NEG)
            acc_ref[...] = jnp.zeros_like(acc_ref)

        q2, k2, v2 = q_ref[...], k_ref[...], v_ref[...]
        _, masks = _head_masks()
        hs = range(2)
        st = [_dot_t(k2, jnp.where(masks[h], q2, jnp.zeros_like(q2))) for h in hs]
        m_prev = [m_ref[h] for h in hs]
        m_new = [jnp.maximum(m_prev[h], jnp.max(st[h], 0, keepdims=True)) for h in hs]
        p = [jnp.exp2(st[h] - m_new[h]).astype(mx) for h in hs]
        pv = [_tdot(v2, p[h]) for h in hs]
        for h in hs:
            m_ref[h] = m_new[h]
            acc_ref[h] = acc_ref[h] * jnp.exp2(m_prev[h] - m_new[h]) + pv[h]

        @pl.when(kk == nk - 1)
        def _():
            a0, a1 = acc_ref[0], acc_ref[1]
            l0, l1 = a0[HEAD:HEAD + 1], a1[HEAD:HEAD + 1]
            o_ref[...] = jnp.concatenate([a0[:HEAD] / l0, a1[:HEAD] / l1], 0).T
            lse_ref[...] = jnp.concatenate([m_ref[0] + jnp.log2(l0), m_ref[1] + jnp.log2(l1), jnp.zeros((6, tq), F32)], 0)

    return pl.pallas_call(
        body, name="attn_b_fwd", grid=(BQ_W // 128, s // tq, nk),
        in_specs=[pl.BlockSpec((tq, 128), lambda j, i, kk: (i, j)),
                  pl.BlockSpec((tk, 128), lambda j, i, kk: (kk, j // 2)),
                  pl.BlockSpec((tk, 128), lambda j, i, kk: (kk, j // 2))],
        out_specs=[pl.BlockSpec((tq, 128), lambda j, i, kk: (i, j)), pl.BlockSpec((None, 8, tq), lambda j, i, kk: (j, 0, i))],
        out_shape=[jax.ShapeDtypeStruct((s, BQ_W), F32), jax.ShapeDtypeStruct((BQ_W // 128, 8, s), F32)],
        scratch_shapes=[pltpu.VMEM((2, 1, tq), F32), pltpu.VMEM((2, 128, tq), F32)],
        compiler_params=_params("parallel", "parallel", "arbitrary"),
    )(q, kd, v1)


def _delta_b(do, o, tq):
    s = do.shape[0]
    tq = _tile(s, tq)

    def body(do_ref, o_ref, d_ref):
        prod = do_ref[...] * o_ref[...]
        row = lax.broadcasted_iota(jnp.int32, (8, 128), 0)
        lane = lax.broadcasted_iota(jnp.int32, (8, 128), 1)
        sel = jnp.where(((row == 0) & (lane < HEAD)) | ((row == 1) & (lane >= HEAD)), 1.0, 0.0).astype(F32)
        d_ref[...] = lax.dot_general(sel, prod, (((1,), (1,)), ((), ())), preferred_element_type=F32,
                                     precision=lax.Precision.HIGHEST)

    qs = pl.BlockSpec((tq, 128), lambda j, i: (i, j))
    return pl.pallas_call(
        body, name="attn_b_delta", grid=(BQ_W // 128, s // tq), in_specs=[qs, qs],
        out_specs=pl.BlockSpec((None, 8, tq), lambda j, i: (j, 0, i)),
        out_shape=jax.ShapeDtypeStruct((BQ_W // 128, 8, s), F32),
        compiler_params=_params("parallel", "parallel"),
    )(do, o)


def _flash_bwd(q, kd, vd, do, lse, delta, tq, tk):
    s = q.shape[0]
    tq, tk = _tile(s, tq), _tile(s, tk)
    nq, nk = s // tq, s // tk
    group = BQ_W // 128 // 2
    mx = _MXU

    def body(k_ref, v_ref, q_ref, do_ref, lse_ref, dl_ref, dq_hbm, dk_ref, dv_ref, dk_acc, dv_acc, dqt, stage, sem):
        e, kk, jj, i = pl.program_id(0), pl.program_id(1), pl.program_id(2), pl.program_id(3)

        @pl.when((jj == 0) & (i == 0))
        def _():
            dk_acc[...] = jnp.zeros_like(dk_acc)
            dv_acc[...] = jnp.zeros_like(dv_acc)

        @pl.when(kk == 0)
        def _():
            dqt[jj, i] = jnp.zeros((128, tq), F32)

        q2, k2, v2, do2 = q_ref[...], k_ref[...], v_ref[...], do_ref[...].astype(mx)
        lse8, dl8 = lse_ref[...], dl_ref[...]
        _, masks = _head_masks()
        hs = range(2)
        qh = [jnp.where(masks[h], q2, jnp.zeros_like(q2)) for h in hs]
        doh = [jnp.where(masks[h], do2, jnp.zeros_like(do2)) for h in hs]
        st = [_dot_t(k2, qh[h]) for h in hs]
        dpt = [_dot_t(v2, doh[h]) for h in hs]
        p = [jnp.exp2(st[h] - lse8[h:h + 1]) for h in hs]
        ds = [(p[h] * (dpt[h] - dl8[h:h + 1])).astype(mx) for h in hs]
        p = [p[h].astype(mx) for h in hs]
        dv_acc[...] += jnp.dot(p[0], doh[0], preferred_element_type=F32) + jnp.dot(p[1], doh[1], preferred_element_type=F32)
        dk_acc[...] += jnp.dot(ds[0], qh[0], preferred_element_type=F32) + jnp.dot(ds[1], qh[1], preferred_element_type=F32)
        dqt[jj, i] += jnp.where(_row_lo(), _tdot(k2, ds[0]), _tdot(k2, ds[1]))

        @pl.when(kk == nk - 1)
        def _():
            stage[...] = dqt[jj, i].T
            lane0 = pl.multiple_of((group * e + jj) * 128, 128)
            cp = pltpu.make_async_copy(stage, dq_hbm.at[pl.ds(pl.multiple_of(i * tq, tq), tq), pl.ds(lane0, 128)], sem)
            cp.start()
            cp.wait()

        @pl.when((jj == group - 1) & (i == nq - 1))
        def _():
            dk_ref[...] = (dk_acc[...] + pltpu.roll(dk_acc[...], HEAD, 1)) * LN2
            dv_ref[...] = dv_acc[...] + pltpu.roll(dv_acc[...], HEAD, 1)

    ks = pl.BlockSpec((tk, 128), lambda e, kk, jj, i: (kk, e))
    qs = pl.BlockSpec((tq, 128), lambda e, kk, jj, i: (i, group * e + jj))
    st = pl.BlockSpec((None, 8, tq), lambda e, kk, jj, i: (group * e + jj, 0, i))
    return pl.pallas_call(
        body, name="attn_b_bwd", grid=(BKV_W // HEAD, nk, group, nq),
        in_specs=[ks, ks, qs, qs, st, st], out_specs=[ANY, ks, ks],
        out_shape=[jax.ShapeDtypeStruct((s, BQ_W), F32)] + [jax.ShapeDtypeStruct((s, 2 * BKV_W), F32)] * 2,
        scratch_shapes=[pltpu.VMEM((tk, 128), F32)] * 2 + [pltpu.VMEM((group, nq, 128, tq), F32), pltpu.VMEM((tq, 128), F32),
                                                          pltpu.SemaphoreType.DMA],
        compiler_params=_params("parallel", "arbitrary", "arbitrary", "arbitrary"),
    )(kd, vd, q, do, lse, delta)


def _p_and_ds(items, masks):
    mx = _MXU
    keys = [(n, h) for n in range(len(items)) for h in range(2)]
    qh = {(n, h): jnp.where(masks[h], items[n][0], jnp.zeros_like(items[n][0])) for n, h in keys}
    doh = {(n, h): jnp.where(masks[h], items[n][3], jnp.zeros_like(items[n][3])) for n, h in keys}
    sc = {}
    for n, h in keys:
        s_h = _dot_t(qh[n, h], items[n][1])
        if items[n][7] is not None:
            s_h = s_h + items[n][7][h]
        sc[n, h] = jnp.where(items[n][6], s_h, NEG)
    dp = {(n, h): _dot_t(doh[n, h].astype(mx), items[n][2]) for n, h in keys}
    lse = {(n, h): jnp.max(items[n][5][:, h * 128:(h + 1) * 128], -1, keepdims=True) for n, h in keys}
    delta = {(n, h): jnp.sum(doh[n, h] * items[n][4], -1, keepdims=True) for n, h in keys}
    p = {key: jnp.exp(sc[key] - lse[key]) for key in keys}
    ds = {key: p[key] * (dp[key] - delta[key]) for key in keys}
    return [[(qh[n, h], p[n, h], ds[n, h], doh[n, h]) for h in range(2)] for n in range(len(items))]


class _BandA:
    hb, has_bias, name = 1, False, "a"

    def __init__(self, nb):
        self.nb = nb

    def mask(self, qidx, kidx):
        n = self.nb * BAND
        return (jnp.abs(qidx - kidx) <= A_RADIUS) & (kidx >= 0) & (kidx < n) & (qidx >= 0) & (qidx < n)


class _BandC:
    hb, has_bias, name = 3, True, "c"

    def __init__(self, nb):
        self.nb = nb
        self.rows = nb * BAND // GRID_W
        per = BAND // GRID_W
        assert self.rows >= C_ROWS and (C_ROWS - 1) // per <= self.hb
        assert (self.rows - 1) // per - (self.rows - C_ROWS) // per <= self.hb

    def mask(self, qidx, kidx):
        n = self.nb * BAND
        sh = GRID_W.bit_length() - 1
        qrow, cq = qidx >> sh, qidx & (GRID_W - 1)
        krow, ck = kidx >> sh, kidx & (GRID_W - 1)
        r0 = jnp.clip(qrow - C_ROWS // 2, 0, self.rows - C_ROWS)
        c0 = jnp.clip(cq - C_COLS // 2, 0, GRID_W - C_COLS)
        ok = (qidx >= 0) & (qidx < n) & (kidx >= 0) & (kidx < n)
        return ok & (krow >= r0) & (krow < r0 + C_ROWS) & (ck >= c0) & (ck < c0 + C_COLS)


def _bias_tile(off, a):
    return (BAND // GRID_W) * off - a + (C_ROWS - 1) + 2


def _band_bias_k(band, bt_ref, h):
    per = BAND // GRID_W
    return jnp.concatenate([jnp.concatenate([bt_ref[h, _bias_tile(off, a)] for off in range(-band.hb, band.hb + 1)], 1)
                            for a in range(per)], 0)


def _band_bias_q(band, bt_ref, h):
    per = BAND // GRID_W
    return jnp.concatenate([bt_ref[h, _bias_tile(-off, a)] for off in range(-band.hb, band.hb + 1) for a in range(per)], 0)


def _band_split(nb, ncb):
    cb = max(c for c in (4, 2, 1) if ncb % c == 0)
    rb = max(r for r in (4, 2, 1) if nb % r == 0 and r * cb <= 8)
    return rb, cb


def _band_specs(band, rb, cb, nb, width):
    def edge(first):
        return pl.BlockSpec((BAND, cb * width), lambda c, i: (jnp.clip(i * rb + first, 0, nb - 1), c))

    main = pl.BlockSpec((rb * BAND, cb * width), lambda c, i: (i, c))
    return [edge(t - band.hb) for t in range(band.hb)] + [main] + [edge(rb + t) for t in range(band.hb)]


def _band_rows(band, refs, rb, r, lanes):
    hb = band.hb
    parts = []
    for b in range(r, r + 2 * hb + 1):
        if b < hb:
            parts.append(refs[b][:, lanes])
        elif b < hb + rb:
            parts.append(refs[hb][(b - hb) * BAND:(b - hb + 1) * BAND, lanes])
        else:
            parts.append(refs[b - rb + 1][:, lanes])
    return jnp.concatenate(parts, 0)


def _band_idx(band, blk, rows_of_blocks, axis):
    shape = (rows_of_blocks * BAND, 1) if axis == 0 else (1, rows_of_blocks * BAND)
    return blk * BAND + lax.broadcasted_iota(jnp.int32, shape, axis)


def _band_fwd(band, q, k, v, bt=None):
    n, w = q.shape
    nb, ncb, nband = n // BAND, w // 128, 2 * band.hb + 1
    rb, cb = _band_split(nb, ncb)
    mx = _MXU
    raw = not band.has_bias

    def body(*refs):
        q_ref, k_refs, v_refs = refs[0], refs[1:1 + nband], refs[1 + nband:1 + 2 * nband]
        rest = refs[1 + 2 * nband:]
        bt_ref = rest[0] if band.has_bias else None
        outs = rest[1:] if band.has_bias else rest
        i = pl.program_id(1)
        lo, masks = _head_masks()
        subs = [(r, c) for r in range(rb) for c in range(cb)]
        mask = {r: band.mask(_band_idx(band, i * rb + r, 1, 0), _band_idx(band, i * rb + r - band.hb, nband, 1)) for r in range(rb)}
        lanes = {c: slice(c * 128, (c + 1) * 128) for c in range(cb)}
        rows = {r: slice(r * BAND, (r + 1) * BAND) for r in range(rb)}
        sc = {}
        for r, c in subs:
            q2, kcat = q_ref[rows[r], lanes[c]], _band_rows(band, k_refs, rb, r, lanes[c])
            for h in range(2):
                s_h = _dot_t(jnp.where(masks[h], q2, jnp.zeros_like(q2)), kcat)
                if band.has_bias:
                    s_h = s_h + _band_bias_k(band, bt_ref, 2 * c + h)
                sc[r, c, h] = jnp.where(mask[r], s_h, NEG)
        ms = {key: jnp.max(s_h, -1, keepdims=True) for key, s_h in sc.items()}
        ps = {key: jnp.exp(s_h - ms[key]) for key, s_h in sc.items()}
        ls = {key: jnp.sum(p, -1, keepdims=True) for key, p in ps.items()}
        os_ = {}
        for r, c in subs:
            vcat = _band_rows(band, v_refs, rb, r, lanes[c])
            for h in range(2):
                os_[r, c, h] = jnp.dot(ps[r, c, h].astype(mx), vcat, preferred_element_type=F32)
        for r, c in subs:
            st_lanes = [slice(c * 256 + h * 128, c * 256 + (h + 1) * 128) for h in range(2)]
            if raw:
                o_ref, m_ref, l_ref = outs
                o_ref[rows[r], lanes[c]] = jnp.where(lo, os_[r, c, 0], os_[r, c, 1])
                for h in range(2):
                    m_ref[rows[r], st_lanes[h]] = _rep(ms[r, c, h], BAND)
                    l_ref[rows[r], st_lanes[h]] = _rep(ls[r, c, h], BAND)
            else:
                o_ref, lse_ref = outs
                o_ref[rows[r], lanes[c]] = jnp.where(lo, os_[r, c, 0] / ls[r, c, 0], os_[r, c, 1] / ls[r, c, 1])
                for h in range(2):
                    lse_ref[rows[r], st_lanes[h]] = _rep(ms[r, c, h] + jnp.log(ls[r, c, h]), BAND)

    qs = pl.BlockSpec((rb * BAND, cb * 128), lambda c, i: (i, c))
    ks = _band_specs(band, rb, cb, nb, 128)
    st = pl.BlockSpec((rb * BAND, cb * 256), lambda c, i: (i, c))
    in_specs, args = [qs] + ks + ks, [q] + [k] * nband + [v] * nband
    if band.has_bias:
        in_specs.append(pl.BlockSpec((2 * cb, BT_TILES, GRID_W, 128), lambda c, i: (c, 0, 0, 0)))
        args.append(bt)
    n_stats = 2 if raw else 1
    return pl.pallas_call(
        body, name="attn_%s_fwd" % band.name, grid=(ncb // cb, nb // rb), in_specs=in_specs,
        out_specs=[qs] + [st] * n_stats,
        out_shape=[jax.ShapeDtypeStruct((n, w), F32)] + [jax.ShapeDtypeStruct((n, 2 * w), F32)] * n_stats,
        compiler_params=_params("parallel", "arbitrary"),
    )(*args)


def _band_dq(band, q, k, v, do, o, lse, bt=None):
    n, w = q.shape
    nb, ncb, nband = n // BAND, w // 128, 2 * band.hb + 1
    rb, cb = _band_split(nb, ncb)
    mx = _MXU
    per = BAND // GRID_W

    def body(*refs):
        q_ref, k_refs, v_refs = refs[0], refs[1:1 + nband], refs[1 + nband:1 + 2 * nband]
        do_ref, o_ref, lse_ref = refs[1 + 2 * nband:4 + 2 * nband]
        rest = refs[4 + 2 * nband:]
        dq_ref = rest[1] if band.has_bias else rest[0]
        i = pl.program_id(1)
        lo, masks = _head_masks()
        if band.has_bias:
            dbt_ref = rest[2]

            @pl.when(i == 0)
            def _():
                dbt_ref[...] = jnp.zeros_like(dbt_ref)

        subs = [(r, c) for r in range(rb) for c in range(cb)]
        mask = {r: band.mask(_band_idx(band, i * rb + r, 1, 0), _band_idx(band, i * rb + r - band.hb, nband, 1)) for r in range(rb)}
        items, kcats = [], []
        for r, c in subs:
            lanes, rows = slice(c * 128, (c + 1) * 128), slice(r * BAND, (r + 1) * BAND)
            kcats.append(_band_rows(band, k_refs, rb, r, lanes))
            bias = [_band_bias_k(band, rest[0], 2 * c + h) for h in range(2)] if band.has_bias else None
            items.append((q_ref[rows, lanes], kcats[-1], _band_rows(band, v_refs, rb, r, lanes), do_ref[rows, lanes],
                          o_ref[rows, lanes], lse_ref[rows, c * 256:(c + 1) * 256], mask[r], bias))
        res = _p_and_ds(items, masks)
        dqs = [[jnp.dot(ds.astype(mx), kcat, preferred_element_type=F32) for _, _, ds, _ in hs] for hs, kcat in zip(res, kcats)]
        for (r, c), hs, dq in zip(subs, res, dqs):
            dq_ref[r * BAND:(r + 1) * BAND, c * 128:(c + 1) * 128] = jnp.where(lo, dq[0], dq[1])
            if band.has_bias:
                for h in range(2):
                    ds = hs[h][2]
                    for a in range(per):
                        for t in range(nband):
                            tile = ds[a * GRID_W:(a + 1) * GRID_W, t * 128:(t + 1) * 128]
                            dbt_ref[2 * c + h, _bias_tile(t - band.hb, a)] += tile

    qs = pl.BlockSpec((rb * BAND, cb * 128), lambda c, i: (i, c))
    ks = _band_specs(band, rb, cb, nb, 128)
    st = pl.BlockSpec((rb * BAND, cb * 256), lambda c, i: (i, c))
    in_specs, args = [qs] + ks + ks + [qs, qs, st], [q] + [k] * nband + [v] * nband + [do, o, lse]
    out_specs, out_shape = [qs], [jax.ShapeDtypeStruct((n, w), F32)]
    if band.has_bias:
        bts = pl.BlockSpec((2 * cb, BT_TILES, GRID_W, 128), lambda c, i: (c, 0, 0, 0))
        in_specs.append(bts)
        args.append(bt)
        out_specs.append(bts)
        out_shape.append(jax.ShapeDtypeStruct(bt.shape, F32))
    return pl.pallas_call(
        body, name="attn_%s_dq" % band.name, grid=(ncb // cb, nb // rb), in_specs=in_specs, out_specs=out_specs,
        out_shape=out_shape, compiler_params=_params("parallel", "arbitrary"),
    )(*args)


def _band_dkv(band, q, k, v, do, o, lse, bt=None):
    n, w = q.shape
    nb, ncb, nband = n // BAND, w // 128, 2 * band.hb + 1
    rb, cb = _band_split(nb, ncb)
    mx = _MXU

    def body(*refs):
        k_ref, v_ref = refs[0], refs[1]
        q_refs, do_refs, o_refs, lse_refs = [refs[2 + g * nband:2 + (g + 1) * nband] for g in range(4)]
        rest = refs[2 + 4 * nband:]
        dk_ref, dv_ref = rest[-2], rest[-1]
        i = pl.program_id(1)
        _, masks = _head_masks()
        subs = [(r, c) for r in range(rb) for c in range(cb)]
        mask = {r: band.mask(_band_idx(band, i * rb + r - band.hb, nband, 0), _band_idx(band, i * rb + r, 1, 1)) for r in range(rb)}
        items = []
        for r, c in subs:
            lanes, rows = slice(c * 128, (c + 1) * 128), slice(r * BAND, (r + 1) * BAND)
            qcat, docat, ocat = [_band_rows(band, g, rb, r, lanes) for g in (q_refs, do_refs, o_refs)]
            lsecat = _band_rows(band, lse_refs, rb, r, slice(c * 256, (c + 1) * 256))
            bias = [_band_bias_q(band, rest[0], 2 * c + h) for h in range(2)] if band.has_bias else None
            items.append((qcat, k_ref[rows, lanes], v_ref[rows, lanes], docat, ocat, lsecat, mask[r], bias))
        res = _p_and_ds(items, masks)
        dks = [sum(_tdot(ds.astype(mx), qh) for qh, _, ds, _ in hs) for hs in res]
        dvs = [sum(_tdot(p.astype(mx), doh.astype(mx)) for _, p, _, doh in hs) for hs in res]
        for (r, c), dk, dv in zip(subs, dks, dvs):
            dk_ref[r * BAND:(r + 1) * BAND, c * 128:(c + 1) * 128] = dk
            dv_ref[r * BAND:(r + 1) * BAND, c * 128:(c + 1) * 128] = dv

    ks = pl.BlockSpec((rb * BAND, cb * 128), lambda c, i: (i, c))
    in_specs = [ks, ks] + _band_specs(band, rb, cb, nb, 128) * 3 + _band_specs(band, rb, cb, nb, 256)
    args = [k, v] + [q] * nband + [do] * nband + [o] * nband + [lse] * nband
    if band.has_bias:
        in_specs.append(pl.BlockSpec((2 * cb, BT_TILES, GRID_W, 128), lambda c, i: (c, 0, 0, 0)))
        args.append(bt)
    return pl.pallas_call(
        body, name="attn_%s_dkv" % band.name, grid=(ncb // cb, nb // rb), in_specs=in_specs, out_specs=[ks, ks],
        out_shape=[jax.ShapeDtypeStruct((n, w), F32)] * 2,
        compiler_params=_params("parallel", "arbitrary"),
    )(*args)


def _dc_onehot():
    c = np.arange(GRID_W)
    dc = np.clip(c[None, :] - c[:, None] + (C_COLS - 1), 0, 2 * C_COLS - 2).reshape(-1)
    m = np.zeros((GRID_W * GRID_W, 128), np.float32)
    m[np.arange(dc.size), dc] = 1.0
    return m


def _bias_tiles(rpb):
    h, nr, ncol = rpb.shape
    flat = jnp.pad(rpb.reshape(h * nr, ncol), ((0, (-h * nr) % 8), (0, 128 - ncol)))
    tiles = _mm(flat, jnp.asarray(_dc_onehot().T), name="rpb_tiles", exact=True, tn=GRID_W * GRID_W)
    tiles = tiles[:h * nr].reshape(h, nr, GRID_W, GRID_W)
    tiles = jnp.pad(tiles, ((0, 0), (2, BT_TILES + 1 - nr - 2), (0, 0), (0, 0)))
    return jnp.concatenate([tiles[:, :BT_TILES], tiles[:, 1:BT_TILES + 1]], -1)


def _bias_tiles_grad(dbt, nr, ncol):
    h = dbt.shape[0]
    d = dbt[:, 2:2 + nr, :, :GRID_W] + dbt[:, 1:1 + nr, :, GRID_W:]
    flat = jnp.pad(d.reshape(h * nr, GRID_W * GRID_W), ((0, (-h * nr) % 8), (0, 0)))
    g = _mm(flat, jnp.asarray(_dc_onehot()), name="rpb_grad", exact=True, tk=GRID_W * GRID_W)
    return g[:h * nr, :ncol].reshape(h, nr, ncol)


TM = 256
TQ_B, TK_B = 1024, 2048


def _relu2(acc):
    r = jnp.maximum(acc, 0.0)
    return (r * r,)


def _layer_fwd(x, xb, w, sm, tabs, alpha):
    tab_a, tab_q, tab_k = tabs
    s, d = x.shape
    ha = _mm(xb, w["in"], name="in_a", tn=QKV_W, b_cols=QKV_W, b_off=0)
    hb = _mm(xb, w["in"], name="in_b", tn=QKV_W, b_cols=QKV_W, b_off=1)
    hc = _mm(xb, w["in"], name="in_c", tn=QKV_W, b_cols=QKV_W, b_off=2)
    hg = _mm(xb, w["in"], name="in_g", outs=(_MXU,), tn=QKV_W, b_cols=3 * d, b_off=3)

    qa, ka, va = _prep_a(ha, tab_a, TM)
    stats = [_band_fwd(_BandA(s // dil // BAND), qa[dil], ka[dil], va[dil]) for dil in A_DILATIONS]
    oas, lse_a = _combine_a(*zip(*stats), s, TM)
    oa = oas[1]

    qb, kd, vd, v1 = _prep_b(hb, sm["q_norm"], sm["k_norm"], tab_q, tab_k, TM)
    ob, lse_b = _flash_fwd(qb, kd, v1, TQ_B, TK_B)

    qc, kc, vc = _prep_c(hc, TM)
    bt = _bias_tiles(sm["rpb"])
    oc, lse_c = _band_fwd(_BandC(s // BAND), qc, kc, vc, bt)

    pa = _mm(oa, w["br_a"], name="br_a", outs=(_MXU,))
    pb = _mm(ob, w["br_b"], name="br_b", outs=(_MXU,))
    pc = _mm(oc, w["br_c"], name="br_c", outs=(_MXU,))
    merged = _gate_merge(hg, sm["b_gate"], pa, pb, pc, TM)
    ln = dict(outs=(F32, F32, _MXU), epilogue=_ln_epilogue(alpha), tm=512, tn=d)
    r1, x1, x1b = _mm(merged, w["out"], name="w_out_ln1", extras=(x, sm["ln1_g"], sm["ln1_b"]), **ln)
    act = _mm(x1b, w["up"], name="w_up", outs=(_MXU,), epilogue=_relu2)
    r2, x2, x2b = _mm(act, w["down"], name="w_down_ln2", extras=(x1, sm["ln2_g"], sm["ln2_b"]), **ln)
    saved = dict(xb=xb, hb=hb, hg=hg, qa=qa, ka=ka, va=va, oa=oa, oas=oas, lse_a=lse_a, qb=qb, kd=kd, vd=vd, ob=ob, lse_b=lse_b,
                 qc=qc, kc=kc, vc=vc, oc=oc, lse_c=lse_c, bt=bt, pa=pa, pb=pb, pc=pc, merged=merged, r1=r1, x1b=x1b,
                 act=act, r2=r2)
    return x2, x2b, saved


def _layer_bwd(dx2, w, sm, sv, tabs, alpha):
    tab_a, tab_q, tab_k = tabs
    s, d = dx2.shape
    g = {}
    dr2, dr2b, dg2, db2 = _ln_bwd(dx2, sv["r2"], sm["ln2_g"], "ln2_bwd", TM)
    g["ln2_g"], g["ln2_b"] = dg2.sum(0), db2.sum(0)
    du = _mm(dr2b, w["down"], mode="nt", name="d_act", outs=(_MXU,), extras=(sv["act"],),
             epilogue=lambda acc, act: (acc * (2.0 * jnp.sqrt(act.astype(F32))),))
    g["w_down"] = _mm(sv["act"], dr2b, mode="tn", name="g_w_down")
    g["w_up"] = _mm(sv["x1b"], du, mode="tn", name="g_w_up")
    dx1 = _mm(du, w["up"], mode="nt", name="d_x1", extras=(dr2,), epilogue=lambda acc, e: (acc + alpha * e,))
    dr1, dr1b, dg1, db1 = _ln_bwd(dx1, sv["r1"], sm["ln1_g"], "ln1_bwd", TM)
    g["ln1_g"], g["ln1_b"] = dg1.sum(0), db1.sum(0)
    g["w_out"] = _mm(sv["merged"], dr1b, mode="tn", name="g_w_out")
    dmerged = _mm(dr1b, w["out"], mode="nt", name="d_merged")
    dpa, dpb, dpc, dlog, gb = _gate_bwd(dmerged, sv["hg"], sm["b_gate"], sv["pa"], sv["pb"], sv["pc"], TM)
    g["b_gate"] = gb.sum(0)
    g["w_branch_a"] = _mm(sv["oa"], dpa, mode="tn", name="g_br_a")
    g["w_branch_b"] = _mm(sv["ob"], dpb, mode="tn", name="g_br_b")
    g["w_branch_c"] = _mm(sv["oc"], dpc, mode="tn", name="g_br_c")
    doa = _mm(dpa, w["br_a"], mode="nt", name="d_oa")
    dob = _mm(dpb, w["br_b"], mode="nt", name="d_ob")
    doc = _mm(dpc, w["br_c"], mode="nt", name="d_oc")

    dqs, dks, dvs = [], [], []
    doas = _to_dilations(doa, "d_oa_layouts", TM)
    for dil in A_DILATIONS:
        band = _BandA(s // dil // BAND)
        args = [t[dil] for t in (sv["qa"], sv["ka"], sv["va"], doas, sv["oas"], sv["lse_a"])]
        dqs.append(_band_dq(band, *args)[0])
        dk_c, dv_c = _band_dkv(band, *args)
        dks.append(dk_c)
        dvs.append(dv_c)
    dha = _post_a(dqs, dks, dvs, tab_a, s, TM)

    dqb, dkd, dvd = _flash_bwd(sv["qb"], sv["kd"], sv["vd"], dob, sv["lse_b"], _delta_b(dob, sv["ob"], TQ_B), TQ_B, TK_B)
    dhb, gq, gk = _post_b(dqb, dkd, dvd, sv["hb"], sm["q_norm"], sm["k_norm"], tab_q, tab_k, TM)
    g["q_norm_b"] = gq.sum(0).reshape(-1, HEAD).sum(0)
    g["k_norm_b"] = gk.sum(0).reshape(-1, HEAD).sum(0)

    band_c = _BandC(s // BAND)
    cargs = (sv["qc"], sv["kc"], sv["vc"], doc, sv["oc"], sv["lse_c"], sv["bt"])
    dqc, dbt = _band_dq(band_c, *cargs)
    dkc, dvc = _band_dkv(band_c, *cargs)
    dhc = _post_c(dqc, dkc, dvc, TM)
    g["rpb_c"] = _bias_tiles_grad(dbt, 2 * C_ROWS - 1, 2 * C_COLS - 1)

    xb = sv["xb"]
    g["w_in"] = jnp.concatenate([_mm(xb, dh, mode="tn", name="g_in_" + nm)
                                 for nm, dh in (("a", dha), ("b", dhb), ("c", dhc), ("g", dlog))], 1)
    dx = _mm(dha, w["in"], mode="nt", name="d_x_a", tk=QKV_W, b_cols=QKV_W, b_off=0, extras=(dr1,),
             epilogue=lambda acc, e: (acc + alpha * e,))
    for nm, dh, off in (("b", dhb, 1), ("c", dhc, 2), ("g", dlog, 3)):
        dx = _mm(dh, w["in"], mode="nt", name="d_x_" + nm, tk=QKV_W, b_cols=dh.shape[1], b_off=off, extras=(dx,),
                 epilogue=lambda acc, e: (acc + e,))
    return dx, g


BIG = ("w_in", "w_branch_a", "w_branch_b", "w_branch_c", "w_out", "w_up", "w_down")
ROW_SHARDED = ("w_out", "w_down")
SMALL = ("b_gate", "q_norm_b", "k_norm_b", "rpb_c", "ln1_g", "ln1_b", "ln2_g", "ln2_b")


def _local_step(x, target, wfull, small):
    s, d = x.shape
    depth = wfull["w_in"].shape[0]
    alpha = (2 * depth) ** 0.25
    tabs = _tables(s)
    ws, sms = [], []
    names = dict(w_in="in", w_branch_a="br_a", w_branch_b="br_b", w_branch_c="br_c", w_out="out", w_up="up", w_down="down")
    for l in range(depth):
        ws.append({short: (wfull[n], l) for n, short in names.items()})
        sms.append(dict(b_gate=small["b_gate"][l][None], q_norm=jnp.tile(small["q_norm_b"][l], BQ_W // HEAD)[None],
                        k_norm=jnp.tile(small["k_norm_b"][l], BKV_W // HEAD)[None], rpb=small["rpb_c"][l],
                        ln1_g=small["ln1_g"][l][None], ln1_b=small["ln1_b"][l][None],
                        ln2_g=small["ln2_g"][l][None], ln2_b=small["ln2_b"][l][None]))
    saved = []
    h, hb = x, x.astype(_MXU)
    for l in range(depth):
        h, hb, sv = _layer_fwd(h, hb, ws[l], sms[l], tabs, alpha)
        saved.append(sv)
    sq, dy = _loss_head(h, target, TM)
    grads = [None] * depth
    for l in reversed(range(depth)):
        dy, grads[l] = _layer_bwd(dy, ws[l], sms[l], saved[l], tabs, alpha)
    stacked = {k: jnp.stack([gl[k] for gl in grads]) for k in grads[0]}
    return sq, dy, stacked


def _place():
    return lax.axis_index("x"), lax.axis_index("y"), lax.axis_index("c")


def _flip(a, b):
    return a + b - 2 * a * b


def _other_chips(x, y):
    return [(1 - x, y), (x, 1 - y), (1 - x, 1 - y)]


def _gather_shards(flat):
    r = flat.shape[0]
    rh = r // 2

    def body(src, out, send_sems, recv_sems):
        x, y, c = _place()
        me, sibling = (x, y), (x, y, 1 - c)
        n1, n2, dg = (_flip(x, 1 - c), _flip(y, c)), (_flip(x, c), _flip(y, 1 - c)), (1 - x, 1 - y)

        def half(chip, hc):
            return out.at[2 * chip[0] + chip[1], pl.ds(hc * rh, rh), :]

        def copy(k, src_ref, dst_ref, to):
            return pltpu.make_async_remote_copy(src_ref=src_ref, dst_ref=dst_ref, send_sem=send_sems.at[k],
                                                recv_sem=recv_sems.at[k], device_id=to, device_id_type=MESH)

        own = src.at[pl.ds(c * rh, rh), :]
        sends = [copy(0, own, half(me, c), (*n1, c)), copy(1, own, half(me, c), (*n2, c)), copy(6, src, out.at[2 * x + y], sibling)]
        for cp in sends:
            cp.start()
        for k, chip, j in ((0, n1, c), (1, n2, 1 - c), (2, dg, 2)):
            copy(k, half(chip, c), half(chip, c), sibling).wait_recv()
            if k == 0:
                sends.append(copy(2, half(n1, c), half(n1, c), (*n2, c)))
                sends[-1].start()
            sends.append(copy(3 + j, half(chip, c), half(chip, c), sibling))
            sends[-1].start()
        for j, chip in enumerate(_other_chips(x, y)):
            copy(3 + j, half(chip, 1 - c), half(chip, 1 - c), sibling).wait_recv()
        copy(6, src, out.at[2 * x + y], sibling).wait_recv()
        for cp in sends:
            cp.wait_send()

    return pl.pallas_call(
        body, name="gather_weights", in_specs=[ANY], out_specs=ANY,
        out_shape=jax.ShapeDtypeStruct((4, r, 128), flat.dtype),
        scratch_shapes=[pltpu.SemaphoreType.DMA((7,)), pltpu.SemaphoreType.DMA((7,))],
    )(flat)


def _pair_exchange(part):
    _, r, _ = part.shape
    rh = r // 2

    def body(src, out, send_sem, recv_sem):
        x, y, c = _place()
        cp = pltpu.make_async_remote_copy(src_ref=src.at[:, pl.ds((1 - c) * rh, rh), :], dst_ref=out, send_sem=send_sem,
                                          recv_sem=recv_sem, device_id=(x, y, 1 - c), device_id_type=MESH)
        cp.start()
        cp.wait()

    return pl.pallas_call(
        body, name="grad_pair_exchange", in_specs=[ANY], out_specs=ANY,
        out_shape=jax.ShapeDtypeStruct((4, rh, 128), part.dtype),
        scratch_shapes=[pltpu.SemaphoreType.DMA, pltpu.SemaphoreType.DMA],
    )(part)


def _chip_exchange(t):
    _, rh, _ = t.shape

    def body(src, out, send_sems, recv_sems):
        x, y, c = _place()
        cps = [pltpu.make_async_remote_copy(src_ref=src.at[2 * chip[0] + chip[1]], dst_ref=out.at[k], send_sem=send_sems.at[k],
                                            recv_sem=recv_sems.at[k], device_id=(*chip, c), device_id_type=MESH)
               for k, chip in enumerate(_other_chips(x, y))]
        for cp in cps:
            cp.start()
        for cp in cps:
            cp.wait()

    return pl.pallas_call(
        body, name="grad_chip_exchange", in_specs=[ANY], out_specs=ANY,
        out_shape=jax.ShapeDtypeStruct((3, rh, 128), t.dtype),
        scratch_shapes=[pltpu.SemaphoreType.DMA((3,)), pltpu.SemaphoreType.DMA((3,))],
    )(t)


def _pair_share(half):
    rh = half.shape[0]

    def body(src, out, send_sem, recv_sem):
        x, y, c = _place()
        cp = pltpu.make_async_remote_copy(src_ref=src, dst_ref=out, send_sem=send_sem, recv_sem=recv_sem,
                                          device_id=(x, y, 1 - c), device_id_type=MESH)
        cp.start()
        cp.wait()

    theirs = pl.pallas_call(
        body, name="grad_pair_share", in_specs=[ANY], out_specs=ANY,
        out_shape=jax.ShapeDtypeStruct((rh, 128), half.dtype),
        scratch_shapes=[pltpu.SemaphoreType.DMA, pltpu.SemaphoreType.DMA],
    )(half)

    tr = _tile(rh, 2048, 8)

    def join(c_ref, mine_ref, theirs_ref, o_ref):
        o_ref[...] = jnp.where(pl.program_id(0) == c_ref[0], mine_ref[...], theirs_ref[...])

    spec = pl.BlockSpec((tr, 128), lambda h, i, c_ref: (i, 0))
    c = jnp.reshape(lax.axis_index("c"), (1,)).astype(jnp.int32)
    return pl.pallas_call(
        join, name="grad_pair_join",
        grid_spec=pltpu.PrefetchScalarGridSpec(
            num_scalar_prefetch=1, grid=(2, rh // tr), in_specs=[spec, spec],
            out_specs=pl.BlockSpec((tr, 128), lambda h, i, c_ref: (h * (rh // tr) + i, 0))),
        out_shape=jax.ShapeDtypeStruct((2 * rh, 128), half.dtype),
        compiler_params=_params("parallel", "parallel"),
    )(c, half, theirs)


def _gather_all(v):
    r = v.shape[0]

    def body(src, out, send_sems, recv_sems, local_sem):
        x, y, c = _place()
        me = 4 * x + 2 * y + c
        mine = pltpu.make_async_copy(src, out.at[me], local_sem)
        mine.start()
        cps = []
        for k in range(1, 8):
            fx, fy, fc = (k >> 2) & 1, (k >> 1) & 1, k & 1
            peer = (x + fx - 2 * x * fx, y + fy - 2 * y * fy, c + fc - 2 * c * fc)
            cps.append(pltpu.make_async_remote_copy(src_ref=src, dst_ref=out.at[me], send_sem=send_sems.at[k - 1],
                                                    recv_sem=recv_sems.at[k - 1], device_id=peer, device_id_type=MESH))
        for cp in cps:
            cp.start()
        for k in range(1, 8):
            fx, fy, fc = (k >> 2) & 1, (k >> 1) & 1, k & 1
            frm = 4 * (x + fx - 2 * x * fx) + 2 * (y + fy - 2 * y * fy) + (c + fc - 2 * c * fc)
            pltpu.make_async_remote_copy(src_ref=src, dst_ref=out.at[frm], send_sem=send_sems.at[k - 1],
                                         recv_sem=recv_sems.at[k - 1], device_id=(x, y, c), device_id_type=MESH).wait_recv()
        for cp in cps:
            cp.wait_send()
        mine.wait()

    return pl.pallas_call(
        body, name="gather_small_grads", in_specs=[ANY], out_specs=ANY,
        out_shape=jax.ShapeDtypeStruct((8, r, 128), v.dtype),
        scratch_shapes=[pltpu.SemaphoreType.DMA((7,)), pltpu.SemaphoreType.DMA((7,)), pltpu.SemaphoreType.DMA],
    )(v)


def _sum_slots(parts, name):
    n, r, _ = parts.shape
    tr = _tile(r, 1024, 8)

    def body(p_ref, o_ref):
        acc = p_ref[0]
        for j in range(1, n):
            acc = acc + p_ref[j]
        o_ref[...] = acc

    return pl.pallas_call(
        body, name=name, grid=(r // tr,), in_specs=[pl.BlockSpec((n, tr, 128), lambda i: (0, i, 0))],
        out_specs=pl.BlockSpec((tr, 128), lambda i: (i, 0)), out_shape=jax.ShapeDtypeStruct((r, 128), parts.dtype),
        compiler_params=_params("parallel"),
    )(parts)


def _add_sibling_half(part, recv, c):
    _, rh, _ = recv.shape
    tr = _tile(rh, 1024, 8)
    nblk = rh // tr

    def body(c_ref, p_ref, r_ref, o_ref):
        o_ref[...] = (p_ref[...].astype(F32) + r_ref[...].astype(F32)).astype(o_ref.dtype)

    return pl.pallas_call(
        body, name="grad_pair_sum",
        grid_spec=pltpu.PrefetchScalarGridSpec(
            num_scalar_prefetch=1, grid=(4, nblk),
            in_specs=[pl.BlockSpec((None, tr, 128), lambda j, i, c_ref: (j, c_ref[0] * nblk + i, 0)),
                      pl.BlockSpec((None, tr, 128), lambda j, i, c_ref: (j, i, 0))],
            out_specs=pl.BlockSpec((None, tr, 128), lambda j, i, c_ref: (j, i, 0))),
        out_shape=jax.ShapeDtypeStruct(recv.shape, recv.dtype),
        compiler_params=_params("parallel", "parallel"),
    )(c, part, recv)


def _add_chips(t, recv, me):
    _, rh, _ = t.shape
    tr = _tile(rh, 1024, 8)

    def body(me_ref, t_ref, r_ref, o_ref):
        f = lambda v: v.astype(F32)
        o_ref[...] = ((f(t_ref[...]) + f(r_ref[0])) + f(r_ref[1])) + f(r_ref[2])

    return pl.pallas_call(
        body, name="grad_chip_sum",
        grid_spec=pltpu.PrefetchScalarGridSpec(
            num_scalar_prefetch=1, grid=(rh // tr,),
            in_specs=[pl.BlockSpec((None, tr, 128), lambda i, me_ref: (me_ref[0], i, 0)),
                      pl.BlockSpec((3, tr, 128), lambda i, me_ref: (0, i, 0))],
            out_specs=pl.BlockSpec((tr, 128), lambda i, me_ref: (i, 0))),
        out_shape=jax.ShapeDtypeStruct((rh, 128), F32),
        compiler_params=_params("parallel"),
    )(me, t, recv)


def _reduce_scatter(part):
    x, y, c = _place()
    t = _add_sibling_half(part, _pair_exchange(part), jnp.reshape(c, (1,)).astype(jnp.int32))
    half = _add_chips(t, _chip_exchange(t), jnp.reshape(2 * x + y, (1,)).astype(jnp.int32))
    return _pair_share(half)


def _to_rows(parts, mult):
    flat = jnp.concatenate([p.reshape(-1) for p in parts])
    flat = jnp.pad(flat, (0, (-flat.size) % (128 * mult)))
    return flat.reshape(-1, 128)


def _from_rows(rows, shapes):
    flat, out, at = rows.reshape(-1), [], 0
    for shp in shapes:
        n = int(np.prod(shp))
        out.append(flat[at:at + n].reshape(shp))
        at += n
    return out


def _full_from_shards(g, name, shard_shape):
    depth = shard_shape[0]
    if name in ROW_SHARDED:
        return jnp.moveaxis(g, 0, 1).reshape(depth, 4 * shard_shape[1], shard_shape[2])
    return jnp.moveaxis(g, 0, 2).reshape(depth, shard_shape[1], 4 * shard_shape[2])


def _shards_from_full(full, name):
    depth, rows, cols = full.shape
    if name in ROW_SHARDED:
        return jnp.moveaxis(full.reshape(depth, 4, rows // 4, cols), 1, 0)
    return jnp.moveaxis(full.reshape(depth, rows, 4, cols // 4), 2, 0)


def kernel(x, w_in, b_gate, q_norm_b, k_norm_b, rpb_c, w_branch_a, w_branch_b, w_branch_c, w_out, ln1_g, ln1_b, w_up, w_down, ln2_g, ln2_b, loss_target, m_w_in, m_b_gate, m_q_norm_b, m_k_norm_b, m_rpb_c, m_w_branch_a, m_w_branch_b, m_w_branch_c, m_w_out, m_ln1_g, m_ln1_b, m_w_up, m_w_down, m_ln2_g, m_ln2_b, v_w_in, v_b_gate, v_q_norm_b, v_k_norm_b, v_rpb_c, v_w_branch_a, v_w_branch_b, v_w_branch_c, v_w_out, v_ln1_g, v_ln1_b, v_w_up, v_w_down, v_ln2_g, v_ln2_b):
    args = dict(locals())
    big_shard = {n: args[n] for n in BIG}
    small = {n: args[n] for n in SMALL}
    shapes = [big_shard[n].shape for n in BIG]

    flat = _to_rows([big_shard[n].astype(_MXU) for n in BIG], 32)
    gathered = _gather_shards(flat)
    per_chip = [_from_rows(gathered[j], shapes) for j in range(4)]
    wfull = {n: _full_from_shards(jnp.stack([per_chip[j][i] for j in range(4)]), n, shapes[i]) for i, n in enumerate(BIG)}

    sq, grad_x, grads = _local_step(x[0], loss_target[0], wfull, small)
    loss = lax.psum(0.5 * jnp.sum(sq) / x.shape[-1], ("x", "y", "c"))

    part = jnp.stack([_to_rows([_shards_from_full(grads[n], n)[j].astype(_MXU) for n in BIG], 32) for j in range(4)])
    g_big = _from_rows(_reduce_scatter(part), shapes)
    small_shapes = [small[n].shape for n in SMALL]
    g_small = _from_rows(_sum_slots(_gather_all(_to_rows([grads[n] for n in SMALL], 8)), "small_grad_sum"), small_shapes)
    grad = dict(zip(BIG, g_big))
    grad.update(zip(SMALL, g_small))

    delta, new_m, new_v = {}, {}, {}
    for n in BIG:
        shp = big_shard[n].shape
        two_d = lambda t: t.reshape(-1, shp[-1])
        res = _adamw(two_d(big_shard[n]), two_d(grad[n]), two_d(args["m_" + n]), two_d(args["v_" + n]), "adamw_" + n)
        delta[n], new_m[n], new_v[n] = [t.reshape(shp) for t in res]
    packed = [_to_rows([args[pre + n] for n in SMALL], 8) for pre in ("", "m_", "v_")]
    res = _adamw(packed[0], _to_rows([grad[n] for n in SMALL], 8), packed[1], packed[2], "adamw_small")
    for dst, rows in zip((delta, new_m, new_v), res):
        dst.update(zip(SMALL, _from_rows(rows, small_shapes)))

    order = ("w_in", "b_gate", "q_norm_b", "k_norm_b", "rpb_c", "w_branch_a", "w_branch_b", "w_branch_c", "w_out",
             "ln1_g", "ln1_b", "w_up", "w_down", "ln2_g", "ln2_b")
    return (loss, grad_x[None], *[grad[n] for n in order], *[delta[n] for n in order],
            *[new_m[n] for n in order], *[new_v[n] for n in order])
```

```python
import functools

import numpy as np
import jax
import jax.numpy as jnp
from jax import lax
from jax.experimental import pallas as pl
from jax.experimental.pallas import tpu as pltpu

F32 = jnp.float32
_MXU = jnp.bfloat16

HEAD = 64
A_W, BQ_W, BKV_W, C_W = 256, 512, 128, 256
QKV_W = 768
A_DILATIONS = (1, 4, 16)
A_RADIUS = 64
A_ROPE_HALF = 8
AX_ROPE_HALF = 16
ROPE_THETA = 500000.0
AX_THETA = 10000.0
GRID_W = 64
C_ROWS = 8
C_COLS = 16
BAND = 128
BT_TILES = 18
LN_EPS = 1e-5
RMS_EPS = 1e-6
NEG = -1e30
SCALE = HEAD ** -0.5
LOG2E = 1.4426950408889634
LN2 = 0.6931471805599453
ADAM_LR, ADAM_B1, ADAM_B2, ADAM_EPS, ADAM_WD, ADAM_STEP = 0.001, 0.9, 0.999, 1e-08, 0.01, 10
V7X_VMEM_LIMIT = 48 * 1024 * 1024
MESH = pl.DeviceIdType.MESH
ANY = pl.BlockSpec(memory_space=pl.ANY)


def _params(*sem):
    return pltpu.CompilerParams(dimension_semantics=sem or None, vmem_limit_bytes=V7X_VMEM_LIMIT)


def _tile(n, pref, align=128):
    if n <= pref:
        return n
    t = (pref // align) * align
    while t >= align:
        if n % t == 0:
            return t
        t -= align
    return n


def _mm(a, b, *, name, mode="nn", outs=((F32),), epilogue=None, extras=(), tm=1024, tn=1024, tk=2048, exact=False,
        b_cols=None, b_off=0, out_chips=False):
    b, b_lead, b_axis = (tuple(b) + (None, None))[:3] if isinstance(b, tuple) else (b, None, None)
    m, k = a.shape if mode != "tn" else a.shape[::-1]
    b_rows = b.shape[-2] * (4 if b_axis == "rows" else 1)
    b_last = b_cols or b.shape[-1] * (4 if b_axis == "cols" else 1)
    k2, n = (b_rows, b_last) if mode != "nt" else (b_last, b_rows)
    assert k == k2, (a.shape, b.shape, mode)
    cap_rows = b.shape[-2] if b_axis == "rows" else None
    cap_cols = b.shape[-1] if b_axis == "cols" else (n // 4 if out_chips else None)
    cap_n, cap_k = (cap_cols, cap_rows) if mode != "nt" else (cap_rows, cap_cols)
    tm, tn, tk = _tile(m, tm), _tile(cap_n or n, min(tn, cap_n or tn)), _tile(cap_k or k, min(tk, cap_k or tk))
    nk = k // tk
    n_ex, n_out = len(extras), len(outs)
    mx = F32 if exact else _MXU
    prec = lax.Precision.HIGHEST if exact else None
    dims = {"nn": (((1,), (0,)), ((), ())), "nt": (((1,), (1,)), ((), ())), "tn": (((0,), (0,)), ((), ()))}[mode]

    def body(*refs):
        a_ref, b_ref = refs[0], refs[1]
        ex = refs[2:2 + n_ex]
        out_refs = refs[2 + n_ex:2 + n_ex + n_out]
        kk = pl.program_id(2)
        av, bv = a_ref[...].astype(mx), b_ref[...].astype(mx)
        part = lax.dot_general(av, bv, dims, preferred_element_type=F32, precision=prec)

        def finish(res):
            vals = epilogue(res, *[e[...] for e in ex]) if epilogue is not None else (res,)
            for o, v in zip(out_refs, vals):
                o[...] = v.astype(o.dtype)

        if nk == 1:
            finish(part)
        else:
            acc = refs[-1]

            @pl.when(kk == 0)
            def _():
                acc[...] = part

            @pl.when((kk > 0) & (kk < nk - 1))
            def _():
                acc[...] += part

            @pl.when(kk == nk - 1)
            def _():
                finish(acc[...] + part)

    a_spec = pl.BlockSpec((tm, tk), lambda i, j, kk: (i, kk)) if mode != "tn" else pl.BlockSpec((tk, tm), lambda i, j, kk: (kk, i))
    b_tile = (tn, tk) if mode == "nt" else (tk, tn)

    def b_index(i, j, kk):
        rc = [j, kk + b_off] if mode == "nt" else [kk, j + b_off]
        if b_axis is None:
            return (() if b_lead is None else (b_lead,)) + tuple(rc)
        ax = 0 if b_axis == "rows" else 1
        per = b.shape[-2 + ax] // b_tile[ax]
        chip, rc[ax] = rc[ax] // per, rc[ax] % per
        return (chip, b_lead) + tuple(rc)

    b_spec = pl.BlockSpec((None,) * (b.ndim - 2) + b_tile, b_index)
    o_spec = pl.BlockSpec((tm, tn), lambda i, j, kk: (i, j))
    if out_chips:
        per_out = n // 4 // tn
        out_specs = [pl.BlockSpec((None, tm, tn), lambda i, j, kk: (j // per_out, i, j % per_out))] * n_out
        out_shape = [jax.ShapeDtypeStruct((4, m, n // 4), d) for d in outs]
    else:
        out_specs, out_shape = [o_spec] * n_out, [jax.ShapeDtypeStruct((m, n), d) for d in outs]
    res = pl.pallas_call(
        body, name=name, grid=(m // tm, n // tn, nk),
        in_specs=[a_spec, b_spec] + [o_spec if e.shape[0] > 1 else pl.BlockSpec((1, tn), lambda i, j, kk: (0, j)) for e in extras],
        out_specs=out_specs, out_shape=out_shape,
        scratch_shapes=[pltpu.VMEM((tm, tn), F32)] if nk > 1 else [],
        compiler_params=_params("parallel", "parallel", "arbitrary"),
    )(a, b, *extras)
    return res[0] if n_out == 1 else res


def _rows(tm, width, cb=0):
    return pl.BlockSpec((tm, width), lambda t: (t, cb))


def _whole(arr):
    nd = arr.ndim
    return pl.BlockSpec(arr.shape, lambda t: (0,) * nd)


def _rowwise(fn, name, rows, tm, ins, outs):
    n_in, n_out = len(ins), len(outs)
    dil_in = [spec[1:] if isinstance(spec, tuple) else None for _, spec in ins]
    in_specs = [_rows(tm // spec[1], spec[1] * spec[2]) if isinstance(spec, tuple) else spec for _, spec in ins]
    scratch = [pltpu.VMEM((di[1] // 128, tm, 128), F32) for di in dil_in if di] + \
              [pltpu.VMEM((n // 128, tm, 128), F32) for n, _, kind in outs if isinstance(kind, int)]

    def body(*refs):
        scr = list(refs[n_in + n_out:])
        blocks = []
        for r, di in zip(refs[:n_in], dil_in):
            if di is None:
                blocks.append(r[...])
            else:
                d, n = di
                s_ref = scr.pop(0)
                for j in range(d):
                    for b in range(n // 128):
                        lanes = slice(j * n + b * 128, j * n + (b + 1) * 128)
                        s_ref.at[b][pl.ds(j, tm // d, stride=d), :] = r[:, lanes].astype(F32)
                blocks.append(jnp.concatenate([s_ref[b] for b in range(n // 128)], 1))
        vals = fn(*blocks)
        first = pl.program_id(0) == 0
        for (ncols, _, kind), o, v in zip(outs, refs[n_in:n_in + n_out], vals):
            if kind == "row":
                o[...] = v.astype(o.dtype)
            elif isinstance(kind, int):
                s_ref = scr.pop(0)
                for b in range(ncols // 128):
                    s_ref[b] = v[:, b * 128:(b + 1) * 128].astype(F32)
                for j in range(kind):
                    for b in range(ncols // 128):
                        lanes = slice(j * ncols + b * 128, j * ncols + (b + 1) * 128)
                        o[:, lanes] = s_ref.at[b][pl.ds(j, tm // kind, stride=kind), :].astype(o.dtype)
            else:
                part = v.reshape(tm // 8, 8, ncols).sum(0)

                @pl.when(first)
                def _(o=o, part=part):
                    o[...] = part

                @pl.when(jnp.logical_not(first))
                def _(o=o, part=part):
                    o[...] += part

    def out_spec(n, kind):
        if kind == "row":
            return _rows(tm, n), (rows, n)
        if isinstance(kind, int):
            return _rows(tm // kind, kind * n), (rows // kind, kind * n)
        return pl.BlockSpec((8, n), lambda t: (0, 0)), (8, n)

    specs = [out_spec(n, kind) for n, _, kind in outs]
    res = pl.pallas_call(
        body, name=name, grid=(rows // tm,),
        in_specs=in_specs, out_specs=[s for s, _ in specs],
        out_shape=[jax.ShapeDtypeStruct(shp, d) for (_, shp), (_, d, _) in zip(specs, outs)],
        scratch_shapes=scratch, compiler_params=_params("arbitrary"),
    )(*[a for a, _ in ins])
    return res


def _lane_lo(width=128):
    return (lax.broadcasted_iota(jnp.int32, (1, width), 1) & (HEAD * 2 - 1)) < HEAD


def _group_sum(x):
    w = x.shape[-1]
    sh = HEAD.bit_length() - 1
    same = (lax.broadcasted_iota(jnp.int32, (w, w), 0) >> sh) == (lax.broadcasted_iota(jnp.int32, (w, w), 1) >> sh)
    ones = jnp.where(same, 1.0, 0.0).astype(jnp.bfloat16)
    hi = x.astype(jnp.bfloat16)
    lo = (x - hi.astype(F32)).astype(jnp.bfloat16)
    return jnp.dot(hi, ones, preferred_element_type=F32) + jnp.dot(lo, ones, preferred_element_type=F32)


def _rot(x, c, sm, sp, shift):
    w = x.shape[-1]
    return x * c + pltpu.roll(x, w - shift, 1) * sm + pltpu.roll(x, shift, 1) * sp


def _rot_t(dy, c, sm, sp, shift):
    w = dy.shape[-1]
    return dy * c + pltpu.roll(dy * sm, shift, 1) + pltpu.roll(dy * sp, w - shift, 1)


def _rope_tables(pos_parts, half, thetas):
    cs, sms, sps = [], [], []
    for pos, theta in zip(pos_parts, thetas):
        inv = theta ** (-jnp.arange(half, dtype=F32) / half)
        ang = pos.astype(F32)[:, None] * inv[None, :]
        co, si, ze = jnp.cos(ang), jnp.sin(ang), jnp.zeros_like(ang)
        cs += [co, co]
        sms += [-si, ze]
        sps += [ze, si]
    return [jnp.concatenate(t, axis=1) for t in (cs, sms, sps)]


def _tables(s):
    pos = jnp.arange(s)
    ca, sma, spa = _rope_tables([pos], A_ROPE_HALF, [ROPE_THETA])
    pad = HEAD - 2 * A_ROPE_HALF
    ca = jnp.concatenate([ca, jnp.ones((s, pad), F32)], 1)
    sma, spa = [jnp.concatenate([t, jnp.zeros((s, pad), F32)], 1) for t in (sma, spa)]
    tab_a = [jnp.tile(t, (1, A_W // HEAD)) for t in (ca, sma, spa)]
    ax = _rope_tables([pos // GRID_W, pos % GRID_W], AX_ROPE_HALF, [AX_THETA, AX_THETA])
    tab_q = [jnp.tile(t, (1, BQ_W // HEAD)) for t in ax]
    tab_k = [jnp.tile(t, (1, BKV_W // HEAD)) for t in ax]
    return tab_a, tab_q, tab_k


def _prep_a(ha, tab, tm):
    s = ha.shape[0]

    def fn(h, c, sm, sp):
        q, k, v = h[:, :A_W], h[:, A_W:2 * A_W], h[:, 2 * A_W:]
        q, k = _rot(q, c, sm, sp, A_ROPE_HALF) * SCALE, _rot(k, c, sm, sp, A_ROPE_HALF)
        return [t for t in (q, k, v) for _ in A_DILATIONS]

    res = _rowwise(fn, "prep_a", s, tm, [(ha, _rows(tm, QKV_W))] + [(t, _rows(tm, A_W)) for t in tab],
                   [(A_W, _MXU, _dil_kind(d)) for _ in range(3) for d in A_DILATIONS])
    n = len(A_DILATIONS)
    return [dict(zip(A_DILATIONS, res[i * n:(i + 1) * n])) for i in range(3)]


def _dil_kind(d):
    return "row" if d == 1 else d


def _dil_spec(d, tm, ncols):
    return _rows(tm, ncols) if d == 1 else ("dil", d, ncols)


def _to_dilations(x, name, tm):
    s, n = x.shape
    res = _rowwise(lambda v: [v for d in A_DILATIONS if d > 1], name, s, tm, [(x, _rows(tm, n))],
                   [(n, x.dtype, d) for d in A_DILATIONS if d > 1])
    return {1: x, **dict(zip([d for d in A_DILATIONS if d > 1], res))}


def _rms(x, g):
    ms = _group_sum(x * x) * (1.0 / HEAD)
    return x * lax.rsqrt(ms + RMS_EPS) * g


def _prep_b(hb, gq, gk, tab_q, tab_k, tm):
    s = hb.shape[0]

    def fn(h, gq, gk, cq, smq, spq, ck, smk, spk):
        xq, xk, v = h[:, :BQ_W], h[:, BQ_W:BQ_W + BKV_W], h[:, BQ_W + BKV_W:]
        q = _rot(_rms(xq, gq), cq, smq, spq, AX_ROPE_HALF) * (SCALE * LOG2E)
        k = _rot(_rms(xk, gk), ck, smk, spk, AX_ROPE_HALF)
        lo = _lane_lo()
        kr, vr = pltpu.roll(k, HEAD, 1), pltpu.roll(v, HEAD, 1)
        kd = jnp.concatenate([jnp.where(lo, k, kr), jnp.where(lo, kr, k)], 1)
        vd = jnp.concatenate([jnp.where(lo, v, vr), jnp.where(lo, vr, v)], 1)
        v1 = jnp.concatenate([jnp.where(lo, v, 1.0), jnp.where(lo, vr, 1.0)], 1)
        return q, kd, vd, v1

    ins = [(hb, _rows(tm, QKV_W)), (gq, _whole(gq)), (gk, _whole(gk))]
    ins += [(t, _rows(tm, BQ_W)) for t in tab_q] + [(t, _rows(tm, BKV_W)) for t in tab_k]
    return _rowwise(fn, "prep_b", s, tm, ins, [(BQ_W, _MXU, "row")] + [(2 * BKV_W, _MXU, "row")] * 3)


def _prep_c(hc, tm):
    def fn(h):
        return h[:, :C_W] * SCALE, h[:, C_W:2 * C_W], h[:, 2 * C_W:]

    return _rowwise(fn, "prep_c", hc.shape[0], tm, [(hc, _rows(tm, QKV_W))], [(C_W, _MXU, "row")] * 3)


def _combine_a(os_, ms, ls, s, tm):
    def fn(o1, o2, o3, m1, m2, m3, l1, l2, l3):
        lo = _lane_lo()
        outs, lses = [], []
        for p in range(A_W // 128):
            st = slice(p * 256, (p + 1) * 256)
            mm = [m[:, st] for m in (m1, m2, m3)]
            ll = [l[:, st] for l in (l1, l2, l3)]
            mmax = jnp.maximum(jnp.maximum(mm[0], mm[1]), mm[2])
            ws = [jnp.exp(m - mmax) for m in mm]
            den = ws[0] * ll[0] + ws[1] * ll[1] + ws[2] * ll[2]
            lses.append(mmax + jnp.log(den))
            num = sum(jnp.where(lo, w[:, :128], w[:, 128:]) * o[:, p * 128:(p + 1) * 128] for w, o in zip(ws, (o1, o2, o3)))
            outs.append(num / jnp.where(lo, den[:, :128], den[:, 128:]))
        o, lse = jnp.concatenate(outs, 1), jnp.concatenate(lses, 1)
        return [o] * len(A_DILATIONS) + [lse] * len(A_DILATIONS)

    ins = [(t, _dil_spec(d, tm, w)) for ts, w in ((os_, A_W), (ms, 2 * A_W), (ls, 2 * A_W)) for t, d in zip(ts, A_DILATIONS)]
    res = _rowwise(fn, "combine_a", s, tm, ins,
                   [(w, F32, _dil_kind(d)) for w in (A_W, 2 * A_W) for d in A_DILATIONS])
    n = len(A_DILATIONS)
    return dict(zip(A_DILATIONS, res[:n])), dict(zip(A_DILATIONS, res[n:]))


def _gates(hg, bg, d):
    return [jax.nn.sigmoid(hg[:, i * d:(i + 1) * d] + bg[:, i * d:(i + 1) * d]) for i in range(3)]


def _gate_merge(hg, bg, pa, pb, pc, tm):
    s, d = pa.shape

    def fn(hg, bg, pa, pb, pc):
        g = _gates(hg, bg, d)
        return (g[0] * pa + g[1] * pb + g[2] * pc,)

    ins = [(hg, _rows(tm, 3 * d)), (bg, _whole(bg))] + [(p, _rows(tm, d)) for p in (pa, pb, pc)]
    return _rowwise(fn, "gate_merge", s, tm, ins, [(d, _MXU, "row")])[0]


def _gate_bwd(dm, hg, bg, pa, pb, pc, tm):
    s, d = pa.shape

    def fn(dm, hg, bg, pa, pb, pc):
        g = _gates(hg, bg, d)
        dlog = jnp.concatenate([dm * p * gi * (1.0 - gi) for p, gi in zip((pa, pb, pc), g)], 1)
        return dm * g[0], dm * g[1], dm * g[2], dlog, dlog

    ins = [(dm, _rows(tm, d)), (hg, _rows(tm, 3 * d)), (bg, _whole(bg))] + [(p, _rows(tm, d)) for p in (pa, pb, pc)]
    return _rowwise(fn, "gate_bwd", s, tm, ins, [(d, _MXU, "row")] * 3 + [(3 * d, _MXU, "row"), (3 * d, F32, "acc")])


def _ln_stats(r):
    mu = jnp.mean(r, -1, keepdims=True)
    xc = r - mu
    var = jnp.mean(xc * xc, -1, keepdims=True)
    rstd = lax.rsqrt(var + LN_EPS)
    return xc * rstd, rstd


def _ln_epilogue(alpha):
    def fn(br, x, g, b):
        r = alpha * x + br
        xhat, _ = _ln_stats(r)
        y = xhat * g + b
        return r, y, y

    return fn


def _ln_bwd(dy, r, g, name, tm):
    s, d = r.shape

    def fn(dy, r, g):
        xhat, rstd = _ln_stats(r)
        dxh = dy * g
        dr = rstd * (dxh - jnp.mean(dxh, -1, keepdims=True) - xhat * jnp.mean(dxh * xhat, -1, keepdims=True))
        return dr, dr, dy * xhat, dy

    ins = [(dy, _rows(tm, d)), (r, _rows(tm, d)), (g, _whole(g))]
    return _rowwise(fn, name, s, tm, ins, [(d, F32, "row"), (d, _MXU, "row"), (d, F32, "acc"), (d, F32, "acc")])


def _loss_head(y, target, tm):
    s, d = y.shape

    def fn(y, t):
        diff = y - t
        return diff * diff, diff * (1.0 / d)

    sq, dy = _rowwise(fn, "loss_head", s, tm, [(y, _rows(tm, d)), (target, _rows(tm, d))], [(d, F32, "acc"), (d, F32, "row")])
    return sq, dy


def _post_a(dqs, dks, dvs, tab, s, tm):
    def fn(q1, q2, q3, k1, k2, k3, v1, v2, v3, c, sm, sp):
        dq = _rot_t((q1 + q2 + q3) * SCALE, c, sm, sp, A_ROPE_HALF)
        dk = _rot_t(k1 + k2 + k3, c, sm, sp, A_ROPE_HALF)
        return (jnp.concatenate([dq, dk, v1 + v2 + v3], 1),)

    ins = [(t, _dil_spec(d, tm, A_W)) for ts in (dqs, dks, dvs) for t, d in zip(ts, A_DILATIONS)]
    ins += [(t, _rows(tm, A_W)) for t in tab]
    return _rowwise(fn, "post_a", s, tm, ins, [(QKV_W, _MXU, "row")])[0]


def _post_b(dq, dkd, dvd, hb, gq, gk, tab_q, tab_k, tm):
    s = dq.shape[0]

    def back(dz, x, g, c, sm, sp):
        dy = _rot_t(dz, c, sm, sp, AX_ROPE_HALF)
        rstd = lax.rsqrt(_group_sum(x * x) * (1.0 / HEAD) + RMS_EPS)
        xh = x * rstd
        dxh = dy * g
        return rstd * (dxh - xh * (_group_sum(dxh * xh) * (1.0 / HEAD))), dy * xh

    def fn(dq, dkd, dvd, h, gq, gk, cq, smq, spq, ck, smk, spk):
        lo = _lane_lo()
        dk = jnp.where(lo, dkd[:, :128], dkd[:, 128:])
        dv = jnp.where(lo, dvd[:, :128], dvd[:, 128:])
        dxq, dgq = back(dq * SCALE, h[:, :BQ_W], gq, cq, smq, spq)
        dxk, dgk = back(dk, h[:, BQ_W:BQ_W + BKV_W], gk, ck, smk, spk)
        return jnp.concatenate([dxq, dxk, dv], 1), dgq, dgk

    ins = [(dq, _rows(tm, BQ_W)), (dkd, _rows(tm, 2 * BKV_W)), (dvd, _rows(tm, 2 * BKV_W)), (hb, _rows(tm, QKV_W)),
           (gq, _whole(gq)), (gk, _whole(gk))]
    ins += [(t, _rows(tm, BQ_W)) for t in tab_q] + [(t, _rows(tm, BKV_W)) for t in tab_k]
    return _rowwise(fn, "post_b", s, tm, ins, [(QKV_W, _MXU, "row"), (BQ_W, F32, "acc"), (BKV_W, F32, "acc")])


def _post_c(dq, dk, dv, tm):
    def fn(dq, dk, dv):
        return (jnp.concatenate([dq * SCALE, dk, dv], 1),)

    return _rowwise(fn, "post_c", dq.shape[0], tm, [(t, _rows(tm, C_W)) for t in (dq, dk, dv)], [(QKV_W, _MXU, "row")])[0]


def _adamw(w, g, m, v, name):
    rows, cols = w.shape
    tm = _tile(rows, 256, 8)

    def fn(w, g, m, v):
        m = ADAM_B1 * m + (1.0 - ADAM_B1) * g
        v = ADAM_B2 * v + (1.0 - ADAM_B2) * (g * g)
        m_hat = m / (1.0 - ADAM_B1 ** ADAM_STEP)
        v_hat = v / (1.0 - ADAM_B2 ** ADAM_STEP)
        delta = -ADAM_LR * (m_hat / (jnp.sqrt(v_hat) + ADAM_EPS) + ADAM_WD * w)
        return delta, m, v

    return _rowwise(fn, name, rows, tm, [(t, _rows(tm, cols)) for t in (w, g, m, v)], [(cols, F32, "row")] * 3)


def _dot_t(a, b):
    return lax.dot_general(a, b, (((1,), (1,)), ((), ())), preferred_element_type=F32)


def _tdot(a, b):
    return lax.dot_general(a, b, (((0,), (0,)), ((), ())), preferred_element_type=F32)


def _head_masks():
    lo = _lane_lo()
    return lo, (lo, jnp.logical_not(lo))


def _rep(x, rows):
    return jnp.broadcast_to(x, (rows, 128))


def _row_lo():
    return lax.broadcasted_iota(jnp.int32, (128, 1), 0) < HEAD


def _flash_fwd(q, kd, v1, tq, tk):
    s = q.shape[0]
    tq, tk = _tile(s, tq), _tile(s, tk)
    nk = s // tk
    mx = _MXU

    def body(q_ref, k_ref, v_ref, o_ref, lse_ref, m_ref, acc_ref):
        kk = pl.program_id(2)

        @pl.when(kk == 0)
        def _():
            m_ref[...] = jnp.full_like(m_ref, NEG)
            acc_ref[...] = jnp.zeros_like(acc_ref)

        q2, k2, v2 = q_ref[...], k_ref[...], v_ref[...]
        _, masks = _head_masks()
        hs = range(2)
        st = [_dot_t(k2, jnp.where(masks[h], q2, jnp.zeros_like(q2))) for h in hs]
        m_prev = [m_ref[h] for h in hs]
        m_new = [jnp.maximum(m_prev[h], jnp.max(st[h], 0, keepdims=True)) for h in hs]
        p = [jnp.exp2(st[h] - m_new[h]).astype(mx) for h in hs]
        pv = [_tdot(v2, p[h]) for h in hs]
        for h in hs:
            m_ref[h] = m_new[h]
            acc_ref[h] = acc_ref[h] * jnp.exp2(m_prev[h] - m_new[h]) + pv[h]

        @pl.when(kk == nk - 1)
        def _():
            a0, a1 = acc_ref[0], acc_ref[1]
            l0, l1 = a0[HEAD:HEAD + 1], a1[HEAD:HEAD + 1]
            o_ref[...] = jnp.concatenate([a0[:HEAD] / l0, a1[:HEAD] / l1], 0).T
            lse_ref[...] = jnp.concatenate([m_ref[0] + jnp.log2(l0), m_ref[1] + jnp.log2(l1), jnp.zeros((6, tq), F32)], 0)

    return pl.pallas_call(
        body, name="attn_b_fwd", grid=(BQ_W // 128, s // tq, nk),
        in_specs=[pl.BlockSpec((tq, 128), lambda j, i, kk: (i, j)),
                  pl.BlockSpec((tk, 128), lambda j, i, kk: (kk, j // 2)),
                  pl.BlockSpec((tk, 128), lambda j, i, kk: (kk, j // 2))],
        out_specs=[pl.BlockSpec((tq, 128), lambda j, i, kk: (i, j)), pl.BlockSpec((None, 8, tq), lambda j, i, kk: (j, 0, i))],
        out_shape=[jax.ShapeDtypeStruct((s, BQ_W), F32), jax.ShapeDtypeStruct((BQ_W // 128, 8, s), F32)],
        scratch_shapes=[pltpu.VMEM((2, 1, tq), F32), pltpu.VMEM((2, 128, tq), F32)],
        compiler_params=_params("parallel", "parallel", "arbitrary"),
    )(q, kd, v1)


def _delta_b(do, o, tq):
    s = do.shape[0]
    tq = _tile(s, tq)

    def body(do_ref, o_ref, d_ref):
        prod = do_ref[...] * o_ref[...]
        row = lax.broadcasted_iota(jnp.int32, (8, 128), 0)
        lane = lax.broadcasted_iota(jnp.int32, (8, 128), 1)
        sel = jnp.where(((row == 0) & (lane < HEAD)) | ((row == 1) & (lane >= HEAD)), 1.0, 0.0).astype(F32)
        d_ref[...] = lax.dot_general(sel, prod, (((1,), (1,)), ((), ())), preferred_element_type=F32,
                                     precision=lax.Precision.HIGHEST)

    qs = pl.BlockSpec((tq, 128), lambda j, i: (i, j))
    return pl.pallas_call(
        body, name="attn_b_delta", grid=(BQ_W // 128, s // tq), in_specs=[qs, qs],
        out_specs=pl.BlockSpec((None, 8, tq), lambda j, i: (j, 0, i)),
        out_shape=jax.ShapeDtypeStruct((BQ_W // 128, 8, s), F32),
        compiler_params=_params("parallel", "parallel"),
    )(do, o)


def _flash_bwd(q, kd, vd, do, lse, delta, tq, tk):
    s = q.shape[0]
    tq, tk = _tile(s, tq), _tile(s, tk)
    nq, nk = s // tq, s // tk
    group = BQ_W // 128 // 2
    mx = _MXU

    def body(k_ref, v_ref, q_ref, do_ref, lse_ref, dl_ref, dq_hbm, dk_ref, dv_ref, dk_acc, dv_acc, dqt, stage, sem):
        e, kk, jj, i = pl.program_id(0), pl.program_id(1), pl.program_id(2), pl.program_id(3)

        @pl.when((jj == 0) & (i == 0))
        def _():
            dk_acc[...] = jnp.zeros_like(dk_acc)
            dv_acc[...] = jnp.zeros_like(dv_acc)

        @pl.when(kk == 0)
        def _():
            dqt[jj, i] = jnp.zeros((128, tq), F32)

        q2, k2, v2, do2 = q_ref[...], k_ref[...], v_ref[...], do_ref[...].astype(mx)
        lse8, dl8 = lse_ref[...], dl_ref[...]
        _, masks = _head_masks()
        hs = range(2)
        qh = [jnp.where(masks[h], q2, jnp.zeros_like(q2)) for h in hs]
        doh = [jnp.where(masks[h], do2, jnp.zeros_like(do2)) for h in hs]
        st = [_dot_t(k2, qh[h]) for h in hs]
        dpt = [_dot_t(v2, doh[h]) for h in hs]
        p = [jnp.exp2(st[h] - lse8[h:h + 1]) for h in hs]
        ds = [(p[h] * (dpt[h] - dl8[h:h + 1])).astype(mx) for h in hs]
        p = [p[h].astype(mx) for h in hs]
        dv_acc[...] += jnp.dot(p[0], doh[0], preferred_element_type=F32) + jnp.dot(p[1], doh[1], preferred_element_type=F32)
        dk_acc[...] += jnp.dot(ds[0], qh[0], preferred_element_type=F32) + jnp.dot(ds[1], qh[1], preferred_element_type=F32)
        dqt[jj, i] += jnp.where(_row_lo(), _tdot(k2, ds[0]), _tdot(k2, ds[1]))

        @pl.when(kk == nk - 1)
        def _():
            stage[...] = dqt[jj, i].T
            lane0 = pl.multiple_of((group * e + jj) * 128, 128)
            cp = pltpu.make_async_copy(stage, dq_hbm.at[pl.ds(pl.multiple_of(i * tq, tq), tq), pl.ds(lane0, 128)], sem)
            cp.start()
            cp.wait()

        @pl.when((jj == group - 1) & (i == nq - 1))
        def _():
            dk_ref[...] = (dk_acc[...] + pltpu.roll(dk_acc[...], HEAD, 1)) * LN2
            dv_ref[...] = dv_acc[...] + pltpu.roll(dv_acc[...], HEAD, 1)

    ks = pl.BlockSpec((tk, 128), lambda e, kk, jj, i: (kk, e))
    qs = pl.BlockSpec((tq, 128), lambda e, kk, jj, i: (i, group * e + jj))
    st = pl.BlockSpec((None, 8, tq), lambda e, kk, jj, i: (group * e + jj, 0, i))
    return pl.pallas_call(
        body, name="attn_b_bwd", grid=(BKV_W // HEAD, nk, group, nq),
        in_specs=[ks, ks, qs, qs, st, st], out_specs=[ANY, ks, ks],
        out_shape=[jax.ShapeDtypeStruct((s, BQ_W), F32)] + [jax.ShapeDtypeStruct((s, 2 * BKV_W), F32)] * 2,
        scratch_shapes=[pltpu.VMEM((tk, 128), F32)] * 2 + [pltpu.VMEM((group, nq, 128, tq), F32), pltpu.VMEM((tq, 128), F32),
                                                          pltpu.SemaphoreType.DMA],
        compiler_params=_params("parallel", "arbitrary", "arbitrary", "arbitrary"),
    )(kd, vd, q, do, lse, delta)


def _p_and_ds(items, masks):
    mx = _MXU
    keys = [(n, h) for n in range(len(items)) for h in range(2)]
    qh = {(n, h): jnp.where(masks[h], items[n][0], jnp.zeros_like(items[n][0])) for n, h in keys}
    doh = {(n, h): jnp.where(masks[h], items[n][3], jnp.zeros_like(items[n][3])) for n, h in keys}
    sc = {}
    for n, h in keys:
        s_h = _dot_t(qh[n, h], items[n][1])
        if items[n][7] is not None:
            s_h = s_h + items[n][7][h]
        sc[n, h] = jnp.where(items[n][6], s_h, NEG)
    dp = {(n, h): _dot_t(doh[n, h].astype(mx), items[n][2]) for n, h in keys}
    lse = {(n, h): jnp.max(items[n][5][:, h * 128:(h + 1) * 128], -1, keepdims=True) for n, h in keys}
    delta = {(n, h): jnp.sum(doh[n, h] * items[n][4], -1, keepdims=True) for n, h in keys}
    p = {key: jnp.exp(sc[key] - lse[key]) for key in keys}
    ds = {key: p[key] * (dp[key] - delta[key]) for key in keys}
    return [[(qh[n, h], p[n, h], ds[n, h], doh[n, h]) for h in range(2)] for n in range(len(items))]


class _BandA:
    hb, has_bias, name = 1, False, "a"

    def __init__(self, nb):
        self.nb = nb

    def mask(self, qidx, kidx):
        n = self.nb * BAND
        return (jnp.abs(qidx - kidx) <= A_RADIUS) & (kidx >= 0) & (kidx < n) & (qidx >= 0) & (qidx < n)


class _BandC:
    hb, has_bias, name = 3, True, "c"

    def __init__(self, nb):
        self.nb = nb
        self.rows = nb * BAND // GRID_W
        per = BAND // GRID_W
        assert self.rows >= C_ROWS and (C_ROWS - 1) // per <= self.hb
        assert (self.rows - 1) // per - (self.rows - C_ROWS) // per <= self.hb

    def mask(self, qidx, kidx):
        n = self.nb * BAND
        sh = GRID_W.bit_length() - 1
        qrow, cq = qidx >> sh, qidx & (GRID_W - 1)
        krow, ck = kidx >> sh, kidx & (GRID_W - 1)
        r0 = jnp.clip(qrow - C_ROWS // 2, 0, self.rows - C_ROWS)
        c0 = jnp.clip(cq - C_COLS // 2, 0, GRID_W - C_COLS)
        ok = (qidx >= 0) & (qidx < n) & (kidx >= 0) & (kidx < n)
        return ok & (krow >= r0) & (krow < r0 + C_ROWS) & (ck >= c0) & (ck < c0 + C_COLS)


def _bias_tile(off, a):
    return (BAND // GRID_W) * off - a + (C_ROWS - 1) + 2


def _band_bias_k(band, bt_ref, h):
    per = BAND // GRID_W
    return jnp.concatenate([jnp.concatenate([bt_ref[h, _bias_tile(off, a)] for off in range(-band.hb, band.hb + 1)], 1)
                            for a in range(per)], 0)


def _band_bias_q(band, bt_ref, h):
    per = BAND // GRID_W
    return jnp.concatenate([bt_ref[h, _bias_tile(-off, a)] for off in range(-band.hb, band.hb + 1) for a in range(per)], 0)


def _band_split(nb, ncb):
    cb = max(c for c in (4, 2, 1) if ncb % c == 0)
    rb = max(r for r in (4, 2, 1) if nb % r == 0 and r * cb <= 8)
    return rb, cb


def _band_specs(band, rb, cb, nb, width):
    def edge(first):
        return pl.BlockSpec((BAND, cb * width), lambda c, i: (jnp.clip(i * rb + first, 0, nb - 1), c))

    main = pl.BlockSpec((rb * BAND, cb * width), lambda c, i: (i, c))
    return [edge(t - band.hb) for t in range(band.hb)] + [main] + [edge(rb + t) for t in range(band.hb)]


def _band_rows(band, refs, rb, r, lanes):
    hb = band.hb
    parts = []
    for b in range(r, r + 2 * hb + 1):
        if b < hb:
            parts.append(refs[b][:, lanes])
        elif b < hb + rb:
            parts.append(refs[hb][(b - hb) * BAND:(b - hb + 1) * BAND, lanes])
        else:
            parts.append(refs[b - rb + 1][:, lanes])
    return jnp.concatenate(parts, 0)


def _band_idx(band, blk, rows_of_blocks, axis):
    shape = (rows_of_blocks * BAND, 1) if axis == 0 else (1, rows_of_blocks * BAND)
    return blk * BAND + lax.broadcasted_iota(jnp.int32, shape, axis)


def _band_fwd(band, q, k, v, bt=None):
    n, w = q.shape
    nb, ncb, nband = n // BAND, w // 128, 2 * band.hb + 1
    rb, cb = _band_split(nb, ncb)
    mx = _MXU
    raw = not band.has_bias

    def body(*refs):
        q_ref, k_refs, v_refs = refs[0], refs[1:1 + nband], refs[1 + nband:1 + 2 * nband]
        rest = refs[1 + 2 * nband:]
        bt_ref = rest[0] if band.has_bias else None
        outs = rest[1:] if band.has_bias else rest
        i = pl.program_id(1)
        lo, masks = _head_masks()
        subs = [(r, c) for r in range(rb) for c in range(cb)]
        mask = {r: band.mask(_band_idx(band, i * rb + r, 1, 0), _band_idx(band, i * rb + r - band.hb, nband, 1)) for r in range(rb)}
        lanes = {c: slice(c * 128, (c + 1) * 128) for c in range(cb)}
        rows = {r: slice(r * BAND, (r + 1) * BAND) for r in range(rb)}
        sc = {}
        for r, c in subs:
            q2, kcat = q_ref[rows[r], lanes[c]], _band_rows(band, k_refs, rb, r, lanes[c])
            for h in range(2):
                s_h = _dot_t(jnp.where(masks[h], q2, jnp.zeros_like(q2)), kcat)
                if band.has_bias:
                    s_h = s_h + _band_bias_k(band, bt_ref, 2 * c + h)
                sc[r, c, h] = jnp.where(mask[r], s_h, NEG)
        ms = {key: jnp.max(s_h, -1, keepdims=True) for key, s_h in sc.items()}
        ps = {key: jnp.exp(s_h - ms[key]) for key, s_h in sc.items()}
        ls = {key: jnp.sum(p, -1, keepdims=True) for key, p in ps.items()}
        os_ = {}
        for r, c in subs:
            vcat = _band_rows(band, v_refs, rb, r, lanes[c])
            for h in range(2):
                os_[r, c, h] = jnp.dot(ps[r, c, h].astype(mx), vcat, preferred_element_type=F32)
        for r, c in subs:
            st_lanes = [slice(c * 256 + h * 128, c * 256 + (h + 1) * 128) for h in range(2)]
            if raw:
                o_ref, m_ref, l_ref = outs
                o_ref[rows[r], lanes[c]] = jnp.where(lo, os_[r, c, 0], os_[r, c, 1])
                for h in range(2):
                    m_ref[rows[r], st_lanes[h]] = _rep(ms[r, c, h], BAND)
                    l_ref[rows[r], st_lanes[h]] = _rep(ls[r, c, h], BAND)
            else:
                o_ref, lse_ref = outs
                o_ref[rows[r], lanes[c]] = jnp.where(lo, os_[r, c, 0] / ls[r, c, 0], os_[r, c, 1] / ls[r, c, 1])
                for h in range(2):
                    lse_ref[rows[r], st_lanes[h]] = _rep(ms[r, c, h] + jnp.log(ls[r, c, h]), BAND)

    qs = pl.BlockSpec((rb * BAND, cb * 128), lambda c, i: (i, c))
    ks = _band_specs(band, rb, cb, nb, 128)
    st = pl.BlockSpec((rb * BAND, cb * 256), lambda c, i: (i, c))
    in_specs, args = [qs] + ks + ks, [q] + [k] * nband + [v] * nband
    if band.has_bias:
        in_specs.append(pl.BlockSpec((2 * cb, BT_TILES, GRID_W, 128), lambda c, i: (c, 0, 0, 0)))
        args.append(bt)
    n_stats = 2 if raw else 1
    return pl.pallas_call(
        body, name="attn_%s_fwd" % band.name, grid=(ncb // cb, nb // rb), in_specs=in_specs,
        out_specs=[qs] + [st] * n_stats,
        out_shape=[jax.ShapeDtypeStruct((n, w), F32)] + [jax.ShapeDtypeStruct((n, 2 * w), F32)] * n_stats,
        compiler_params=_params("parallel", "arbitrary"),
    )(*args)


def _band_dq(band, q, k, v, do, o, lse, bt=None):
    n, w = q.shape
    nb, ncb, nband = n // BAND, w // 128, 2 * band.hb + 1
    rb, cb = _band_split(nb, ncb)
    mx = _MXU
    per = BAND // GRID_W

    def body(*refs):
        q_ref, k_refs, v_refs = refs[0], refs[1:1 + nband], refs[1 + nband:1 + 2 * nband]
        do_ref, o_ref, lse_ref = refs[1 + 2 * nband:4 + 2 * nband]
        rest = refs[4 + 2 * nband:]
        dq_ref = rest[1] if band.has_bias else rest[0]
        i = pl.program_id(1)
        lo, masks = _head_masks()
        if band.has_bias:
            dbt_ref = rest[2]

            @pl.when(i == 0)
            def _():
                dbt_ref[...] = jnp.zeros_like(dbt_ref)

        subs = [(r, c) for r in range(rb) for c in range(cb)]
        mask = {r: band.mask(_band_idx(band, i * rb + r, 1, 0), _band_idx(band, i * rb + r - band.hb, nband, 1)) for r in range(rb)}
        items, kcats = [], []
        for r, c in subs:
            lanes, rows = slice(c * 128, (c + 1) * 128), slice(r * BAND, (r + 1) * BAND)
            kcats.append(_band_rows(band, k_refs, rb, r, lanes))
            bias = [_band_bias_k(band, rest[0], 2 * c + h) for h in range(2)] if band.has_bias else None
            items.append((q_ref[rows, lanes], kcats[-1], _band_rows(band, v_refs, rb, r, lanes), do_ref[rows, lanes],
                          o_ref[rows, lanes], lse_ref[rows, c * 256:(c + 1) * 256], mask[r], bias))
        res = _p_and_ds(items, masks)
        dqs = [[jnp.dot(ds.astype(mx), kcat, preferred_element_type=F32) for _, _, ds, _ in hs] for hs, kcat in zip(res, kcats)]
        for (r, c), hs, dq in zip(subs, res, dqs):
            dq_ref[r * BAND:(r + 1) * BAND, c * 128:(c + 1) * 128] = jnp.where(lo, dq[0], dq[1])
            if band.has_bias:
                for h in range(2):
                    ds = hs[h][2]
                    for a in range(per):
                        for t in range(nband):
                            tile = ds[a * GRID_W:(a + 1) * GRID_W, t * 128:(t + 1) * 128]
                            dbt_ref[2 * c + h, _bias_tile(t - band.hb, a)] += tile

    qs = pl.BlockSpec((rb * BAND, cb * 128), lambda c, i: (i, c))
    ks = _band_specs(band, rb, cb, nb, 128)
    st = pl.BlockSpec((rb * BAND, cb * 256), lambda c, i: (i, c))
    in_specs, args = [qs] + ks + ks + [qs, qs, st], [q] + [k] * nband + [v] * nband + [do, o, lse]
    out_specs, out_shape = [qs], [jax.ShapeDtypeStruct((n, w), F32)]
    if band.has_bias:
        bts = pl.BlockSpec((2 * cb, BT_TILES, GRID_W, 128), lambda c, i: (c, 0, 0, 0))
        in_specs.append(bts)
        args.append(bt)
        out_specs.append(bts)
        out_shape.append(jax.ShapeDtypeStruct(bt.shape, F32))
    return pl.pallas_call(
        body, name="attn_%s_dq" % band.name, grid=(ncb // cb, nb // rb), in_specs=in_specs, out_specs=out_specs,
        out_shape=out_shape, compiler_params=_params("parallel", "arbitrary"),
    )(*args)


def _band_dkv(band, q, k, v, do, o, lse, bt=None):
    n, w = q.shape
    nb, ncb, nband = n // BAND, w // 128, 2 * band.hb + 1
    rb, cb = _band_split(nb, ncb)
    mx = _MXU

    def body(*refs):
        k_ref, v_ref = refs[0], refs[1]
        q_refs, do_refs, o_refs, lse_refs = [refs[2 + g * nband:2 + (g + 1) * nband] for g in range(4)]
        rest = refs[2 + 4 * nband:]
        dk_ref, dv_ref = rest[-2], rest[-1]
        i = pl.program_id(1)
        _, masks = _head_masks()
        subs = [(r, c) for r in range(rb) for c in range(cb)]
        mask = {r: band.mask(_band_idx(band, i * rb + r - band.hb, nband, 0), _band_idx(band, i * rb + r, 1, 1)) for r in range(rb)}
        items = []
        for r, c in subs:
            lanes, rows = slice(c * 128, (c + 1) * 128), slice(r * BAND, (r + 1) * BAND)
            qcat, docat, ocat = [_band_rows(band, g, rb, r, lanes) for g in (q_refs, do_refs, o_refs)]
            lsecat = _band_rows(band, lse_refs, rb, r, slice(c * 256, (c + 1) * 256))
            bias = [_band_bias_q(band, rest[0], 2 * c + h) for h in range(2)] if band.has_bias else None
            items.append((qcat, k_ref[rows, lanes], v_ref[rows, lanes], docat, ocat, lsecat, mask[r], bias))
        res = _p_and_ds(items, masks)
        dks = [sum(_tdot(ds.astype(mx), qh) for qh, _, ds, _ in hs) for hs in res]
        dvs = [sum(_tdot(p.astype(mx), doh.astype(mx)) for _, p, _, doh in hs) for hs in res]
        for (r, c), dk, dv in zip(subs, dks, dvs):
            dk_ref[r * BAND:(r + 1) * BAND, c * 128:(c + 1) * 128] = dk
            dv_ref[r * BAND:(r + 1) * BAND, c * 128:(c + 1) * 128] = dv

    ks = pl.BlockSpec((rb * BAND, cb * 128), lambda c, i: (i, c))
    in_specs = [ks, ks] + _band_specs(band, rb, cb, nb, 128) * 3 + _band_specs(band, rb, cb, nb, 256)
    args = [k, v] + [q] * nband + [do] * nband + [o] * nband + [lse] * nband
    if band.has_bias:
        in_specs.append(pl.BlockSpec((2 * cb, BT_TILES, GRID_W, 128), lambda c, i: (c, 0, 0, 0)))
        args.append(bt)
    return pl.pallas_call(
        body, name="attn_%s_dkv" % band.name, grid=(ncb // cb, nb // rb), in_specs=in_specs, out_specs=[ks, ks],
        out_shape=[jax.ShapeDtypeStruct((n, w), F32)] * 2,
        compiler_params=_params("parallel", "arbitrary"),
    )(*args)


def _dc_onehot():
    c = np.arange(GRID_W)
    dc = np.clip(c[None, :] - c[:, None] + (C_COLS - 1), 0, 2 * C_COLS - 2).reshape(-1)
    m = np.zeros((GRID_W * GRID_W, 128), np.float32)
    m[np.arange(dc.size), dc] = 1.0
    return m


def _bias_tiles(rpb):
    h, nr, ncol = rpb.shape
    flat = jnp.pad(rpb.reshape(h * nr, ncol), ((0, (-h * nr) % 8), (0, 128 - ncol)))
    tiles = _mm(flat, jnp.asarray(_dc_onehot().T), name="rpb_tiles", exact=True, tn=GRID_W * GRID_W)
    tiles = tiles[:h * nr].reshape(h, nr, GRID_W, GRID_W)
    tiles = jnp.pad(tiles, ((0, 0), (2, BT_TILES + 1 - nr - 2), (0, 0), (0, 0)))
    return jnp.concatenate([tiles[:, :BT_TILES], tiles[:, 1:BT_TILES + 1]], -1)


def _bias_tiles_grad(dbt, nr, ncol):
    h = dbt.shape[0]
    d = dbt[:, 2:2 + nr, :, :GRID_W] + dbt[:, 1:1 + nr, :, GRID_W:]
    flat = jnp.pad(d.reshape(h * nr, GRID_W * GRID_W), ((0, (-h * nr) % 8), (0, 0)))
    g = _mm(flat, jnp.asarray(_dc_onehot()), name="rpb_grad", exact=True, tk=GRID_W * GRID_W)
    return g[:h * nr, :ncol].reshape(h, nr, ncol)


TM = 256
TQ_B, TK_B = 1024, 2048


def _relu2(acc):
    r = jnp.maximum(acc, 0.0)
    return (r * r,)


def _layer_fwd(x, xb, w, sm, tabs, alpha):
    tab_a, tab_q, tab_k = tabs
    s, d = x.shape
    ha = _mm(xb, w["in"], name="in_a", tn=QKV_W, b_cols=QKV_W, b_off=0)
    hb = _mm(xb, w["in"], name="in_b", tn=QKV_W, b_cols=QKV_W, b_off=1)
    hc = _mm(xb, w["in"], name="in_c", tn=QKV_W, b_cols=QKV_W, b_off=2)
    hg = _mm(xb, w["in"], name="in_g", outs=(_MXU,), tn=QKV_W, b_cols=3 * d, b_off=3)

    qa, ka, va = _prep_a(ha, tab_a, TM)
    stats = [_band_fwd(_BandA(s // dil // BAND), qa[dil], ka[dil], va[dil]) for dil in A_DILATIONS]
    oas, lse_a = _combine_a(*zip(*stats), s, TM)
    oa = oas[1]

    qb, kd, vd, v1 = _prep_b(hb, sm["q_norm"], sm["k_norm"], tab_q, tab_k, TM)
    ob, lse_b = _flash_fwd(qb, kd, v1, TQ_B, TK_B)

    qc, kc, vc = _prep_c(hc, TM)
    bt = _bias_tiles(sm["rpb"])
    oc, lse_c = _band_fwd(_BandC(s // BAND), qc, kc, vc, bt)

    pa = _mm(oa, w["br_a"], name="br_a", outs=(_MXU,))
    pb = _mm(ob, w["br_b"], name="br_b", outs=(_MXU,))
    pc = _mm(oc, w["br_c"], name="br_c", outs=(_MXU,))
    merged = _gate_merge(hg, sm["b_gate"], pa, pb, pc, TM)
    ln = dict(outs=(F32, F32, _MXU), epilogue=_ln_epilogue(alpha), tm=512, tn=d)
    r1, x1, x1b = _mm(merged, w["out"], name="w_out_ln1", extras=(x, sm["ln1_g"], sm["ln1_b"]), **ln)
    act = _mm(x1b, w["up"], name="w_up", outs=(_MXU,), epilogue=_relu2)
    r2, x2, x2b = _mm(act, w["down"], name="w_down_ln2", extras=(x1, sm["ln2_g"], sm["ln2_b"]), **ln)
    saved = dict(xb=xb, hb=hb, hg=hg, qa=qa, ka=ka, va=va, oa=oa, oas=oas, lse_a=lse_a, qb=qb, kd=kd, vd=vd, ob=ob, lse_b=lse_b,
                 qc=qc, kc=kc, vc=vc, oc=oc, lse_c=lse_c, bt=bt, pa=pa, pb=pb, pc=pc, merged=merged, r1=r1, x1b=x1b,
                 act=act, r2=r2)
    return x2, x2b, saved


def _layer_bwd(dx2, w, sm, sv, tabs, alpha):
    tab_a, tab_q, tab_k = tabs
    s, d = dx2.shape
    g = {}
    dr2, dr2b, dg2, db2 = _ln_bwd(dx2, sv["r2"], sm["ln2_g"], "ln2_bwd", TM)
    g["ln2_g"], g["ln2_b"] = dg2.sum(0), db2.sum(0)
    du = _mm(dr2b, w["down"], mode="nt", name="d_act", outs=(_MXU,), extras=(sv["act"],),
             epilogue=lambda acc, act: (acc * (2.0 * jnp.sqrt(act.astype(F32))),))
    g["w_down"] = _mm(sv["act"], dr2b, mode="tn", name="g_w_down", outs=(_MXU,)).reshape(4, -1, d)
    g["w_up"] = _mm(sv["x1b"], du, mode="tn", name="g_w_up", outs=(_MXU,), out_chips=True)
    dx1 = _mm(du, w["up"], mode="nt", name="d_x1", extras=(dr2,), epilogue=lambda acc, e: (acc + alpha * e,))
    dr1, dr1b, dg1, db1 = _ln_bwd(dx1, sv["r1"], sm["ln1_g"], "ln1_bwd", TM)
    g["ln1_g"], g["ln1_b"] = dg1.sum(0), db1.sum(0)
    g["w_out"] = _mm(sv["merged"], dr1b, mode="tn", name="g_w_out", outs=(_MXU,)).reshape(4, -1, d)
    dmerged = _mm(dr1b, w["out"], mode="nt", name="d_merged")
    dpa, dpb, dpc, dlog, gb = _gate_bwd(dmerged, sv["hg"], sm["b_gate"], sv["pa"], sv["pb"], sv["pc"], TM)
    g["b_gate"] = gb.sum(0)
    g["w_branch_a"] = _mm(sv["oa"], dpa, mode="tn", name="g_br_a", outs=(_MXU,), out_chips=True)
    g["w_branch_b"] = _mm(sv["ob"], dpb, mode="tn", name="g_br_b", outs=(_MXU,), out_chips=True)
    g["w_branch_c"] = _mm(sv["oc"], dpc, mode="tn", name="g_br_c", outs=(_MXU,), out_chips=True)
    doa = _mm(dpa, w["br_a"], mode="nt", name="d_oa")
    dob = _mm(dpb, w["br_b"], mode="nt", name="d_ob")
    doc = _mm(dpc, w["br_c"], mode="nt", name="d_oc")

    dqs, dks, dvs = [], [], []
    doas = _to_dilations(doa, "d_oa_layouts", TM)
    for dil in A_DILATIONS:
        band = _BandA(s // dil // BAND)
        args = [t[dil] for t in (sv["qa"], sv["ka"], sv["va"], doas, sv["oas"], sv["lse_a"])]
        dqs.append(_band_dq(band, *args)[0])
        dk_c, dv_c = _band_dkv(band, *args)
        dks.append(dk_c)
        dvs.append(dv_c)
    dha = _post_a(dqs, dks, dvs, tab_a, s, TM)

    dqb, dkd, dvd = _flash_bwd(sv["qb"], sv["kd"], sv["vd"], dob, sv["lse_b"], _delta_b(dob, sv["ob"], TQ_B), TQ_B, TK_B)
    dhb, gq, gk = _post_b(dqb, dkd, dvd, sv["hb"], sm["q_norm"], sm["k_norm"], tab_q, tab_k, TM)
    g["q_norm_b"] = gq.sum(0).reshape(-1, HEAD).sum(0)
    g["k_norm_b"] = gk.sum(0).reshape(-1, HEAD).sum(0)

    band_c = _BandC(s // BAND)
    cargs = (sv["qc"], sv["kc"], sv["vc"], doc, sv["oc"], sv["lse_c"], sv["bt"])
    dqc, dbt = _band_dq(band_c, *cargs)
    dkc, dvc = _band_dkv(band_c, *cargs)
    dhc = _post_c(dqc, dkc, dvc, TM)
    g["rpb_c"] = _bias_tiles_grad(dbt, 2 * C_ROWS - 1, 2 * C_COLS - 1)

    xb = sv["xb"]
    g["w_in"] = jnp.concatenate([_mm(xb, dh, mode="tn", name="g_in_" + nm)
                                 for nm, dh in (("a", dha), ("b", dhb), ("c", dhc), ("g", dlog))], 1)
    dx = _mm(dha, w["in"], mode="nt", name="d_x_a", tk=QKV_W, b_cols=QKV_W, b_off=0, extras=(dr1,),
             epilogue=lambda acc, e: (acc + alpha * e,))
    for nm, dh, off in (("b", dhb, 1), ("c", dhc, 2), ("g", dlog, 3)):
        dx = _mm(dh, w["in"], mode="nt", name="d_x_" + nm, tk=QKV_W, b_cols=dh.shape[1], b_off=off, extras=(dx,),
                 epilogue=lambda acc, e: (acc + e,))
    return dx, g


BIG = ("w_in", "w_branch_a", "w_branch_b", "w_branch_c", "w_out", "w_up", "w_down")
ROW_SHARDED = ("w_out", "w_down")
SMALL = ("b_gate", "q_norm_b", "k_norm_b", "rpb_c", "ln1_g", "ln1_b", "ln2_g", "ln2_b")


def _local_step(x, target, gathered, small):
    s, d = x.shape
    depth = gathered["w_in"].shape[1]
    alpha = (2 * depth) ** 0.25
    tabs = _tables(s)
    ws, sms = [], []
    names = dict(w_branch_a="br_a", w_branch_b="br_b", w_branch_c="br_c", w_out="out", w_up="up", w_down="down")
    w_in = _full_from_shards(gathered["w_in"], "w_in")
    for l in range(depth):
        w = {short: (gathered[n], l, "rows" if n in ROW_SHARDED else "cols") for n, short in names.items()}
        ws.append(dict(w, **{"in": (w_in, l)}))
        sms.append(dict(b_gate=small["b_gate"][l][None], q_norm=jnp.tile(small["q_norm_b"][l], BQ_W // HEAD)[None],
                        k_norm=jnp.tile(small["k_norm_b"][l], BKV_W // HEAD)[None], rpb=small["rpb_c"][l],
                        ln1_g=small["ln1_g"][l][None], ln1_b=small["ln1_b"][l][None],
                        ln2_g=small["ln2_g"][l][None], ln2_b=small["ln2_b"][l][None]))
    saved = []
    h, hb = x, x.astype(_MXU)
    for l in range(depth):
        h, hb, sv = _layer_fwd(h, hb, ws[l], sms[l], tabs, alpha)
        saved.append(sv)
    sq, dy = _loss_head(h, target, TM)
    grads = [None] * depth
    for l in reversed(range(depth)):
        dy, grads[l] = _layer_bwd(dy, ws[l], sms[l], saved[l], tabs, alpha)
    stacked = {k: jnp.stack([gl[k] for gl in grads], 1 if k in names else 0) for k in grads[0]}
    stacked["w_in"] = _shards_from_full(stacked["w_in"], "w_in").astype(_MXU)
    return sq, dy, stacked


def _place():
    return lax.axis_index("x"), lax.axis_index("y"), lax.axis_index("c")


def _flip(a, b):
    return a + b - 2 * a * b


def _other_chips(x, y):
    return [(1 - x, y), (x, 1 - y), (1 - x, 1 - y)]


def _comm_call(body, name, tensors, out_shapes, n_sems):
    n = len(tensors)

    def wrapped(*refs):
        body(refs[:n], refs[n:2 * n], refs[2 * n], refs[2 * n + 1])

    return pl.pallas_call(
        wrapped, name=name, in_specs=[ANY] * n, out_specs=[ANY] * n,
        out_shape=[jax.ShapeDtypeStruct(s, t.dtype) for s, t in zip(out_shapes, tensors)],
        scratch_shapes=[pltpu.SemaphoreType.DMA((n_sems, n)), pltpu.SemaphoreType.DMA((n_sems, n))],
    )(*tensors)


def _gather_shards(shards):
    lh = shards[0].shape[0] // 2

    def body(srcs, outs, send_sems, recv_sems):
        x, y, c = _place()
        me, sibling = (x, y), (x, y, 1 - c)
        n1, n2, dg = (_flip(x, 1 - c), _flip(y, c)), (_flip(x, c), _flip(y, 1 - c)), (1 - x, 1 - y)
        sends = []

        def half(t, chip, hc):
            return outs[t].at[2 * chip[0] + chip[1], pl.ds(hc * lh, lh)]

        def copy(k, t, src_ref, dst_ref, to):
            return pltpu.make_async_remote_copy(src_ref=src_ref, dst_ref=dst_ref, send_sem=send_sems.at[k, t],
                                                recv_sem=recv_sems.at[k, t], device_id=to, device_id_type=MESH)

        def start(cp):
            cp.start()
            sends.append(cp)

        for t, src in enumerate(srcs):
            own = src.at[pl.ds(c * lh, lh)]
            start(copy(0, t, own, half(t, me, c), (*n1, c)))
            start(copy(1, t, own, half(t, me, c), (*n2, c)))
            start(copy(6, t, src, outs[t].at[2 * x + y], sibling))
        for k, chip, j in ((0, n1, c), (1, n2, 1 - c), (2, dg, 2)):
            for t in range(len(srcs)):
                copy(k, t, half(t, chip, c), half(t, chip, c), sibling).wait_recv()
                if k == 0:
                    start(copy(2, t, half(t, n1, c), half(t, n1, c), (*n2, c)))
                start(copy(3 + j, t, half(t, chip, c), half(t, chip, c), sibling))
        for t, src in enumerate(srcs):
            for j, chip in enumerate(_other_chips(x, y)):
                copy(3 + j, t, half(t, chip, 1 - c), half(t, chip, 1 - c), sibling).wait_recv()
            copy(6, t, src, outs[t].at[2 * x + y], sibling).wait_recv()
        for cp in sends:
            cp.wait_send()

    return _comm_call(body, "gather_weights", shards, [(4,) + s.shape for s in shards], 7)


def _pair_exchange(parts):
    lh = parts[0].shape[1] // 2

    def body(srcs, outs, send_sems, recv_sems):
        x, y, c = _place()
        cps = [pltpu.make_async_remote_copy(src_ref=src.at[:, pl.ds((1 - c) * lh, lh)], dst_ref=out, send_sem=send_sems.at[0, t],
                                            recv_sem=recv_sems.at[0, t], device_id=(x, y, 1 - c), device_id_type=MESH)
               for t, (src, out) in enumerate(zip(srcs, outs))]
        for cp in cps:
            cp.start()
        for cp in cps:
            cp.wait()

    return _comm_call(body, "grad_pair_exchange", parts, [(4, lh) + p.shape[2:] for p in parts], 1)


def _chip_exchange(ts):
    def body(srcs, outs, send_sems, recv_sems):
        x, y, c = _place()
        cps = [pltpu.make_async_remote_copy(src_ref=src.at[2 * chip[0] + chip[1]], dst_ref=out.at[k], send_sem=send_sems.at[k, t],
                                            recv_sem=recv_sems.at[k, t], device_id=(*chip, c), device_id_type=MESH)
               for t, (src, out) in enumerate(zip(srcs, outs)) for k, chip in enumerate(_other_chips(x, y))]
        for cp in cps:
            cp.start()
        for cp in cps:
            cp.wait()

    return _comm_call(body, "grad_chip_exchange", ts, [(3,) + t.shape[1:] for t in ts], 3)


def _pair_share(halves):
    def body(srcs, outs, send_sems, recv_sems):
        x, y, c = _place()
        cps = [pltpu.make_async_remote_copy(src_ref=src, dst_ref=out, send_sem=send_sems.at[0, t], recv_sem=recv_sems.at[0, t],
                                            device_id=(x, y, 1 - c), device_id_type=MESH)
               for t, (src, out) in enumerate(zip(srcs, outs))]
        for cp in cps:
            cp.start()
        for cp in cps:
            cp.wait()

    theirs = _comm_call(body, "grad_pair_share", halves, [h.shape for h in halves], 1)
    c = jnp.reshape(lax.axis_index("c"), (1,)).astype(jnp.int32)
    return [_join_halves(mine, other, c, "grad_pair_join_%d" % t) for t, (mine, other) in enumerate(zip(halves, theirs))]


def _rows_view(t, lead):
    return t.reshape(t.shape[:lead] + (-1, t.shape[-1]))


def _join_halves(mine, theirs, c, name):
    shape = (2 * mine.shape[0],) + mine.shape[1:]
    mine, theirs = _rows_view(mine, 0), _rows_view(theirs, 0)
    rh, cols = mine.shape
    tr = _tile(rh, 1024, 8)

    def join(c_ref, mine_ref, theirs_ref, o_ref):
        o_ref[...] = jnp.where(pl.program_id(0) == c_ref[0], mine_ref[...], theirs_ref[...])

    spec = pl.BlockSpec((tr, cols), lambda h, i, c_ref: (i, 0))
    return pl.pallas_call(
        join, name=name,
        grid_spec=pltpu.PrefetchScalarGridSpec(
            num_scalar_prefetch=1, grid=(2, rh // tr), in_specs=[spec, spec],
            out_specs=pl.BlockSpec((tr, cols), lambda h, i, c_ref: (h * (rh // tr) + i, 0))),
        out_shape=jax.ShapeDtypeStruct((2 * rh, cols), mine.dtype),
        compiler_params=_params("parallel", "parallel"),
    )(c, mine, theirs).reshape(shape)


def _gather_all(v):
    r = v.shape[0]

    def body(src, out, send_sems, recv_sems, local_sem):
        x, y, c = _place()
        me = 4 * x + 2 * y + c
        mine = pltpu.make_async_copy(src, out.at[me], local_sem)
        mine.start()
        cps = []
        for k in range(1, 8):
            fx, fy, fc = (k >> 2) & 1, (k >> 1) & 1, k & 1
            peer = (x + fx - 2 * x * fx, y + fy - 2 * y * fy, c + fc - 2 * c * fc)
            cps.append(pltpu.make_async_remote_copy(src_ref=src, dst_ref=out.at[me], send_sem=send_sems.at[k - 1],
                                                    recv_sem=recv_sems.at[k - 1], device_id=peer, device_id_type=MESH))
        for cp in cps:
            cp.start()
        for k in range(1, 8):
            fx, fy, fc = (k >> 2) & 1, (k >> 1) & 1, k & 1
            frm = 4 * (x + fx - 2 * x * fx) + 2 * (y + fy - 2 * y * fy) + (c + fc - 2 * c * fc)
            pltpu.make_async_remote_copy(src_ref=src, dst_ref=out.at[frm], send_sem=send_sems.at[k - 1],
                                         recv_sem=recv_sems.at[k - 1], device_id=(x, y, c), device_id_type=MESH).wait_recv()
        for cp in cps:
            cp.wait_send()
        mine.wait()

    return pl.pallas_call(
        body, name="gather_small_grads", in_specs=[ANY], out_specs=ANY,
        out_shape=jax.ShapeDtypeStruct((8, r, 128), v.dtype),
        scratch_shapes=[pltpu.SemaphoreType.DMA((7,)), pltpu.SemaphoreType.DMA((7,)), pltpu.SemaphoreType.DMA],
    )(v)


def _sum_slots(parts, name):
    n, r, _ = parts.shape
    tr = _tile(r, 1024, 8)

    def body(p_ref, o_ref):
        acc = p_ref[0]
        for j in range(1, n):
            acc = acc + p_ref[j]
        o_ref[...] = acc

    return pl.pallas_call(
        body, name=name, grid=(r // tr,), in_specs=[pl.BlockSpec((n, tr, 128), lambda i: (0, i, 0))],
        out_specs=pl.BlockSpec((tr, 128), lambda i: (i, 0)), out_shape=jax.ShapeDtypeStruct((r, 128), parts.dtype),
        compiler_params=_params("parallel"),
    )(parts)


def _add_sibling_half(part, recv, c, name):
    shape = recv.shape
    part, recv = _rows_view(part, 1), _rows_view(recv, 1)
    _, rh, cols = recv.shape
    tr = _tile(rh, 1024, 16)
    nblk = rh // tr

    def body(c_ref, p_ref, r_ref, o_ref):
        o_ref[...] = (p_ref[...].astype(F32) + r_ref[...].astype(F32)).astype(o_ref.dtype)

    return pl.pallas_call(
        body, name=name,
        grid_spec=pltpu.PrefetchScalarGridSpec(
            num_scalar_prefetch=1, grid=(4, nblk),
            in_specs=[pl.BlockSpec((None, tr, cols), lambda j, i, c_ref: (j, c_ref[0] * nblk + i, 0)),
                      pl.BlockSpec((None, tr, cols), lambda j, i, c_ref: (j, i, 0))],
            out_specs=pl.BlockSpec((None, tr, cols), lambda j, i, c_ref: (j, i, 0))),
        out_shape=jax.ShapeDtypeStruct(recv.shape, recv.dtype),
        compiler_params=_params("parallel", "parallel"),
    )(c, part, recv).reshape(shape)


def _add_chips(t, recv, me, name):
    shape = t.shape[1:]
    t, recv = _rows_view(t, 1), _rows_view(recv, 1)
    _, rh, cols = t.shape
    tr = _tile(rh, 1024, 16)

    def body(me_ref, t_ref, r_ref, o_ref):
        f = lambda v: v.astype(F32)
        o_ref[...] = ((f(t_ref[...]) + f(r_ref[0])) + f(r_ref[1])) + f(r_ref[2])

    return pl.pallas_call(
        body, name=name,
        grid_spec=pltpu.PrefetchScalarGridSpec(
            num_scalar_prefetch=1, grid=(rh // tr,),
            in_specs=[pl.BlockSpec((None, tr, cols), lambda i, me_ref: (me_ref[0], i, 0)),
                      pl.BlockSpec((3, tr, cols), lambda i, me_ref: (0, i, 0))],
            out_specs=pl.BlockSpec((tr, cols), lambda i, me_ref: (i, 0))),
        out_shape=jax.ShapeDtypeStruct((rh, cols), F32),
        compiler_params=_params("parallel"),
    )(me, t, recv).reshape(shape)


def _reduce_scatter(parts):
    x, y, c = _place()
    c1, me = jnp.reshape(c, (1,)).astype(jnp.int32), jnp.reshape(2 * x + y, (1,)).astype(jnp.int32)
    ts = [_add_sibling_half(p, r, c1, "grad_pair_sum_%d" % i) for i, (p, r) in enumerate(zip(parts, _pair_exchange(parts)))]
    halves = [_add_chips(t, r, me, "grad_chip_sum_%d" % i) for i, (t, r) in enumerate(zip(ts, _chip_exchange(ts)))]
    return _pair_share(halves)


def _to_rows(parts, mult):
    flat = jnp.concatenate([p.reshape(-1) for p in parts])
    flat = jnp.pad(flat, (0, (-flat.size) % (128 * mult)))
    return flat.reshape(-1, 128)


def _from_rows(rows, shapes):
    flat, out, at = rows.reshape(-1), [], 0
    for shp in shapes:
        n = int(np.prod(shp))
        out.append(flat[at:at + n].reshape(shp))
        at += n
    return out


def _full_from_shards(g, name):
    _, depth, rows, cols = g.shape
    if name in ROW_SHARDED:
        return jnp.moveaxis(g, 0, 1).reshape(depth, 4 * rows, cols)
    return jnp.moveaxis(g, 0, 2).reshape(depth, rows, 4 * cols)


def _shards_from_full(full, name):
    depth, rows, cols = full.shape
    if name in ROW_SHARDED:
        return jnp.moveaxis(full.reshape(depth, 4, rows // 4, cols), 1, 0)
    return jnp.moveaxis(full.reshape(depth, rows, 4, cols // 4), 2, 0)


def kernel(x, w_in, b_gate, q_norm_b, k_norm_b, rpb_c, w_branch_a, w_branch_b, w_branch_c, w_out, ln1_g, ln1_b, w_up, w_down, ln2_g, ln2_b, loss_target, m_w_in, m_b_gate, m_q_norm_b, m_k_norm_b, m_rpb_c, m_w_branch_a, m_w_branch_b, m_w_branch_c, m_w_out, m_ln1_g, m_ln1_b, m_w_up, m_w_down, m_ln2_g, m_ln2_b, v_w_in, v_b_gate, v_q_norm_b, v_k_norm_b, v_rpb_c, v_w_branch_a, v_w_branch_b, v_w_branch_c, v_w_out, v_ln1_g, v_ln1_b, v_w_up, v_w_down, v_ln2_g, v_ln2_b):
    args = dict(locals())
    big_shard = {n: args[n] for n in BIG}
    small = {n: args[n] for n in SMALL}

    gathered = dict(zip(BIG, _gather_shards([big_shard[n].astype(_MXU) for n in BIG])))

    sq, grad_x, grads = _local_step(x[0], loss_target[0], gathered, small)
    loss = lax.psum(0.5 * jnp.sum(sq) / x.shape[-1], ("x", "y", "c"))

    g_big = _reduce_scatter([grads[n] for n in BIG])
    small_shapes = [small[n].shape for n in SMALL]
    g_small = _from_rows(_sum_slots(_gather_all(_to_rows([grads[n] for n in SMALL], 8)), "small_grad_sum"), small_shapes)
    grad = dict(zip(BIG, g_big))
    grad.update(zip(SMALL, g_small))

    delta, new_m, new_v = {}, {}, {}
    for n in BIG:
        shp = big_shard[n].shape
        two_d = lambda t: t.reshape(-1, shp[-1])
        res = _adamw(two_d(big_shard[n]), two_d(grad[n]), two_d(args["m_" + n]), two_d(args["v_" + n]), "adamw_" + n)
        delta[n], new_m[n], new_v[n] = [t.reshape(shp) for t in res]
    packed = [_to_rows([args[pre + n] for n in SMALL], 8) for pre in ("", "m_", "v_")]
    res = _adamw(packed[0], _to_rows([grad[n] for n in SMALL], 8), packed[1], packed[2], "adamw_small")
    for dst, rows in zip((delta, new_m, new_v), res):
        dst.update(zip(SMALL, _from_rows(rows, small_shapes)))

    order = ("w_in", "b_gate", "q_norm_b", "k_norm_b", "rpb_c", "w_branch_a", "w_branch_b", "w_branch_c", "w_out",
             "ln1_g", "ln1_b", "w_up", "w_down", "ln2_g", "ln2_b")
    return (loss, grad_x[None], *[grad[n] for n in order], *[delta[n] for n in order],
            *[new_m[n] for n in order], *[new_v[n] for n in order])
```

```python
import functools

import numpy as np
import jax
import jax.numpy as jnp
from jax import lax
from jax.experimental import pallas as pl
from jax.experimental.pallas import tpu as pltpu

F32 = jnp.float32
_MXU = jnp.bfloat16

HEAD = 64
A_W, BQ_W, BKV_W, C_W = 256, 512, 128, 256
QKV_W = 768
A_DILATIONS = (1, 4, 16)
A_RADIUS = 64
A_ROPE_HALF = 8
AX_ROPE_HALF = 16
ROPE_THETA = 500000.0
AX_THETA = 10000.0
GRID_W = 64
C_ROWS = 8
C_COLS = 16
BAND = 128
BT_TILES = 18
LN_EPS = 1e-5
RMS_EPS = 1e-6
NEG = -1e30
SCALE = HEAD ** -0.5
LOG2E = 1.4426950408889634
LN2 = 0.6931471805599453
ADAM_LR, ADAM_B1, ADAM_B2, ADAM_EPS, ADAM_WD, ADAM_STEP = 0.001, 0.9, 0.999, 1e-08, 0.01, 10
V7X_VMEM_LIMIT = 48 * 1024 * 1024
MESH = pl.DeviceIdType.MESH
ANY = pl.BlockSpec(memory_space=pl.ANY)


def _params(*sem):
    return pltpu.CompilerParams(dimension_semantics=sem or None, vmem_limit_bytes=V7X_VMEM_LIMIT)


def _tile(n, pref, align=128):
    if n <= pref:
        return n
    t = (pref // align) * align
    while t >= align:
        if n % t == 0:
            return t
        t -= align
    return n


def _mm(a, b, *, name, mode="nn", outs=((F32),), epilogue=None, extras=(), tm=1024, tn=1024, tk=2048, exact=False,
        b_cols=None, b_off=0, out_chips=False):
    b, b_lead, b_axis = (tuple(b) + (None, None))[:3] if isinstance(b, tuple) else (b, None, None)
    m, k = a.shape if mode != "tn" else a.shape[::-1]
    b_rows = b.shape[-2] * (4 if b_axis == "rows" else 1)
    b_last = b_cols or b.shape[-1] * (4 if b_axis == "cols" else 1)
    k2, n = (b_rows, b_last) if mode != "nt" else (b_last, b_rows)
    assert k == k2, (a.shape, b.shape, mode)
    cap_rows = b.shape[-2] if b_axis == "rows" else None
    cap_cols = b.shape[-1] if b_axis == "cols" else (n // 4 if out_chips else None)
    cap_n, cap_k = (cap_cols, cap_rows) if mode != "nt" else (cap_rows, cap_cols)
    tm, tn, tk = _tile(m, tm), _tile(cap_n or n, min(tn, cap_n or tn)), _tile(cap_k or k, min(tk, cap_k or tk))
    nk = k // tk
    n_ex, n_out = len(extras), len(outs)
    mx = F32 if exact else _MXU
    prec = lax.Precision.HIGHEST if exact else None
    dims = {"nn": (((1,), (0,)), ((), ())), "nt": (((1,), (1,)), ((), ())), "tn": (((0,), (0,)), ((), ()))}[mode]

    def body(*refs):
        a_ref, b_ref = refs[0], refs[1]
        ex = refs[2:2 + n_ex]
        out_refs = refs[2 + n_ex:2 + n_ex + n_out]
        kk = pl.program_id(2)
        av, bv = a_ref[...].astype(mx), b_ref[...].astype(mx)
        part = lax.dot_general(av, bv, dims, preferred_element_type=F32, precision=prec)

        def finish(res):
            vals = epilogue(res, *[e[...] for e in ex]) if epilogue is not None else (res,)
            for o, v in zip(out_refs, vals):
                o[...] = v.astype(o.dtype)

        if nk == 1:
            finish(part)
        else:
            acc = refs[-1]

            @pl.when(kk == 0)
            def _():
                acc[...] = part

            @pl.when((kk > 0) & (kk < nk - 1))
            def _():
                acc[...] += part

            @pl.when(kk == nk - 1)
            def _():
                finish(acc[...] + part)

    a_spec = pl.BlockSpec((tm, tk), lambda i, j, kk: (i, kk)) if mode != "tn" else pl.BlockSpec((tk, tm), lambda i, j, kk: (kk, i))
    b_tile = (tn, tk) if mode == "nt" else (tk, tn)

    def b_index(i, j, kk):
        rc = [j, kk + b_off] if mode == "nt" else [kk, j + b_off]
        if b_axis is None:
            return (() if b_lead is None else (b_lead,)) + tuple(rc)
        ax = 0 if b_axis == "rows" else 1
        per = b.shape[-2 + ax] // b_tile[ax]
        chip, rc[ax] = rc[ax] // per, rc[ax] % per
        return (chip, b_lead) + tuple(rc)

    b_spec = pl.BlockSpec((None,) * (b.ndim - 2) + b_tile, b_index)
    o_spec = pl.BlockSpec((tm, tn), lambda i, j, kk: (i, j))
    if out_chips:
        per_out = n // 4 // tn
        out_specs = [pl.BlockSpec((None, tm, tn), lambda i, j, kk: (j // per_out, i, j % per_out))] * n_out
        out_shape = [jax.ShapeDtypeStruct((4, m, n // 4), d) for d in outs]
    else:
        out_specs, out_shape = [o_spec] * n_out, [jax.ShapeDtypeStruct((m, n), d) for d in outs]
    res = pl.pallas_call(
        body, name=name, grid=(m // tm, n // tn, nk),
        in_specs=[a_spec, b_spec] + [o_spec if e.shape[0] > 1 else pl.BlockSpec((1, tn), lambda i, j, kk: (0, j)) for e in extras],
        out_specs=out_specs, out_shape=out_shape,
        scratch_shapes=[pltpu.VMEM((tm, tn), F32)] if nk > 1 else [],
        compiler_params=_params("parallel", "parallel", "arbitrary"),
    )(a, b, *extras)
    return res[0] if n_out == 1 else res


def _rows(tm, width, cb=0):
    return pl.BlockSpec((tm, width), lambda t: (t, cb))


def _whole(arr):
    nd = arr.ndim
    return pl.BlockSpec(arr.shape, lambda t: (0,) * nd)


def _rowwise(fn, name, rows, tm, ins, outs):
    n_in, n_out = len(ins), len(outs)
    dil_in = [spec[1:] if isinstance(spec, tuple) else None for _, spec in ins]
    in_specs = [_rows(tm // spec[1], spec[1] * spec[2]) if isinstance(spec, tuple) else spec for _, spec in ins]
    scratch = [pltpu.VMEM((di[1] // 128, tm, 128), F32) for di in dil_in if di] + \
              [pltpu.VMEM((n // 128, tm, 128), F32) for n, _, kind in outs if isinstance(kind, int)]

    def body(*refs):
        scr = list(refs[n_in + n_out:])
        blocks = []
        for r, di in zip(refs[:n_in], dil_in):
            if di is None:
                blocks.append(r[...])
            else:
                d, n = di
                s_ref = scr.pop(0)
                for j in range(d):
                    for b in range(n // 128):
                        lanes = slice(j * n + b * 128, j * n + (b + 1) * 128)
                        s_ref.at[b][pl.ds(j, tm // d, stride=d), :] = r[:, lanes].astype(F32)
                blocks.append(jnp.concatenate([s_ref[b] for b in range(n // 128)], 1))
        vals = fn(*blocks)
        first = pl.program_id(0) == 0
        for (ncols, _, kind), o, v in zip(outs, refs[n_in:n_in + n_out], vals):
            if kind == "row":
                o[...] = v.astype(o.dtype)
            elif isinstance(kind, int):
                s_ref = scr.pop(0)
                for b in range(ncols // 128):
                    s_ref[b] = v[:, b * 128:(b + 1) * 128].astype(F32)
                for j in range(kind):
                    for b in range(ncols // 128):
                        lanes = slice(j * ncols + b * 128, j * ncols + (b + 1) * 128)
                        o[:, lanes] = s_ref.at[b][pl.ds(j, tm // kind, stride=kind), :].astype(o.dtype)
            else:
                part = v.reshape(tm // 8, 8, ncols).sum(0)

                @pl.when(first)
                def _(o=o, part=part):
                    o[...] = part

                @pl.when(jnp.logical_not(first))
                def _(o=o, part=part):
                    o[...] += part

    def out_spec(n, kind):
        if kind == "row":
            return _rows(tm, n), (rows, n)
        if isinstance(kind, int):
            return _rows(tm // kind, kind * n), (rows // kind, kind * n)
        return pl.BlockSpec((8, n), lambda t: (0, 0)), (8, n)

    specs = [out_spec(n, kind) for n, _, kind in outs]
    res = pl.pallas_call(
        body, name=name, grid=(rows // tm,),
        in_specs=in_specs, out_specs=[s for s, _ in specs],
        out_shape=[jax.ShapeDtypeStruct(shp, d) for (_, shp), (_, d, _) in zip(specs, outs)],
        scratch_shapes=scratch, compiler_params=_params("arbitrary"),
    )(*[a for a, _ in ins])
    return res


def _lane_lo(width=128):
    return (lax.broadcasted_iota(jnp.int32, (1, width), 1) & (HEAD * 2 - 1)) < HEAD


def _group_sum(x):
    w = x.shape[-1]
    sh = HEAD.bit_length() - 1
    same = (lax.broadcasted_iota(jnp.int32, (w, w), 0) >> sh) == (lax.broadcasted_iota(jnp.int32, (w, w), 1) >> sh)
    ones = jnp.where(same, 1.0, 0.0).astype(jnp.bfloat16)
    hi = x.astype(jnp.bfloat16)
    lo = (x - hi.astype(F32)).astype(jnp.bfloat16)
    return jnp.dot(hi, ones, preferred_element_type=F32) + jnp.dot(lo, ones, preferred_element_type=F32)


def _rot(x, c, sm, sp, shift):
    w = x.shape[-1]
    return x * c + pltpu.roll(x, w - shift, 1) * sm + pltpu.roll(x, shift, 1) * sp


def _rot_t(dy, c, sm, sp, shift):
    w = dy.shape[-1]
    return dy * c + pltpu.roll(dy * sm, shift, 1) + pltpu.roll(dy * sp, w - shift, 1)


def _rope_tables(pos_parts, half, thetas):
    cs, sms, sps = [], [], []
    for pos, theta in zip(pos_parts, thetas):
        inv = theta ** (-jnp.arange(half, dtype=F32) / half)
        ang = pos.astype(F32)[:, None] * inv[None, :]
        co, si, ze = jnp.cos(ang), jnp.sin(ang), jnp.zeros_like(ang)
        cs += [co, co]
        sms += [-si, ze]
        sps += [ze, si]
    return [jnp.concatenate(t, axis=1) for t in (cs, sms, sps)]


def _tables(s):
    pos = jnp.arange(s)
    ca, sma, spa = _rope_tables([pos], A_ROPE_HALF, [ROPE_THETA])
    pad = HEAD - 2 * A_ROPE_HALF
    ca = jnp.concatenate([ca, jnp.ones((s, pad), F32)], 1)
    sma, spa = [jnp.concatenate([t, jnp.zeros((s, pad), F32)], 1) for t in (sma, spa)]
    tab_a = [jnp.tile(t, (1, A_W // HEAD)) for t in (ca, sma, spa)]
    ax = _rope_tables([pos // GRID_W, pos % GRID_W], AX_ROPE_HALF, [AX_THETA, AX_THETA])
    tab_q = [jnp.tile(t, (1, BQ_W // HEAD)) for t in ax]
    tab_k = [jnp.tile(t, (1, BKV_W // HEAD)) for t in ax]
    return tab_a, tab_q, tab_k


def _prep_a(ha, tab, tm):
    s = ha.shape[0]

    def fn(h, c, sm, sp):
        q, k, v = h[:, :A_W], h[:, A_W:2 * A_W], h[:, 2 * A_W:]
        q, k = _rot(q, c, sm, sp, A_ROPE_HALF) * SCALE, _rot(k, c, sm, sp, A_ROPE_HALF)
        return [t for t in (q, k, v) for _ in A_DILATIONS]

    res = _rowwise(fn, "prep_a", s, tm, [(ha, _rows(tm, QKV_W))] + [(t, _rows(tm, A_W)) for t in tab],
                   [(A_W, _MXU, _dil_kind(d)) for _ in range(3) for d in A_DILATIONS])
    n = len(A_DILATIONS)
    return [dict(zip(A_DILATIONS, res[i * n:(i + 1) * n])) for i in range(3)]


def _dil_kind(d):
    return "row" if d == 1 else d


def _dil_spec(d, tm, ncols):
    return _rows(tm, ncols) if d == 1 else ("dil", d, ncols)


def _to_dilations(x, name, tm):
    s, n = x.shape
    res = _rowwise(lambda v: [v for d in A_DILATIONS if d > 1], name, s, tm, [(x, _rows(tm, n))],
                   [(n, x.dtype, d) for d in A_DILATIONS if d > 1])
    return {1: x, **dict(zip([d for d in A_DILATIONS if d > 1], res))}


def _rms(x, g):
    ms = _group_sum(x * x) * (1.0 / HEAD)
    return x * lax.rsqrt(ms + RMS_EPS) * g


def _prep_b(hb, gq, gk, tab_q, tab_k, tm):
    s = hb.shape[0]

    def fn(h, gq, gk, cq, smq, spq, ck, smk, spk):
        xq, xk, v = h[:, :BQ_W], h[:, BQ_W:BQ_W + BKV_W], h[:, BQ_W + BKV_W:]
        q = _rot(_rms(xq, gq), cq, smq, spq, AX_ROPE_HALF) * (SCALE * LOG2E)
        k = _rot(_rms(xk, gk), ck, smk, spk, AX_ROPE_HALF)
        lo = _lane_lo()
        kr, vr = pltpu.roll(k, HEAD, 1), pltpu.roll(v, HEAD, 1)
        kd = jnp.concatenate([jnp.where(lo, k, kr), jnp.where(lo, kr, k)], 1)
        vd = jnp.concatenate([jnp.where(lo, v, vr), jnp.where(lo, vr, v)], 1)
        v1 = jnp.concatenate([jnp.where(lo, v, 1.0), jnp.where(lo, vr, 1.0)], 1)
        return q, kd, vd, v1

    ins = [(hb, _rows(tm, QKV_W)), (gq, _whole(gq)), (gk, _whole(gk))]
    ins += [(t, _rows(tm, BQ_W)) for t in tab_q] + [(t, _rows(tm, BKV_W)) for t in tab_k]
    return _rowwise(fn, "prep_b", s, tm, ins, [(BQ_W, _MXU, "row")] + [(2 * BKV_W, _MXU, "row")] * 3)


def _prep_c(hc, tm):
    def fn(h):
        return h[:, :C_W] * SCALE, h[:, C_W:2 * C_W], h[:, 2 * C_W:]

    return _rowwise(fn, "prep_c", hc.shape[0], tm, [(hc, _rows(tm, QKV_W))], [(C_W, _MXU, "row")] * 3)


def _combine_a(os_, ms, ls, s, tm):
    def fn(o1, o2, o3, m1, m2, m3, l1, l2, l3):
        lo = _lane_lo()
        outs, lses = [], []
        for p in range(A_W // 128):
            st = slice(p * 256, (p + 1) * 256)
            mm = [m[:, st] for m in (m1, m2, m3)]
            ll = [l[:, st] for l in (l1, l2, l3)]
            mmax = jnp.maximum(jnp.maximum(mm[0], mm[1]), mm[2])
            ws = [jnp.exp(m - mmax) for m in mm]
            den = ws[0] * ll[0] + ws[1] * ll[1] + ws[2] * ll[2]
            lses.append(mmax + jnp.log(den))
            num = sum(jnp.where(lo, w[:, :128], w[:, 128:]) * o[:, p * 128:(p + 1) * 128] for w, o in zip(ws, (o1, o2, o3)))
            outs.append(num / jnp.where(lo, den[:, :128], den[:, 128:]))
        o, lse = jnp.concatenate(outs, 1), jnp.concatenate(lses, 1)
        return [o] * len(A_DILATIONS) + [lse] * len(A_DILATIONS)

    ins = [(t, _dil_spec(d, tm, w)) for ts, w in ((os_, A_W), (ms, 2 * A_W), (ls, 2 * A_W)) for t, d in zip(ts, A_DILATIONS)]
    res = _rowwise(fn, "combine_a", s, tm, ins,
                   [(w, F32, _dil_kind(d)) for w in (A_W, 2 * A_W) for d in A_DILATIONS])
    n = len(A_DILATIONS)
    return dict(zip(A_DILATIONS, res[:n])), dict(zip(A_DILATIONS, res[n:]))


def _gates(hg, bg, d):
    return [jax.nn.sigmoid(hg[:, i * d:(i + 1) * d] + bg[:, i * d:(i + 1) * d]) for i in range(3)]


def _gate_merge(hg, bg, pa, pb, pc, tm):
    s, d = pa.shape

    def fn(hg, bg, pa, pb, pc):
        g = _gates(hg, bg, d)
        return (g[0] * pa + g[1] * pb + g[2] * pc,)

    ins = [(hg, _rows(tm, 3 * d)), (bg, _whole(bg))] + [(p, _rows(tm, d)) for p in (pa, pb, pc)]
    return _rowwise(fn, "gate_merge", s, tm, ins, [(d, _MXU, "row")])[0]


def _gate_bwd(dm, hg, bg, pa, pb, pc, tm):
    s, d = pa.shape

    def fn(dm, hg, bg, pa, pb, pc):
        g = _gates(hg, bg, d)
        dlog = jnp.concatenate([dm * p * gi * (1.0 - gi) for p, gi in zip((pa, pb, pc), g)], 1)
        return dm * g[0], dm * g[1], dm * g[2], dlog, dlog

    ins = [(dm, _rows(tm, d)), (hg, _rows(tm, 3 * d)), (bg, _whole(bg))] + [(p, _rows(tm, d)) for p in (pa, pb, pc)]
    return _rowwise(fn, "gate_bwd", s, tm, ins, [(d, _MXU, "row")] * 3 + [(3 * d, _MXU, "row"), (3 * d, F32, "acc")])


def _ln_stats(r):
    mu = jnp.mean(r, -1, keepdims=True)
    xc = r - mu
    var = jnp.mean(xc * xc, -1, keepdims=True)
    rstd = lax.rsqrt(var + LN_EPS)
    return xc * rstd, rstd


def _ln_epilogue(alpha):
    def fn(br, x, g, b):
        r = alpha * x + br
        xhat, _ = _ln_stats(r)
        y = xhat * g + b
        return r, y, y

    return fn


def _ln_bwd(dy, r, g, name, tm):
    s, d = r.shape

    def fn(dy, r, g):
        xhat, rstd = _ln_stats(r)
        dxh = dy * g
        dr = rstd * (dxh - jnp.mean(dxh, -1, keepdims=True) - xhat * jnp.mean(dxh * xhat, -1, keepdims=True))
        return dr, dr, dy * xhat, dy

    ins = [(dy, _rows(tm, d)), (r, _rows(tm, d)), (g, _whole(g))]
    return _rowwise(fn, name, s, tm, ins, [(d, F32, "row"), (d, _MXU, "row"), (d, F32, "acc"), (d, F32, "acc")])


def _loss_head(y, target, tm):
    s, d = y.shape

    def fn(y, t):
        diff = y - t
        return diff * diff, diff * (1.0 / d)

    sq, dy = _rowwise(fn, "loss_head", s, tm, [(y, _rows(tm, d)), (target, _rows(tm, d))], [(d, F32, "acc"), (d, F32, "row")])
    return sq, dy


def _post_a(dqs, dks, dvs, tab, s, tm):
    def fn(q1, q2, q3, k1, k2, k3, v1, v2, v3, c, sm, sp):
        dq = _rot_t((q1 + q2 + q3) * SCALE, c, sm, sp, A_ROPE_HALF)
        dk = _rot_t(k1 + k2 + k3, c, sm, sp, A_ROPE_HALF)
        return (jnp.concatenate([dq, dk, v1 + v2 + v3], 1),)

    ins = [(t, _dil_spec(d, tm, A_W)) for ts in (dqs, dks, dvs) for t, d in zip(ts, A_DILATIONS)]
    ins += [(t, _rows(tm, A_W)) for t in tab]
    return _rowwise(fn, "post_a", s, tm, ins, [(QKV_W, _MXU, "row")])[0]


def _post_b(dq, dkd, dvd, hb, gq, gk, tab_q, tab_k, tm):
    s = dq.shape[0]

    def back(dz, x, g, c, sm, sp):
        dy = _rot_t(dz, c, sm, sp, AX_ROPE_HALF)
        rstd = lax.rsqrt(_group_sum(x * x) * (1.0 / HEAD) + RMS_EPS)
        xh = x * rstd
        dxh = dy * g
        return rstd * (dxh - xh * (_group_sum(dxh * xh) * (1.0 / HEAD))), dy * xh

    def fn(dq, dkd, dvd, h, gq, gk, cq, smq, spq, ck, smk, spk):
        lo = _lane_lo()
        dk = jnp.where(lo, dkd[:, :128], dkd[:, 128:])
        dv = jnp.where(lo, dvd[:, :128], dvd[:, 128:])
        dxq, dgq = back(dq * SCALE, h[:, :BQ_W], gq, cq, smq, spq)
        dxk, dgk = back(dk, h[:, BQ_W:BQ_W + BKV_W], gk, ck, smk, spk)
        return jnp.concatenate([dxq, dxk, dv], 1), dgq, dgk

    ins = [(dq, _rows(tm, BQ_W)), (dkd, _rows(tm, 2 * BKV_W)), (dvd, _rows(tm, 2 * BKV_W)), (hb, _rows(tm, QKV_W)),
           (gq, _whole(gq)), (gk, _whole(gk))]
    ins += [(t, _rows(tm, BQ_W)) for t in tab_q] + [(t, _rows(tm, BKV_W)) for t in tab_k]
    return _rowwise(fn, "post_b", s, tm, ins, [(QKV_W, _MXU, "row"), (BQ_W, F32, "acc"), (BKV_W, F32, "acc")])


def _post_c(dq, dk, dv, tm):
    def fn(dq, dk, dv):
        return (jnp.concatenate([dq * SCALE, dk, dv], 1),)

    return _rowwise(fn, "post_c", dq.shape[0], tm, [(t, _rows(tm, C_W)) for t in (dq, dk, dv)], [(QKV_W, _MXU, "row")])[0]


def _adamw(w, g, m, v, name):
    rows, cols = w.shape
    tm = _tile(rows, 256, 8)

    def fn(w, g, m, v):
        m = ADAM_B1 * m + (1.0 - ADAM_B1) * g
        v = ADAM_B2 * v + (1.0 - ADAM_B2) * (g * g)
        m_hat = m / (1.0 - ADAM_B1 ** ADAM_STEP)
        v_hat = v / (1.0 - ADAM_B2 ** ADAM_STEP)
        delta = -ADAM_LR * (m_hat / (jnp.sqrt(v_hat) + ADAM_EPS) + ADAM_WD * w)
        return delta, m, v

    return _rowwise(fn, name, rows, tm, [(t, _rows(tm, cols)) for t in (w, g, m, v)], [(cols, F32, "row")] * 3)


def _dot_t(a, b):
    return lax.dot_general(a, b, (((1,), (1,)), ((), ())), preferred_element_type=F32)


def _tdot(a, b):
    return lax.dot_general(a, b, (((0,), (0,)), ((), ())), preferred_element_type=F32)


def _head_masks():
    lo = _lane_lo()
    return lo, (lo, jnp.logical_not(lo))


def _rep(x, rows):
    return jnp.broadcast_to(x, (rows, 128))


def _row_lo():
    return lax.broadcasted_iota(jnp.int32, (128, 1), 0) < HEAD


def _flash_fwd(q, kd, v1, tq, tk):
    s = q.shape[0]
    tq, tk = _tile(s, tq), _tile(s, tk)
    nk = s // tk
    mx = _MXU

    def body(q_ref, k_ref, v_ref, o_ref, lse_ref, m_ref, acc_ref):
        kk = pl.program_id(2)

        @pl.when(kk == 0)
        def _():
            m_ref[...] = jnp.full_like(m_ref, NEG)
            acc_ref[...] = jnp.zeros_like(acc_ref)

        q2, k2, v2 = q_ref[...], k_ref[...], v_ref[...]
        _, masks = _head_masks()
        hs = range(2)
        st = [_dot_t(k2, jnp.where(masks[h], q2, jnp.zeros_like(q2))) for h in hs]
        m_prev = [m_ref[h] for h in hs]
        m_new = [jnp.maximum(m_prev[h], jnp.max(st[h], 0, keepdims=True)) for h in hs]
        p = [jnp.exp2(st[h] - m_new[h]).astype(mx) for h in hs]
        pv = [_tdot(v2, p[h]) for h in hs]
        for h in hs:
            m_ref[h] = m_new[h]
            acc_ref[h] = acc_ref[h] * jnp.exp2(m_prev[h] - m_new[h]) + pv[h]

        @pl.when(kk == nk - 1)
        def _():
            a0, a1 = acc_ref[0], acc_ref[1]
            l0, l1 = a0[HEAD:HEAD + 1], a1[HEAD:HEAD + 1]
            o_ref[...] = jnp.concatenate([a0[:HEAD] / l0, a1[:HEAD] / l1], 0).T
            lse_ref[...] = jnp.concatenate([m_ref[0] + jnp.log2(l0), m_ref[1] + jnp.log2(l1), jnp.zeros((6, tq), F32)], 0)

    return pl.pallas_call(
        body, name="attn_b_fwd", grid=(BQ_W // 128, s // tq, nk),
        in_specs=[pl.BlockSpec((tq, 128), lambda j, i, kk: (i, j)),
                  pl.BlockSpec((tk, 128), lambda j, i, kk: (kk, j // 2)),
                  pl.BlockSpec((tk, 128), lambda j, i, kk: (kk, j // 2))],
        out_specs=[pl.BlockSpec((tq, 128), lambda j, i, kk: (i, j)), pl.BlockSpec((None, 8, tq), lambda j, i, kk: (j, 0, i))],
        out_shape=[jax.ShapeDtypeStruct((s, BQ_W), F32), jax.ShapeDtypeStruct((BQ_W // 128, 8, s), F32)],
        scratch_shapes=[pltpu.VMEM((2, 1, tq), F32), pltpu.VMEM((2, 128, tq), F32)],
        compiler_params=_params("parallel", "parallel", "arbitrary"),
    )(q, kd, v1)


def _delta_b(do, o, tq):
    s = do.shape[0]
    tq = _tile(s, tq)

    def body(do_ref, o_ref, d_ref):
        prod = do_ref[...] * o_ref[...]
        row = lax.broadcasted_iota(jnp.int32, (8, 128), 0)
        lane = lax.broadcasted_iota(jnp.int32, (8, 128), 1)
        sel = jnp.where(((row == 0) & (lane < HEAD)) | ((row == 1) & (lane >= HEAD)), 1.0, 0.0).astype(F32)
        d_ref[...] = lax.dot_general(sel, prod, (((1,), (1,)), ((), ())), preferred_element_type=F32,
                                     precision=lax.Precision.HIGHEST)

    qs = pl.BlockSpec((tq, 128), lambda j, i: (i, j))
    return pl.pallas_call(
        body, name="attn_b_delta", grid=(BQ_W // 128, s // tq), in_specs=[qs, qs],
        out_specs=pl.BlockSpec((None, 8, tq), lambda j, i: (j, 0, i)),
        out_shape=jax.ShapeDtypeStruct((BQ_W // 128, 8, s), F32),
        compiler_params=_params("parallel", "parallel"),
    )(do, o)


def _flash_bwd(q, kd, vd, do, lse, delta, tq, tk):
    s = q.shape[0]
    tq, tk = _tile(s, tq), _tile(s, tk)
    nq, nk = s // tq, s // tk
    group = BQ_W // 128 // 2
    mx = _MXU

    def body(k_ref, v_ref, q_ref, do_ref, lse_ref, dl_ref, dq_hbm, dk_ref, dv_ref, dk_acc, dv_acc, dqt, stage, sem):
        e, kk, jj, i = pl.program_id(0), pl.program_id(1), pl.program_id(2), pl.program_id(3)

        @pl.when((jj == 0) & (i == 0))
        def _():
            dk_acc[...] = jnp.zeros_like(dk_acc)
            dv_acc[...] = jnp.zeros_like(dv_acc)

        @pl.when(kk == 0)
        def _():
            dqt[jj, i] = jnp.zeros((128, tq), F32)

        q2, k2, v2, do2 = q_ref[...], k_ref[...], v_ref[...], do_ref[...].astype(mx)
        lse8, dl8 = lse_ref[...], dl_ref[...]
        _, masks = _head_masks()
        hs = range(2)
        qh = [jnp.where(masks[h], q2, jnp.zeros_like(q2)) for h in hs]
        doh = [jnp.where(masks[h], do2, jnp.zeros_like(do2)) for h in hs]
        st = [_dot_t(k2, qh[h]) for h in hs]
        dpt = [_dot_t(v2, doh[h]) for h in hs]
        p = [jnp.exp2(st[h] - lse8[h:h + 1]) for h in hs]
        ds = [(p[h] * (dpt[h] - dl8[h:h + 1])).astype(mx) for h in hs]
        p = [p[h].astype(mx) for h in hs]
        dv_acc[...] += jnp.dot(p[0], doh[0], preferred_element_type=F32) + jnp.dot(p[1], doh[1], preferred_element_type=F32)
        dk_acc[...] += jnp.dot(ds[0], qh[0], preferred_element_type=F32) + jnp.dot(ds[1], qh[1], preferred_element_type=F32)
        dqt[jj, i] += jnp.where(_row_lo(), _tdot(k2, ds[0]), _tdot(k2, ds[1]))

        @pl.when(kk == nk - 1)
        def _():
            stage[...] = dqt[jj, i].T
            lane0 = pl.multiple_of((group * e + jj) * 128, 128)
            cp = pltpu.make_async_copy(stage, dq_hbm.at[pl.ds(pl.multiple_of(i * tq, tq), tq), pl.ds(lane0, 128)], sem)
            cp.start()
            cp.wait()

        @pl.when((jj == group - 1) & (i == nq - 1))
        def _():
            dk_ref[...] = (dk_acc[...] + pltpu.roll(dk_acc[...], HEAD, 1)) * LN2
            dv_ref[...] = dv_acc[...] + pltpu.roll(dv_acc[...], HEAD, 1)

    ks = pl.BlockSpec((tk, 128), lambda e, kk, jj, i: (kk, e))
    qs = pl.BlockSpec((tq, 128), lambda e, kk, jj, i: (i, group * e + jj))
    st = pl.BlockSpec((None, 8, tq), lambda e, kk, jj, i: (group * e + jj, 0, i))
    return pl.pallas_call(
        body, name="attn_b_bwd", grid=(BKV_W // HEAD, nk, group, nq),
        in_specs=[ks, ks, qs, qs, st, st], out_specs=[ANY, ks, ks],
        out_shape=[jax.ShapeDtypeStruct((s, BQ_W), F32)] + [jax.ShapeDtypeStruct((s, 2 * BKV_W), F32)] * 2,
        scratch_shapes=[pltpu.VMEM((tk, 128), F32)] * 2 + [pltpu.VMEM((group, nq, 128, tq), F32), pltpu.VMEM((tq, 128), F32),
                                                          pltpu.SemaphoreType.DMA],
        compiler_params=_params("parallel", "arbitrary", "arbitrary", "arbitrary"),
    )(kd, vd, q, do, lse, delta)


def _p_and_ds(items, masks):
    mx = _MXU
    keys = [(n, h) for n in range(len(items)) for h in range(2)]
    qh = {(n, h): jnp.where(masks[h], items[n][0], jnp.zeros_like(items[n][0])) for n, h in keys}
    doh = {(n, h): jnp.where(masks[h], items[n][3], jnp.zeros_like(items[n][3])) for n, h in keys}
    sc = {}
    for n, h in keys:
        s_h = _dot_t(qh[n, h], items[n][1])
        if items[n][7] is not None:
            s_h = s_h + items[n][7][h]
        sc[n, h] = jnp.where(items[n][6], s_h, NEG)
    dp = {(n, h): _dot_t(doh[n, h].astype(mx), items[n][2]) for n, h in keys}
    lse = {(n, h): jnp.max(items[n][5][:, h * 128:(h + 1) * 128], -1, keepdims=True) for n, h in keys}
    delta = {(n, h): jnp.sum(doh[n, h] * items[n][4], -1, keepdims=True) for n, h in keys}
    p = {key: jnp.exp(sc[key] - lse[key]) for key in keys}
    ds = {key: p[key] * (dp[key] - delta[key]) for key in keys}
    return [[(qh[n, h], p[n, h], ds[n, h], doh[n, h]) for h in range(2)] for n in range(len(items))]


class _BandA:
    hb, has_bias, name = 1, False, "a"

    def __init__(self, nb):
        self.nb = nb

    def mask(self, qidx, kidx):
        n = self.nb * BAND
        return (jnp.abs(qidx - kidx) <= A_RADIUS) & (kidx >= 0) & (kidx < n) & (qidx >= 0) & (qidx < n)


class _BandC:
    hb, has_bias, name = 3, True, "c"

    def __init__(self, nb):
        self.nb = nb
        self.rows = nb * BAND // GRID_W
        per = BAND // GRID_W
        assert self.rows >= C_ROWS and (C_ROWS - 1) // per <= self.hb
        assert (self.rows - 1) // per - (self.rows - C_ROWS) // per <= self.hb

    def mask(self, qidx, kidx):
        n = self.nb * BAND
        sh = GRID_W.bit_length() - 1
        qrow, cq = qidx >> sh, qidx & (GRID_W - 1)
        krow, ck = kidx >> sh, kidx & (GRID_W - 1)
        r0 = jnp.clip(qrow - C_ROWS // 2, 0, self.rows - C_ROWS)
        c0 = jnp.clip(cq - C_COLS // 2, 0, GRID_W - C_COLS)
        ok = (qidx >= 0) & (qidx < n) & (kidx >= 0) & (kidx < n)
        return ok & (krow >= r0) & (krow < r0 + C_ROWS) & (ck >= c0) & (ck < c0 + C_COLS)


def _bias_tile(off, a):
    return (BAND // GRID_W) * off - a + (C_ROWS - 1) + 2


def _band_bias_k(band, bt_ref, h):
    per = BAND // GRID_W
    return jnp.concatenate([jnp.concatenate([bt_ref[h, _bias_tile(off, a)] for off in range(-band.hb, band.hb + 1)], 1)
                            for a in range(per)], 0)


def _band_bias_q(band, bt_ref, h):
    per = BAND // GRID_W
    return jnp.concatenate([bt_ref[h, _bias_tile(-off, a)] for off in range(-band.hb, band.hb + 1) for a in range(per)], 0)


def _band_split(nb, ncb):
    cb = max(c for c in (4, 2, 1) if ncb % c == 0)
    rb = max(r for r in (4, 2, 1) if nb % r == 0 and r * cb <= 16)
    return rb, cb


def _band_specs(band, rb, cb, nb, width):
    def edge(first):
        return pl.BlockSpec((BAND, cb * width), lambda c, i: (jnp.clip(i * rb + first, 0, nb - 1), c))

    main = pl.BlockSpec((rb * BAND, cb * width), lambda c, i: (i, c))
    return [edge(t - band.hb) for t in range(band.hb)] + [main] + [edge(rb + t) for t in range(band.hb)]


def _band_rows(band, refs, rb, r, lanes):
    hb = band.hb
    parts = []
    for b in range(r, r + 2 * hb + 1):
        if b < hb:
            parts.append(refs[b][:, lanes])
        elif b < hb + rb:
            parts.append(refs[hb][(b - hb) * BAND:(b - hb + 1) * BAND, lanes])
        else:
            parts.append(refs[b - rb + 1][:, lanes])
    return jnp.concatenate(parts, 0)


def _band_idx(band, blk, rows_of_blocks, axis):
    shape = (rows_of_blocks * BAND, 1) if axis == 0 else (1, rows_of_blocks * BAND)
    return blk * BAND + lax.broadcasted_iota(jnp.int32, shape, axis)


def _band_fwd(band, q, k, v, bt=None):
    n, w = q.shape
    nb, ncb, nband = n // BAND, w // 128, 2 * band.hb + 1
    rb, cb = _band_split(nb, ncb)
    mx = _MXU
    raw = not band.has_bias

    def body(*refs):
        q_ref, k_refs, v_refs = refs[0], refs[1:1 + nband], refs[1 + nband:1 + 2 * nband]
        rest = refs[1 + 2 * nband:]
        bt_ref = rest[0] if band.has_bias else None
        outs = rest[1:] if band.has_bias else rest
        i = pl.program_id(1)
        lo, masks = _head_masks()
        subs = [(r, c) for r in range(rb) for c in range(cb)]
        mask = {r: band.mask(_band_idx(band, i * rb + r, 1, 0), _band_idx(band, i * rb + r - band.hb, nband, 1)) for r in range(rb)}
        lanes = {c: slice(c * 128, (c + 1) * 128) for c in range(cb)}
        rows = {r: slice(r * BAND, (r + 1) * BAND) for r in range(rb)}
        sc = {}
        for r, c in subs:
            q2, kcat = q_ref[rows[r], lanes[c]], _band_rows(band, k_refs, rb, r, lanes[c])
            for h in range(2):
                s_h = _dot_t(jnp.where(masks[h], q2, jnp.zeros_like(q2)), kcat)
                if band.has_bias:
                    s_h = s_h + _band_bias_k(band, bt_ref, 2 * c + h)
                sc[r, c, h] = jnp.where(mask[r], s_h, NEG)
        ms = {key: jnp.max(s_h, -1, keepdims=True) for key, s_h in sc.items()}
        ps = {key: jnp.exp(s_h - ms[key]) for key, s_h in sc.items()}
        ls = {key: jnp.sum(p, -1, keepdims=True) for key, p in ps.items()}
        os_ = {}
        for r, c in subs:
            vcat = _band_rows(band, v_refs, rb, r, lanes[c])
            for h in range(2):
                os_[r, c, h] = jnp.dot(ps[r, c, h].astype(mx), vcat, preferred_element_type=F32)
        for r, c in subs:
            st_lanes = [slice(c * 256 + h * 128, c * 256 + (h + 1) * 128) for h in range(2)]
            if raw:
                o_ref, m_ref, l_ref = outs
                o_ref[rows[r], lanes[c]] = jnp.where(lo, os_[r, c, 0], os_[r, c, 1])
                for h in range(2):
                    m_ref[rows[r], st_lanes[h]] = _rep(ms[r, c, h], BAND)
                    l_ref[rows[r], st_lanes[h]] = _rep(ls[r, c, h], BAND)
            else:
                o_ref, lse_ref = outs
                o_ref[rows[r], lanes[c]] = jnp.where(lo, os_[r, c, 0] / ls[r, c, 0], os_[r, c, 1] / ls[r, c, 1])
                for h in range(2):
                    lse_ref[rows[r], st_lanes[h]] = _rep(ms[r, c, h] + jnp.log(ls[r, c, h]), BAND)

    qs = pl.BlockSpec((rb * BAND, cb * 128), lambda c, i: (i, c))
    ks = _band_specs(band, rb, cb, nb, 128)
    st = pl.BlockSpec((rb * BAND, cb * 256), lambda c, i: (i, c))
    in_specs, args = [qs] + ks + ks, [q] + [k] * nband + [v] * nband
    if band.has_bias:
        in_specs.append(pl.BlockSpec((2 * cb, BT_TILES, GRID_W, 128), lambda c, i: (c, 0, 0, 0)))
        args.append(bt)
    n_stats = 2 if raw else 1
    return pl.pallas_call(
        body, name="attn_%s_fwd" % band.name, grid=(ncb // cb, nb // rb), in_specs=in_specs,
        out_specs=[qs] + [st] * n_stats,
        out_shape=[jax.ShapeDtypeStruct((n, w), F32)] + [jax.ShapeDtypeStruct((n, 2 * w), F32)] * n_stats,
        compiler_params=_params("parallel", "arbitrary"),
    )(*args)


def _band_dq(band, q, k, v, do, o, lse, bt=None):
    n, w = q.shape
    nb, ncb, nband = n // BAND, w // 128, 2 * band.hb + 1
    rb, cb = _band_split(nb, ncb)
    mx = _MXU
    per = BAND // GRID_W

    def body(*refs):
        q_ref, k_refs, v_refs = refs[0], refs[1:1 + nband], refs[1 + nband:1 + 2 * nband]
        do_ref, o_ref, lse_ref = refs[1 + 2 * nband:4 + 2 * nband]
        rest = refs[4 + 2 * nband:]
        dq_ref = rest[1] if band.has_bias else rest[0]
        i = pl.program_id(1)
        lo, masks = _head_masks()
        if band.has_bias:
            dbt_ref = rest[2]

            @pl.when(i == 0)
            def _():
                dbt_ref[...] = jnp.zeros_like(dbt_ref)

        subs = [(r, c) for r in range(rb) for c in range(cb)]
        mask = {r: band.mask(_band_idx(band, i * rb + r, 1, 0), _band_idx(band, i * rb + r - band.hb, nband, 1)) for r in range(rb)}
        items, kcats = [], []
        for r, c in subs:
            lanes, rows = slice(c * 128, (c + 1) * 128), slice(r * BAND, (r + 1) * BAND)
            kcats.append(_band_rows(band, k_refs, rb, r, lanes))
            bias = [_band_bias_k(band, rest[0], 2 * c + h) for h in range(2)] if band.has_bias else None
            items.append((q_ref[rows, lanes], kcats[-1], _band_rows(band, v_refs, rb, r, lanes), do_ref[rows, lanes],
                          o_ref[rows, lanes], lse_ref[rows, c * 256:(c + 1) * 256], mask[r], bias))
        res = _p_and_ds(items, masks)
        dqs = [[jnp.dot(ds.astype(mx), kcat, preferred_element_type=F32) for _, _, ds, _ in hs] for hs, kcat in zip(res, kcats)]
        for (r, c), hs, dq in zip(subs, res, dqs):
            dq_ref[r * BAND:(r + 1) * BAND, c * 128:(c + 1) * 128] = jnp.where(lo, dq[0], dq[1])
            if band.has_bias:
                for h in range(2):
                    ds = hs[h][2]
                    for a in range(per):
                        for t in range(nband):
                            tile = ds[a * GRID_W:(a + 1) * GRID_W, t * 128:(t + 1) * 128]
                            dbt_ref[2 * c + h, _bias_tile(t - band.hb, a)] += tile

    qs = pl.BlockSpec((rb * BAND, cb * 128), lambda c, i: (i, c))
    ks = _band_specs(band, rb, cb, nb, 128)
    st = pl.BlockSpec((rb * BAND, cb * 256), lambda c, i: (i, c))
    in_specs, args = [qs] + ks + ks + [qs, qs, st], [q] + [k] * nband + [v] * nband + [do, o, lse]
    out_specs, out_shape = [qs], [jax.ShapeDtypeStruct((n, w), F32)]
    if band.has_bias:
        bts = pl.BlockSpec((2 * cb, BT_TILES, GRID_W, 128), lambda c, i: (c, 0, 0, 0))
        in_specs.append(bts)
        args.append(bt)
        out_specs.append(bts)
        out_shape.append(jax.ShapeDtypeStruct(bt.shape, F32))
    return pl.pallas_call(
        body, name="attn_%s_dq" % band.name, grid=(ncb // cb, nb // rb), in_specs=in_specs, out_specs=out_specs,
        out_shape=out_shape, compiler_params=_params("parallel", "arbitrary"),
    )(*args)


def _band_dkv(band, q, k, v, do, o, lse, bt=None):
    n, w = q.shape
    nb, ncb, nband = n // BAND, w // 128, 2 * band.hb + 1
    rb, cb = _band_split(nb, ncb)
    mx = _MXU

    def body(*refs):
        k_ref, v_ref = refs[0], refs[1]
        q_refs, do_refs, o_refs, lse_refs = [refs[2 + g * nband:2 + (g + 1) * nband] for g in range(4)]
        rest = refs[2 + 4 * nband:]
        dk_ref, dv_ref = rest[-2], rest[-1]
        i = pl.program_id(1)
        _, masks = _head_masks()
        subs = [(r, c) for r in range(rb) for c in range(cb)]
        mask = {r: band.mask(_band_idx(band, i * rb + r - band.hb, nband, 0), _band_idx(band, i * rb + r, 1, 1)) for r in range(rb)}
        items = []
        for r, c in subs:
            lanes, rows = slice(c * 128, (c + 1) * 128), slice(r * BAND, (r + 1) * BAND)
            qcat, docat, ocat = [_band_rows(band, g, rb, r, lanes) for g in (q_refs, do_refs, o_refs)]
            lsecat = _band_rows(band, lse_refs, rb, r, slice(c * 256, (c + 1) * 256))
            bias = [_band_bias_q(band, rest[0], 2 * c + h) for h in range(2)] if band.has_bias else None
            items.append((qcat, k_ref[rows, lanes], v_ref[rows, lanes], docat, ocat, lsecat, mask[r], bias))
        res = _p_and_ds(items, masks)
        dks = [sum(_tdot(ds.astype(mx), qh) for qh, _, ds, _ in hs) for hs in res]
        dvs = [sum(_tdot(p.astype(mx), doh.astype(mx)) for _, p, _, doh in hs) for hs in res]
        for (r, c), dk, dv in zip(subs, dks, dvs):
            dk_ref[r * BAND:(r + 1) * BAND, c * 128:(c + 1) * 128] = dk
            dv_ref[r * BAND:(r + 1) * BAND, c * 128:(c + 1) * 128] = dv

    ks = pl.BlockSpec((rb * BAND, cb * 128), lambda c, i: (i, c))
    in_specs = [ks, ks] + _band_specs(band, rb, cb, nb, 128) * 3 + _band_specs(band, rb, cb, nb, 256)
    args = [k, v] + [q] * nband + [do] * nband + [o] * nband + [lse] * nband
    if band.has_bias:
        in_specs.append(pl.BlockSpec((2 * cb, BT_TILES, GRID_W, 128), lambda c, i: (c, 0, 0, 0)))
        args.append(bt)
    return pl.pallas_call(
        body, name="attn_%s_dkv" % band.name, grid=(ncb // cb, nb // rb), in_specs=in_specs, out_specs=[ks, ks],
        out_shape=[jax.ShapeDtypeStruct((n, w), F32)] * 2,
        compiler_params=_params("parallel", "arbitrary"),
    )(*args)


def _dc_onehot():
    c = np.arange(GRID_W)
    dc = np.clip(c[None, :] - c[:, None] + (C_COLS - 1), 0, 2 * C_COLS - 2).reshape(-1)
    m = np.zeros((GRID_W * GRID_W, 128), np.float32)
    m[np.arange(dc.size), dc] = 1.0
    return m


def _bias_tiles(rpb):
    h, nr, ncol = rpb.shape
    flat = jnp.pad(rpb.reshape(h * nr, ncol), ((0, (-h * nr) % 8), (0, 128 - ncol)))
    tiles = _mm(flat, jnp.asarray(_dc_onehot().T), name="rpb_tiles", exact=True, tn=GRID_W * GRID_W)
    tiles = tiles[:h * nr].reshape(h, nr, GRID_W, GRID_W)
    tiles = jnp.pad(tiles, ((0, 0), (2, BT_TILES + 1 - nr - 2), (0, 0), (0, 0)))
    return jnp.concatenate([tiles[:, :BT_TILES], tiles[:, 1:BT_TILES + 1]], -1)


def _bias_tiles_grad(dbt, nr, ncol):
    h = dbt.shape[0]
    d = dbt[:, 2:2 + nr, :, :GRID_W] + dbt[:, 1:1 + nr, :, GRID_W:]
    flat = jnp.pad(d.reshape(h * nr, GRID_W * GRID_W), ((0, (-h * nr) % 8), (0, 0)))
    g = _mm(flat, jnp.asarray(_dc_onehot()), name="rpb_grad", exact=True, tk=GRID_W * GRID_W)
    return g[:h * nr, :ncol].reshape(h, nr, ncol)


TM = 256
TQ_B, TK_B = 1024, 2048


def _relu2(acc):
    r = jnp.maximum(acc, 0.0)
    return (r * r,)


def _layer_fwd(x, xb, w, sm, tabs, alpha):
    tab_a, tab_q, tab_k = tabs
    s, d = x.shape
    ha = _mm(xb, w["in"], name="in_a", tn=QKV_W, b_cols=QKV_W, b_off=0)
    hb = _mm(xb, w["in"], name="in_b", tn=QKV_W, b_cols=QKV_W, b_off=1)
    hc = _mm(xb, w["in"], name="in_c", tn=QKV_W, b_cols=QKV_W, b_off=2)
    hg = _mm(xb, w["in"], name="in_g", outs=(_MXU,), tn=QKV_W, b_cols=3 * d, b_off=3)

    qa, ka, va = _prep_a(ha, tab_a, TM)
    stats = [_band_fwd(_BandA(s // dil // BAND), qa[dil], ka[dil], va[dil]) for dil in A_DILATIONS]
    oas, lse_a = _combine_a(*zip(*stats), s, TM)
    oa = oas[1]

    qb, kd, vd, v1 = _prep_b(hb, sm["q_norm"], sm["k_norm"], tab_q, tab_k, TM)
    ob, lse_b = _flash_fwd(qb, kd, v1, TQ_B, TK_B)

    qc, kc, vc = _prep_c(hc, TM)
    bt = _bias_tiles(sm["rpb"])
    oc, lse_c = _band_fwd(_BandC(s // BAND), qc, kc, vc, bt)

    pa = _mm(oa, w["br_a"], name="br_a", outs=(_MXU,))
    pb = _mm(ob, w["br_b"], name="br_b", outs=(_MXU,))
    pc = _mm(oc, w["br_c"], name="br_c", outs=(_MXU,))
    merged = _gate_merge(hg, sm["b_gate"], pa, pb, pc, TM)
    ln = dict(outs=(F32, F32, _MXU), epilogue=_ln_epilogue(alpha), tm=512, tn=d)
    r1, x1, x1b = _mm(merged, w["out"], name="w_out_ln1", extras=(x, sm["ln1_g"], sm["ln1_b"]), **ln)
    act = _mm(x1b, w["up"], name="w_up", outs=(_MXU,), epilogue=_relu2)
    r2, x2, x2b = _mm(act, w["down"], name="w_down_ln2", extras=(x1, sm["ln2_g"], sm["ln2_b"]), **ln)
    saved = dict(xb=xb, hb=hb, hg=hg, qa=qa, ka=ka, va=va, oa=oa, oas=oas, lse_a=lse_a, qb=qb, kd=kd, vd=vd, ob=ob, lse_b=lse_b,
                 qc=qc, kc=kc, vc=vc, oc=oc, lse_c=lse_c, bt=bt, pa=pa, pb=pb, pc=pc, merged=merged, r1=r1, x1b=x1b,
                 act=act, r2=r2)
    return x2, x2b, saved


def _layer_bwd(dx2, w, sm, sv, tabs, alpha):
    tab_a, tab_q, tab_k = tabs
    s, d = dx2.shape
    g = {}
    dr2, dr2b, dg2, db2 = _ln_bwd(dx2, sv["r2"], sm["ln2_g"], "ln2_bwd", TM)
    g["ln2_g"], g["ln2_b"] = dg2.sum(0), db2.sum(0)
    du = _mm(dr2b, w["down"], mode="nt", name="d_act", outs=(_MXU,), extras=(sv["act"],),
             epilogue=lambda acc, act: (acc * (2.0 * jnp.sqrt(act.astype(F32))),))
    g["w_down"] = _mm(sv["act"], dr2b, mode="tn", name="g_w_down", outs=(_MXU,)).reshape(4, -1, d)
    g["w_up"] = _mm(sv["x1b"], du, mode="tn", name="g_w_up", outs=(_MXU,), out_chips=True)
    dx1 = _mm(du, w["up"], mode="nt", name="d_x1", extras=(dr2,), epilogue=lambda acc, e: (acc + alpha * e,))
    dr1, dr1b, dg1, db1 = _ln_bwd(dx1, sv["r1"], sm["ln1_g"], "ln1_bwd", TM)
    g["ln1_g"], g["ln1_b"] = dg1.sum(0), db1.sum(0)
    g["w_out"] = _mm(sv["merged"], dr1b, mode="tn", name="g_w_out", outs=(_MXU,))
    dmerged = _mm(dr1b, w["out"], mode="nt", name="d_merged")
    dpa, dpb, dpc, dlog, gb = _gate_bwd(dmerged, sv["hg"], sm["b_gate"], sv["pa"], sv["pb"], sv["pc"], TM)
    g["b_gate"] = gb.sum(0)
    g["w_branch_a"] = _mm(sv["oa"], dpa, mode="tn", name="g_br_a", outs=(_MXU,))
    g["w_branch_b"] = _mm(sv["ob"], dpb, mode="tn", name="g_br_b", outs=(_MXU,))
    g["w_branch_c"] = _mm(sv["oc"], dpc, mode="tn", name="g_br_c", outs=(_MXU,))
    doa = _mm(dpa, w["br_a"], mode="nt", name="d_oa")
    dob = _mm(dpb, w["br_b"], mode="nt", name="d_ob")
    doc = _mm(dpc, w["br_c"], mode="nt", name="d_oc")

    dqs, dks, dvs = [], [], []
    doas = _to_dilations(doa, "d_oa_layouts", TM)
    for dil in A_DILATIONS:
        band = _BandA(s // dil // BAND)
        args = [t[dil] for t in (sv["qa"], sv["ka"], sv["va"], doas, sv["oas"], sv["lse_a"])]
        dqs.append(_band_dq(band, *args)[0])
        dk_c, dv_c = _band_dkv(band, *args)
        dks.append(dk_c)
        dvs.append(dv_c)
    dha = _post_a(dqs, dks, dvs, tab_a, s, TM)

    dqb, dkd, dvd = _flash_bwd(sv["qb"], sv["kd"], sv["vd"], dob, sv["lse_b"], _delta_b(dob, sv["ob"], TQ_B), TQ_B, TK_B)
    dhb, gq, gk = _post_b(dqb, dkd, dvd, sv["hb"], sm["q_norm"], sm["k_norm"], tab_q, tab_k, TM)
    g["q_norm_b"] = gq.sum(0).reshape(-1, HEAD).sum(0)
    g["k_norm_b"] = gk.sum(0).reshape(-1, HEAD).sum(0)

    band_c = _BandC(s // BAND)
    cargs = (sv["qc"], sv["kc"], sv["vc"], doc, sv["oc"], sv["lse_c"], sv["bt"])
    dqc, dbt = _band_dq(band_c, *cargs)
    dkc, dvc = _band_dkv(band_c, *cargs)
    dhc = _post_c(dqc, dkc, dvc, TM)
    g["rpb_c"] = _bias_tiles_grad(dbt, 2 * C_ROWS - 1, 2 * C_COLS - 1)

    xb = sv["xb"]
    g["w_in"] = jnp.concatenate([_mm(xb, dh, mode="tn", name="g_in_" + nm, outs=(_MXU,))
                                 for nm, dh in (("a", dha), ("b", dhb), ("c", dhc), ("g", dlog))], 1)
    dx = _mm(dha, w["in"], mode="nt", name="d_x_a", tk=QKV_W, b_cols=QKV_W, b_off=0, extras=(dr1,),
             epilogue=lambda acc, e: (acc + alpha * e,))
    for nm, dh, off in (("b", dhb, 1), ("c", dhc, 2), ("g", dlog, 3)):
        dx = _mm(dh, w["in"], mode="nt", name="d_x_" + nm, tk=QKV_W, b_cols=dh.shape[1], b_off=off, extras=(dx,),
                 epilogue=lambda acc, e: (acc + e,))
    return dx, g


BIG = ("w_in", "w_branch_a", "w_branch_b", "w_branch_c", "w_out", "w_up", "w_down")
ROW_SHARDED = ("w_out", "w_down")
AS_GATHERED = ("w_up", "w_down")
SMALL = ("b_gate", "q_norm_b", "k_norm_b", "rpb_c", "ln1_g", "ln1_b", "ln2_g", "ln2_b")


def _local_step(x, target, gathered, small):
    s, d = x.shape
    depth = gathered["w_in"].shape[1]
    alpha = (2 * depth) ** 0.25
    tabs = _tables(s)
    ws, sms = [], []
    names = dict(w_in="in", w_branch_a="br_a", w_branch_b="br_b", w_branch_c="br_c", w_out="out", w_up="up", w_down="down")
    whole = {n: _full_from_shards(gathered[n], n) for n in names if n not in AS_GATHERED}
    for l in range(depth):
        ws.append({short: (whole[n], l) if n in whole else (gathered[n], l, "rows" if n in ROW_SHARDED else "cols")
                   for n, short in names.items()})
        sms.append(dict(b_gate=small["b_gate"][l][None], q_norm=jnp.tile(small["q_norm_b"][l], BQ_W // HEAD)[None],
                        k_norm=jnp.tile(small["k_norm_b"][l], BKV_W // HEAD)[None], rpb=small["rpb_c"][l],
                        ln1_g=small["ln1_g"][l][None], ln1_b=small["ln1_b"][l][None],
                        ln2_g=small["ln2_g"][l][None], ln2_b=small["ln2_b"][l][None]))
    saved = []
    h, hb = x, x.astype(_MXU)
    for l in range(depth):
        h, hb, sv = _layer_fwd(h, hb, ws[l], sms[l], tabs, alpha)
        saved.append(sv)
    sq, dy = _loss_head(h, target, TM)
    grads = [None] * depth
    for l in reversed(range(depth)):
        dy, grads[l] = _layer_bwd(dy, ws[l], sms[l], saved[l], tabs, alpha)
    stacked = {k: jnp.stack([gl[k] for gl in grads], 1 if k in AS_GATHERED else 0) for k in grads[0]}
    for n in whole:
        stacked[n] = _shards_from_full(stacked[n], n).astype(_MXU)
    return sq, dy, stacked


def _place():
    return lax.axis_index("x"), lax.axis_index("y"), lax.axis_index("c")


def _flip(a, b):
    return a + b - 2 * a * b


def _other_chips(x, y):
    return [(1 - x, y), (x, 1 - y), (1 - x, 1 - y)]


def _comm_call(body, name, tensors, out_shapes, n_sems):
    n = len(tensors)

    def wrapped(*refs):
        body(refs[:n], refs[n:2 * n], refs[2 * n], refs[2 * n + 1])

    return pl.pallas_call(
        wrapped, name=name, in_specs=[ANY] * n, out_specs=[ANY] * n,
        out_shape=[jax.ShapeDtypeStruct(s, t.dtype) for s, t in zip(out_shapes, tensors)],
        scratch_shapes=[pltpu.SemaphoreType.DMA((n_sems, n)), pltpu.SemaphoreType.DMA((n_sems, n))],
    )(*tensors)


def _gather_shards(shards):
    lh = shards[0].shape[0] // 2

    def body(srcs, outs, send_sems, recv_sems):
        x, y, c = _place()
        me, sibling = (x, y), (x, y, 1 - c)
        n1, n2, dg = (_flip(x, 1 - c), _flip(y, c)), (_flip(x, c), _flip(y, 1 - c)), (1 - x, 1 - y)
        sends = []

        def half(t, chip, hc):
            return outs[t].at[2 * chip[0] + chip[1], pl.ds(hc * lh, lh)]

        def copy(k, t, src_ref, dst_ref, to):
            return pltpu.make_async_remote_copy(src_ref=src_ref, dst_ref=dst_ref, send_sem=send_sems.at[k, t],
                                                recv_sem=recv_sems.at[k, t], device_id=to, device_id_type=MESH)

        def start(cp):
            cp.start()
            sends.append(cp)

        for t, src in enumerate(srcs):
            own = src.at[pl.ds(c * lh, lh)]
            start(copy(0, t, own, half(t, me, c), (*n1, c)))
            start(copy(1, t, own, half(t, me, c), (*n2, c)))
            start(copy(6, t, src, outs[t].at[2 * x + y], sibling))
        for k, chip, j in ((0, n1, c), (1, n2, 1 - c), (2, dg, 2)):
            for t in range(len(srcs)):
                copy(k, t, half(t, chip, c), half(t, chip, c), sibling).wait_recv()
                if k == 0:
                    start(copy(2, t, half(t, n1, c), half(t, n1, c), (*n2, c)))
                start(copy(3 + j, t, half(t, chip, c), half(t, chip, c), sibling))
        for t, src in enumerate(srcs):
            for j, chip in enumerate(_other_chips(x, y)):
                copy(3 + j, t, half(t, chip, 1 - c), half(t, chip, 1 - c), sibling).wait_recv()
            copy(6, t, src, outs[t].at[2 * x + y], sibling).wait_recv()
        for cp in sends:
            cp.wait_send()

    return _comm_call(body, "gather_weights", shards, [(4,) + s.shape for s in shards], 7)


def _pair_exchange(parts):
    lh = parts[0].shape[1] // 2

    def body(srcs, outs, send_sems, recv_sems):
        x, y, c = _place()
        cps = [pltpu.make_async_remote_copy(src_ref=src.at[:, pl.ds((1 - c) * lh, lh)], dst_ref=out, send_sem=send_sems.at[0, t],
                                            recv_sem=recv_sems.at[0, t], device_id=(x, y, 1 - c), device_id_type=MESH)
               for t, (src, out) in enumerate(zip(srcs, outs))]
        for cp in cps:
            cp.start()
        for cp in cps:
            cp.wait()

    return _comm_call(body, "grad_pair_exchange", parts, [(4, lh) + p.shape[2:] for p in parts], 1)


def _chip_exchange(ts):
    def body(srcs, outs, send_sems, recv_sems):
        x, y, c = _place()
        cps = [pltpu.make_async_remote_copy(src_ref=src.at[2 * chip[0] + chip[1]], dst_ref=out.at[k], send_sem=send_sems.at[k, t],
                                            recv_sem=recv_sems.at[k, t], device_id=(*chip, c), device_id_type=MESH)
               for t, (src, out) in enumerate(zip(srcs, outs)) for k, chip in enumerate(_other_chips(x, y))]
        for cp in cps:
            cp.start()
        for cp in cps:
            cp.wait()

    return _comm_call(body, "grad_chip_exchange", ts, [(3,) + t.shape[1:] for t in ts], 3)


def _pair_share(halves):
    def body(srcs, outs, send_sems, recv_sems):
        x, y, c = _place()
        cps = [pltpu.make_async_remote_copy(src_ref=src, dst_ref=out, send_sem=send_sems.at[0, t], recv_sem=recv_sems.at[0, t],
                                            device_id=(x, y, 1 - c), device_id_type=MESH)
               for t, (src, out) in enumerate(zip(srcs, outs))]
        for cp in cps:
            cp.start()
        for cp in cps:
            cp.wait()

    theirs = _comm_call(body, "grad_pair_share", halves, [h.shape for h in halves], 1)
    c = jnp.reshape(lax.axis_index("c"), (1,)).astype(jnp.int32)
    return [_join_halves(mine, other, c, "grad_pair_join_%d" % t) for t, (mine, other) in enumerate(zip(halves, theirs))]


def _rows_view(t, lead):
    return t.reshape(t.shape[:lead] + (-1, t.shape[-1]))


def _join_halves(mine, theirs, c, name):
    shape = (2 * mine.shape[0],) + mine.shape[1:]
    mine, theirs = _rows_view(mine, 0), _rows_view(theirs, 0)
    rh, cols = mine.shape
    tr = _tile(rh, 1024, 8)

    def join(c_ref, mine_ref, theirs_ref, o_ref):
        o_ref[...] = jnp.where(pl.program_id(0) == c_ref[0], mine_ref[...], theirs_ref[...])

    spec = pl.BlockSpec((tr, cols), lambda h, i, c_ref: (i, 0))
    return pl.pallas_call(
        join, name=name,
        grid_spec=pltpu.PrefetchScalarGridSpec(
            num_scalar_prefetch=1, grid=(2, rh // tr), in_specs=[spec, spec],
            out_specs=pl.BlockSpec((tr, cols), lambda h, i, c_ref: (h * (rh // tr) + i, 0))),
        out_shape=jax.ShapeDtypeStruct((2 * rh, cols), mine.dtype),
        compiler_params=_params("parallel", "parallel"),
    )(c, mine, theirs).reshape(shape)


def _gather_all(v):
    r = v.shape[0]

    def body(src, out, send_sems, recv_sems, local_sem):
        x, y, c = _place()
        me = 4 * x + 2 * y + c
        mine = pltpu.make_async_copy(src, out.at[me], local_sem)
        mine.start()
        cps = []
        for k in range(1, 8):
            fx, fy, fc = (k >> 2) & 1, (k >> 1) & 1, k & 1
            peer = (x + fx - 2 * x * fx, y + fy - 2 * y * fy, c + fc - 2 * c * fc)
            cps.append(pltpu.make_async_remote_copy(src_ref=src, dst_ref=out.at[me], send_sem=send_sems.at[k - 1],
                                                    recv_sem=recv_sems.at[k - 1], device_id=peer, device_id_type=MESH))
        for cp in cps:
            cp.start()
        for k in range(1, 8):
            fx, fy, fc = (k >> 2) & 1, (k >> 1) & 1, k & 1
            frm = 4 * (x + fx - 2 * x * fx) + 2 * (y + fy - 2 * y * fy) + (c + fc - 2 * c * fc)
            pltpu.make_async_remote_copy(src_ref=src, dst_ref=out.at[frm], send_sem=send_sems.at[k - 1],
                                         recv_sem=recv_sems.at[k - 1], device_id=(x, y, c), device_id_type=MESH).wait_recv()
        for cp in cps:
            cp.wait_send()
        mine.wait()

    return pl.pallas_call(
        body, name="gather_small_grads", in_specs=[ANY], out_specs=ANY,
        out_shape=jax.ShapeDtypeStruct((8, r, 128), v.dtype),
        scratch_shapes=[pltpu.SemaphoreType.DMA((7,)), pltpu.SemaphoreType.DMA((7,)), pltpu.SemaphoreType.DMA],
    )(v)


def _sum_slots(parts, name):
    n, r, _ = parts.shape
    tr = _tile(r, 1024, 8)

    def body(p_ref, o_ref):
        acc = p_ref[0]
        for j in range(1, n):
            acc = acc + p_ref[j]
        o_ref[...] = acc

    return pl.pallas_call(
        body, name=name, grid=(r // tr,), in_specs=[pl.BlockSpec((n, tr, 128), lambda i: (0, i, 0))],
        out_specs=pl.BlockSpec((tr, 128), lambda i: (i, 0)), out_shape=jax.ShapeDtypeStruct((r, 128), parts.dtype),
        compiler_params=_params("parallel"),
    )(parts)


def _add_sibling_half(part, recv, c, name):
    shape = recv.shape
    part, recv = _rows_view(part, 1), _rows_view(recv, 1)
    _, rh, cols = recv.shape
    tr = _tile(rh, 1024, 16)
    nblk = rh // tr

    def body(c_ref, p_ref, r_ref, o_ref):
        o_ref[...] = (p_ref[...].astype(F32) + r_ref[...].astype(F32)).astype(o_ref.dtype)

    return pl.pallas_call(
        body, name=name,
        grid_spec=pltpu.PrefetchScalarGridSpec(
            num_scalar_prefetch=1, grid=(4, nblk),
            in_specs=[pl.BlockSpec((None, tr, cols), lambda j, i, c_ref: (j, c_ref[0] * nblk + i, 0)),
                      pl.BlockSpec((None, tr, cols), lambda j, i, c_ref: (j, i, 0))],
            out_specs=pl.BlockSpec((None, tr, cols), lambda j, i, c_ref: (j, i, 0))),
        out_shape=jax.ShapeDtypeStruct(recv.shape, recv.dtype),
        compiler_params=_params("parallel", "parallel"),
    )(c, part, recv).reshape(shape)


def _add_chips(t, recv, me, name):
    shape = t.shape[1:]
    t, recv = _rows_view(t, 1), _rows_view(recv, 1)
    _, rh, cols = t.shape
    tr = _tile(rh, 1024, 16)

    def body(me_ref, t_ref, r_ref, o_ref):
        f = lambda v: v.astype(F32)
        o_ref[...] = ((f(t_ref[...]) + f(r_ref[0])) + f(r_ref[1])) + f(r_ref[2])

    return pl.pallas_call(
        body, name=name,
        grid_spec=pltpu.PrefetchScalarGridSpec(
            num_scalar_prefetch=1, grid=(rh // tr,),
            in_specs=[pl.BlockSpec((None, tr, cols), lambda i, me_ref: (me_ref[0], i, 0)),
                      pl.BlockSpec((3, tr, cols), lambda i, me_ref: (0, i, 0))],
            out_specs=pl.BlockSpec((tr, cols), lambda i, me_ref: (i, 0))),
        out_shape=jax.ShapeDtypeStruct((rh, cols), F32),
        compiler_params=_params("parallel"),
    )(me, t, recv).reshape(shape)


def _reduce_scatter(parts):
    x, y, c = _place()
    c1, me = jnp.reshape(c, (1,)).astype(jnp.int32), jnp.reshape(2 * x + y, (1,)).astype(jnp.int32)
    ts = [_add_sibling_half(p, r, c1, "grad_pair_sum_%d" % i) for i, (p, r) in enumerate(zip(parts, _pair_exchange(parts)))]
    halves = [_add_chips(t, r, me, "grad_chip_sum_%d" % i) for i, (t, r) in enumerate(zip(ts, _chip_exchange(ts)))]
    return _pair_share(halves)


def _to_rows(parts, mult):
    flat = jnp.concatenate([p.reshape(-1) for p in parts])
    flat = jnp.pad(flat, (0, (-flat.size) % (128 * mult)))
    return flat.reshape(-1, 128)


def _from_rows(rows, shapes):
    flat, out, at = rows.reshape(-1), [], 0
    for shp in shapes:
        n = int(np.prod(shp))
        out.append(flat[at:at + n].reshape(shp))
        at += n
    return out


def _full_from_shards(g, name):
    _, depth, rows, cols = g.shape
    if name in ROW_SHARDED:
        return jnp.moveaxis(g, 0, 1).reshape(depth, 4 * rows, cols)
    return jnp.moveaxis(g, 0, 2).reshape(depth, rows, 4 * cols)


def _shards_from_full(full, name):
    depth, rows, cols = full.shape
    if name in ROW_SHARDED:
        return jnp.moveaxis(full.reshape(depth, 4, rows // 4, cols), 1, 0)
    return jnp.moveaxis(full.reshape(depth, rows, 4, cols // 4), 2, 0)


def kernel(x, w_in, b_gate, q_norm_b, k_norm_b, rpb_c, w_branch_a, w_branch_b, w_branch_c, w_out, ln1_g, ln1_b, w_up, w_down, ln2_g, ln2_b, loss_target, m_w_in, m_b_gate, m_q_norm_b, m_k_norm_b, m_rpb_c, m_w_branch_a, m_w_branch_b, m_w_branch_c, m_w_out, m_ln1_g, m_ln1_b, m_w_up, m_w_down, m_ln2_g, m_ln2_b, v_w_in, v_b_gate, v_q_norm_b, v_k_norm_b, v_rpb_c, v_w_branch_a, v_w_branch_b, v_w_branch_c, v_w_out, v_ln1_g, v_ln1_b, v_w_up, v_w_down, v_ln2_g, v_ln2_b):
    args = dict(locals())
    big_shard = {n: args[n] for n in BIG}
    small = {n: args[n] for n in SMALL}

    gathered = dict(zip(BIG, _gather_shards([big_shard[n].astype(_MXU) for n in BIG])))

    sq, grad_x, grads = _local_step(x[0], loss_target[0], gathered, small)
    loss = lax.psum(0.5 * jnp.sum(sq) / x.shape[-1], ("x", "y", "c"))

    g_big = _reduce_scatter([grads[n] for n in BIG])
    small_shapes = [small[n].shape for n in SMALL]
    g_small = _from_rows(_sum_slots(_gather_all(_to_rows([grads[n] for n in SMALL], 8)), "small_grad_sum"), small_shapes)
    grad = dict(zip(BIG, g_big))
    grad.update(zip(SMALL, g_small))

    delta, new_m, new_v = {}, {}, {}
    for n in BIG:
        shp = big_shard[n].shape
        two_d = lambda t: t.reshape(-1, shp[-1])
        res = _adamw(two_d(big_shard[n]), two_d(grad[n]), two_d(args["m_" + n]), two_d(args["v_" + n]), "adamw_" + n)
        delta[n], new_m[n], new_v[n] = [t.reshape(shp) for t in res]
    packed = [_to_rows([args[pre + n] for n in SMALL], 8) for pre in ("", "m_", "v_")]
    res = _adamw(packed[0], _to_rows([grad[n] for n in SMALL], 8), packed[1], packed[2], "adamw_small")
    for dst, rows in zip((delta, new_m, new_v), res):
        dst.update(zip(SMALL, _from_rows(rows, small_shapes)))

    order = ("w_in", "b_gate", "q_norm_b", "k_norm_b", "rpb_c", "w_branch_a", "w_branch_b", "w_branch_c", "w_out",
             "ln1_g", "ln1_b", "w_up", "w_down", "ln2_g", "ln2_b")
    return (loss, grad_x[None], *[grad[n] for n in order], *[delta[n] for n in order],
            *[new_m[n] for n in order], *[new_v[n] for n in order])
```

```python
import functools

import numpy as np
import jax
import jax.numpy as jnp
from jax import lax
from jax.experimental import pallas as pl
from jax.experimental.pallas import tpu as pltpu

F32 = jnp.float32
_MXU = jnp.bfloat16

HEAD = 64
A_W, BQ_W, BKV_W, C_W = 256, 512, 128, 256
QKV_W = 768
A_DILATIONS = (1, 4, 16)
A_RADIUS = 64
A_ROPE_HALF = 8
AX_ROPE_HALF = 16
ROPE_THETA = 500000.0
AX_THETA = 10000.0
GRID_W = 64
C_ROWS = 8
C_COLS = 16
BAND = 128
BT_TILES = 18
LN_EPS = 1e-5
RMS_EPS = 1e-6
NEG = -1e30
SCALE = HEAD ** -0.5
LOG2E = 1.4426950408889634
LN2 = 0.6931471805599453
ADAM_LR, ADAM_B1, ADAM_B2, ADAM_EPS, ADAM_WD, ADAM_STEP = 0.001, 0.9, 0.999, 1e-08, 0.01, 10
V7X_VMEM_LIMIT = 48 * 1024 * 1024
MESH = pl.DeviceIdType.MESH
ANY = pl.BlockSpec(memory_space=pl.ANY)


def _params(*sem):
    return pltpu.CompilerParams(dimension_semantics=sem or None, vmem_limit_bytes=V7X_VMEM_LIMIT)


def _tile(n, pref, align=128):
    if n <= pref:
        return n
    t = (pref // align) * align
    while t >= align:
        if n % t == 0:
            return t
        t -= align
    return n


def _mm(a, b, *, name, mode="nn", outs=((F32),), epilogue=None, extras=(), tm=1024, tn=1024, tk=2048, exact=False,
        b_cols=None, b_off=0, out_chips=False):
    b, b_lead, b_axis = (tuple(b) + (None, None))[:3] if isinstance(b, tuple) else (b, None, None)
    m, k = a.shape if mode != "tn" else a.shape[::-1]
    b_rows = b.shape[-2] * (4 if b_axis == "rows" else 1)
    b_last = b_cols or b.shape[-1] * (4 if b_axis == "cols" else 1)
    k2, n = (b_rows, b_last) if mode != "nt" else (b_last, b_rows)
    assert k == k2, (a.shape, b.shape, mode)
    cap_rows = b.shape[-2] if b_axis == "rows" else None
    cap_cols = b.shape[-1] if b_axis == "cols" else (n // 4 if out_chips else None)
    cap_n, cap_k = (cap_cols, cap_rows) if mode != "nt" else (cap_rows, cap_cols)
    tm, tn, tk = _tile(m, tm), _tile(cap_n or n, min(tn, cap_n or tn)), _tile(cap_k or k, min(tk, cap_k or tk))
    nk = k // tk
    n_ex, n_out = len(extras), len(outs)
    mx = F32 if exact else _MXU
    prec = lax.Precision.HIGHEST if exact else None
    dims = {"nn": (((1,), (0,)), ((), ())), "nt": (((1,), (1,)), ((), ())), "tn": (((0,), (0,)), ((), ()))}[mode]

    def body(*refs):
        a_ref, b_ref = refs[0], refs[1]
        ex = refs[2:2 + n_ex]
        out_refs = refs[2 + n_ex:2 + n_ex + n_out]
        kk = pl.program_id(2)
        av, bv = a_ref[...].astype(mx), b_ref[...].astype(mx)
        part = lax.dot_general(av, bv, dims, preferred_element_type=F32, precision=prec)

        def finish(res):
            vals = epilogue(res, *[e[...] for e in ex]) if epilogue is not None else (res,)
            for o, v in zip(out_refs, vals):
                o[...] = v.astype(o.dtype)

        if nk == 1:
            finish(part)
        else:
            acc = refs[-1]

            @pl.when(kk == 0)
            def _():
                acc[...] = part

            @pl.when((kk > 0) & (kk < nk - 1))
            def _():
                acc[...] += part

            @pl.when(kk == nk - 1)
            def _():
                finish(acc[...] + part)

    a_spec = pl.BlockSpec((tm, tk), lambda i, j, kk: (i, kk)) if mode != "tn" else pl.BlockSpec((tk, tm), lambda i, j, kk: (kk, i))
    b_tile = (tn, tk) if mode == "nt" else (tk, tn)

    def b_index(i, j, kk):
        rc = [j, kk + b_off] if mode == "nt" else [kk, j + b_off]
        if b_axis is None:
            return (() if b_lead is None else (b_lead,)) + tuple(rc)
        ax = 0 if b_axis == "rows" else 1
        per = b.shape[-2 + ax] // b_tile[ax]
        chip, rc[ax] = rc[ax] // per, rc[ax] % per
        return (chip, b_lead) + tuple(rc)

    b_spec = pl.BlockSpec((None,) * (b.ndim - 2) + b_tile, b_index)
    o_spec = pl.BlockSpec((tm, tn), lambda i, j, kk: (i, j))
    if out_chips:
        per_out = n // 4 // tn
        out_specs = [pl.BlockSpec((None, tm, tn), lambda i, j, kk: (j // per_out, i, j % per_out))] * n_out
        out_shape = [jax.ShapeDtypeStruct((4, m, n // 4), d) for d in outs]
    else:
        out_specs, out_shape = [o_spec] * n_out, [jax.ShapeDtypeStruct((m, n), d) for d in outs]
    res = pl.pallas_call(
        body, name=name, grid=(m // tm, n // tn, nk),
        in_specs=[a_spec, b_spec] + [o_spec if e.shape[0] > 1 else pl.BlockSpec((1, tn), lambda i, j, kk: (0, j)) for e in extras],
        out_specs=out_specs, out_shape=out_shape,
        scratch_shapes=[pltpu.VMEM((tm, tn), F32)] if nk > 1 else [],
        compiler_params=_params("parallel", "parallel", "arbitrary"),
    )(a, b, *extras)
    return res[0] if n_out == 1 else res


def _rows(tm, width, cb=0):
    return pl.BlockSpec((tm, width), lambda t: (t, cb))


def _whole(arr):
    nd = arr.ndim
    return pl.BlockSpec(arr.shape, lambda t: (0,) * nd)


def _rowwise(fn, name, rows, tm, ins, outs):
    n_in, n_out = len(ins), len(outs)
    dil_in = [spec[1:] if isinstance(spec, tuple) else None for _, spec in ins]
    in_specs = [_rows(tm // spec[1], spec[1] * spec[2]) if isinstance(spec, tuple) else spec for _, spec in ins]
    scratch = [pltpu.VMEM((di[1] // 128, tm, 128), F32) for di in dil_in if di] + \
              [pltpu.VMEM((n // 128, tm, 128), F32) for n, _, kind in outs if isinstance(kind, int)]

    def body(*refs):
        scr = list(refs[n_in + n_out:])
        blocks = []
        for r, di in zip(refs[:n_in], dil_in):
            if di is None:
                blocks.append(r[...])
            else:
                d, n = di
                s_ref = scr.pop(0)
                for j in range(d):
                    for b in range(n // 128):
                        lanes = slice(j * n + b * 128, j * n + (b + 1) * 128)
                        s_ref.at[b][pl.ds(j, tm // d, stride=d), :] = r[:, lanes].astype(F32)
                blocks.append(jnp.concatenate([s_ref[b] for b in range(n // 128)], 1))
        vals = fn(*blocks)
        first = pl.program_id(0) == 0
        for (ncols, _, kind), o, v in zip(outs, refs[n_in:n_in + n_out], vals):
            if kind == "row":
                o[...] = v.astype(o.dtype)
            elif isinstance(kind, int):
                s_ref = scr.pop(0)
                for b in range(ncols // 128):
                    s_ref[b] = v[:, b * 128:(b + 1) * 128].astype(F32)
                for j in range(kind):
                    for b in range(ncols // 128):
                        lanes = slice(j * ncols + b * 128, j * ncols + (b + 1) * 128)
                        o[:, lanes] = s_ref.at[b][pl.ds(j, tm // kind, stride=kind), :].astype(o.dtype)
            else:
                part = v.reshape(tm // 8, 8, ncols).sum(0)

                @pl.when(first)
                def _(o=o, part=part):
                    o[...] = part

                @pl.when(jnp.logical_not(first))
                def _(o=o, part=part):
                    o[...] += part

    def out_spec(n, kind):
        if kind == "row":
            return _rows(tm, n), (rows, n)
        if isinstance(kind, int):
            return _rows(tm // kind, kind * n), (rows // kind, kind * n)
        return pl.BlockSpec((8, n), lambda t: (0, 0)), (8, n)

    specs = [out_spec(n, kind) for n, _, kind in outs]
    res = pl.pallas_call(
        body, name=name, grid=(rows // tm,),
        in_specs=in_specs, out_specs=[s for s, _ in specs],
        out_shape=[jax.ShapeDtypeStruct(shp, d) for (_, shp), (_, d, _) in zip(specs, outs)],
        scratch_shapes=scratch, compiler_params=_params("arbitrary"),
    )(*[a for a, _ in ins])
    return res


def _lane_lo(width=128):
    return (lax.broadcasted_iota(jnp.int32, (1, width), 1) & (HEAD * 2 - 1)) < HEAD


def _group_sum(x):
    w = x.shape[-1]
    sh = HEAD.bit_length() - 1
    same = (lax.broadcasted_iota(jnp.int32, (w, w), 0) >> sh) == (lax.broadcasted_iota(jnp.int32, (w, w), 1) >> sh)
    ones = jnp.where(same, 1.0, 0.0).astype(jnp.bfloat16)
    hi = x.astype(jnp.bfloat16)
    lo = (x - hi.astype(F32)).astype(jnp.bfloat16)
    return jnp.dot(hi, ones, preferred_element_type=F32) + jnp.dot(lo, ones, preferred_element_type=F32)


def _rot(x, c, sm, sp, shift):
    w = x.shape[-1]
    return x * c + pltpu.roll(x, w - shift, 1) * sm + pltpu.roll(x, shift, 1) * sp


def _rot_t(dy, c, sm, sp, shift):
    w = dy.shape[-1]
    return dy * c + pltpu.roll(dy * sm, shift, 1) + pltpu.roll(dy * sp, w - shift, 1)


def _rope_tables(pos_parts, half, thetas):
    cs, sms, sps = [], [], []
    for pos, theta in zip(pos_parts, thetas):
        inv = theta ** (-jnp.arange(half, dtype=F32) / half)
        ang = pos.astype(F32)[:, None] * inv[None, :]
        co, si, ze = jnp.cos(ang), jnp.sin(ang), jnp.zeros_like(ang)
        cs += [co, co]
        sms += [-si, ze]
        sps += [ze, si]
    return [jnp.concatenate(t, axis=1) for t in (cs, sms, sps)]


def _tables(s):
    pos = jnp.arange(s)
    ca, sma, spa = _rope_tables([pos], A_ROPE_HALF, [ROPE_THETA])
    pad = HEAD - 2 * A_ROPE_HALF
    ca = jnp.concatenate([ca, jnp.ones((s, pad), F32)], 1)
    sma, spa = [jnp.concatenate([t, jnp.zeros((s, pad), F32)], 1) for t in (sma, spa)]
    tab_a = [jnp.tile(t, (1, A_W // HEAD)) for t in (ca, sma, spa)]
    ax = _rope_tables([pos // GRID_W, pos % GRID_W], AX_ROPE_HALF, [AX_THETA, AX_THETA])
    tab_q = [jnp.tile(t, (1, BQ_W // HEAD)) for t in ax]
    tab_k = [jnp.tile(t, (1, BKV_W // HEAD)) for t in ax]
    return tab_a, tab_q, tab_k


def _prep_a(ha, tab, tm):
    s = ha.shape[0]

    def fn(h, c, sm, sp):
        q, k, v = h[:, :A_W], h[:, A_W:2 * A_W], h[:, 2 * A_W:]
        q, k = _rot(q, c, sm, sp, A_ROPE_HALF) * SCALE, _rot(k, c, sm, sp, A_ROPE_HALF)
        return [t for t in (q, k, v) for _ in A_DILATIONS]

    res = _rowwise(fn, "prep_a", s, tm, [(ha, _rows(tm, QKV_W))] + [(t, _rows(tm, A_W)) for t in tab],
                   [(A_W, _MXU, _dil_kind(d)) for _ in range(3) for d in A_DILATIONS])
    n = len(A_DILATIONS)
    return [dict(zip(A_DILATIONS, res[i * n:(i + 1) * n])) for i in range(3)]


def _dil_kind(d):
    return "row" if d == 1 else d


def _dil_spec(d, tm, ncols):
    return _rows(tm, ncols) if d == 1 else ("dil", d, ncols)


def _to_dilations(x, name, tm):
    s, n = x.shape
    res = _rowwise(lambda v: [v for d in A_DILATIONS if d > 1], name, s, tm, [(x, _rows(tm, n))],
                   [(n, x.dtype, d) for d in A_DILATIONS if d > 1])
    return {1: x, **dict(zip([d for d in A_DILATIONS if d > 1], res))}


def _rms(x, g):
    ms = _group_sum(x * x) * (1.0 / HEAD)
    return x * lax.rsqrt(ms + RMS_EPS) * g


def _prep_b(hb, gq, gk, tab_q, tab_k, tm):
    s = hb.shape[0]

    def fn(h, gq, gk, cq, smq, spq, ck, smk, spk):
        xq, xk, v = h[:, :BQ_W], h[:, BQ_W:BQ_W + BKV_W], h[:, BQ_W + BKV_W:]
        q = _rot(_rms(xq, gq), cq, smq, spq, AX_ROPE_HALF) * (SCALE * LOG2E)
        k = _rot(_rms(xk, gk), ck, smk, spk, AX_ROPE_HALF)
        lo = _lane_lo()
        kr, vr = pltpu.roll(k, HEAD, 1), pltpu.roll(v, HEAD, 1)
        kd = jnp.concatenate([jnp.where(lo, k, kr), jnp.where(lo, kr, k)], 1)
        vd = jnp.concatenate([jnp.where(lo, v, vr), jnp.where(lo, vr, v)], 1)
        v1 = jnp.concatenate([jnp.where(lo, v, 1.0), jnp.where(lo, vr, 1.0)], 1)
        return q, kd, vd, v1

    ins = [(hb, _rows(tm, QKV_W)), (gq, _whole(gq)), (gk, _whole(gk))]
    ins += [(t, _rows(tm, BQ_W)) for t in tab_q] + [(t, _rows(tm, BKV_W)) for t in tab_k]
    return _rowwise(fn, "prep_b", s, tm, ins, [(BQ_W, _MXU, "row")] + [(2 * BKV_W, _MXU, "row")] * 3)


def _prep_c(hc, tm):
    def fn(h):
        return h[:, :C_W] * SCALE, h[:, C_W:2 * C_W], h[:, 2 * C_W:]

    return _rowwise(fn, "prep_c", hc.shape[0], tm, [(hc, _rows(tm, QKV_W))], [(C_W, _MXU, "row")] * 3)


def _combine_a(os_, ms, ls, s, tm):
    def fn(o1, o2, o3, m1, m2, m3, l1, l2, l3):
        lo = _lane_lo()
        outs, lses = [], []
        for p in range(A_W // 128):
            st = slice(p * 256, (p + 1) * 256)
            mm = [m[:, st] for m in (m1, m2, m3)]
            ll = [l[:, st] for l in (l1, l2, l3)]
            mmax = jnp.maximum(jnp.maximum(mm[0], mm[1]), mm[2])
            ws = [jnp.exp(m - mmax) for m in mm]
            den = ws[0] * ll[0] + ws[1] * ll[1] + ws[2] * ll[2]
            lses.append(mmax + jnp.log(den))
            num = sum(jnp.where(lo, w[:, :128], w[:, 128:]) * o[:, p * 128:(p + 1) * 128] for w, o in zip(ws, (o1, o2, o3)))
            outs.append(num / jnp.where(lo, den[:, :128], den[:, 128:]))
        o, lse = jnp.concatenate(outs, 1), jnp.concatenate(lses, 1)
        return [o] * len(A_DILATIONS) + [lse] * len(A_DILATIONS)

    ins = [(t, _dil_spec(d, tm, w)) for ts, w in ((os_, A_W), (ms, 2 * A_W), (ls, 2 * A_W)) for t, d in zip(ts, A_DILATIONS)]
    res = _rowwise(fn, "combine_a", s, tm, ins,
                   [(w, F32, _dil_kind(d)) for w in (A_W, 2 * A_W) for d in A_DILATIONS])
    n = len(A_DILATIONS)
    return dict(zip(A_DILATIONS, res[:n])), dict(zip(A_DILATIONS, res[n:]))


def _gates(hg, bg, d):
    return [jax.nn.sigmoid(hg[:, i * d:(i + 1) * d] + bg[:, i * d:(i + 1) * d]) for i in range(3)]


def _gate_merge(hg, bg, pa, pb, pc, tm):
    s, d = pa.shape

    def fn(hg, bg, pa, pb, pc):
        g = _gates(hg, bg, d)
        return (g[0] * pa + g[1] * pb + g[2] * pc,)

    ins = [(hg, _rows(tm, 3 * d)), (bg, _whole(bg))] + [(p, _rows(tm, d)) for p in (pa, pb, pc)]
    return _rowwise(fn, "gate_merge", s, tm, ins, [(d, _MXU, "row")])[0]


def _gate_bwd(dm, hg, bg, pa, pb, pc, tm):
    s, d = pa.shape

    def fn(dm, hg, bg, pa, pb, pc):
        g = _gates(hg, bg, d)
        dlog = jnp.concatenate([dm * p * gi * (1.0 - gi) for p, gi in zip((pa, pb, pc), g)], 1)
        return dm * g[0], dm * g[1], dm * g[2], dlog, dlog

    ins = [(dm, _rows(tm, d)), (hg, _rows(tm, 3 * d)), (bg, _whole(bg))] + [(p, _rows(tm, d)) for p in (pa, pb, pc)]
    return _rowwise(fn, "gate_bwd", s, tm, ins, [(d, _MXU, "row")] * 3 + [(3 * d, _MXU, "row"), (3 * d, F32, "acc")])


def _ln_stats(r):
    mu = jnp.mean(r, -1, keepdims=True)
    xc = r - mu
    var = jnp.mean(xc * xc, -1, keepdims=True)
    rstd = lax.rsqrt(var + LN_EPS)
    return xc * rstd, rstd


def _ln_epilogue(alpha):
    def fn(br, x, g, b):
        r = alpha * x + br
        xhat, _ = _ln_stats(r)
        y = xhat * g + b
        return r, y, y

    return fn


def _ln_bwd(dy, r, g, name, tm):
    s, d = r.shape

    def fn(dy, r, g):
        xhat, rstd = _ln_stats(r)
        dxh = dy * g
        dr = rstd * (dxh - jnp.mean(dxh, -1, keepdims=True) - xhat * jnp.mean(dxh * xhat, -1, keepdims=True))
        return dr, dr, dy * xhat, dy

    ins = [(dy, _rows(tm, d)), (r, _rows(tm, d)), (g, _whole(g))]
    return _rowwise(fn, name, s, tm, ins, [(d, F32, "row"), (d, _MXU, "row"), (d, F32, "acc"), (d, F32, "acc")])


def _loss_head(y, target, tm):
    s, d = y.shape

    def fn(y, t):
        diff = y - t
        return diff * diff, diff * (1.0 / d)

    sq, dy = _rowwise(fn, "loss_head", s, tm, [(y, _rows(tm, d)), (target, _rows(tm, d))], [(d, F32, "acc"), (d, F32, "row")])
    return sq, dy


def _post_a(dqs, dks, dvs, tab, s, tm):
    def fn(q1, q2, q3, k1, k2, k3, v1, v2, v3, c, sm, sp):
        dq = _rot_t((q1 + q2 + q3) * SCALE, c, sm, sp, A_ROPE_HALF)
        dk = _rot_t(k1 + k2 + k3, c, sm, sp, A_ROPE_HALF)
        return (jnp.concatenate([dq, dk, v1 + v2 + v3], 1),)

    ins = [(t, _dil_spec(d, tm, A_W)) for ts in (dqs, dks, dvs) for t, d in zip(ts, A_DILATIONS)]
    ins += [(t, _rows(tm, A_W)) for t in tab]
    return _rowwise(fn, "post_a", s, tm, ins, [(QKV_W, _MXU, "row")])[0]


def _post_b(dq, dkd, dvd, hb, gq, gk, tab_q, tab_k, tm):
    s = dq.shape[0]

    def back(dz, x, g, c, sm, sp):
        dy = _rot_t(dz, c, sm, sp, AX_ROPE_HALF)
        rstd = lax.rsqrt(_group_sum(x * x) * (1.0 / HEAD) + RMS_EPS)
        xh = x * rstd
        dxh = dy * g
        return rstd * (dxh - xh * (_group_sum(dxh * xh) * (1.0 / HEAD))), dy * xh

    def fn(dq, dkd, dvd, h, gq, gk, cq, smq, spq, ck, smk, spk):
        lo = _lane_lo()
        dk = jnp.where(lo, dkd[:, :128], dkd[:, 128:])
        dv = jnp.where(lo, dvd[:, :128], dvd[:, 128:])
        dxq, dgq = back(dq * SCALE, h[:, :BQ_W], gq, cq, smq, spq)
        dxk, dgk = back(dk, h[:, BQ_W:BQ_W + BKV_W], gk, ck, smk, spk)
        return jnp.concatenate([dxq, dxk, dv], 1), dgq, dgk

    ins = [(dq, _rows(tm, BQ_W)), (dkd, _rows(tm, 2 * BKV_W)), (dvd, _rows(tm, 2 * BKV_W)), (hb, _rows(tm, QKV_W)),
           (gq, _whole(gq)), (gk, _whole(gk))]
    ins += [(t, _rows(tm, BQ_W)) for t in tab_q] + [(t, _rows(tm, BKV_W)) for t in tab_k]
    return _rowwise(fn, "post_b", s, tm, ins, [(QKV_W, _MXU, "row"), (BQ_W, F32, "acc"), (BKV_W, F32, "acc")])


def _post_c(dq, dk, dv, tm):
    def fn(dq, dk, dv):
        return (jnp.concatenate([dq * SCALE, dk, dv], 1),)

    return _rowwise(fn, "post_c", dq.shape[0], tm, [(t, _rows(tm, C_W)) for t in (dq, dk, dv)], [(QKV_W, _MXU, "row")])[0]


def _adamw(w, g, m, v, name):
    rows, cols = w.shape
    tm = _tile(rows, 256, 8)

    def fn(w, g, m, v):
        m = ADAM_B1 * m + (1.0 - ADAM_B1) * g
        v = ADAM_B2 * v + (1.0 - ADAM_B2) * (g * g)
        m_hat = m / (1.0 - ADAM_B1 ** ADAM_STEP)
        v_hat = v / (1.0 - ADAM_B2 ** ADAM_STEP)
        delta = -ADAM_LR * (m_hat / (jnp.sqrt(v_hat) + ADAM_EPS) + ADAM_WD * w)
        return delta, m, v

    return _rowwise(fn, name, rows, tm, [(t, _rows(tm, cols)) for t in (w, g, m, v)], [(cols, F32, "row")] * 3)


def _dot_t(a, b):
    return lax.dot_general(a, b, (((1,), (1,)), ((), ())), preferred_element_type=F32)


def _tdot(a, b):
    return lax.dot_general(a, b, (((0,), (0,)), ((), ())), preferred_element_type=F32)


def _head_masks():
    lo = _lane_lo()
    return lo, (lo, jnp.logical_not(lo))


def _rep(x, rows):
    return jnp.broadcast_to(x, (rows, 128))


def _row_lo():
    return lax.broadcasted_iota(jnp.int32, (128, 1), 0) < HEAD


def _flash_fwd(q, kd, v1, tq, tk, ride=None):
    s = q.shape[0]
    tq, tk = _tile(s, tq), _tile(s, tk)
    nq, nk = s // tq, s // tk
    mx = _MXU
    n_ride = len(ride[0]) if ride else 0
    steps = BQ_W // 128 * nq * nk

    def body(*refs):
        q_ref, k_ref, v_ref = refs[:3]
        o_ref, lse_ref = refs[3 + n_ride:5 + n_ride]
        m_ref, acc_ref = refs[5 + 2 * n_ride:7 + 2 * n_ride]
        kk = pl.program_id(2)
        if ride:
            comm = (refs[3:3 + n_ride], refs[5 + n_ride:5 + 2 * n_ride], refs[-2], refs[-1])
            step = (pl.program_id(0) * nq + pl.program_id(1)) * nk + kk
            for at, phase in zip((0, steps // 3), ride[3][:2]):
                pl.when(step == at)(functools.partial(phase, *comm))

        @pl.when(kk == 0)
        def _():
            m_ref[...] = jnp.full_like(m_ref, NEG)
            acc_ref[...] = jnp.zeros_like(acc_ref)

        q2, k2, v2 = q_ref[...], k_ref[...], v_ref[...]
        _, masks = _head_masks()
        hs = range(2)
        st = [_dot_t(k2, jnp.where(masks[h], q2, jnp.zeros_like(q2))) for h in hs]
        m_prev = [m_ref[h] for h in hs]
        m_new = [jnp.maximum(m_prev[h], jnp.max(st[h], 0, keepdims=True)) for h in hs]
        p = [jnp.exp2(st[h] - m_new[h]).astype(mx) for h in hs]
        pv = [_tdot(v2, p[h]) for h in hs]
        for h in hs:
            m_ref[h] = m_new[h]
            acc_ref[h] = acc_ref[h] * jnp.exp2(m_prev[h] - m_new[h]) + pv[h]

        @pl.when(kk == nk - 1)
        def _():
            a0, a1 = acc_ref[0], acc_ref[1]
            l0, l1 = a0[HEAD:HEAD + 1], a1[HEAD:HEAD + 1]
            o_ref[...] = jnp.concatenate([a0[:HEAD] / l0, a1[:HEAD] / l1], 0).T
            lse_ref[...] = jnp.concatenate([m_ref[0] + jnp.log2(l0), m_ref[1] + jnp.log2(l1), jnp.zeros((6, tq), F32)], 0)

        if ride:
            pl.when(step == steps - 1)(functools.partial(ride[3][2], *comm))

    ride_in = list(ride[0]) if ride else []
    ride_out = [jax.ShapeDtypeStruct(shp, t.dtype) for shp, t in zip(ride[1], ride[0])] if ride else []
    ride_sems = [pltpu.SemaphoreType.DMA((ride[2], n_ride))] * 2 if ride else []
    res = pl.pallas_call(
        body, name="attn_b_fwd_gather" if ride else "attn_b_fwd", grid=(BQ_W // 128, nq, nk),
        in_specs=[pl.BlockSpec((tq, 128), lambda j, i, kk: (i, j)),
                  pl.BlockSpec((tk, 128), lambda j, i, kk: (kk, j // 2)),
                  pl.BlockSpec((tk, 128), lambda j, i, kk: (kk, j // 2))] + [ANY] * n_ride,
        out_specs=[pl.BlockSpec((tq, 128), lambda j, i, kk: (i, j)), pl.BlockSpec((None, 8, tq), lambda j, i, kk: (j, 0, i))]
        + [ANY] * n_ride,
        out_shape=[jax.ShapeDtypeStruct((s, BQ_W), F32), jax.ShapeDtypeStruct((BQ_W // 128, 8, s), F32)] + ride_out,
        scratch_shapes=[pltpu.VMEM((2, 1, tq), F32), pltpu.VMEM((2, 128, tq), F32)] + ride_sems,
        compiler_params=_params("arbitrary", "arbitrary", "arbitrary") if ride else _params("parallel", "parallel", "arbitrary"),
    )(q, kd, v1, *ride_in)
    return res[0], res[1], res[2:]


def _delta_b(do, o, tq):
    s = do.shape[0]
    tq = _tile(s, tq)

    def body(do_ref, o_ref, d_ref):
        prod = do_ref[...] * o_ref[...]
        row = lax.broadcasted_iota(jnp.int32, (8, 128), 0)
        lane = lax.broadcasted_iota(jnp.int32, (8, 128), 1)
        sel = jnp.where(((row == 0) & (lane < HEAD)) | ((row == 1) & (lane >= HEAD)), 1.0, 0.0).astype(F32)
        d_ref[...] = lax.dot_general(sel, prod, (((1,), (1,)), ((), ())), preferred_element_type=F32,
                                     precision=lax.Precision.HIGHEST)

    qs = pl.BlockSpec((tq, 128), lambda j, i: (i, j))
    return pl.pallas_call(
        body, name="attn_b_delta", grid=(BQ_W // 128, s // tq), in_specs=[qs, qs],
        out_specs=pl.BlockSpec((None, 8, tq), lambda j, i: (j, 0, i)),
        out_shape=jax.ShapeDtypeStruct((BQ_W // 128, 8, s), F32),
        compiler_params=_params("parallel", "parallel"),
    )(do, o)


def _flash_bwd(q, kd, vd, do, lse, delta, tq, tk):
    s = q.shape[0]
    tq, tk = _tile(s, tq), _tile(s, tk)
    nq, nk = s // tq, s // tk
    group = BQ_W // 128 // 2
    mx = _MXU

    def body(k_ref, v_ref, q_ref, do_ref, lse_ref, dl_ref, dq_hbm, dk_ref, dv_ref, dk_acc, dv_acc, dqt, stage, sem):
        e, kk, jj, i = pl.program_id(0), pl.program_id(1), pl.program_id(2), pl.program_id(3)

        @pl.when((jj == 0) & (i == 0))
        def _():
            dk_acc[...] = jnp.zeros_like(dk_acc)
            dv_acc[...] = jnp.zeros_like(dv_acc)

        @pl.when(kk == 0)
        def _():
            dqt[jj, i] = jnp.zeros((128, tq), F32)

        q2, k2, v2, do2 = q_ref[...], k_ref[...], v_ref[...], do_ref[...].astype(mx)
        lse8, dl8 = lse_ref[...], dl_ref[...]
        _, masks = _head_masks()
        hs = range(2)
        qh = [jnp.where(masks[h], q2, jnp.zeros_like(q2)) for h in hs]
        doh = [jnp.where(masks[h], do2, jnp.zeros_like(do2)) for h in hs]
        st = [_dot_t(k2, qh[h]) for h in hs]
        dpt = [_dot_t(v2, doh[h]) for h in hs]
        p = [jnp.exp2(st[h] - lse8[h:h + 1]) for h in hs]
        ds = [(p[h] * (dpt[h] - dl8[h:h + 1])).astype(mx) for h in hs]
        p = [p[h].astype(mx) for h in hs]
        dv_acc[...] += jnp.dot(p[0], doh[0], preferred_element_type=F32) + jnp.dot(p[1], doh[1], preferred_element_type=F32)
        dk_acc[...] += jnp.dot(ds[0], qh[0], preferred_element_type=F32) + jnp.dot(ds[1], qh[1], preferred_element_type=F32)
        dqt[jj, i] += jnp.where(_row_lo(), _tdot(k2, ds[0]), _tdot(k2, ds[1]))

        @pl.when(kk == nk - 1)
        def _():
            stage[...] = dqt[jj, i].T
            lane0 = pl.multiple_of((group * e + jj) * 128, 128)
            cp = pltpu.make_async_copy(stage, dq_hbm.at[pl.ds(pl.multiple_of(i * tq, tq), tq), pl.ds(lane0, 128)], sem)
            cp.start()
            cp.wait()

        @pl.when((jj == group - 1) & (i == nq - 1))
        def _():
            dk_ref[...] = (dk_acc[...] + pltpu.roll(dk_acc[...], HEAD, 1)) * LN2
            dv_ref[...] = dv_acc[...] + pltpu.roll(dv_acc[...], HEAD, 1)

    ks = pl.BlockSpec((tk, 128), lambda e, kk, jj, i: (kk, e))
    qs = pl.BlockSpec((tq, 128), lambda e, kk, jj, i: (i, group * e + jj))
    st = pl.BlockSpec((None, 8, tq), lambda e, kk, jj, i: (group * e + jj, 0, i))
    return pl.pallas_call(
        body, name="attn_b_bwd", grid=(BKV_W // HEAD, nk, group, nq),
        in_specs=[ks, ks, qs, qs, st, st], out_specs=[ANY, ks, ks],
        out_shape=[jax.ShapeDtypeStruct((s, BQ_W), F32)] + [jax.ShapeDtypeStruct((s, 2 * BKV_W), F32)] * 2,
        scratch_shapes=[pltpu.VMEM((tk, 128), F32)] * 2 + [pltpu.VMEM((group, nq, 128, tq), F32), pltpu.VMEM((tq, 128), F32),
                                                          pltpu.SemaphoreType.DMA],
        compiler_params=_params("parallel", "arbitrary", "arbitrary", "arbitrary"),
    )(kd, vd, q, do, lse, delta)


def _p_and_ds(items, masks):
    mx = _MXU
    keys = [(n, h) for n in range(len(items)) for h in range(2)]
    qh = {(n, h): jnp.where(masks[h], items[n][0], jnp.zeros_like(items[n][0])) for n, h in keys}
    doh = {(n, h): jnp.where(masks[h], items[n][3], jnp.zeros_like(items[n][3])) for n, h in keys}
    sc = {}
    for n, h in keys:
        s_h = _dot_t(qh[n, h], items[n][1])
        if items[n][7] is not None:
            s_h = s_h + items[n][7][h]
        sc[n, h] = jnp.where(items[n][6], s_h, NEG)
    dp = {(n, h): _dot_t(doh[n, h].astype(mx), items[n][2]) for n, h in keys}
    lse = {(n, h): jnp.max(items[n][5][:, h * 128:(h + 1) * 128], -1, keepdims=True) for n, h in keys}
    delta = {(n, h): jnp.sum(doh[n, h] * items[n][4], -1, keepdims=True) for n, h in keys}
    p = {key: jnp.exp(sc[key] - lse[key]) for key in keys}
    ds = {key: p[key] * (dp[key] - delta[key]) for key in keys}
    return [[(qh[n, h], p[n, h], ds[n, h], doh[n, h]) for h in range(2)] for n in range(len(items))]


class _BandA:
    hb, has_bias, name = 1, False, "a"

    def __init__(self, nb):
        self.nb = nb

    def mask(self, qidx, kidx):
        n = self.nb * BAND
        return (jnp.abs(qidx - kidx) <= A_RADIUS) & (kidx >= 0) & (kidx < n) & (qidx >= 0) & (qidx < n)


class _BandC:
    hb, has_bias, name = 3, True, "c"

    def __init__(self, nb):
        self.nb = nb
        self.rows = nb * BAND // GRID_W
        per = BAND // GRID_W
        assert self.rows >= C_ROWS and (C_ROWS - 1) // per <= self.hb
        assert (self.rows - 1) // per - (self.rows - C_ROWS) // per <= self.hb

    def mask(self, qidx, kidx):
        n = self.nb * BAND
        sh = GRID_W.bit_length() - 1
        qrow, cq = qidx >> sh, qidx & (GRID_W - 1)
        krow, ck = kidx >> sh, kidx & (GRID_W - 1)
        r0 = jnp.clip(qrow - C_ROWS // 2, 0, self.rows - C_ROWS)
        c0 = jnp.clip(cq - C_COLS // 2, 0, GRID_W - C_COLS)
        ok = (qidx >= 0) & (qidx < n) & (kidx >= 0) & (kidx < n)
        return ok & (krow >= r0) & (krow < r0 + C_ROWS) & (ck >= c0) & (ck < c0 + C_COLS)


def _bias_tile(off, a):
    return (BAND // GRID_W) * off - a + (C_ROWS - 1) + 2


def _band_bias_k(band, bt_ref, h):
    per = BAND // GRID_W
    return jnp.concatenate([jnp.concatenate([bt_ref[h, _bias_tile(off, a)] for off in range(-band.hb, band.hb + 1)], 1)
                            for a in range(per)], 0)


def _band_bias_q(band, bt_ref, h):
    per = BAND // GRID_W
    return jnp.concatenate([bt_ref[h, _bias_tile(-off, a)] for off in range(-band.hb, band.hb + 1) for a in range(per)], 0)


def _band_split(nb, ncb):
    cb = max(c for c in (4, 2, 1) if ncb % c == 0)
    rb = max(r for r in (4, 2, 1) if nb % r == 0 and r * cb <= 16)
    return rb, cb


def _band_specs(band, rb, cb, nb, width):
    def edge(first):
        return pl.BlockSpec((BAND, cb * width), lambda c, i: (jnp.clip(i * rb + first, 0, nb - 1), c))

    main = pl.BlockSpec((rb * BAND, cb * width), lambda c, i: (i, c))
    return [edge(t - band.hb) for t in range(band.hb)] + [main] + [edge(rb + t) for t in range(band.hb)]


def _band_rows(band, refs, rb, r, lanes):
    hb = band.hb
    parts = []
    for b in range(r, r + 2 * hb + 1):
        if b < hb:
            parts.append(refs[b][:, lanes])
        elif b < hb + rb:
            parts.append(refs[hb][(b - hb) * BAND:(b - hb + 1) * BAND, lanes])
        else:
            parts.append(refs[b - rb + 1][:, lanes])
    return jnp.concatenate(parts, 0)


def _band_idx(band, blk, rows_of_blocks, axis):
    shape = (rows_of_blocks * BAND, 1) if axis == 0 else (1, rows_of_blocks * BAND)
    return blk * BAND + lax.broadcasted_iota(jnp.int32, shape, axis)


def _band_fwd(band, q, k, v, bt=None):
    n, w = q.shape
    nb, ncb, nband = n // BAND, w // 128, 2 * band.hb + 1
    rb, cb = _band_split(nb, ncb)
    mx = _MXU
    raw = not band.has_bias

    def body(*refs):
        q_ref, k_refs, v_refs = refs[0], refs[1:1 + nband], refs[1 + nband:1 + 2 * nband]
        rest = refs[1 + 2 * nband:]
        bt_ref = rest[0] if band.has_bias else None
        outs = rest[1:] if band.has_bias else rest
        i = pl.program_id(1)
        lo, masks = _head_masks()
        subs = [(r, c) for r in range(rb) for c in range(cb)]
        mask = {r: band.mask(_band_idx(band, i * rb + r, 1, 0), _band_idx(band, i * rb + r - band.hb, nband, 1)) for r in range(rb)}
        lanes = {c: slice(c * 128, (c + 1) * 128) for c in range(cb)}
        rows = {r: slice(r * BAND, (r + 1) * BAND) for r in range(rb)}
        sc = {}
        for r, c in subs:
            q2, kcat = q_ref[rows[r], lanes[c]], _band_rows(band, k_refs, rb, r, lanes[c])
            for h in range(2):
                s_h = _dot_t(jnp.where(masks[h], q2, jnp.zeros_like(q2)), kcat)
                if band.has_bias:
                    s_h = s_h + _band_bias_k(band, bt_ref, 2 * c + h)
                sc[r, c, h] = jnp.where(mask[r], s_h, NEG)
        ms = {key: jnp.max(s_h, -1, keepdims=True) for key, s_h in sc.items()}
        ps = {key: jnp.exp(s_h - ms[key]) for key, s_h in sc.items()}
        ls = {key: jnp.sum(p, -1, keepdims=True) for key, p in ps.items()}
        os_ = {}
        for r, c in subs:
            vcat = _band_rows(band, v_refs, rb, r, lanes[c])
            for h in range(2):
                os_[r, c, h] = jnp.dot(ps[r, c, h].astype(mx), vcat, preferred_element_type=F32)
        for r, c in subs:
            st_lanes = [slice(c * 256 + h * 128, c * 256 + (h + 1) * 128) for h in range(2)]
            if raw:
                o_ref, m_ref, l_ref = outs
                o_ref[rows[r], lanes[c]] = jnp.where(lo, os_[r, c, 0], os_[r, c, 1])
                for h in range(2):
                    m_ref[rows[r], st_lanes[h]] = _rep(ms[r, c, h], BAND)
                    l_ref[rows[r], st_lanes[h]] = _rep(ls[r, c, h], BAND)
            else:
                o_ref, lse_ref = outs
                o_ref[rows[r], lanes[c]] = jnp.where(lo, os_[r, c, 0] / ls[r, c, 0], os_[r, c, 1] / ls[r, c, 1])
                for h in range(2):
                    lse_ref[rows[r], st_lanes[h]] = _rep(ms[r, c, h] + jnp.log(ls[r, c, h]), BAND)

    qs = pl.BlockSpec((rb * BAND, cb * 128), lambda c, i: (i, c))
    ks = _band_specs(band, rb, cb, nb, 128)
    st = pl.BlockSpec((rb * BAND, cb * 256), lambda c, i: (i, c))
    in_specs, args = [qs] + ks + ks, [q] + [k] * nband + [v] * nband
    if band.has_bias:
        in_specs.append(pl.BlockSpec((2 * cb, BT_TILES, GRID_W, 128), lambda c, i: (c, 0, 0, 0)))
        args.append(bt)
    n_stats = 2 if raw else 1
    return pl.pallas_call(
        body, name="attn_%s_fwd" % band.name, grid=(ncb // cb, nb // rb), in_specs=in_specs,
        out_specs=[qs] + [st] * n_stats,
        out_shape=[jax.ShapeDtypeStruct((n, w), F32)] + [jax.ShapeDtypeStruct((n, 2 * w), F32)] * n_stats,
        compiler_params=_params("parallel", "arbitrary"),
    )(*args)


def _band_dq(band, q, k, v, do, o, lse, bt=None):
    n, w = q.shape
    nb, ncb, nband = n // BAND, w // 128, 2 * band.hb + 1
    rb, cb = _band_split(nb, ncb)
    mx = _MXU
    per = BAND // GRID_W

    def body(*refs):
        q_ref, k_refs, v_refs = refs[0], refs[1:1 + nband], refs[1 + nband:1 + 2 * nband]
        do_ref, o_ref, lse_ref = refs[1 + 2 * nband:4 + 2 * nband]
        rest = refs[4 + 2 * nband:]
        dq_ref = rest[1] if band.has_bias else rest[0]
        i = pl.program_id(1)
        lo, masks = _head_masks()
        if band.has_bias:
            dbt_ref = rest[2]

            @pl.when(i == 0)
            def _():
                dbt_ref[...] = jnp.zeros_like(dbt_ref)

        subs = [(r, c) for r in range(rb) for c in range(cb)]
        mask = {r: band.mask(_band_idx(band, i * rb + r, 1, 0), _band_idx(band, i * rb + r - band.hb, nband, 1)) for r in range(rb)}
        items, kcats = [], []
        for r, c in subs:
            lanes, rows = slice(c * 128, (c + 1) * 128), slice(r * BAND, (r + 1) * BAND)
            kcats.append(_band_rows(band, k_refs, rb, r, lanes))
            bias = [_band_bias_k(band, rest[0], 2 * c + h) for h in range(2)] if band.has_bias else None
            items.append((q_ref[rows, lanes], kcats[-1], _band_rows(band, v_refs, rb, r, lanes), do_ref[rows, lanes],
                          o_ref[rows, lanes], lse_ref[rows, c * 256:(c + 1) * 256], mask[r], bias))
        res = _p_and_ds(items, masks)
        dqs = [[jnp.dot(ds.astype(mx), kcat, preferred_element_type=F32) for _, _, ds, _ in hs] for hs, kcat in zip(res, kcats)]
        for (r, c), hs, dq in zip(subs, res, dqs):
            dq_ref[r * BAND:(r + 1) * BAND, c * 128:(c + 1) * 128] = jnp.where(lo, dq[0], dq[1])
            if band.has_bias:
                for h in range(2):
                    ds = hs[h][2]
                    for a in range(per):
                        for t in range(nband):
                            tile = ds[a * GRID_W:(a + 1) * GRID_W, t * 128:(t + 1) * 128]
                            dbt_ref[2 * c + h, _bias_tile(t - band.hb, a)] += tile

    qs = pl.BlockSpec((rb * BAND, cb * 128), lambda c, i: (i, c))
    ks = _band_specs(band, rb, cb, nb, 128)
    st = pl.BlockSpec((rb * BAND, cb * 256), lambda c, i: (i, c))
    in_specs, args = [qs] + ks + ks + [qs, qs, st], [q] + [k] * nband + [v] * nband + [do, o, lse]
    out_specs, out_shape = [qs], [jax.ShapeDtypeStruct((n, w), F32)]
    if band.has_bias:
        bts = pl.BlockSpec((2 * cb, BT_TILES, GRID_W, 128), lambda c, i: (c, 0, 0, 0))
        in_specs.append(bts)
        args.append(bt)
        out_specs.append(bts)
        out_shape.append(jax.ShapeDtypeStruct(bt.shape, F32))
    return pl.pallas_call(
        body, name="attn_%s_dq" % band.name, grid=(ncb // cb, nb // rb), in_specs=in_specs, out_specs=out_specs,
        out_shape=out_shape, compiler_params=_params("parallel", "arbitrary"),
    )(*args)


def _band_dkv(band, q, k, v, do, o, lse, bt=None):
    n, w = q.shape
    nb, ncb, nband = n // BAND, w // 128, 2 * band.hb + 1
    rb, cb = _band_split(nb, ncb)
    mx = _MXU

    def body(*refs):
        k_ref, v_ref = refs[0], refs[1]
        q_refs, do_refs, o_refs, lse_refs = [refs[2 + g * nband:2 + (g + 1) * nband] for g in range(4)]
        rest = refs[2 + 4 * nband:]
        dk_ref, dv_ref = rest[-2], rest[-1]
        i = pl.program_id(1)
        _, masks = _head_masks()
        subs = [(r, c) for r in range(rb) for c in range(cb)]
        mask = {r: band.mask(_band_idx(band, i * rb + r - band.hb, nband, 0), _band_idx(band, i * rb + r, 1, 1)) for r in range(rb)}
        items = []
        for r, c in subs:
            lanes, rows = slice(c * 128, (c + 1) * 128), slice(r * BAND, (r + 1) * BAND)
            qcat, docat, ocat = [_band_rows(band, g, rb, r, lanes) for g in (q_refs, do_refs, o_refs)]
            lsecat = _band_rows(band, lse_refs, rb, r, slice(c * 256, (c + 1) * 256))
            bias = [_band_bias_q(band, rest[0], 2 * c + h) for h in range(2)] if band.has_bias else None
            items.append((qcat, k_ref[rows, lanes], v_ref[rows, lanes], docat, ocat, lsecat, mask[r], bias))
        res = _p_and_ds(items, masks)
        dks = [sum(_tdot(ds.astype(mx), qh) for qh, _, ds, _ in hs) for hs in res]
        dvs = [sum(_tdot(p.astype(mx), doh.astype(mx)) for _, p, _, doh in hs) for hs in res]
        for (r, c), dk, dv in zip(subs, dks, dvs):
            dk_ref[r * BAND:(r + 1) * BAND, c * 128:(c + 1) * 128] = dk
            dv_ref[r * BAND:(r + 1) * BAND, c * 128:(c + 1) * 128] = dv

    ks = pl.BlockSpec((rb * BAND, cb * 128), lambda c, i: (i, c))
    in_specs = [ks, ks] + _band_specs(band, rb, cb, nb, 128) * 3 + _band_specs(band, rb, cb, nb, 256)
    args = [k, v] + [q] * nband + [do] * nband + [o] * nband + [lse] * nband
    if band.has_bias:
        in_specs.append(pl.BlockSpec((2 * cb, BT_TILES, GRID_W, 128), lambda c, i: (c, 0, 0, 0)))
        args.append(bt)
    return pl.pallas_call(
        body, name="attn_%s_dkv" % band.name, grid=(ncb // cb, nb // rb), in_specs=in_specs, out_specs=[ks, ks],
        out_shape=[jax.ShapeDtypeStruct((n, w), F32)] * 2,
        compiler_params=_params("parallel", "arbitrary"),
    )(*args)


def _dc_onehot():
    c = np.arange(GRID_W)
    dc = np.clip(c[None, :] - c[:, None] + (C_COLS - 1), 0, 2 * C_COLS - 2).reshape(-1)
    m = np.zeros((GRID_W * GRID_W, 128), np.float32)
    m[np.arange(dc.size), dc] = 1.0
    return m


def _bias_tiles(rpb):
    h, nr, ncol = rpb.shape
    flat = jnp.pad(rpb.reshape(h * nr, ncol), ((0, (-h * nr) % 8), (0, 128 - ncol)))
    tiles = _mm(flat, jnp.asarray(_dc_onehot().T), name="rpb_tiles", exact=True, tn=GRID_W * GRID_W)
    tiles = tiles[:h * nr].reshape(h, nr, GRID_W, GRID_W)
    tiles = jnp.pad(tiles, ((0, 0), (2, BT_TILES + 1 - nr - 2), (0, 0), (0, 0)))
    return jnp.concatenate([tiles[:, :BT_TILES], tiles[:, 1:BT_TILES + 1]], -1)


def _bias_tiles_grad(dbt, nr, ncol):
    h = dbt.shape[0]
    d = dbt[:, 2:2 + nr, :, :GRID_W] + dbt[:, 1:1 + nr, :, GRID_W:]
    flat = jnp.pad(d.reshape(h * nr, GRID_W * GRID_W), ((0, (-h * nr) % 8), (0, 0)))
    g = _mm(flat, jnp.asarray(_dc_onehot()), name="rpb_grad", exact=True, tk=GRID_W * GRID_W)
    return g[:h * nr, :ncol].reshape(h, nr, ncol)


TM = 256
TQ_B, TK_B = 1024, 2048


def _relu2(acc):
    r = jnp.maximum(acc, 0.0)
    return (r * r,)


def _layer_fwd(x, xb, w, sm, tabs, alpha, ride=None):
    tab_a, tab_q, tab_k = tabs
    s, d = x.shape
    ha = _mm(xb, w["in"], name="in_a", tn=QKV_W, b_cols=QKV_W, b_off=0)
    hb = _mm(xb, w["in"], name="in_b", tn=QKV_W, b_cols=QKV_W, b_off=1)
    hc = _mm(xb, w["in"], name="in_c", tn=QKV_W, b_cols=QKV_W, b_off=2)
    hg = _mm(xb, w["in"], name="in_g", outs=(_MXU,), tn=QKV_W, b_cols=3 * d, b_off=3)

    qa, ka, va = _prep_a(ha, tab_a, TM)
    stats = [_band_fwd(_BandA(s // dil // BAND), qa[dil], ka[dil], va[dil]) for dil in A_DILATIONS]
    oas, lse_a = _combine_a(*zip(*stats), s, TM)
    oa = oas[1]

    qb, kd, vd, v1 = _prep_b(hb, sm["q_norm"], sm["k_norm"], tab_q, tab_k, TM)
    ob, lse_b, rode = _flash_fwd(qb, kd, v1, TQ_B, TK_B, ride)

    qc, kc, vc = _prep_c(hc, TM)
    bt = _bias_tiles(sm["rpb"])
    oc, lse_c = _band_fwd(_BandC(s // BAND), qc, kc, vc, bt)

    pa = _mm(oa, w["br_a"], name="br_a", outs=(_MXU,))
    pb = _mm(ob, w["br_b"], name="br_b", outs=(_MXU,))
    pc = _mm(oc, w["br_c"], name="br_c", outs=(_MXU,))
    merged = _gate_merge(hg, sm["b_gate"], pa, pb, pc, TM)
    ln = dict(outs=(F32, F32, _MXU), epilogue=_ln_epilogue(alpha), tm=512, tn=d)
    r1, x1, x1b = _mm(merged, w["out"], name="w_out_ln1", extras=(x, sm["ln1_g"], sm["ln1_b"]), **ln)
    act = _mm(x1b, w["up"], name="w_up", outs=(_MXU,), epilogue=_relu2)
    r2, x2, x2b = _mm(act, w["down"], name="w_down_ln2", extras=(x1, sm["ln2_g"], sm["ln2_b"]), **ln)
    saved = dict(xb=xb, hb=hb, hg=hg, qa=qa, ka=ka, va=va, oa=oa, oas=oas, lse_a=lse_a, qb=qb, kd=kd, vd=vd, ob=ob, lse_b=lse_b,
                 qc=qc, kc=kc, vc=vc, oc=oc, lse_c=lse_c, bt=bt, pa=pa, pb=pb, pc=pc, merged=merged, r1=r1, x1b=x1b,
                 act=act, r2=r2)
    return x2, x2b, saved, rode


def _layer_bwd(dx2, w, sm, sv, tabs, alpha):
    tab_a, tab_q, tab_k = tabs
    s, d = dx2.shape
    g = {}
    dr2, dr2b, dg2, db2 = _ln_bwd(dx2, sv["r2"], sm["ln2_g"], "ln2_bwd", TM)
    g["ln2_g"], g["ln2_b"] = dg2.sum(0), db2.sum(0)
    du = _mm(dr2b, w["down"], mode="nt", name="d_act", outs=(_MXU,), extras=(sv["act"],),
             epilogue=lambda acc, act: (acc * (2.0 * jnp.sqrt(act.astype(F32))),))
    g["w_down"] = _mm(sv["act"], dr2b, mode="tn", name="g_w_down", outs=(_MXU,)).reshape(4, -1, d)
    g["w_up"] = _mm(sv["x1b"], du, mode="tn", name="g_w_up", outs=(_MXU,), out_chips=True)
    dx1 = _mm(du, w["up"], mode="nt", name="d_x1", extras=(dr2,), epilogue=lambda acc, e: (acc + alpha * e,))
    dr1, dr1b, dg1, db1 = _ln_bwd(dx1, sv["r1"], sm["ln1_g"], "ln1_bwd", TM)
    g["ln1_g"], g["ln1_b"] = dg1.sum(0), db1.sum(0)
    g["w_out"] = _mm(sv["merged"], dr1b, mode="tn", name="g_w_out", outs=(_MXU,))
    dmerged = _mm(dr1b, w["out"], mode="nt", name="d_merged")
    dpa, dpb, dpc, dlog, gb = _gate_bwd(dmerged, sv["hg"], sm["b_gate"], sv["pa"], sv["pb"], sv["pc"], TM)
    g["b_gate"] = gb.sum(0)
    g["w_branch_a"] = _mm(sv["oa"], dpa, mode="tn", name="g_br_a", outs=(_MXU,))
    g["w_branch_b"] = _mm(sv["ob"], dpb, mode="tn", name="g_br_b", outs=(_MXU,))
    g["w_branch_c"] = _mm(sv["oc"], dpc, mode="tn", name="g_br_c", outs=(_MXU,))
    doa = _mm(dpa, w["br_a"], mode="nt", name="d_oa")
    dob = _mm(dpb, w["br_b"], mode="nt", name="d_ob")
    doc = _mm(dpc, w["br_c"], mode="nt", name="d_oc")

    dqs, dks, dvs = [], [], []
    doas = _to_dilations(doa, "d_oa_layouts", TM)
    for dil in A_DILATIONS:
        band = _BandA(s // dil // BAND)
        args = [t[dil] for t in (sv["qa"], sv["ka"], sv["va"], doas, sv["oas"], sv["lse_a"])]
        dqs.append(_band_dq(band, *args)[0])
        dk_c, dv_c = _band_dkv(band, *args)
        dks.append(dk_c)
        dvs.append(dv_c)
    dha = _post_a(dqs, dks, dvs, tab_a, s, TM)

    dqb, dkd, dvd = _flash_bwd(sv["qb"], sv["kd"], sv["vd"], dob, sv["lse_b"], _delta_b(dob, sv["ob"], TQ_B), TQ_B, TK_B)
    dhb, gq, gk = _post_b(dqb, dkd, dvd, sv["hb"], sm["q_norm"], sm["k_norm"], tab_q, tab_k, TM)
    g["q_norm_b"] = gq.sum(0).reshape(-1, HEAD).sum(0)
    g["k_norm_b"] = gk.sum(0).reshape(-1, HEAD).sum(0)

    band_c = _BandC(s // BAND)
    cargs = (sv["qc"], sv["kc"], sv["vc"], doc, sv["oc"], sv["lse_c"], sv["bt"])
    dqc, dbt = _band_dq(band_c, *cargs)
    dkc, dvc = _band_dkv(band_c, *cargs)
    dhc = _post_c(dqc, dkc, dvc, TM)
    g["rpb_c"] = _bias_tiles_grad(dbt, 2 * C_ROWS - 1, 2 * C_COLS - 1)

    xb = sv["xb"]
    g["w_in"] = jnp.concatenate([_mm(xb, dh, mode="tn", name="g_in_" + nm, outs=(_MXU,))
                                 for nm, dh in (("a", dha), ("b", dhb), ("c", dhc), ("g", dlog))], 1)
    dx = _mm(dha, w["in"], mode="nt", name="d_x_a", tk=QKV_W, b_cols=QKV_W, b_off=0, extras=(dr1,),
             epilogue=lambda acc, e: (acc + alpha * e,))
    for nm, dh, off in (("b", dhb, 1), ("c", dhc, 2), ("g", dlog, 3)):
        dx = _mm(dh, w["in"], mode="nt", name="d_x_" + nm, tk=QKV_W, b_cols=dh.shape[1], b_off=off, extras=(dx,),
                 epilogue=lambda acc, e: (acc + e,))
    return dx, g


BIG = ("w_in", "w_branch_a", "w_branch_b", "w_branch_c", "w_out", "w_up", "w_down")
ROW_SHARDED = ("w_out", "w_down")
AS_GATHERED = ("w_up", "w_down")
SMALL = ("b_gate", "q_norm_b", "k_norm_b", "rpb_c", "ln1_g", "ln1_b", "ln2_g", "ln2_b")


def _layer_weights(gathered):
    names = dict(w_in="in", w_branch_a="br_a", w_branch_b="br_b", w_branch_c="br_c", w_out="out", w_up="up", w_down="down")
    whole = {n: _full_from_shards(gathered[n], n) for n in names if n not in AS_GATHERED}
    return [{short: (whole[n], l) if n in whole else (gathered[n], l, "rows" if n in ROW_SHARDED else "cols")
             for n, short in names.items()} for l in range(gathered["w_in"].shape[1])]


def _local_step(x, target, gathered, small, rest=None):
    s, d = x.shape
    depth = small["b_gate"].shape[0]
    alpha = (2 * depth) ** 0.25
    tabs = _tables(s)
    ws = _layer_weights(gathered)
    sms = [dict(b_gate=small["b_gate"][l][None], q_norm=jnp.tile(small["q_norm_b"][l], BQ_W // HEAD)[None],
                k_norm=jnp.tile(small["k_norm_b"][l], BKV_W // HEAD)[None], rpb=small["rpb_c"][l],
                ln1_g=small["ln1_g"][l][None], ln1_b=small["ln1_b"][l][None],
                ln2_g=small["ln2_g"][l][None], ln2_b=small["ln2_b"][l][None]) for l in range(depth)]
    saved = []
    h, hb = x, x.astype(_MXU)
    for l in range(depth):
        h, hb, sv, rode = _layer_fwd(h, hb, ws[l], sms[l], tabs, alpha, rest if l == 0 else None)
        if rode:
            ws += _layer_weights(dict(zip(BIG, rode)))
        saved.append(sv)
    sq, dy = _loss_head(h, target, TM)
    grads = [None] * depth
    for l in reversed(range(depth)):
        dy, grads[l] = _layer_bwd(dy, ws[l], sms[l], saved[l], tabs, alpha)
    stacked = {k: jnp.stack([gl[k] for gl in grads], 1 if k in AS_GATHERED else 0) for k in grads[0]}
    for n in BIG:
        if n not in AS_GATHERED:
            stacked[n] = _shards_from_full(stacked[n], n).astype(_MXU)
    return sq, dy, stacked


def _place():
    return lax.axis_index("x"), lax.axis_index("y"), lax.axis_index("c")


def _flip(a, b):
    return a + b - 2 * a * b


def _other_chips(x, y):
    return [(1 - x, y), (x, 1 - y), (1 - x, 1 - y)]


def _comm_call(body, name, tensors, out_shapes, n_sems):
    n = len(tensors)

    def wrapped(*refs):
        body(refs[:n], refs[n:2 * n], refs[2 * n], refs[2 * n + 1])

    return pl.pallas_call(
        wrapped, name=name, in_specs=[ANY] * n, out_specs=[ANY] * n,
        out_shape=[jax.ShapeDtypeStruct(s, t.dtype) for s, t in zip(out_shapes, tensors)],
        scratch_shapes=[pltpu.SemaphoreType.DMA((n_sems, n)), pltpu.SemaphoreType.DMA((n_sems, n))],
    )(*tensors)


GATHER_SEMS = 7


def _gather_phases(lo, n_layers):
    def ctx(srcs, outs, send_sems, recv_sems):
        x, y, c = _place()
        n1, n2, dg = (_flip(x, 1 - c), _flip(y, c)), (_flip(x, c), _flip(y, 1 - c)), (1 - x, 1 - y)

        def half(t, chip, hc):
            rh = outs[t].shape[2] // 2
            return outs[t].at[2 * chip[0] + chip[1], :, pl.ds(hc * rh, rh)]

        def copy(k, t, src_ref, dst_ref, to):
            return pltpu.make_async_remote_copy(src_ref=src_ref, dst_ref=dst_ref, send_sem=send_sems.at[k, t],
                                                recv_sem=recv_sems.at[k, t], device_id=to, device_id_type=MESH)

        def sends(t, ks):
            rh = srcs[t].shape[1] // 2
            own, mine = srcs[t].at[pl.ds(lo, n_layers), pl.ds(c * rh, rh)], srcs[t].at[pl.ds(lo, n_layers)]
            table = {0: (own, half(t, (x, y), c), (*n1, c)), 1: (own, half(t, (x, y), c), (*n2, c)),
                     2: (half(t, n1, c), half(t, n1, c), (*n2, c)), 6: (mine, outs[t].at[2 * x + y], (x, y, 1 - c)),
                     "n1": (half(t, n1, c), half(t, n1, c), (x, y, 1 - c)), "n2": (half(t, n2, c), half(t, n2, c), (x, y, 1 - c)),
                     "dg": (half(t, dg, c), half(t, dg, c), (x, y, 1 - c))}
            sem = {0: 0, 1: 1, 2: 2, 6: 6, "n1": 3 + c, "n2": 4 - c, "dg": 5}
            return [copy(sem[k], t, *table[k]) for k in ks]

        def arrived(k, t, chip, hc):
            copy(k, t, half(t, chip, hc), half(t, chip, hc), (x, y, 1 - c)).wait_recv()

        return x, y, c, n1, n2, dg, sends, arrived, range(len(srcs))

    def phase0(*refs):
        *_, sends, _, ts = ctx(*refs)
        for t in ts:
            for cp in sends(t, (0, 1, 6)):
                cp.start()

    def phase1(*refs):
        x, y, c, n1, n2, dg, sends, arrived, ts = ctx(*refs)
        for t in ts:
            arrived(0, t, n1, c)
            for cp in sends(t, (2, "n1")):
                cp.start()
        for t in ts:
            arrived(1, t, n2, c)
            sends(t, ("n2",))[0].start()

    def phase2(*refs):
        x, y, c, n1, n2, dg, sends, arrived, ts = ctx(*refs)
        srcs, outs = refs[0], refs[1]
        for t in ts:
            arrived(2, t, dg, c)
            sends(t, ("dg",))[0].start()
        for t in ts:
            for j, chip in enumerate(_other_chips(x, y)):
                arrived(3 + j, t, chip, 1 - c)
            sends(t, (6,))[0].wait_recv()
            for cp in sends(t, (0, 1, 2, 6, "n1", "n2", "dg")):
                cp.wait_send()

    return [phase0, phase1, phase2]


def _gather_out_shapes(shards, n_layers):
    return [(4, n_layers) + s.shape[1:] for s in shards]


def _gather_shards(shards, lo, n_layers):
    phases = _gather_phases(lo, n_layers)

    def body(*refs):
        for f in phases:
            f(*refs)

    return _comm_call(body, "gather_weights", shards, _gather_out_shapes(shards, n_layers), GATHER_SEMS)


def _pair_exchange(parts):
    lh = parts[0].shape[1] // 2

    def body(srcs, outs, send_sems, recv_sems):
        x, y, c = _place()
        cps = [pltpu.make_async_remote_copy(src_ref=src.at[:, pl.ds((1 - c) * lh, lh)], dst_ref=out, send_sem=send_sems.at[0, t],
                                            recv_sem=recv_sems.at[0, t], device_id=(x, y, 1 - c), device_id_type=MESH)
               for t, (src, out) in enumerate(zip(srcs, outs))]
        for cp in cps:
            cp.start()
        for cp in cps:
            cp.wait()

    return _comm_call(body, "grad_pair_exchange", parts, [(4, lh) + p.shape[2:] for p in parts], 1)


def _chip_exchange(ts):
    def body(srcs, outs, send_sems, recv_sems):
        x, y, c = _place()
        cps = [pltpu.make_async_remote_copy(src_ref=src.at[2 * chip[0] + chip[1]], dst_ref=out.at[k], send_sem=send_sems.at[k, t],
                                            recv_sem=recv_sems.at[k, t], device_id=(*chip, c), device_id_type=MESH)
               for t, (src, out) in enumerate(zip(srcs, outs)) for k, chip in enumerate(_other_chips(x, y))]
        for cp in cps:
            cp.start()
        for cp in cps:
            cp.wait()

    return _comm_call(body, "grad_chip_exchange", ts, [(3,) + t.shape[1:] for t in ts], 3)


def _pair_share(halves):
    def body(srcs, outs, send_sems, recv_sems):
        x, y, c = _place()
        cps = [pltpu.make_async_remote_copy(src_ref=src, dst_ref=out, send_sem=send_sems.at[0, t], recv_sem=recv_sems.at[0, t],
                                            device_id=(x, y, 1 - c), device_id_type=MESH)
               for t, (src, out) in enumerate(zip(srcs, outs))]
        for cp in cps:
            cp.start()
        for cp in cps:
            cp.wait()

    theirs = _comm_call(body, "grad_pair_share", halves, [h.shape for h in halves], 1)
    c = jnp.reshape(lax.axis_index("c"), (1,)).astype(jnp.int32)
    return [_join_halves(mine, other, c, "grad_pair_join_%d" % t) for t, (mine, other) in enumerate(zip(halves, theirs))]


def _rows_view(t, lead):
    return t.reshape(t.shape[:lead] + (-1, t.shape[-1]))


def _join_halves(mine, theirs, c, name):
    shape = (2 * mine.shape[0],) + mine.shape[1:]
    mine, theirs = _rows_view(mine, 0), _rows_view(theirs, 0)
    rh, cols = mine.shape
    tr = _tile(rh, 1024, 8)

    def join(c_ref, mine_ref, theirs_ref, o_ref):
        o_ref[...] = jnp.where(pl.program_id(0) == c_ref[0], mine_ref[...], theirs_ref[...])

    spec = pl.BlockSpec((tr, cols), lambda h, i, c_ref: (i, 0))
    return pl.pallas_call(
        join, name=name,
        grid_spec=pltpu.PrefetchScalarGridSpec(
            num_scalar_prefetch=1, grid=(2, rh // tr), in_specs=[spec, spec],
            out_specs=pl.BlockSpec((tr, cols), lambda h, i, c_ref: (h * (rh // tr) + i, 0))),
        out_shape=jax.ShapeDtypeStruct((2 * rh, cols), mine.dtype),
        compiler_params=_params("parallel", "parallel"),
    )(c, mine, theirs).reshape(shape)


def _gather_all(v):
    r = v.shape[0]

    def body(src, out, send_sems, recv_sems, local_sem):
        x, y, c = _place()
        me = 4 * x + 2 * y + c
        mine = pltpu.make_async_copy(src, out.at[me], local_sem)
        mine.start()
        cps = []
        for k in range(1, 8):
            fx, fy, fc = (k >> 2) & 1, (k >> 1) & 1, k & 1
            peer = (x + fx - 2 * x * fx, y + fy - 2 * y * fy, c + fc - 2 * c * fc)
            cps.append(pltpu.make_async_remote_copy(src_ref=src, dst_ref=out.at[me], send_sem=send_sems.at[k - 1],
                                                    recv_sem=recv_sems.at[k - 1], device_id=peer, device_id_type=MESH))
        for cp in cps:
            cp.start()
        for k in range(1, 8):
            fx, fy, fc = (k >> 2) & 1, (k >> 1) & 1, k & 1
            frm = 4 * (x + fx - 2 * x * fx) + 2 * (y + fy - 2 * y * fy) + (c + fc - 2 * c * fc)
            pltpu.make_async_remote_copy(src_ref=src, dst_ref=out.at[frm], send_sem=send_sems.at[k - 1],
                                         recv_sem=recv_sems.at[k - 1], device_id=(x, y, c), device_id_type=MESH).wait_recv()
        for cp in cps:
            cp.wait_send()
        mine.wait()

    return pl.pallas_call(
        body, name="gather_small_grads", in_specs=[ANY], out_specs=ANY,
        out_shape=jax.ShapeDtypeStruct((8, r, 128), v.dtype),
        scratch_shapes=[pltpu.SemaphoreType.DMA((7,)), pltpu.SemaphoreType.DMA((7,)), pltpu.SemaphoreType.DMA],
    )(v)


def _sum_slots(parts, name):
    n, r, _ = parts.shape
    tr = _tile(r, 1024, 8)

    def body(p_ref, o_ref):
        acc = p_ref[0]
        for j in range(1, n):
            acc = acc + p_ref[j]
        o_ref[...] = acc

    return pl.pallas_call(
        body, name=name, grid=(r // tr,), in_specs=[pl.BlockSpec((n, tr, 128), lambda i: (0, i, 0))],
        out_specs=pl.BlockSpec((tr, 128), lambda i: (i, 0)), out_shape=jax.ShapeDtypeStruct((r, 128), parts.dtype),
        compiler_params=_params("parallel"),
    )(parts)


def _add_sibling_half(part, recv, c, name):
    shape = recv.shape
    part, recv = _rows_view(part, 1), _rows_view(recv, 1)
    _, rh, cols = recv.shape
    tr = _tile(rh, 1024, 16)
    nblk = rh // tr

    def body(c_ref, p_ref, r_ref, o_ref):
        o_ref[...] = (p_ref[...].astype(F32) + r_ref[...].astype(F32)).astype(o_ref.dtype)

    return pl.pallas_call(
        body, name=name,
        grid_spec=pltpu.PrefetchScalarGridSpec(
            num_scalar_prefetch=1, grid=(4, nblk),
            in_specs=[pl.BlockSpec((None, tr, cols), lambda j, i, c_ref: (j, c_ref[0] * nblk + i, 0)),
                      pl.BlockSpec((None, tr, cols), lambda j, i, c_ref: (j, i, 0))],
            out_specs=pl.BlockSpec((None, tr, cols), lambda j, i, c_ref: (j, i, 0))),
        out_shape=jax.ShapeDtypeStruct(recv.shape, recv.dtype),
        compiler_params=_params("parallel", "parallel"),
    )(c, part, recv).reshape(shape)


def _add_chips(t, recv, me, name):
    shape = t.shape[1:]
    t, recv = _rows_view(t, 1), _rows_view(recv, 1)
    _, rh, cols = t.shape
    tr = _tile(rh, 1024, 16)

    def body(me_ref, t_ref, r_ref, o_ref):
        f = lambda v: v.astype(F32)
        o_ref[...] = ((f(t_ref[...]) + f(r_ref[0])) + f(r_ref[1])) + f(r_ref[2])

    return pl.pallas_call(
        body, name=name,
        grid_spec=pltpu.PrefetchScalarGridSpec(
            num_scalar_prefetch=1, grid=(rh // tr,),
            in_specs=[pl.BlockSpec((None, tr, cols), lambda i, me_ref: (me_ref[0], i, 0)),
                      pl.BlockSpec((3, tr, cols), lambda i, me_ref: (0, i, 0))],
            out_specs=pl.BlockSpec((tr, cols), lambda i, me_ref: (i, 0))),
        out_shape=jax.ShapeDtypeStruct((rh, cols), F32),
        compiler_params=_params("parallel"),
    )(me, t, recv).reshape(shape)


def _reduce_scatter(parts):
    x, y, c = _place()
    c1, me = jnp.reshape(c, (1,)).astype(jnp.int32), jnp.reshape(2 * x + y, (1,)).astype(jnp.int32)
    ts = [_add_sibling_half(p, r, c1, "grad_pair_sum_%d" % i) for i, (p, r) in enumerate(zip(parts, _pair_exchange(parts)))]
    halves = [_add_chips(t, r, me, "grad_chip_sum_%d" % i) for i, (t, r) in enumerate(zip(ts, _chip_exchange(ts)))]
    return _pair_share(halves)


def _to_rows(parts, mult):
    flat = jnp.concatenate([p.reshape(-1) for p in parts])
    flat = jnp.pad(flat, (0, (-flat.size) % (128 * mult)))
    return flat.reshape(-1, 128)


def _from_rows(rows, shapes):
    flat, out, at = rows.reshape(-1), [], 0
    for shp in shapes:
        n = int(np.prod(shp))
        out.append(flat[at:at + n].reshape(shp))
        at += n
    return out


def _full_from_shards(g, name):
    _, depth, rows, cols = g.shape
    if name in ROW_SHARDED:
        return jnp.moveaxis(g, 0, 1).reshape(depth, 4 * rows, cols)
    return jnp.moveaxis(g, 0, 2).reshape(depth, rows, 4 * cols)


def _shards_from_full(full, name):
    depth, rows, cols = full.shape
    if name in ROW_SHARDED:
        return jnp.moveaxis(full.reshape(depth, 4, rows // 4, cols), 1, 0)
    return jnp.moveaxis(full.reshape(depth, rows, 4, cols // 4), 2, 0)


def kernel(x, w_in, b_gate, q_norm_b, k_norm_b, rpb_c, w_branch_a, w_branch_b, w_branch_c, w_out, ln1_g, ln1_b, w_up, w_down, ln2_g, ln2_b, loss_target, m_w_in, m_b_gate, m_q_norm_b, m_k_norm_b, m_rpb_c, m_w_branch_a, m_w_branch_b, m_w_branch_c, m_w_out, m_ln1_g, m_ln1_b, m_w_up, m_w_down, m_ln2_g, m_ln2_b, v_w_in, v_b_gate, v_q_norm_b, v_k_norm_b, v_rpb_c, v_w_branch_a, v_w_branch_b, v_w_branch_c, v_w_out, v_ln1_g, v_ln1_b, v_w_up, v_w_down, v_ln2_g, v_ln2_b):
    args = dict(locals())
    big_shard = {n: args[n] for n in BIG}
    small = {n: args[n] for n in SMALL}

    shards = [big_shard[n].astype(_MXU) for n in BIG]
    later = shards[0].shape[0] - 1
    first = dict(zip(BIG, _gather_shards(shards, 0, 1)))
    rest = (shards, _gather_out_shapes(shards, later), GATHER_SEMS, _gather_phases(1, later)) if later else None

    sq, grad_x, grads = _local_step(x[0], loss_target[0], first, small, rest)
    loss = lax.psum(0.5 * jnp.sum(sq) / x.shape[-1], ("x", "y", "c"))

    g_big = _reduce_scatter([grads[n] for n in BIG])
    small_shapes = [small[n].shape for n in SMALL]
    g_small = _from_rows(_sum_slots(_gather_all(_to_rows([grads[n] for n in SMALL], 8)), "small_grad_sum"), small_shapes)
    grad = dict(zip(BIG, g_big))
    grad.update(zip(SMALL, g_small))

    delta, new_m, new_v = {}, {}, {}
    for n in BIG:
        shp = big_shard[n].shape
        two_d = lambda t: t.reshape(-1, shp[-1])
        res = _adamw(two_d(big_shard[n]), two_d(grad[n]), two_d(args["m_" + n]), two_d(args["v_" + n]), "adamw_" + n)
        delta[n], new_m[n], new_v[n] = [t.reshape(shp) for t in res]
    packed = [_to_rows([args[pre + n] for n in SMALL], 8) for pre in ("", "m_", "v_")]
    res = _adamw(packed[0], _to_rows([grad[n] for n in SMALL], 8), packed[1], packed[2], "adamw_small")
    for dst, rows in zip((delta, new_m, new_v), res):
        dst.update(zip(SMALL, _from_rows(rows, small_shapes)))

    order = ("w_in", "b_gate", "q_norm_b", "k_norm_b", "rpb_c", "w_branch_a", "w_branch_b", "w_branch_c", "w_out",
             "ln1_g", "ln1_b", "w_up", "w_down", "ln2_g", "ln2_b")
    return (loss, grad_x[None], *[grad[n] for n in order], *[delta[n] for n in order],
            *[new_m[n] for n in order], *[new_v[n] for n in order])
```

```python
import functools

import numpy as np
import jax
import jax.numpy as jnp
from jax import lax
from jax.experimental import pallas as pl
from jax.experimental.pallas import tpu as pltpu

F32 = jnp.float32
_MXU = jnp.bfloat16

HEAD = 64
A_W, BQ_W, BKV_W, C_W = 256, 512, 128, 256
QKV_W = 768
A_DILATIONS = (1, 4, 16)
A_RADIUS = 64
A_ROPE_HALF = 8
AX_ROPE_HALF = 16
ROPE_THETA = 500000.0
AX_THETA = 10000.0
GRID_W = 64
C_ROWS = 8
C_COLS = 16
BAND = 128
BT_TILES = 18
LN_EPS = 1e-5
RMS_EPS = 1e-6
NEG = -1e30
SCALE = HEAD ** -0.5
LOG2E = 1.4426950408889634
LN2 = 0.6931471805599453
ADAM_LR, ADAM_B1, ADAM_B2, ADAM_EPS, ADAM_WD, ADAM_STEP = 0.001, 0.9, 0.999, 1e-08, 0.01, 10
V7X_VMEM_LIMIT = 48 * 1024 * 1024
MESH = pl.DeviceIdType.MESH
ANY = pl.BlockSpec(memory_space=pl.ANY)


def _params(*sem):
    return pltpu.CompilerParams(dimension_semantics=sem or None, vmem_limit_bytes=V7X_VMEM_LIMIT)


def _tile(n, pref, align=128):
    if n <= pref:
        return n
    t = (pref // align) * align
    while t >= align:
        if n % t == 0:
            return t
        t -= align
    return n


def _mm(a, b, *, name, mode="nn", outs=((F32),), epilogue=None, extras=(), tm=1024, tn=1024, tk=2048, exact=False,
        b_cols=None, b_off=0, out_chips=False):
    b, b_lead, b_axis = (tuple(b) + (None, None))[:3] if isinstance(b, tuple) else (b, None, None)
    m, k = a.shape if mode != "tn" else a.shape[::-1]
    b_rows = b.shape[-2] * (4 if b_axis == "rows" else 1)
    b_last = b_cols or b.shape[-1] * (4 if b_axis == "cols" else 1)
    k2, n = (b_rows, b_last) if mode != "nt" else (b_last, b_rows)
    assert k == k2, (a.shape, b.shape, mode)
    cap_rows = b.shape[-2] if b_axis == "rows" else None
    cap_cols = b.shape[-1] if b_axis == "cols" else (n // 4 if out_chips else None)
    cap_n, cap_k = (cap_cols, cap_rows) if mode != "nt" else (cap_rows, cap_cols)
    tm, tn, tk = _tile(m, tm), _tile(cap_n or n, min(tn, cap_n or tn)), _tile(cap_k or k, min(tk, cap_k or tk))
    nk = k // tk
    n_ex, n_out = len(extras), len(outs)
    mx = F32 if exact else _MXU
    prec = lax.Precision.HIGHEST if exact else None
    dims = {"nn": (((1,), (0,)), ((), ())), "nt": (((1,), (1,)), ((), ())), "tn": (((0,), (0,)), ((), ()))}[mode]

    def body(*refs):
        a_ref, b_ref = refs[0], refs[1]
        ex = refs[2:2 + n_ex]
        out_refs = refs[2 + n_ex:2 + n_ex + n_out]
        kk = pl.program_id(2)
        av, bv = a_ref[...].astype(mx), b_ref[...].astype(mx)
        part = lax.dot_general(av, bv, dims, preferred_element_type=F32, precision=prec)

        def finish(res):
            vals = epilogue(res, *[e[...] for e in ex]) if epilogue is not None else (res,)
            for o, v in zip(out_refs, vals):
                o[...] = v.astype(o.dtype)

        if nk == 1:
            finish(part)
        else:
            acc = refs[-1]

            @pl.when(kk == 0)
            def _():
                acc[...] = part

            @pl.when((kk > 0) & (kk < nk - 1))
            def _():
                acc[...] += part

            @pl.when(kk == nk - 1)
            def _():
                finish(acc[...] + part)

    a_spec = pl.BlockSpec((tm, tk), lambda i, j, kk: (i, kk)) if mode != "tn" else pl.BlockSpec((tk, tm), lambda i, j, kk: (kk, i))
    b_tile = (tn, tk) if mode == "nt" else (tk, tn)

    def b_index(i, j, kk):
        rc = [j, kk + b_off] if mode == "nt" else [kk, j + b_off]
        if b_axis is None:
            return (() if b_lead is None else (b_lead,)) + tuple(rc)
        ax = 0 if b_axis == "rows" else 1
        per = b.shape[-2 + ax] // b_tile[ax]
        chip, rc[ax] = rc[ax] // per, rc[ax] % per
        return (chip, b_lead) + tuple(rc)

    b_spec = pl.BlockSpec((None,) * (b.ndim - 2) + b_tile, b_index)
    o_spec = pl.BlockSpec((tm, tn), lambda i, j, kk: (i, j))
    if out_chips:
        per_out = n // 4 // tn
        out_specs = [pl.BlockSpec((None, tm, tn), lambda i, j, kk: (j // per_out, i, j % per_out))] * n_out
        out_shape = [jax.ShapeDtypeStruct((4, m, n // 4), d) for d in outs]
    else:
        out_specs, out_shape = [o_spec] * n_out, [jax.ShapeDtypeStruct((m, n), d) for d in outs]
    res = pl.pallas_call(
        body, name=name, grid=(m // tm, n // tn, nk),
        in_specs=[a_spec, b_spec] + [o_spec if e.shape[0] > 1 else pl.BlockSpec((1, tn), lambda i, j, kk: (0, j)) for e in extras],
        out_specs=out_specs, out_shape=out_shape,
        scratch_shapes=[pltpu.VMEM((tm, tn), F32)] if nk > 1 else [],
        compiler_params=_params("parallel", "parallel", "arbitrary"),
    )(a, b, *extras)
    return res[0] if n_out == 1 else res


def _rows(tm, width, cb=0):
    return pl.BlockSpec((tm, width), lambda t: (t, cb))


def _whole(arr):
    nd = arr.ndim
    return pl.BlockSpec(arr.shape, lambda t: (0,) * nd)


def _rowwise(fn, name, rows, tm, ins, outs):
    n_in, n_out = len(ins), len(outs)
    dil_in = [spec[1:] if isinstance(spec, tuple) else None for _, spec in ins]
    in_specs = [_rows(tm // spec[1], spec[1] * spec[2]) if isinstance(spec, tuple) else spec for _, spec in ins]
    scratch = [pltpu.VMEM((di[1] // 128, tm, 128), F32) for di in dil_in if di] + \
              [pltpu.VMEM((n // 128, tm, 128), F32) for n, _, kind in outs if isinstance(kind, int)]

    def body(*refs):
        scr = list(refs[n_in + n_out:])
        blocks = []
        for r, di in zip(refs[:n_in], dil_in):
            if di is None:
                blocks.append(r[...])
            else:
                d, n = di
                s_ref = scr.pop(0)
                for j in range(d):
                    for b in range(n // 128):
                        lanes = slice(j * n + b * 128, j * n + (b + 1) * 128)
                        s_ref.at[b][pl.ds(j, tm // d, stride=d), :] = r[:, lanes].astype(F32)
                blocks.append(jnp.concatenate([s_ref[b] for b in range(n // 128)], 1))
        vals = fn(*blocks)
        first = pl.program_id(0) == 0
        for (ncols, _, kind), o, v in zip(outs, refs[n_in:n_in + n_out], vals):
            if kind == "row":
                o[...] = v.astype(o.dtype)
            elif isinstance(kind, int):
                s_ref = scr.pop(0)
                for b in range(ncols // 128):
                    s_ref[b] = v[:, b * 128:(b + 1) * 128].astype(F32)
                for j in range(kind):
                    for b in range(ncols // 128):
                        lanes = slice(j * ncols + b * 128, j * ncols + (b + 1) * 128)
                        o[:, lanes] = s_ref.at[b][pl.ds(j, tm // kind, stride=kind), :].astype(o.dtype)
            else:
                part = v.reshape(tm // 8, 8, ncols).sum(0)

                @pl.when(first)
                def _(o=o, part=part):
                    o[...] = part

                @pl.when(jnp.logical_not(first))
                def _(o=o, part=part):
                    o[...] += part

    def out_spec(n, kind):
        if kind == "row":
            return _rows(tm, n), (rows, n)
        if isinstance(kind, int):
            return _rows(tm // kind, kind * n), (rows // kind, kind * n)
        return pl.BlockSpec((8, n), lambda t: (0, 0)), (8, n)

    specs = [out_spec(n, kind) for n, _, kind in outs]
    res = pl.pallas_call(
        body, name=name, grid=(rows // tm,),
        in_specs=in_specs, out_specs=[s for s, _ in specs],
        out_shape=[jax.ShapeDtypeStruct(shp, d) for (_, shp), (_, d, _) in zip(specs, outs)],
        scratch_shapes=scratch, compiler_params=_params("arbitrary"),
    )(*[a for a, _ in ins])
    return res


def _lane_lo(width=128):
    return (lax.broadcasted_iota(jnp.int32, (1, width), 1) & (HEAD * 2 - 1)) < HEAD


def _group_sum(x):
    w = x.shape[-1]
    sh = HEAD.bit_length() - 1
    same = (lax.broadcasted_iota(jnp.int32, (w, w), 0) >> sh) == (lax.broadcasted_iota(jnp.int32, (w, w), 1) >> sh)
    ones = jnp.where(same, 1.0, 0.0).astype(jnp.bfloat16)
    hi = x.astype(jnp.bfloat16)
    lo = (x - hi.astype(F32)).astype(jnp.bfloat16)
    return jnp.dot(hi, ones, preferred_element_type=F32) + jnp.dot(lo, ones, preferred_element_type=F32)


def _rot(x, c, sm, sp, shift):
    w = x.shape[-1]
    return x * c + pltpu.roll(x, w - shift, 1) * sm + pltpu.roll(x, shift, 1) * sp


def _rot_t(dy, c, sm, sp, shift):
    w = dy.shape[-1]
    return dy * c + pltpu.roll(dy * sm, shift, 1) + pltpu.roll(dy * sp, w - shift, 1)


def _rope_tables(pos_parts, half, thetas):
    cs, sms, sps = [], [], []
    for pos, theta in zip(pos_parts, thetas):
        inv = theta ** (-jnp.arange(half, dtype=F32) / half)
        ang = pos.astype(F32)[:, None] * inv[None, :]
        co, si, ze = jnp.cos(ang), jnp.sin(ang), jnp.zeros_like(ang)
        cs += [co, co]
        sms += [-si, ze]
        sps += [ze, si]
    return [jnp.concatenate(t, axis=1) for t in (cs, sms, sps)]


def _tables(s):
    pos = jnp.arange(s)
    ca, sma, spa = _rope_tables([pos], A_ROPE_HALF, [ROPE_THETA])
    pad = HEAD - 2 * A_ROPE_HALF
    ca = jnp.concatenate([ca, jnp.ones((s, pad), F32)], 1)
    sma, spa = [jnp.concatenate([t, jnp.zeros((s, pad), F32)], 1) for t in (sma, spa)]
    tab_a = [jnp.tile(t, (1, A_W // HEAD)) for t in (ca, sma, spa)]
    ax = _rope_tables([pos // GRID_W, pos % GRID_W], AX_ROPE_HALF, [AX_THETA, AX_THETA])
    tab_q = [jnp.tile(t, (1, BQ_W // HEAD)) for t in ax]
    tab_k = [jnp.tile(t, (1, BKV_W // HEAD)) for t in ax]
    return tab_a, tab_q, tab_k


def _prep_a(ha, tab, tm):
    s = ha.shape[0]

    def fn(h, c, sm, sp):
        q, k, v = h[:, :A_W], h[:, A_W:2 * A_W], h[:, 2 * A_W:]
        q, k = _rot(q, c, sm, sp, A_ROPE_HALF) * SCALE, _rot(k, c, sm, sp, A_ROPE_HALF)
        return [t for t in (q, k, v) for _ in A_DILATIONS]

    res = _rowwise(fn, "prep_a", s, tm, [(ha, _rows(tm, QKV_W))] + [(t, _rows(tm, A_W)) for t in tab],
                   [(A_W, _MXU, _dil_kind(d)) for _ in range(3) for d in A_DILATIONS])
    n = len(A_DILATIONS)
    return [dict(zip(A_DILATIONS, res[i * n:(i + 1) * n])) for i in range(3)]


def _dil_kind(d):
    return "row" if d == 1 else d


def _dil_spec(d, tm, ncols):
    return _rows(tm, ncols) if d == 1 else ("dil", d, ncols)


def _to_dilations(x, name, tm):
    s, n = x.shape
    res = _rowwise(lambda v: [v for d in A_DILATIONS if d > 1], name, s, tm, [(x, _rows(tm, n))],
                   [(n, x.dtype, d) for d in A_DILATIONS if d > 1])
    return {1: x, **dict(zip([d for d in A_DILATIONS if d > 1], res))}


def _rms(x, g):
    ms = _group_sum(x * x) * (1.0 / HEAD)
    return x * lax.rsqrt(ms + RMS_EPS) * g


def _prep_b(hb, gq, gk, tab_q, tab_k, tm):
    s = hb.shape[0]

    def fn(h, gq, gk, cq, smq, spq, ck, smk, spk):
        xq, xk, v = h[:, :BQ_W], h[:, BQ_W:BQ_W + BKV_W], h[:, BQ_W + BKV_W:]
        q = _rot(_rms(xq, gq), cq, smq, spq, AX_ROPE_HALF) * (SCALE * LOG2E)
        k = _rot(_rms(xk, gk), ck, smk, spk, AX_ROPE_HALF)
        lo = _lane_lo()
        kr, vr = pltpu.roll(k, HEAD, 1), pltpu.roll(v, HEAD, 1)
        kd = jnp.concatenate([jnp.where(lo, k, kr), jnp.where(lo, kr, k)], 1)
        vd = jnp.concatenate([jnp.where(lo, v, vr), jnp.where(lo, vr, v)], 1)
        v1 = jnp.concatenate([jnp.where(lo, v, 1.0), jnp.where(lo, vr, 1.0)], 1)
        return q, kd, vd, v1

    ins = [(hb, _rows(tm, QKV_W)), (gq, _whole(gq)), (gk, _whole(gk))]
    ins += [(t, _rows(tm, BQ_W)) for t in tab_q] + [(t, _rows(tm, BKV_W)) for t in tab_k]
    return _rowwise(fn, "prep_b", s, tm, ins, [(BQ_W, _MXU, "row")] + [(2 * BKV_W, _MXU, "row")] * 3)


def _prep_c(hc, tm):
    def fn(h):
        return h[:, :C_W] * SCALE, h[:, C_W:2 * C_W], h[:, 2 * C_W:]

    return _rowwise(fn, "prep_c", hc.shape[0], tm, [(hc, _rows(tm, QKV_W))], [(C_W, _MXU, "row")] * 3)


def _combine_a(os_, ms, ls, s, tm):
    def fn(o1, o2, o3, m1, m2, m3, l1, l2, l3):
        lo = _lane_lo()
        outs, lses = [], []
        for p in range(A_W // 128):
            st = slice(p * 256, (p + 1) * 256)
            mm = [m[:, st] for m in (m1, m2, m3)]
            ll = [l[:, st] for l in (l1, l2, l3)]
            mmax = jnp.maximum(jnp.maximum(mm[0], mm[1]), mm[2])
            ws = [jnp.exp(m - mmax) for m in mm]
            den = ws[0] * ll[0] + ws[1] * ll[1] + ws[2] * ll[2]
            lses.append(mmax + jnp.log(den))
            num = sum(jnp.where(lo, w[:, :128], w[:, 128:]) * o[:, p * 128:(p + 1) * 128] for w, o in zip(ws, (o1, o2, o3)))
            outs.append(num / jnp.where(lo, den[:, :128], den[:, 128:]))
        o, lse = jnp.concatenate(outs, 1), jnp.concatenate(lses, 1)
        return [o] * len(A_DILATIONS) + [lse] * len(A_DILATIONS)

    ins = [(t, _dil_spec(d, tm, w)) for ts, w in ((os_, A_W), (ms, 2 * A_W), (ls, 2 * A_W)) for t, d in zip(ts, A_DILATIONS)]
    res = _rowwise(fn, "combine_a", s, tm, ins,
                   [(w, F32, _dil_kind(d)) for w in (A_W, 2 * A_W) for d in A_DILATIONS])
    n = len(A_DILATIONS)
    return dict(zip(A_DILATIONS, res[:n])), dict(zip(A_DILATIONS, res[n:]))


def _gates(hg, bg, d):
    return [jax.nn.sigmoid(hg[:, i * d:(i + 1) * d] + bg[:, i * d:(i + 1) * d]) for i in range(3)]


def _gate_merge(hg, bg, pa, pb, pc, tm):
    s, d = pa.shape

    def fn(hg, bg, pa, pb, pc):
        g = _gates(hg, bg, d)
        return (g[0] * pa + g[1] * pb + g[2] * pc,)

    ins = [(hg, _rows(tm, 3 * d)), (bg, _whole(bg))] + [(p, _rows(tm, d)) for p in (pa, pb, pc)]
    return _rowwise(fn, "gate_merge", s, tm, ins, [(d, _MXU, "row")])[0]


def _gate_bwd(dm, hg, bg, pa, pb, pc, tm):
    s, d = pa.shape

    def fn(dm, hg, bg, pa, pb, pc):
        g = _gates(hg, bg, d)
        dlog = jnp.concatenate([dm * p * gi * (1.0 - gi) for p, gi in zip((pa, pb, pc), g)], 1)
        return dm * g[0], dm * g[1], dm * g[2], dlog, dlog

    ins = [(dm, _rows(tm, d)), (hg, _rows(tm, 3 * d)), (bg, _whole(bg))] + [(p, _rows(tm, d)) for p in (pa, pb, pc)]
    return _rowwise(fn, "gate_bwd", s, tm, ins, [(d, _MXU, "row")] * 3 + [(3 * d, _MXU, "row"), (3 * d, F32, "acc")])


def _ln_stats(r):
    mu = jnp.mean(r, -1, keepdims=True)
    xc = r - mu
    var = jnp.mean(xc * xc, -1, keepdims=True)
    rstd = lax.rsqrt(var + LN_EPS)
    return xc * rstd, rstd


def _ln_epilogue(alpha):
    def fn(br, x, g, b):
        r = alpha * x + br
        xhat, _ = _ln_stats(r)
        y = xhat * g + b
        return r, y, y

    return fn


def _ln_bwd(dy, r, g, name, tm):
    s, d = r.shape

    def fn(dy, r, g):
        xhat, rstd = _ln_stats(r)
        dxh = dy * g
        dr = rstd * (dxh - jnp.mean(dxh, -1, keepdims=True) - xhat * jnp.mean(dxh * xhat, -1, keepdims=True))
        return dr, dr, dy * xhat, dy

    ins = [(dy, _rows(tm, d)), (r, _rows(tm, d)), (g, _whole(g))]
    return _rowwise(fn, name, s, tm, ins, [(d, F32, "row"), (d, _MXU, "row"), (d, F32, "acc"), (d, F32, "acc")])


def _loss_head(y, target, tm):
    s, d = y.shape

    def fn(y, t):
        diff = y - t
        return diff * diff, diff * (1.0 / d)

    sq, dy = _rowwise(fn, "loss_head", s, tm, [(y, _rows(tm, d)), (target, _rows(tm, d))], [(d, F32, "acc"), (d, F32, "row")])
    return sq, dy


def _post_a(dqs, dks, dvs, tab, s, tm):
    def fn(q1, q2, q3, k1, k2, k3, v1, v2, v3, c, sm, sp):
        dq = _rot_t((q1 + q2 + q3) * SCALE, c, sm, sp, A_ROPE_HALF)
        dk = _rot_t(k1 + k2 + k3, c, sm, sp, A_ROPE_HALF)
        return (jnp.concatenate([dq, dk, v1 + v2 + v3], 1),)

    ins = [(t, _dil_spec(d, tm, A_W)) for ts in (dqs, dks, dvs) for t, d in zip(ts, A_DILATIONS)]
    ins += [(t, _rows(tm, A_W)) for t in tab]
    return _rowwise(fn, "post_a", s, tm, ins, [(QKV_W, _MXU, "row")])[0]


def _post_b(dq, dkd, dvd, hb, gq, gk, tab_q, tab_k, tm):
    s = dq.shape[0]

    def back(dz, x, g, c, sm, sp):
        dy = _rot_t(dz, c, sm, sp, AX_ROPE_HALF)
        rstd = lax.rsqrt(_group_sum(x * x) * (1.0 / HEAD) + RMS_EPS)
        xh = x * rstd
        dxh = dy * g
        return rstd * (dxh - xh * (_group_sum(dxh * xh) * (1.0 / HEAD))), dy * xh

    def fn(dq, dkd, dvd, h, gq, gk, cq, smq, spq, ck, smk, spk):
        lo = _lane_lo()
        dk = jnp.where(lo, dkd[:, :128], dkd[:, 128:])
        dv = jnp.where(lo, dvd[:, :128], dvd[:, 128:])
        dxq, dgq = back(dq * SCALE, h[:, :BQ_W], gq, cq, smq, spq)
        dxk, dgk = back(dk, h[:, BQ_W:BQ_W + BKV_W], gk, ck, smk, spk)
        return jnp.concatenate([dxq, dxk, dv], 1), dgq, dgk

    ins = [(dq, _rows(tm, BQ_W)), (dkd, _rows(tm, 2 * BKV_W)), (dvd, _rows(tm, 2 * BKV_W)), (hb, _rows(tm, QKV_W)),
           (gq, _whole(gq)), (gk, _whole(gk))]
    ins += [(t, _rows(tm, BQ_W)) for t in tab_q] + [(t, _rows(tm, BKV_W)) for t in tab_k]
    return _rowwise(fn, "post_b", s, tm, ins, [(QKV_W, _MXU, "row"), (BQ_W, F32, "acc"), (BKV_W, F32, "acc")])


def _post_c(dq, dk, dv, tm):
    def fn(dq, dk, dv):
        return (jnp.concatenate([dq * SCALE, dk, dv], 1),)

    return _rowwise(fn, "post_c", dq.shape[0], tm, [(t, _rows(tm, C_W)) for t in (dq, dk, dv)], [(QKV_W, _MXU, "row")])[0]


def _adamw(w, g, m, v, name):
    rows, cols = w.shape
    tm = _tile(rows, 256, 8)

    def fn(w, g, m, v):
        m = ADAM_B1 * m + (1.0 - ADAM_B1) * g
        v = ADAM_B2 * v + (1.0 - ADAM_B2) * (g * g)
        m_hat = m / (1.0 - ADAM_B1 ** ADAM_STEP)
        v_hat = v / (1.0 - ADAM_B2 ** ADAM_STEP)
        delta = -ADAM_LR * (m_hat / (jnp.sqrt(v_hat) + ADAM_EPS) + ADAM_WD * w)
        return delta, m, v

    return _rowwise(fn, name, rows, tm, [(t, _rows(tm, cols)) for t in (w, g, m, v)], [(cols, F32, "row")] * 3)


def _dot_t(a, b):
    return lax.dot_general(a, b, (((1,), (1,)), ((), ())), preferred_element_type=F32)


def _tdot(a, b):
    return lax.dot_general(a, b, (((0,), (0,)), ((), ())), preferred_element_type=F32)


def _head_masks():
    lo = _lane_lo()
    return lo, (lo, jnp.logical_not(lo))


def _rep(x, rows):
    return jnp.broadcast_to(x, (rows, 128))


def _row_lo():
    return lax.broadcasted_iota(jnp.int32, (128, 1), 0) < HEAD


def _flash_fwd(q, kd, v1, tq, tk, ride=None):
    s = q.shape[0]
    tq, tk = _tile(s, tq), _tile(s, tk)
    nq, nk = s // tq, s // tk
    mx = _MXU
    n_ride = len(ride[0]) if ride else 0
    steps = BQ_W // 128 * nq * nk

    def body(*refs):
        q_ref, k_ref, v_ref = refs[:3]
        o_ref, lse_ref = refs[3 + n_ride:5 + n_ride]
        m_ref, acc_ref = refs[5 + 2 * n_ride:7 + 2 * n_ride]
        kk = pl.program_id(2)
        if ride:
            comm = (refs[3:3 + n_ride], refs[5 + n_ride:5 + 2 * n_ride], refs[-2], refs[-1])
            step = (pl.program_id(0) * nq + pl.program_id(1)) * nk + kk
            for at, phase in zip((0, steps // 3), ride[3][:2]):
                pl.when(step == at)(functools.partial(phase, *comm))

        @pl.when(kk == 0)
        def _():
            m_ref[...] = jnp.full_like(m_ref, NEG)
            acc_ref[...] = jnp.zeros_like(acc_ref)

        q2, k2, v2 = q_ref[...], k_ref[...], v_ref[...]
        _, masks = _head_masks()
        hs = range(2)
        st = [_dot_t(k2, jnp.where(masks[h], q2, jnp.zeros_like(q2))) for h in hs]
        m_prev = [m_ref[h] for h in hs]
        m_new = [jnp.maximum(m_prev[h], jnp.max(st[h], 0, keepdims=True)) for h in hs]
        p = [jnp.exp2(st[h] - m_new[h]).astype(mx) for h in hs]
        pv = [_tdot(v2, p[h]) for h in hs]
        for h in hs:
            m_ref[h] = m_new[h]
            acc_ref[h] = acc_ref[h] * jnp.exp2(m_prev[h] - m_new[h]) + pv[h]

        @pl.when(kk == nk - 1)
        def _():
            a0, a1 = acc_ref[0], acc_ref[1]
            l0, l1 = a0[HEAD:HEAD + 1], a1[HEAD:HEAD + 1]
            o_ref[...] = jnp.concatenate([a0[:HEAD] / l0, a1[:HEAD] / l1], 0).T
            lse_ref[...] = jnp.concatenate([m_ref[0] + jnp.log2(l0), m_ref[1] + jnp.log2(l1), jnp.zeros((6, tq), F32)], 0)

        if ride:
            pl.when(step == steps - 1)(functools.partial(ride[3][2], *comm))

    ride_in = list(ride[0]) if ride else []
    ride_out = [jax.ShapeDtypeStruct(shp, t.dtype) for shp, t in zip(ride[1], ride[0])] if ride else []
    ride_sems = [pltpu.SemaphoreType.DMA((ride[2], n_ride))] * 2 if ride else []
    res = pl.pallas_call(
        body, name="attn_b_fwd_gather" if ride else "attn_b_fwd", grid=(BQ_W // 128, nq, nk),
        in_specs=[pl.BlockSpec((tq, 128), lambda j, i, kk: (i, j)),
                  pl.BlockSpec((tk, 128), lambda j, i, kk: (kk, j // 2)),
                  pl.BlockSpec((tk, 128), lambda j, i, kk: (kk, j // 2))] + [ANY] * n_ride,
        out_specs=[pl.BlockSpec((tq, 128), lambda j, i, kk: (i, j)), pl.BlockSpec((None, 8, tq), lambda j, i, kk: (j, 0, i))]
        + [ANY] * n_ride,
        out_shape=[jax.ShapeDtypeStruct((s, BQ_W), F32), jax.ShapeDtypeStruct((BQ_W // 128, 8, s), F32)] + ride_out,
        scratch_shapes=[pltpu.VMEM((2, 1, tq), F32), pltpu.VMEM((2, 128, tq), F32)] + ride_sems,
        compiler_params=_params("arbitrary", "arbitrary", "arbitrary") if ride else _params("parallel", "parallel", "arbitrary"),
    )(q, kd, v1, *ride_in)
    return res[0], res[1], res[2:]


def _delta_b(do, o, tq):
    s = do.shape[0]
    tq = _tile(s, tq)

    def body(do_ref, o_ref, d_ref):
        prod = do_ref[...] * o_ref[...]
        row = lax.broadcasted_iota(jnp.int32, (8, 128), 0)
        lane = lax.broadcasted_iota(jnp.int32, (8, 128), 1)
        sel = jnp.where(((row == 0) & (lane < HEAD)) | ((row == 1) & (lane >= HEAD)), 1.0, 0.0).astype(F32)
        d_ref[...] = lax.dot_general(sel, prod, (((1,), (1,)), ((), ())), preferred_element_type=F32,
                                     precision=lax.Precision.HIGHEST)

    qs = pl.BlockSpec((tq, 128), lambda j, i: (i, j))
    return pl.pallas_call(
        body, name="attn_b_delta", grid=(BQ_W // 128, s // tq), in_specs=[qs, qs],
        out_specs=pl.BlockSpec((None, 8, tq), lambda j, i: (j, 0, i)),
        out_shape=jax.ShapeDtypeStruct((BQ_W // 128, 8, s), F32),
        compiler_params=_params("parallel", "parallel"),
    )(do, o)


def _flash_bwd(q, kd, vd, do, lse, delta, tq, tk, ride=None):
    s = q.shape[0]
    tq, tk = _tile(s, tq), _tile(s, tk)
    nq, nk = s // tq, s // tk
    group = BQ_W // 128 // 2
    mx = _MXU
    n_ride = len(ride[0]) if ride else 0
    steps = BKV_W // HEAD * nk * group * nq

    def body(*refs):
        k_ref, v_ref, q_ref, do_ref, lse_ref, dl_ref = refs[:6]
        dq_hbm, dk_ref, dv_ref = refs[6 + n_ride:9 + n_ride]
        dk_acc, dv_acc, dqt, stage, sem = refs[9 + 2 * n_ride:14 + 2 * n_ride]
        e, kk, jj, i = pl.program_id(0), pl.program_id(1), pl.program_id(2), pl.program_id(3)
        if ride:
            comm = (refs[6:6 + n_ride], refs[9 + n_ride:9 + 2 * n_ride], refs[-2], refs[-1])
            step = ((e * nk + kk) * group + jj) * nq + i
            for at, phase in zip((0, steps // 3), ride[3][:2]):
                pl.when(step == at)(functools.partial(phase, *comm))

        @pl.when((jj == 0) & (i == 0))
        def _():
            dk_acc[...] = jnp.zeros_like(dk_acc)
            dv_acc[...] = jnp.zeros_like(dv_acc)

        @pl.when(kk == 0)
        def _():
            dqt[jj, i] = jnp.zeros((128, tq), F32)

        q2, k2, v2, do2 = q_ref[...], k_ref[...], v_ref[...], do_ref[...].astype(mx)
        lse8, dl8 = lse_ref[...], dl_ref[...]
        _, masks = _head_masks()
        hs = range(2)
        qh = [jnp.where(masks[h], q2, jnp.zeros_like(q2)) for h in hs]
        doh = [jnp.where(masks[h], do2, jnp.zeros_like(do2)) for h in hs]
        st = [_dot_t(k2, qh[h]) for h in hs]
        dpt = [_dot_t(v2, doh[h]) for h in hs]
        p = [jnp.exp2(st[h] - lse8[h:h + 1]) for h in hs]
        ds = [(p[h] * (dpt[h] - dl8[h:h + 1])).astype(mx) for h in hs]
        p = [p[h].astype(mx) for h in hs]
        dv_acc[...] += jnp.dot(p[0], doh[0], preferred_element_type=F32) + jnp.dot(p[1], doh[1], preferred_element_type=F32)
        dk_acc[...] += jnp.dot(ds[0], qh[0], preferred_element_type=F32) + jnp.dot(ds[1], qh[1], preferred_element_type=F32)
        dqt[jj, i] += jnp.where(_row_lo(), _tdot(k2, ds[0]), _tdot(k2, ds[1]))

        @pl.when(kk == nk - 1)
        def _():
            stage[...] = dqt[jj, i].T
            lane0 = pl.multiple_of((group * e + jj) * 128, 128)
            cp = pltpu.make_async_copy(stage, dq_hbm.at[pl.ds(pl.multiple_of(i * tq, tq), tq), pl.ds(lane0, 128)], sem)
            cp.start()
            cp.wait()

        @pl.when((jj == group - 1) & (i == nq - 1))
        def _():
            dk_ref[...] = (dk_acc[...] + pltpu.roll(dk_acc[...], HEAD, 1)) * LN2
            dv_ref[...] = dv_acc[...] + pltpu.roll(dv_acc[...], HEAD, 1)

        if ride:
            pl.when(step == steps - 1)(functools.partial(ride[3][2], *comm))

    ks = pl.BlockSpec((tk, 128), lambda e, kk, jj, i: (kk, e))
    qs = pl.BlockSpec((tq, 128), lambda e, kk, jj, i: (i, group * e + jj))
    st = pl.BlockSpec((None, 8, tq), lambda e, kk, jj, i: (group * e + jj, 0, i))
    ride_in = list(ride[0]) if ride else []
    ride_out = [jax.ShapeDtypeStruct(shp, t.dtype) for shp, t in zip(ride[1], ride[0])] if ride else []
    ride_sems = [pltpu.SemaphoreType.DMA((ride[2], n_ride))] * 2 if ride else []
    res = pl.pallas_call(
        body, name="attn_b_bwd_exchange" if ride else "attn_b_bwd", grid=(BKV_W // HEAD, nk, group, nq),
        in_specs=[ks, ks, qs, qs, st, st] + [ANY] * n_ride, out_specs=[ANY, ks, ks] + [ANY] * n_ride,
        out_shape=[jax.ShapeDtypeStruct((s, BQ_W), F32)] + [jax.ShapeDtypeStruct((s, 2 * BKV_W), F32)] * 2 + ride_out,
        scratch_shapes=[pltpu.VMEM((tk, 128), F32)] * 2 + [pltpu.VMEM((group, nq, 128, tq), F32), pltpu.VMEM((tq, 128), F32),
                                                          pltpu.SemaphoreType.DMA] + ride_sems,
        compiler_params=_params("arbitrary", "arbitrary", "arbitrary", "arbitrary"),
    )(kd, vd, q, do, lse, delta, *ride_in)
    return res[0], res[1], res[2], res[3:]


def _p_and_ds(items, masks):
    mx = _MXU
    keys = [(n, h) for n in range(len(items)) for h in range(2)]
    qh = {(n, h): jnp.where(masks[h], items[n][0], jnp.zeros_like(items[n][0])) for n, h in keys}
    doh = {(n, h): jnp.where(masks[h], items[n][3], jnp.zeros_like(items[n][3])) for n, h in keys}
    sc = {}
    for n, h in keys:
        s_h = _dot_t(qh[n, h], items[n][1])
        if items[n][7] is not None:
            s_h = s_h + items[n][7][h]
        sc[n, h] = jnp.where(items[n][6], s_h, NEG)
    dp = {(n, h): _dot_t(doh[n, h].astype(mx), items[n][2]) for n, h in keys}
    lse = {(n, h): jnp.max(items[n][5][:, h * 128:(h + 1) * 128], -1, keepdims=True) for n, h in keys}
    delta = {(n, h): jnp.sum(doh[n, h] * items[n][4], -1, keepdims=True) for n, h in keys}
    p = {key: jnp.exp(sc[key] - lse[key]) for key in keys}
    ds = {key: p[key] * (dp[key] - delta[key]) for key in keys}
    return [[(qh[n, h], p[n, h], ds[n, h], doh[n, h]) for h in range(2)] for n in range(len(items))]


class _BandA:
    hb, has_bias, name = 1, False, "a"

    def __init__(self, nb):
        self.nb = nb

    def mask(self, qidx, kidx):
        n = self.nb * BAND
        return (jnp.abs(qidx - kidx) <= A_RADIUS) & (kidx >= 0) & (kidx < n) & (qidx >= 0) & (qidx < n)


class _BandC:
    hb, has_bias, name = 3, True, "c"

    def __init__(self, nb):
        self.nb = nb
        self.rows = nb * BAND // GRID_W
        per = BAND // GRID_W
        assert self.rows >= C_ROWS and (C_ROWS - 1) // per <= self.hb
        assert (self.rows - 1) // per - (self.rows - C_ROWS) // per <= self.hb

    def mask(self, qidx, kidx):
        n = self.nb * BAND
        sh = GRID_W.bit_length() - 1
        qrow, cq = qidx >> sh, qidx & (GRID_W - 1)
        krow, ck = kidx >> sh, kidx & (GRID_W - 1)
        r0 = jnp.clip(qrow - C_ROWS // 2, 0, self.rows - C_ROWS)
        c0 = jnp.clip(cq - C_COLS // 2, 0, GRID_W - C_COLS)
        ok = (qidx >= 0) & (qidx < n) & (kidx >= 0) & (kidx < n)
        return ok & (krow >= r0) & (krow < r0 + C_ROWS) & (ck >= c0) & (ck < c0 + C_COLS)


def _bias_tile(off, a):
    return (BAND // GRID_W) * off - a + (C_ROWS - 1) + 2


def _band_bias_k(band, bt_ref, h):
    per = BAND // GRID_W
    return jnp.concatenate([jnp.concatenate([bt_ref[h, _bias_tile(off, a)] for off in range(-band.hb, band.hb + 1)], 1)
                            for a in range(per)], 0)


def _band_bias_q(band, bt_ref, h):
    per = BAND // GRID_W
    return jnp.concatenate([bt_ref[h, _bias_tile(-off, a)] for off in range(-band.hb, band.hb + 1) for a in range(per)], 0)


def _band_split(nb, ncb):
    cb = max(c for c in (4, 2, 1) if ncb % c == 0)
    rb = max(r for r in (4, 2, 1) if nb % r == 0 and r * cb <= 16)
    return rb, cb


def _band_specs(band, rb, cb, nb, width):
    def edge(first):
        return pl.BlockSpec((BAND, cb * width), lambda c, i: (jnp.clip(i * rb + first, 0, nb - 1), c))

    main = pl.BlockSpec((rb * BAND, cb * width), lambda c, i: (i, c))
    return [edge(t - band.hb) for t in range(band.hb)] + [main] + [edge(rb + t) for t in range(band.hb)]


def _band_rows(band, refs, rb, r, lanes):
    hb = band.hb
    parts = []
    for b in range(r, r + 2 * hb + 1):
        if b < hb:
            parts.append(refs[b][:, lanes])
        elif b < hb + rb:
            parts.append(refs[hb][(b - hb) * BAND:(b - hb + 1) * BAND, lanes])
        else:
            parts.append(refs[b - rb + 1][:, lanes])
    return jnp.concatenate(parts, 0)


def _band_idx(band, blk, rows_of_blocks, axis):
    shape = (rows_of_blocks * BAND, 1) if axis == 0 else (1, rows_of_blocks * BAND)
    return blk * BAND + lax.broadcasted_iota(jnp.int32, shape, axis)


def _band_fwd(band, q, k, v, bt=None):
    n, w = q.shape
    nb, ncb, nband = n // BAND, w // 128, 2 * band.hb + 1
    rb, cb = _band_split(nb, ncb)
    mx = _MXU
    raw = not band.has_bias

    def body(*refs):
        q_ref, k_refs, v_refs = refs[0], refs[1:1 + nband], refs[1 + nband:1 + 2 * nband]
        rest = refs[1 + 2 * nband:]
        bt_ref = rest[0] if band.has_bias else None
        outs = rest[1:] if band.has_bias else rest
        i = pl.program_id(1)
        lo, masks = _head_masks()
        subs = [(r, c) for r in range(rb) for c in range(cb)]
        mask = {r: band.mask(_band_idx(band, i * rb + r, 1, 0), _band_idx(band, i * rb + r - band.hb, nband, 1)) for r in range(rb)}
        lanes = {c: slice(c * 128, (c + 1) * 128) for c in range(cb)}
        rows = {r: slice(r * BAND, (r + 1) * BAND) for r in range(rb)}
        sc = {}
        for r, c in subs:
            q2, kcat = q_ref[rows[r], lanes[c]], _band_rows(band, k_refs, rb, r, lanes[c])
            for h in range(2):
                s_h = _dot_t(jnp.where(masks[h], q2, jnp.zeros_like(q2)), kcat)
                if band.has_bias:
                    s_h = s_h + _band_bias_k(band, bt_ref, 2 * c + h)
                sc[r, c, h] = jnp.where(mask[r], s_h, NEG)
        ms = {key: jnp.max(s_h, -1, keepdims=True) for key, s_h in sc.items()}
        ps = {key: jnp.exp(s_h - ms[key]) for key, s_h in sc.items()}
        ls = {key: jnp.sum(p, -1, keepdims=True) for key, p in ps.items()}
        os_ = {}
        for r, c in subs:
            vcat = _band_rows(band, v_refs, rb, r, lanes[c])
            for h in range(2):
                os_[r, c, h] = jnp.dot(ps[r, c, h].astype(mx), vcat, preferred_element_type=F32)
        for r, c in subs:
            st_lanes = [slice(c * 256 + h * 128, c * 256 + (h + 1) * 128) for h in range(2)]
            if raw:
                o_ref, m_ref, l_ref = outs
                o_ref[rows[r], lanes[c]] = jnp.where(lo, os_[r, c, 0], os_[r, c, 1])
                for h in range(2):
                    m_ref[rows[r], st_lanes[h]] = _rep(ms[r, c, h], BAND)
                    l_ref[rows[r], st_lanes[h]] = _rep(ls[r, c, h], BAND)
            else:
                o_ref, lse_ref = outs
                o_ref[rows[r], lanes[c]] = jnp.where(lo, os_[r, c, 0] / ls[r, c, 0], os_[r, c, 1] / ls[r, c, 1])
                for h in range(2):
                    lse_ref[rows[r], st_lanes[h]] = _rep(ms[r, c, h] + jnp.log(ls[r, c, h]), BAND)

    qs = pl.BlockSpec((rb * BAND, cb * 128), lambda c, i: (i, c))
    ks = _band_specs(band, rb, cb, nb, 128)
    st = pl.BlockSpec((rb * BAND, cb * 256), lambda c, i: (i, c))
    in_specs, args = [qs] + ks + ks, [q] + [k] * nband + [v] * nband
    if band.has_bias:
        in_specs.append(pl.BlockSpec((2 * cb, BT_TILES, GRID_W, 128), lambda c, i: (c, 0, 0, 0)))
        args.append(bt)
    n_stats = 2 if raw else 1
    return pl.pallas_call(
        body, name="attn_%s_fwd" % band.name, grid=(ncb // cb, nb // rb), in_specs=in_specs,
        out_specs=[qs] + [st] * n_stats,
        out_shape=[jax.ShapeDtypeStruct((n, w), F32)] + [jax.ShapeDtypeStruct((n, 2 * w), F32)] * n_stats,
        compiler_params=_params("parallel", "arbitrary"),
    )(*args)


def _band_dq(band, q, k, v, do, o, lse, bt=None):
    n, w = q.shape
    nb, ncb, nband = n // BAND, w // 128, 2 * band.hb + 1
    rb, cb = _band_split(nb, ncb)
    mx = _MXU
    per = BAND // GRID_W

    def body(*refs):
        q_ref, k_refs, v_refs = refs[0], refs[1:1 + nband], refs[1 + nband:1 + 2 * nband]
        do_ref, o_ref, lse_ref = refs[1 + 2 * nband:4 + 2 * nband]
        rest = refs[4 + 2 * nband:]
        dq_ref = rest[1] if band.has_bias else rest[0]
        i = pl.program_id(1)
        lo, masks = _head_masks()
        if band.has_bias:
            dbt_ref = rest[2]

            @pl.when(i == 0)
            def _():
                dbt_ref[...] = jnp.zeros_like(dbt_ref)

        subs = [(r, c) for r in range(rb) for c in range(cb)]
        mask = {r: band.mask(_band_idx(band, i * rb + r, 1, 0), _band_idx(band, i * rb + r - band.hb, nband, 1)) for r in range(rb)}
        items, kcats = [], []
        for r, c in subs:
            lanes, rows = slice(c * 128, (c + 1) * 128), slice(r * BAND, (r + 1) * BAND)
            kcats.append(_band_rows(band, k_refs, rb, r, lanes))
            bias = [_band_bias_k(band, rest[0], 2 * c + h) for h in range(2)] if band.has_bias else None
            items.append((q_ref[rows, lanes], kcats[-1], _band_rows(band, v_refs, rb, r, lanes), do_ref[rows, lanes],
                          o_ref[rows, lanes], lse_ref[rows, c * 256:(c + 1) * 256], mask[r], bias))
        res = _p_and_ds(items, masks)
        dqs = [[jnp.dot(ds.astype(mx), kcat, preferred_element_type=F32) for _, _, ds, _ in hs] for hs, kcat in zip(res, kcats)]
        for (r, c), hs, dq in zip(subs, res, dqs):
            dq_ref[r * BAND:(r + 1) * BAND, c * 128:(c + 1) * 128] = jnp.where(lo, dq[0], dq[1])
            if band.has_bias:
                for h in range(2):
                    ds = hs[h][2]
                    for a in range(per):
                        for t in range(nband):
                            tile = ds[a * GRID_W:(a + 1) * GRID_W, t * 128:(t + 1) * 128]
                            dbt_ref[2 * c + h, _bias_tile(t - band.hb, a)] += tile

    qs = pl.BlockSpec((rb * BAND, cb * 128), lambda c, i: (i, c))
    ks = _band_specs(band, rb, cb, nb, 128)
    st = pl.BlockSpec((rb * BAND, cb * 256), lambda c, i: (i, c))
    in_specs, args = [qs] + ks + ks + [qs, qs, st], [q] + [k] * nband + [v] * nband + [do, o, lse]
    out_specs, out_shape = [qs], [jax.ShapeDtypeStruct((n, w), F32)]
    if band.has_bias:
        bts = pl.BlockSpec((2 * cb, BT_TILES, GRID_W, 128), lambda c, i: (c, 0, 0, 0))
        in_specs.append(bts)
        args.append(bt)
        out_specs.append(bts)
        out_shape.append(jax.ShapeDtypeStruct(bt.shape, F32))
    return pl.pallas_call(
        body, name="attn_%s_dq" % band.name, grid=(ncb // cb, nb // rb), in_specs=in_specs, out_specs=out_specs,
        out_shape=out_shape, compiler_params=_params("parallel", "arbitrary"),
    )(*args)


def _band_dkv(band, q, k, v, do, o, lse, bt=None):
    n, w = q.shape
    nb, ncb, nband = n // BAND, w // 128, 2 * band.hb + 1
    rb, cb = _band_split(nb, ncb)
    mx = _MXU

    def body(*refs):
        k_ref, v_ref = refs[0], refs[1]
        q_refs, do_refs, o_refs, lse_refs = [refs[2 + g * nband:2 + (g + 1) * nband] for g in range(4)]
        rest = refs[2 + 4 * nband:]
        dk_ref, dv_ref = rest[-2], rest[-1]
        i = pl.program_id(1)
        _, masks = _head_masks()
        subs = [(r, c) for r in range(rb) for c in range(cb)]
        mask = {r: band.mask(_band_idx(band, i * rb + r - band.hb, nband, 0), _band_idx(band, i * rb + r, 1, 1)) for r in range(rb)}
        items = []
        for r, c in subs:
            lanes, rows = slice(c * 128, (c + 1) * 128), slice(r * BAND, (r + 1) * BAND)
            qcat, docat, ocat = [_band_rows(band, g, rb, r, lanes) for g in (q_refs, do_refs, o_refs)]
            lsecat = _band_rows(band, lse_refs, rb, r, slice(c * 256, (c + 1) * 256))
            bias = [_band_bias_q(band, rest[0], 2 * c + h) for h in range(2)] if band.has_bias else None
            items.append((qcat, k_ref[rows, lanes], v_ref[rows, lanes], docat, ocat, lsecat, mask[r], bias))
        res = _p_and_ds(items, masks)
        dks = [sum(_tdot(ds.astype(mx), qh) for qh, _, ds, _ in hs) for hs in res]
        dvs = [sum(_tdot(p.astype(mx), doh.astype(mx)) for _, p, _, doh in hs) for hs in res]
        for (r, c), dk, dv in zip(subs, dks, dvs):
            dk_ref[r * BAND:(r + 1) * BAND, c * 128:(c + 1) * 128] = dk
            dv_ref[r * BAND:(r + 1) * BAND, c * 128:(c + 1) * 128] = dv

    ks = pl.BlockSpec((rb * BAND, cb * 128), lambda c, i: (i, c))
    in_specs = [ks, ks] + _band_specs(band, rb, cb, nb, 128) * 3 + _band_specs(band, rb, cb, nb, 256)
    args = [k, v] + [q] * nband + [do] * nband + [o] * nband + [lse] * nband
    if band.has_bias:
        in_specs.append(pl.BlockSpec((2 * cb, BT_TILES, GRID_W, 128), lambda c, i: (c, 0, 0, 0)))
        args.append(bt)
    return pl.pallas_call(
        body, name="attn_%s_dkv" % band.name, grid=(ncb // cb, nb // rb), in_specs=in_specs, out_specs=[ks, ks],
        out_shape=[jax.ShapeDtypeStruct((n, w), F32)] * 2,
        compiler_params=_params("parallel", "arbitrary"),
    )(*args)


def _dc_onehot():
    c = np.arange(GRID_W)
    dc = np.clip(c[None, :] - c[:, None] + (C_COLS - 1), 0, 2 * C_COLS - 2).reshape(-1)
    m = np.zeros((GRID_W * GRID_W, 128), np.float32)
    m[np.arange(dc.size), dc] = 1.0
    return m


def _bias_tiles(rpb):
    h, nr, ncol = rpb.shape
    flat = jnp.pad(rpb.reshape(h * nr, ncol), ((0, (-h * nr) % 8), (0, 128 - ncol)))
    tiles = _mm(flat, jnp.asarray(_dc_onehot().T), name="rpb_tiles", exact=True, tn=GRID_W * GRID_W)
    tiles = tiles[:h * nr].reshape(h, nr, GRID_W, GRID_W)
    tiles = jnp.pad(tiles, ((0, 0), (2, BT_TILES + 1 - nr - 2), (0, 0), (0, 0)))
    return jnp.concatenate([tiles[:, :BT_TILES], tiles[:, 1:BT_TILES + 1]], -1)


def _bias_tiles_grad(dbt, nr, ncol):
    h = dbt.shape[0]
    d = dbt[:, 2:2 + nr, :, :GRID_W] + dbt[:, 1:1 + nr, :, GRID_W:]
    flat = jnp.pad(d.reshape(h * nr, GRID_W * GRID_W), ((0, (-h * nr) % 8), (0, 0)))
    g = _mm(flat, jnp.asarray(_dc_onehot()), name="rpb_grad", exact=True, tk=GRID_W * GRID_W)
    return g[:h * nr, :ncol].reshape(h, nr, ncol)


TM = 256
TQ_B, TK_B = 1024, 2048


def _relu2(acc):
    r = jnp.maximum(acc, 0.0)
    return (r * r,)


def _layer_fwd(x, xb, w, sm, tabs, alpha, ride=None):
    tab_a, tab_q, tab_k = tabs
    s, d = x.shape
    ha = _mm(xb, w["in"], name="in_a", tn=QKV_W, b_cols=QKV_W, b_off=0)
    hb = _mm(xb, w["in"], name="in_b", tn=QKV_W, b_cols=QKV_W, b_off=1)
    hc = _mm(xb, w["in"], name="in_c", tn=QKV_W, b_cols=QKV_W, b_off=2)
    hg = _mm(xb, w["in"], name="in_g", outs=(_MXU,), tn=QKV_W, b_cols=3 * d, b_off=3)

    qa, ka, va = _prep_a(ha, tab_a, TM)
    stats = [_band_fwd(_BandA(s // dil // BAND), qa[dil], ka[dil], va[dil]) for dil in A_DILATIONS]
    oas, lse_a = _combine_a(*zip(*stats), s, TM)
    oa = oas[1]

    qb, kd, vd, v1 = _prep_b(hb, sm["q_norm"], sm["k_norm"], tab_q, tab_k, TM)
    ob, lse_b, rode = _flash_fwd(qb, kd, v1, TQ_B, TK_B, ride)

    qc, kc, vc = _prep_c(hc, TM)
    bt = _bias_tiles(sm["rpb"])
    oc, lse_c = _band_fwd(_BandC(s // BAND), qc, kc, vc, bt)

    pa = _mm(oa, w["br_a"], name="br_a", outs=(_MXU,))
    pb = _mm(ob, w["br_b"], name="br_b", outs=(_MXU,))
    pc = _mm(oc, w["br_c"], name="br_c", outs=(_MXU,))
    merged = _gate_merge(hg, sm["b_gate"], pa, pb, pc, TM)
    ln = dict(outs=(F32, F32, _MXU), epilogue=_ln_epilogue(alpha), tm=512, tn=d)
    r1, x1, x1b = _mm(merged, w["out"], name="w_out_ln1", extras=(x, sm["ln1_g"], sm["ln1_b"]), **ln)
    act = _mm(x1b, w["up"], name="w_up", outs=(_MXU,), epilogue=_relu2)
    r2, x2, x2b = _mm(act, w["down"], name="w_down_ln2", extras=(x1, sm["ln2_g"], sm["ln2_b"]), **ln)
    saved = dict(xb=xb, hb=hb, hg=hg, qa=qa, ka=ka, va=va, oa=oa, oas=oas, lse_a=lse_a, qb=qb, kd=kd, vd=vd, ob=ob, lse_b=lse_b,
                 qc=qc, kc=kc, vc=vc, oc=oc, lse_c=lse_c, bt=bt, pa=pa, pb=pb, pc=pc, merged=merged, r1=r1, x1b=x1b,
                 act=act, r2=r2)
    return x2, x2b, saved, rode


def _layer_bwd(dx2, w, sm, sv, tabs, alpha, ride=None):
    tab_a, tab_q, tab_k = tabs
    s, d = dx2.shape
    g = {}
    dr2, dr2b, dg2, db2 = _ln_bwd(dx2, sv["r2"], sm["ln2_g"], "ln2_bwd", TM)
    g["ln2_g"], g["ln2_b"] = dg2.sum(0), db2.sum(0)
    du = _mm(dr2b, w["down"], mode="nt", name="d_act", outs=(_MXU,), extras=(sv["act"],),
             epilogue=lambda acc, act: (acc * (2.0 * jnp.sqrt(act.astype(F32))),))
    g["w_down"] = _mm(sv["act"], dr2b, mode="tn", name="g_w_down", outs=(_MXU,)).reshape(4, -1, d)
    g["w_up"] = _mm(sv["x1b"], du, mode="tn", name="g_w_up", outs=(_MXU,), out_chips=True)
    dx1 = _mm(du, w["up"], mode="nt", name="d_x1", extras=(dr2,), epilogue=lambda acc, e: (acc + alpha * e,))
    dr1, dr1b, dg1, db1 = _ln_bwd(dx1, sv["r1"], sm["ln1_g"], "ln1_bwd", TM)
    g["ln1_g"], g["ln1_b"] = dg1.sum(0), db1.sum(0)
    g["w_out"] = _mm(sv["merged"], dr1b, mode="tn", name="g_w_out", outs=(_MXU,))
    dmerged = _mm(dr1b, w["out"], mode="nt", name="d_merged")
    dpa, dpb, dpc, dlog, gb = _gate_bwd(dmerged, sv["hg"], sm["b_gate"], sv["pa"], sv["pb"], sv["pc"], TM)
    g["b_gate"] = gb.sum(0)
    g["w_branch_a"] = _mm(sv["oa"], dpa, mode="tn", name="g_br_a", outs=(_MXU,))
    g["w_branch_b"] = _mm(sv["ob"], dpb, mode="tn", name="g_br_b", outs=(_MXU,))
    g["w_branch_c"] = _mm(sv["oc"], dpc, mode="tn", name="g_br_c", outs=(_MXU,))
    doa = _mm(dpa, w["br_a"], mode="nt", name="d_oa")
    dob = _mm(dpb, w["br_b"], mode="nt", name="d_ob")
    doc = _mm(dpc, w["br_c"], mode="nt", name="d_oc")

    dqs, dks, dvs = [], [], []
    doas = _to_dilations(doa, "d_oa_layouts", TM)
    for dil in A_DILATIONS:
        band = _BandA(s // dil // BAND)
        args = [t[dil] for t in (sv["qa"], sv["ka"], sv["va"], doas, sv["oas"], sv["lse_a"])]
        dqs.append(_band_dq(band, *args)[0])
        dk_c, dv_c = _band_dkv(band, *args)
        dks.append(dk_c)
        dvs.append(dv_c)
    dha = _post_a(dqs, dks, dvs, tab_a, s, TM)

    dqb, dkd, dvd, rode = _flash_bwd(sv["qb"], sv["kd"], sv["vd"], dob, sv["lse_b"], _delta_b(dob, sv["ob"], TQ_B),
                                     TQ_B, TK_B, ride)
    dhb, gq, gk = _post_b(dqb, dkd, dvd, sv["hb"], sm["q_norm"], sm["k_norm"], tab_q, tab_k, TM)
    g["q_norm_b"] = gq.sum(0).reshape(-1, HEAD).sum(0)
    g["k_norm_b"] = gk.sum(0).reshape(-1, HEAD).sum(0)

    band_c = _BandC(s // BAND)
    cargs = (sv["qc"], sv["kc"], sv["vc"], doc, sv["oc"], sv["lse_c"], sv["bt"])
    dqc, dbt = _band_dq(band_c, *cargs)
    dkc, dvc = _band_dkv(band_c, *cargs)
    dhc = _post_c(dqc, dkc, dvc, TM)
    g["rpb_c"] = _bias_tiles_grad(dbt, 2 * C_ROWS - 1, 2 * C_COLS - 1)

    xb = sv["xb"]
    g["w_in"] = jnp.concatenate([_mm(xb, dh, mode="tn", name="g_in_" + nm, outs=(_MXU,))
                                 for nm, dh in (("a", dha), ("b", dhb), ("c", dhc), ("g", dlog))], 1)
    dx = _mm(dha, w["in"], mode="nt", name="d_x_a", tk=QKV_W, b_cols=QKV_W, b_off=0, extras=(dr1,),
             epilogue=lambda acc, e: (acc + alpha * e,))
    for nm, dh, off in (("b", dhb, 1), ("c", dhc, 2), ("g", dlog, 3)):
        dx = _mm(dh, w["in"], mode="nt", name="d_x_" + nm, tk=QKV_W, b_cols=dh.shape[1], b_off=off, extras=(dx,),
                 epilogue=lambda acc, e: (acc + e,))
    return dx, g, rode


BIG = ("w_in", "w_branch_a", "w_branch_b", "w_branch_c", "w_out", "w_up", "w_down")
ROW_SHARDED = ("w_out", "w_down")
AS_GATHERED = ("w_up", "w_down")
SMALL = ("b_gate", "q_norm_b", "k_norm_b", "rpb_c", "ln1_g", "ln1_b", "ln2_g", "ln2_b")


def _layer_weights(gathered):
    names = dict(w_in="in", w_branch_a="br_a", w_branch_b="br_b", w_branch_c="br_c", w_out="out", w_up="up", w_down="down")
    whole = {n: _full_from_shards(gathered[n], n) for n in names if n not in AS_GATHERED}
    return [{short: (whole[n], l) if n in whole else (gathered[n], l, "rows" if n in ROW_SHARDED else "cols")
             for n, short in names.items()} for l in range(gathered["w_in"].shape[1])]


def _local_step(x, target, gathered, small, rest=None):
    s, d = x.shape
    depth = small["b_gate"].shape[0]
    alpha = (2 * depth) ** 0.25
    tabs = _tables(s)
    ws = _layer_weights(gathered)
    sms = [dict(b_gate=small["b_gate"][l][None], q_norm=jnp.tile(small["q_norm_b"][l], BQ_W // HEAD)[None],
                k_norm=jnp.tile(small["k_norm_b"][l], BKV_W // HEAD)[None], rpb=small["rpb_c"][l],
                ln1_g=small["ln1_g"][l][None], ln1_b=small["ln1_b"][l][None],
                ln2_g=small["ln2_g"][l][None], ln2_b=small["ln2_b"][l][None]) for l in range(depth)]
    saved = []
    h, hb = x, x.astype(_MXU)
    for l in range(depth):
        h, hb, sv, rode = _layer_fwd(h, hb, ws[l], sms[l], tabs, alpha, rest if l == 0 else None)
        if rode:
            ws += _layer_weights(dict(zip(BIG, rode)))
        saved.append(sv)
    sq, dy = _loss_head(h, target, TM)

    def stack(gs):
        out = {k: jnp.stack([gl[k] for gl in gs], 1 if k in AS_GATHERED else 0) for k in gs[0]}
        for n in BIG:
            if n not in AS_GATHERED:
                out[n] = _shards_from_full(out[n], n).astype(_MXU)
        return out

    grads = [None] * depth
    reduced_later = None
    for l in reversed(range(depth)):
        ride = ts = None
        if rest is not None and l == 0 and depth > 1:
            ts = _reduce_pairs([stack(grads[1:])[n] for n in BIG])
            ride = (ts, _chip_out_shapes(ts), CHIP_SEMS, _chip_phases())
        dy, grads[l], rode = _layer_bwd(dy, ws[l], sms[l], saved[l], tabs, alpha, ride)
        if ride:
            reduced_later = _reduce_finish(ts, rode)
    if reduced_later is None:
        return sq, dy, stack(grads)
    first = stack(grads[:1])
    reduced = _reduce_scatter([first[n] for n in BIG])
    out = {n: jnp.concatenate([a, b], 0) for n, a, b in zip(BIG, reduced, reduced_later)}
    out.update({k: jnp.concatenate([first[k], stack(grads[1:])[k]], 0) for k in first if k not in BIG})
    return sq, dy, out


def _place():
    return lax.axis_index("x"), lax.axis_index("y"), lax.axis_index("c")


def _flip(a, b):
    return a + b - 2 * a * b


def _other_chips(x, y):
    return [(1 - x, y), (x, 1 - y), (1 - x, 1 - y)]


def _comm_call(body, name, tensors, out_shapes, n_sems):
    n = len(tensors)

    def wrapped(*refs):
        body(refs[:n], refs[n:2 * n], refs[2 * n], refs[2 * n + 1])

    return pl.pallas_call(
        wrapped, name=name, in_specs=[ANY] * n, out_specs=[ANY] * n,
        out_shape=[jax.ShapeDtypeStruct(s, t.dtype) for s, t in zip(out_shapes, tensors)],
        scratch_shapes=[pltpu.SemaphoreType.DMA((n_sems, n)), pltpu.SemaphoreType.DMA((n_sems, n))],
    )(*tensors)


GATHER_SEMS = 7


def _gather_phases(lo, n_layers):
    def ctx(srcs, outs, send_sems, recv_sems):
        x, y, c = _place()
        n1, n2, dg = (_flip(x, 1 - c), _flip(y, c)), (_flip(x, c), _flip(y, 1 - c)), (1 - x, 1 - y)

        def half(t, chip, hc):
            rh = outs[t].shape[2] // 2
            return outs[t].at[2 * chip[0] + chip[1], :, pl.ds(hc * rh, rh)]

        def copy(k, t, src_ref, dst_ref, to):
            return pltpu.make_async_remote_copy(src_ref=src_ref, dst_ref=dst_ref, send_sem=send_sems.at[k, t],
                                                recv_sem=recv_sems.at[k, t], device_id=to, device_id_type=MESH)

        def sends(t, ks):
            rh = srcs[t].shape[1] // 2
            own, mine = srcs[t].at[pl.ds(lo, n_layers), pl.ds(c * rh, rh)], srcs[t].at[pl.ds(lo, n_layers)]
            table = {0: (own, half(t, (x, y), c), (*n1, c)), 1: (own, half(t, (x, y), c), (*n2, c)),
                     2: (half(t, n1, c), half(t, n1, c), (*n2, c)), 6: (mine, outs[t].at[2 * x + y], (x, y, 1 - c)),
                     "n1": (half(t, n1, c), half(t, n1, c), (x, y, 1 - c)), "n2": (half(t, n2, c), half(t, n2, c), (x, y, 1 - c)),
                     "dg": (half(t, dg, c), half(t, dg, c), (x, y, 1 - c))}
            sem = {0: 0, 1: 1, 2: 2, 6: 6, "n1": 3 + c, "n2": 4 - c, "dg": 5}
            return [copy(sem[k], t, *table[k]) for k in ks]

        def arrived(k, t, chip, hc):
            copy(k, t, half(t, chip, hc), half(t, chip, hc), (x, y, 1 - c)).wait_recv()

        return x, y, c, n1, n2, dg, sends, arrived, range(len(srcs))

    def phase0(*refs):
        *_, sends, _, ts = ctx(*refs)
        for t in ts:
            for cp in sends(t, (0, 1, 6)):
                cp.start()

    def phase1(*refs):
        x, y, c, n1, n2, dg, sends, arrived, ts = ctx(*refs)
        for t in ts:
            arrived(0, t, n1, c)
            for cp in sends(t, (2, "n1")):
                cp.start()
        for t in ts:
            arrived(1, t, n2, c)
            sends(t, ("n2",))[0].start()

    def phase2(*refs):
        x, y, c, n1, n2, dg, sends, arrived, ts = ctx(*refs)
        srcs, outs = refs[0], refs[1]
        for t in ts:
            arrived(2, t, dg, c)
            sends(t, ("dg",))[0].start()
        for t in ts:
            for j, chip in enumerate(_other_chips(x, y)):
                arrived(3 + j, t, chip, 1 - c)
            sends(t, (6,))[0].wait_recv()
            for cp in sends(t, (0, 1, 2, 6, "n1", "n2", "dg")):
                cp.wait_send()

    return [phase0, phase1, phase2]


def _gather_out_shapes(shards, n_layers):
    return [(4, n_layers) + s.shape[1:] for s in shards]


def _gather_shards(shards, lo, n_layers):
    phases = _gather_phases(lo, n_layers)

    def body(*refs):
        for f in phases:
            f(*refs)

    return _comm_call(body, "gather_weights", shards, _gather_out_shapes(shards, n_layers), GATHER_SEMS)


def _pair_exchange(parts):
    def body(srcs, outs, send_sems, recv_sems):
        x, y, c = _place()
        cps = [pltpu.make_async_remote_copy(src_ref=src.at[:, :, pl.ds((1 - c) * (src.shape[2] // 2), src.shape[2] // 2)],
                                            dst_ref=out, send_sem=send_sems.at[0, t], recv_sem=recv_sems.at[0, t],
                                            device_id=(x, y, 1 - c), device_id_type=MESH)
               for t, (src, out) in enumerate(zip(srcs, outs))]
        for cp in cps:
            cp.start()
        for cp in cps:
            cp.wait()

    return _comm_call(body, "grad_pair_exchange", parts, [p.shape[:2] + (p.shape[2] // 2, p.shape[3]) for p in parts], 1)


CHIP_SEMS = 3


def _chip_phases():
    def copies(srcs, outs, send_sems, recv_sems):
        x, y, c = _place()
        return [pltpu.make_async_remote_copy(src_ref=src.at[2 * chip[0] + chip[1]], dst_ref=out.at[k], send_sem=send_sems.at[k, t],
                                             recv_sem=recv_sems.at[k, t], device_id=(*chip, c), device_id_type=MESH)
                for t, (src, out) in enumerate(zip(srcs, outs)) for k, chip in enumerate(_other_chips(x, y))]

    def start(*refs):
        for cp in copies(*refs):
            cp.start()

    def wait(*refs):
        for cp in copies(*refs):
            cp.wait()

    return [start, lambda *refs: None, wait]


def _chip_out_shapes(ts):
    return [(3,) + t.shape[1:] for t in ts]


def _chip_exchange(ts):
    phases = _chip_phases()

    def body(*refs):
        for f in phases:
            f(*refs)

    return _comm_call(body, "grad_chip_exchange", ts, _chip_out_shapes(ts), CHIP_SEMS)


def _pair_share(halves):
    def body(srcs, outs, send_sems, recv_sems):
        x, y, c = _place()
        cps = [pltpu.make_async_remote_copy(src_ref=src, dst_ref=out, send_sem=send_sems.at[0, t], recv_sem=recv_sems.at[0, t],
                                            device_id=(x, y, 1 - c), device_id_type=MESH)
               for t, (src, out) in enumerate(zip(srcs, outs))]
        for cp in cps:
            cp.start()
        for cp in cps:
            cp.wait()

    theirs = _comm_call(body, "grad_pair_share", halves, [h.shape for h in halves], 1)
    c = jnp.reshape(lax.axis_index("c"), (1,)).astype(jnp.int32)
    return [_join_halves(mine, other, c, "grad_pair_join_%d" % t) for t, (mine, other) in enumerate(zip(halves, theirs))]


def _rows_view(t, lead):
    return t.reshape(t.shape[:lead] + (-1, t.shape[-1]))


def _join_halves(mine, theirs, c, name):
    n, rh, cols = mine.shape
    tr = _tile(rh, 1024, 8)

    def join(c_ref, mine_ref, theirs_ref, o_ref):
        o_ref[...] = jnp.where(pl.program_id(1) == c_ref[0], mine_ref[...], theirs_ref[...])

    spec = pl.BlockSpec((None, tr, cols), lambda l, h, i, c_ref: (l, i, 0))
    return pl.pallas_call(
        join, name=name,
        grid_spec=pltpu.PrefetchScalarGridSpec(
            num_scalar_prefetch=1, grid=(n, 2, rh // tr), in_specs=[spec, spec],
            out_specs=pl.BlockSpec((None, tr, cols), lambda l, h, i, c_ref: (l, h * (rh // tr) + i, 0))),
        out_shape=jax.ShapeDtypeStruct((n, 2 * rh, cols), mine.dtype),
        compiler_params=_params("parallel", "parallel", "parallel"),
    )(c, mine, theirs)


def _gather_all(v):
    r = v.shape[0]

    def body(src, out, send_sems, recv_sems, local_sem):
        x, y, c = _place()
        me = 4 * x + 2 * y + c
        mine = pltpu.make_async_copy(src, out.at[me], local_sem)
        mine.start()
        cps = []
        for k in range(1, 8):
            fx, fy, fc = (k >> 2) & 1, (k >> 1) & 1, k & 1
            peer = (x + fx - 2 * x * fx, y + fy - 2 * y * fy, c + fc - 2 * c * fc)
            cps.append(pltpu.make_async_remote_copy(src_ref=src, dst_ref=out.at[me], send_sem=send_sems.at[k - 1],
                                                    recv_sem=recv_sems.at[k - 1], device_id=peer, device_id_type=MESH))
        for cp in cps:
            cp.start()
        for k in range(1, 8):
            fx, fy, fc = (k >> 2) & 1, (k >> 1) & 1, k & 1
            frm = 4 * (x + fx - 2 * x * fx) + 2 * (y + fy - 2 * y * fy) + (c + fc - 2 * c * fc)
            pltpu.make_async_remote_copy(src_ref=src, dst_ref=out.at[frm], send_sem=send_sems.at[k - 1],
                                         recv_sem=recv_sems.at[k - 1], device_id=(x, y, c), device_id_type=MESH).wait_recv()
        for cp in cps:
            cp.wait_send()
        mine.wait()

    return pl.pallas_call(
        body, name="gather_small_grads", in_specs=[ANY], out_specs=ANY,
        out_shape=jax.ShapeDtypeStruct((8, r, 128), v.dtype),
        scratch_shapes=[pltpu.SemaphoreType.DMA((7,)), pltpu.SemaphoreType.DMA((7,)), pltpu.SemaphoreType.DMA],
    )(v)


def _sum_slots(parts, name):
    n, r, _ = parts.shape
    tr = _tile(r, 1024, 8)

    def body(p_ref, o_ref):
        acc = p_ref[0]
        for j in range(1, n):
            acc = acc + p_ref[j]
        o_ref[...] = acc

    return pl.pallas_call(
        body, name=name, grid=(r // tr,), in_specs=[pl.BlockSpec((n, tr, 128), lambda i: (0, i, 0))],
        out_specs=pl.BlockSpec((tr, 128), lambda i: (i, 0)), out_shape=jax.ShapeDtypeStruct((r, 128), parts.dtype),
        compiler_params=_params("parallel"),
    )(parts)


def _add_sibling_half(part, recv, c, name):
    _, n, rh, cols = recv.shape
    tr = _tile(rh, 1024, 16)
    nblk = rh // tr

    def body(c_ref, p_ref, r_ref, o_ref):
        o_ref[...] = (p_ref[...].astype(F32) + r_ref[...].astype(F32)).astype(o_ref.dtype)

    blk = (None, None, tr, cols)
    return pl.pallas_call(
        body, name=name,
        grid_spec=pltpu.PrefetchScalarGridSpec(
            num_scalar_prefetch=1, grid=(4, n, nblk),
            in_specs=[pl.BlockSpec(blk, lambda j, l, i, c_ref: (j, l, c_ref[0] * nblk + i, 0)),
                      pl.BlockSpec(blk, lambda j, l, i, c_ref: (j, l, i, 0))],
            out_specs=pl.BlockSpec(blk, lambda j, l, i, c_ref: (j, l, i, 0))),
        out_shape=jax.ShapeDtypeStruct(recv.shape, recv.dtype),
        compiler_params=_params("parallel", "parallel", "parallel"),
    )(c, part, recv)


def _add_chips(t, recv, me, name):
    shape = t.shape[1:]
    t, recv = _rows_view(t, 1), _rows_view(recv, 1)
    _, rh, cols = t.shape
    tr = _tile(rh, 1024, 16)

    def body(me_ref, t_ref, r_ref, o_ref):
        f = lambda v: v.astype(F32)
        o_ref[...] = ((f(t_ref[...]) + f(r_ref[0])) + f(r_ref[1])) + f(r_ref[2])

    return pl.pallas_call(
        body, name=name,
        grid_spec=pltpu.PrefetchScalarGridSpec(
            num_scalar_prefetch=1, grid=(rh // tr,),
            in_specs=[pl.BlockSpec((None, tr, cols), lambda i, me_ref: (me_ref[0], i, 0)),
                      pl.BlockSpec((3, tr, cols), lambda i, me_ref: (0, i, 0))],
            out_specs=pl.BlockSpec((tr, cols), lambda i, me_ref: (i, 0))),
        out_shape=jax.ShapeDtypeStruct((rh, cols), F32),
        compiler_params=_params("parallel"),
    )(me, t, recv).reshape(shape)


def _reduce_pairs(parts):
    c1 = jnp.reshape(lax.axis_index("c"), (1,)).astype(jnp.int32)
    return [_add_sibling_half(p, r, c1, "grad_pair_sum_%d" % i) for i, (p, r) in enumerate(zip(parts, _pair_exchange(parts)))]


def _reduce_finish(ts, recv):
    me = jnp.reshape(2 * lax.axis_index("x") + lax.axis_index("y"), (1,)).astype(jnp.int32)
    return _pair_share([_add_chips(t, r, me, "grad_chip_sum_%d" % i) for i, (t, r) in enumerate(zip(ts, recv))])


def _reduce_scatter(parts):
    ts = _reduce_pairs(parts)
    return _reduce_finish(ts, _chip_exchange(ts))


def _to_rows(parts, mult):
    flat = jnp.concatenate([p.reshape(-1) for p in parts])
    flat = jnp.pad(flat, (0, (-flat.size) % (128 * mult)))
    return flat.reshape(-1, 128)


def _from_rows(rows, shapes):
    flat, out, at = rows.reshape(-1), [], 0
    for shp in shapes:
        n = int(np.prod(shp))
        out.append(flat[at:at + n].reshape(shp))
        at += n
    return out


def _full_from_shards(g, name):
    _, depth, rows, cols = g.shape
    if name in ROW_SHARDED:
        return jnp.moveaxis(g, 0, 1).reshape(depth, 4 * rows, cols)
    return jnp.moveaxis(g, 0, 2).reshape(depth, rows, 4 * cols)


def _shards_from_full(full, name):
    depth, rows, cols = full.shape
    if name in ROW_SHARDED:
        return jnp.moveaxis(full.reshape(depth, 4, rows // 4, cols), 1, 0)
    return jnp.moveaxis(full.reshape(depth, rows, 4, cols // 4), 2, 0)


def kernel(x, w_in, b_gate, q_norm_b, k_norm_b, rpb_c, w_branch_a, w_branch_b, w_branch_c, w_out, ln1_g, ln1_b, w_up, w_down, ln2_g, ln2_b, loss_target, m_w_in, m_b_gate, m_q_norm_b, m_k_norm_b, m_rpb_c, m_w_branch_a, m_w_branch_b, m_w_branch_c, m_w_out, m_ln1_g, m_ln1_b, m_w_up, m_w_down, m_ln2_g, m_ln2_b, v_w_in, v_b_gate, v_q_norm_b, v_k_norm_b, v_rpb_c, v_w_branch_a, v_w_branch_b, v_w_branch_c, v_w_out, v_ln1_g, v_ln1_b, v_w_up, v_w_down, v_ln2_g, v_ln2_b):
    args = dict(locals())
    big_shard = {n: args[n] for n in BIG}
    small = {n: args[n] for n in SMALL}

    shards = [big_shard[n].astype(_MXU) for n in BIG]
    later = shards[0].shape[0] - 1
    first = dict(zip(BIG, _gather_shards(shards, 0, 1)))
    rest = (shards, _gather_out_shapes(shards, later), GATHER_SEMS, _gather_phases(1, later)) if later else None

    sq, grad_x, grads = _local_step(x[0], loss_target[0], first, small, rest)
    loss = lax.psum(0.5 * jnp.sum(sq) / x.shape[-1], ("x", "y", "c"))

    g_big = [grads[n] for n in BIG] if rest else _reduce_scatter([grads[n] for n in BIG])
    small_shapes = [small[n].shape for n in SMALL]
    g_small = _from_rows(_sum_slots(_gather_all(_to_rows([grads[n] for n in SMALL], 8)), "small_grad_sum"), small_shapes)
    grad = dict(zip(BIG, g_big))
    grad.update(zip(SMALL, g_small))

    delta, new_m, new_v = {}, {}, {}
    for n in BIG:
        shp = big_shard[n].shape
        two_d = lambda t: t.reshape(-1, shp[-1])
        res = _adamw(two_d(big_shard[n]), two_d(grad[n]), two_d(args["m_" + n]), two_d(args["v_" + n]), "adamw_" + n)
        delta[n], new_m[n], new_v[n] = [t.reshape(shp) for t in res]
    packed = [_to_rows([args[pre + n] for n in SMALL], 8) for pre in ("", "m_", "v_")]
    res = _adamw(packed[0], _to_rows([grad[n] for n in SMALL], 8), packed[1], packed[2], "adamw_small")
    for dst, rows in zip((delta, new_m, new_v), res):
        dst.update(zip(SMALL, _from_rows(rows, small_shapes)))

    order = ("w_in", "b_gate", "q_norm_b", "k_norm_b", "rpb_c", "w_branch_a", "w_branch_b", "w_branch_c", "w_out",
             "ln1_g", "ln1_b", "w_up", "w_down", "ln2_g", "ln2_b")
    return (loss, grad_x[None], *[grad[n] for n in order], *[delta[n] for n in order],
            *[new_m[n] for n in order], *[new_v[n] for n in order])
```

```python
import functools

import numpy as np
import jax
import jax.numpy as jnp
from jax import lax
from jax.experimental import pallas as pl
from jax.experimental.pallas import tpu as pltpu

F32 = jnp.float32
_MXU = jnp.bfloat16

HEAD = 64
A_W, BQ_W, BKV_W, C_W = 256, 512, 128, 256
QKV_W = 768
A_DILATIONS = (1, 4, 16)
A_RADIUS = 64
A_ROPE_HALF = 8
AX_ROPE_HALF = 16
ROPE_THETA = 500000.0
AX_THETA = 10000.0
GRID_W = 64
C_ROWS = 8
C_COLS = 16
BAND = 128
BT_TILES = 18
LN_EPS = 1e-5
RMS_EPS = 1e-6
NEG = -1e30
SCALE = HEAD ** -0.5
LOG2E = 1.4426950408889634
LN2 = 0.6931471805599453
ADAM_LR, ADAM_B1, ADAM_B2, ADAM_EPS, ADAM_WD, ADAM_STEP = 0.001, 0.9, 0.999, 1e-08, 0.01, 10
V7X_VMEM_LIMIT = 48 * 1024 * 1024
MESH = pl.DeviceIdType.MESH
ANY = pl.BlockSpec(memory_space=pl.ANY)


def _params(*sem):
    return pltpu.CompilerParams(dimension_semantics=sem or None, vmem_limit_bytes=V7X_VMEM_LIMIT)


def _tile(n, pref, align=128):
    if n <= pref:
        return n
    t = (pref // align) * align
    while t >= align:
        if n % t == 0:
            return t
        t -= align
    return n


def _mm(a, b, *, name, mode="nn", outs=((F32),), epilogue=None, extras=(), tm=1024, tn=1024, tk=2048, exact=False,
        b_cols=None, b_off=0, out_chips=False):
    b, b_lead, b_axis = (tuple(b) + (None, None))[:3] if isinstance(b, tuple) else (b, None, None)
    m, k = a.shape if mode != "tn" else a.shape[::-1]
    b_rows = b.shape[-2] * (4 if b_axis == "rows" else 1)
    b_last = b_cols or b.shape[-1] * (4 if b_axis == "cols" else 1)
    k2, n = (b_rows, b_last) if mode != "nt" else (b_last, b_rows)
    assert k == k2, (a.shape, b.shape, mode)
    cap_rows = b.shape[-2] if b_axis == "rows" else None
    cap_cols = b.shape[-1] if b_axis == "cols" else (n // 4 if out_chips else None)
    cap_n, cap_k = (cap_cols, cap_rows) if mode != "nt" else (cap_rows, cap_cols)
    tm, tn, tk = _tile(m, tm), _tile(cap_n or n, min(tn, cap_n or tn)), _tile(cap_k or k, min(tk, cap_k or tk))
    nk = k // tk
    n_ex, n_out = len(extras), len(outs)
    mx = F32 if exact else _MXU
    prec = lax.Precision.HIGHEST if exact else None
    dims = {"nn": (((1,), (0,)), ((), ())), "nt": (((1,), (1,)), ((), ())), "tn": (((0,), (0,)), ((), ()))}[mode]

    def body(*refs):
        a_ref, b_ref = refs[0], refs[1]
        ex = refs[2:2 + n_ex]
        out_refs = refs[2 + n_ex:2 + n_ex + n_out]
        kk = pl.program_id(2)
        av, bv = a_ref[...].astype(mx), b_ref[...].astype(mx)
        part = lax.dot_general(av, bv, dims, preferred_element_type=F32, precision=prec)

        def finish(res):
            vals = epilogue(res, *[e[...] for e in ex]) if epilogue is not None else (res,)
            for o, v in zip(out_refs, vals):
                o[...] = v.astype(o.dtype)

        if nk == 1:
            finish(part)
        else:
            acc = refs[-1]

            @pl.when(kk == 0)
            def _():
                acc[...] = part

            @pl.when((kk > 0) & (kk < nk - 1))
            def _():
                acc[...] += part

            @pl.when(kk == nk - 1)
            def _():
                finish(acc[...] + part)

    a_spec = pl.BlockSpec((tm, tk), lambda i, j, kk: (i, kk)) if mode != "tn" else pl.BlockSpec((tk, tm), lambda i, j, kk: (kk, i))
    b_tile = (tn, tk) if mode == "nt" else (tk, tn)

    def b_index(i, j, kk):
        rc = [j, kk + b_off] if mode == "nt" else [kk, j + b_off]
        if b_axis is None:
            return (() if b_lead is None else (b_lead,)) + tuple(rc)
        ax = 0 if b_axis == "rows" else 1
        per = b.shape[-2 + ax] // b_tile[ax]
        chip, rc[ax] = rc[ax] // per, rc[ax] % per
        return (chip, b_lead) + tuple(rc)

    b_spec = pl.BlockSpec((None,) * (b.ndim - 2) + b_tile, b_index)
    o_spec = pl.BlockSpec((tm, tn), lambda i, j, kk: (i, j))
    if out_chips:
        per_out = n // 4 // tn
        out_specs = [pl.BlockSpec((None, tm, tn), lambda i, j, kk: (j // per_out, i, j % per_out))] * n_out
        out_shape = [jax.ShapeDtypeStruct((4, m, n // 4), d) for d in outs]
    else:
        out_specs, out_shape = [o_spec] * n_out, [jax.ShapeDtypeStruct((m, n), d) for d in outs]
    res = pl.pallas_call(
        body, name=name, grid=(m // tm, n // tn, nk),
        in_specs=[a_spec, b_spec] + [o_spec if e.shape[0] > 1 else pl.BlockSpec((1, tn), lambda i, j, kk: (0, j)) for e in extras],
        out_specs=out_specs, out_shape=out_shape,
        scratch_shapes=[pltpu.VMEM((tm, tn), F32)] if nk > 1 else [],
        compiler_params=_params("parallel", "parallel", "arbitrary"),
    )(a, b, *extras)
    return res[0] if n_out == 1 else res


def _rows(tm, width, cb=0):
    return pl.BlockSpec((tm, width), lambda t: (t, cb))


def _whole(arr):
    nd = arr.ndim
    return pl.BlockSpec(arr.shape, lambda t: (0,) * nd)


def _rowwise(fn, name, rows, tm, ins, outs, into=None):
    into = into or {}
    n_in, n_out, n_into = len(ins), len(outs), len(into)
    dil_in = [spec[1:] if isinstance(spec, tuple) else None for _, spec in ins]
    in_specs = [_rows(tm // spec[1], spec[1] * spec[2]) if isinstance(spec, tuple) else spec for _, spec in ins]
    scratch = [pltpu.VMEM((di[1] // 128, tm, 128), F32) for di in dil_in if di] + \
              [pltpu.VMEM((n // 128, tm, 128), F32) for n, _, kind in outs if isinstance(kind, int)]

    def body(*refs):
        scr = list(refs[n_in + n_into + n_out:])
        blocks = []
        for r, di in zip(refs[:n_in], dil_in):
            if di is None:
                blocks.append(r[...])
            else:
                d, n = di
                s_ref = scr.pop(0)
                for j in range(d):
                    for b in range(n // 128):
                        lanes = slice(j * n + b * 128, j * n + (b + 1) * 128)
                        s_ref.at[b][pl.ds(j, tm // d, stride=d), :] = r[:, lanes].astype(F32)
                blocks.append(jnp.concatenate([s_ref[b] for b in range(n // 128)], 1))
        vals = fn(*blocks)
        first = pl.program_id(0) == 0
        for (ncols, _, kind), o, v in zip(outs, refs[n_in + n_into:n_in + n_into + n_out], vals):
            if kind == "row" or isinstance(kind, tuple):
                o[...] = v.astype(o.dtype)
            elif isinstance(kind, int):
                s_ref = scr.pop(0)
                for b in range(ncols // 128):
                    s_ref[b] = v[:, b * 128:(b + 1) * 128].astype(F32)
                for j in range(kind):
                    for b in range(ncols // 128):
                        lanes = slice(j * ncols + b * 128, j * ncols + (b + 1) * 128)
                        o[:, lanes] = s_ref.at[b][pl.ds(j, tm // kind, stride=kind), :].astype(o.dtype)
            else:
                part = v.reshape(tm // 8, 8, ncols).sum(0)

                @pl.when(first)
                def _(o=o, part=part):
                    o[...] = part

                @pl.when(jnp.logical_not(first))
                def _(o=o, part=part):
                    o[...] += part

    def out_spec(n, kind):
        if kind == "row":
            return _rows(tm, n), (rows, n)
        if isinstance(kind, int):
            return _rows(tm // kind, kind * n), (rows // kind, kind * n)
        if isinstance(kind, tuple):
            return _rows(tm, n, kind[2]), (rows, kind[1])
        return pl.BlockSpec((8, n), lambda t: (0, 0)), (8, n)

    specs = [out_spec(n, kind) for n, _, kind in outs]
    res = pl.pallas_call(
        body, name=name, grid=(rows // tm,),
        in_specs=in_specs + [ANY] * n_into, out_specs=[s for s, _ in specs],
        out_shape=[jax.ShapeDtypeStruct(shp, d) for (_, shp), (_, d, _) in zip(specs, outs)],
        input_output_aliases={n_in + e: o for e, o in enumerate(into)},
        scratch_shapes=scratch, compiler_params=_params("arbitrary"),
    )(*[a for a, _ in ins], *into.values())
    return res


def _lane_lo(width=128):
    return (lax.broadcasted_iota(jnp.int32, (1, width), 1) & (HEAD * 2 - 1)) < HEAD


def _group_sum(x):
    w = x.shape[-1]
    sh = HEAD.bit_length() - 1
    same = (lax.broadcasted_iota(jnp.int32, (w, w), 0) >> sh) == (lax.broadcasted_iota(jnp.int32, (w, w), 1) >> sh)
    ones = jnp.where(same, 1.0, 0.0).astype(jnp.bfloat16)
    hi = x.astype(jnp.bfloat16)
    lo = (x - hi.astype(F32)).astype(jnp.bfloat16)
    return jnp.dot(hi, ones, preferred_element_type=F32) + jnp.dot(lo, ones, preferred_element_type=F32)


def _rot(x, c, sm, sp, shift):
    w = x.shape[-1]
    return x * c + pltpu.roll(x, w - shift, 1) * sm + pltpu.roll(x, shift, 1) * sp


def _rot_t(dy, c, sm, sp, shift):
    w = dy.shape[-1]
    return dy * c + pltpu.roll(dy * sm, shift, 1) + pltpu.roll(dy * sp, w - shift, 1)


def _rope_tables(pos_parts, half, thetas):
    cs, sms, sps = [], [], []
    for pos, theta in zip(pos_parts, thetas):
        inv = theta ** (-jnp.arange(half, dtype=F32) / half)
        ang = pos.astype(F32)[:, None] * inv[None, :]
        co, si, ze = jnp.cos(ang), jnp.sin(ang), jnp.zeros_like(ang)
        cs += [co, co]
        sms += [-si, ze]
        sps += [ze, si]
    return [jnp.concatenate(t, axis=1) for t in (cs, sms, sps)]


def _tables(s):
    pos = jnp.arange(s)
    ca, sma, spa = _rope_tables([pos], A_ROPE_HALF, [ROPE_THETA])
    pad = HEAD - 2 * A_ROPE_HALF
    ca = jnp.concatenate([ca, jnp.ones((s, pad), F32)], 1)
    sma, spa = [jnp.concatenate([t, jnp.zeros((s, pad), F32)], 1) for t in (sma, spa)]
    tab_a = [jnp.tile(t, (1, A_W // HEAD)) for t in (ca, sma, spa)]
    ax = _rope_tables([pos // GRID_W, pos % GRID_W], AX_ROPE_HALF, [AX_THETA, AX_THETA])
    tab_q = [jnp.tile(t, (1, BQ_W // HEAD)) for t in ax]
    tab_k = [jnp.tile(t, (1, BKV_W // HEAD)) for t in ax]
    return tab_a, tab_q, tab_k


def _prep_a(ha, tab, tm):
    s = ha.shape[0]

    def fn(h, c, sm, sp):
        q, k, v = h[:, :A_W], h[:, A_W:2 * A_W], h[:, 2 * A_W:]
        q, k = _rot(q, c, sm, sp, A_ROPE_HALF) * SCALE, _rot(k, c, sm, sp, A_ROPE_HALF)
        return [t for t in (q, k, v) for _ in A_DILATIONS]

    res = _rowwise(fn, "prep_a", s, tm, [(ha, _rows(tm, QKV_W))] + [(t, _rows(tm, A_W)) for t in tab],
                   [(A_W, _MXU, _dil_kind(d)) for _ in range(3) for d in A_DILATIONS])
    n = len(A_DILATIONS)
    return [dict(zip(A_DILATIONS, res[i * n:(i + 1) * n])) for i in range(3)]


def _dil_kind(d):
    return "row" if d == 1 else d


def _dil_spec(d, tm, ncols):
    return _rows(tm, ncols) if d == 1 else ("dil", d, ncols)


def _to_dilations(x, name, tm):
    s, n = x.shape
    res = _rowwise(lambda v: [v for d in A_DILATIONS if d > 1], name, s, tm, [(x, _rows(tm, n))],
                   [(n, x.dtype, d) for d in A_DILATIONS if d > 1])
    return {1: x, **dict(zip([d for d in A_DILATIONS if d > 1], res))}


def _rms(x, g):
    ms = _group_sum(x * x) * (1.0 / HEAD)
    return x * lax.rsqrt(ms + RMS_EPS) * g


def _prep_b(hb, gq, gk, tab_q, tab_k, tm):
    s = hb.shape[0]

    def fn(h, gq, gk, cq, smq, spq, ck, smk, spk):
        xq, xk, v = h[:, :BQ_W], h[:, BQ_W:BQ_W + BKV_W], h[:, BQ_W + BKV_W:]
        q = _rot(_rms(xq, gq), cq, smq, spq, AX_ROPE_HALF) * (SCALE * LOG2E)
        k = _rot(_rms(xk, gk), ck, smk, spk, AX_ROPE_HALF)
        lo = _lane_lo()
        kr, vr = pltpu.roll(k, HEAD, 1), pltpu.roll(v, HEAD, 1)
        kd = jnp.concatenate([jnp.where(lo, k, kr), jnp.where(lo, kr, k)], 1)
        vd = jnp.concatenate([jnp.where(lo, v, vr), jnp.where(lo, vr, v)], 1)
        v1 = jnp.concatenate([jnp.where(lo, v, 1.0), jnp.where(lo, vr, 1.0)], 1)
        return q, kd, vd, v1

    ins = [(hb, _rows(tm, QKV_W)), (gq, _whole(gq)), (gk, _whole(gk))]
    ins += [(t, _rows(tm, BQ_W)) for t in tab_q] + [(t, _rows(tm, BKV_W)) for t in tab_k]
    return _rowwise(fn, "prep_b", s, tm, ins, [(BQ_W, _MXU, "row")] + [(2 * BKV_W, _MXU, "row")] * 3)


def _prep_c(hc, tm):
    def fn(h):
        return h[:, :C_W] * SCALE, h[:, C_W:2 * C_W], h[:, 2 * C_W:]

    return _rowwise(fn, "prep_c", hc.shape[0], tm, [(hc, _rows(tm, QKV_W))], [(C_W, _MXU, "row")] * 3)


def _combine_a(os_, ms, ls, s, tm):
    def fn(o1, o2, o3, m1, m2, m3, l1, l2, l3):
        lo = _lane_lo()
        outs, lses = [], []
        for p in range(A_W // 128):
            st = slice(p * 256, (p + 1) * 256)
            mm = [m[:, st] for m in (m1, m2, m3)]
            ll = [l[:, st] for l in (l1, l2, l3)]
            mmax = jnp.maximum(jnp.maximum(mm[0], mm[1]), mm[2])
            ws = [jnp.exp(m - mmax) for m in mm]
            den = ws[0] * ll[0] + ws[1] * ll[1] + ws[2] * ll[2]
            lses.append(mmax + jnp.log(den))
            num = sum(jnp.where(lo, w[:, :128], w[:, 128:]) * o[:, p * 128:(p + 1) * 128] for w, o in zip(ws, (o1, o2, o3)))
            outs.append(num / jnp.where(lo, den[:, :128], den[:, 128:]))
        o, lse = jnp.concatenate(outs, 1), jnp.concatenate(lses, 1)
        return [o] * len(A_DILATIONS) + [lse] * len(A_DILATIONS)

    ins = [(t, _dil_spec(d, tm, w)) for ts, w in ((os_, A_W), (ms, 2 * A_W), (ls, 2 * A_W)) for t, d in zip(ts, A_DILATIONS)]
    res = _rowwise(fn, "combine_a", s, tm, ins,
                   [(w, F32, _dil_kind(d)) for w in (A_W, 2 * A_W) for d in A_DILATIONS])
    n = len(A_DILATIONS)
    return dict(zip(A_DILATIONS, res[:n])), dict(zip(A_DILATIONS, res[n:]))


def _gates(hg, bg, d):
    return [jax.nn.sigmoid(hg[:, i * d:(i + 1) * d] + bg[:, i * d:(i + 1) * d]) for i in range(3)]


def _gate_merge(hg, bg, pa, pb, pc, tm):
    s, d = pa.shape

    def fn(hg, bg, pa, pb, pc):
        g = _gates(hg, bg, d)
        return (g[0] * pa + g[1] * pb + g[2] * pc,)

    ins = [(hg, _rows(tm, 3 * d)), (bg, _whole(bg))] + [(p, _rows(tm, d)) for p in (pa, pb, pc)]
    return _rowwise(fn, "gate_merge", s, tm, ins, [(d, _MXU, "row")])[0]


def _gate_bwd(dm, hg, bg, pa, pb, pc, tm):
    s, d = pa.shape

    def fn(dm, hg, bg, pa, pb, pc):
        g = _gates(hg, bg, d)
        dlog = jnp.concatenate([dm * p * gi * (1.0 - gi) for p, gi in zip((pa, pb, pc), g)], 1)
        return dm * g[0], dm * g[1], dm * g[2], dlog, dlog

    ins = [(dm, _rows(tm, d)), (hg, _rows(tm, 3 * d)), (bg, _whole(bg))] + [(p, _rows(tm, d)) for p in (pa, pb, pc)]
    return _rowwise(fn, "gate_bwd", s, tm, ins, [(d, _MXU, "row")] * 3 + [(3 * d, _MXU, "row"), (3 * d, F32, "acc")])


def _ln_stats(r):
    mu = jnp.mean(r, -1, keepdims=True)
    xc = r - mu
    var = jnp.mean(xc * xc, -1, keepdims=True)
    rstd = lax.rsqrt(var + LN_EPS)
    return xc * rstd, rstd


def _ln_epilogue(alpha):
    def fn(br, x, g, b):
        r = alpha * x + br
        xhat, _ = _ln_stats(r)
        y = xhat * g + b
        return r, y, y

    return fn


def _ln_bwd(dy, r, g, name, tm):
    s, d = r.shape

    def fn(dy, r, g):
        xhat, rstd = _ln_stats(r)
        dxh = dy * g
        dr = rstd * (dxh - jnp.mean(dxh, -1, keepdims=True) - xhat * jnp.mean(dxh * xhat, -1, keepdims=True))
        return dr, dr, dy * xhat, dy

    ins = [(dy, _rows(tm, d)), (r, _rows(tm, d)), (g, _whole(g))]
    return _rowwise(fn, name, s, tm, ins, [(d, F32, "row"), (d, _MXU, "row"), (d, F32, "acc"), (d, F32, "acc")])


def _loss_head(y, target, tm):
    s, d = y.shape

    def fn(y, t):
        diff = y - t
        return diff * diff, diff * (1.0 / d)

    sq, dy = _rowwise(fn, "loss_head", s, tm, [(y, _rows(tm, d)), (target, _rows(tm, d))], [(d, F32, "acc"), (d, F32, "row")])
    return sq, dy


def _post_a(dqs, dks, dvs, tab, s, tm):
    def fn(q1, q2, q3, k1, k2, k3, v1, v2, v3, c, sm, sp):
        dq = _rot_t((q1 + q2 + q3) * SCALE, c, sm, sp, A_ROPE_HALF)
        dk = _rot_t(k1 + k2 + k3, c, sm, sp, A_ROPE_HALF)
        return (jnp.concatenate([dq, dk, v1 + v2 + v3], 1),)

    ins = [(t, _dil_spec(d, tm, A_W)) for ts in (dqs, dks, dvs) for t, d in zip(ts, A_DILATIONS)]
    ins += [(t, _rows(tm, A_W)) for t in tab]
    return _rowwise(fn, "post_a", s, tm, ins, [(QKV_W, _MXU, ("slot", 3 * QKV_W, 0))])[0]


def _post_b(dq, dkd, dvd, hb, gq, gk, tab_q, tab_k, tm, dh):
    s = dq.shape[0]

    def back(dz, x, g, c, sm, sp):
        dy = _rot_t(dz, c, sm, sp, AX_ROPE_HALF)
        rstd = lax.rsqrt(_group_sum(x * x) * (1.0 / HEAD) + RMS_EPS)
        xh = x * rstd
        dxh = dy * g
        return rstd * (dxh - xh * (_group_sum(dxh * xh) * (1.0 / HEAD))), dy * xh

    def fn(dq, dkd, dvd, h, gq, gk, cq, smq, spq, ck, smk, spk):
        lo = _lane_lo()
        dk = jnp.where(lo, dkd[:, :128], dkd[:, 128:])
        dv = jnp.where(lo, dvd[:, :128], dvd[:, 128:])
        dxq, dgq = back(dq * SCALE, h[:, :BQ_W], gq, cq, smq, spq)
        dxk, dgk = back(dk, h[:, BQ_W:BQ_W + BKV_W], gk, ck, smk, spk)
        return jnp.concatenate([dxq, dxk, dv], 1), dgq, dgk

    ins = [(dq, _rows(tm, BQ_W)), (dkd, _rows(tm, 2 * BKV_W)), (dvd, _rows(tm, 2 * BKV_W)), (hb, _rows(tm, QKV_W)),
           (gq, _whole(gq)), (gk, _whole(gk))]
    ins += [(t, _rows(tm, BQ_W)) for t in tab_q] + [(t, _rows(tm, BKV_W)) for t in tab_k]
    return _rowwise(fn, "post_b", s, tm, ins, [(QKV_W, _MXU, ("slot", 3 * QKV_W, 1)), (BQ_W, F32, "acc"), (BKV_W, F32, "acc")],
                    into={0: dh})


def _post_c(dq, dk, dv, tm, dh):
    def fn(dq, dk, dv):
        return (jnp.concatenate([dq * SCALE, dk, dv], 1),)

    return _rowwise(fn, "post_c", dq.shape[0], tm, [(t, _rows(tm, C_W)) for t in (dq, dk, dv)],
                    [(QKV_W, _MXU, ("slot", 3 * QKV_W, 2))], into={0: dh})[0]


def _adamw(w, g, m, v, name):
    rows, cols = w.shape
    tm = _tile(rows, 256, 8)

    def fn(w, g, m, v):
        m = ADAM_B1 * m + (1.0 - ADAM_B1) * g
        v = ADAM_B2 * v + (1.0 - ADAM_B2) * (g * g)
        m_hat = m / (1.0 - ADAM_B1 ** ADAM_STEP)
        v_hat = v / (1.0 - ADAM_B2 ** ADAM_STEP)
        delta = -ADAM_LR * (m_hat / (jnp.sqrt(v_hat) + ADAM_EPS) + ADAM_WD * w)
        return delta, m, v

    return _rowwise(fn, name, rows, tm, [(t, _rows(tm, cols)) for t in (w, g, m, v)], [(cols, F32, "row")] * 3)


def _dot_t(a, b):
    return lax.dot_general(a, b, (((1,), (1,)), ((), ())), preferred_element_type=F32)


def _tdot(a, b):
    return lax.dot_general(a, b, (((0,), (0,)), ((), ())), preferred_element_type=F32)


def _head_masks():
    lo = _lane_lo()
    return lo, (lo, jnp.logical_not(lo))


def _rep(x, rows):
    return jnp.broadcast_to(x, (rows, 128))


def _row_lo():
    return lax.broadcasted_iota(jnp.int32, (128, 1), 0) < HEAD


def _flash_fwd(q, kd, v1, tq, tk, ride=None):
    s = q.shape[0]
    tq, tk = _tile(s, tq), _tile(s, tk)
    nq, nk = s // tq, s // tk
    mx = _MXU
    n_ride = len(ride[0]) if ride else 0
    steps = BQ_W // 128 * nq * nk

    def body(*refs):
        q_ref, k_ref, v_ref = refs[:3]
        o_ref, lse_ref = refs[3 + n_ride:5 + n_ride]
        m_ref, acc_ref = refs[5 + 2 * n_ride:7 + 2 * n_ride]
        kk = pl.program_id(2)
        if ride:
            comm = (refs[3:3 + n_ride], refs[5 + n_ride:5 + 2 * n_ride], refs[-2], refs[-1])
            step = (pl.program_id(0) * nq + pl.program_id(1)) * nk + kk
            for at, phase in zip((0, steps // 3), ride[3][:2]):
                pl.when(step == at)(functools.partial(phase, *comm))

        @pl.when(kk == 0)
        def _():
            m_ref[...] = jnp.full_like(m_ref, NEG)
            acc_ref[...] = jnp.zeros_like(acc_ref)

        q2, k2, v2 = q_ref[...], k_ref[...], v_ref[...]
        _, masks = _head_masks()
        hs = range(2)
        st = [_dot_t(k2, jnp.where(masks[h], q2, jnp.zeros_like(q2))) for h in hs]
        m_prev = [m_ref[h] for h in hs]
        m_new = [jnp.maximum(m_prev[h], jnp.max(st[h], 0, keepdims=True)) for h in hs]
        p = [jnp.exp2(st[h] - m_new[h]).astype(mx) for h in hs]
        pv = [_tdot(v2, p[h]) for h in hs]
        for h in hs:
            m_ref[h] = m_new[h]
            acc_ref[h] = acc_ref[h] * jnp.exp2(m_prev[h] - m_new[h]) + pv[h]

        @pl.when(kk == nk - 1)
        def _():
            a0, a1 = acc_ref[0], acc_ref[1]
            l0, l1 = a0[HEAD:HEAD + 1], a1[HEAD:HEAD + 1]
            o_ref[...] = jnp.concatenate([a0[:HEAD] / l0, a1[:HEAD] / l1], 0).T
            lse_ref[...] = jnp.concatenate([m_ref[0] + jnp.log2(l0), m_ref[1] + jnp.log2(l1), jnp.zeros((6, tq), F32)], 0)

        if ride:
            pl.when(step == steps - 1)(functools.partial(ride[3][2], *comm))

    ride_in = list(ride[0]) if ride else []
    ride_out = [jax.ShapeDtypeStruct(shp, t.dtype) for shp, t in zip(ride[1], ride[0])] if ride else []
    ride_sems = [pltpu.SemaphoreType.DMA((ride[2], n_ride))] * 2 if ride else []
    res = pl.pallas_call(
        body, name="attn_b_fwd_gather" if ride else "attn_b_fwd", grid=(BQ_W // 128, nq, nk),
        in_specs=[pl.BlockSpec((tq, 128), lambda j, i, kk: (i, j)),
                  pl.BlockSpec((tk, 128), lambda j, i, kk: (kk, j // 2)),
                  pl.BlockSpec((tk, 128), lambda j, i, kk: (kk, j // 2))] + [ANY] * n_ride,
        out_specs=[pl.BlockSpec((tq, 128), lambda j, i, kk: (i, j)), pl.BlockSpec((None, 8, tq), lambda j, i, kk: (j, 0, i))]
        + [ANY] * n_ride,
        out_shape=[jax.ShapeDtypeStruct((s, BQ_W), F32), jax.ShapeDtypeStruct((BQ_W // 128, 8, s), F32)] + ride_out,
        scratch_shapes=[pltpu.VMEM((2, 1, tq), F32), pltpu.VMEM((2, 128, tq), F32)] + ride_sems,
        compiler_params=_params("arbitrary", "arbitrary", "arbitrary") if ride else _params("parallel", "parallel", "arbitrary"),
    )(q, kd, v1, *ride_in)
    return res[0], res[1], res[2:]


def _delta_b(do, o, tq):
    s = do.shape[0]
    tq = _tile(s, tq)

    def body(do_ref, o_ref, d_ref):
        prod = do_ref[...] * o_ref[...]
        row = lax.broadcasted_iota(jnp.int32, (8, 128), 0)
        lane = lax.broadcasted_iota(jnp.int32, (8, 128), 1)
        sel = jnp.where(((row == 0) & (lane < HEAD)) | ((row == 1) & (lane >= HEAD)), 1.0, 0.0).astype(F32)
        d_ref[...] = lax.dot_general(sel, prod, (((1,), (1,)), ((), ())), preferred_element_type=F32,
                                     precision=lax.Precision.HIGHEST)

    qs = pl.BlockSpec((tq, 128), lambda j, i: (i, j))
    return pl.pallas_call(
        body, name="attn_b_delta", grid=(BQ_W // 128, s // tq), in_specs=[qs, qs],
        out_specs=pl.BlockSpec((None, 8, tq), lambda j, i: (j, 0, i)),
        out_shape=jax.ShapeDtypeStruct((BQ_W // 128, 8, s), F32),
        compiler_params=_params("parallel", "parallel"),
    )(do, o)


def _flash_bwd(q, kd, vd, do, lse, delta, tq, tk, ride=None):
    s = q.shape[0]
    tq, tk = _tile(s, tq), _tile(s, tk)
    nq, nk = s // tq, s // tk
    group = BQ_W // 128 // 2
    mx = _MXU
    n_ride = len(ride[0]) if ride else 0
    steps = BKV_W // HEAD * nk * group * nq

    def body(*refs):
        k_ref, v_ref, q_ref, do_ref, lse_ref, dl_ref = refs[:6]
        dq_hbm, dk_ref, dv_ref = refs[6 + n_ride:9 + n_ride]
        dk_acc, dv_acc, dqt, stage, sem = refs[9 + 2 * n_ride:14 + 2 * n_ride]
        e, kk, jj, i = pl.program_id(0), pl.program_id(1), pl.program_id(2), pl.program_id(3)
        if ride:
            comm = (refs[6:6 + n_ride], refs[9 + n_ride:9 + 2 * n_ride], refs[-2], refs[-1])
            step = ((e * nk + kk) * group + jj) * nq + i
            for at, phase in zip((0, steps // 3), ride[3][:2]):
                pl.when(step == at)(functools.partial(phase, *comm))

        @pl.when((jj == 0) & (i == 0))
        def _():
            dk_acc[...] = jnp.zeros_like(dk_acc)
            dv_acc[...] = jnp.zeros_like(dv_acc)

        @pl.when(kk == 0)
        def _():
            dqt[jj, i] = jnp.zeros((128, tq), F32)

        q2, k2, v2, do2 = q_ref[...], k_ref[...], v_ref[...], do_ref[...].astype(mx)
        lse8, dl8 = lse_ref[...], dl_ref[...]
        _, masks = _head_masks()
        hs = range(2)
        qh = [jnp.where(masks[h], q2, jnp.zeros_like(q2)) for h in hs]
        doh = [jnp.where(masks[h], do2, jnp.zeros_like(do2)) for h in hs]
        st = [_dot_t(k2, qh[h]) for h in hs]
        dpt = [_dot_t(v2, doh[h]) for h in hs]
        p = [jnp.exp2(st[h] - lse8[h:h + 1]) for h in hs]
        ds = [(p[h] * (dpt[h] - dl8[h:h + 1])).astype(mx) for h in hs]
        p = [p[h].astype(mx) for h in hs]
        dv_acc[...] += jnp.dot(p[0], doh[0], preferred_element_type=F32) + jnp.dot(p[1], doh[1], preferred_element_type=F32)
        dk_acc[...] += jnp.dot(ds[0], qh[0], preferred_element_type=F32) + jnp.dot(ds[1], qh[1], preferred_element_type=F32)
        dqt[jj, i] += jnp.where(_row_lo(), _tdot(k2, ds[0]), _tdot(k2, ds[1]))

        @pl.when(kk == nk - 1)
        def _():
            stage[...] = dqt[jj, i].T
            lane0 = pl.multiple_of((group * e + jj) * 128, 128)
            cp = pltpu.make_async_copy(stage, dq_hbm.at[pl.ds(pl.multiple_of(i * tq, tq), tq), pl.ds(lane0, 128)], sem)
            cp.start()
            cp.wait()

        @pl.when((jj == group - 1) & (i == nq - 1))
        def _():
            dk_ref[...] = (dk_acc[...] + pltpu.roll(dk_acc[...], HEAD, 1)) * LN2
            dv_ref[...] = dv_acc[...] + pltpu.roll(dv_acc[...], HEAD, 1)

        if ride:
            pl.when(step == steps - 1)(functools.partial(ride[3][2], *comm))

    ks = pl.BlockSpec((tk, 128), lambda e, kk, jj, i: (kk, e))
    qs = pl.BlockSpec((tq, 128), lambda e, kk, jj, i: (i, group * e + jj))
    st = pl.BlockSpec((None, 8, tq), lambda e, kk, jj, i: (group * e + jj, 0, i))
    ride_in = list(ride[0]) if ride else []
    ride_out = [jax.ShapeDtypeStruct(shp, t.dtype) for shp, t in zip(ride[1], ride[0])] if ride else []
    ride_sems = [pltpu.SemaphoreType.DMA((ride[2], n_ride))] * 2 if ride else []
    res = pl.pallas_call(
        body, name="attn_b_bwd_exchange" if ride else "attn_b_bwd", grid=(BKV_W // HEAD, nk, group, nq),
        in_specs=[ks, ks, qs, qs, st, st] + [ANY] * n_ride, out_specs=[ANY, ks, ks] + [ANY] * n_ride,
        out_shape=[jax.ShapeDtypeStruct((s, BQ_W), F32)] + [jax.ShapeDtypeStruct((s, 2 * BKV_W), F32)] * 2 + ride_out,
        scratch_shapes=[pltpu.VMEM((tk, 128), F32)] * 2 + [pltpu.VMEM((group, nq, 128, tq), F32), pltpu.VMEM((tq, 128), F32),
                                                          pltpu.SemaphoreType.DMA] + ride_sems,
        compiler_params=_params("arbitrary", "arbitrary", "arbitrary", "arbitrary"),
    )(kd, vd, q, do, lse, delta, *ride_in)
    return res[0], res[1], res[2], res[3:]


def _p_and_ds(items, masks):
    mx = _MXU
    keys = [(n, h) for n in range(len(items)) for h in range(2)]
    qh = {(n, h): jnp.where(masks[h], items[n][0], jnp.zeros_like(items[n][0])) for n, h in keys}
    doh = {(n, h): jnp.where(masks[h], items[n][3], jnp.zeros_like(items[n][3])) for n, h in keys}
    sc = {}
    for n, h in keys:
        s_h = _dot_t(qh[n, h], items[n][1])
        if items[n][7] is not None:
            s_h = s_h + items[n][7][h]
        sc[n, h] = jnp.where(items[n][6], s_h, NEG)
    dp = {(n, h): _dot_t(doh[n, h].astype(mx), items[n][2]) for n, h in keys}
    lse = {(n, h): jnp.max(items[n][5][:, h * 128:(h + 1) * 128], -1, keepdims=True) for n, h in keys}
    delta = {(n, h): jnp.sum(doh[n, h] * items[n][4], -1, keepdims=True) for n, h in keys}
    p = {key: jnp.exp(sc[key] - lse[key]) for key in keys}
    ds = {key: p[key] * (dp[key] - delta[key]) for key in keys}
    return [[(qh[n, h], p[n, h], ds[n, h], doh[n, h]) for h in range(2)] for n in range(len(items))]


class _BandA:
    hb, has_bias, name = 1, False, "a"

    def __init__(self, nb):
        self.nb = nb

    def mask(self, qidx, kidx):
        n = self.nb * BAND
        return (jnp.abs(qidx - kidx) <= A_RADIUS) & (kidx >= 0) & (kidx < n) & (qidx >= 0) & (qidx < n)


class _BandC:
    hb, has_bias, name = 3, True, "c"

    def __init__(self, nb):
        self.nb = nb
        self.rows = nb * BAND // GRID_W
        per = BAND // GRID_W
        assert self.rows >= C_ROWS and (C_ROWS - 1) // per <= self.hb
        assert (self.rows - 1) // per - (self.rows - C_ROWS) // per <= self.hb

    def mask(self, qidx, kidx):
        n = self.nb * BAND
        sh = GRID_W.bit_length() - 1
        qrow, cq = qidx >> sh, qidx & (GRID_W - 1)
        krow, ck = kidx >> sh, kidx & (GRID_W - 1)
        r0 = jnp.clip(qrow - C_ROWS // 2, 0, self.rows - C_ROWS)
        c0 = jnp.clip(cq - C_COLS // 2, 0, GRID_W - C_COLS)
        ok = (qidx >= 0) & (qidx < n) & (kidx >= 0) & (kidx < n)
        return ok & (krow >= r0) & (krow < r0 + C_ROWS) & (ck >= c0) & (ck < c0 + C_COLS)


def _bias_tile(off, a):
    return (BAND // GRID_W) * off - a + (C_ROWS - 1) + 2


def _band_bias_k(band, bt_ref, h):
    per = BAND // GRID_W
    return jnp.concatenate([jnp.concatenate([bt_ref[h, _bias_tile(off, a)] for off in range(-band.hb, band.hb + 1)], 1)
                            for a in range(per)], 0)


def _band_bias_q(band, bt_ref, h):
    per = BAND // GRID_W
    return jnp.concatenate([bt_ref[h, _bias_tile(-off, a)] for off in range(-band.hb, band.hb + 1) for a in range(per)], 0)


def _band_split(nb, ncb):
    cb = max(c for c in (4, 2, 1) if ncb % c == 0)
    rb = max(r for r in (4, 2, 1) if nb % r == 0 and r * cb <= 16)
    return rb, cb


def _band_specs(band, rb, cb, nb, width):
    def edge(first):
        return pl.BlockSpec((BAND, cb * width), lambda c, i: (jnp.clip(i * rb + first, 0, nb - 1), c))

    main = pl.BlockSpec((rb * BAND, cb * width), lambda c, i: (i, c))
    return [edge(t - band.hb) for t in range(band.hb)] + [main] + [edge(rb + t) for t in range(band.hb)]


def _band_rows(band, refs, rb, r, lanes):
    hb = band.hb
    parts = []
    for b in range(r, r + 2 * hb + 1):
        if b < hb:
            parts.append(refs[b][:, lanes])
        elif b < hb + rb:
            parts.append(refs[hb][(b - hb) * BAND:(b - hb + 1) * BAND, lanes])
        else:
            parts.append(refs[b - rb + 1][:, lanes])
    return jnp.concatenate(parts, 0)


def _band_idx(band, blk, rows_of_blocks, axis):
    shape = (rows_of_blocks * BAND, 1) if axis == 0 else (1, rows_of_blocks * BAND)
    return blk * BAND + lax.broadcasted_iota(jnp.int32, shape, axis)


def _band_fwd(band, q, k, v, bt=None):
    n, w = q.shape
    nb, ncb, nband = n // BAND, w // 128, 2 * band.hb + 1
    rb, cb = _band_split(nb, ncb)
    mx = _MXU
    raw = not band.has_bias

    def body(*refs):
        q_ref, k_refs, v_refs = refs[0], refs[1:1 + nband], refs[1 + nband:1 + 2 * nband]
        rest = refs[1 + 2 * nband:]
        bt_ref = rest[0] if band.has_bias else None
        outs = rest[1:] if band.has_bias else rest
        i = pl.program_id(1)
        lo, masks = _head_masks()
        subs = [(r, c) for r in range(rb) for c in range(cb)]
        mask = {r: band.mask(_band_idx(band, i * rb + r, 1, 0), _band_idx(band, i * rb + r - band.hb, nband, 1)) for r in range(rb)}
        lanes = {c: slice(c * 128, (c + 1) * 128) for c in range(cb)}
        rows = {r: slice(r * BAND, (r + 1) * BAND) for r in range(rb)}
        sc = {}
        for r, c in subs:
            q2, kcat = q_ref[rows[r], lanes[c]], _band_rows(band, k_refs, rb, r, lanes[c])
            for h in range(2):
                s_h = _dot_t(jnp.where(masks[h], q2, jnp.zeros_like(q2)), kcat)
                if band.has_bias:
                    s_h = s_h + _band_bias_k(band, bt_ref, 2 * c + h)
                sc[r, c, h] = jnp.where(mask[r], s_h, NEG)
        ms = {key: jnp.max(s_h, -1, keepdims=True) for key, s_h in sc.items()}
        ps = {key: jnp.exp(s_h - ms[key]) for key, s_h in sc.items()}
        ls = {key: jnp.sum(p, -1, keepdims=True) for key, p in ps.items()}
        os_ = {}
        for r, c in subs:
            vcat = _band_rows(band, v_refs, rb, r, lanes[c])
            for h in range(2):
                os_[r, c, h] = jnp.dot(ps[r, c, h].astype(mx), vcat, preferred_element_type=F32)
        for r, c in subs:
            st_lanes = [slice(c * 256 + h * 128, c * 256 + (h + 1) * 128) for h in range(2)]
            if raw:
                o_ref, m_ref, l_ref = outs
                o_ref[rows[r], lanes[c]] = jnp.where(lo, os_[r, c, 0], os_[r, c, 1])
                for h in range(2):
                    m_ref[rows[r], st_lanes[h]] = _rep(ms[r, c, h], BAND)
                    l_ref[rows[r], st_lanes[h]] = _rep(ls[r, c, h], BAND)
            else:
                o_ref, lse_ref = outs
                o_ref[rows[r], lanes[c]] = jnp.where(lo, os_[r, c, 0] / ls[r, c, 0], os_[r, c, 1] / ls[r, c, 1])
                for h in range(2):
                    lse_ref[rows[r], st_lanes[h]] = _rep(ms[r, c, h] + jnp.log(ls[r, c, h]), BAND)

    qs = pl.BlockSpec((rb * BAND, cb * 128), lambda c, i: (i, c))
    ks = _band_specs(band, rb, cb, nb, 128)
    st = pl.BlockSpec((rb * BAND, cb * 256), lambda c, i: (i, c))
    in_specs, args = [qs] + ks + ks, [q] + [k] * nband + [v] * nband
    if band.has_bias:
        in_specs.append(pl.BlockSpec((2 * cb, BT_TILES, GRID_W, 128), lambda c, i: (c, 0, 0, 0)))
        args.append(bt)
    n_stats = 2 if raw else 1
    return pl.pallas_call(
        body, name="attn_%s_fwd" % band.name, grid=(ncb // cb, nb // rb), in_specs=in_specs,
        out_specs=[qs] + [st] * n_stats,
        out_shape=[jax.ShapeDtypeStruct((n, w), F32)] + [jax.ShapeDtypeStruct((n, 2 * w), F32)] * n_stats,
        compiler_params=_params("parallel", "arbitrary"),
    )(*args)


def _band_dq(band, q, k, v, do, o, lse, bt=None):
    n, w = q.shape
    nb, ncb, nband = n // BAND, w // 128, 2 * band.hb + 1
    rb, cb = _band_split(nb, ncb)
    mx = _MXU
    per = BAND // GRID_W

    def body(*refs):
        q_ref, k_refs, v_refs = refs[0], refs[1:1 + nband], refs[1 + nband:1 + 2 * nband]
        do_ref, o_ref, lse_ref = refs[1 + 2 * nband:4 + 2 * nband]
        rest = refs[4 + 2 * nband:]
        dq_ref = rest[1] if band.has_bias else rest[0]
        i = pl.program_id(1)
        lo, masks = _head_masks()
        if band.has_bias:
            dbt_ref = rest[2]

            @pl.when(i == 0)
            def _():
                dbt_ref[...] = jnp.zeros_like(dbt_ref)

        subs = [(r, c) for r in range(rb) for c in range(cb)]
        mask = {r: band.mask(_band_idx(band, i * rb + r, 1, 0), _band_idx(band, i * rb + r - band.hb, nband, 1)) for r in range(rb)}
        items, kcats = [], []
        for r, c in subs:
            lanes, rows = slice(c * 128, (c + 1) * 128), slice(r * BAND, (r + 1) * BAND)
            kcats.append(_band_rows(band, k_refs, rb, r, lanes))
            bias = [_band_bias_k(band, rest[0], 2 * c + h) for h in range(2)] if band.has_bias else None
            items.append((q_ref[rows, lanes], kcats[-1], _band_rows(band, v_refs, rb, r, lanes), do_ref[rows, lanes],
                          o_ref[rows, lanes], lse_ref[rows, c * 256:(c + 1) * 256], mask[r], bias))
        res = _p_and_ds(items, masks)
        dqs = [[jnp.dot(ds.astype(mx), kcat, preferred_element_type=F32) for _, _, ds, _ in hs] for hs, kcat in zip(res, kcats)]
        for (r, c), hs, dq in zip(subs, res, dqs):
            dq_ref[r * BAND:(r + 1) * BAND, c * 128:(c + 1) * 128] = jnp.where(lo, dq[0], dq[1])
            if band.has_bias:
                for h in range(2):
                    ds = hs[h][2]
                    for a in range(per):
                        for t in range(nband):
                            tile = ds[a * GRID_W:(a + 1) * GRID_W, t * 128:(t + 1) * 128]
                            dbt_ref[2 * c + h, _bias_tile(t - band.hb, a)] += tile

    qs = pl.BlockSpec((rb * BAND, cb * 128), lambda c, i: (i, c))
    ks = _band_specs(band, rb, cb, nb, 128)
    st = pl.BlockSpec((rb * BAND, cb * 256), lambda c, i: (i, c))
    in_specs, args = [qs] + ks + ks + [qs, qs, st], [q] + [k] * nband + [v] * nband + [do, o, lse]
    out_specs, out_shape = [qs], [jax.ShapeDtypeStruct((n, w), F32)]
    if band.has_bias:
        bts = pl.BlockSpec((2 * cb, BT_TILES, GRID_W, 128), lambda c, i: (c, 0, 0, 0))
        in_specs.append(bts)
        args.append(bt)
        out_specs.append(bts)
        out_shape.append(jax.ShapeDtypeStruct(bt.shape, F32))
    return pl.pallas_call(
        body, name="attn_%s_dq" % band.name, grid=(ncb // cb, nb // rb), in_specs=in_specs, out_specs=out_specs,
        out_shape=out_shape, compiler_params=_params("parallel", "arbitrary"),
    )(*args)


def _band_dkv(band, q, k, v, do, o, lse, bt=None):
    n, w = q.shape
    nb, ncb, nband = n // BAND, w // 128, 2 * band.hb + 1
    rb, cb = _band_split(nb, ncb)
    mx = _MXU

    def body(*refs):
        k_ref, v_ref = refs[0], refs[1]
        q_refs, do_refs, o_refs, lse_refs = [refs[2 + g * nband:2 + (g + 1) * nband] for g in range(4)]
        rest = refs[2 + 4 * nband:]
        dk_ref, dv_ref = rest[-2], rest[-1]
        i = pl.program_id(1)
        _, masks = _head_masks()
        subs = [(r, c) for r in range(rb) for c in range(cb)]
        mask = {r: band.mask(_band_idx(band, i * rb + r - band.hb, nband, 0), _band_idx(band, i * rb + r, 1, 1)) for r in range(rb)}
        items = []
        for r, c in subs:
            lanes, rows = slice(c * 128, (c + 1) * 128), slice(r * BAND, (r + 1) * BAND)
            qcat, docat, ocat = [_band_rows(band, g, rb, r, lanes) for g in (q_refs, do_refs, o_refs)]
            lsecat = _band_rows(band, lse_refs, rb, r, slice(c * 256, (c + 1) * 256))
            bias = [_band_bias_q(band, rest[0], 2 * c + h) for h in range(2)] if band.has_bias else None
            items.append((qcat, k_ref[rows, lanes], v_ref[rows, lanes], docat, ocat, lsecat, mask[r], bias))
        res = _p_and_ds(items, masks)
        dks = [sum(_tdot(ds.astype(mx), qh) for qh, _, ds, _ in hs) for hs in res]
        dvs = [sum(_tdot(p.astype(mx), doh.astype(mx)) for _, p, _, doh in hs) for hs in res]
        for (r, c), dk, dv in zip(subs, dks, dvs):
            dk_ref[r * BAND:(r + 1) * BAND, c * 128:(c + 1) * 128] = dk
            dv_ref[r * BAND:(r + 1) * BAND, c * 128:(c + 1) * 128] = dv

    ks = pl.BlockSpec((rb * BAND, cb * 128), lambda c, i: (i, c))
    in_specs = [ks, ks] + _band_specs(band, rb, cb, nb, 128) * 3 + _band_specs(band, rb, cb, nb, 256)
    args = [k, v] + [q] * nband + [do] * nband + [o] * nband + [lse] * nband
    if band.has_bias:
        in_specs.append(pl.BlockSpec((2 * cb, BT_TILES, GRID_W, 128), lambda c, i: (c, 0, 0, 0)))
        args.append(bt)
    return pl.pallas_call(
        body, name="attn_%s_dkv" % band.name, grid=(ncb // cb, nb // rb), in_specs=in_specs, out_specs=[ks, ks],
        out_shape=[jax.ShapeDtypeStruct((n, w), F32)] * 2,
        compiler_params=_params("parallel", "arbitrary"),
    )(*args)


def _dc_onehot():
    c = np.arange(GRID_W)
    dc = np.clip(c[None, :] - c[:, None] + (C_COLS - 1), 0, 2 * C_COLS - 2).reshape(-1)
    m = np.zeros((GRID_W * GRID_W, 128), np.float32)
    m[np.arange(dc.size), dc] = 1.0
    return m


def _bias_tiles(rpb):
    h, nr, ncol = rpb.shape
    flat = jnp.pad(rpb.reshape(h * nr, ncol), ((0, (-h * nr) % 8), (0, 128 - ncol)))
    tiles = _mm(flat, jnp.asarray(_dc_onehot().T), name="rpb_tiles", exact=True, tn=GRID_W * GRID_W)
    tiles = tiles[:h * nr].reshape(h, nr, GRID_W, GRID_W)
    tiles = jnp.pad(tiles, ((0, 0), (2, BT_TILES + 1 - nr - 2), (0, 0), (0, 0)))
    return jnp.concatenate([tiles[:, :BT_TILES], tiles[:, 1:BT_TILES + 1]], -1)


def _bias_tiles_grad(dbt, nr, ncol):
    h = dbt.shape[0]
    d = dbt[:, 2:2 + nr, :, :GRID_W] + dbt[:, 1:1 + nr, :, GRID_W:]
    flat = jnp.pad(d.reshape(h * nr, GRID_W * GRID_W), ((0, (-h * nr) % 8), (0, 0)))
    g = _mm(flat, jnp.asarray(_dc_onehot()), name="rpb_grad", exact=True, tk=GRID_W * GRID_W)
    return g[:h * nr, :ncol].reshape(h, nr, ncol)


TM = 512
TQ_B, TK_B = 1024, 2048


def _relu2(acc):
    r = jnp.maximum(acc, 0.0)
    return (r * r,)


def _layer_fwd(x, xb, w, sm, tabs, alpha, ride=None):
    tab_a, tab_q, tab_k = tabs
    s, d = x.shape
    ha = _mm(xb, w["in"], name="in_a", tn=QKV_W, b_cols=QKV_W, b_off=0)
    hb = _mm(xb, w["in"], name="in_b", tn=QKV_W, b_cols=QKV_W, b_off=1)
    hc = _mm(xb, w["in"], name="in_c", tn=QKV_W, b_cols=QKV_W, b_off=2)
    hg = _mm(xb, w["in"], name="in_g", outs=(_MXU,), tn=QKV_W, b_cols=3 * d, b_off=3)

    qa, ka, va = _prep_a(ha, tab_a, TM)
    stats = [_band_fwd(_BandA(s // dil // BAND), qa[dil], ka[dil], va[dil]) for dil in A_DILATIONS]
    oas, lse_a = _combine_a(*zip(*stats), s, TM)
    oa = oas[1]

    qb, kd, vd, v1 = _prep_b(hb, sm["q_norm"], sm["k_norm"], tab_q, tab_k, TM)
    ob, lse_b, rode = _flash_fwd(qb, kd, v1, TQ_B, TK_B, ride)

    qc, kc, vc = _prep_c(hc, TM)
    bt = _bias_tiles(sm["rpb"])
    oc, lse_c = _band_fwd(_BandC(s // BAND), qc, kc, vc, bt)

    pa = _mm(oa, w["br_a"], name="br_a", outs=(_MXU,))
    pb = _mm(ob, w["br_b"], name="br_b", outs=(_MXU,))
    pc = _mm(oc, w["br_c"], name="br_c", outs=(_MXU,))
    merged = _gate_merge(hg, sm["b_gate"], pa, pb, pc, TM)
    ln = dict(outs=(F32, F32, _MXU), epilogue=_ln_epilogue(alpha), tm=512, tn=d)
    r1, x1, x1b = _mm(merged, w["out"], name="w_out_ln1", extras=(x, sm["ln1_g"], sm["ln1_b"]), **ln)
    act = _mm(x1b, w["up"], name="w_up", outs=(_MXU,), epilogue=_relu2)
    r2, x2, x2b = _mm(act, w["down"], name="w_down_ln2", extras=(x1, sm["ln2_g"], sm["ln2_b"]), **ln)
    saved = dict(xb=xb, hb=hb, hg=hg, qa=qa, ka=ka, va=va, oa=oa, oas=oas, lse_a=lse_a, qb=qb, kd=kd, vd=vd, ob=ob, lse_b=lse_b,
                 qc=qc, kc=kc, vc=vc, oc=oc, lse_c=lse_c, bt=bt, pa=pa, pb=pb, pc=pc, merged=merged, r1=r1, x1b=x1b,
                 act=act, r2=r2)
    return x2, x2b, saved, rode


def _layer_bwd(dx2, w, sm, sv, tabs, alpha, ride=None):
    tab_a, tab_q, tab_k = tabs
    s, d = dx2.shape
    g = {}
    dr2, dr2b, dg2, db2 = _ln_bwd(dx2, sv["r2"], sm["ln2_g"], "ln2_bwd", TM)
    g["ln2_g"], g["ln2_b"] = dg2.sum(0), db2.sum(0)
    du = _mm(dr2b, w["down"], mode="nt", name="d_act", outs=(_MXU,), extras=(sv["act"],),
             epilogue=lambda acc, act: (acc * (2.0 * jnp.sqrt(act.astype(F32))),))
    g["w_down"] = _mm(sv["act"], dr2b, mode="tn", name="g_w_down", outs=(_MXU,)).reshape(4, -1, d)
    g["w_up"] = _mm(sv["x1b"], du, mode="tn", name="g_w_up", outs=(_MXU,), out_chips=True)
    dx1 = _mm(du, w["up"], mode="nt", name="d_x1", extras=(dr2,), epilogue=lambda acc, e: (acc + alpha * e,))
    dr1, dr1b, dg1, db1 = _ln_bwd(dx1, sv["r1"], sm["ln1_g"], "ln1_bwd", TM)
    g["ln1_g"], g["ln1_b"] = dg1.sum(0), db1.sum(0)
    g["w_out"] = _mm(sv["merged"], dr1b, mode="tn", name="g_w_out", outs=(_MXU,))
    dmerged = _mm(dr1b, w["out"], mode="nt", name="d_merged")
    dpa, dpb, dpc, dlog, gb = _gate_bwd(dmerged, sv["hg"], sm["b_gate"], sv["pa"], sv["pb"], sv["pc"], TM)
    g["b_gate"] = gb.sum(0)
    g["w_branch_a"] = _mm(sv["oa"], dpa, mode="tn", name="g_br_a", outs=(_MXU,))
    g["w_branch_b"] = _mm(sv["ob"], dpb, mode="tn", name="g_br_b", outs=(_MXU,))
    g["w_branch_c"] = _mm(sv["oc"], dpc, mode="tn", name="g_br_c", outs=(_MXU,))
    doa = _mm(dpa, w["br_a"], mode="nt", name="d_oa")
    dob = _mm(dpb, w["br_b"], mode="nt", name="d_ob")
    doc = _mm(dpc, w["br_c"], mode="nt", name="d_oc")

    dqs, dks, dvs = [], [], []
    doas = _to_dilations(doa, "d_oa_layouts", TM)
    for dil in A_DILATIONS:
        band = _BandA(s // dil // BAND)
        args = [t[dil] for t in (sv["qa"], sv["ka"], sv["va"], doas, sv["oas"], sv["lse_a"])]
        dqs.append(_band_dq(band, *args)[0])
        dk_c, dv_c = _band_dkv(band, *args)
        dks.append(dk_c)
        dvs.append(dv_c)
    dh = _post_a(dqs, dks, dvs, tab_a, s, TM)

    dqb, dkd, dvd, rode = _flash_bwd(sv["qb"], sv["kd"], sv["vd"], dob, sv["lse_b"], _delta_b(dob, sv["ob"], TQ_B),
                                     TQ_B, TK_B, ride)
    dh, gq, gk = _post_b(dqb, dkd, dvd, sv["hb"], sm["q_norm"], sm["k_norm"], tab_q, tab_k, TM, dh)
    g["q_norm_b"] = gq.sum(0).reshape(-1, HEAD).sum(0)
    g["k_norm_b"] = gk.sum(0).reshape(-1, HEAD).sum(0)

    band_c = _BandC(s // BAND)
    cargs = (sv["qc"], sv["kc"], sv["vc"], doc, sv["oc"], sv["lse_c"], sv["bt"])
    dqc, dbt = _band_dq(band_c, *cargs)
    dkc, dvc = _band_dkv(band_c, *cargs)
    dh = _post_c(dqc, dkc, dvc, TM, dh)
    g["rpb_c"] = _bias_tiles_grad(dbt, 2 * C_ROWS - 1, 2 * C_COLS - 1)

    xb = sv["xb"]
    g["w_in"] = jnp.concatenate([_mm(xb, t, mode="tn", name="g_in_" + nm, outs=(_MXU,), tn=QKV_W)
                                 for nm, t in (("qkv", dh), ("g", dlog))], 1)
    dx = _mm(dh, w["in"], mode="nt", name="d_x_qkv", tk=QKV_W, b_cols=dh.shape[1], b_off=0, extras=(dr1,),
             epilogue=lambda acc, e: (acc + alpha * e,))
    dx = _mm(dlog, w["in"], mode="nt", name="d_x_g", tk=QKV_W, b_cols=dlog.shape[1], b_off=3, extras=(dx,),
             epilogue=lambda acc, e: (acc + e,))
    return dx, g, rode


BIG = ("w_in", "w_branch_a", "w_branch_b", "w_branch_c", "w_out", "w_up", "w_down")
ROW_SHARDED = ("w_out", "w_down")
AS_GATHERED = ("w_up", "w_down")
SMALL = ("b_gate", "q_norm_b", "k_norm_b", "rpb_c", "ln1_g", "ln1_b", "ln2_g", "ln2_b")


def _layer_weights(gathered):
    names = dict(w_in="in", w_branch_a="br_a", w_branch_b="br_b", w_branch_c="br_c", w_out="out", w_up="up", w_down="down")
    whole = {n: _full_from_shards(gathered[n], n) for n in names if n not in AS_GATHERED}
    return [{short: (whole[n], l) if n in whole else (gathered[n], l, "rows" if n in ROW_SHARDED else "cols")
             for n, short in names.items()} for l in range(gathered["w_in"].shape[1])]


def _local_step(x, target, gathered, small, rest=None):
    s, d = x.shape
    depth = small["b_gate"].shape[0]
    alpha = (2 * depth) ** 0.25
    tabs = _tables(s)
    ws = _layer_weights(gathered)
    sms = [dict(b_gate=small["b_gate"][l][None], q_norm=jnp.tile(small["q_norm_b"][l], BQ_W // HEAD)[None],
                k_norm=jnp.tile(small["k_norm_b"][l], BKV_W // HEAD)[None], rpb=small["rpb_c"][l],
                ln1_g=small["ln1_g"][l][None], ln1_b=small["ln1_b"][l][None],
                ln2_g=small["ln2_g"][l][None], ln2_b=small["ln2_b"][l][None]) for l in range(depth)]
    saved = []
    h, hb = x, x.astype(_MXU)
    for l in range(depth):
        h, hb, sv, rode = _layer_fwd(h, hb, ws[l], sms[l], tabs, alpha, rest if l == 0 else None)
        if rode:
            ws += _layer_weights(dict(zip(BIG, rode)))
        saved.append(sv)
    sq, dy = _loss_head(h, target, TM)

    def stack(gs):
        out = {k: jnp.stack([gl[k] for gl in gs], 1 if k in AS_GATHERED else 0) for k in gs[0]}
        for n in BIG:
            if n not in AS_GATHERED:
                out[n] = _shards_from_full(out[n], n).astype(_MXU)
        return out

    grads = [None] * depth
    reduced_later = None
    for l in reversed(range(depth)):
        ride = ts = None
        if rest is not None and l == 0 and depth > 1:
            ts = _reduce_pairs([stack(grads[1:])[n] for n in BIG])
            ride = (ts, _chip_out_shapes(ts), CHIP_SEMS, _chip_phases())
        dy, grads[l], rode = _layer_bwd(dy, ws[l], sms[l], saved[l], tabs, alpha, ride)
        if ride:
            reduced_later = _reduce_finish(ts, rode)
    if reduced_later is None:
        return sq, dy, stack(grads)
    first = stack(grads[:1])
    reduced = _reduce_scatter([first[n] for n in BIG])
    out = {n: jnp.concatenate([a, b], 0) for n, a, b in zip(BIG, reduced, reduced_later)}
    out.update({k: jnp.concatenate([first[k], stack(grads[1:])[k]], 0) for k in first if k not in BIG})
    return sq, dy, out


def _place():
    return lax.axis_index("x"), lax.axis_index("y"), lax.axis_index("c")


def _flip(a, b):
    return a + b - 2 * a * b


def _other_chips(x, y):
    return [(1 - x, y), (x, 1 - y), (1 - x, 1 - y)]


def _comm_call(body, name, tensors, out_shapes, n_sems):
    n = len(tensors)

    def wrapped(*refs):
        body(refs[:n], refs[n:2 * n], refs[2 * n], refs[2 * n + 1])

    return pl.pallas_call(
        wrapped, name=name, in_specs=[ANY] * n, out_specs=[ANY] * n,
        out_shape=[jax.ShapeDtypeStruct(s, t.dtype) for s, t in zip(out_shapes, tensors)],
        scratch_shapes=[pltpu.SemaphoreType.DMA((n_sems, n)), pltpu.SemaphoreType.DMA((n_sems, n))],
    )(*tensors)


GATHER_SEMS = 7


def _gather_phases(lo, n_layers):
    def ctx(srcs, outs, send_sems, recv_sems):
        x, y, c = _place()
        n1, n2, dg = (_flip(x, 1 - c), _flip(y, c)), (_flip(x, c), _flip(y, 1 - c)), (1 - x, 1 - y)

        def half(t, chip, hc):
            rh = outs[t].shape[2] // 2
            return outs[t].at[2 * chip[0] + chip[1], :, pl.ds(hc * rh, rh)]

        def copy(k, t, src_ref, dst_ref, to):
            return pltpu.make_async_remote_copy(src_ref=src_ref, dst_ref=dst_ref, send_sem=send_sems.at[k, t],
                                                recv_sem=recv_sems.at[k, t], device_id=to, device_id_type=MESH)

        def sends(t, ks):
            rh = srcs[t].shape[1] // 2
            own, mine = srcs[t].at[pl.ds(lo, n_layers), pl.ds(c * rh, rh)], srcs[t].at[pl.ds(lo, n_layers)]
            table = {0: (own, half(t, (x, y), c), (*n1, c)), 1: (own, half(t, (x, y), c), (*n2, c)),
                     2: (half(t, n1, c), half(t, n1, c), (*n2, c)), 6: (mine, outs[t].at[2 * x + y], (x, y, 1 - c)),
                     "n1": (half(t, n1, c), half(t, n1, c), (x, y, 1 - c)), "n2": (half(t, n2, c), half(t, n2, c), (x, y, 1 - c)),
                     "dg": (half(t, dg, c), half(t, dg, c), (x, y, 1 - c))}
            sem = {0: 0, 1: 1, 2: 2, 6: 6, "n1": 3 + c, "n2": 4 - c, "dg": 5}
            return [copy(sem[k], t, *table[k]) for k in ks]

        def arrived(k, t, chip, hc):
            copy(k, t, half(t, chip, hc), half(t, chip, hc), (x, y, 1 - c)).wait_recv()

        return x, y, c, n1, n2, dg, sends, arrived, range(len(srcs))

    def phase0(*refs):
        *_, sends, _, ts = ctx(*refs)
        for t in ts:
            for cp in sends(t, (0, 1, 6)):
                cp.start()

    def phase1(*refs):
        x, y, c, n1, n2, dg, sends, arrived, ts = ctx(*refs)
        for t in ts:
            arrived(0, t, n1, c)
            for cp in sends(t, (2, "n1")):
                cp.start()
        for t in ts:
            arrived(1, t, n2, c)
            sends(t, ("n2",))[0].start()

    def phase2(*refs):
        x, y, c, n1, n2, dg, sends, arrived, ts = ctx(*refs)
        srcs, outs = refs[0], refs[1]
        for t in ts:
            arrived(2, t, dg, c)
            sends(t, ("dg",))[0].start()
        for t in ts:
            for j, chip in enumerate(_other_chips(x, y)):
                arrived(3 + j, t, chip, 1 - c)
            sends(t, (6,))[0].wait_recv()
            for cp in sends(t, (0, 1, 2, 6, "n1", "n2", "dg")):
                cp.wait_send()

    return [phase0, phase1, phase2]


def _gather_out_shapes(shards, n_layers):
    return [(4, n_layers) + s.shape[1:] for s in shards]


def _gather_shards(shards, lo, n_layers):
    phases = _gather_phases(lo, n_layers)

    def body(*refs):
        for f in phases:
            f(*refs)

    return _comm_call(body, "gather_weights", shards, _gather_out_shapes(shards, n_layers), GATHER_SEMS)


def _pair_exchange(parts):
    def body(srcs, outs, send_sems, recv_sems):
        x, y, c = _place()
        cps = [pltpu.make_async_remote_copy(src_ref=src.at[:, :, pl.ds((1 - c) * (src.shape[2] // 2), src.shape[2] // 2)],
                                            dst_ref=out, send_sem=send_sems.at[0, t], recv_sem=recv_sems.at[0, t],
                                            device_id=(x, y, 1 - c), device_id_type=MESH)
               for t, (src, out) in enumerate(zip(srcs, outs))]
        for cp in cps:
            cp.start()
        for cp in cps:
            cp.wait()

    return _comm_call(body, "grad_pair_exchange", parts, [p.shape[:2] + (p.shape[2] // 2, p.shape[3]) for p in parts], 1)


CHIP_SEMS = 3


def _chip_phases():
    def copies(srcs, outs, send_sems, recv_sems):
        x, y, c = _place()
        return [pltpu.make_async_remote_copy(src_ref=src.at[2 * chip[0] + chip[1]], dst_ref=out.at[k], send_sem=send_sems.at[k, t],
                                             recv_sem=recv_sems.at[k, t], device_id=(*chip, c), device_id_type=MESH)
                for t, (src, out) in enumerate(zip(srcs, outs)) for k, chip in enumerate(_other_chips(x, y))]

    def start(*refs):
        for cp in copies(*refs):
            cp.start()

    def wait(*refs):
        for cp in copies(*refs):
            cp.wait()

    return [start, lambda *refs: None, wait]


def _chip_out_shapes(ts):
    return [(3,) + t.shape[1:] for t in ts]


def _chip_exchange(ts):
    phases = _chip_phases()

    def body(*refs):
        for f in phases:
            f(*refs)

    return _comm_call(body, "grad_chip_exchange", ts, _chip_out_shapes(ts), CHIP_SEMS)


def _pair_share(halves):
    def body(srcs, outs, send_sems, recv_sems):
        x, y, c = _place()
        cps = [pltpu.make_async_remote_copy(src_ref=src, dst_ref=out, send_sem=send_sems.at[0, t], recv_sem=recv_sems.at[0, t],
                                            device_id=(x, y, 1 - c), device_id_type=MESH)
               for t, (src, out) in enumerate(zip(srcs, outs))]
        for cp in cps:
            cp.start()
        for cp in cps:
            cp.wait()

    theirs = _comm_call(body, "grad_pair_share", halves, [h.shape for h in halves], 1)
    c = jnp.reshape(lax.axis_index("c"), (1,)).astype(jnp.int32)
    return [_join_halves(mine, other, c, "grad_pair_join_%d" % t) for t, (mine, other) in enumerate(zip(halves, theirs))]


def _rows_view(t, lead):
    return t.reshape(t.shape[:lead] + (-1, t.shape[-1]))


def _join_halves(mine, theirs, c, name):
    n, rh, cols = mine.shape
    tr = _tile(rh, 1024, 8)

    def join(c_ref, mine_ref, theirs_ref, o_ref):
        o_ref[...] = jnp.where(pl.program_id(1) == c_ref[0], mine_ref[...], theirs_ref[...])

    spec = pl.BlockSpec((None, tr, cols), lambda l, h, i, c_ref: (l, i, 0))
    return pl.pallas_call(
        join, name=name,
        grid_spec=pltpu.PrefetchScalarGridSpec(
            num_scalar_prefetch=1, grid=(n, 2, rh // tr), in_specs=[spec, spec],
            out_specs=pl.BlockSpec((None, tr, cols), lambda l, h, i, c_ref: (l, h * (rh // tr) + i, 0))),
        out_shape=jax.ShapeDtypeStruct((n, 2 * rh, cols), mine.dtype),
        compiler_params=_params("parallel", "parallel", "parallel"),
    )(c, mine, theirs)


def _gather_all(v):
    r = v.shape[0]

    def body(src, out, send_sems, recv_sems, local_sem):
        x, y, c = _place()
        me = 4 * x + 2 * y + c
        mine = pltpu.make_async_copy(src, out.at[me], local_sem)
        mine.start()
        cps = []
        for k in range(1, 8):
            fx, fy, fc = (k >> 2) & 1, (k >> 1) & 1, k & 1
            peer = (x + fx - 2 * x * fx, y + fy - 2 * y * fy, c + fc - 2 * c * fc)
            cps.append(pltpu.make_async_remote_copy(src_ref=src, dst_ref=out.at[me], send_sem=send_sems.at[k - 1],
                                                    recv_sem=recv_sems.at[k - 1], device_id=peer, device_id_type=MESH))
        for cp in cps:
            cp.start()
        for k in range(1, 8):
            fx, fy, fc = (k >> 2) & 1, (k >> 1) & 1, k & 1
            frm = 4 * (x + fx - 2 * x * fx) + 2 * (y + fy - 2 * y * fy) + (c + fc - 2 * c * fc)
            pltpu.make_async_remote_copy(src_ref=src, dst_ref=out.at[frm], send_sem=send_sems.at[k - 1],
                                         recv_sem=recv_sems.at[k - 1], device_id=(x, y, c), device_id_type=MESH).wait_recv()
        for cp in cps:
            cp.wait_send()
        mine.wait()

    return pl.pallas_call(
        body, name="gather_small_grads", in_specs=[ANY], out_specs=ANY,
        out_shape=jax.ShapeDtypeStruct((8, r, 128), v.dtype),
        scratch_shapes=[pltpu.SemaphoreType.DMA((7,)), pltpu.SemaphoreType.DMA((7,)), pltpu.SemaphoreType.DMA],
    )(v)


def _sum_slots(parts, name):
    n, r, _ = parts.shape
    tr = _tile(r, 1024, 8)

    def body(p_ref, o_ref):
        acc = p_ref[0]
        for j in range(1, n):
            acc = acc + p_ref[j]
        o_ref[...] = acc

    return pl.pallas_call(
        body, name=name, grid=(r // tr,), in_specs=[pl.BlockSpec((n, tr, 128), lambda i: (0, i, 0))],
        out_specs=pl.BlockSpec((tr, 128), lambda i: (i, 0)), out_shape=jax.ShapeDtypeStruct((r, 128), parts.dtype),
        compiler_params=_params("parallel"),
    )(parts)


def _add_sibling_half(part, recv, c, name):
    _, n, rh, cols = recv.shape
    tr = _tile(rh, 1024, 16)
    nblk = rh // tr

    def body(c_ref, p_ref, r_ref, o_ref):
        o_ref[...] = (p_ref[...].astype(F32) + r_ref[...].astype(F32)).astype(o_ref.dtype)

    blk = (None, None, tr, cols)
    return pl.pallas_call(
        body, name=name,
        grid_spec=pltpu.PrefetchScalarGridSpec(
            num_scalar_prefetch=1, grid=(4, n, nblk),
            in_specs=[pl.BlockSpec(blk, lambda j, l, i, c_ref: (j, l, c_ref[0] * nblk + i, 0)),
                      pl.BlockSpec(blk, lambda j, l, i, c_ref: (j, l, i, 0))],
            out_specs=pl.BlockSpec(blk, lambda j, l, i, c_ref: (j, l, i, 0))),
        out_shape=jax.ShapeDtypeStruct(recv.shape, recv.dtype),
        compiler_params=_params("parallel", "parallel", "parallel"),
    )(c, part, recv)


def _add_chips(t, recv, me, name):
    shape = t.shape[1:]
    t, recv = _rows_view(t, 1), _rows_view(recv, 1)
    _, rh, cols = t.shape
    tr = _tile(rh, 1024, 16)

    def body(me_ref, t_ref, r_ref, o_ref):
        f = lambda v: v.astype(F32)
        o_ref[...] = ((f(t_ref[...]) + f(r_ref[0])) + f(r_ref[1])) + f(r_ref[2])

    return pl.pallas_call(
        body, name=name,
        grid_spec=pltpu.PrefetchScalarGridSpec(
            num_scalar_prefetch=1, grid=(rh // tr,),
            in_specs=[pl.BlockSpec((None, tr, cols), lambda i, me_ref: (me_ref[0], i, 0)),
                      pl.BlockSpec((3, tr, cols), lambda i, me_ref: (0, i, 0))],
            out_specs=pl.BlockSpec((tr, cols), lambda i, me_ref: (i, 0))),
        out_shape=jax.ShapeDtypeStruct((rh, cols), F32),
        compiler_params=_params("parallel"),
    )(me, t, recv).reshape(shape)


def _reduce_pairs(parts):
    c1 = jnp.reshape(lax.axis_index("c"), (1,)).astype(jnp.int32)
    return [_add_sibling_half(p, r, c1, "grad_pair_sum_%d" % i) for i, (p, r) in enumerate(zip(parts, _pair_exchange(parts)))]


def _reduce_finish(ts, recv):
    me = jnp.reshape(2 * lax.axis_index("x") + lax.axis_index("y"), (1,)).astype(jnp.int32)
    return _pair_share([_add_chips(t, r, me, "grad_chip_sum_%d" % i) for i, (t, r) in enumerate(zip(ts, recv))])


def _reduce_scatter(parts):
    ts = _reduce_pairs(parts)
    return _reduce_finish(ts, _chip_exchange(ts))


def _to_rows(parts, mult):
    flat = jnp.concatenate([p.reshape(-1) for p in parts])
    flat = jnp.pad(flat, (0, (-flat.size) % (128 * mult)))
    return flat.reshape(-1, 128)


def _from_rows(rows, shapes):
    flat, out, at = rows.reshape(-1), [], 0
    for shp in shapes:
        n = int(np.prod(shp))
        out.append(flat[at:at + n].reshape(shp))
        at += n
    return out


def _full_from_shards(g, name):
    _, depth, rows, cols = g.shape
    if name in ROW_SHARDED:
        return jnp.moveaxis(g, 0, 1).reshape(depth, 4 * rows, cols)
    return jnp.moveaxis(g, 0, 2).reshape(depth, rows, 4 * cols)


def _shards_from_full(full, name):
    depth, rows, cols = full.shape
    if name in ROW_SHARDED:
        return jnp.moveaxis(full.reshape(depth, 4, rows // 4, cols), 1, 0)
    return jnp.moveaxis(full.reshape(depth, rows, 4, cols // 4), 2, 0)


def kernel(x, w_in, b_gate, q_norm_b, k_norm_b, rpb_c, w_branch_a, w_branch_b, w_branch_c, w_out, ln1_g, ln1_b, w_up, w_down, ln2_g, ln2_b, loss_target, m_w_in, m_b_gate, m_q_norm_b, m_k_norm_b, m_rpb_c, m_w_branch_a, m_w_branch_b, m_w_branch_c, m_w_out, m_ln1_g, m_ln1_b, m_w_up, m_w_down, m_ln2_g, m_ln2_b, v_w_in, v_b_gate, v_q_norm_b, v_k_norm_b, v_rpb_c, v_w_branch_a, v_w_branch_b, v_w_branch_c, v_w_out, v_ln1_g, v_ln1_b, v_w_up, v_w_down, v_ln2_g, v_ln2_b):
    args = dict(locals())
    big_shard = {n: args[n] for n in BIG}
    small = {n: args[n] for n in SMALL}

    shards = [big_shard[n].astype(_MXU) for n in BIG]
    later = shards[0].shape[0] - 1
    first = dict(zip(BIG, _gather_shards(shards, 0, 1)))
    rest = (shards, _gather_out_shapes(shards, later), GATHER_SEMS, _gather_phases(1, later)) if later else None

    sq, grad_x, grads = _local_step(x[0], loss_target[0], first, small, rest)
    loss = lax.psum(0.5 * jnp.sum(sq) / x.shape[-1], ("x", "y", "c"))

    g_big = [grads[n] for n in BIG] if rest else _reduce_scatter([grads[n] for n in BIG])
    small_shapes = [small[n].shape for n in SMALL]
    g_small = _from_rows(_sum_slots(_gather_all(_to_rows([grads[n] for n in SMALL], 8)), "small_grad_sum"), small_shapes)
    grad = dict(zip(BIG, g_big))
    grad.update(zip(SMALL, g_small))

    delta, new_m, new_v = {}, {}, {}
    for n in BIG:
        shp = big_shard[n].shape
        two_d = lambda t: t.reshape(-1, shp[-1])
        res = _adamw(two_d(big_shard[n]), two_d(grad[n]), two_d(args["m_" + n]), two_d(args["v_" + n]), "adamw_" + n)
        delta[n], new_m[n], new_v[n] = [t.reshape(shp) for t in res]
    packed = [_to_rows([args[pre + n] for n in SMALL], 8) for pre in ("", "m_", "v_")]
    res = _adamw(packed[0], _to_rows([grad[n] for n in SMALL], 8), packed[1], packed[2], "adamw_small")
    for dst, rows in zip((delta, new_m, new_v), res):
        dst.update(zip(SMALL, _from_rows(rows, small_shapes)))

    order = ("w_in", "b_gate", "q_norm_b", "k_norm_b", "rpb_c", "w_branch_a", "w_branch_b", "w_branch_c", "w_out",
             "ln1_g", "ln1_b", "w_up", "w_down", "ln2_g", "ln2_b")
    return (loss, grad_x[None], *[grad[n] for n in order], *[delta[n] for n in order],
            *[new_m[n] for n in order], *[new_v[n] for n in order])
```

```python
import functools

import numpy as np
import jax
import jax.numpy as jnp
from jax import lax
from jax.experimental import pallas as pl
from jax.experimental.pallas import tpu as pltpu

F32 = jnp.float32
_MXU = jnp.bfloat16

HEAD = 64
A_W, BQ_W, BKV_W, C_W = 256, 512, 128, 256
QKV_W = 768
A_DILATIONS = (1, 4, 16)
A_RADIUS = 64
A_ROPE_HALF = 8
AX_ROPE_HALF = 16
ROPE_THETA = 500000.0
AX_THETA = 10000.0
GRID_W = 64
C_ROWS = 8
C_COLS = 16
BAND = 128
BT_TILES = 18
LN_EPS = 1e-5
RMS_EPS = 1e-6
NEG = -1e30
SCALE = HEAD ** -0.5
LOG2E = 1.4426950408889634
LN2 = 0.6931471805599453
ADAM_LR, ADAM_B1, ADAM_B2, ADAM_EPS, ADAM_WD, ADAM_STEP = 0.001, 0.9, 0.999, 1e-08, 0.01, 10
V7X_VMEM_LIMIT = 48 * 1024 * 1024
MESH = pl.DeviceIdType.MESH
ANY = pl.BlockSpec(memory_space=pl.ANY)


def _params(*sem):
    return pltpu.CompilerParams(dimension_semantics=sem or None, vmem_limit_bytes=V7X_VMEM_LIMIT)


def _tile(n, pref, align=128):
    if n <= pref:
        return n
    t = (pref // align) * align
    while t >= align:
        if n % t == 0:
            return t
        t -= align
    return n


def _mm(a, b, *, name, mode="nn", outs=((F32),), epilogue=None, extras=(), tm=1024, tn=1024, tk=2048, exact=False,
        b_cols=None, b_off=0, out_chips=False):
    b, b_lead, b_axis = (tuple(b) + (None, None))[:3] if isinstance(b, tuple) else (b, None, None)
    m, k = a.shape if mode != "tn" else a.shape[::-1]
    b_rows = b.shape[-2] * (4 if b_axis == "rows" else 1)
    b_last = b_cols or b.shape[-1] * (4 if b_axis == "cols" else 1)
    k2, n = (b_rows, b_last) if mode != "nt" else (b_last, b_rows)
    assert k == k2, (a.shape, b.shape, mode)
    cap_rows = b.shape[-2] if b_axis == "rows" else None
    cap_cols = b.shape[-1] if b_axis == "cols" else (n // 4 if out_chips else None)
    cap_n, cap_k = (cap_cols, cap_rows) if mode != "nt" else (cap_rows, cap_cols)
    tm, tn, tk = _tile(m, tm), _tile(cap_n or n, min(tn, cap_n or tn)), _tile(cap_k or k, min(tk, cap_k or tk))
    nk = k // tk
    n_ex, n_out = len(extras), len(outs)
    mx = F32 if exact else _MXU
    prec = lax.Precision.HIGHEST if exact else None
    dims = {"nn": (((1,), (0,)), ((), ())), "nt": (((1,), (1,)), ((), ())), "tn": (((0,), (0,)), ((), ()))}[mode]

    def body(*refs):
        a_ref, b_ref = refs[0], refs[1]
        ex = refs[2:2 + n_ex]
        out_refs = refs[2 + n_ex:2 + n_ex + n_out]
        kk = pl.program_id(2)
        av, bv = a_ref[...].astype(mx), b_ref[...].astype(mx)
        part = lax.dot_general(av, bv, dims, preferred_element_type=F32, precision=prec)

        def finish(res):
            vals = epilogue(res, *[e[...] for e in ex]) if epilogue is not None else (res,)
            for o, v in zip(out_refs, vals):
                o[...] = v.astype(o.dtype)

        if nk == 1:
            finish(part)
        else:
            acc = refs[-1]

            @pl.when(kk == 0)
            def _():
                acc[...] = part

            @pl.when((kk > 0) & (kk < nk - 1))
            def _():
                acc[...] += part

            @pl.when(kk == nk - 1)
            def _():
                finish(acc[...] + part)

    a_spec = pl.BlockSpec((tm, tk), lambda i, j, kk: (i, kk)) if mode != "tn" else pl.BlockSpec((tk, tm), lambda i, j, kk: (kk, i))
    b_tile = (tn, tk) if mode == "nt" else (tk, tn)

    def b_index(i, j, kk):
        rc = [j, kk + b_off] if mode == "nt" else [kk, j + b_off]
        if b_axis is None:
            return (() if b_lead is None else (b_lead,)) + tuple(rc)
        ax = 0 if b_axis == "rows" else 1
        per = b.shape[-2 + ax] // b_tile[ax]
        chip, rc[ax] = rc[ax] // per, rc[ax] % per
        return (chip, b_lead) + tuple(rc)

    b_spec = pl.BlockSpec((None,) * (b.ndim - 2) + b_tile, b_index)
    o_spec = pl.BlockSpec((tm, tn), lambda i, j, kk: (i, j))
    if out_chips:
        per_out = n // 4 // tn
        out_specs = [pl.BlockSpec((None, tm, tn), lambda i, j, kk: (j // per_out, i, j % per_out))] * n_out
        out_shape = [jax.ShapeDtypeStruct((4, m, n // 4), d) for d in outs]
    else:
        out_specs, out_shape = [o_spec] * n_out, [jax.ShapeDtypeStruct((m, n), d) for d in outs]
    res = pl.pallas_call(
        body, name=name, grid=(m // tm, n // tn, nk),
        in_specs=[a_spec, b_spec] + [o_spec if e.shape[0] > 1 else pl.BlockSpec((1, tn), lambda i, j, kk: (0, j)) for e in extras],
        out_specs=out_specs, out_shape=out_shape,
        scratch_shapes=[pltpu.VMEM((tm, tn), F32)] if nk > 1 else [],
        compiler_params=_params("parallel", "parallel", "arbitrary"),
    )(a, b, *extras)
    return res[0] if n_out == 1 else res


def _rows(tm, width, cb=0):
    return pl.BlockSpec((tm, width), lambda t: (t, cb))


def _whole(arr):
    nd = arr.ndim
    return pl.BlockSpec(arr.shape, lambda t: (0,) * nd)


def _rowwise(fn, name, rows, tm, ins, outs, into=None):
    into = into or {}
    n_in, n_out, n_into = len(ins), len(outs), len(into)
    dil_in = [spec[1:] if isinstance(spec, tuple) else None for _, spec in ins]
    in_specs = [_rows(tm // spec[1], spec[1] * spec[2]) if isinstance(spec, tuple) else spec for _, spec in ins]
    scratch = [pltpu.VMEM((di[1] // 128, tm, 128), F32) for di in dil_in if di] + \
              [pltpu.VMEM((n // 128, tm, 128), F32) for n, _, kind in outs if isinstance(kind, int)]

    def body(*refs):
        scr = list(refs[n_in + n_into + n_out:])
        blocks = []
        for r, di in zip(refs[:n_in], dil_in):
            if di is None:
                blocks.append(r[...])
            else:
                d, n = di
                s_ref = scr.pop(0)
                for j in range(d):
                    for b in range(n // 128):
                        lanes = slice(j * n + b * 128, j * n + (b + 1) * 128)
                        s_ref.at[b][pl.ds(j, tm // d, stride=d), :] = r[:, lanes].astype(F32)
                blocks.append(jnp.concatenate([s_ref[b] for b in range(n // 128)], 1))
        vals = fn(*blocks)
        first = pl.program_id(0) == 0
        for (ncols, _, kind), o, v in zip(outs, refs[n_in + n_into:n_in + n_into + n_out], vals):
            if kind == "row" or isinstance(kind, tuple):
                o[...] = v.astype(o.dtype)
            elif isinstance(kind, int):
                s_ref = scr.pop(0)
                for b in range(ncols // 128):
                    s_ref[b] = v[:, b * 128:(b + 1) * 128].astype(F32)
                for j in range(kind):
                    for b in range(ncols // 128):
                        lanes = slice(j * ncols + b * 128, j * ncols + (b + 1) * 128)
                        o[:, lanes] = s_ref.at[b][pl.ds(j, tm // kind, stride=kind), :].astype(o.dtype)
            else:
                part = v.reshape(tm // 8, 8, ncols).sum(0)

                @pl.when(first)
                def _(o=o, part=part):
                    o[...] = part

                @pl.when(jnp.logical_not(first))
                def _(o=o, part=part):
                    o[...] += part

    def out_spec(n, kind):
        if kind == "row":
            return _rows(tm, n), (rows, n)
        if isinstance(kind, int):
            return _rows(tm // kind, kind * n), (rows // kind, kind * n)
        if isinstance(kind, tuple):
            return _rows(tm, n, kind[2]), (rows, kind[1])
        return pl.BlockSpec((8, n), lambda t: (0, 0)), (8, n)

    specs = [out_spec(n, kind) for n, _, kind in outs]
    res = pl.pallas_call(
        body, name=name, grid=(rows // tm,),
        in_specs=in_specs + [ANY] * n_into, out_specs=[s for s, _ in specs],
        out_shape=[jax.ShapeDtypeStruct(shp, d) for (_, shp), (_, d, _) in zip(specs, outs)],
        input_output_aliases={n_in + e: o for e, o in enumerate(into)},
        scratch_shapes=scratch, compiler_params=_params("arbitrary"),
    )(*[a for a, _ in ins], *into.values())
    return res


def _lane_lo(width=128):
    return (lax.broadcasted_iota(jnp.int32, (1, width), 1) & (HEAD * 2 - 1)) < HEAD


def _group_sum(x):
    w = x.shape[-1]
    sh = HEAD.bit_length() - 1
    same = (lax.broadcasted_iota(jnp.int32, (w, w), 0) >> sh) == (lax.broadcasted_iota(jnp.int32, (w, w), 1) >> sh)
    ones = jnp.where(same, 1.0, 0.0).astype(jnp.bfloat16)
    hi = x.astype(jnp.bfloat16)
    lo = (x - hi.astype(F32)).astype(jnp.bfloat16)
    return jnp.dot(hi, ones, preferred_element_type=F32) + jnp.dot(lo, ones, preferred_element_type=F32)


def _rot(x, c, sm, sp, shift):
    w = x.shape[-1]
    return x * c + pltpu.roll(x, w - shift, 1) * sm + pltpu.roll(x, shift, 1) * sp


def _rot_t(dy, c, sm, sp, shift):
    w = dy.shape[-1]
    return dy * c + pltpu.roll(dy * sm, shift, 1) + pltpu.roll(dy * sp, w - shift, 1)


def _rope_tables(pos_parts, half, thetas):
    cs, sms, sps = [], [], []
    for pos, theta in zip(pos_parts, thetas):
        inv = theta ** (-jnp.arange(half, dtype=F32) / half)
        ang = pos.astype(F32)[:, None] * inv[None, :]
        co, si, ze = jnp.cos(ang), jnp.sin(ang), jnp.zeros_like(ang)
        cs += [co, co]
        sms += [-si, ze]
        sps += [ze, si]
    return [jnp.concatenate(t, axis=1) for t in (cs, sms, sps)]


def _tables(s):
    pos = jnp.arange(s)
    ca, sma, spa = _rope_tables([pos], A_ROPE_HALF, [ROPE_THETA])
    pad = HEAD - 2 * A_ROPE_HALF
    ca = jnp.concatenate([ca, jnp.ones((s, pad), F32)], 1)
    sma, spa = [jnp.concatenate([t, jnp.zeros((s, pad), F32)], 1) for t in (sma, spa)]
    tab_a = [jnp.tile(t, (1, A_W // HEAD)) for t in (ca, sma, spa)]
    ax = _rope_tables([pos // GRID_W, pos % GRID_W], AX_ROPE_HALF, [AX_THETA, AX_THETA])
    tab_q = [jnp.tile(t, (1, BQ_W // HEAD)) for t in ax]
    tab_k = [jnp.tile(t, (1, BKV_W // HEAD)) for t in ax]
    return tab_a, tab_q, tab_k


def _prep_a(ha, tab, tm):
    s = ha.shape[0]

    def fn(h, c, sm, sp):
        q, k, v = h[:, :A_W], h[:, A_W:2 * A_W], h[:, 2 * A_W:]
        q, k = _rot(q, c, sm, sp, A_ROPE_HALF) * SCALE, _rot(k, c, sm, sp, A_ROPE_HALF)
        return [t for t in (q, k, v) for _ in A_DILATIONS]

    res = _rowwise(fn, "prep_a", s, tm, [(ha, _rows(tm, QKV_W))] + [(t, _rows(tm, A_W)) for t in tab],
                   [(A_W, _MXU, _dil_kind(d)) for _ in range(3) for d in A_DILATIONS])
    n = len(A_DILATIONS)
    return [dict(zip(A_DILATIONS, res[i * n:(i + 1) * n])) for i in range(3)]


def _dil_kind(d):
    return "row" if d == 1 else d


def _dil_spec(d, tm, ncols):
    return _rows(tm, ncols) if d == 1 else ("dil", d, ncols)


def _to_dilations(x, name, tm):
    s, n = x.shape
    res = _rowwise(lambda v: [v for d in A_DILATIONS if d > 1], name, s, tm, [(x, _rows(tm, n))],
                   [(n, x.dtype, d) for d in A_DILATIONS if d > 1])
    return {1: x, **dict(zip([d for d in A_DILATIONS if d > 1], res))}


def _rms(x, g):
    ms = _group_sum(x * x) * (1.0 / HEAD)
    return x * lax.rsqrt(ms + RMS_EPS) * g


def _prep_b(hb, gq, gk, tab_q, tab_k, tm):
    s = hb.shape[0]

    def fn(h, gq, gk, cq, smq, spq, ck, smk, spk):
        xq, xk, v = h[:, :BQ_W], h[:, BQ_W:BQ_W + BKV_W], h[:, BQ_W + BKV_W:]
        q = _rot(_rms(xq, gq), cq, smq, spq, AX_ROPE_HALF) * (SCALE * LOG2E)
        k = _rot(_rms(xk, gk), ck, smk, spk, AX_ROPE_HALF)
        lo = _lane_lo()
        kr, vr = pltpu.roll(k, HEAD, 1), pltpu.roll(v, HEAD, 1)
        kd = jnp.concatenate([jnp.where(lo, k, kr), jnp.where(lo, kr, k)], 1)
        vd = jnp.concatenate([jnp.where(lo, v, vr), jnp.where(lo, vr, v)], 1)
        v1 = jnp.concatenate([jnp.where(lo, v, 1.0), jnp.where(lo, vr, 1.0)], 1)
        return q, kd, vd, v1

    ins = [(hb, _rows(tm, QKV_W)), (gq, _whole(gq)), (gk, _whole(gk))]
    ins += [(t, _rows(tm, BQ_W)) for t in tab_q] + [(t, _rows(tm, BKV_W)) for t in tab_k]
    return _rowwise(fn, "prep_b", s, tm, ins, [(BQ_W, _MXU, "row")] + [(2 * BKV_W, _MXU, "row")] * 3)


def _prep_c(hc, tm):
    def fn(h):
        return h[:, :C_W] * SCALE, h[:, C_W:2 * C_W], h[:, 2 * C_W:]

    return _rowwise(fn, "prep_c", hc.shape[0], tm, [(hc, _rows(tm, QKV_W))], [(C_W, _MXU, "row")] * 3)


def _combine_a(os_, ms, ls, s, tm):
    def fn(o1, o2, o3, m1, m2, m3, l1, l2, l3):
        lo = _lane_lo()
        outs, lses = [], []
        for p in range(A_W // 128):
            st = slice(p * 256, (p + 1) * 256)
            mm = [m[:, st] for m in (m1, m2, m3)]
            ll = [l[:, st] for l in (l1, l2, l3)]
            mmax = jnp.maximum(jnp.maximum(mm[0], mm[1]), mm[2])
            ws = [jnp.exp(m - mmax) for m in mm]
            den = ws[0] * ll[0] + ws[1] * ll[1] + ws[2] * ll[2]
            lses.append(mmax + jnp.log(den))
            num = sum(jnp.where(lo, w[:, :128], w[:, 128:]) * o[:, p * 128:(p + 1) * 128] for w, o in zip(ws, (o1, o2, o3)))
            outs.append(num / jnp.where(lo, den[:, :128], den[:, 128:]))
        o, lse = jnp.concatenate(outs, 1), jnp.concatenate(lses, 1)
        return [o] * len(A_DILATIONS) + [lse] * len(A_DILATIONS)

    ins = [(t, _dil_spec(d, tm, w)) for ts, w in ((os_, A_W), (ms, 2 * A_W), (ls, 2 * A_W)) for t, d in zip(ts, A_DILATIONS)]
    res = _rowwise(fn, "combine_a", s, tm, ins,
                   [(w, F32, _dil_kind(d)) for w in (A_W, 2 * A_W) for d in A_DILATIONS])
    n = len(A_DILATIONS)
    return dict(zip(A_DILATIONS, res[:n])), dict(zip(A_DILATIONS, res[n:]))


def _gates(hg, bg, d):
    return [jax.nn.sigmoid(hg[:, i * d:(i + 1) * d] + bg[:, i * d:(i + 1) * d]) for i in range(3)]


def _gate_merge(hg, bg, pa, pb, pc, tm):
    s, d = pa.shape

    def fn(hg, bg, pa, pb, pc):
        g = _gates(hg, bg, d)
        return (g[0] * pa + g[1] * pb + g[2] * pc,)

    ins = [(hg, _rows(tm, 3 * d)), (bg, _whole(bg))] + [(p, _rows(tm, d)) for p in (pa, pb, pc)]
    return _rowwise(fn, "gate_merge", s, tm, ins, [(d, _MXU, "row")])[0]


def _gate_bwd(dm, hg, bg, pa, pb, pc, tm):
    s, d = pa.shape

    def fn(dm, hg, bg, pa, pb, pc):
        g = _gates(hg, bg, d)
        dlog = jnp.concatenate([dm * p * gi * (1.0 - gi) for p, gi in zip((pa, pb, pc), g)], 1)
        return dm * g[0], dm * g[1], dm * g[2], dlog, dlog

    ins = [(dm, _rows(tm, d)), (hg, _rows(tm, 3 * d)), (bg, _whole(bg))] + [(p, _rows(tm, d)) for p in (pa, pb, pc)]
    return _rowwise(fn, "gate_bwd", s, tm, ins, [(d, _MXU, "row")] * 3 + [(3 * d, _MXU, "row"), (3 * d, F32, "acc")])


def _ln_stats(r):
    mu = jnp.mean(r, -1, keepdims=True)
    xc = r - mu
    var = jnp.mean(xc * xc, -1, keepdims=True)
    rstd = lax.rsqrt(var + LN_EPS)
    return xc * rstd, rstd


def _ln_epilogue(alpha):
    def fn(br, x, g, b):
        r = alpha * x + br
        xhat, _ = _ln_stats(r)
        y = xhat * g + b
        return r, y, y

    return fn


def _ln_bwd(dy, r, g, name, tm):
    s, d = r.shape

    def fn(dy, r, g):
        xhat, rstd = _ln_stats(r)
        dxh = dy * g
        dr = rstd * (dxh - jnp.mean(dxh, -1, keepdims=True) - xhat * jnp.mean(dxh * xhat, -1, keepdims=True))
        return dr, dr, dy * xhat, dy

    ins = [(dy, _rows(tm, d)), (r, _rows(tm, d)), (g, _whole(g))]
    return _rowwise(fn, name, s, tm, ins, [(d, F32, "row"), (d, _MXU, "row"), (d, F32, "acc"), (d, F32, "acc")])


def _loss_head(y, target, tm):
    s, d = y.shape

    def fn(y, t):
        diff = y - t
        return diff * diff, diff * (1.0 / d)

    sq, dy = _rowwise(fn, "loss_head", s, tm, [(y, _rows(tm, d)), (target, _rows(tm, d))], [(d, F32, "acc"), (d, F32, "row")])
    return sq, dy


def _post_a(dqs, dks, dvs, tab, s, tm):
    def fn(q1, q2, q3, k1, k2, k3, v1, v2, v3, c, sm, sp):
        dq = _rot_t((q1 + q2 + q3) * SCALE, c, sm, sp, A_ROPE_HALF)
        dk = _rot_t(k1 + k2 + k3, c, sm, sp, A_ROPE_HALF)
        return (jnp.concatenate([dq, dk, v1 + v2 + v3], 1),)

    ins = [(t, _dil_spec(d, tm, A_W)) for ts in (dqs, dks, dvs) for t, d in zip(ts, A_DILATIONS)]
    ins += [(t, _rows(tm, A_W)) for t in tab]
    return _rowwise(fn, "post_a", s, tm, ins, [(QKV_W, _MXU, ("slot", 3 * QKV_W, 0))])[0]


def _post_b(dq, dkd, dvd, hb, gq, gk, tab_q, tab_k, tm, dh):
    s = dq.shape[0]

    def back(dz, x, g, c, sm, sp):
        dy = _rot_t(dz, c, sm, sp, AX_ROPE_HALF)
        rstd = lax.rsqrt(_group_sum(x * x) * (1.0 / HEAD) + RMS_EPS)
        xh = x * rstd
        dxh = dy * g
        return rstd * (dxh - xh * (_group_sum(dxh * xh) * (1.0 / HEAD))), dy * xh

    def fn(dq, dkd, dvd, h, gq, gk, cq, smq, spq, ck, smk, spk):
        lo = _lane_lo()
        dk = jnp.where(lo, dkd[:, :128], dkd[:, 128:])
        dv = jnp.where(lo, dvd[:, :128], dvd[:, 128:])
        dxq, dgq = back(dq * SCALE, h[:, :BQ_W], gq, cq, smq, spq)
        dxk, dgk = back(dk, h[:, BQ_W:BQ_W + BKV_W], gk, ck, smk, spk)
        return jnp.concatenate([dxq, dxk, dv], 1), dgq, dgk

    ins = [(dq, _rows(tm, BQ_W)), (dkd, _rows(tm, 2 * BKV_W)), (dvd, _rows(tm, 2 * BKV_W)), (hb, _rows(tm, QKV_W)),
           (gq, _whole(gq)), (gk, _whole(gk))]
    ins += [(t, _rows(tm, BQ_W)) for t in tab_q] + [(t, _rows(tm, BKV_W)) for t in tab_k]
    return _rowwise(fn, "post_b", s, tm, ins, [(QKV_W, _MXU, ("slot", 3 * QKV_W, 1)), (BQ_W, F32, "acc"), (BKV_W, F32, "acc")],
                    into={0: dh})


def _post_c(dq, dk, dv, tm, dh):
    def fn(dq, dk, dv):
        return (jnp.concatenate([dq * SCALE, dk, dv], 1),)

    return _rowwise(fn, "post_c", dq.shape[0], tm, [(t, _rows(tm, C_W)) for t in (dq, dk, dv)],
                    [(QKV_W, _MXU, ("slot", 3 * QKV_W, 2))], into={0: dh})[0]


def _adamw(w, g, m, v, name):
    rows, cols = w.shape
    tm = _tile(rows, 256, 8)

    def fn(w, g, m, v):
        m = ADAM_B1 * m + (1.0 - ADAM_B1) * g
        v = ADAM_B2 * v + (1.0 - ADAM_B2) * (g * g)
        m_hat = m / (1.0 - ADAM_B1 ** ADAM_STEP)
        v_hat = v / (1.0 - ADAM_B2 ** ADAM_STEP)
        delta = -ADAM_LR * (m_hat / (jnp.sqrt(v_hat) + ADAM_EPS) + ADAM_WD * w)
        return delta, m, v

    return _rowwise(fn, name, rows, tm, [(t, _rows(tm, cols)) for t in (w, g, m, v)], [(cols, F32, "row")] * 3)


def _dot_t(a, b):
    return lax.dot_general(a, b, (((1,), (1,)), ((), ())), preferred_element_type=F32)


def _tdot(a, b):
    return lax.dot_general(a, b, (((0,), (0,)), ((), ())), preferred_element_type=F32)


def _head_masks():
    lo = _lane_lo()
    return lo, (lo, jnp.logical_not(lo))


def _rep(x, rows):
    return jnp.broadcast_to(x, (rows, 128))


def _row_lo():
    return lax.broadcasted_iota(jnp.int32, (128, 1), 0) < HEAD


def _flash_fwd(q, kd, v1, tq, tk, ride=None):
    s = q.shape[0]
    tq, tk = _tile(s, tq), _tile(s, tk)
    nq, nk = s // tq, s // tk
    mx = _MXU
    n_ride = len(ride[0]) if ride else 0
    steps = BQ_W // 128 * nq * nk

    def body(*refs):
        q_ref, k_ref, v_ref = refs[:3]
        o_ref, lse_ref = refs[3 + n_ride:5 + n_ride]
        m_ref, acc_ref = refs[5 + 2 * n_ride:7 + 2 * n_ride]
        kk = pl.program_id(2)
        if ride:
            comm = (refs[3:3 + n_ride], refs[5 + n_ride:5 + 2 * n_ride], refs[-2], refs[-1])
            step = (pl.program_id(0) * nq + pl.program_id(1)) * nk + kk
            for at, phase in zip((0, steps // 3), ride[3][:2]):
                pl.when(step == at)(functools.partial(phase, *comm))

        @pl.when(kk == 0)
        def _():
            m_ref[...] = jnp.full_like(m_ref, NEG)
            acc_ref[...] = jnp.zeros_like(acc_ref)

        q2, k2, v2 = q_ref[...], k_ref[...], v_ref[...]
        _, masks = _head_masks()
        hs = range(2)
        st = [_dot_t(k2, jnp.where(masks[h], q2, jnp.zeros_like(q2))) for h in hs]
        m_prev = [m_ref[h] for h in hs]
        m_new = [jnp.maximum(m_prev[h], jnp.max(st[h], 0, keepdims=True)) for h in hs]
        p = [jnp.exp2(st[h] - m_new[h]).astype(mx) for h in hs]
        pv = [_tdot(v2, p[h]) for h in hs]
        for h in hs:
            m_ref[h] = m_new[h]
            acc_ref[h] = acc_ref[h] * jnp.exp2(m_prev[h] - m_new[h]) + pv[h]

        @pl.when(kk == nk - 1)
        def _():
            a0, a1 = acc_ref[0], acc_ref[1]
            l0, l1 = a0[HEAD:HEAD + 1], a1[HEAD:HEAD + 1]
            o_ref[...] = jnp.concatenate([a0[:HEAD] / l0, a1[:HEAD] / l1], 0).T
            lse_ref[...] = jnp.concatenate([m_ref[0] + jnp.log2(l0), m_ref[1] + jnp.log2(l1), jnp.zeros((6, tq), F32)], 0)

        if ride:
            pl.when(step == steps - 1)(functools.partial(ride[3][2], *comm))

    ride_in = list(ride[0]) if ride else []
    ride_out = [jax.ShapeDtypeStruct(shp, t.dtype) for shp, t in zip(ride[1], ride[0])] if ride else []
    ride_sems = [pltpu.SemaphoreType.DMA((ride[2], n_ride))] * 2 if ride else []
    res = pl.pallas_call(
        body, name="attn_b_fwd_gather" if ride else "attn_b_fwd", grid=(BQ_W // 128, nq, nk),
        in_specs=[pl.BlockSpec((tq, 128), lambda j, i, kk: (i, j)),
                  pl.BlockSpec((tk, 128), lambda j, i, kk: (kk, j // 2)),
                  pl.BlockSpec((tk, 128), lambda j, i, kk: (kk, j // 2))] + [ANY] * n_ride,
        out_specs=[pl.BlockSpec((tq, 128), lambda j, i, kk: (i, j)), pl.BlockSpec((None, 8, tq), lambda j, i, kk: (j, 0, i))]
        + [ANY] * n_ride,
        out_shape=[jax.ShapeDtypeStruct((s, BQ_W), F32), jax.ShapeDtypeStruct((BQ_W // 128, 8, s), F32)] + ride_out,
        scratch_shapes=[pltpu.VMEM((2, 1, tq), F32), pltpu.VMEM((2, 128, tq), F32)] + ride_sems,
        compiler_params=_params("arbitrary", "arbitrary", "arbitrary") if ride else _params("parallel", "parallel", "arbitrary"),
    )(q, kd, v1, *ride_in)
    return res[0], res[1], res[2:]


def _delta_b(do, o, tq):
    s = do.shape[0]
    tq = _tile(s, tq)

    def body(do_ref, o_ref, d_ref):
        prod = do_ref[...] * o_ref[...]
        row = lax.broadcasted_iota(jnp.int32, (8, 128), 0)
        lane = lax.broadcasted_iota(jnp.int32, (8, 128), 1)
        sel = jnp.where(((row == 0) & (lane < HEAD)) | ((row == 1) & (lane >= HEAD)), 1.0, 0.0).astype(F32)
        d_ref[...] = lax.dot_general(sel, prod, (((1,), (1,)), ((), ())), preferred_element_type=F32,
                                     precision=lax.Precision.HIGHEST)

    qs = pl.BlockSpec((tq, 128), lambda j, i: (i, j))
    return pl.pallas_call(
        body, name="attn_b_delta", grid=(BQ_W // 128, s // tq), in_specs=[qs, qs],
        out_specs=pl.BlockSpec((None, 8, tq), lambda j, i: (j, 0, i)),
        out_shape=jax.ShapeDtypeStruct((BQ_W // 128, 8, s), F32),
        compiler_params=_params("parallel", "parallel"),
    )(do, o)


def _flash_bwd(q, kd, vd, do, lse, delta, tq, tk, ride=None):
    s = q.shape[0]
    tq, tk = _tile(s, tq), _tile(s, tk)
    nq, nk = s // tq, s // tk
    group = BQ_W // 128 // 2
    mx = _MXU
    n_ride = len(ride[0]) if ride else 0
    steps = BKV_W // HEAD * nk * group * nq

    def body(*refs):
        k_ref, v_ref, q_ref, do_ref, lse_ref, dl_ref = refs[:6]
        dq_hbm, dk_ref, dv_ref = refs[6 + n_ride:9 + n_ride]
        dk_acc, dv_acc, dqt, stage, sem = refs[9 + 2 * n_ride:14 + 2 * n_ride]
        e, kk, jj, i = pl.program_id(0), pl.program_id(1), pl.program_id(2), pl.program_id(3)
        if ride:
            comm = (refs[6:6 + n_ride], refs[9 + n_ride:9 + 2 * n_ride], refs[-2], refs[-1])
            step = ((e * nk + kk) * group + jj) * nq + i
            for at, phase in zip((0, steps // 3), ride[3][:2]):
                pl.when(step == at)(functools.partial(phase, *comm))

        @pl.when((jj == 0) & (i == 0))
        def _():
            dk_acc[...] = jnp.zeros_like(dk_acc)
            dv_acc[...] = jnp.zeros_like(dv_acc)

        @pl.when(kk == 0)
        def _():
            dqt[jj, i] = jnp.zeros((128, tq), F32)

        q2, k2, v2, do2 = q_ref[...], k_ref[...], v_ref[...], do_ref[...].astype(mx)
        lse8, dl8 = lse_ref[...], dl_ref[...]
        _, masks = _head_masks()
        hs = range(2)
        qh = [jnp.where(masks[h], q2, jnp.zeros_like(q2)) for h in hs]
        doh = [jnp.where(masks[h], do2, jnp.zeros_like(do2)) for h in hs]
        st = [_dot_t(k2, qh[h]) for h in hs]
        dpt = [_dot_t(v2, doh[h]) for h in hs]
        p = [jnp.exp2(st[h] - lse8[h:h + 1]) for h in hs]
        ds = [(p[h] * (dpt[h] - dl8[h:h + 1])).astype(mx) for h in hs]
        p = [p[h].astype(mx) for h in hs]
        dv_acc[...] += jnp.dot(p[0], doh[0], preferred_element_type=F32) + jnp.dot(p[1], doh[1], preferred_element_type=F32)
        dk_acc[...] += jnp.dot(ds[0], qh[0], preferred_element_type=F32) + jnp.dot(ds[1], qh[1], preferred_element_type=F32)
        dqt[jj, i] += jnp.where(_row_lo(), _tdot(k2, ds[0]), _tdot(k2, ds[1]))

        @pl.when(kk == nk - 1)
        def _():
            stage[...] = dqt[jj, i].T
            lane0 = pl.multiple_of((group * e + jj) * 128, 128)
            cp = pltpu.make_async_copy(stage, dq_hbm.at[pl.ds(pl.multiple_of(i * tq, tq), tq), pl.ds(lane0, 128)], sem)
            cp.start()
            cp.wait()

        @pl.when((jj == group - 1) & (i == nq - 1))
        def _():
            dk_ref[...] = (dk_acc[...] + pltpu.roll(dk_acc[...], HEAD, 1)) * LN2
            dv_ref[...] = dv_acc[...] + pltpu.roll(dv_acc[...], HEAD, 1)

        if ride:
            pl.when(step == steps - 1)(functools.partial(ride[3][2], *comm))

    ks = pl.BlockSpec((tk, 128), lambda e, kk, jj, i: (kk, e))
    qs = pl.BlockSpec((tq, 128), lambda e, kk, jj, i: (i, group * e + jj))
    st = pl.BlockSpec((None, 8, tq), lambda e, kk, jj, i: (group * e + jj, 0, i))
    ride_in = list(ride[0]) if ride else []
    ride_out = [jax.ShapeDtypeStruct(shp, t.dtype) for shp, t in zip(ride[1], ride[0])] if ride else []
    ride_sems = [pltpu.SemaphoreType.DMA((ride[2], n_ride))] * 2 if ride else []
    res = pl.pallas_call(
        body, name="attn_b_bwd_exchange" if ride else "attn_b_bwd", grid=(BKV_W // HEAD, nk, group, nq),
        in_specs=[ks, ks, qs, qs, st, st] + [ANY] * n_ride, out_specs=[ANY, ks, ks] + [ANY] * n_ride,
        out_shape=[jax.ShapeDtypeStruct((s, BQ_W), F32)] + [jax.ShapeDtypeStruct((s, 2 * BKV_W), F32)] * 2 + ride_out,
        scratch_shapes=[pltpu.VMEM((tk, 128), F32)] * 2 + [pltpu.VMEM((group, nq, 128, tq), F32), pltpu.VMEM((tq, 128), F32),
                                                          pltpu.SemaphoreType.DMA] + ride_sems,
        compiler_params=_params("arbitrary", "arbitrary", "arbitrary", "arbitrary"),
    )(kd, vd, q, do, lse, delta, *ride_in)
    return res[0], res[1], res[2], res[3:]


def _p_and_ds(items, masks):
    mx = _MXU
    keys = [(n, h) for n in range(len(items)) for h in range(2)]
    qh = {(n, h): jnp.where(masks[h], items[n][0], jnp.zeros_like(items[n][0])) for n, h in keys}
    doh = {(n, h): jnp.where(masks[h], items[n][3], jnp.zeros_like(items[n][3])) for n, h in keys}
    sc = {}
    for n, h in keys:
        s_h = _dot_t(qh[n, h], items[n][1])
        if items[n][7] is not None:
            s_h = s_h + items[n][7][h]
        sc[n, h] = jnp.where(items[n][6], s_h, NEG)
    dp = {(n, h): _dot_t(doh[n, h].astype(mx), items[n][2]) for n, h in keys}
    lse = {(n, h): jnp.max(items[n][5][:, h * 128:(h + 1) * 128], -1, keepdims=True) for n, h in keys}
    delta = {(n, h): jnp.sum(doh[n, h] * items[n][4], -1, keepdims=True) for n, h in keys}
    p = {key: jnp.exp(sc[key] - lse[key]) for key in keys}
    ds = {key: p[key] * (dp[key] - delta[key]) for key in keys}
    return [[(qh[n, h], p[n, h], ds[n, h], doh[n, h]) for h in range(2)] for n in range(len(items))]


class _BandA:
    hb, inner, has_bias, name = 1, 1, False, "a"

    def __init__(self, nb):
        self.nb = nb

    def mask(self, qidx, kidx):
        n = self.nb * BAND
        return (jnp.abs(qidx - kidx) <= A_RADIUS) & (kidx >= 0) & (kidx < n) & (qidx >= 0) & (qidx < n)


class _BandC:
    hb, inner, has_bias, name = 3, 2, True, "c"

    def __init__(self, nb):
        self.nb = nb
        self.rows = nb * BAND // GRID_W
        per = BAND // GRID_W
        assert self.rows >= C_ROWS and (C_ROWS - 1) // per <= self.hb
        assert (self.rows - 1) // per - (self.rows - C_ROWS) // per <= self.hb

    def mask(self, qidx, kidx):
        n = self.nb * BAND
        sh = GRID_W.bit_length() - 1
        qrow, cq = qidx >> sh, qidx & (GRID_W - 1)
        krow, ck = kidx >> sh, kidx & (GRID_W - 1)
        r0 = jnp.clip(qrow - C_ROWS // 2, 0, self.rows - C_ROWS)
        c0 = jnp.clip(cq - C_COLS // 2, 0, GRID_W - C_COLS)
        ok = (qidx >= 0) & (qidx < n) & (kidx >= 0) & (kidx < n)
        return ok & (krow >= r0) & (krow < r0 + C_ROWS) & (ck >= c0) & (ck < c0 + C_COLS)


def _bias_tile(off, a):
    return (BAND // GRID_W) * off - a + (C_ROWS - 1) + 2


def _band_bias_k(band, bt_ref, h, hw):
    per = BAND // GRID_W
    return jnp.concatenate([jnp.concatenate([bt_ref[h, _bias_tile(off, a)] for off in range(-hw, hw + 1)], 1)
                            for a in range(per)], 0)


def _band_bias_q(band, bt_ref, h, hw):
    per = BAND // GRID_W
    return jnp.concatenate([bt_ref[h, _bias_tile(-off, a)] for off in range(-hw, hw + 1) for a in range(per)], 0)


def _band_halfwidths(band, i, rb, nb, of_keys):
    if band.inner == band.hb:
        return [(None, band.hb)]
    lo, hi = i * rb, i * rb + rb - 1
    edge = ((lo <= band.hb) | (hi >= nb - 1 - band.hb)) if of_keys else ((lo == 0) | (hi == nb - 1))
    return [(edge, band.hb), (jnp.logical_not(edge), band.inner)]


def _band_split(nb, ncb):
    cb = max(c for c in (4, 2, 1) if ncb % c == 0)
    rb = max(r for r in (4, 2, 1) if nb % r == 0 and r * cb <= 16)
    return rb, cb


def _band_specs(band, rb, cb, nb, width):
    def edge(first):
        return pl.BlockSpec((BAND, cb * width), lambda c, i: (jnp.clip(i * rb + first, 0, nb - 1), c))

    main = pl.BlockSpec((rb * BAND, cb * width), lambda c, i: (i, c))
    return [edge(t - band.hb) for t in range(band.hb)] + [main] + [edge(rb + t) for t in range(band.hb)]


def _band_rows(band, refs, rb, r, lanes, hw):
    hb = band.hb
    parts = []
    for b in range(r + hb - hw, r + hb + hw + 1):
        if b < hb:
            parts.append(refs[b][:, lanes])
        elif b < hb + rb:
            parts.append(refs[hb][(b - hb) * BAND:(b - hb + 1) * BAND, lanes])
        else:
            parts.append(refs[b - rb + 1][:, lanes])
    return jnp.concatenate(parts, 0)


def _band_idx(band, blk, rows_of_blocks, axis):
    shape = (rows_of_blocks * BAND, 1) if axis == 0 else (1, rows_of_blocks * BAND)
    return blk * BAND + lax.broadcasted_iota(jnp.int32, shape, axis)


def _band_fwd(band, q, k, v, bt=None):
    n, w = q.shape
    nb, ncb, nband = n // BAND, w // 128, 2 * band.hb + 1
    rb, cb = _band_split(nb, ncb)
    mx = _MXU
    raw = not band.has_bias

    def body(*refs):
        q_ref, k_refs, v_refs = refs[0], refs[1:1 + nband], refs[1 + nband:1 + 2 * nband]
        rest = refs[1 + 2 * nband:]
        bt_ref = rest[0] if band.has_bias else None
        outs = rest[1:] if band.has_bias else rest
        i = pl.program_id(1)
        lo, masks = _head_masks()
        subs = [(r, c) for r in range(rb) for c in range(cb)]
        lanes = {c: slice(c * 128, (c + 1) * 128) for c in range(cb)}
        rows = {r: slice(r * BAND, (r + 1) * BAND) for r in range(rb)}

        def compute(hw):
            mask = {r: band.mask(_band_idx(band, i * rb + r, 1, 0), _band_idx(band, i * rb + r - hw, 2 * hw + 1, 1))
                    for r in range(rb)}
            sc = {}
            for r, c in subs:
                q2, kcat = q_ref[rows[r], lanes[c]], _band_rows(band, k_refs, rb, r, lanes[c], hw)
                for h in range(2):
                    s_h = _dot_t(jnp.where(masks[h], q2, jnp.zeros_like(q2)), kcat)
                    if band.has_bias:
                        s_h = s_h + _band_bias_k(band, bt_ref, 2 * c + h, hw)
                    sc[r, c, h] = jnp.where(mask[r], s_h, NEG)
            ms = {key: jnp.max(s_h, -1, keepdims=True) for key, s_h in sc.items()}
            ps = {key: jnp.exp(s_h - ms[key]) for key, s_h in sc.items()}
            ls = {key: jnp.sum(p, -1, keepdims=True) for key, p in ps.items()}
            os_ = {}
            for r, c in subs:
                vcat = _band_rows(band, v_refs, rb, r, lanes[c], hw)
                for h in range(2):
                    os_[r, c, h] = jnp.dot(ps[r, c, h].astype(mx), vcat, preferred_element_type=F32)
            for r, c in subs:
                st_lanes = [slice(c * 256 + h * 128, c * 256 + (h + 1) * 128) for h in range(2)]
                if raw:
                    o_ref, m_ref, l_ref = outs
                    o_ref[rows[r], lanes[c]] = jnp.where(lo, os_[r, c, 0], os_[r, c, 1])
                    for h in range(2):
                        m_ref[rows[r], st_lanes[h]] = _rep(ms[r, c, h], BAND)
                        l_ref[rows[r], st_lanes[h]] = _rep(ls[r, c, h], BAND)
                else:
                    o_ref, lse_ref = outs
                    o_ref[rows[r], lanes[c]] = jnp.where(lo, os_[r, c, 0] / ls[r, c, 0], os_[r, c, 1] / ls[r, c, 1])
                    for h in range(2):
                        lse_ref[rows[r], st_lanes[h]] = _rep(ms[r, c, h] + jnp.log(ls[r, c, h]), BAND)

        for pred, hw in _band_halfwidths(band, i, rb, nb, False):
            compute(hw) if pred is None else pl.when(pred)(functools.partial(compute, hw))

    qs = pl.BlockSpec((rb * BAND, cb * 128), lambda c, i: (i, c))
    ks = _band_specs(band, rb, cb, nb, 128)
    st = pl.BlockSpec((rb * BAND, cb * 256), lambda c, i: (i, c))
    in_specs, args = [qs] + ks + ks, [q] + [k] * nband + [v] * nband
    if band.has_bias:
        in_specs.append(pl.BlockSpec((2 * cb, BT_TILES, GRID_W, 128), lambda c, i: (c, 0, 0, 0)))
        args.append(bt)
    n_stats = 2 if raw else 1
    return pl.pallas_call(
        body, name="attn_%s_fwd" % band.name, grid=(ncb // cb, nb // rb), in_specs=in_specs,
        out_specs=[qs] + [st] * n_stats,
        out_shape=[jax.ShapeDtypeStruct((n, w), F32)] + [jax.ShapeDtypeStruct((n, 2 * w), F32)] * n_stats,
        compiler_params=_params("parallel", "arbitrary"),
    )(*args)


def _band_dq(band, q, k, v, do, o, lse, bt=None):
    n, w = q.shape
    nb, ncb, nband = n // BAND, w // 128, 2 * band.hb + 1
    rb, cb = _band_split(nb, ncb)
    mx = _MXU
    per = BAND // GRID_W

    def body(*refs):
        q_ref, k_refs, v_refs = refs[0], refs[1:1 + nband], refs[1 + nband:1 + 2 * nband]
        do_ref, o_ref, lse_ref = refs[1 + 2 * nband:4 + 2 * nband]
        rest = refs[4 + 2 * nband:]
        dq_ref = rest[1] if band.has_bias else rest[0]
        i = pl.program_id(1)
        lo, masks = _head_masks()
        if band.has_bias:
            dbt_ref = rest[2]

            @pl.when(i == 0)
            def _():
                dbt_ref[...] = jnp.zeros_like(dbt_ref)

        subs = [(r, c) for r in range(rb) for c in range(cb)]

        def compute(hw):
            mask = {r: band.mask(_band_idx(band, i * rb + r, 1, 0), _band_idx(band, i * rb + r - hw, 2 * hw + 1, 1))
                    for r in range(rb)}
            items, kcats = [], []
            for r, c in subs:
                lanes, rows = slice(c * 128, (c + 1) * 128), slice(r * BAND, (r + 1) * BAND)
                kcats.append(_band_rows(band, k_refs, rb, r, lanes, hw))
                bias = [_band_bias_k(band, rest[0], 2 * c + h, hw) for h in range(2)] if band.has_bias else None
                items.append((q_ref[rows, lanes], kcats[-1], _band_rows(band, v_refs, rb, r, lanes, hw), do_ref[rows, lanes],
                              o_ref[rows, lanes], lse_ref[rows, c * 256:(c + 1) * 256], mask[r], bias))
            res = _p_and_ds(items, masks)
            dqs = [[jnp.dot(ds.astype(mx), kcat, preferred_element_type=F32) for _, _, ds, _ in hs]
                   for hs, kcat in zip(res, kcats)]
            for (r, c), hs, dq in zip(subs, res, dqs):
                dq_ref[r * BAND:(r + 1) * BAND, c * 128:(c + 1) * 128] = jnp.where(lo, dq[0], dq[1])
                if band.has_bias:
                    for h in range(2):
                        ds = hs[h][2]
                        for a in range(per):
                            for t in range(2 * hw + 1):
                                tile = ds[a * GRID_W:(a + 1) * GRID_W, t * 128:(t + 1) * 128]
                                dbt_ref[2 * c + h, _bias_tile(t - hw, a)] += tile

        for pred, hw in _band_halfwidths(band, i, rb, nb, False):
            compute(hw) if pred is None else pl.when(pred)(functools.partial(compute, hw))

    qs = pl.BlockSpec((rb * BAND, cb * 128), lambda c, i: (i, c))
    ks = _band_specs(band, rb, cb, nb, 128)
    st = pl.BlockSpec((rb * BAND, cb * 256), lambda c, i: (i, c))
    in_specs, args = [qs] + ks + ks + [qs, qs, st], [q] + [k] * nband + [v] * nband + [do, o, lse]
    out_specs, out_shape = [qs], [jax.ShapeDtypeStruct((n, w), F32)]
    if band.has_bias:
        bts = pl.BlockSpec((2 * cb, BT_TILES, GRID_W, 128), lambda c, i: (c, 0, 0, 0))
        in_specs.append(bts)
        args.append(bt)
        out_specs.append(bts)
        out_shape.append(jax.ShapeDtypeStruct(bt.shape, F32))
    return pl.pallas_call(
        body, name="attn_%s_dq" % band.name, grid=(ncb // cb, nb // rb), in_specs=in_specs, out_specs=out_specs,
        out_shape=out_shape, compiler_params=_params("parallel", "arbitrary"),
    )(*args)


def _band_dkv(band, q, k, v, do, o, lse, bt=None):
    n, w = q.shape
    nb, ncb, nband = n // BAND, w // 128, 2 * band.hb + 1
    rb, cb = _band_split(nb, ncb)
    mx = _MXU

    def body(*refs):
        k_ref, v_ref = refs[0], refs[1]
        q_refs, do_refs, o_refs, lse_refs = [refs[2 + g * nband:2 + (g + 1) * nband] for g in range(4)]
        rest = refs[2 + 4 * nband:]
        dk_ref, dv_ref = rest[-2], rest[-1]
        i = pl.program_id(1)
        _, masks = _head_masks()
        subs = [(r, c) for r in range(rb) for c in range(cb)]

        def compute(hw):
            mask = {r: band.mask(_band_idx(band, i * rb + r - hw, 2 * hw + 1, 0), _band_idx(band, i * rb + r, 1, 1))
                    for r in range(rb)}
            items = []
            for r, c in subs:
                lanes, rows = slice(c * 128, (c + 1) * 128), slice(r * BAND, (r + 1) * BAND)
                qcat, docat, ocat = [_band_rows(band, g, rb, r, lanes, hw) for g in (q_refs, do_refs, o_refs)]
                lsecat = _band_rows(band, lse_refs, rb, r, slice(c * 256, (c + 1) * 256), hw)
                bias = [_band_bias_q(band, rest[0], 2 * c + h, hw) for h in range(2)] if band.has_bias else None
                items.append((qcat, k_ref[rows, lanes], v_ref[rows, lanes], docat, ocat, lsecat, mask[r], bias))
            res = _p_and_ds(items, masks)
            dks = [sum(_tdot(ds.astype(mx), qh) for qh, _, ds, _ in hs) for hs in res]
            dvs = [sum(_tdot(p.astype(mx), doh.astype(mx)) for _, p, _, doh in hs) for hs in res]
            for (r, c), dk, dv in zip(subs, dks, dvs):
                dk_ref[r * BAND:(r + 1) * BAND, c * 128:(c + 1) * 128] = dk
                dv_ref[r * BAND:(r + 1) * BAND, c * 128:(c + 1) * 128] = dv

        for pred, hw in _band_halfwidths(band, i, rb, nb, True):
            compute(hw) if pred is None else pl.when(pred)(functools.partial(compute, hw))

    ks = pl.BlockSpec((rb * BAND, cb * 128), lambda c, i: (i, c))
    in_specs = [ks, ks] + _band_specs(band, rb, cb, nb, 128) * 3 + _band_specs(band, rb, cb, nb, 256)
    args = [k, v] + [q] * nband + [do] * nband + [o] * nband + [lse] * nband
    if band.has_bias:
        in_specs.append(pl.BlockSpec((2 * cb, BT_TILES, GRID_W, 128), lambda c, i: (c, 0, 0, 0)))
        args.append(bt)
    return pl.pallas_call(
        body, name="attn_%s_dkv" % band.name, grid=(ncb // cb, nb // rb), in_specs=in_specs, out_specs=[ks, ks],
        out_shape=[jax.ShapeDtypeStruct((n, w), F32)] * 2,
        compiler_params=_params("parallel", "arbitrary"),
    )(*args)


def _dc_onehot():
    c = np.arange(GRID_W)
    dc = np.clip(c[None, :] - c[:, None] + (C_COLS - 1), 0, 2 * C_COLS - 2).reshape(-1)
    m = np.zeros((GRID_W * GRID_W, 128), np.float32)
    m[np.arange(dc.size), dc] = 1.0
    return m


def _bias_tiles(rpb):
    h, nr, ncol = rpb.shape
    flat = jnp.pad(rpb.reshape(h * nr, ncol), ((0, (-h * nr) % 8), (0, 128 - ncol)))
    tiles = _mm(flat, jnp.asarray(_dc_onehot().T), name="rpb_tiles", exact=True, tn=GRID_W * GRID_W)
    tiles = tiles[:h * nr].reshape(h, nr, GRID_W, GRID_W)
    tiles = jnp.pad(tiles, ((0, 0), (2, BT_TILES + 1 - nr - 2), (0, 0), (0, 0)))
    return jnp.concatenate([tiles[:, :BT_TILES], tiles[:, 1:BT_TILES + 1]], -1)


def _bias_tiles_grad(dbt, nr, ncol):
    h = dbt.shape[0]
    d = dbt[:, 2:2 + nr, :, :GRID_W] + dbt[:, 1:1 + nr, :, GRID_W:]
    flat = jnp.pad(d.reshape(h * nr, GRID_W * GRID_W), ((0, (-h * nr) % 8), (0, 0)))
    g = _mm(flat, jnp.asarray(_dc_onehot()), name="rpb_grad", exact=True, tk=GRID_W * GRID_W)
    return g[:h * nr, :ncol].reshape(h, nr, ncol)


TM = 512
TQ_B, TK_B = 1024, 2048


def _relu2(acc):
    r = jnp.maximum(acc, 0.0)
    return (r * r,)


def _layer_fwd(x, xb, w, sm, tabs, alpha, ride=None):
    tab_a, tab_q, tab_k = tabs
    s, d = x.shape
    ha = _mm(xb, w["in"], name="in_a", tn=QKV_W, b_cols=QKV_W, b_off=0)
    hb = _mm(xb, w["in"], name="in_b", tn=QKV_W, b_cols=QKV_W, b_off=1)
    hc = _mm(xb, w["in"], name="in_c", tn=QKV_W, b_cols=QKV_W, b_off=2)
    hg = _mm(xb, w["in"], name="in_g", outs=(_MXU,), tn=QKV_W, b_cols=3 * d, b_off=3)

    qa, ka, va = _prep_a(ha, tab_a, TM)
    stats = [_band_fwd(_BandA(s // dil // BAND), qa[dil], ka[dil], va[dil]) for dil in A_DILATIONS]
    oas, lse_a = _combine_a(*zip(*stats), s, TM)
    oa = oas[1]

    qb, kd, vd, v1 = _prep_b(hb, sm["q_norm"], sm["k_norm"], tab_q, tab_k, TM)
    ob, lse_b, rode = _flash_fwd(qb, kd, v1, TQ_B, TK_B, ride)

    qc, kc, vc = _prep_c(hc, TM)
    bt = _bias_tiles(sm["rpb"])
    oc, lse_c = _band_fwd(_BandC(s // BAND), qc, kc, vc, bt)

    pa = _mm(oa, w["br_a"], name="br_a", outs=(_MXU,))
    pb = _mm(ob, w["br_b"], name="br_b", outs=(_MXU,))
    pc = _mm(oc, w["br_c"], name="br_c", outs=(_MXU,))
    merged = _gate_merge(hg, sm["b_gate"], pa, pb, pc, TM)
    ln = dict(outs=(F32, F32, _MXU), epilogue=_ln_epilogue(alpha), tm=512, tn=d)
    r1, x1, x1b = _mm(merged, w["out"], name="w_out_ln1", extras=(x, sm["ln1_g"], sm["ln1_b"]), **ln)
    act = _mm(x1b, w["up"], name="w_up", outs=(_MXU,), epilogue=_relu2)
    r2, x2, x2b = _mm(act, w["down"], name="w_down_ln2", extras=(x1, sm["ln2_g"], sm["ln2_b"]), **ln)
    saved = dict(xb=xb, hb=hb, hg=hg, qa=qa, ka=ka, va=va, oa=oa, oas=oas, lse_a=lse_a, qb=qb, kd=kd, vd=vd, ob=ob, lse_b=lse_b,
                 qc=qc, kc=kc, vc=vc, oc=oc, lse_c=lse_c, bt=bt, pa=pa, pb=pb, pc=pc, merged=merged, r1=r1, x1b=x1b,
                 act=act, r2=r2)
    return x2, x2b, saved, rode


def _layer_bwd(dx2, w, sm, sv, tabs, alpha, ride=None):
    tab_a, tab_q, tab_k = tabs
    s, d = dx2.shape
    g = {}
    dr2, dr2b, dg2, db2 = _ln_bwd(dx2, sv["r2"], sm["ln2_g"], "ln2_bwd", TM)
    g["ln2_g"], g["ln2_b"] = dg2.sum(0), db2.sum(0)
    du = _mm(dr2b, w["down"], mode="nt", name="d_act", outs=(_MXU,), extras=(sv["act"],),
             epilogue=lambda acc, act: (acc * (2.0 * jnp.sqrt(act.astype(F32))),))
    g["w_down"] = _mm(sv["act"], dr2b, mode="tn", name="g_w_down", outs=(_MXU,)).reshape(4, -1, d)
    g["w_up"] = _mm(sv["x1b"], du, mode="tn", name="g_w_up", outs=(_MXU,), out_chips=True)
    dx1 = _mm(du, w["up"], mode="nt", name="d_x1", extras=(dr2,), epilogue=lambda acc, e: (acc + alpha * e,))
    dr1, dr1b, dg1, db1 = _ln_bwd(dx1, sv["r1"], sm["ln1_g"], "ln1_bwd", TM)
    g["ln1_g"], g["ln1_b"] = dg1.sum(0), db1.sum(0)
    g["w_out"] = _mm(sv["merged"], dr1b, mode="tn", name="g_w_out", outs=(_MXU,))
    dmerged = _mm(dr1b, w["out"], mode="nt", name="d_merged")
    dpa, dpb, dpc, dlog, gb = _gate_bwd(dmerged, sv["hg"], sm["b_gate"], sv["pa"], sv["pb"], sv["pc"], TM)
    g["b_gate"] = gb.sum(0)
    g["w_branch_a"] = _mm(sv["oa"], dpa, mode="tn", name="g_br_a", outs=(_MXU,))
    g["w_branch_b"] = _mm(sv["ob"], dpb, mode="tn", name="g_br_b", outs=(_MXU,))
    g["w_branch_c"] = _mm(sv["oc"], dpc, mode="tn", name="g_br_c", outs=(_MXU,))
    doa = _mm(dpa, w["br_a"], mode="nt", name="d_oa")
    dob = _mm(dpb, w["br_b"], mode="nt", name="d_ob")
    doc = _mm(dpc, w["br_c"], mode="nt", name="d_oc")

    dqs, dks, dvs = [], [], []
    doas = _to_dilations(doa, "d_oa_layouts", TM)
    for dil in A_DILATIONS:
        band = _BandA(s // dil // BAND)
        args = [t[dil] for t in (sv["qa"], sv["ka"], sv["va"], doas, sv["oas"], sv["lse_a"])]
        dqs.append(_band_dq(band, *args)[0])
        dk_c, dv_c = _band_dkv(band, *args)
        dks.append(dk_c)
        dvs.append(dv_c)
    dh = _post_a(dqs, dks, dvs, tab_a, s, TM)

    dqb, dkd, dvd, rode = _flash_bwd(sv["qb"], sv["kd"], sv["vd"], dob, sv["lse_b"], _delta_b(dob, sv["ob"], TQ_B),
                                     TQ_B, TK_B, ride)
    dh, gq, gk = _post_b(dqb, dkd, dvd, sv["hb"], sm["q_norm"], sm["k_norm"], tab_q, tab_k, TM, dh)
    g["q_norm_b"] = gq.sum(0).reshape(-1, HEAD).sum(0)
    g["k_norm_b"] = gk.sum(0).reshape(-1, HEAD).sum(0)

    band_c = _BandC(s // BAND)
    cargs = (sv["qc"], sv["kc"], sv["vc"], doc, sv["oc"], sv["lse_c"], sv["bt"])
    dqc, dbt = _band_dq(band_c, *cargs)
    dkc, dvc = _band_dkv(band_c, *cargs)
    dh = _post_c(dqc, dkc, dvc, TM, dh)
    g["rpb_c"] = _bias_tiles_grad(dbt, 2 * C_ROWS - 1, 2 * C_COLS - 1)

    xb = sv["xb"]
    g["w_in"] = jnp.concatenate([_mm(xb, t, mode="tn", name="g_in_" + nm, outs=(_MXU,), tn=QKV_W)
                                 for nm, t in (("qkv", dh), ("g", dlog))], 1)
    dx = _mm(dh, w["in"], mode="nt", name="d_x_qkv", tk=QKV_W, b_cols=dh.shape[1], b_off=0, extras=(dr1,),
             epilogue=lambda acc, e: (acc + alpha * e,))
    dx = _mm(dlog, w["in"], mode="nt", name="d_x_g", tk=QKV_W, b_cols=dlog.shape[1], b_off=3, extras=(dx,),
             epilogue=lambda acc, e: (acc + e,))
    return dx, g, rode


BIG = ("w_in", "w_branch_a", "w_branch_b", "w_branch_c", "w_out", "w_up", "w_down")
ROW_SHARDED = ("w_out", "w_down")
AS_GATHERED = ("w_up", "w_down")
SMALL = ("b_gate", "q_norm_b", "k_norm_b", "rpb_c", "ln1_g", "ln1_b", "ln2_g", "ln2_b")


def _layer_weights(gathered):
    names = dict(w_in="in", w_branch_a="br_a", w_branch_b="br_b", w_branch_c="br_c", w_out="out", w_up="up", w_down="down")
    whole = {n: _full_from_shards(gathered[n], n) for n in names if n not in AS_GATHERED}
    return [{short: (whole[n], l) if n in whole else (gathered[n], l, "rows" if n in ROW_SHARDED else "cols")
             for n, short in names.items()} for l in range(gathered["w_in"].shape[1])]


def _local_step(x, target, gathered, small, rest=None):
    s, d = x.shape
    depth = small["b_gate"].shape[0]
    alpha = (2 * depth) ** 0.25
    tabs = _tables(s)
    ws = _layer_weights(gathered)
    sms = [dict(b_gate=small["b_gate"][l][None], q_norm=jnp.tile(small["q_norm_b"][l], BQ_W // HEAD)[None],
                k_norm=jnp.tile(small["k_norm_b"][l], BKV_W // HEAD)[None], rpb=small["rpb_c"][l],
                ln1_g=small["ln1_g"][l][None], ln1_b=small["ln1_b"][l][None],
                ln2_g=small["ln2_g"][l][None], ln2_b=small["ln2_b"][l][None]) for l in range(depth)]
    saved = []
    h, hb = x, x.astype(_MXU)
    for l in range(depth):
        h, hb, sv, rode = _layer_fwd(h, hb, ws[l], sms[l], tabs, alpha, rest if l == 0 else None)
        if rode:
            ws += _layer_weights(dict(zip(BIG, rode)))
        saved.append(sv)
    sq, dy = _loss_head(h, target, TM)

    def stack(gs):
        out = {k: jnp.stack([gl[k] for gl in gs], 1 if k in AS_GATHERED else 0) for k in gs[0]}
        for n in BIG:
            if n not in AS_GATHERED:
                out[n] = _shards_from_full(out[n], n).astype(_MXU)
        return out

    grads = [None] * depth
    reduced_later = None
    for l in reversed(range(depth)):
        ride = ts = None
        if rest is not None and l == 0 and depth > 1:
            ts = _reduce_pairs([stack(grads[1:])[n] for n in BIG])
            ride = (ts, _chip_out_shapes(ts), CHIP_SEMS, _chip_phases())
        dy, grads[l], rode = _layer_bwd(dy, ws[l], sms[l], saved[l], tabs, alpha, ride)
        if ride:
            reduced_later = _reduce_finish(ts, rode)
    if reduced_later is None:
        return sq, dy, stack(grads)
    first = stack(grads[:1])
    reduced = _reduce_scatter([first[n] for n in BIG])
    out = {n: jnp.concatenate([a, b], 0) for n, a, b in zip(BIG, reduced, reduced_later)}
    out.update({k: jnp.concatenate([first[k], stack(grads[1:])[k]], 0) for k in first if k not in BIG})
    return sq, dy, out


def _place():
    return lax.axis_index("x"), lax.axis_index("y"), lax.axis_index("c")


def _flip(a, b):
    return a + b - 2 * a * b


def _other_chips(x, y):
    return [(1 - x, y), (x, 1 - y), (1 - x, 1 - y)]


def _comm_call(body, name, tensors, out_shapes, n_sems):
    n = len(tensors)

    def wrapped(*refs):
        body(refs[:n], refs[n:2 * n], refs[2 * n], refs[2 * n + 1])

    return pl.pallas_call(
        wrapped, name=name, in_specs=[ANY] * n, out_specs=[ANY] * n,
        out_shape=[jax.ShapeDtypeStruct(s, t.dtype) for s, t in zip(out_shapes, tensors)],
        scratch_shapes=[pltpu.SemaphoreType.DMA((n_sems, n)), pltpu.SemaphoreType.DMA((n_sems, n))],
    )(*tensors)


GATHER_SEMS = 7


def _gather_phases(lo, n_layers):
    def ctx(srcs, outs, send_sems, recv_sems):
        x, y, c = _place()
        n1, n2, dg = (_flip(x, 1 - c), _flip(y, c)), (_flip(x, c), _flip(y, 1 - c)), (1 - x, 1 - y)

        def half(t, chip, hc):
            rh = outs[t].shape[2] // 2
            return outs[t].at[2 * chip[0] + chip[1], :, pl.ds(hc * rh, rh)]

        def copy(k, t, src_ref, dst_ref, to):
            return pltpu.make_async_remote_copy(src_ref=src_ref, dst_ref=dst_ref, send_sem=send_sems.at[k, t],
                                                recv_sem=recv_sems.at[k, t], device_id=to, device_id_type=MESH)

        def sends(t, ks):
            rh = srcs[t].shape[1] // 2
            own, mine = srcs[t].at[pl.ds(lo, n_layers), pl.ds(c * rh, rh)], srcs[t].at[pl.ds(lo, n_layers)]
            table = {0: (own, half(t, (x, y), c), (*n1, c)), 1: (own, half(t, (x, y), c), (*n2, c)),
                     2: (half(t, n1, c), half(t, n1, c), (*n2, c)), 6: (mine, outs[t].at[2 * x + y], (x, y, 1 - c)),
                     "n1": (half(t, n1, c), half(t, n1, c), (x, y, 1 - c)), "n2": (half(t, n2, c), half(t, n2, c), (x, y, 1 - c)),
                     "dg": (half(t, dg, c), half(t, dg, c), (x, y, 1 - c))}
            sem = {0: 0, 1: 1, 2: 2, 6: 6, "n1": 3 + c, "n2": 4 - c, "dg": 5}
            return [copy(sem[k], t, *table[k]) for k in ks]

        def arrived(k, t, chip, hc):
            copy(k, t, half(t, chip, hc), half(t, chip, hc), (x, y, 1 - c)).wait_recv()

        return x, y, c, n1, n2, dg, sends, arrived, range(len(srcs))

    def phase0(*refs):
        *_, sends, _, ts = ctx(*refs)
        for t in ts:
            for cp in sends(t, (0, 1, 6)):
                cp.start()

    def phase1(*refs):
        x, y, c, n1, n2, dg, sends, arrived, ts = ctx(*refs)
        for t in ts:
            arrived(0, t, n1, c)
            for cp in sends(t, (2, "n1")):
                cp.start()
        for t in ts:
            arrived(1, t, n2, c)
            sends(t, ("n2",))[0].start()

    def phase2(*refs):
        x, y, c, n1, n2, dg, sends, arrived, ts = ctx(*refs)
        srcs, outs = refs[0], refs[1]
        for t in ts:
            arrived(2, t, dg, c)
            sends(t, ("dg",))[0].start()
        for t in ts:
            for j, chip in enumerate(_other_chips(x, y)):
                arrived(3 + j, t, chip, 1 - c)
            sends(t, (6,))[0].wait_recv()
            for cp in sends(t, (0, 1, 2, 6, "n1", "n2", "dg")):
                cp.wait_send()

    return [phase0, phase1, phase2]


def _gather_out_shapes(shards, n_layers):
    return [(4, n_layers) + s.shape[1:] for s in shards]


def _gather_shards(shards, lo, n_layers):
    phases = _gather_phases(lo, n_layers)

    def body(*refs):
        for f in phases:
            f(*refs)

    return _comm_call(body, "gather_weights", shards, _gather_out_shapes(shards, n_layers), GATHER_SEMS)


def _pair_exchange(parts):
    def body(srcs, outs, send_sems, recv_sems):
        x, y, c = _place()
        cps = [pltpu.make_async_remote_copy(src_ref=src.at[:, :, pl.ds((1 - c) * (src.shape[2] // 2), src.shape[2] // 2)],
                                            dst_ref=out, send_sem=send_sems.at[0, t], recv_sem=recv_sems.at[0, t],
                                            device_id=(x, y, 1 - c), device_id_type=MESH)
               for t, (src, out) in enumerate(zip(srcs, outs))]
        for cp in cps:
            cp.start()
        for cp in cps:
            cp.wait()

    return _comm_call(body, "grad_pair_exchange", parts, [p.shape[:2] + (p.shape[2] // 2, p.shape[3]) for p in parts], 1)


CHIP_SEMS = 3


def _chip_phases():
    def copies(srcs, outs, send_sems, recv_sems):
        x, y, c = _place()
        return [pltpu.make_async_remote_copy(src_ref=src.at[2 * chip[0] + chip[1]], dst_ref=out.at[k], send_sem=send_sems.at[k, t],
                                             recv_sem=recv_sems.at[k, t], device_id=(*chip, c), device_id_type=MESH)
                for t, (src, out) in enumerate(zip(srcs, outs)) for k, chip in enumerate(_other_chips(x, y))]

    def start(*refs):
        for cp in copies(*refs):
            cp.start()

    def wait(*refs):
        for cp in copies(*refs):
            cp.wait()

    return [start, lambda *refs: None, wait]


def _chip_out_shapes(ts):
    return [(3,) + t.shape[1:] for t in ts]


def _chip_exchange(ts):
    phases = _chip_phases()

    def body(*refs):
        for f in phases:
            f(*refs)

    return _comm_call(body, "grad_chip_exchange", ts, _chip_out_shapes(ts), CHIP_SEMS)


def _pair_share(halves):
    def body(srcs, outs, send_sems, recv_sems):
        x, y, c = _place()
        cps = [pltpu.make_async_remote_copy(src_ref=src, dst_ref=out, send_sem=send_sems.at[0, t], recv_sem=recv_sems.at[0, t],
                                            device_id=(x, y, 1 - c), device_id_type=MESH)
               for t, (src, out) in enumerate(zip(srcs, outs))]
        for cp in cps:
            cp.start()
        for cp in cps:
            cp.wait()

    theirs = _comm_call(body, "grad_pair_share", halves, [h.shape for h in halves], 1)
    c = jnp.reshape(lax.axis_index("c"), (1,)).astype(jnp.int32)
    return [_join_halves(mine, other, c, "grad_pair_join_%d" % t) for t, (mine, other) in enumerate(zip(halves, theirs))]


def _rows_view(t, lead):
    return t.reshape(t.shape[:lead] + (-1, t.shape[-1]))


def _join_halves(mine, theirs, c, name):
    n, rh, cols = mine.shape
    tr = _tile(rh, 1024, 8)

    def join(c_ref, mine_ref, theirs_ref, o_ref):
        o_ref[...] = jnp.where(pl.program_id(1) == c_ref[0], mine_ref[...], theirs_ref[...])

    spec = pl.BlockSpec((None, tr, cols), lambda l, h, i, c_ref: (l, i, 0))
    return pl.pallas_call(
        join, name=name,
        grid_spec=pltpu.PrefetchScalarGridSpec(
            num_scalar_prefetch=1, grid=(n, 2, rh // tr), in_specs=[spec, spec],
            out_specs=pl.BlockSpec((None, tr, cols), lambda l, h, i, c_ref: (l, h * (rh // tr) + i, 0))),
        out_shape=jax.ShapeDtypeStruct((n, 2 * rh, cols), mine.dtype),
        compiler_params=_params("parallel", "parallel", "parallel"),
    )(c, mine, theirs)


def _gather_all(v):
    r = v.shape[0]

    def body(src, out, send_sems, recv_sems, local_sem):
        x, y, c = _place()
        me = 4 * x + 2 * y + c
        mine = pltpu.make_async_copy(src, out.at[me], local_sem)
        mine.start()
        cps = []
        for k in range(1, 8):
            fx, fy, fc = (k >> 2) & 1, (k >> 1) & 1, k & 1
            peer = (x + fx - 2 * x * fx, y + fy - 2 * y * fy, c + fc - 2 * c * fc)
            cps.append(pltpu.make_async_remote_copy(src_ref=src, dst_ref=out.at[me], send_sem=send_sems.at[k - 1],
                                                    recv_sem=recv_sems.at[k - 1], device_id=peer, device_id_type=MESH))
        for cp in cps:
            cp.start()
        for k in range(1, 8):
            fx, fy, fc = (k >> 2) & 1, (k >> 1) & 1, k & 1
            frm = 4 * (x + fx - 2 * x * fx) + 2 * (y + fy - 2 * y * fy) + (c + fc - 2 * c * fc)
            pltpu.make_async_remote_copy(src_ref=src, dst_ref=out.at[frm], send_sem=send_sems.at[k - 1],
                                         recv_sem=recv_sems.at[k - 1], device_id=(x, y, c), device_id_type=MESH).wait_recv()
        for cp in cps:
            cp.wait_send()
        mine.wait()

    return pl.pallas_call(
        body, name="gather_small_grads", in_specs=[ANY], out_specs=ANY,
        out_shape=jax.ShapeDtypeStruct((8, r, 128), v.dtype),
        scratch_shapes=[pltpu.SemaphoreType.DMA((7,)), pltpu.SemaphoreType.DMA((7,)), pltpu.SemaphoreType.DMA],
    )(v)


def _sum_slots(parts, name):
    n, r, _ = parts.shape
    tr = _tile(r, 1024, 8)

    def body(p_ref, o_ref):
        acc = p_ref[0]
        for j in range(1, n):
            acc = acc + p_ref[j]
        o_ref[...] = acc

    return pl.pallas_call(
        body, name=name, grid=(r // tr,), in_specs=[pl.BlockSpec((n, tr, 128), lambda i: (0, i, 0))],
        out_specs=pl.BlockSpec((tr, 128), lambda i: (i, 0)), out_shape=jax.ShapeDtypeStruct((r, 128), parts.dtype),
        compiler_params=_params("parallel"),
    )(parts)


def _add_sibling_half(part, recv, c, name):
    _, n, rh, cols = recv.shape
    tr = _tile(rh, 1024, 16)
    nblk = rh // tr

    def body(c_ref, p_ref, r_ref, o_ref):
        o_ref[...] = (p_ref[...].astype(F32) + r_ref[...].astype(F32)).astype(o_ref.dtype)

    blk = (None, None, tr, cols)
    return pl.pallas_call(
        body, name=name,
        grid_spec=pltpu.PrefetchScalarGridSpec(
            num_scalar_prefetch=1, grid=(4, n, nblk),
            in_specs=[pl.BlockSpec(blk, lambda j, l, i, c_ref: (j, l, c_ref[0] * nblk + i, 0)),
                      pl.BlockSpec(blk, lambda j, l, i, c_ref: (j, l, i, 0))],
            out_specs=pl.BlockSpec(blk, lambda j, l, i, c_ref: (j, l, i, 0))),
        out_shape=jax.ShapeDtypeStruct(recv.shape, recv.dtype),
        compiler_params=_params("parallel", "parallel", "parallel"),
    )(c, part, recv)


def _add_chips(t, recv, me, name):
    shape = t.shape[1:]
    t, recv = _rows_view(t, 1), _rows_view(recv, 1)
    _, rh, cols = t.shape
    tr = _tile(rh, 1024, 16)

    def body(me_ref, t_ref, r_ref, o_ref):
        f = lambda v: v.astype(F32)
        o_ref[...] = ((f(t_ref[...]) + f(r_ref[0])) + f(r_ref[1])) + f(r_ref[2])

    return pl.pallas_call(
        body, name=name,
        grid_spec=pltpu.PrefetchScalarGridSpec(
            num_scalar_prefetch=1, grid=(rh // tr,),
            in_specs=[pl.BlockSpec((None, tr, cols), lambda i, me_ref: (me_ref[0], i, 0)),
                      pl.BlockSpec((3, tr, cols), lambda i, me_ref: (0, i, 0))],
            out_specs=pl.BlockSpec((tr, cols), lambda i, me_ref: (i, 0))),
        out_shape=jax.ShapeDtypeStruct((rh, cols), F32),
        compiler_params=_params("parallel"),
    )(me, t, recv).reshape(shape)


def _reduce_pairs(parts):
    c1 = jnp.reshape(lax.axis_index("c"), (1,)).astype(jnp.int32)
    return [_add_sibling_half(p, r, c1, "grad_pair_sum_%d" % i) for i, (p, r) in enumerate(zip(parts, _pair_exchange(parts)))]


def _reduce_finish(ts, recv):
    me = jnp.reshape(2 * lax.axis_index("x") + lax.axis_index("y"), (1,)).astype(jnp.int32)
    return _pair_share([_add_chips(t, r, me, "grad_chip_sum_%d" % i) for i, (t, r) in enumerate(zip(ts, recv))])


def _reduce_scatter(parts):
    ts = _reduce_pairs(parts)
    return _reduce_finish(ts, _chip_exchange(ts))


def _to_rows(parts, mult):
    flat = jnp.concatenate([p.reshape(-1) for p in parts])
    flat = jnp.pad(flat, (0, (-flat.size) % (128 * mult)))
    return flat.reshape(-1, 128)


def _from_rows(rows, shapes):
    flat, out, at = rows.reshape(-1), [], 0
    for shp in shapes:
        n = int(np.prod(shp))
        out.append(flat[at:at + n].reshape(shp))
        at += n
    return out


def _full_from_shards(g, name):
    _, depth, rows, cols = g.shape
    if name in ROW_SHARDED:
        return jnp.moveaxis(g, 0, 1).reshape(depth, 4 * rows, cols)
    return jnp.moveaxis(g, 0, 2).reshape(depth, rows, 4 * cols)


def _shards_from_full(full, name):
    depth, rows, cols = full.shape
    if name in ROW_SHARDED:
        return jnp.moveaxis(full.reshape(depth, 4, rows // 4, cols), 1, 0)
    return jnp.moveaxis(full.reshape(depth, rows, 4, cols // 4), 2, 0)


def kernel(x, w_in, b_gate, q_norm_b, k_norm_b, rpb_c, w_branch_a, w_branch_b, w_branch_c, w_out, ln1_g, ln1_b, w_up, w_down, ln2_g, ln2_b, loss_target, m_w_in, m_b_gate, m_q_norm_b, m_k_norm_b, m_rpb_c, m_w_branch_a, m_w_branch_b, m_w_branch_c, m_w_out, m_ln1_g, m_ln1_b, m_w_up, m_w_down, m_ln2_g, m_ln2_b, v_w_in, v_b_gate, v_q_norm_b, v_k_norm_b, v_rpb_c, v_w_branch_a, v_w_branch_b, v_w_branch_c, v_w_out, v_ln1_g, v_ln1_b, v_w_up, v_w_down, v_ln2_g, v_ln2_b):
    args = dict(locals())
    big_shard = {n: args[n] for n in BIG}
    small = {n: args[n] for n in SMALL}

    shards = [big_shard[n].astype(_MXU) for n in BIG]
    later = shards[0].shape[0] - 1
    first = dict(zip(BIG, _gather_shards(shards, 0, 1)))
    rest = (shards, _gather_out_shapes(shards, later), GATHER_SEMS, _gather_phases(1, later)) if later else None

    sq, grad_x, grads = _local_step(x[0], loss_target[0], first, small, rest)
    loss = lax.psum(0.5 * jnp.sum(sq) / x.shape[-1], ("x", "y", "c"))

    g_big = [grads[n] for n in BIG] if rest else _reduce_scatter([grads[n] for n in BIG])
    small_shapes = [small[n].shape for n in SMALL]
    g_small = _from_rows(_sum_slots(_gather_all(_to_rows([grads[n] for n in SMALL], 8)), "small_grad_sum"), small_shapes)
    grad = dict(zip(BIG, g_big))
    grad.update(zip(SMALL, g_small))

    delta, new_m, new_v = {}, {}, {}
    for n in BIG:
        shp = big_shard[n].shape
        two_d = lambda t: t.reshape(-1, shp[-1])
        res = _adamw(two_d(big_shard[n]), two_d(grad[n]), two_d(args["m_" + n]), two_d(args["v_" + n]), "adamw_" + n)
        delta[n], new_m[n], new_v[n] = [t.reshape(shp) for t in res]
    packed = [_to_rows([args[pre + n] for n in SMALL], 8) for pre in ("", "m_", "v_")]
    res = _adamw(packed[0], _to_rows([grad[n] for n in SMALL], 8), packed[1], packed[2], "adamw_small")
    for dst, rows in zip((delta, new_m, new_v), res):
        dst.update(zip(SMALL, _from_rows(rows, small_shapes)))

    order = ("w_in", "b_gate", "q_norm_b", "k_norm_b", "rpb_c", "w_branch_a", "w_branch_b", "w_branch_c", "w_out",
             "ln1_g", "ln1_b", "w_up", "w_down", "ln2_g", "ln2_b")
    return (loss, grad_x[None], *[grad[n] for n in order], *[delta[n] for n in order],
            *[new_m[n] for n in order], *[new_v[n] for n in order])
```

```python
import functools

import numpy as np
import jax
import jax.numpy as jnp
from jax import lax
from jax.experimental import pallas as pl
from jax.experimental.pallas import tpu as pltpu

F32 = jnp.float32
_MXU = jnp.bfloat16

HEAD = 64
A_W, BQ_W, BKV_W, C_W = 256, 512, 128, 256
QKV_W = 768
A_DILATIONS = (1, 4, 16)
A_RADIUS = 64
A_ROPE_HALF = 8
AX_ROPE_HALF = 16
ROPE_THETA = 500000.0
AX_THETA = 10000.0
GRID_W = 64
C_ROWS = 8
C_COLS = 16
BAND = 128
BT_TILES = 18
LN_EPS = 1e-5
RMS_EPS = 1e-6
NEG = -1e30
SCALE = HEAD ** -0.5
LOG2E = 1.4426950408889634
LN2 = 0.6931471805599453
ADAM_LR, ADAM_B1, ADAM_B2, ADAM_EPS, ADAM_WD, ADAM_STEP = 0.001, 0.9, 0.999, 1e-08, 0.01, 10
V7X_VMEM_LIMIT = 48 * 1024 * 1024
MESH = pl.DeviceIdType.MESH
ANY = pl.BlockSpec(memory_space=pl.ANY)


def _params(*sem):
    return pltpu.CompilerParams(dimension_semantics=sem or None, vmem_limit_bytes=V7X_VMEM_LIMIT)


def _tile(n, pref, align=128):
    if n <= pref:
        return n
    t = (pref // align) * align
    while t >= align:
        if n % t == 0:
            return t
        t -= align
    return n


def _mm(a, b, *, name, mode="nn", outs=((F32),), epilogue=None, extras=(), tm=1024, tn=1024, tk=2048, exact=False,
        b_cols=None, b_off=0, out_chips=False):
    b, b_lead, b_axis = (tuple(b) + (None, None))[:3] if isinstance(b, tuple) else (b, None, None)
    m, k = a.shape if mode != "tn" else a.shape[::-1]
    b_rows = b.shape[-2] * (4 if b_axis == "rows" else 1)
    b_last = b_cols or b.shape[-1] * (4 if b_axis == "cols" else 1)
    k2, n = (b_rows, b_last) if mode != "nt" else (b_last, b_rows)
    assert k == k2, (a.shape, b.shape, mode)
    cap_rows = b.shape[-2] if b_axis == "rows" else None
    cap_cols = b.shape[-1] if b_axis == "cols" else (n // 4 if out_chips else None)
    cap_n, cap_k = (cap_cols, cap_rows) if mode != "nt" else (cap_rows, cap_cols)
    tm, tn, tk = _tile(m, tm), _tile(cap_n or n, min(tn, cap_n or tn)), _tile(cap_k or k, min(tk, cap_k or tk))
    nk = k // tk
    n_ex, n_out = len(extras), len(outs)
    mx = F32 if exact else _MXU
    prec = lax.Precision.HIGHEST if exact else None
    dims = {"nn": (((1,), (0,)), ((), ())), "nt": (((1,), (1,)), ((), ())), "tn": (((0,), (0,)), ((), ()))}[mode]

    def body(*refs):
        a_ref, b_ref = refs[0], refs[1]
        ex = refs[2:2 + n_ex]
        out_refs = refs[2 + n_ex:2 + n_ex + n_out]
        kk = pl.program_id(2)
        av, bv = a_ref[...].astype(mx), b_ref[...].astype(mx)
        part = lax.dot_general(av, bv, dims, preferred_element_type=F32, precision=prec)

        def finish(res):
            vals = epilogue(res, *[e[...] for e in ex]) if epilogue is not None else (res,)
            for o, v in zip(out_refs, vals):
                o[...] = v.astype(o.dtype)

        if nk == 1:
            finish(part)
        else:
            acc = refs[-1]

            @pl.when(kk == 0)
            def _():
                acc[...] = part

            @pl.when((kk > 0) & (kk < nk - 1))
            def _():
                acc[...] += part

            @pl.when(kk == nk - 1)
            def _():
                finish(acc[...] + part)

    a_spec = pl.BlockSpec((tm, tk), lambda i, j, kk: (i, kk)) if mode != "tn" else pl.BlockSpec((tk, tm), lambda i, j, kk: (kk, i))
    b_tile = (tn, tk) if mode == "nt" else (tk, tn)

    def b_index(i, j, kk):
        rc = [j, kk + b_off] if mode == "nt" else [kk, j + b_off]
        if b_axis is None:
            return (() if b_lead is None else (b_lead,)) + tuple(rc)
        ax = 0 if b_axis == "rows" else 1
        per = b.shape[-2 + ax] // b_tile[ax]
        chip, rc[ax] = rc[ax] // per, rc[ax] % per
        return (chip, b_lead) + tuple(rc)

    b_spec = pl.BlockSpec((None,) * (b.ndim - 2) + b_tile, b_index)
    o_spec = pl.BlockSpec((tm, tn), lambda i, j, kk: (i, j))
    if out_chips:
        per_out = n // 4 // tn
        out_specs = [pl.BlockSpec((None, tm, tn), lambda i, j, kk: (j // per_out, i, j % per_out))] * n_out
        out_shape = [jax.ShapeDtypeStruct((4, m, n // 4), d) for d in outs]
    else:
        out_specs, out_shape = [o_spec] * n_out, [jax.ShapeDtypeStruct((m, n), d) for d in outs]
    res = pl.pallas_call(
        body, name=name, grid=(m // tm, n // tn, nk),
        in_specs=[a_spec, b_spec] + [o_spec if e.shape[0] > 1 else pl.BlockSpec((1, tn), lambda i, j, kk: (0, j)) for e in extras],
        out_specs=out_specs, out_shape=out_shape,
        scratch_shapes=[pltpu.VMEM((tm, tn), F32)] if nk > 1 else [],
        compiler_params=_params("parallel", "parallel", "arbitrary"),
    )(a, b, *extras)
    return res[0] if n_out == 1 else res


def _rows(tm, width, cb=0):
    return pl.BlockSpec((tm, width), lambda t: (t, cb))


def _whole(arr):
    nd = arr.ndim
    return pl.BlockSpec(arr.shape, lambda t: (0,) * nd)


def _rowwise(fn, name, rows, tm, ins, outs, into=None):
    into = into or {}
    n_in, n_out, n_into = len(ins), len(outs), len(into)
    dil_in = [spec[1:] if isinstance(spec, tuple) else None for _, spec in ins]
    in_specs = [_rows(tm // spec[1], spec[1] * spec[2]) if isinstance(spec, tuple) else spec for _, spec in ins]
    scratch = [pltpu.VMEM((di[1] // 128, tm, 128), F32) for di in dil_in if di] + \
              [pltpu.VMEM((n // 128, tm, 128), F32) for n, _, kind in outs if isinstance(kind, int)]

    def body(*refs):
        scr = list(refs[n_in + n_into + n_out:])
        blocks = []
        for r, di in zip(refs[:n_in], dil_in):
            if di is None:
                blocks.append(r[...])
            else:
                d, n = di
                s_ref = scr.pop(0)
                for j in range(d):
                    for b in range(n // 128):
                        lanes = slice(j * n + b * 128, j * n + (b + 1) * 128)
                        s_ref.at[b][pl.ds(j, tm // d, stride=d), :] = r[:, lanes].astype(F32)
                blocks.append(jnp.concatenate([s_ref[b] for b in range(n // 128)], 1))
        vals = fn(*blocks)
        first = pl.program_id(0) == 0
        for (ncols, _, kind), o, v in zip(outs, refs[n_in + n_into:n_in + n_into + n_out], vals):
            if kind == "row" or isinstance(kind, tuple):
                o[...] = v.astype(o.dtype)
            elif isinstance(kind, int):
                s_ref = scr.pop(0)
                for b in range(ncols // 128):
                    s_ref[b] = v[:, b * 128:(b + 1) * 128].astype(F32)
                for j in range(kind):
                    for b in range(ncols // 128):
                        lanes = slice(j * ncols + b * 128, j * ncols + (b + 1) * 128)
                        o[:, lanes] = s_ref.at[b][pl.ds(j, tm // kind, stride=kind), :].astype(o.dtype)
            else:
                part = v.reshape(tm // 8, 8, ncols).sum(0)

                @pl.when(first)
                def _(o=o, part=part):
                    o[...] = part

                @pl.when(jnp.logical_not(first))
                def _(o=o, part=part):
                    o[...] += part

    def out_spec(n, kind):
        if kind == "row":
            return _rows(tm, n), (rows, n)
        if isinstance(kind, int):
            return _rows(tm // kind, kind * n), (rows // kind, kind * n)
        if isinstance(kind, tuple):
            return _rows(tm, n, kind[2]), (rows, kind[1])
        return pl.BlockSpec((8, n), lambda t: (0, 0)), (8, n)

    specs = [out_spec(n, kind) for n, _, kind in outs]
    res = pl.pallas_call(
        body, name=name, grid=(rows // tm,),
        in_specs=in_specs + [ANY] * n_into, out_specs=[s for s, _ in specs],
        out_shape=[jax.ShapeDtypeStruct(shp, d) for (_, shp), (_, d, _) in zip(specs, outs)],
        input_output_aliases={n_in + e: o for e, o in enumerate(into)},
        scratch_shapes=scratch, compiler_params=_params("arbitrary"),
    )(*[a for a, _ in ins], *into.values())
    return res


def _lane_lo(width=128):
    return (lax.broadcasted_iota(jnp.int32, (1, width), 1) & (HEAD * 2 - 1)) < HEAD


def _group_sum(x):
    w = x.shape[-1]
    sh = HEAD.bit_length() - 1
    same = (lax.broadcasted_iota(jnp.int32, (w, w), 0) >> sh) == (lax.broadcasted_iota(jnp.int32, (w, w), 1) >> sh)
    ones = jnp.where(same, 1.0, 0.0).astype(jnp.bfloat16)
    hi = x.astype(jnp.bfloat16)
    lo = (x - hi.astype(F32)).astype(jnp.bfloat16)
    return jnp.dot(hi, ones, preferred_element_type=F32) + jnp.dot(lo, ones, preferred_element_type=F32)


def _rot(x, c, sm, sp, shift):
    w = x.shape[-1]
    return x * c + pltpu.roll(x, w - shift, 1) * sm + pltpu.roll(x, shift, 1) * sp


def _rot_t(dy, c, sm, sp, shift):
    w = dy.shape[-1]
    return dy * c + pltpu.roll(dy * sm, shift, 1) + pltpu.roll(dy * sp, w - shift, 1)


def _rope_tables(pos_parts, half, thetas):
    cs, sms, sps = [], [], []
    for pos, theta in zip(pos_parts, thetas):
        inv = theta ** (-jnp.arange(half, dtype=F32) / half)
        ang = pos.astype(F32)[:, None] * inv[None, :]
        co, si, ze = jnp.cos(ang), jnp.sin(ang), jnp.zeros_like(ang)
        cs += [co, co]
        sms += [-si, ze]
        sps += [ze, si]
    return [jnp.concatenate(t, axis=1) for t in (cs, sms, sps)]


def _tables(s):
    pos = jnp.arange(s)
    ca, sma, spa = _rope_tables([pos], A_ROPE_HALF, [ROPE_THETA])
    pad = HEAD - 2 * A_ROPE_HALF
    ca = jnp.concatenate([ca, jnp.ones((s, pad), F32)], 1)
    sma, spa = [jnp.concatenate([t, jnp.zeros((s, pad), F32)], 1) for t in (sma, spa)]
    tab_a = [jnp.tile(t, (1, A_W // HEAD)) for t in (ca, sma, spa)]
    ax = _rope_tables([pos // GRID_W, pos % GRID_W], AX_ROPE_HALF, [AX_THETA, AX_THETA])
    tab_q = [jnp.tile(t, (1, BQ_W // HEAD)) for t in ax]
    tab_k = [jnp.tile(t, (1, BKV_W // HEAD)) for t in ax]
    return tab_a, tab_q, tab_k


def _prep_a(ha, tab, tm):
    s = ha.shape[0]

    def fn(h, c, sm, sp):
        q, k, v = h[:, :A_W], h[:, A_W:2 * A_W], h[:, 2 * A_W:]
        q, k = _rot(q, c, sm, sp, A_ROPE_HALF) * SCALE, _rot(k, c, sm, sp, A_ROPE_HALF)
        return [t for t in (q, k, v) for _ in A_DILATIONS]

    res = _rowwise(fn, "prep_a", s, tm, [(ha, _rows(tm, QKV_W))] + [(t, _rows(tm, A_W)) for t in tab],
                   [(A_W, _MXU, _dil_kind(d)) for _ in range(3) for d in A_DILATIONS])
    n = len(A_DILATIONS)
    return [dict(zip(A_DILATIONS, res[i * n:(i + 1) * n])) for i in range(3)]


def _dil_kind(d):
    return "row" if d == 1 else d


def _dil_spec(d, tm, ncols):
    return _rows(tm, ncols) if d == 1 else ("dil", d, ncols)


def _to_dilations(x, name, tm):
    s, n = x.shape
    res = _rowwise(lambda v: [v for d in A_DILATIONS if d > 1], name, s, tm, [(x, _rows(tm, n))],
                   [(n, x.dtype, d) for d in A_DILATIONS if d > 1])
    return {1: x, **dict(zip([d for d in A_DILATIONS if d > 1], res))}


def _rms(x, g):
    ms = _group_sum(x * x) * (1.0 / HEAD)
    return x * lax.rsqrt(ms + RMS_EPS) * g


def _prep_b(hb, gq, gk, tab_q, tab_k, tm):
    s = hb.shape[0]

    def fn(h, gq, gk, cq, smq, spq, ck, smk, spk):
        xq, xk, v = h[:, :BQ_W], h[:, BQ_W:BQ_W + BKV_W], h[:, BQ_W + BKV_W:]
        q = _rot(_rms(xq, gq), cq, smq, spq, AX_ROPE_HALF) * (SCALE * LOG2E)
        k = _rot(_rms(xk, gk), ck, smk, spk, AX_ROPE_HALF)
        lo = _lane_lo()
        kr, vr = pltpu.roll(k, HEAD, 1), pltpu.roll(v, HEAD, 1)
        kd = jnp.concatenate([jnp.where(lo, k, kr), jnp.where(lo, kr, k)], 1)
        vd = jnp.concatenate([jnp.where(lo, v, vr), jnp.where(lo, vr, v)], 1)
        v1 = jnp.concatenate([jnp.where(lo, v, 1.0), jnp.where(lo, vr, 1.0)], 1)
        return q, kd, vd, v1

    ins = [(hb, _rows(tm, QKV_W)), (gq, _whole(gq)), (gk, _whole(gk))]
    ins += [(t, _rows(tm, BQ_W)) for t in tab_q] + [(t, _rows(tm, BKV_W)) for t in tab_k]
    return _rowwise(fn, "prep_b", s, tm, ins, [(BQ_W, _MXU, "row")] + [(2 * BKV_W, _MXU, "row")] * 3)


def _prep_c(hc, tm):
    def fn(h):
        return h[:, :C_W] * SCALE, h[:, C_W:2 * C_W], h[:, 2 * C_W:]

    return _rowwise(fn, "prep_c", hc.shape[0], tm, [(hc, _rows(tm, QKV_W))], [(C_W, _MXU, "row")] * 3)


def _combine_a(os_, ms, ls, s, tm):
    def fn(o1, o2, o3, m1, m2, m3, l1, l2, l3):
        lo = _lane_lo()
        outs, lses = [], []
        for p in range(A_W // 128):
            st = slice(p * 256, (p + 1) * 256)
            mm = [m[:, st] for m in (m1, m2, m3)]
            ll = [l[:, st] for l in (l1, l2, l3)]
            mmax = jnp.maximum(jnp.maximum(mm[0], mm[1]), mm[2])
            ws = [jnp.exp(m - mmax) for m in mm]
            den = ws[0] * ll[0] + ws[1] * ll[1] + ws[2] * ll[2]
            lses.append(mmax + jnp.log(den))
            num = sum(jnp.where(lo, w[:, :128], w[:, 128:]) * o[:, p * 128:(p + 1) * 128] for w, o in zip(ws, (o1, o2, o3)))
            outs.append(num / jnp.where(lo, den[:, :128], den[:, 128:]))
        o, lse = jnp.concatenate(outs, 1), jnp.concatenate(lses, 1)
        return [o] * len(A_DILATIONS) + [lse] * len(A_DILATIONS)

    ins = [(t, _dil_spec(d, tm, w)) for ts, w in ((os_, A_W), (ms, 2 * A_W), (ls, 2 * A_W)) for t, d in zip(ts, A_DILATIONS)]
    res = _rowwise(fn, "combine_a", s, tm, ins,
                   [(w, F32, _dil_kind(d)) for w in (A_W, 2 * A_W) for d in A_DILATIONS])
    n = len(A_DILATIONS)
    return dict(zip(A_DILATIONS, res[:n])), dict(zip(A_DILATIONS, res[n:]))


def _gates(hg, bg, d):
    return [jax.nn.sigmoid(hg[:, i * d:(i + 1) * d] + bg[:, i * d:(i + 1) * d]) for i in range(3)]


def _gate_merge(hg, bg, pa, pb, pc, tm):
    s, d = pa.shape

    def fn(hg, bg, pa, pb, pc):
        g = _gates(hg, bg, d)
        return (g[0] * pa + g[1] * pb + g[2] * pc,)

    ins = [(hg, _rows(tm, 3 * d)), (bg, _whole(bg))] + [(p, _rows(tm, d)) for p in (pa, pb, pc)]
    return _rowwise(fn, "gate_merge", s, tm, ins, [(d, _MXU, "row")])[0]


def _gate_bwd(dm, hg, bg, pa, pb, pc, tm):
    s, d = pa.shape

    def fn(dm, hg, bg, pa, pb, pc):
        g = _gates(hg, bg, d)
        dlog = jnp.concatenate([dm * p * gi * (1.0 - gi) for p, gi in zip((pa, pb, pc), g)], 1)
        return dm * g[0], dm * g[1], dm * g[2], dlog, dlog

    ins = [(dm, _rows(tm, d)), (hg, _rows(tm, 3 * d)), (bg, _whole(bg))] + [(p, _rows(tm, d)) for p in (pa, pb, pc)]
    return _rowwise(fn, "gate_bwd", s, tm, ins, [(d, _MXU, "row")] * 3 + [(3 * d, _MXU, "row"), (3 * d, F32, "acc")])


def _ln_stats(r):
    mu = jnp.mean(r, -1, keepdims=True)
    xc = r - mu
    var = jnp.mean(xc * xc, -1, keepdims=True)
    rstd = lax.rsqrt(var + LN_EPS)
    return xc * rstd, rstd


def _ln_epilogue(alpha):
    def fn(br, x, g, b):
        r = alpha * x + br
        xhat, _ = _ln_stats(r)
        y = xhat * g + b
        return r, y, y

    return fn


def _ln_bwd(dy, r, g, name, tm):
    s, d = r.shape

    def fn(dy, r, g):
        xhat, rstd = _ln_stats(r)
        dxh = dy * g
        dr = rstd * (dxh - jnp.mean(dxh, -1, keepdims=True) - xhat * jnp.mean(dxh * xhat, -1, keepdims=True))
        return dr, dr, dy * xhat, dy

    ins = [(dy, _rows(tm, d)), (r, _rows(tm, d)), (g, _whole(g))]
    return _rowwise(fn, name, s, tm, ins, [(d, F32, "row"), (d, _MXU, "row"), (d, F32, "acc"), (d, F32, "acc")])


def _loss_head(y, target, tm):
    s, d = y.shape

    def fn(y, t):
        diff = y - t
        return diff * diff, diff * (1.0 / d)

    sq, dy = _rowwise(fn, "loss_head", s, tm, [(y, _rows(tm, d)), (target, _rows(tm, d))], [(d, F32, "acc"), (d, F32, "row")])
    return sq, dy


def _post_a(dqs, dks, dvs, tab, s, tm):
    def fn(q1, q2, q3, k1, k2, k3, v1, v2, v3, c, sm, sp):
        dq = _rot_t((q1 + q2 + q3) * SCALE, c, sm, sp, A_ROPE_HALF)
        dk = _rot_t(k1 + k2 + k3, c, sm, sp, A_ROPE_HALF)
        return (jnp.concatenate([dq, dk, v1 + v2 + v3], 1),)

    ins = [(t, _dil_spec(d, tm, A_W)) for ts in (dqs, dks, dvs) for t, d in zip(ts, A_DILATIONS)]
    ins += [(t, _rows(tm, A_W)) for t in tab]
    return _rowwise(fn, "post_a", s, tm, ins, [(QKV_W, _MXU, ("slot", 3 * QKV_W, 0))])[0]


def _post_b(dq, dkd, dvd, hb, gq, gk, tab_q, tab_k, tm, dh):
    s = dq.shape[0]

    def back(dz, x, g, c, sm, sp):
        dy = _rot_t(dz, c, sm, sp, AX_ROPE_HALF)
        rstd = lax.rsqrt(_group_sum(x * x) * (1.0 / HEAD) + RMS_EPS)
        xh = x * rstd
        dxh = dy * g
        return rstd * (dxh - xh * (_group_sum(dxh * xh) * (1.0 / HEAD))), dy * xh

    def fn(dq, dkd, dvd, h, gq, gk, cq, smq, spq, ck, smk, spk):
        lo = _lane_lo()
        dk = jnp.where(lo, dkd[:, :128], dkd[:, 128:])
        dv = jnp.where(lo, dvd[:, :128], dvd[:, 128:])
        dxq, dgq = back(dq * SCALE, h[:, :BQ_W], gq, cq, smq, spq)
        dxk, dgk = back(dk, h[:, BQ_W:BQ_W + BKV_W], gk, ck, smk, spk)
        return jnp.concatenate([dxq, dxk, dv], 1), dgq, dgk

    ins = [(dq, _rows(tm, BQ_W)), (dkd, _rows(tm, 2 * BKV_W)), (dvd, _rows(tm, 2 * BKV_W)), (hb, _rows(tm, QKV_W)),
           (gq, _whole(gq)), (gk, _whole(gk))]
    ins += [(t, _rows(tm, BQ_W)) for t in tab_q] + [(t, _rows(tm, BKV_W)) for t in tab_k]
    return _rowwise(fn, "post_b", s, tm, ins, [(QKV_W, _MXU, ("slot", 3 * QKV_W, 1)), (BQ_W, F32, "acc"), (BKV_W, F32, "acc")],
                    into={0: dh})


def _post_c(dq, dk, dv, tm, dh):
    def fn(dq, dk, dv):
        return (jnp.concatenate([dq * SCALE, dk, dv], 1),)

    return _rowwise(fn, "post_c", dq.shape[0], tm, [(t, _rows(tm, C_W)) for t in (dq, dk, dv)],
                    [(QKV_W, _MXU, ("slot", 3 * QKV_W, 2))], into={0: dh})[0]


def _adamw(w, g, m, v, name):
    rows, cols = w.shape
    tm = _tile(rows, 256, 8)

    def fn(w, g, m, v):
        m = ADAM_B1 * m + (1.0 - ADAM_B1) * g
        v = ADAM_B2 * v + (1.0 - ADAM_B2) * (g * g)
        m_hat = m / (1.0 - ADAM_B1 ** ADAM_STEP)
        v_hat = v / (1.0 - ADAM_B2 ** ADAM_STEP)
        delta = -ADAM_LR * (m_hat / (jnp.sqrt(v_hat) + ADAM_EPS) + ADAM_WD * w)
        return delta, m, v

    return _rowwise(fn, name, rows, tm, [(t, _rows(tm, cols)) for t in (w, g, m, v)], [(cols, F32, "row")] * 3)


def _dot_t(a, b):
    return lax.dot_general(a, b, (((1,), (1,)), ((), ())), preferred_element_type=F32)


def _tdot(a, b):
    return lax.dot_general(a, b, (((0,), (0,)), ((), ())), preferred_element_type=F32)


def _head_masks():
    lo = _lane_lo()
    return lo, (lo, jnp.logical_not(lo))


def _rep(x, rows):
    return jnp.broadcast_to(x, (rows, 128))


def _row_lo():
    return lax.broadcasted_iota(jnp.int32, (128, 1), 0) < HEAD


def _flash_fwd(q, kd, v1, tq, tk, ride=None):
    s = q.shape[0]
    tq, tk = _tile(s, tq), _tile(s, tk)
    nq, nk = s // tq, s // tk
    mx = _MXU
    n_ride = len(ride[0]) if ride else 0
    steps = BQ_W // 128 * nq * nk

    def body(*refs):
        q_ref, k_ref, v_ref = refs[:3]
        o_ref, lse_ref = refs[3 + n_ride:5 + n_ride]
        m_ref, acc_ref = refs[5 + 2 * n_ride:7 + 2 * n_ride]
        kk = pl.program_id(2)
        if ride:
            comm = (refs[3:3 + n_ride], refs[5 + n_ride:5 + 2 * n_ride], refs[-2], refs[-1])
            step = (pl.program_id(0) * nq + pl.program_id(1)) * nk + kk
            for at, phase in zip((0, steps // 3), ride[3][:2]):
                pl.when(step == at)(functools.partial(phase, *comm))

        @pl.when(kk == 0)
        def _():
            m_ref[...] = jnp.full_like(m_ref, NEG)
            acc_ref[...] = jnp.zeros_like(acc_ref)

        q2, k2, v2 = q_ref[...], k_ref[...], v_ref[...]
        _, masks = _head_masks()
        hs = range(2)
        st = [_dot_t(k2, jnp.where(masks[h], q2, jnp.zeros_like(q2))) for h in hs]
        m_prev = [m_ref[h] for h in hs]
        m_new = [jnp.maximum(m_prev[h], jnp.max(st[h], 0, keepdims=True)) for h in hs]
        p = [jnp.exp2(st[h] - m_new[h]).astype(mx) for h in hs]
        pv = [_tdot(v2, p[h]) for h in hs]
        for h in hs:
            m_ref[h] = m_new[h]
            acc_ref[h] = acc_ref[h] * jnp.exp2(m_prev[h] - m_new[h]) + pv[h]

        @pl.when(kk == nk - 1)
        def _():
            a0, a1 = acc_ref[0], acc_ref[1]
            l0, l1 = a0[HEAD:HEAD + 1], a1[HEAD:HEAD + 1]
            o_ref[...] = jnp.concatenate([a0[:HEAD] / l0, a1[:HEAD] / l1], 0).T
            lse_ref[...] = jnp.concatenate([m_ref[0] + jnp.log2(l0), m_ref[1] + jnp.log2(l1), jnp.zeros((6, tq), F32)], 0)

        if ride:
            pl.when(step == steps - 1)(functools.partial(ride[3][2], *comm))

    ride_in = list(ride[0]) if ride else []
    ride_out = [jax.ShapeDtypeStruct(shp, t.dtype) for shp, t in zip(ride[1], ride[0])] if ride else []
    ride_sems = [pltpu.SemaphoreType.DMA((ride[2], n_ride))] * 2 if ride else []
    res = pl.pallas_call(
        body, name="attn_b_fwd_gather" if ride else "attn_b_fwd", grid=(BQ_W // 128, nq, nk),
        in_specs=[pl.BlockSpec((tq, 128), lambda j, i, kk: (i, j)),
                  pl.BlockSpec((tk, 128), lambda j, i, kk: (kk, j // 2)),
                  pl.BlockSpec((tk, 128), lambda j, i, kk: (kk, j // 2))] + [ANY] * n_ride,
        out_specs=[pl.BlockSpec((tq, 128), lambda j, i, kk: (i, j)), pl.BlockSpec((None, 8, tq), lambda j, i, kk: (j, 0, i))]
        + [ANY] * n_ride,
        out_shape=[jax.ShapeDtypeStruct((s, BQ_W), F32), jax.ShapeDtypeStruct((BQ_W // 128, 8, s), F32)] + ride_out,
        scratch_shapes=[pltpu.VMEM((2, 1, tq), F32), pltpu.VMEM((2, 128, tq), F32)] + ride_sems,
        compiler_params=_params("arbitrary", "arbitrary", "arbitrary") if ride else _params("parallel", "parallel", "arbitrary"),
    )(q, kd, v1, *ride_in)
    return res[0], res[1], res[2:]


def _delta_b(do, o, tq):
    s = do.shape[0]
    tq = _tile(s, tq)

    def body(do_ref, o_ref, d_ref):
        prod = do_ref[...] * o_ref[...]
        row = lax.broadcasted_iota(jnp.int32, (8, 128), 0)
        lane = lax.broadcasted_iota(jnp.int32, (8, 128), 1)
        sel = jnp.where(((row == 0) & (lane < HEAD)) | ((row == 1) & (lane >= HEAD)), 1.0, 0.0).astype(F32)
        d_ref[...] = lax.dot_general(sel, prod, (((1,), (1,)), ((), ())), preferred_element_type=F32,
                                     precision=lax.Precision.HIGHEST)

    qs = pl.BlockSpec((tq, 128), lambda j, i: (i, j))
    return pl.pallas_call(
        body, name="attn_b_delta", grid=(BQ_W // 128, s // tq), in_specs=[qs, qs],
        out_specs=pl.BlockSpec((None, 8, tq), lambda j, i: (j, 0, i)),
        out_shape=jax.ShapeDtypeStruct((BQ_W // 128, 8, s), F32),
        compiler_params=_params("parallel", "parallel"),
    )(do, o)


def _flash_bwd(q, kd, vd, do, lse, delta, tq, tk, ride=None):
    s = q.shape[0]
    tq, tk = _tile(s, tq), _tile(s, tk)
    nq, nk = s // tq, s // tk
    group = BQ_W // 128 // 2
    mx = _MXU
    n_ride = len(ride[0]) if ride else 0
    steps = BKV_W // HEAD * nk * group * nq

    def body(*refs):
        k_ref, v_ref, q_ref, do_ref, lse_ref, dl_ref = refs[:6]
        dq_hbm, dk_ref, dv_ref = refs[6 + n_ride:9 + n_ride]
        dk_acc, dv_acc, dqt, stage, sem = refs[9 + 2 * n_ride:14 + 2 * n_ride]
        e, kk, jj, i = pl.program_id(0), pl.program_id(1), pl.program_id(2), pl.program_id(3)
        if ride:
            comm = (refs[6:6 + n_ride], refs[9 + n_ride:9 + 2 * n_ride], refs[-2], refs[-1])
            step = ((e * nk + kk) * group + jj) * nq + i
            for at, phase in zip((0, steps // 3), ride[3][:2]):
                pl.when(step == at)(functools.partial(phase, *comm))

        @pl.when((jj == 0) & (i == 0))
        def _():
            dk_acc[...] = jnp.zeros_like(dk_acc)
            dv_acc[...] = jnp.zeros_like(dv_acc)

        @pl.when(kk == 0)
        def _():
            dqt[jj, i] = jnp.zeros((128, tq), F32)

        q2, k2, v2, do2 = q_ref[...], k_ref[...], v_ref[...], do_ref[...].astype(mx)
        lse8, dl8 = lse_ref[...], dl_ref[...]
        _, masks = _head_masks()
        hs = range(2)
        qh = [jnp.where(masks[h], q2, jnp.zeros_like(q2)) for h in hs]
        doh = [jnp.where(masks[h], do2, jnp.zeros_like(do2)) for h in hs]
        st = [_dot_t(k2, qh[h]) for h in hs]
        dpt = [_dot_t(v2, doh[h]) for h in hs]
        p = [jnp.exp2(st[h] - lse8[h:h + 1]) for h in hs]
        ds = [(p[h] * (dpt[h] - dl8[h:h + 1])).astype(mx) for h in hs]
        p = [p[h].astype(mx) for h in hs]
        dv_acc[...] += jnp.dot(p[0], doh[0], preferred_element_type=F32) + jnp.dot(p[1], doh[1], preferred_element_type=F32)
        dk_acc[...] += jnp.dot(ds[0], qh[0], preferred_element_type=F32) + jnp.dot(ds[1], qh[1], preferred_element_type=F32)
        dqt[jj, i] += jnp.where(_row_lo(), _tdot(k2, ds[0]), _tdot(k2, ds[1]))

        @pl.when(kk == nk - 1)
        def _():
            stage[...] = dqt[jj, i].T
            lane0 = pl.multiple_of((group * e + jj) * 128, 128)
            cp = pltpu.make_async_copy(stage, dq_hbm.at[pl.ds(pl.multiple_of(i * tq, tq), tq), pl.ds(lane0, 128)], sem)
            cp.start()
            cp.wait()

        @pl.when((jj == group - 1) & (i == nq - 1))
        def _():
            dk_ref[...] = (dk_acc[...] + pltpu.roll(dk_acc[...], HEAD, 1)) * LN2
            dv_ref[...] = dv_acc[...] + pltpu.roll(dv_acc[...], HEAD, 1)

        if ride:
            pl.when(step == steps - 1)(functools.partial(ride[3][2], *comm))

    ks = pl.BlockSpec((tk, 128), lambda e, kk, jj, i: (kk, e))
    qs = pl.BlockSpec((tq, 128), lambda e, kk, jj, i: (i, group * e + jj))
    st = pl.BlockSpec((None, 8, tq), lambda e, kk, jj, i: (group * e + jj, 0, i))
    ride_in = list(ride[0]) if ride else []
    ride_out = [jax.ShapeDtypeStruct(shp, t.dtype) for shp, t in zip(ride[1], ride[0])] if ride else []
    ride_sems = [pltpu.SemaphoreType.DMA((ride[2], n_ride))] * 2 if ride else []
    res = pl.pallas_call(
        body, name="attn_b_bwd_exchange" if ride else "attn_b_bwd", grid=(BKV_W // HEAD, nk, group, nq),
        in_specs=[ks, ks, qs, qs, st, st] + [ANY] * n_ride, out_specs=[ANY, ks, ks] + [ANY] * n_ride,
        out_shape=[jax.ShapeDtypeStruct((s, BQ_W), F32)] + [jax.ShapeDtypeStruct((s, 2 * BKV_W), F32)] * 2 + ride_out,
        scratch_shapes=[pltpu.VMEM((tk, 128), F32)] * 2 + [pltpu.VMEM((group, nq, 128, tq), F32), pltpu.VMEM((tq, 128), F32),
                                                          pltpu.SemaphoreType.DMA] + ride_sems,
        compiler_params=_params("arbitrary", "arbitrary", "arbitrary", "arbitrary"),
    )(kd, vd, q, do, lse, delta, *ride_in)
    return res[0], res[1], res[2], res[3:]


def _p_and_ds(items, masks):
    mx = _MXU
    keys = [(n, h) for n in range(len(items)) for h in range(2)]
    qh = {(n, h): jnp.where(masks[h], items[n][0], jnp.zeros_like(items[n][0])) for n, h in keys}
    doh = {(n, h): jnp.where(masks[h], items[n][3], jnp.zeros_like(items[n][3])) for n, h in keys}
    sc = {}
    for n, h in keys:
        s_h = _dot_t(qh[n, h], items[n][1])
        if items[n][7] is not None:
            s_h = s_h + items[n][7][h]
        sc[n, h] = jnp.where(items[n][6], s_h, NEG)
    dp = {(n, h): _dot_t(doh[n, h].astype(mx), items[n][2]) for n, h in keys}
    lse = {(n, h): jnp.max(items[n][5][:, h * 128:(h + 1) * 128], -1, keepdims=True) for n, h in keys}
    delta = {(n, h): jnp.sum(doh[n, h] * items[n][4], -1, keepdims=True) for n, h in keys}
    p = {key: jnp.exp(sc[key] - lse[key]) for key in keys}
    ds = {key: p[key] * (dp[key] - delta[key]) for key in keys}
    return [[(qh[n, h], p[n, h], ds[n, h], doh[n, h]) for h in range(2)] for n in range(len(items))]


class _BandA:
    hb, inner, has_bias, name = 1, 1, False, "a"

    def __init__(self, nb):
        self.nb = nb

    def mask(self, qidx, kidx):
        n = self.nb * BAND
        return (jnp.abs(qidx - kidx) <= A_RADIUS) & (kidx >= 0) & (kidx < n) & (qidx >= 0) & (qidx < n)


class _BandC:
    hb, inner, has_bias, name = 3, 2, True, "c"

    def __init__(self, nb):
        self.nb = nb
        self.rows = nb * BAND // GRID_W
        per = BAND // GRID_W
        assert self.rows >= C_ROWS and (C_ROWS - 1) // per <= self.hb
        assert (self.rows - 1) // per - (self.rows - C_ROWS) // per <= self.hb

    def mask(self, qidx, kidx):
        n = self.nb * BAND
        sh = GRID_W.bit_length() - 1
        qrow, cq = qidx >> sh, qidx & (GRID_W - 1)
        krow, ck = kidx >> sh, kidx & (GRID_W - 1)
        r0 = jnp.clip(qrow - C_ROWS // 2, 0, self.rows - C_ROWS)
        c0 = jnp.clip(cq - C_COLS // 2, 0, GRID_W - C_COLS)
        ok = (qidx >= 0) & (qidx < n) & (kidx >= 0) & (kidx < n)
        return ok & (krow >= r0) & (krow < r0 + C_ROWS) & (ck >= c0) & (ck < c0 + C_COLS)


def _bias_tile(off, a):
    return (BAND // GRID_W) * off - a + (C_ROWS - 1) + 2


def _band_bias_k(band, bt_ref, h, hw):
    per = BAND // GRID_W
    return jnp.concatenate([jnp.concatenate([bt_ref[h, _bias_tile(off, a)] for off in range(-hw, hw + 1)], 1)
                            for a in range(per)], 0)


def _band_bias_q(band, bt_ref, h, hw):
    per = BAND // GRID_W
    return jnp.concatenate([bt_ref[h, _bias_tile(-off, a)] for off in range(-hw, hw + 1) for a in range(per)], 0)


def _band_halfwidths(band, i, rb, nb, of_keys):
    if band.inner == band.hb:
        return [(None, band.hb)]
    lo, hi = i * rb, i * rb + rb - 1
    edge = ((lo <= band.hb) | (hi >= nb - 1 - band.hb)) if of_keys else ((lo == 0) | (hi == nb - 1))
    return [(edge, band.hb), (jnp.logical_not(edge), band.inner)]


def _band_split(nb, ncb):
    cb = max(c for c in (4, 2, 1) if ncb % c == 0)
    rb = max(r for r in (4, 2, 1) if nb % r == 0 and r * cb <= 16)
    return rb, cb


def _band_specs(band, rb, cb, nb, width):
    def edge(first):
        return pl.BlockSpec((BAND, cb * width), lambda c, i: (jnp.clip(i * rb + first, 0, nb - 1), c))

    main = pl.BlockSpec((rb * BAND, cb * width), lambda c, i: (i, c))
    return [edge(t - band.hb) for t in range(band.hb)] + [main] + [edge(rb + t) for t in range(band.hb)]


def _band_rows(band, refs, rb, r, lanes, hw):
    hb = band.hb
    parts = []
    for b in range(r + hb - hw, r + hb + hw + 1):
        if b < hb:
            parts.append(refs[b][:, lanes])
        elif b < hb + rb:
            parts.append(refs[hb][(b - hb) * BAND:(b - hb + 1) * BAND, lanes])
        else:
            parts.append(refs[b - rb + 1][:, lanes])
    return jnp.concatenate(parts, 0)


def _band_idx(band, blk, rows_of_blocks, axis):
    shape = (rows_of_blocks * BAND, 1) if axis == 0 else (1, rows_of_blocks * BAND)
    return blk * BAND + lax.broadcasted_iota(jnp.int32, shape, axis)


def _band_fwd(band, q, k, v, bt=None):
    n, w = q.shape
    nb, ncb, nband = n // BAND, w // 128, 2 * band.hb + 1
    rb, cb = _band_split(nb, ncb)
    mx = _MXU
    raw = not band.has_bias

    def body(*refs):
        q_ref, k_refs, v_refs = refs[0], refs[1:1 + nband], refs[1 + nband:1 + 2 * nband]
        rest = refs[1 + 2 * nband:]
        bt_ref = rest[0] if band.has_bias else None
        outs = rest[1:] if band.has_bias else rest
        i = pl.program_id(1)
        lo, masks = _head_masks()
        subs = [(r, c) for r in range(rb) for c in range(cb)]
        lanes = {c: slice(c * 128, (c + 1) * 128) for c in range(cb)}
        rows = {r: slice(r * BAND, (r + 1) * BAND) for r in range(rb)}

        def compute(hw):
            mask = {r: band.mask(_band_idx(band, i * rb + r, 1, 0), _band_idx(band, i * rb + r - hw, 2 * hw + 1, 1))
                    for r in range(rb)}
            sc = {}
            for r, c in subs:
                q2, kcat = q_ref[rows[r], lanes[c]], _band_rows(band, k_refs, rb, r, lanes[c], hw)
                for h in range(2):
                    s_h = _dot_t(jnp.where(masks[h], q2, jnp.zeros_like(q2)), kcat)
                    if band.has_bias:
                        s_h = s_h + _band_bias_k(band, bt_ref, 2 * c + h, hw)
                    sc[r, c, h] = jnp.where(mask[r], s_h, NEG)
            ms = {key: jnp.max(s_h, -1, keepdims=True) for key, s_h in sc.items()}
            ps = {key: jnp.exp(s_h - ms[key]) for key, s_h in sc.items()}
            ls = {key: jnp.sum(p, -1, keepdims=True) for key, p in ps.items()}
            os_ = {}
            for r, c in subs:
                vcat = _band_rows(band, v_refs, rb, r, lanes[c], hw)
                for h in range(2):
                    os_[r, c, h] = jnp.dot(ps[r, c, h].astype(mx), vcat, preferred_element_type=F32)
            for r, c in subs:
                st_lanes = [slice(c * 256 + h * 128, c * 256 + (h + 1) * 128) for h in range(2)]
                if raw:
                    o_ref, m_ref, l_ref = outs
                    o_ref[rows[r], lanes[c]] = jnp.where(lo, os_[r, c, 0], os_[r, c, 1])
                    for h in range(2):
                        m_ref[rows[r], st_lanes[h]] = _rep(ms[r, c, h], BAND)
                        l_ref[rows[r], st_lanes[h]] = _rep(ls[r, c, h], BAND)
                else:
                    o_ref, lse_ref = outs
                    o_ref[rows[r], lanes[c]] = jnp.where(lo, os_[r, c, 0] / ls[r, c, 0], os_[r, c, 1] / ls[r, c, 1])
                    for h in range(2):
                        lse_ref[rows[r], st_lanes[h]] = _rep(ms[r, c, h] + jnp.log(ls[r, c, h]), BAND)

        for pred, hw in _band_halfwidths(band, i, rb, nb, False):
            compute(hw) if pred is None else pl.when(pred)(functools.partial(compute, hw))

    qs = pl.BlockSpec((rb * BAND, cb * 128), lambda c, i: (i, c))
    ks = _band_specs(band, rb, cb, nb, 128)
    st = pl.BlockSpec((rb * BAND, cb * 256), lambda c, i: (i, c))
    in_specs, args = [qs] + ks + ks, [q] + [k] * nband + [v] * nband
    if band.has_bias:
        in_specs.append(pl.BlockSpec((2 * cb, BT_TILES, GRID_W, 128), lambda c, i: (c, 0, 0, 0)))
        args.append(bt)
    n_stats = 2 if raw else 1
    return pl.pallas_call(
        body, name="attn_%s_fwd" % band.name, grid=(ncb // cb, nb // rb), in_specs=in_specs,
        out_specs=[qs] + [st] * n_stats,
        out_shape=[jax.ShapeDtypeStruct((n, w), F32)] + [jax.ShapeDtypeStruct((n, 2 * w), F32)] * n_stats,
        compiler_params=_params("parallel", "arbitrary"),
    )(*args)


def _band_dq(band, q, k, v, do, o, lse, bt=None):
    n, w = q.shape
    nb, ncb, nband = n // BAND, w // 128, 2 * band.hb + 1
    rb, cb = _band_split(nb, ncb)
    mx = _MXU
    per = BAND // GRID_W

    def body(*refs):
        q_ref, k_refs, v_refs = refs[0], refs[1:1 + nband], refs[1 + nband:1 + 2 * nband]
        do_ref, o_ref, lse_ref = refs[1 + 2 * nband:4 + 2 * nband]
        rest = refs[4 + 2 * nband:]
        dq_ref = rest[1] if band.has_bias else rest[0]
        i = pl.program_id(1)
        lo, masks = _head_masks()
        if band.has_bias:
            dbt_ref = rest[2]

            @pl.when(i == 0)
            def _():
                dbt_ref[...] = jnp.zeros_like(dbt_ref)

        subs = [(r, c) for r in range(rb) for c in range(cb)]

        def compute(hw):
            mask = {r: band.mask(_band_idx(band, i * rb + r, 1, 0), _band_idx(band, i * rb + r - hw, 2 * hw + 1, 1))
                    for r in range(rb)}
            items, kcats = [], []
            for r, c in subs:
                lanes, rows = slice(c * 128, (c + 1) * 128), slice(r * BAND, (r + 1) * BAND)
                kcats.append(_band_rows(band, k_refs, rb, r, lanes, hw))
                bias = [_band_bias_k(band, rest[0], 2 * c + h, hw) for h in range(2)] if band.has_bias else None
                items.append((q_ref[rows, lanes], kcats[-1], _band_rows(band, v_refs, rb, r, lanes, hw), do_ref[rows, lanes],
                              o_ref[rows, lanes], lse_ref[rows, c * 256:(c + 1) * 256], mask[r], bias))
            res = _p_and_ds(items, masks)
            dqs = [[jnp.dot(ds.astype(mx), kcat, preferred_element_type=F32) for _, _, ds, _ in hs]
                   for hs, kcat in zip(res, kcats)]
            for (r, c), hs, dq in zip(subs, res, dqs):
                dq_ref[r * BAND:(r + 1) * BAND, c * 128:(c + 1) * 128] = jnp.where(lo, dq[0], dq[1])
                if band.has_bias:
                    for h in range(2):
                        ds = hs[h][2]
                        for a in range(per):
                            for t in range(2 * hw + 1):
                                tile = ds[a * GRID_W:(a + 1) * GRID_W, t * 128:(t + 1) * 128]
                                dbt_ref[2 * c + h, _bias_tile(t - hw, a)] += tile

        for pred, hw in _band_halfwidths(band, i, rb, nb, False):
            compute(hw) if pred is None else pl.when(pred)(functools.partial(compute, hw))

    qs = pl.BlockSpec((rb * BAND, cb * 128), lambda c, i: (i, c))
    ks = _band_specs(band, rb, cb, nb, 128)
    st = pl.BlockSpec((rb * BAND, cb * 256), lambda c, i: (i, c))
    in_specs, args = [qs] + ks + ks + [qs, qs, st], [q] + [k] * nband + [v] * nband + [do, o, lse]
    out_specs, out_shape = [qs], [jax.ShapeDtypeStruct((n, w), F32)]
    if band.has_bias:
        bts = pl.BlockSpec((2 * cb, BT_TILES, GRID_W, 128), lambda c, i: (c, 0, 0, 0))
        in_specs.append(bts)
        args.append(bt)
        out_specs.append(bts)
        out_shape.append(jax.ShapeDtypeStruct(bt.shape, F32))
    return pl.pallas_call(
        body, name="attn_%s_dq" % band.name, grid=(ncb // cb, nb // rb), in_specs=in_specs, out_specs=out_specs,
        out_shape=out_shape, compiler_params=_params("parallel", "arbitrary"),
    )(*args)


def _band_dkv(band, q, k, v, do, o, lse, bt=None):
    n, w = q.shape
    nb, ncb, nband = n // BAND, w // 128, 2 * band.hb + 1
    rb, cb = _band_split(nb, ncb)
    mx = _MXU

    def body(*refs):
        k_ref, v_ref = refs[0], refs[1]
        q_refs, do_refs, o_refs, lse_refs = [refs[2 + g * nband:2 + (g + 1) * nband] for g in range(4)]
        rest = refs[2 + 4 * nband:]
        dk_ref, dv_ref = rest[-2], rest[-1]
        i = pl.program_id(1)
        _, masks = _head_masks()
        subs = [(r, c) for r in range(rb) for c in range(cb)]

        def compute(hw):
            mask = {r: band.mask(_band_idx(band, i * rb + r - hw, 2 * hw + 1, 0), _band_idx(band, i * rb + r, 1, 1))
                    for r in range(rb)}
            items = []
            for r, c in subs:
                lanes, rows = slice(c * 128, (c + 1) * 128), slice(r * BAND, (r + 1) * BAND)
                qcat, docat, ocat = [_band_rows(band, g, rb, r, lanes, hw) for g in (q_refs, do_refs, o_refs)]
                lsecat = _band_rows(band, lse_refs, rb, r, slice(c * 256, (c + 1) * 256), hw)
                bias = [_band_bias_q(band, rest[0], 2 * c + h, hw) for h in range(2)] if band.has_bias else None
                items.append((qcat, k_ref[rows, lanes], v_ref[rows, lanes], docat, ocat, lsecat, mask[r], bias))
            res = _p_and_ds(items, masks)
            dks = [sum(_tdot(ds.astype(mx), qh) for qh, _, ds, _ in hs) for hs in res]
            dvs = [sum(_tdot(p.astype(mx), doh.astype(mx)) for _, p, _, doh in hs) for hs in res]
            for (r, c), dk, dv in zip(subs, dks, dvs):
                dk_ref[r * BAND:(r + 1) * BAND, c * 128:(c + 1) * 128] = dk
                dv_ref[r * BAND:(r + 1) * BAND, c * 128:(c + 1) * 128] = dv

        for pred, hw in _band_halfwidths(band, i, rb, nb, True):
            compute(hw) if pred is None else pl.when(pred)(functools.partial(compute, hw))

    ks = pl.BlockSpec((rb * BAND, cb * 128), lambda c, i: (i, c))
    in_specs = [ks, ks] + _band_specs(band, rb, cb, nb, 128) * 3 + _band_specs(band, rb, cb, nb, 256)
    args = [k, v] + [q] * nband + [do] * nband + [o] * nband + [lse] * nband
    if band.has_bias:
        in_specs.append(pl.BlockSpec((2 * cb, BT_TILES, GRID_W, 128), lambda c, i: (c, 0, 0, 0)))
        args.append(bt)
    return pl.pallas_call(
        body, name="attn_%s_dkv" % band.name, grid=(ncb // cb, nb // rb), in_specs=in_specs, out_specs=[ks, ks],
        out_shape=[jax.ShapeDtypeStruct((n, w), F32)] * 2,
        compiler_params=_params("parallel", "arbitrary"),
    )(*args)


def _dc_onehot():
    c = np.arange(GRID_W)
    dc = np.clip(c[None, :] - c[:, None] + (C_COLS - 1), 0, 2 * C_COLS - 2).reshape(-1)
    m = np.zeros((GRID_W * GRID_W, 128), np.float32)
    m[np.arange(dc.size), dc] = 1.0
    return m


def _bias_tiles(rpb):
    h, nr, ncol = rpb.shape
    flat = jnp.pad(rpb.reshape(h * nr, ncol), ((0, (-h * nr) % 8), (0, 128 - ncol)))
    tiles = _mm(flat, jnp.asarray(_dc_onehot().T), name="rpb_tiles", exact=True, tn=GRID_W * GRID_W)
    tiles = tiles[:h * nr].reshape(h, nr, GRID_W, GRID_W)
    tiles = jnp.pad(tiles, ((0, 0), (2, BT_TILES + 1 - nr - 2), (0, 0), (0, 0)))
    return jnp.concatenate([tiles[:, :BT_TILES], tiles[:, 1:BT_TILES + 1]], -1)


def _bias_tiles_grad(dbt, nr, ncol):
    h = dbt.shape[0]
    d = dbt[:, 2:2 + nr, :, :GRID_W] + dbt[:, 1:1 + nr, :, GRID_W:]
    flat = jnp.pad(d.reshape(h * nr, GRID_W * GRID_W), ((0, (-h * nr) % 8), (0, 0)))
    g = _mm(flat, jnp.asarray(_dc_onehot()), name="rpb_grad", exact=True, tk=GRID_W * GRID_W)
    return g[:h * nr, :ncol].reshape(h, nr, ncol)


TM = 512
TQ_B, TK_B = 1024, 2048


def _relu2(acc):
    r = jnp.maximum(acc, 0.0)
    return (r * r,)


def _layer_fwd(x, xb, w, sm, tabs, alpha, ride=None):
    tab_a, tab_q, tab_k = tabs
    s, d = x.shape
    ha = _mm(xb, w["in"], name="in_a", tn=QKV_W, b_cols=QKV_W, b_off=0)
    hb = _mm(xb, w["in"], name="in_b", tn=QKV_W, b_cols=QKV_W, b_off=1)
    hc = _mm(xb, w["in"], name="in_c", tn=QKV_W, b_cols=QKV_W, b_off=2)
    hg = _mm(xb, w["in"], name="in_g", outs=(_MXU,), tn=QKV_W, b_cols=3 * d, b_off=3)

    qa, ka, va = _prep_a(ha, tab_a, TM)
    stats = [_band_fwd(_BandA(s // dil // BAND), qa[dil], ka[dil], va[dil]) for dil in A_DILATIONS]
    oas, lse_a = _combine_a(*zip(*stats), s, TM)
    oa = oas[1]

    qb, kd, vd, v1 = _prep_b(hb, sm["q_norm"], sm["k_norm"], tab_q, tab_k, TM)
    ob, lse_b, rode = _flash_fwd(qb, kd, v1, TQ_B, TK_B, ride)

    qc, kc, vc = _prep_c(hc, TM)
    bt = _bias_tiles(sm["rpb"])
    oc, lse_c = _band_fwd(_BandC(s // BAND), qc, kc, vc, bt)

    pa = _mm(oa, w["br_a"], name="br_a", outs=(_MXU,))
    pb = _mm(ob, w["br_b"], name="br_b", outs=(_MXU,))
    pc = _mm(oc, w["br_c"], name="br_c", outs=(_MXU,))
    merged = _gate_merge(hg, sm["b_gate"], pa, pb, pc, TM)
    ln = dict(outs=(F32, F32, _MXU), epilogue=_ln_epilogue(alpha), tn=d)
    r1, x1, x1b = _mm(merged, w["out"], name="w_out_ln1", extras=(x, sm["ln1_g"], sm["ln1_b"]), **ln)
    act = _mm(x1b, w["up"], name="w_up", outs=(_MXU,), epilogue=_relu2)
    r2, x2, x2b = _mm(act, w["down"], name="w_down_ln2", extras=(x1, sm["ln2_g"], sm["ln2_b"]), **ln)
    saved = dict(xb=xb, hb=hb, hg=hg, qa=qa, ka=ka, va=va, oa=oa, oas=oas, lse_a=lse_a, qb=qb, kd=kd, vd=vd, ob=ob, lse_b=lse_b,
                 qc=qc, kc=kc, vc=vc, oc=oc, lse_c=lse_c, bt=bt, pa=pa, pb=pb, pc=pc, merged=merged, r1=r1, x1b=x1b,
                 act=act, r2=r2)
    return x2, x2b, saved, rode


def _layer_bwd(dx2, w, sm, sv, tabs, alpha, ride=None):
    tab_a, tab_q, tab_k = tabs
    s, d = dx2.shape
    g = {}
    dr2, dr2b, dg2, db2 = _ln_bwd(dx2, sv["r2"], sm["ln2_g"], "ln2_bwd", TM)
    g["ln2_g"], g["ln2_b"] = dg2.sum(0), db2.sum(0)
    du = _mm(dr2b, w["down"], mode="nt", name="d_act", outs=(_MXU,), extras=(sv["act"],),
             epilogue=lambda acc, act: (acc * (2.0 * jnp.sqrt(act.astype(F32))),))
    g["w_down"] = _mm(sv["act"], dr2b, mode="tn", name="g_w_down", outs=(_MXU,)).reshape(4, -1, d)
    g["w_up"] = _mm(sv["x1b"], du, mode="tn", name="g_w_up", outs=(_MXU,), out_chips=True)
    dx1 = _mm(du, w["up"], mode="nt", name="d_x1", extras=(dr2,), epilogue=lambda acc, e: (acc + alpha * e,))
    dr1, dr1b, dg1, db1 = _ln_bwd(dx1, sv["r1"], sm["ln1_g"], "ln1_bwd", TM)
    g["ln1_g"], g["ln1_b"] = dg1.sum(0), db1.sum(0)
    g["w_out"] = _mm(sv["merged"], dr1b, mode="tn", name="g_w_out", outs=(_MXU,))
    dmerged = _mm(dr1b, w["out"], mode="nt", name="d_merged")
    dpa, dpb, dpc, dlog, gb = _gate_bwd(dmerged, sv["hg"], sm["b_gate"], sv["pa"], sv["pb"], sv["pc"], TM)
    g["b_gate"] = gb.sum(0)
    g["w_branch_a"] = _mm(sv["oa"], dpa, mode="tn", name="g_br_a", outs=(_MXU,))
    g["w_branch_b"] = _mm(sv["ob"], dpb, mode="tn", name="g_br_b", outs=(_MXU,))
    g["w_branch_c"] = _mm(sv["oc"], dpc, mode="tn", name="g_br_c", outs=(_MXU,))
    doa = _mm(dpa, w["br_a"], mode="nt", name="d_oa")
    dob = _mm(dpb, w["br_b"], mode="nt", name="d_ob")
    doc = _mm(dpc, w["br_c"], mode="nt", name="d_oc")

    dqs, dks, dvs = [], [], []
    doas = _to_dilations(doa, "d_oa_layouts", TM)
    for dil in A_DILATIONS:
        band = _BandA(s // dil // BAND)
        args = [t[dil] for t in (sv["qa"], sv["ka"], sv["va"], doas, sv["oas"], sv["lse_a"])]
        dqs.append(_band_dq(band, *args)[0])
        dk_c, dv_c = _band_dkv(band, *args)
        dks.append(dk_c)
        dvs.append(dv_c)
    dh = _post_a(dqs, dks, dvs, tab_a, s, TM)

    dqb, dkd, dvd, rode = _flash_bwd(sv["qb"], sv["kd"], sv["vd"], dob, sv["lse_b"], _delta_b(dob, sv["ob"], TQ_B),
                                     TQ_B, TK_B, ride)
    dh, gq, gk = _post_b(dqb, dkd, dvd, sv["hb"], sm["q_norm"], sm["k_norm"], tab_q, tab_k, TM, dh)
    g["q_norm_b"] = gq.sum(0).reshape(-1, HEAD).sum(0)
    g["k_norm_b"] = gk.sum(0).reshape(-1, HEAD).sum(0)

    band_c = _BandC(s // BAND)
    cargs = (sv["qc"], sv["kc"], sv["vc"], doc, sv["oc"], sv["lse_c"], sv["bt"])
    dqc, dbt = _band_dq(band_c, *cargs)
    dkc, dvc = _band_dkv(band_c, *cargs)
    dh = _post_c(dqc, dkc, dvc, TM, dh)
    g["rpb_c"] = _bias_tiles_grad(dbt, 2 * C_ROWS - 1, 2 * C_COLS - 1)

    xb = sv["xb"]
    g["w_in"] = jnp.concatenate([_mm(xb, t, mode="tn", name="g_in_" + nm, outs=(_MXU,), tn=QKV_W)
                                 for nm, t in (("qkv", dh), ("g", dlog))], 1)
    dx = _mm(dh, w["in"], mode="nt", name="d_x_qkv", tk=QKV_W, b_cols=dh.shape[1], b_off=0, extras=(dr1,),
             epilogue=lambda acc, e: (acc + alpha * e,))
    dx = _mm(dlog, w["in"], mode="nt", name="d_x_g", tk=QKV_W, b_cols=dlog.shape[1], b_off=3, extras=(dx,),
             epilogue=lambda acc, e: (acc + e,))
    return dx, g, rode


BIG = ("w_in", "w_branch_a", "w_branch_b", "w_branch_c", "w_out", "w_up", "w_down")
ROW_SHARDED = ("w_out", "w_down")
AS_GATHERED = ("w_up", "w_down")
SMALL = ("b_gate", "q_norm_b", "k_norm_b", "rpb_c", "ln1_g", "ln1_b", "ln2_g", "ln2_b")


def _layer_weights(gathered):
    names = dict(w_in="in", w_branch_a="br_a", w_branch_b="br_b", w_branch_c="br_c", w_out="out", w_up="up", w_down="down")
    whole = {n: _full_from_shards(gathered[n], n) for n in names if n not in AS_GATHERED}
    return [{short: (whole[n], l) if n in whole else (gathered[n], l, "rows" if n in ROW_SHARDED else "cols")
             for n, short in names.items()} for l in range(gathered["w_in"].shape[1])]


def _local_step(x, target, gathered, small, rest=None):
    s, d = x.shape
    depth = small["b_gate"].shape[0]
    alpha = (2 * depth) ** 0.25
    tabs = _tables(s)
    ws = _layer_weights(gathered)
    sms = [dict(b_gate=small["b_gate"][l][None], q_norm=jnp.tile(small["q_norm_b"][l], BQ_W // HEAD)[None],
                k_norm=jnp.tile(small["k_norm_b"][l], BKV_W // HEAD)[None], rpb=small["rpb_c"][l],
                ln1_g=small["ln1_g"][l][None], ln1_b=small["ln1_b"][l][None],
                ln2_g=small["ln2_g"][l][None], ln2_b=small["ln2_b"][l][None]) for l in range(depth)]
    saved = []
    h, hb = x, x.astype(_MXU)
    for l in range(depth):
        h, hb, sv, rode = _layer_fwd(h, hb, ws[l], sms[l], tabs, alpha, rest if l == 0 else None)
        if rode:
            ws += _layer_weights(dict(zip(BIG, rode)))
        saved.append(sv)
    sq, dy = _loss_head(h, target, TM)

    def stack(gs):
        out = {k: jnp.stack([gl[k] for gl in gs], 1 if k in AS_GATHERED else 0) for k in gs[0]}
        for n in BIG:
            if n not in AS_GATHERED:
                out[n] = _shards_from_full(out[n], n).astype(_MXU)
        return out

    grads = [None] * depth
    reduced_later = None
    for l in reversed(range(depth)):
        ride = ts = None
        if rest is not None and l == 0 and depth > 1:
            ts = _reduce_pairs([stack(grads[1:])[n] for n in BIG])
            ride = (ts, _chip_out_shapes(ts), CHIP_SEMS, _chip_phases())
        dy, grads[l], rode = _layer_bwd(dy, ws[l], sms[l], saved[l], tabs, alpha, ride)
        if ride:
            reduced_later = _reduce_finish(ts, rode)
    if reduced_later is None:
        return sq, dy, stack(grads)
    first = stack(grads[:1])
    reduced = _reduce_scatter([first[n] for n in BIG])
    out = {n: jnp.concatenate([a, b], 0) for n, a, b in zip(BIG, reduced, reduced_later)}
    out.update({k: jnp.concatenate([first[k], stack(grads[1:])[k]], 0) for k in first if k not in BIG})
    return sq, dy, out


def _place():
    return lax.axis_index("x"), lax.axis_index("y"), lax.axis_index("c")


def _flip(a, b):
    return a + b - 2 * a * b


def _other_chips(x, y):
    return [(1 - x, y), (x, 1 - y), (1 - x, 1 - y)]


def _comm_call(body, name, tensors, out_shapes, n_sems):
    n = len(tensors)

    def wrapped(*refs):
        body(refs[:n], refs[n:2 * n], refs[2 * n], refs[2 * n + 1])

    return pl.pallas_call(
        wrapped, name=name, in_specs=[ANY] * n, out_specs=[ANY] * n,
        out_shape=[jax.ShapeDtypeStruct(s, t.dtype) for s, t in zip(out_shapes, tensors)],
        scratch_shapes=[pltpu.SemaphoreType.DMA((n_sems, n)), pltpu.SemaphoreType.DMA((n_sems, n))],
    )(*tensors)


GATHER_SEMS = 7


def _gather_phases(lo, n_layers):
    def ctx(srcs, outs, send_sems, recv_sems):
        x, y, c = _place()
        n1, n2, dg = (_flip(x, 1 - c), _flip(y, c)), (_flip(x, c), _flip(y, 1 - c)), (1 - x, 1 - y)

        def half(t, chip, hc):
            rh = outs[t].shape[2] // 2
            return outs[t].at[2 * chip[0] + chip[1], :, pl.ds(hc * rh, rh)]

        def copy(k, t, src_ref, dst_ref, to):
            return pltpu.make_async_remote_copy(src_ref=src_ref, dst_ref=dst_ref, send_sem=send_sems.at[k, t],
                                                recv_sem=recv_sems.at[k, t], device_id=to, device_id_type=MESH)

        def sends(t, ks):
            rh = srcs[t].shape[1] // 2
            own, mine = srcs[t].at[pl.ds(lo, n_layers), pl.ds(c * rh, rh)], srcs[t].at[pl.ds(lo, n_layers)]
            table = {0: (own, half(t, (x, y), c), (*n1, c)), 1: (own, half(t, (x, y), c), (*n2, c)),
                     2: (half(t, n1, c), half(t, n1, c), (*n2, c)), 6: (mine, outs[t].at[2 * x + y], (x, y, 1 - c)),
                     "n1": (half(t, n1, c), half(t, n1, c), (x, y, 1 - c)), "n2": (half(t, n2, c), half(t, n2, c), (x, y, 1 - c)),
                     "dg": (half(t, dg, c), half(t, dg, c), (x, y, 1 - c))}
            sem = {0: 0, 1: 1, 2: 2, 6: 6, "n1": 3 + c, "n2": 4 - c, "dg": 5}
            return [copy(sem[k], t, *table[k]) for k in ks]

        def arrived(k, t, chip, hc):
            copy(k, t, half(t, chip, hc), half(t, chip, hc), (x, y, 1 - c)).wait_recv()

        return x, y, c, n1, n2, dg, sends, arrived, range(len(srcs))

    def phase0(*refs):
        *_, sends, _, ts = ctx(*refs)
        for t in ts:
            for cp in sends(t, (0, 1, 6)):
                cp.start()

    def phase1(*refs):
        x, y, c, n1, n2, dg, sends, arrived, ts = ctx(*refs)
        for t in ts:
            arrived(0, t, n1, c)
            for cp in sends(t, (2, "n1")):
                cp.start()
        for t in ts:
            arrived(1, t, n2, c)
            sends(t, ("n2",))[0].start()

    def phase2(*refs):
        x, y, c, n1, n2, dg, sends, arrived, ts = ctx(*refs)
        srcs, outs = refs[0], refs[1]
        for t in ts:
            arrived(2, t, dg, c)
            sends(t, ("dg",))[0].start()
        for t in ts:
            for j, chip in enumerate(_other_chips(x, y)):
                arrived(3 + j, t, chip, 1 - c)
            sends(t, (6,))[0].wait_recv()
            for cp in sends(t, (0, 1, 2, 6, "n1", "n2", "dg")):
                cp.wait_send()

    return [phase0, phase1, phase2]


def _gather_out_shapes(shards, n_layers):
    return [(4, n_layers) + s.shape[1:] for s in shards]


def _gather_shards(shards, lo, n_layers):
    phases = _gather_phases(lo, n_layers)

    def body(*refs):
        for f in phases:
            f(*refs)

    return _comm_call(body, "gather_weights", shards, _gather_out_shapes(shards, n_layers), GATHER_SEMS)


def _pair_exchange(parts):
    def body(srcs, outs, send_sems, recv_sems):
        x, y, c = _place()
        cps = [pltpu.make_async_remote_copy(src_ref=src.at[:, :, pl.ds((1 - c) * (src.shape[2] // 2), src.shape[2] // 2)],
                                            dst_ref=out, send_sem=send_sems.at[0, t], recv_sem=recv_sems.at[0, t],
                                            device_id=(x, y, 1 - c), device_id_type=MESH)
               for t, (src, out) in enumerate(zip(srcs, outs))]
        for cp in cps:
            cp.start()
        for cp in cps:
            cp.wait()

    return _comm_call(body, "grad_pair_exchange", parts, [p.shape[:2] + (p.shape[2] // 2, p.shape[3]) for p in parts], 1)


CHIP_SEMS = 3


def _chip_phases():
    def copies(srcs, outs, send_sems, recv_sems):
        x, y, c = _place()
        return [pltpu.make_async_remote_copy(src_ref=src.at[2 * chip[0] + chip[1]], dst_ref=out.at[k], send_sem=send_sems.at[k, t],
                                             recv_sem=recv_sems.at[k, t], device_id=(*chip, c), device_id_type=MESH)
                for t, (src, out) in enumerate(zip(srcs, outs)) for k, chip in enumerate(_other_chips(x, y))]

    def start(*refs):
        for cp in copies(*refs):
            cp.start()

    def wait(*refs):
        for cp in copies(*refs):
            cp.wait()

    return [start, lambda *refs: None, wait]


def _chip_out_shapes(ts):
    return [(3,) + t.shape[1:] for t in ts]


def _chip_exchange(ts):
    phases = _chip_phases()

    def body(*refs):
        for f in phases:
            f(*refs)

    return _comm_call(body, "grad_chip_exchange", ts, _chip_out_shapes(ts), CHIP_SEMS)


def _pair_share(halves):
    def body(srcs, outs, send_sems, recv_sems):
        x, y, c = _place()
        cps = [pltpu.make_async_remote_copy(src_ref=src, dst_ref=out, send_sem=send_sems.at[0, t], recv_sem=recv_sems.at[0, t],
                                            device_id=(x, y, 1 - c), device_id_type=MESH)
               for t, (src, out) in enumerate(zip(srcs, outs))]
        for cp in cps:
            cp.start()
        for cp in cps:
            cp.wait()

    theirs = _comm_call(body, "grad_pair_share", halves, [h.shape for h in halves], 1)
    c = jnp.reshape(lax.axis_index("c"), (1,)).astype(jnp.int32)
    return [_join_halves(mine, other, c, "grad_pair_join_%d" % t) for t, (mine, other) in enumerate(zip(halves, theirs))]


def _rows_view(t, lead):
    return t.reshape(t.shape[:lead] + (-1, t.shape[-1]))


def _join_halves(mine, theirs, c, name):
    n, rh, cols = mine.shape
    tr = _tile(rh, 1024, 8)

    def join(c_ref, mine_ref, theirs_ref, o_ref):
        o_ref[...] = jnp.where(pl.program_id(1) == c_ref[0], mine_ref[...], theirs_ref[...])

    spec = pl.BlockSpec((None, tr, cols), lambda l, h, i, c_ref: (l, i, 0))
    return pl.pallas_call(
        join, name=name,
        grid_spec=pltpu.PrefetchScalarGridSpec(
            num_scalar_prefetch=1, grid=(n, 2, rh // tr), in_specs=[spec, spec],
            out_specs=pl.BlockSpec((None, tr, cols), lambda l, h, i, c_ref: (l, h * (rh // tr) + i, 0))),
        out_shape=jax.ShapeDtypeStruct((n, 2 * rh, cols), mine.dtype),
        compiler_params=_params("parallel", "parallel", "parallel"),
    )(c, mine, theirs)


def _gather_all(v):
    r = v.shape[0]

    def body(src, out, send_sems, recv_sems, local_sem):
        x, y, c = _place()
        me = 4 * x + 2 * y + c
        mine = pltpu.make_async_copy(src, out.at[me], local_sem)
        mine.start()
        cps = []
        for k in range(1, 8):
            fx, fy, fc = (k >> 2) & 1, (k >> 1) & 1, k & 1
            peer = (x + fx - 2 * x * fx, y + fy - 2 * y * fy, c + fc - 2 * c * fc)
            cps.append(pltpu.make_async_remote_copy(src_ref=src, dst_ref=out.at[me], send_sem=send_sems.at[k - 1],
                                                    recv_sem=recv_sems.at[k - 1], device_id=peer, device_id_type=MESH))
        for cp in cps:
            cp.start()
        for k in range(1, 8):
            fx, fy, fc = (k >> 2) & 1, (k >> 1) & 1, k & 1
            frm = 4 * (x + fx - 2 * x * fx) + 2 * (y + fy - 2 * y * fy) + (c + fc - 2 * c * fc)
            pltpu.make_async_remote_copy(src_ref=src, dst_ref=out.at[frm], send_sem=send_sems.at[k - 1],
                                         recv_sem=recv_sems.at[k - 1], device_id=(x, y, c), device_id_type=MESH).wait_recv()
        for cp in cps:
            cp.wait_send()
        mine.wait()

    return pl.pallas_call(
        body, name="gather_small_grads", in_specs=[ANY], out_specs=ANY,
        out_shape=jax.ShapeDtypeStruct((8, r, 128), v.dtype),
        scratch_shapes=[pltpu.SemaphoreType.DMA((7,)), pltpu.SemaphoreType.DMA((7,)), pltpu.SemaphoreType.DMA],
    )(v)


def _sum_slots(parts, name):
    n, r, _ = parts.shape
    tr = _tile(r, 1024, 8)

    def body(p_ref, o_ref):
        acc = p_ref[0]
        for j in range(1, n):
            acc = acc + p_ref[j]
        o_ref[...] = acc

    return pl.pallas_call(
        body, name=name, grid=(r // tr,), in_specs=[pl.BlockSpec((n, tr, 128), lambda i: (0, i, 0))],
        out_specs=pl.BlockSpec((tr, 128), lambda i: (i, 0)), out_shape=jax.ShapeDtypeStruct((r, 128), parts.dtype),
        compiler_params=_params("parallel"),
    )(parts)


def _add_sibling_half(part, recv, c, name):
    _, n, rh, cols = recv.shape
    tr = _tile(rh, 1024, 16)
    nblk = rh // tr

    def body(c_ref, p_ref, r_ref, o_ref):
        o_ref[...] = (p_ref[...].astype(F32) + r_ref[...].astype(F32)).astype(o_ref.dtype)

    blk = (None, None, tr, cols)
    return pl.pallas_call(
        body, name=name,
        grid_spec=pltpu.PrefetchScalarGridSpec(
            num_scalar_prefetch=1, grid=(4, n, nblk),
            in_specs=[pl.BlockSpec(blk, lambda j, l, i, c_ref: (j, l, c_ref[0] * nblk + i, 0)),
                      pl.BlockSpec(blk, lambda j, l, i, c_ref: (j, l, i, 0))],
            out_specs=pl.BlockSpec(blk, lambda j, l, i, c_ref: (j, l, i, 0))),
        out_shape=jax.ShapeDtypeStruct(recv.shape, recv.dtype),
        compiler_params=_params("parallel", "parallel", "parallel"),
    )(c, part, recv)


def _add_chips(t, recv, me, name):
    shape = t.shape[1:]
    t, recv = _rows_view(t, 1), _rows_view(recv, 1)
    _, rh, cols = t.shape
    tr = _tile(rh, 1024, 16)

    def body(me_ref, t_ref, r_ref, o_ref):
        f = lambda v: v.astype(F32)
        o_ref[...] = ((f(t_ref[...]) + f(r_ref[0])) + f(r_ref[1])) + f(r_ref[2])

    return pl.pallas_call(
        body, name=name,
        grid_spec=pltpu.PrefetchScalarGridSpec(
            num_scalar_prefetch=1, grid=(rh // tr,),
            in_specs=[pl.BlockSpec((None, tr, cols), lambda i, me_ref: (me_ref[0], i, 0)),
                      pl.BlockSpec((3, tr, cols), lambda i, me_ref: (0, i, 0))],
            out_specs=pl.BlockSpec((tr, cols), lambda i, me_ref: (i, 0))),
        out_shape=jax.ShapeDtypeStruct((rh, cols), F32),
        compiler_params=_params("parallel"),
    )(me, t, recv).reshape(shape)


def _reduce_pairs(parts):
    c1 = jnp.reshape(lax.axis_index("c"), (1,)).astype(jnp.int32)
    return [_add_sibling_half(p, r, c1, "grad_pair_sum_%d" % i) for i, (p, r) in enumerate(zip(parts, _pair_exchange(parts)))]


def _reduce_finish(ts, recv):
    me = jnp.reshape(2 * lax.axis_index("x") + lax.axis_index("y"), (1,)).astype(jnp.int32)
    return _pair_share([_add_chips(t, r, me, "grad_chip_sum_%d" % i) for i, (t, r) in enumerate(zip(ts, recv))])


def _reduce_scatter(parts):
    ts = _reduce_pairs(parts)
    return _reduce_finish(ts, _chip_exchange(ts))


def _to_rows(parts, mult):
    flat = jnp.concatenate([p.reshape(-1) for p in parts])
    flat = jnp.pad(flat, (0, (-flat.size) % (128 * mult)))
    return flat.reshape(-1, 128)


def _from_rows(rows, shapes):
    flat, out, at = rows.reshape(-1), [], 0
    for shp in shapes:
        n = int(np.prod(shp))
        out.append(flat[at:at + n].reshape(shp))
        at += n
    return out


def _full_from_shards(g, name):
    _, depth, rows, cols = g.shape
    if name in ROW_SHARDED:
        return jnp.moveaxis(g, 0, 1).reshape(depth, 4 * rows, cols)
    return jnp.moveaxis(g, 0, 2).reshape(depth, rows, 4 * cols)


def _shards_from_full(full, name):
    depth, rows, cols = full.shape
    if name in ROW_SHARDED:
        return jnp.moveaxis(full.reshape(depth, 4, rows // 4, cols), 1, 0)
    return jnp.moveaxis(full.reshape(depth, rows, 4, cols // 4), 2, 0)


def kernel(x, w_in, b_gate, q_norm_b, k_norm_b, rpb_c, w_branch_a, w_branch_b, w_branch_c, w_out, ln1_g, ln1_b, w_up, w_down, ln2_g, ln2_b, loss_target, m_w_in, m_b_gate, m_q_norm_b, m_k_norm_b, m_rpb_c, m_w_branch_a, m_w_branch_b, m_w_branch_c, m_w_out, m_ln1_g, m_ln1_b, m_w_up, m_w_down, m_ln2_g, m_ln2_b, v_w_in, v_b_gate, v_q_norm_b, v_k_norm_b, v_rpb_c, v_w_branch_a, v_w_branch_b, v_w_branch_c, v_w_out, v_ln1_g, v_ln1_b, v_w_up, v_w_down, v_ln2_g, v_ln2_b):
    args = dict(locals())
    big_shard = {n: args[n] for n in BIG}
    small = {n: args[n] for n in SMALL}

    shards = [big_shard[n].astype(_MXU) for n in BIG]
    later = shards[0].shape[0] - 1
    first = dict(zip(BIG, _gather_shards(shards, 0, 1)))
    rest = (shards, _gather_out_shapes(shards, later), GATHER_SEMS, _gather_phases(1, later)) if later else None

    sq, grad_x, grads = _local_step(x[0], loss_target[0], first, small, rest)
    loss = lax.psum(0.5 * jnp.sum(sq) / x.shape[-1], ("x", "y", "c"))

    g_big = [grads[n] for n in BIG] if rest else _reduce_scatter([grads[n] for n in BIG])
    small_shapes = [small[n].shape for n in SMALL]
    g_small = _from_rows(_sum_slots(_gather_all(_to_rows([grads[n] for n in SMALL], 8)), "small_grad_sum"), small_shapes)
    grad = dict(zip(BIG, g_big))
    grad.update(zip(SMALL, g_small))

    delta, new_m, new_v = {}, {}, {}
    for n in BIG:
        shp = big_shard[n].shape
        two_d = lambda t: t.reshape(-1, shp[-1])
        res = _adamw(two_d(big_shard[n]), two_d(grad[n]), two_d(args["m_" + n]), two_d(args["v_" + n]), "adamw_" + n)
        delta[n], new_m[n], new_v[n] = [t.reshape(shp) for t in res]
    packed = [_to_rows([args[pre + n] for n in SMALL], 8) for pre in ("", "m_", "v_")]
    res = _adamw(packed[0], _to_rows([grad[n] for n in SMALL], 8), packed[1], packed[2], "adamw_small")
    for dst, rows in zip((delta, new_m, new_v), res):
        dst.update(zip(SMALL, _from_rows(rows, small_shapes)))

    order = ("w_in", "b_gate", "q_norm_b", "k_norm_b", "rpb_c", "w_branch_a", "w_branch_b", "w_branch_c", "w_out",
             "ln1_g", "ln1_b", "w_up", "w_down", "ln2_g", "ln2_b")
    return (loss, grad_x[None], *[grad[n] for n in order], *[delta[n] for n in order],
            *[new_m[n] for n in order], *[new_v[n] for n in order])
```

```python
import functools

import numpy as np
import jax
import jax.numpy as jnp
from jax import lax
from jax.experimental import pallas as pl
from jax.experimental.pallas import tpu as pltpu

F32 = jnp.float32
_MXU = jnp.bfloat16

HEAD = 64
A_W, BQ_W, BKV_W, C_W = 256, 512, 128, 256
QKV_W = 768
A_DILATIONS = (1, 4, 16)
A_RADIUS = 64
A_ROPE_HALF = 8
AX_ROPE_HALF = 16
ROPE_THETA = 500000.0
AX_THETA = 10000.0
GRID_W = 64
C_ROWS = 8
C_COLS = 16
BAND = 128
BT_TILES = 18
LN_EPS = 1e-5
RMS_EPS = 1e-6
NEG = -1e30
SCALE = HEAD ** -0.5
LOG2E = 1.4426950408889634
LN2 = 0.6931471805599453
ADAM_LR, ADAM_B1, ADAM_B2, ADAM_EPS, ADAM_WD, ADAM_STEP = 0.001, 0.9, 0.999, 1e-08, 0.01, 10
V7X_VMEM_LIMIT = 48 * 1024 * 1024
MESH = pl.DeviceIdType.MESH
ANY = pl.BlockSpec(memory_space=pl.ANY)


def _params(*sem):
    return pltpu.CompilerParams(dimension_semantics=sem or None, vmem_limit_bytes=V7X_VMEM_LIMIT)


def _tile(n, pref, align=128):
    if n <= pref:
        return n
    t = (pref // align) * align
    while t >= align:
        if n % t == 0:
            return t
        t -= align
    return n


def _mm(a, b, *, name, mode="nn", outs=((F32),), epilogue=None, extras=(), tm=1024, tn=1024, tk=2048, exact=False,
        b_cols=None, b_off=0, out_chips=False):
    b, b_lead, b_axis = (tuple(b) + (None, None))[:3] if isinstance(b, tuple) else (b, None, None)
    m, k = a.shape if mode != "tn" else a.shape[::-1]
    b_rows = b.shape[-2] * (4 if b_axis == "rows" else 1)
    b_last = b_cols or b.shape[-1] * (4 if b_axis == "cols" else 1)
    k2, n = (b_rows, b_last) if mode != "nt" else (b_last, b_rows)
    assert k == k2, (a.shape, b.shape, mode)
    cap_rows = b.shape[-2] if b_axis == "rows" else None
    cap_cols = b.shape[-1] if b_axis == "cols" else (n // 4 if out_chips else None)
    cap_n, cap_k = (cap_cols, cap_rows) if mode != "nt" else (cap_rows, cap_cols)
    tm, tn, tk = _tile(m, tm), _tile(cap_n or n, min(tn, cap_n or tn)), _tile(cap_k or k, min(tk, cap_k or tk))
    nk = k // tk
    n_ex, n_out = len(extras), len(outs)
    mx = F32 if exact else _MXU
    prec = lax.Precision.HIGHEST if exact else None
    dims = {"nn": (((1,), (0,)), ((), ())), "nt": (((1,), (1,)), ((), ())), "tn": (((0,), (0,)), ((), ()))}[mode]

    def body(*refs):
        a_ref, b_ref = refs[0], refs[1]
        ex = refs[2:2 + n_ex]
        out_refs = refs[2 + n_ex:2 + n_ex + n_out]
        kk = pl.program_id(2)
        av, bv = a_ref[...].astype(mx), b_ref[...].astype(mx)
        part = lax.dot_general(av, bv, dims, preferred_element_type=F32, precision=prec)

        def finish(res):
            vals = epilogue(res, *[e[...] for e in ex]) if epilogue is not None else (res,)
            for o, v in zip(out_refs, vals):
                o[...] = v.astype(o.dtype)

        if nk == 1:
            finish(part)
        else:
            acc = refs[-1]

            @pl.when(kk == 0)
            def _():
                acc[...] = part

            @pl.when((kk > 0) & (kk < nk - 1))
            def _():
                acc[...] += part

            @pl.when(kk == nk - 1)
            def _():
                finish(acc[...] + part)

    a_spec = pl.BlockSpec((tm, tk), lambda i, j, kk: (i, kk)) if mode != "tn" else pl.BlockSpec((tk, tm), lambda i, j, kk: (kk, i))
    b_tile = (tn, tk) if mode == "nt" else (tk, tn)

    def b_index(i, j, kk):
        rc = [j, kk + b_off] if mode == "nt" else [kk, j + b_off]
        if b_axis is None:
            return (() if b_lead is None else (b_lead,)) + tuple(rc)
        ax = 0 if b_axis == "rows" else 1
        per = b.shape[-2 + ax] // b_tile[ax]
        chip, rc[ax] = rc[ax] // per, rc[ax] % per
        return (chip, b_lead) + tuple(rc)

    b_spec = pl.BlockSpec((None,) * (b.ndim - 2) + b_tile, b_index)
    o_spec = pl.BlockSpec((tm, tn), lambda i, j, kk: (i, j))
    if out_chips:
        per_out = n // 4 // tn
        out_specs = [pl.BlockSpec((None, tm, tn), lambda i, j, kk: (j // per_out, i, j % per_out))] * n_out
        out_shape = [jax.ShapeDtypeStruct((4, m, n // 4), d) for d in outs]
    else:
        out_specs, out_shape = [o_spec] * n_out, [jax.ShapeDtypeStruct((m, n), d) for d in outs]
    res = pl.pallas_call(
        body, name=name, grid=(m // tm, n // tn, nk),
        in_specs=[a_spec, b_spec] + [o_spec if e.shape[0] > 1 else pl.BlockSpec((1, tn), lambda i, j, kk: (0, j)) for e in extras],
        out_specs=out_specs, out_shape=out_shape,
        scratch_shapes=[pltpu.VMEM((tm, tn), F32)] if nk > 1 else [],
        compiler_params=_params("parallel", "parallel", "arbitrary"),
    )(a, b, *extras)
    return res[0] if n_out == 1 else res


def _rows(tm, width, cb=0):
    return pl.BlockSpec((tm, width), lambda t: (t, cb))


def _whole(arr):
    nd = arr.ndim
    return pl.BlockSpec(arr.shape, lambda t: (0,) * nd)


def _rowwise(fn, name, rows, tm, ins, outs, into=None):
    into = into or {}
    n_in, n_out, n_into = len(ins), len(outs), len(into)
    dil_in = [spec[1:] if isinstance(spec, tuple) else None for _, spec in ins]
    in_specs = [_rows(tm // spec[1], spec[1] * spec[2]) if isinstance(spec, tuple) else spec for _, spec in ins]
    scratch = [pltpu.VMEM((di[1] // 128, tm, 128), F32) for di in dil_in if di] + \
              [pltpu.VMEM((n // 128, tm, 128), F32) for n, _, kind in outs if isinstance(kind, int)]

    def body(*refs):
        scr = list(refs[n_in + n_into + n_out:])
        blocks = []
        for r, di in zip(refs[:n_in], dil_in):
            if di is None:
                blocks.append(r[...])
            else:
                d, n = di
                s_ref = scr.pop(0)
                for j in range(d):
                    for b in range(n // 128):
                        lanes = slice(j * n + b * 128, j * n + (b + 1) * 128)
                        s_ref.at[b][pl.ds(j, tm // d, stride=d), :] = r[:, lanes].astype(F32)
                blocks.append(jnp.concatenate([s_ref[b] for b in range(n // 128)], 1))
        vals = fn(*blocks)
        first = pl.program_id(0) == 0
        for (ncols, _, kind), o, v in zip(outs, refs[n_in + n_into:n_in + n_into + n_out], vals):
            if kind == "row" or isinstance(kind, tuple):
                o[...] = v.astype(o.dtype)
            elif isinstance(kind, int):
                s_ref = scr.pop(0)
                for b in range(ncols // 128):
                    s_ref[b] = v[:, b * 128:(b + 1) * 128].astype(F32)
                for j in range(kind):
                    for b in range(ncols // 128):
                        lanes = slice(j * ncols + b * 128, j * ncols + (b + 1) * 128)
                        o[:, lanes] = s_ref.at[b][pl.ds(j, tm // kind, stride=kind), :].astype(o.dtype)
            else:
                part = v.reshape(tm // 8, 8, ncols).sum(0)

                @pl.when(first)
                def _(o=o, part=part):
                    o[...] = part

                @pl.when(jnp.logical_not(first))
                def _(o=o, part=part):
                    o[...] += part

    def out_spec(n, kind):
        if kind == "row":
            return _rows(tm, n), (rows, n)
        if isinstance(kind, int):
            return _rows(tm // kind, kind * n), (rows // kind, kind * n)
        if isinstance(kind, tuple):
            return _rows(tm, n, kind[2]), (rows, kind[1])
        return pl.BlockSpec((8, n), lambda t: (0, 0)), (8, n)

    specs = [out_spec(n, kind) for n, _, kind in outs]
    res = pl.pallas_call(
        body, name=name, grid=(rows // tm,),
        in_specs=in_specs + [ANY] * n_into, out_specs=[s for s, _ in specs],
        out_shape=[jax.ShapeDtypeStruct(shp, d) for (_, shp), (_, d, _) in zip(specs, outs)],
        input_output_aliases={n_in + e: o for e, o in enumerate(into)},
        scratch_shapes=scratch, compiler_params=_params("arbitrary"),
    )(*[a for a, _ in ins], *into.values())
    return res


def _lane_lo(width=128):
    return (lax.broadcasted_iota(jnp.int32, (1, width), 1) & (HEAD * 2 - 1)) < HEAD


def _group_sum(x):
    w = x.shape[-1]
    sh = HEAD.bit_length() - 1
    same = (lax.broadcasted_iota(jnp.int32, (w, w), 0) >> sh) == (lax.broadcasted_iota(jnp.int32, (w, w), 1) >> sh)
    ones = jnp.where(same, 1.0, 0.0).astype(jnp.bfloat16)
    hi = x.astype(jnp.bfloat16)
    lo = (x - hi.astype(F32)).astype(jnp.bfloat16)
    return jnp.dot(hi, ones, preferred_element_type=F32) + jnp.dot(lo, ones, preferred_element_type=F32)


def _rot(x, c, sm, sp, shift):
    w = x.shape[-1]
    return x * c + pltpu.roll(x, w - shift, 1) * sm + pltpu.roll(x, shift, 1) * sp


def _rot_t(dy, c, sm, sp, shift):
    w = dy.shape[-1]
    return dy * c + pltpu.roll(dy * sm, shift, 1) + pltpu.roll(dy * sp, w - shift, 1)


def _rope_tables(pos_parts, half, thetas):
    cs, sms, sps = [], [], []
    for pos, theta in zip(pos_parts, thetas):
        inv = theta ** (-jnp.arange(half, dtype=F32) / half)
        ang = pos.astype(F32)[:, None] * inv[None, :]
        co, si, ze = jnp.cos(ang), jnp.sin(ang), jnp.zeros_like(ang)
        cs += [co, co]
        sms += [-si, ze]
        sps += [ze, si]
    return [jnp.concatenate(t, axis=1) for t in (cs, sms, sps)]


def _tables(s):
    pos = jnp.arange(s)
    ca, sma, spa = _rope_tables([pos], A_ROPE_HALF, [ROPE_THETA])
    pad = HEAD - 2 * A_ROPE_HALF
    ca = jnp.concatenate([ca, jnp.ones((s, pad), F32)], 1)
    sma, spa = [jnp.concatenate([t, jnp.zeros((s, pad), F32)], 1) for t in (sma, spa)]
    tab_a = [jnp.tile(t, (1, A_W // HEAD)) for t in (ca, sma, spa)]
    ax = _rope_tables([pos // GRID_W, pos % GRID_W], AX_ROPE_HALF, [AX_THETA, AX_THETA])
    tab_q = [jnp.tile(t, (1, BQ_W // HEAD)) for t in ax]
    tab_k = [jnp.tile(t, (1, BKV_W // HEAD)) for t in ax]
    return tab_a, tab_q, tab_k


def _prep_a(ha, tab, tm):
    s = ha.shape[0]

    def fn(h, c, sm, sp):
        q, k, v = h[:, :A_W], h[:, A_W:2 * A_W], h[:, 2 * A_W:]
        q, k = _rot(q, c, sm, sp, A_ROPE_HALF) * SCALE, _rot(k, c, sm, sp, A_ROPE_HALF)
        return [t for t in (q, k, v) for _ in A_DILATIONS]

    res = _rowwise(fn, "prep_a", s, tm, [(ha, _rows(tm, QKV_W))] + [(t, _rows(tm, A_W)) for t in tab],
                   [(A_W, _MXU, _dil_kind(d)) for _ in range(3) for d in A_DILATIONS])
    n = len(A_DILATIONS)
    return [dict(zip(A_DILATIONS, res[i * n:(i + 1) * n])) for i in range(3)]


def _dil_kind(d):
    return "row" if d == 1 else d


def _dil_spec(d, tm, ncols):
    return _rows(tm, ncols) if d == 1 else ("dil", d, ncols)


def _to_dilations(x, name, tm):
    s, n = x.shape
    res = _rowwise(lambda v: [v for d in A_DILATIONS if d > 1], name, s, tm, [(x, _rows(tm, n))],
                   [(n, x.dtype, d) for d in A_DILATIONS if d > 1])
    return {1: x, **dict(zip([d for d in A_DILATIONS if d > 1], res))}


def _rms(x, g):
    ms = _group_sum(x * x) * (1.0 / HEAD)
    return x * lax.rsqrt(ms + RMS_EPS) * g


def _prep_b(hb, gq, gk, tab_q, tab_k, tm):
    s = hb.shape[0]

    def fn(h, gq, gk, cq, smq, spq, ck, smk, spk):
        xq, xk, v = h[:, :BQ_W], h[:, BQ_W:BQ_W + BKV_W], h[:, BQ_W + BKV_W:]
        q = _rot(_rms(xq, gq), cq, smq, spq, AX_ROPE_HALF) * (SCALE * LOG2E)
        k = _rot(_rms(xk, gk), ck, smk, spk, AX_ROPE_HALF)
        lo = _lane_lo()
        kr, vr = pltpu.roll(k, HEAD, 1), pltpu.roll(v, HEAD, 1)
        kd = jnp.concatenate([jnp.where(lo, k, kr), jnp.where(lo, kr, k)], 1)
        vd = jnp.concatenate([jnp.where(lo, v, vr), jnp.where(lo, vr, v)], 1)
        v1 = jnp.concatenate([jnp.where(lo, v, 1.0), jnp.where(lo, vr, 1.0)], 1)
        return q, kd, vd, v1

    ins = [(hb, _rows(tm, QKV_W)), (gq, _whole(gq)), (gk, _whole(gk))]
    ins += [(t, _rows(tm, BQ_W)) for t in tab_q] + [(t, _rows(tm, BKV_W)) for t in tab_k]
    return _rowwise(fn, "prep_b", s, tm, ins, [(BQ_W, _MXU, "row")] + [(2 * BKV_W, _MXU, "row")] * 3)


def _prep_c(hc, tm):
    def fn(h):
        return h[:, :C_W] * SCALE, h[:, C_W:2 * C_W], h[:, 2 * C_W:]

    return _rowwise(fn, "prep_c", hc.shape[0], tm, [(hc, _rows(tm, QKV_W))], [(C_W, _MXU, "row")] * 3)


def _combine_a(os_, ms, ls, s, tm):
    def fn(o1, o2, o3, m1, m2, m3, l1, l2, l3):
        lo = _lane_lo()
        outs, lses = [], []
        for p in range(A_W // 128):
            st = slice(p * 256, (p + 1) * 256)
            mm = [m[:, st] for m in (m1, m2, m3)]
            ll = [l[:, st] for l in (l1, l2, l3)]
            mmax = jnp.maximum(jnp.maximum(mm[0], mm[1]), mm[2])
            ws = [jnp.exp(m - mmax) for m in mm]
            den = ws[0] * ll[0] + ws[1] * ll[1] + ws[2] * ll[2]
            lses.append(mmax + jnp.log(den))
            num = sum(jnp.where(lo, w[:, :128], w[:, 128:]) * o[:, p * 128:(p + 1) * 128] for w, o in zip(ws, (o1, o2, o3)))
            outs.append(num / jnp.where(lo, den[:, :128], den[:, 128:]))
        o, lse = jnp.concatenate(outs, 1), jnp.concatenate(lses, 1)
        return [o] * len(A_DILATIONS) + [lse] * len(A_DILATIONS)

    ins = [(t, _dil_spec(d, tm, w)) for ts, w in ((os_, A_W), (ms, 2 * A_W), (ls, 2 * A_W)) for t, d in zip(ts, A_DILATIONS)]
    res = _rowwise(fn, "combine_a", s, tm, ins,
                   [(w, F32, _dil_kind(d)) for w in (A_W, 2 * A_W) for d in A_DILATIONS])
    n = len(A_DILATIONS)
    return dict(zip(A_DILATIONS, res[:n])), dict(zip(A_DILATIONS, res[n:]))


def _gates(hg, bg, d):
    return [jax.nn.sigmoid(hg[:, i * d:(i + 1) * d] + bg[:, i * d:(i + 1) * d]) for i in range(3)]


def _gate_merge(hg, bg, pa, pb, pc, tm):
    s, d = pa.shape

    def fn(hg, bg, pa, pb, pc):
        g = _gates(hg, bg, d)
        return (g[0] * pa + g[1] * pb + g[2] * pc,)

    ins = [(hg, _rows(tm, 3 * d)), (bg, _whole(bg))] + [(p, _rows(tm, d)) for p in (pa, pb, pc)]
    return _rowwise(fn, "gate_merge", s, tm, ins, [(d, _MXU, "row")])[0]


def _gate_bwd(dm, hg, bg, pa, pb, pc, tm):
    s, d = pa.shape

    def fn(dm, hg, bg, pa, pb, pc):
        g = _gates(hg, bg, d)
        dlog = jnp.concatenate([dm * p * gi * (1.0 - gi) for p, gi in zip((pa, pb, pc), g)], 1)
        return dm * g[0], dm * g[1], dm * g[2], dlog, dlog

    ins = [(dm, _rows(tm, d)), (hg, _rows(tm, 3 * d)), (bg, _whole(bg))] + [(p, _rows(tm, d)) for p in (pa, pb, pc)]
    return _rowwise(fn, "gate_bwd", s, tm, ins, [(d, _MXU, "row")] * 3 + [(3 * d, _MXU, "row"), (3 * d, F32, "acc")])


def _ln_stats(r):
    mu = jnp.mean(r, -1, keepdims=True)
    xc = r - mu
    var = jnp.mean(xc * xc, -1, keepdims=True)
    rstd = lax.rsqrt(var + LN_EPS)
    return xc * rstd, rstd


def _ln_epilogue(alpha):
    def fn(br, x, g, b):
        r = alpha * x + br
        xhat, _ = _ln_stats(r)
        y = xhat * g + b
        return r, y, y

    return fn


def _ln_bwd(dy, r, g, name, tm):
    s, d = r.shape

    def fn(dy, r, g):
        xhat, rstd = _ln_stats(r)
        dxh = dy * g
        dr = rstd * (dxh - jnp.mean(dxh, -1, keepdims=True) - xhat * jnp.mean(dxh * xhat, -1, keepdims=True))
        return dr, dr, dy * xhat, dy

    ins = [(dy, _rows(tm, d)), (r, _rows(tm, d)), (g, _whole(g))]
    return _rowwise(fn, name, s, tm, ins, [(d, F32, "row"), (d, _MXU, "row"), (d, F32, "acc"), (d, F32, "acc")])


def _loss_head(y, target, tm):
    s, d = y.shape

    def fn(y, t):
        diff = y - t
        return diff * diff, diff * (1.0 / d)

    sq, dy = _rowwise(fn, "loss_head", s, tm, [(y, _rows(tm, d)), (target, _rows(tm, d))], [(d, F32, "acc"), (d, F32, "row")])
    return sq, dy


def _post_a(dqs, dks, dvs, tab, s, tm):
    def fn(q1, q2, q3, k1, k2, k3, v1, v2, v3, c, sm, sp):
        dq = _rot_t((q1 + q2 + q3) * SCALE, c, sm, sp, A_ROPE_HALF)
        dk = _rot_t(k1 + k2 + k3, c, sm, sp, A_ROPE_HALF)
        return (jnp.concatenate([dq, dk, v1 + v2 + v3], 1),)

    ins = [(t, _dil_spec(d, tm, A_W)) for ts in (dqs, dks, dvs) for t, d in zip(ts, A_DILATIONS)]
    ins += [(t, _rows(tm, A_W)) for t in tab]
    return _rowwise(fn, "post_a", s, tm, ins, [(QKV_W, _MXU, ("slot", 3 * QKV_W, 0))])[0]


def _post_b(dq, dkd, dvd, hb, gq, gk, tab_q, tab_k, tm, dh):
    s = dq.shape[0]

    def back(dz, x, g, c, sm, sp):
        dy = _rot_t(dz, c, sm, sp, AX_ROPE_HALF)
        rstd = lax.rsqrt(_group_sum(x * x) * (1.0 / HEAD) + RMS_EPS)
        xh = x * rstd
        dxh = dy * g
        return rstd * (dxh - xh * (_group_sum(dxh * xh) * (1.0 / HEAD))), dy * xh

    def fn(dq, dkd, dvd, h, gq, gk, cq, smq, spq, ck, smk, spk):
        lo = _lane_lo()
        dk = jnp.where(lo, dkd[:, :128], dkd[:, 128:])
        dv = jnp.where(lo, dvd[:, :128], dvd[:, 128:])
        dxq, dgq = back(dq * SCALE, h[:, :BQ_W], gq, cq, smq, spq)
        dxk, dgk = back(dk, h[:, BQ_W:BQ_W + BKV_W], gk, ck, smk, spk)
        return jnp.concatenate([dxq, dxk, dv], 1), dgq, dgk

    ins = [(dq, _rows(tm, BQ_W)), (dkd, _rows(tm, 2 * BKV_W)), (dvd, _rows(tm, 2 * BKV_W)), (hb, _rows(tm, QKV_W)),
           (gq, _whole(gq)), (gk, _whole(gk))]
    ins += [(t, _rows(tm, BQ_W)) for t in tab_q] + [(t, _rows(tm, BKV_W)) for t in tab_k]
    return _rowwise(fn, "post_b", s, tm, ins, [(QKV_W, _MXU, ("slot", 3 * QKV_W, 1)), (BQ_W, F32, "acc"), (BKV_W, F32, "acc")],
                    into={0: dh})


def _post_c(dq, dk, dv, tm, dh):
    def fn(dq, dk, dv):
        return (jnp.concatenate([dq * SCALE, dk, dv], 1),)

    return _rowwise(fn, "post_c", dq.shape[0], tm, [(t, _rows(tm, C_W)) for t in (dq, dk, dv)],
                    [(QKV_W, _MXU, ("slot", 3 * QKV_W, 2))], into={0: dh})[0]


def _adamw(w, g, m, v, name):
    rows, cols = w.shape
    tm = _tile(rows, 256, 8)

    def fn(w, g, m, v):
        m = ADAM_B1 * m + (1.0 - ADAM_B1) * g
        v = ADAM_B2 * v + (1.0 - ADAM_B2) * (g * g)
        m_hat = m / (1.0 - ADAM_B1 ** ADAM_STEP)
        v_hat = v / (1.0 - ADAM_B2 ** ADAM_STEP)
        delta = -ADAM_LR * (m_hat / (jnp.sqrt(v_hat) + ADAM_EPS) + ADAM_WD * w)
        return delta, m, v

    return _rowwise(fn, name, rows, tm, [(t, _rows(tm, cols)) for t in (w, g, m, v)], [(cols, F32, "row")] * 3)


def _dot_t(a, b):
    return lax.dot_general(a, b, (((1,), (1,)), ((), ())), preferred_element_type=F32)


def _tdot(a, b):
    return lax.dot_general(a, b, (((0,), (0,)), ((), ())), preferred_element_type=F32)


def _head_masks():
    lo = _lane_lo()
    return lo, (lo, jnp.logical_not(lo))


def _rep(x, rows):
    return jnp.broadcast_to(x, (rows, 128))


def _row_lo():
    return lax.broadcasted_iota(jnp.int32, (128, 1), 0) < HEAD


def _flash_fwd(q, kd, v1, tq, tk, ride=None):
    s = q.shape[0]
    tq, tk = _tile(s, tq), _tile(s, tk)
    nq, nk = s // tq, s // tk
    mx = _MXU
    n_ride = len(ride[0]) if ride else 0
    steps = BQ_W // 128 * nq * nk

    def body(*refs):
        q_ref, k_ref, v_ref = refs[:3]
        o_ref, lse_ref = refs[3 + n_ride:5 + n_ride]
        m_ref, acc_ref = refs[5 + 2 * n_ride:7 + 2 * n_ride]
        kk = pl.program_id(2)
        if ride:
            comm = (refs[3:3 + n_ride], refs[5 + n_ride:5 + 2 * n_ride], refs[-2], refs[-1])
            step = (pl.program_id(0) * nq + pl.program_id(1)) * nk + kk
            for at, phase in zip((0, steps // 3), ride[3][:2]):
                pl.when(step == at)(functools.partial(phase, *comm))

        @pl.when(kk == 0)
        def _():
            m_ref[...] = jnp.full_like(m_ref, NEG)
            acc_ref[...] = jnp.zeros_like(acc_ref)

        q2, k2, v2 = q_ref[...], k_ref[...], v_ref[...]
        _, masks = _head_masks()
        hs = range(2)
        st = [_dot_t(k2, jnp.where(masks[h], q2, jnp.zeros_like(q2))) for h in hs]
        m_prev = [m_ref[h] for h in hs]
        m_new = [jnp.maximum(m_prev[h], jnp.max(st[h], 0, keepdims=True)) for h in hs]
        p = [jnp.exp2(st[h] - m_new[h]).astype(mx) for h in hs]
        pv = [_tdot(v2, p[h]) for h in hs]
        for h in hs:
            m_ref[h] = m_new[h]
            acc_ref[h] = acc_ref[h] * jnp.exp2(m_prev[h] - m_new[h]) + pv[h]

        @pl.when(kk == nk - 1)
        def _():
            a0, a1 = acc_ref[0], acc_ref[1]
            l0, l1 = a0[HEAD:HEAD + 1], a1[HEAD:HEAD + 1]
            o_ref[...] = jnp.concatenate([a0[:HEAD] / l0, a1[:HEAD] / l1], 0).T
            lse_ref[...] = jnp.concatenate([m_ref[0] + jnp.log2(l0), m_ref[1] + jnp.log2(l1), jnp.zeros((6, tq), F32)], 0)

        if ride:
            pl.when(step == steps - 1)(functools.partial(ride[3][2], *comm))

    ride_in = list(ride[0]) if ride else []
    ride_out = [jax.ShapeDtypeStruct(shp, t.dtype) for shp, t in zip(ride[1], ride[0])] if ride else []
    ride_sems = [pltpu.SemaphoreType.DMA((ride[2], n_ride))] * 2 if ride else []
    res = pl.pallas_call(
        body, name="attn_b_fwd_gather" if ride else "attn_b_fwd", grid=(BQ_W // 128, nq, nk),
        in_specs=[pl.BlockSpec((tq, 128), lambda j, i, kk: (i, j)),
                  pl.BlockSpec((tk, 128), lambda j, i, kk: (kk, j // 2)),
                  pl.BlockSpec((tk, 128), lambda j, i, kk: (kk, j // 2))] + [ANY] * n_ride,
        out_specs=[pl.BlockSpec((tq, 128), lambda j, i, kk: (i, j)), pl.BlockSpec((None, 8, tq), lambda j, i, kk: (j, 0, i))]
        + [ANY] * n_ride,
        out_shape=[jax.ShapeDtypeStruct((s, BQ_W), F32), jax.ShapeDtypeStruct((BQ_W // 128, 8, s), F32)] + ride_out,
        scratch_shapes=[pltpu.VMEM((2, 1, tq), F32), pltpu.VMEM((2, 128, tq), F32)] + ride_sems,
        compiler_params=_params("arbitrary", "arbitrary", "arbitrary") if ride else _params("parallel", "parallel", "arbitrary"),
    )(q, kd, v1, *ride_in)
    return res[0], res[1], res[2:]


def _delta_b(do, o, tq):
    s = do.shape[0]
    tq = _tile(s, tq)

    def body(do_ref, o_ref, d_ref):
        prod = do_ref[...] * o_ref[...]
        row = lax.broadcasted_iota(jnp.int32, (8, 128), 0)
        lane = lax.broadcasted_iota(jnp.int32, (8, 128), 1)
        sel = jnp.where(((row == 0) & (lane < HEAD)) | ((row == 1) & (lane >= HEAD)), 1.0, 0.0).astype(F32)
        d_ref[...] = lax.dot_general(sel, prod, (((1,), (1,)), ((), ())), preferred_element_type=F32,
                                     precision=lax.Precision.HIGHEST)

    qs = pl.BlockSpec((tq, 128), lambda j, i: (i, j))
    return pl.pallas_call(
        body, name="attn_b_delta", grid=(BQ_W // 128, s // tq), in_specs=[qs, qs],
        out_specs=pl.BlockSpec((None, 8, tq), lambda j, i: (j, 0, i)),
        out_shape=jax.ShapeDtypeStruct((BQ_W // 128, 8, s), F32),
        compiler_params=_params("parallel", "parallel"),
    )(do, o)


def _flash_bwd(q, kd, vd, do, lse, delta, tq, tk, ride=None):
    s = q.shape[0]
    tq, tk = _tile(s, tq), _tile(s, tk)
    nq, nk = s // tq, s // tk
    group = BQ_W // 128 // 2
    mx = _MXU
    n_ride = len(ride[0]) if ride else 0
    steps = BKV_W // HEAD * nk * group * nq

    def body(*refs):
        k_ref, v_ref, q_ref, do_ref, lse_ref, dl_ref = refs[:6]
        dq_hbm, dk_ref, dv_ref = refs[6 + n_ride:9 + n_ride]
        dk_acc, dv_acc, dqt, stage, sem = refs[9 + 2 * n_ride:14 + 2 * n_ride]
        e, kk, jj, i = pl.program_id(0), pl.program_id(1), pl.program_id(2), pl.program_id(3)
        if ride:
            comm = (refs[6:6 + n_ride], refs[9 + n_ride:9 + 2 * n_ride], refs[-2], refs[-1])
            step = ((e * nk + kk) * group + jj) * nq + i
            for at, phase in zip((0, steps // 3), ride[3][:2]):
                pl.when(step == at)(functools.partial(phase, *comm))

        @pl.when((jj == 0) & (i == 0))
        def _():
            dk_acc[...] = jnp.zeros_like(dk_acc)
            dv_acc[...] = jnp.zeros_like(dv_acc)

        @pl.when(kk == 0)
        def _():
            dqt[jj, i] = jnp.zeros((128, tq), F32)

        q2, k2, v2, do2 = q_ref[...], k_ref[...], v_ref[...], do_ref[...].astype(mx)
        lse8, dl8 = lse_ref[...], dl_ref[...]
        _, masks = _head_masks()
        hs = range(2)
        qh = [jnp.where(masks[h], q2, jnp.zeros_like(q2)) for h in hs]
        doh = [jnp.where(masks[h], do2, jnp.zeros_like(do2)) for h in hs]
        st = [_dot_t(k2, qh[h]) for h in hs]
        dpt = [_dot_t(v2, doh[h]) for h in hs]
        p = [jnp.exp2(st[h] - lse8[h:h + 1]) for h in hs]
        ds = [(p[h] * (dpt[h] - dl8[h:h + 1])).astype(mx) for h in hs]
        p = [p[h].astype(mx) for h in hs]
        dv_acc[...] += jnp.dot(p[0], doh[0], preferred_element_type=F32) + jnp.dot(p[1], doh[1], preferred_element_type=F32)
        dk_acc[...] += jnp.dot(ds[0], qh[0], preferred_element_type=F32) + jnp.dot(ds[1], qh[1], preferred_element_type=F32)
        dqt[jj, i] += jnp.where(_row_lo(), _tdot(k2, ds[0]), _tdot(k2, ds[1]))

        @pl.when(kk == nk - 1)
        def _():
            stage[...] = dqt[jj, i].T
            lane0 = pl.multiple_of((group * e + jj) * 128, 128)
            cp = pltpu.make_async_copy(stage, dq_hbm.at[pl.ds(pl.multiple_of(i * tq, tq), tq), pl.ds(lane0, 128)], sem)
            cp.start()
            cp.wait()

        @pl.when((jj == group - 1) & (i == nq - 1))
        def _():
            dk_ref[...] = (dk_acc[...] + pltpu.roll(dk_acc[...], HEAD, 1)) * LN2
            dv_ref[...] = dv_acc[...] + pltpu.roll(dv_acc[...], HEAD, 1)

        if ride:
            pl.when(step == steps - 1)(functools.partial(ride[3][2], *comm))

    ks = pl.BlockSpec((tk, 128), lambda e, kk, jj, i: (kk, e))
    qs = pl.BlockSpec((tq, 128), lambda e, kk, jj, i: (i, group * e + jj))
    st = pl.BlockSpec((None, 8, tq), lambda e, kk, jj, i: (group * e + jj, 0, i))
    ride_in = list(ride[0]) if ride else []
    ride_out = [jax.ShapeDtypeStruct(shp, t.dtype) for shp, t in zip(ride[1], ride[0])] if ride else []
    ride_sems = [pltpu.SemaphoreType.DMA((ride[2], n_ride))] * 2 if ride else []
    res = pl.pallas_call(
        body, name="attn_b_bwd_exchange" if ride else "attn_b_bwd", grid=(BKV_W // HEAD, nk, group, nq),
        in_specs=[ks, ks, qs, qs, st, st] + [ANY] * n_ride, out_specs=[ANY, ks, ks] + [ANY] * n_ride,
        out_shape=[jax.ShapeDtypeStruct((s, BQ_W), F32)] + [jax.ShapeDtypeStruct((s, 2 * BKV_W), F32)] * 2 + ride_out,
        scratch_shapes=[pltpu.VMEM((tk, 128), F32)] * 2 + [pltpu.VMEM((group, nq, 128, tq), F32), pltpu.VMEM((tq, 128), F32),
                                                          pltpu.SemaphoreType.DMA] + ride_sems,
        compiler_params=_params("arbitrary", "arbitrary", "arbitrary", "arbitrary"),
    )(kd, vd, q, do, lse, delta, *ride_in)
    return res[0], res[1], res[2], res[3:]


def _p_and_ds(items, masks):
    mx = _MXU
    keys = [(n, h) for n in range(len(items)) for h in range(2)]
    qh = {(n, h): jnp.where(masks[h], items[n][0], jnp.zeros_like(items[n][0])) for n, h in keys}
    doh = {(n, h): jnp.where(masks[h], items[n][3], jnp.zeros_like(items[n][3])) for n, h in keys}
    sc = {}
    for n, h in keys:
        s_h = _dot_t(qh[n, h], items[n][1])
        if items[n][7] is not None:
            s_h = s_h + items[n][7][h]
        sc[n, h] = jnp.where(items[n][6], s_h, NEG)
    dp = {(n, h): _dot_t(doh[n, h].astype(mx), items[n][2]) for n, h in keys}
    lse = {(n, h): jnp.max(items[n][5][:, h * 128:(h + 1) * 128], -1, keepdims=True) for n, h in keys}
    delta = {(n, h): jnp.sum(doh[n, h] * items[n][4], -1, keepdims=True) for n, h in keys}
    p = {key: jnp.exp(sc[key] - lse[key]) for key in keys}
    ds = {key: p[key] * (dp[key] - delta[key]) for key in keys}
    return [[(qh[n, h], p[n, h], ds[n, h], doh[n, h]) for h in range(2)] for n in range(len(items))]


class _BandA:
    hb, inner, has_bias, name = 1, 1, False, "a"

    def __init__(self, nb):
        self.nb = nb

    def mask(self, qidx, kidx):
        n = self.nb * BAND
        return (jnp.abs(qidx - kidx) <= A_RADIUS) & (kidx >= 0) & (kidx < n) & (qidx >= 0) & (qidx < n)


class _BandC:
    hb, inner, has_bias, name = 3, 2, True, "c"

    def __init__(self, nb):
        self.nb = nb
        self.rows = nb * BAND // GRID_W
        per = BAND // GRID_W
        assert self.rows >= C_ROWS and (C_ROWS - 1) // per <= self.hb
        assert (self.rows - 1) // per - (self.rows - C_ROWS) // per <= self.hb

    def mask(self, qidx, kidx):
        n = self.nb * BAND
        sh = GRID_W.bit_length() - 1
        qrow, cq = qidx >> sh, qidx & (GRID_W - 1)
        krow, ck = kidx >> sh, kidx & (GRID_W - 1)
        r0 = jnp.clip(qrow - C_ROWS // 2, 0, self.rows - C_ROWS)
        c0 = jnp.clip(cq - C_COLS // 2, 0, GRID_W - C_COLS)
        ok = (qidx >= 0) & (qidx < n) & (kidx >= 0) & (kidx < n)
        return ok & (krow >= r0) & (krow < r0 + C_ROWS) & (ck >= c0) & (ck < c0 + C_COLS)


def _bias_tile(off, a):
    return (BAND // GRID_W) * off - a + (C_ROWS - 1) + 2


def _band_bias_k(band, bt_ref, h, hw):
    per = BAND // GRID_W
    return jnp.concatenate([jnp.concatenate([bt_ref[h, _bias_tile(off, a)] for off in range(-hw, hw + 1)], 1)
                            for a in range(per)], 0)


def _band_bias_q(band, bt_ref, h, hw):
    per = BAND // GRID_W
    return jnp.concatenate([bt_ref[h, _bias_tile(-off, a)] for off in range(-hw, hw + 1) for a in range(per)], 0)


def _band_halfwidths(band, i, rb, nb, of_keys):
    if band.inner == band.hb:
        return [(None, band.hb)]
    lo, hi = i * rb, i * rb + rb - 1
    edge = ((lo <= band.hb) | (hi >= nb - 1 - band.hb)) if of_keys else ((lo == 0) | (hi == nb - 1))
    return [(edge, band.hb), (jnp.logical_not(edge), band.inner)]


def _band_split(nb, ncb):
    cb = max(c for c in (4, 2, 1) if ncb % c == 0)
    rb = max(r for r in (4, 2, 1) if nb % r == 0 and r * cb <= 16)
    return rb, cb


def _band_specs(band, rb, cb, nb, width):
    def edge(first):
        return pl.BlockSpec((BAND, cb * width), lambda c, i: (jnp.clip(i * rb + first, 0, nb - 1), c))

    main = pl.BlockSpec((rb * BAND, cb * width), lambda c, i: (i, c))
    return [edge(t - band.hb) for t in range(band.hb)] + [main] + [edge(rb + t) for t in range(band.hb)]


def _band_rows(band, refs, rb, r, lanes, hw):
    hb = band.hb
    parts = []
    for b in range(r + hb - hw, r + hb + hw + 1):
        if b < hb:
            parts.append(refs[b][:, lanes])
        elif b < hb + rb:
            parts.append(refs[hb][(b - hb) * BAND:(b - hb + 1) * BAND, lanes])
        else:
            parts.append(refs[b - rb + 1][:, lanes])
    return jnp.concatenate(parts, 0)


def _band_idx(band, blk, rows_of_blocks, axis):
    shape = (rows_of_blocks * BAND, 1) if axis == 0 else (1, rows_of_blocks * BAND)
    return blk * BAND + lax.broadcasted_iota(jnp.int32, shape, axis)


def _band_fwd(band, q, k, v, bt=None):
    n, w = q.shape
    nb, ncb, nband = n // BAND, w // 128, 2 * band.hb + 1
    rb, cb = _band_split(nb, ncb)
    mx = _MXU
    raw = not band.has_bias

    def body(*refs):
        q_ref, k_refs, v_refs = refs[0], refs[1:1 + nband], refs[1 + nband:1 + 2 * nband]
        rest = refs[1 + 2 * nband:]
        bt_ref = rest[0] if band.has_bias else None
        outs = rest[1:] if band.has_bias else rest
        i = pl.program_id(1)
        lo, masks = _head_masks()
        subs = [(r, c) for r in range(rb) for c in range(cb)]
        lanes = {c: slice(c * 128, (c + 1) * 128) for c in range(cb)}
        rows = {r: slice(r * BAND, (r + 1) * BAND) for r in range(rb)}

        def compute(hw):
            mask = {r: band.mask(_band_idx(band, i * rb + r, 1, 0), _band_idx(band, i * rb + r - hw, 2 * hw + 1, 1))
                    for r in range(rb)}
            sc = {}
            for r, c in subs:
                q2, kcat = q_ref[rows[r], lanes[c]], _band_rows(band, k_refs, rb, r, lanes[c], hw)
                for h in range(2):
                    s_h = _dot_t(jnp.where(masks[h], q2, jnp.zeros_like(q2)), kcat)
                    if band.has_bias:
                        s_h = s_h + _band_bias_k(band, bt_ref, 2 * c + h, hw)
                    sc[r, c, h] = jnp.where(mask[r], s_h, NEG)
            ms = {key: jnp.max(s_h, -1, keepdims=True) for key, s_h in sc.items()}
            ps = {key: jnp.exp(s_h - ms[key]) for key, s_h in sc.items()}
            ls = {key: jnp.sum(p, -1, keepdims=True) for key, p in ps.items()}
            os_ = {}
            for r, c in subs:
                vcat = _band_rows(band, v_refs, rb, r, lanes[c], hw)
                for h in range(2):
                    os_[r, c, h] = jnp.dot(ps[r, c, h].astype(mx), vcat, preferred_element_type=F32)
            for r, c in subs:
                st_lanes = [slice(c * 256 + h * 128, c * 256 + (h + 1) * 128) for h in range(2)]
                if raw:
                    o_ref, m_ref, l_ref = outs
                    o_ref[rows[r], lanes[c]] = jnp.where(lo, os_[r, c, 0], os_[r, c, 1])
                    for h in range(2):
                        m_ref[rows[r], st_lanes[h]] = _rep(ms[r, c, h], BAND)
                        l_ref[rows[r], st_lanes[h]] = _rep(ls[r, c, h], BAND)
                else:
                    o_ref, lse_ref = outs
                    o_ref[rows[r], lanes[c]] = jnp.where(lo, os_[r, c, 0] / ls[r, c, 0], os_[r, c, 1] / ls[r, c, 1])
                    for h in range(2):
                        lse_ref[rows[r], st_lanes[h]] = _rep(ms[r, c, h] + jnp.log(ls[r, c, h]), BAND)

        for pred, hw in _band_halfwidths(band, i, rb, nb, False):
            compute(hw) if pred is None else pl.when(pred)(functools.partial(compute, hw))

    qs = pl.BlockSpec((rb * BAND, cb * 128), lambda c, i: (i, c))
    ks = _band_specs(band, rb, cb, nb, 128)
    st = pl.BlockSpec((rb * BAND, cb * 256), lambda c, i: (i, c))
    in_specs, args = [qs] + ks + ks, [q] + [k] * nband + [v] * nband
    if band.has_bias:
        in_specs.append(pl.BlockSpec((2 * cb, BT_TILES, GRID_W, 128), lambda c, i: (c, 0, 0, 0)))
        args.append(bt)
    n_stats = 2 if raw else 1
    return pl.pallas_call(
        body, name="attn_%s_fwd" % band.name, grid=(ncb // cb, nb // rb), in_specs=in_specs,
        out_specs=[qs] + [st] * n_stats,
        out_shape=[jax.ShapeDtypeStruct((n, w), F32)] + [jax.ShapeDtypeStruct((n, 2 * w), F32)] * n_stats,
        compiler_params=_params("parallel", "arbitrary"),
    )(*args)


def _band_dq(band, q, k, v, do, o, lse, bt=None):
    n, w = q.shape
    nb, ncb, nband = n // BAND, w // 128, 2 * band.hb + 1
    rb, cb = _band_split(nb, ncb)
    mx = _MXU
    per = BAND // GRID_W

    def body(*refs):
        q_ref, k_refs, v_refs = refs[0], refs[1:1 + nband], refs[1 + nband:1 + 2 * nband]
        do_ref, o_ref, lse_ref = refs[1 + 2 * nband:4 + 2 * nband]
        rest = refs[4 + 2 * nband:]
        dq_ref = rest[1] if band.has_bias else rest[0]
        i = pl.program_id(1)
        lo, masks = _head_masks()
        if band.has_bias:
            dbt_ref = rest[2]

            @pl.when(i == 0)
            def _():
                dbt_ref[...] = jnp.zeros_like(dbt_ref)

        subs = [(r, c) for r in range(rb) for c in range(cb)]

        def compute(hw):
            mask = {r: band.mask(_band_idx(band, i * rb + r, 1, 0), _band_idx(band, i * rb + r - hw, 2 * hw + 1, 1))
                    for r in range(rb)}
            items, kcats = [], []
            for r, c in subs:
                lanes, rows = slice(c * 128, (c + 1) * 128), slice(r * BAND, (r + 1) * BAND)
                kcats.append(_band_rows(band, k_refs, rb, r, lanes, hw))
                bias = [_band_bias_k(band, rest[0], 2 * c + h, hw) for h in range(2)] if band.has_bias else None
                items.append((q_ref[rows, lanes], kcats[-1], _band_rows(band, v_refs, rb, r, lanes, hw), do_ref[rows, lanes],
                              o_ref[rows, lanes], lse_ref[rows, c * 256:(c + 1) * 256], mask[r], bias))
            res = _p_and_ds(items, masks)
            dqs = [[jnp.dot(ds.astype(mx), kcat, preferred_element_type=F32) for _, _, ds, _ in hs]
                   for hs, kcat in zip(res, kcats)]
            for (r, c), hs, dq in zip(subs, res, dqs):
                dq_ref[r * BAND:(r + 1) * BAND, c * 128:(c + 1) * 128] = jnp.where(lo, dq[0], dq[1])
                if band.has_bias:
                    for h in range(2):
                        ds = hs[h][2]
                        for a in range(per):
                            for t in range(2 * hw + 1):
                                tile = ds[a * GRID_W:(a + 1) * GRID_W, t * 128:(t + 1) * 128]
                                dbt_ref[2 * c + h, _bias_tile(t - hw, a)] += tile

        for pred, hw in _band_halfwidths(band, i, rb, nb, False):
            compute(hw) if pred is None else pl.when(pred)(functools.partial(compute, hw))

    qs = pl.BlockSpec((rb * BAND, cb * 128), lambda c, i: (i, c))
    ks = _band_specs(band, rb, cb, nb, 128)
    st = pl.BlockSpec((rb * BAND, cb * 256), lambda c, i: (i, c))
    in_specs, args = [qs] + ks + ks + [qs, qs, st], [q] + [k] * nband + [v] * nband + [do, o, lse]
    out_specs, out_shape = [qs], [jax.ShapeDtypeStruct((n, w), F32)]
    if band.has_bias:
        bts = pl.BlockSpec((2 * cb, BT_TILES, GRID_W, 128), lambda c, i: (c, 0, 0, 0))
        in_specs.append(bts)
        args.append(bt)
        out_specs.append(bts)
        out_shape.append(jax.ShapeDtypeStruct(bt.shape, F32))
    return pl.pallas_call(
        body, name="attn_%s_dq" % band.name, grid=(ncb // cb, nb // rb), in_specs=in_specs, out_specs=out_specs,
        out_shape=out_shape, compiler_params=_params("parallel", "arbitrary"),
    )(*args)


def _band_dkv(band, q, k, v, do, o, lse, bt=None):
    n, w = q.shape
    nb, ncb, nband = n // BAND, w // 128, 2 * band.hb + 1
    rb, cb = _band_split(nb, ncb)
    mx = _MXU

    def body(*refs):
        k_ref, v_ref = refs[0], refs[1]
        q_refs, do_refs, o_refs, lse_refs = [refs[2 + g * nband:2 + (g + 1) * nband] for g in range(4)]
        rest = refs[2 + 4 * nband:]
        dk_ref, dv_ref = rest[-2], rest[-1]
        i = pl.program_id(1)
        _, masks = _head_masks()
        subs = [(r, c) for r in range(rb) for c in range(cb)]

        def compute(hw):
            mask = {r: band.mask(_band_idx(band, i * rb + r - hw, 2 * hw + 1, 0), _band_idx(band, i * rb + r, 1, 1))
                    for r in range(rb)}
            items = []
            for r, c in subs:
                lanes, rows = slice(c * 128, (c + 1) * 128), slice(r * BAND, (r + 1) * BAND)
                qcat, docat, ocat = [_band_rows(band, g, rb, r, lanes, hw) for g in (q_refs, do_refs, o_refs)]
                lsecat = _band_rows(band, lse_refs, rb, r, slice(c * 256, (c + 1) * 256), hw)
                bias = [_band_bias_q(band, rest[0], 2 * c + h, hw) for h in range(2)] if band.has_bias else None
                items.append((qcat, k_ref[rows, lanes], v_ref[rows, lanes], docat, ocat, lsecat, mask[r], bias))
            res = _p_and_ds(items, masks)
            dks = [sum(_tdot(ds.astype(mx), qh) for qh, _, ds, _ in hs) for hs in res]
            dvs = [sum(_tdot(p.astype(mx), doh.astype(mx)) for _, p, _, doh in hs) for hs in res]
            for (r, c), dk, dv in zip(subs, dks, dvs):
                dk_ref[r * BAND:(r + 1) * BAND, c * 128:(c + 1) * 128] = dk
                dv_ref[r * BAND:(r + 1) * BAND, c * 128:(c + 1) * 128] = dv

        for pred, hw in _band_halfwidths(band, i, rb, nb, True):
            compute(hw) if pred is None else pl.when(pred)(functools.partial(compute, hw))

    ks = pl.BlockSpec((rb * BAND, cb * 128), lambda c, i: (i, c))
    in_specs = [ks, ks] + _band_specs(band, rb, cb, nb, 128) * 3 + _band_specs(band, rb, cb, nb, 256)
    args = [k, v] + [q] * nband + [do] * nband + [o] * nband + [lse] * nband
    if band.has_bias:
        in_specs.append(pl.BlockSpec((2 * cb, BT_TILES, GRID_W, 128), lambda c, i: (c, 0, 0, 0)))
        args.append(bt)
    return pl.pallas_call(
        body, name="attn_%s_dkv" % band.name, grid=(ncb // cb, nb // rb), in_specs=in_specs, out_specs=[ks, ks],
        out_shape=[jax.ShapeDtypeStruct((n, w), F32)] * 2,
        compiler_params=_params("parallel", "arbitrary"),
    )(*args)


def _dc_onehot():
    c = np.arange(GRID_W)
    dc = np.clip(c[None, :] - c[:, None] + (C_COLS - 1), 0, 2 * C_COLS - 2).reshape(-1)
    m = np.zeros((GRID_W * GRID_W, 128), np.float32)
    m[np.arange(dc.size), dc] = 1.0
    return m


def _bias_tiles(rpb):
    h, nr, ncol = rpb.shape
    flat = jnp.pad(rpb.reshape(h * nr, ncol), ((0, (-h * nr) % 8), (0, 128 - ncol)))
    tiles = _mm(flat, jnp.asarray(_dc_onehot().T), name="rpb_tiles", exact=True, tn=GRID_W * GRID_W)
    tiles = tiles[:h * nr].reshape(h, nr, GRID_W, GRID_W)
    tiles = jnp.pad(tiles, ((0, 0), (2, BT_TILES + 1 - nr - 2), (0, 0), (0, 0)))
    return jnp.concatenate([tiles[:, :BT_TILES], tiles[:, 1:BT_TILES + 1]], -1)


def _bias_tiles_grad(dbt, nr, ncol):
    h = dbt.shape[0]
    d = dbt[:, 2:2 + nr, :, :GRID_W] + dbt[:, 1:1 + nr, :, GRID_W:]
    flat = jnp.pad(d.reshape(h * nr, GRID_W * GRID_W), ((0, (-h * nr) % 8), (0, 0)))
    g = _mm(flat, jnp.asarray(_dc_onehot()), name="rpb_grad", exact=True, tk=GRID_W * GRID_W)
    return g[:h * nr, :ncol].reshape(h, nr, ncol)


TM = 512
TQ_B, TK_B = 1024, 2048


def _relu2(acc):
    r = jnp.maximum(acc, 0.0)
    return (r * r,)


def _layer_fwd(x, xb, w, sm, tabs, alpha, ride=None):
    tab_a, tab_q, tab_k = tabs
    s, d = x.shape
    ha = _mm(xb, w["in"], name="in_a", tn=QKV_W, b_cols=QKV_W, b_off=0)
    hb = _mm(xb, w["in"], name="in_b", tn=QKV_W, b_cols=QKV_W, b_off=1)
    hc = _mm(xb, w["in"], name="in_c", tn=QKV_W, b_cols=QKV_W, b_off=2)
    hg = _mm(xb, w["in"], name="in_g", outs=(_MXU,), tn=QKV_W, b_cols=3 * d, b_off=3)

    qa, ka, va = _prep_a(ha, tab_a, TM)
    stats = [_band_fwd(_BandA(s // dil // BAND), qa[dil], ka[dil], va[dil]) for dil in A_DILATIONS]
    oas, lse_a = _combine_a(*zip(*stats), s, TM)
    oa = oas[1]

    qb, kd, vd, v1 = _prep_b(hb, sm["q_norm"], sm["k_norm"], tab_q, tab_k, TM)
    ob, lse_b, rode = _flash_fwd(qb, kd, v1, TQ_B, TK_B, ride)

    qc, kc, vc = _prep_c(hc, TM)
    bt = _bias_tiles(sm["rpb"])
    oc, lse_c = _band_fwd(_BandC(s // BAND), qc, kc, vc, bt)

    pa = _mm(oa, w["br_a"], name="br_a", outs=(_MXU,))
    pb = _mm(ob, w["br_b"], name="br_b", outs=(_MXU,))
    pc = _mm(oc, w["br_c"], name="br_c", outs=(_MXU,))
    merged = _gate_merge(hg, sm["b_gate"], pa, pb, pc, TM)
    ln = dict(outs=(F32, F32, _MXU), epilogue=_ln_epilogue(alpha), tn=d)
    r1, x1, x1b = _mm(merged, w["out"], name="w_out_ln1", extras=(x, sm["ln1_g"], sm["ln1_b"]), **ln)
    act = _mm(x1b, w["up"], name="w_up", outs=(_MXU,), epilogue=_relu2)
    r2, x2, x2b = _mm(act, w["down"], name="w_down_ln2", extras=(x1, sm["ln2_g"], sm["ln2_b"]), **ln)
    saved = dict(xb=xb, hb=hb, hg=hg, qa=qa, ka=ka, va=va, oa=oa, oas=oas, lse_a=lse_a, qb=qb, kd=kd, vd=vd, ob=ob, lse_b=lse_b,
                 qc=qc, kc=kc, vc=vc, oc=oc, lse_c=lse_c, bt=bt, pa=pa, pb=pb, pc=pc, merged=merged, r1=r1, x1b=x1b,
                 act=act, r2=r2)
    return x2, x2b, saved, rode


def _layer_bwd(dx2, w, sm, sv, tabs, alpha, ride=None):
    tab_a, tab_q, tab_k = tabs
    s, d = dx2.shape
    g = {}
    dr2, dr2b, dg2, db2 = _ln_bwd(dx2, sv["r2"], sm["ln2_g"], "ln2_bwd", TM)
    g["ln2_g"], g["ln2_b"] = dg2.sum(0), db2.sum(0)
    du = _mm(dr2b, w["down"], mode="nt", name="d_act", outs=(_MXU,), extras=(sv["act"],),
             epilogue=lambda acc, act: (acc * (2.0 * jnp.sqrt(act.astype(F32))),))
    g["w_down"] = _mm(sv["act"], dr2b, mode="tn", name="g_w_down", outs=(_MXU,)).reshape(4, -1, d)
    g["w_up"] = _mm(sv["x1b"], du, mode="tn", name="g_w_up", outs=(_MXU,), out_chips=True)
    dx1 = _mm(du, w["up"], mode="nt", name="d_x1", extras=(dr2,), epilogue=lambda acc, e: (acc + alpha * e,))
    dr1, dr1b, dg1, db1 = _ln_bwd(dx1, sv["r1"], sm["ln1_g"], "ln1_bwd", TM)
    g["ln1_g"], g["ln1_b"] = dg1.sum(0), db1.sum(0)
    g["w_out"] = _mm(sv["merged"], dr1b, mode="tn", name="g_w_out", outs=(_MXU,))
    dmerged = _mm(dr1b, w["out"], mode="nt", name="d_merged", outs=(_MXU,))
    dpa, dpb, dpc, dlog, gb = _gate_bwd(dmerged, sv["hg"], sm["b_gate"], sv["pa"], sv["pb"], sv["pc"], TM)
    g["b_gate"] = gb.sum(0)
    g["w_branch_a"] = _mm(sv["oa"], dpa, mode="tn", name="g_br_a", outs=(_MXU,))
    g["w_branch_b"] = _mm(sv["ob"], dpb, mode="tn", name="g_br_b", outs=(_MXU,))
    g["w_branch_c"] = _mm(sv["oc"], dpc, mode="tn", name="g_br_c", outs=(_MXU,))
    doa = _mm(dpa, w["br_a"], mode="nt", name="d_oa")
    dob = _mm(dpb, w["br_b"], mode="nt", name="d_ob")
    doc = _mm(dpc, w["br_c"], mode="nt", name="d_oc")

    dqs, dks, dvs = [], [], []
    doas = _to_dilations(doa, "d_oa_layouts", TM)
    for dil in A_DILATIONS:
        band = _BandA(s // dil // BAND)
        args = [t[dil] for t in (sv["qa"], sv["ka"], sv["va"], doas, sv["oas"], sv["lse_a"])]
        dqs.append(_band_dq(band, *args)[0])
        dk_c, dv_c = _band_dkv(band, *args)
        dks.append(dk_c)
        dvs.append(dv_c)
    dh = _post_a(dqs, dks, dvs, tab_a, s, TM)

    dqb, dkd, dvd, rode = _flash_bwd(sv["qb"], sv["kd"], sv["vd"], dob, sv["lse_b"], _delta_b(dob, sv["ob"], TQ_B),
                                     TQ_B, TK_B, ride)
    dh, gq, gk = _post_b(dqb, dkd, dvd, sv["hb"], sm["q_norm"], sm["k_norm"], tab_q, tab_k, TM, dh)
    g["q_norm_b"] = gq.sum(0).reshape(-1, HEAD).sum(0)
    g["k_norm_b"] = gk.sum(0).reshape(-1, HEAD).sum(0)

    band_c = _BandC(s // BAND)
    cargs = (sv["qc"], sv["kc"], sv["vc"], doc, sv["oc"], sv["lse_c"], sv["bt"])
    dqc, dbt = _band_dq(band_c, *cargs)
    dkc, dvc = _band_dkv(band_c, *cargs)
    dh = _post_c(dqc, dkc, dvc, TM, dh)
    g["rpb_c"] = _bias_tiles_grad(dbt, 2 * C_ROWS - 1, 2 * C_COLS - 1)

    xb = sv["xb"]
    g["w_in"] = jnp.concatenate([_mm(xb, t, mode="tn", name="g_in_" + nm, outs=(_MXU,), tn=QKV_W)
                                 for nm, t in (("qkv", dh), ("g", dlog))], 1)
    dx = _mm(dh, w["in"], mode="nt", name="d_x_qkv", tk=QKV_W, b_cols=dh.shape[1], b_off=0, extras=(dr1,),
             epilogue=lambda acc, e: (acc + alpha * e,))
    dx = _mm(dlog, w["in"], mode="nt", name="d_x_g", tk=QKV_W, b_cols=dlog.shape[1], b_off=3, extras=(dx,),
             epilogue=lambda acc, e: (acc + e,))
    return dx, g, rode


BIG = ("w_in", "w_branch_a", "w_branch_b", "w_branch_c", "w_out", "w_up", "w_down")
ROW_SHARDED = ("w_out", "w_down")
AS_GATHERED = ("w_up", "w_down")
SMALL = ("b_gate", "q_norm_b", "k_norm_b", "rpb_c", "ln1_g", "ln1_b", "ln2_g", "ln2_b")


def _layer_weights(gathered):
    names = dict(w_in="in", w_branch_a="br_a", w_branch_b="br_b", w_branch_c="br_c", w_out="out", w_up="up", w_down="down")
    whole = {n: _full_from_shards(gathered[n], n) for n in names if n not in AS_GATHERED}
    return [{short: (whole[n], l) if n in whole else (gathered[n], l, "rows" if n in ROW_SHARDED else "cols")
             for n, short in names.items()} for l in range(gathered["w_in"].shape[1])]


def _local_step(x, target, gathered, small, rest=None):
    s, d = x.shape
    depth = small["b_gate"].shape[0]
    alpha = (2 * depth) ** 0.25
    tabs = _tables(s)
    ws = _layer_weights(gathered)
    sms = [dict(b_gate=small["b_gate"][l][None], q_norm=jnp.tile(small["q_norm_b"][l], BQ_W // HEAD)[None],
                k_norm=jnp.tile(small["k_norm_b"][l], BKV_W // HEAD)[None], rpb=small["rpb_c"][l],
                ln1_g=small["ln1_g"][l][None], ln1_b=small["ln1_b"][l][None],
                ln2_g=small["ln2_g"][l][None], ln2_b=small["ln2_b"][l][None]) for l in range(depth)]
    saved = []
    h, hb = x, x.astype(_MXU)
    for l in range(depth):
        h, hb, sv, rode = _layer_fwd(h, hb, ws[l], sms[l], tabs, alpha, rest if l == 0 else None)
        if rode:
            ws += _layer_weights(dict(zip(BIG, rode)))
        saved.append(sv)
    sq, dy = _loss_head(h, target, TM)

    def stack(gs):
        out = {k: jnp.stack([gl[k] for gl in gs], 1 if k in AS_GATHERED else 0) for k in gs[0]}
        for n in BIG:
            if n not in AS_GATHERED:
                out[n] = _shards_from_full(out[n], n).astype(_MXU)
        return out

    grads = [None] * depth
    reduced_later = None
    for l in reversed(range(depth)):
        ride = ts = None
        if rest is not None and l == 0 and depth > 1:
            ts = _reduce_pairs([stack(grads[1:])[n] for n in BIG])
            ride = (ts, _chip_out_shapes(ts), CHIP_SEMS, _chip_phases())
        dy, grads[l], rode = _layer_bwd(dy, ws[l], sms[l], saved[l], tabs, alpha, ride)
        if ride:
            reduced_later = _reduce_finish(ts, rode)
    if reduced_later is None:
        return sq, dy, stack(grads)
    first = stack(grads[:1])
    reduced = _reduce_scatter([first[n] for n in BIG])
    out = {n: jnp.concatenate([a, b], 0) for n, a, b in zip(BIG, reduced, reduced_later)}
    out.update({k: jnp.concatenate([first[k], stack(grads[1:])[k]], 0) for k in first if k not in BIG})
    return sq, dy, out


def _place():
    return lax.axis_index("x"), lax.axis_index("y"), lax.axis_index("c")


def _flip(a, b):
    return a + b - 2 * a * b


def _other_chips(x, y):
    return [(1 - x, y), (x, 1 - y), (1 - x, 1 - y)]


def _comm_call(body, name, tensors, out_shapes, n_sems):
    n = len(tensors)

    def wrapped(*refs):
        body(refs[:n], refs[n:2 * n], refs[2 * n], refs[2 * n + 1])

    return pl.pallas_call(
        wrapped, name=name, in_specs=[ANY] * n, out_specs=[ANY] * n,
        out_shape=[jax.ShapeDtypeStruct(s, t.dtype) for s, t in zip(out_shapes, tensors)],
        scratch_shapes=[pltpu.SemaphoreType.DMA((n_sems, n)), pltpu.SemaphoreType.DMA((n_sems, n))],
    )(*tensors)


GATHER_SEMS = 7


def _gather_phases(lo, n_layers):
    def ctx(srcs, outs, send_sems, recv_sems):
        x, y, c = _place()
        n1, n2, dg = (_flip(x, 1 - c), _flip(y, c)), (_flip(x, c), _flip(y, 1 - c)), (1 - x, 1 - y)

        def half(t, chip, hc):
            rh = outs[t].shape[2] // 2
            return outs[t].at[2 * chip[0] + chip[1], :, pl.ds(hc * rh, rh)]

        def copy(k, t, src_ref, dst_ref, to):
            return pltpu.make_async_remote_copy(src_ref=src_ref, dst_ref=dst_ref, send_sem=send_sems.at[k, t],
                                                recv_sem=recv_sems.at[k, t], device_id=to, device_id_type=MESH)

        def sends(t, ks):
            rh = srcs[t].shape[1] // 2
            own, mine = srcs[t].at[pl.ds(lo, n_layers), pl.ds(c * rh, rh)], srcs[t].at[pl.ds(lo, n_layers)]
            table = {0: (own, half(t, (x, y), c), (*n1, c)), 1: (own, half(t, (x, y), c), (*n2, c)),
                     2: (half(t, n1, c), half(t, n1, c), (*n2, c)), 6: (mine, outs[t].at[2 * x + y], (x, y, 1 - c)),
                     "n1": (half(t, n1, c), half(t, n1, c), (x, y, 1 - c)), "n2": (half(t, n2, c), half(t, n2, c), (x, y, 1 - c)),
                     "dg": (half(t, dg, c), half(t, dg, c), (x, y, 1 - c))}
            sem = {0: 0, 1: 1, 2: 2, 6: 6, "n1": 3 + c, "n2": 4 - c, "dg": 5}
            return [copy(sem[k], t, *table[k]) for k in ks]

        def arrived(k, t, chip, hc):
            copy(k, t, half(t, chip, hc), half(t, chip, hc), (x, y, 1 - c)).wait_recv()

        return x, y, c, n1, n2, dg, sends, arrived, range(len(srcs))

    def phase0(*refs):
        *_, sends, _, ts = ctx(*refs)
        for t in ts:
            for cp in sends(t, (0, 1, 6)):
                cp.start()

    def phase1(*refs):
        x, y, c, n1, n2, dg, sends, arrived, ts = ctx(*refs)
        for t in ts:
            arrived(0, t, n1, c)
            for cp in sends(t, (2, "n1")):
                cp.start()
        for t in ts:
            arrived(1, t, n2, c)
            sends(t, ("n2",))[0].start()

    def phase2(*refs):
        x, y, c, n1, n2, dg, sends, arrived, ts = ctx(*refs)
        srcs, outs = refs[0], refs[1]
        for t in ts:
            arrived(2, t, dg, c)
            sends(t, ("dg",))[0].start()
        for t in ts:
            for j, chip in enumerate(_other_chips(x, y)):
                arrived(3 + j, t, chip, 1 - c)
            sends(t, (6,))[0].wait_recv()
            for cp in sends(t, (0, 1, 2, 6, "n1", "n2", "dg")):
                cp.wait_send()

    return [phase0, phase1, phase2]


def _gather_out_shapes(shards, n_layers):
    return [(4, n_layers) + s.shape[1:] for s in shards]


def _gather_shards(shards, lo, n_layers):
    phases = _gather_phases(lo, n_layers)

    def body(*refs):
        for f in phases:
            f(*refs)

    return _comm_call(body, "gather_weights", shards, _gather_out_shapes(shards, n_layers), GATHER_SEMS)


def _pair_exchange(parts):
    def body(srcs, outs, send_sems, recv_sems):
        x, y, c = _place()
        cps = [pltpu.make_async_remote_copy(src_ref=src.at[:, :, pl.ds((1 - c) * (src.shape[2] // 2), src.shape[2] // 2)],
                                            dst_ref=out, send_sem=send_sems.at[0, t], recv_sem=recv_sems.at[0, t],
                                            device_id=(x, y, 1 - c), device_id_type=MESH)
               for t, (src, out) in enumerate(zip(srcs, outs))]
        for cp in cps:
            cp.start()
        for cp in cps:
            cp.wait()

    return _comm_call(body, "grad_pair_exchange", parts, [p.shape[:2] + (p.shape[2] // 2, p.shape[3]) for p in parts], 1)


CHIP_SEMS = 3


def _chip_phases():
    def copies(srcs, outs, send_sems, recv_sems):
        x, y, c = _place()
        return [pltpu.make_async_remote_copy(src_ref=src.at[2 * chip[0] + chip[1]], dst_ref=out.at[k], send_sem=send_sems.at[k, t],
                                             recv_sem=recv_sems.at[k, t], device_id=(*chip, c), device_id_type=MESH)
                for t, (src, out) in enumerate(zip(srcs, outs)) for k, chip in enumerate(_other_chips(x, y))]

    def start(*refs):
        for cp in copies(*refs):
            cp.start()

    def wait(*refs):
        for cp in copies(*refs):
            cp.wait()

    return [start, lambda *refs: None, wait]


def _chip_out_shapes(ts):
    return [(3,) + t.shape[1:] for t in ts]


def _chip_exchange(ts):
    phases = _chip_phases()

    def body(*refs):
        for f in phases:
            f(*refs)

    return _comm_call(body, "grad_chip_exchange", ts, _chip_out_shapes(ts), CHIP_SEMS)


def _pair_share(halves):
    def body(srcs, outs, send_sems, recv_sems):
        x, y, c = _place()
        cps = [pltpu.make_async_remote_copy(src_ref=src, dst_ref=out, send_sem=send_sems.at[0, t], recv_sem=recv_sems.at[0, t],
                                            device_id=(x, y, 1 - c), device_id_type=MESH)
               for t, (src, out) in enumerate(zip(srcs, outs))]
        for cp in cps:
            cp.start()
        for cp in cps:
            cp.wait()

    theirs = _comm_call(body, "grad_pair_share", halves, [h.shape for h in halves], 1)
    c = jnp.reshape(lax.axis_index("c"), (1,)).astype(jnp.int32)
    return [_join_halves(mine, other, c, "grad_pair_join_%d" % t) for t, (mine, other) in enumerate(zip(halves, theirs))]


def _rows_view(t, lead):
    return t.reshape(t.shape[:lead] + (-1, t.shape[-1]))


def _join_halves(mine, theirs, c, name):
    n, rh, cols = mine.shape
    tr = _tile(rh, 1024, 8)

    def join(c_ref, mine_ref, theirs_ref, o_ref):
        o_ref[...] = jnp.where(pl.program_id(1) == c_ref[0], mine_ref[...], theirs_ref[...])

    spec = pl.BlockSpec((None, tr, cols), lambda l, h, i, c_ref: (l, i, 0))
    return pl.pallas_call(
        join, name=name,
        grid_spec=pltpu.PrefetchScalarGridSpec(
            num_scalar_prefetch=1, grid=(n, 2, rh // tr), in_specs=[spec, spec],
            out_specs=pl.BlockSpec((None, tr, cols), lambda l, h, i, c_ref: (l, h * (rh // tr) + i, 0))),
        out_shape=jax.ShapeDtypeStruct((n, 2 * rh, cols), mine.dtype),
        compiler_params=_params("parallel", "parallel", "parallel"),
    )(c, mine, theirs)


def _gather_all(v):
    r = v.shape[0]

    def body(src, out, send_sems, recv_sems, local_sem):
        x, y, c = _place()
        me = 4 * x + 2 * y + c
        mine = pltpu.make_async_copy(src, out.at[me], local_sem)
        mine.start()
        cps = []
        for k in range(1, 8):
            fx, fy, fc = (k >> 2) & 1, (k >> 1) & 1, k & 1
            peer = (x + fx - 2 * x * fx, y + fy - 2 * y * fy, c + fc - 2 * c * fc)
            cps.append(pltpu.make_async_remote_copy(src_ref=src, dst_ref=out.at[me], send_sem=send_sems.at[k - 1],
                                                    recv_sem=recv_sems.at[k - 1], device_id=peer, device_id_type=MESH))
        for cp in cps:
            cp.start()
        for k in range(1, 8):
            fx, fy, fc = (k >> 2) & 1, (k >> 1) & 1, k & 1
            frm = 4 * (x + fx - 2 * x * fx) + 2 * (y + fy - 2 * y * fy) + (c + fc - 2 * c * fc)
            pltpu.make_async_remote_copy(src_ref=src, dst_ref=out.at[frm], send_sem=send_sems.at[k - 1],
                                         recv_sem=recv_sems.at[k - 1], device_id=(x, y, c), device_id_type=MESH).wait_recv()
        for cp in cps:
            cp.wait_send()
        mine.wait()

    return pl.pallas_call(
        body, name="gather_small_grads", in_specs=[ANY], out_specs=ANY,
        out_shape=jax.ShapeDtypeStruct((8, r, 128), v.dtype),
        scratch_shapes=[pltpu.SemaphoreType.DMA((7,)), pltpu.SemaphoreType.DMA((7,)), pltpu.SemaphoreType.DMA],
    )(v)


def _sum_slots(parts, name):
    n, r, _ = parts.shape
    tr = _tile(r, 1024, 8)

    def body(p_ref, o_ref):
        acc = p_ref[0]
        for j in range(1, n):
            acc = acc + p_ref[j]
        o_ref[...] = acc

    return pl.pallas_call(
        body, name=name, grid=(r // tr,), in_specs=[pl.BlockSpec((n, tr, 128), lambda i: (0, i, 0))],
        out_specs=pl.BlockSpec((tr, 128), lambda i: (i, 0)), out_shape=jax.ShapeDtypeStruct((r, 128), parts.dtype),
        compiler_params=_params("parallel"),
    )(parts)


def _add_sibling_half(part, recv, c, name):
    _, n, rh, cols = recv.shape
    tr = _tile(rh, 1024, 16)
    nblk = rh // tr

    def body(c_ref, p_ref, r_ref, o_ref):
        o_ref[...] = (p_ref[...].astype(F32) + r_ref[...].astype(F32)).astype(o_ref.dtype)

    blk = (None, None, tr, cols)
    return pl.pallas_call(
        body, name=name,
        grid_spec=pltpu.PrefetchScalarGridSpec(
            num_scalar_prefetch=1, grid=(4, n, nblk),
            in_specs=[pl.BlockSpec(blk, lambda j, l, i, c_ref: (j, l, c_ref[0] * nblk + i, 0)),
                      pl.BlockSpec(blk, lambda j, l, i, c_ref: (j, l, i, 0))],
            out_specs=pl.BlockSpec(blk, lambda j, l, i, c_ref: (j, l, i, 0))),
        out_shape=jax.ShapeDtypeStruct(recv.shape, recv.dtype),
        compiler_params=_params("parallel", "parallel", "parallel"),
    )(c, part, recv)


def _add_chips(t, recv, me, name):
    shape = t.shape[1:]
    t, recv = _rows_view(t, 1), _rows_view(recv, 1)
    _, rh, cols = t.shape
    tr = _tile(rh, 1024, 16)

    def body(me_ref, t_ref, r_ref, o_ref):
        f = lambda v: v.astype(F32)
        o_ref[...] = ((f(t_ref[...]) + f(r_ref[0])) + f(r_ref[1])) + f(r_ref[2])

    return pl.pallas_call(
        body, name=name,
        grid_spec=pltpu.PrefetchScalarGridSpec(
            num_scalar_prefetch=1, grid=(rh // tr,),
            in_specs=[pl.BlockSpec((None, tr, cols), lambda i, me_ref: (me_ref[0], i, 0)),
                      pl.BlockSpec((3, tr, cols), lambda i, me_ref: (0, i, 0))],
            out_specs=pl.BlockSpec((tr, cols), lambda i, me_ref: (i, 0))),
        out_shape=jax.ShapeDtypeStruct((rh, cols), F32),
        compiler_params=_params("parallel"),
    )(me, t, recv).reshape(shape)


def _reduce_pairs(parts):
    c1 = jnp.reshape(lax.axis_index("c"), (1,)).astype(jnp.int32)
    return [_add_sibling_half(p, r, c1, "grad_pair_sum_%d" % i) for i, (p, r) in enumerate(zip(parts, _pair_exchange(parts)))]


def _reduce_finish(ts, recv):
    me = jnp.reshape(2 * lax.axis_index("x") + lax.axis_index("y"), (1,)).astype(jnp.int32)
    return _pair_share([_add_chips(t, r, me, "grad_chip_sum_%d" % i) for i, (t, r) in enumerate(zip(ts, recv))])


def _reduce_scatter(parts):
    ts = _reduce_pairs(parts)
    return _reduce_finish(ts, _chip_exchange(ts))


def _to_rows(parts, mult):
    flat = jnp.concatenate([p.reshape(-1) for p in parts])
    flat = jnp.pad(flat, (0, (-flat.size) % (128 * mult)))
    return flat.reshape(-1, 128)


def _from_rows(rows, shapes):
    flat, out, at = rows.reshape(-1), [], 0
    for shp in shapes:
        n = int(np.prod(shp))
        out.append(flat[at:at + n].reshape(shp))
        at += n
    return out


def _full_from_shards(g, name):
    _, depth, rows, cols = g.shape
    if name in ROW_SHARDED:
        return jnp.moveaxis(g, 0, 1).reshape(depth, 4 * rows, cols)
    return jnp.moveaxis(g, 0, 2).reshape(depth, rows, 4 * cols)


def _shards_from_full(full, name):
    depth, rows, cols = full.shape
    if name in ROW_SHARDED:
        return jnp.moveaxis(full.reshape(depth, 4, rows // 4, cols), 1, 0)
    return jnp.moveaxis(full.reshape(depth, rows, 4, cols // 4), 2, 0)


def kernel(x, w_in, b_gate, q_norm_b, k_norm_b, rpb_c, w_branch_a, w_branch_b, w_branch_c, w_out, ln1_g, ln1_b, w_up, w_down, ln2_g, ln2_b, loss_target, m_w_in, m_b_gate, m_q_norm_b, m_k_norm_b, m_rpb_c, m_w_branch_a, m_w_branch_b, m_w_branch_c, m_w_out, m_ln1_g, m_ln1_b, m_w_up, m_w_down, m_ln2_g, m_ln2_b, v_w_in, v_b_gate, v_q_norm_b, v_k_norm_b, v_rpb_c, v_w_branch_a, v_w_branch_b, v_w_branch_c, v_w_out, v_ln1_g, v_ln1_b, v_w_up, v_w_down, v_ln2_g, v_ln2_b):
    args = dict(locals())
    big_shard = {n: args[n] for n in BIG}
    small = {n: args[n] for n in SMALL}

    shards = [big_shard[n].astype(_MXU) for n in BIG]
    later = shards[0].shape[0] - 1
    first = dict(zip(BIG, _gather_shards(shards, 0, 1)))
    rest = (shards, _gather_out_shapes(shards, later), GATHER_SEMS, _gather_phases(1, later)) if later else None

    sq, grad_x, grads = _local_step(x[0], loss_target[0], first, small, rest)
    loss = lax.psum(0.5 * jnp.sum(sq) / x.shape[-1], ("x", "y", "c"))

    g_big = [grads[n] for n in BIG] if rest else _reduce_scatter([grads[n] for n in BIG])
    small_shapes = [small[n].shape for n in SMALL]
    g_small = _from_rows(_sum_slots(_gather_all(_to_rows([grads[n] for n in SMALL], 8)), "small_grad_sum"), small_shapes)
    grad = dict(zip(BIG, g_big))
    grad.update(zip(SMALL, g_small))

    delta, new_m, new_v = {}, {}, {}
    for n in BIG:
        shp = big_shard[n].shape
        two_d = lambda t: t.reshape(-1, shp[-1])
        res = _adamw(two_d(big_shard[n]), two_d(grad[n]), two_d(args["m_" + n]), two_d(args["v_" + n]), "adamw_" + n)
        delta[n], new_m[n], new_v[n] = [t.reshape(shp) for t in res]
    packed = [_to_rows([args[pre + n] for n in SMALL], 8) for pre in ("", "m_", "v_")]
    res = _adamw(packed[0], _to_rows([grad[n] for n in SMALL], 8), packed[1], packed[2], "adamw_small")
    for dst, rows in zip((delta, new_m, new_v), res):
        dst.update(zip(SMALL, _from_rows(rows, small_shapes)))

    order = ("w_in", "b_gate", "q_norm_b", "k_norm_b", "rpb_c", "w_branch_a", "w_branch_b", "w_branch_c", "w_out",
             "ln1_g", "ln1_b", "w_up", "w_down", "ln2_g", "ln2_b")
    return (loss, grad_x[None], *[grad[n] for n in order], *[delta[n] for n in order],
            *[new_m[n] for n in order], *[new_v[n] for n in order])
```
